```python
import math
import jax, jax.numpy as jnp
from jax import lax
import numpy as np

D_MODEL = 1024
BATCH = 8
SEQ = 4096
DEPTH = 4

GRID_W = 64
CTX_LEN = 256
CHUNK = 64
RMS_EPS = 1e-6
MIX_W = D_MODEL
GLA_W = MIX_W // 4
GLA_HEAD_V = 64
GLA_HEADS = GLA_W // GLA_HEAD_V
GLA_HEAD_K = GLA_HEAD_V // 2
GLA_KDIM = GLA_HEADS * GLA_HEAD_K
GLA_RANK = 16
GLA_GATE_TAU = 16.0
SSD_W = MIX_W // 2
SSD_HEAD_DIM = 64
SSD_HEADS = SSD_W // SSD_HEAD_DIM
SSD_GROUPS = 2
SSD_STATE = 128
SSD_CONV_W = 5
SSD_CONV_CH = SSD_W + 2 * SSD_GROUPS * SSD_STATE
RET_W = MIX_W - GLA_W - SSD_W
RET_HEAD_DIM = 64
RET_HEADS = RET_W // RET_HEAD_DIM
ROPE_BASE = 10000.0
GLA_COLS = 2 * GLA_KDIM + 2 * GLA_W + 2 * GLA_RANK
SSD_COLS = SSD_W + SSD_CONV_CH + 2 * SSD_HEADS
RET_COLS = 4 * RET_W
IN_COLS = GLA_COLS + SSD_COLS + RET_COLS
FFN_HIDDEN = -(-8 * D_MODEL // (3 * 256)) * 256

kernel_name = "hybrid_gla_ssd_retention_dit_block"


def rms_norm(x, w, eps=RMS_EPS):
    xf = x.astype(jnp.float32)
    y = xf * lax.rsqrt(jnp.mean(xf * xf, axis=-1, keepdims=True) + eps)
    return (y * w.astype(jnp.float32)).astype(x.dtype)


def layer_norm(x, w, eps=RMS_EPS):
    xf = x.astype(jnp.float32)
    mu = jnp.mean(xf, axis=-1, keepdims=True)
    xc = xf - mu
    y = xc * lax.rsqrt(jnp.mean(xc * xc, axis=-1, keepdims=True) + eps)
    return (y * w.astype(jnp.float32)).astype(x.dtype)


def modulate(x, shift, scale):
    return x * (1 + scale) + shift


def depthwise_conv(u, w, b):
    pad = (w.shape[0] - 1) // 2
    y = lax.conv_general_dilated(u, w[:, None, :], window_strides=(1,), padding=[(pad, pad)],
                                 dimension_numbers=('NWC', 'WIO', 'NWC'),
                                 feature_group_count=u.shape[-1])
    return y + b


def chunked_scan(q, k, v, logg, s0):
    B, T, H, Dk = q.shape
    Dv = v.shape[-1]
    n = T // CHUNK
    scalar = logg.shape[-1] == 1
    mask = jnp.tril(jnp.ones((CHUNK, CHUNK), dtype=bool))[None, :, :, None, None]

    def split(a):
        return a.reshape(B, n, CHUNK, H, a.shape[-1]).swapaxes(0, 1)

    def step(S, inp):
        qc, kc, vc, gc = inp
        G = jnp.cumsum(gc, axis=1)
        diff = G[:, :, None] - G[:, None, :]
        dec = jnp.where(mask, jnp.exp(jnp.minimum(diff, 0.0)), 0.0)
        if scalar:
            scores = jnp.einsum('bihd,bjhd->bijh', qc, kc) * dec[..., 0]
        else:
            scores = jnp.einsum('bihd,bjhd,bijhd->bijh', qc, kc, dec)
        intra = jnp.einsum('bijh,bjhe->bihe', scores, vc)
        inter = jnp.einsum('bihd,bhde->bihe', qc * jnp.exp(G), S)
        G_last = G[:, -1:]
        S_new = jnp.exp(G_last[:, 0])[..., None] * S + jnp.einsum(
            'bjhd,bjhe->bhde', kc * jnp.exp(G_last - G), vc)
        return S_new, intra + inter

    S, o = lax.scan(step, s0, (split(q), split(k), split(v), split(logg)))
    o = o.swapaxes(0, 1).reshape(B, T, H, Dv)
    return o, S


def bidir_scan(ctx_in, lat_in):
    qc, kfc, kbc, vc, gfc, gbc = ctx_in
    ql, kfl, kbl, vl, gfl, gbl = lat_in
    B, _, H, Dk = qc.shape
    Dv = vc.shape[-1]
    s0 = jnp.zeros((B, H, Dk, Dv), qc.dtype)
    flip = lambda a: jnp.flip(a, axis=1)
    oc_f, S_f = chunked_scan(qc, kfc, vc, gfc, s0)
    ol_f, _ = chunked_scan(ql, kfl, vl, gfl, S_f)
    oc_b, S_b = chunked_scan(flip(qc), flip(kbc), flip(vc), flip(gbc), s0)
    ol_b, _ = chunked_scan(flip(ql), flip(kbl), flip(vl), flip(gbl), S_b)
    return oc_f + flip(oc_b), ol_f + flip(ol_b)


def gla_mixer(p_ctx, p_lat, gate_up, gate_b, norm_w):
    def prep(p):
        B, T, _ = p.shape
        q, k, v, r, lr = jnp.split(p, [GLA_KDIM, 2 * GLA_KDIM, 2 * GLA_KDIM + GLA_W,
                                       2 * GLA_KDIM + 2 * GLA_W], axis=-1)
        q = q.reshape(B, T, GLA_HEADS, GLA_HEAD_K) * GLA_HEAD_K ** -0.5
        k = k.reshape(B, T, GLA_HEADS, GLA_HEAD_K)
        v = v.reshape(B, T, GLA_HEADS, GLA_HEAD_V)
        z = jnp.einsum('btnr,nrk->btnk', lr.reshape(B, T, 2, GLA_RANK), gate_up) + gate_b
        logg = jax.nn.log_sigmoid(z) / GLA_GATE_TAU
        g_f = logg[:, :, 0].reshape(B, T, GLA_HEADS, GLA_HEAD_K)
        g_b = logg[:, :, 1].reshape(B, T, GLA_HEADS, GLA_HEAD_K)
        return (q, k, k, v, g_f, g_b), r

    in_c, r_c = prep(p_ctx)
    in_l, r_l = prep(p_lat)
    o_c, o_l = bidir_scan(in_c, in_l)

    def out(o, r):
        B, T = o.shape[:2]
        o = rms_norm(o, norm_w.reshape(GLA_HEADS, GLA_HEAD_V)).reshape(B, T, GLA_W)
        return o * jax.nn.silu(r)

    return out(o_c, r_c), out(o_l, r_l)


def ssd_mixer(p_ctx, p_lat, conv_w, conv_b, dt_bias, a_log, d_skip, norm_w):
    def prep(p):
        B, T, _ = p.shape
        z, xbc, dt = jnp.split(p, [SSD_W, SSD_W + SSD_CONV_CH], axis=-1)
        xbc = jax.nn.silu(depthwise_conv(xbc, conv_w, conv_b))
        xs, bm, cm = jnp.split(xbc, [SSD_W, SSD_W + SSD_GROUPS * SSD_STATE], axis=-1)
        rep = SSD_HEADS // SSD_GROUPS
        xs = xs.reshape(B, T, SSD_HEADS, SSD_HEAD_DIM)
        bm = jnp.repeat(bm.reshape(B, T, SSD_GROUPS, SSD_STATE), rep, axis=2)
        cm = jnp.repeat(cm.reshape(B, T, SSD_GROUPS, SSD_STATE), rep, axis=2)
        dt = jax.nn.softplus(dt.reshape(B, T, 2, SSD_HEADS) + dt_bias)
        logg = dt * (-jnp.exp(a_log))
        k_f = bm * dt[:, :, 0, :, None]
        k_b = bm * dt[:, :, 1, :, None]
        return (cm, k_f, k_b, xs, logg[:, :, 0, :, None], logg[:, :, 1, :, None]), z, xs

    in_c, z_c, x_c = prep(p_ctx)
    in_l, z_l, x_l = prep(p_lat)
    y_c, y_l = bidir_scan(in_c, in_l)

    def out(y, z, xs):
        B, T = y.shape[:2]
        y = (y + d_skip[:, None] * xs).reshape(B, T, SSD_W)
        return rms_norm(y * jax.nn.silu(z), norm_w)

    return out(y_c, z_c, x_c), out(y_l, z_l, x_l)


def apply_rope(t, cos, sin):
    half = t.shape[-1] // 2
    t1, t2 = t[..., :half], t[..., half:]
    return jnp.concatenate([t1 * cos - t2 * sin, t2 * cos + t1 * sin], axis=-1)


def retention_mixer(p_ctx, p_lat, cos, sin, norm_w):
    log_gamma = jnp.log1p(-jnp.exp2(-5.0 - jnp.arange(RET_HEADS, dtype=jnp.float32)))
    log_gamma = log_gamma.astype(p_lat.dtype)

    def prep(p, rotate):
        B, T, _ = p.shape
        q, k, v, g = jnp.split(p, 4, axis=-1)
        q = q.reshape(B, T, RET_HEADS, RET_HEAD_DIM) * RET_HEAD_DIM ** -0.5
        k = k.reshape(B, T, RET_HEADS, RET_HEAD_DIM)
        v = v.reshape(B, T, RET_HEADS, RET_HEAD_DIM)
        if rotate:
            q = apply_rope(q, cos, sin)
            k = apply_rope(k, cos, sin)
        lg = jnp.broadcast_to(log_gamma[:, None], (B, T, RET_HEADS, 1))
        return (q, k, k, v, lg, lg), g

    in_c, g_c = prep(p_ctx, False)
    in_l, g_l = prep(p_lat, True)
    o_c, o_l = bidir_scan(in_c, in_l)

    def out(o, g):
        B, T = o.shape[:2]
        o = layer_norm(o, norm_w.reshape(RET_HEADS, RET_HEAD_DIM)).reshape(B, T, RET_W)
        return o * jax.nn.silu(g)

    return out(o_c, g_c), out(o_l, g_l)


def swiglu(h, w13, w2):
    gate, up = jnp.split(h @ w13, 2, axis=-1)
    return (jax.nn.silu(gate) * up) @ w2


def _fwd_setup_inputs(seed: int = 0) -> dict:
    key = jax.random.key(seed)
    ks = jax.random.split(key, 24)
    f32 = jnp.float32
    nrm = lambda k, shape, s: jax.random.normal(k, shape, f32) * s
    gain = lambda k, shape: 1.0 + 0.05 * jax.random.normal(k, shape, f32)
    D = D_MODEL
    dt = jnp.exp(jax.random.uniform(ks[17], (DEPTH, 2, SSD_HEADS), f32)
                 * (math.log(0.1) - math.log(0.001)) + math.log(0.001))
    return {
        "x": nrm(ks[0], (BATCH, SEQ, D), 1.0),
        "c": nrm(ks[1], (BATCH, D), 1.0),
        "ctx": nrm(ks[2], (BATCH, CTX_LEN, D), 1.0),
        "c_ctx": nrm(ks[3], (D,), 1.0),
        "ada_w": nrm(ks[4], (DEPTH, D, 6 * D), D ** -0.5),
        "ada_b": nrm(ks[5], (DEPTH, 6 * D), 0.02),
        "norm_mix_pre": gain(ks[6], (DEPTH, D)),
        "norm_mix_post": gain(ks[7], (DEPTH, D)),
        "norm_ffn_pre": gain(ks[8], (DEPTH, D)),
        "norm_ffn_post": gain(ks[9], (DEPTH, D)),
        "w_in": nrm(ks[10], (DEPTH, D, IN_COLS), D ** -0.5),
        "w_out": nrm(ks[11], (DEPTH, MIX_W, D), MIX_W ** -0.5),
        "gla_gate_up": nrm(ks[12], (DEPTH, 2, GLA_RANK, GLA_KDIM), GLA_RANK ** -0.5),
        "gla_gate_b": nrm(ks[13], (DEPTH, 2, GLA_KDIM), 0.1),
        "gla_norm": gain(ks[14], (DEPTH, GLA_W)),
        "ssd_conv_w": nrm(ks[15], (DEPTH, SSD_CONV_W, SSD_CONV_CH), SSD_CONV_W ** -0.5),
        "ssd_conv_b": nrm(ks[16], (DEPTH, SSD_CONV_CH), 0.02),
        "ssd_dt_bias": dt + jnp.log(-jnp.expm1(-dt)),
        "ssd_a_log": jnp.log(jax.random.uniform(ks[18], (DEPTH, 2, SSD_HEADS), f32, 1.0, 16.0)),
        "ssd_d": gain(ks[19], (DEPTH, SSD_HEADS)),
        "ssd_norm": gain(ks[20], (DEPTH, SSD_W)),
        "ret_norm": gain(ks[21], (DEPTH, RET_W)),
        "ffn_w13": nrm(ks[22], (DEPTH, D, 2 * FFN_HIDDEN), D ** -0.5),
        "ffn_w2": nrm(ks[23], (DEPTH, FFN_HIDDEN, D), FFN_HIDDEN ** -0.5),
    }


def _fwd_reference(x, c, ctx, c_ctx, ada_w, ada_b, norm_mix_pre, norm_mix_post, norm_ffn_pre,
              norm_ffn_post, w_in, w_out, gla_gate_up, gla_gate_b, gla_norm, ssd_conv_w,
              ssd_conv_b, ssd_dt_bias, ssd_a_log, ssd_d, ssd_norm, ret_norm, ffn_w13, ffn_w2):
    T = x.shape[1]
    rows = T // GRID_W
    row = jnp.repeat(jnp.arange(rows), GRID_W).astype(jnp.float32)
    col = jnp.tile(jnp.arange(GRID_W), rows).astype(jnp.float32)
    n_freq = RET_HEAD_DIM // 4
    inv_freq = ROPE_BASE ** (-jnp.arange(n_freq, dtype=jnp.float32) / n_freq)
    ang = jnp.concatenate([row[:, None] * inv_freq, col[:, None] * inv_freq], axis=-1)
    cos = jnp.cos(ang).astype(x.dtype)[None, :, None, :]
    sin = jnp.sin(ang).astype(x.dtype)[None, :, None, :]

    lat, cx = x, ctx
    s1, s2 = GLA_COLS, GLA_COLS + SSD_COLS
    for l in range(DEPTH):
        last = l == DEPTH - 1
        mod_l = (jax.nn.silu(c) @ ada_w[l] + ada_b[l])[:, None, :]
        mod_c = (jax.nn.silu(c_ctx) @ ada_w[l] + ada_b[l])[None, None, :]
        sh1, sc1, gt1, sh2, sc2, gt2 = jnp.split(mod_l, 6, axis=-1)
        csh1, csc1, cgt1, csh2, csc2, cgt2 = jnp.split(mod_c, 6, axis=-1)

        p_l = modulate(rms_norm(lat, norm_mix_pre[l]), sh1, sc1) @ w_in[l]
        p_c = modulate(rms_norm(cx, norm_mix_pre[l]), csh1, csc1) @ w_in[l]
        gla_c, gla_l = gla_mixer(p_c[..., :s1], p_l[..., :s1],
                                 gla_gate_up[l], gla_gate_b[l], gla_norm[l])
        ssd_c, ssd_l = ssd_mixer(p_c[..., s1:s2], p_l[..., s1:s2], ssd_conv_w[l], ssd_conv_b[l],
                                 ssd_dt_bias[l], ssd_a_log[l], ssd_d[l], ssd_norm[l])
        ret_c, ret_l = retention_mixer(p_c[..., s2:], p_l[..., s2:], cos, sin, ret_norm[l])
        mixed_l = jnp.concatenate([gla_l, ssd_l, ret_l], axis=-1) @ w_out[l]
        lat = lat + gt1 * rms_norm(mixed_l, norm_mix_post[l])

        h_l = modulate(rms_norm(lat, norm_ffn_pre[l]), sh2, sc2)
        lat = lat + gt2 * rms_norm(swiglu(h_l, ffn_w13[l], ffn_w2[l]), norm_ffn_post[l])

        if not last:
            mixed_c = jnp.concatenate([gla_c, ssd_c, ret_c], axis=-1) @ w_out[l]
            cx = cx + cgt1 * rms_norm(mixed_c, norm_mix_post[l])
            h_c = modulate(rms_norm(cx, norm_ffn_pre[l]), csh2, csc2)
            cx = cx + cgt2 * rms_norm(swiglu(h_c, ffn_w13[l], ffn_w2[l]), norm_ffn_post[l])
    return lat


import jax as _jax
import jax.numpy as _jnp

TWIN_FORMAT = 'train_step'
FWD_PARAMS = ['x', 'c', 'ctx', 'c_ctx', 'ada_w', 'ada_b', 'norm_mix_pre', 'norm_mix_post', 'norm_ffn_pre', 'norm_ffn_post', 'w_in', 'w_out', 'gla_gate_up', 'gla_gate_b', 'gla_norm', 'ssd_conv_w', 'ssd_conv_b', 'ssd_dt_bias', 'ssd_a_log', 'ssd_d', 'ssd_norm', 'ret_norm', 'ffn_w13', 'ffn_w2']
TWIN_WEIGHTS = ['c_ctx', 'ada_w', 'ada_b', 'norm_mix_pre', 'norm_mix_post', 'norm_ffn_pre', 'norm_ffn_post', 'w_in', 'w_out', 'gla_gate_up', 'gla_gate_b', 'gla_norm', 'ssd_conv_w', 'ssd_conv_b', 'ssd_dt_bias', 'ssd_a_log', 'ssd_d', 'ssd_norm', 'ret_norm', 'ffn_w13', 'ffn_w2']
TWIN_DIFF_INPUT = 'x'
TWIN_INPUTS = ['x', 'c', 'ctx', 'c_ctx', 'ada_w', 'ada_b', 'norm_mix_pre', 'norm_mix_post', 'norm_ffn_pre', 'norm_ffn_post', 'w_in', 'w_out', 'gla_gate_up', 'gla_gate_b', 'gla_norm', 'ssd_conv_w', 'ssd_conv_b', 'ssd_dt_bias', 'ssd_a_log', 'ssd_d', 'ssd_norm', 'ret_norm', 'ffn_w13', 'ffn_w2', 'loss_target', 'm_c_ctx', 'm_ada_w', 'm_ada_b', 'm_norm_mix_pre', 'm_norm_mix_post', 'm_norm_ffn_pre', 'm_norm_ffn_post', 'm_w_in', 'm_w_out', 'm_gla_gate_up', 'm_gla_gate_b', 'm_gla_norm', 'm_ssd_conv_w', 'm_ssd_conv_b', 'm_ssd_dt_bias', 'm_ssd_a_log', 'm_ssd_d', 'm_ssd_norm', 'm_ret_norm', 'm_ffn_w13', 'm_ffn_w2', 'v_c_ctx', 'v_ada_w', 'v_ada_b', 'v_norm_mix_pre', 'v_norm_mix_post', 'v_norm_ffn_pre', 'v_norm_ffn_post', 'v_w_in', 'v_w_out', 'v_gla_gate_up', 'v_gla_gate_b', 'v_gla_norm', 'v_ssd_conv_w', 'v_ssd_conv_b', 'v_ssd_dt_bias', 'v_ssd_a_log', 'v_ssd_d', 'v_ssd_norm', 'v_ret_norm', 'v_ffn_w13', 'v_ffn_w2']
TWIN_OUTPUTS = ['loss', 'grad_x', 'grad_c_ctx', 'grad_ada_w', 'grad_ada_b', 'grad_norm_mix_pre', 'grad_norm_mix_post', 'grad_norm_ffn_pre', 'grad_norm_ffn_post', 'grad_w_in', 'grad_w_out', 'grad_gla_gate_up', 'grad_gla_gate_b', 'grad_gla_norm', 'grad_ssd_conv_w', 'grad_ssd_conv_b', 'grad_ssd_dt_bias', 'grad_ssd_a_log', 'grad_ssd_d', 'grad_ssd_norm', 'grad_ret_norm', 'grad_ffn_w13', 'grad_ffn_w2', 'delta_c_ctx', 'delta_ada_w', 'delta_ada_b', 'delta_norm_mix_pre', 'delta_norm_mix_post', 'delta_norm_ffn_pre', 'delta_norm_ffn_post', 'delta_w_in', 'delta_w_out', 'delta_gla_gate_up', 'delta_gla_gate_b', 'delta_gla_norm', 'delta_ssd_conv_w', 'delta_ssd_conv_b', 'delta_ssd_dt_bias', 'delta_ssd_a_log', 'delta_ssd_d', 'delta_ssd_norm', 'delta_ret_norm', 'delta_ffn_w13', 'delta_ffn_w2', 'new_m_c_ctx', 'new_m_ada_w', 'new_m_ada_b', 'new_m_norm_mix_pre', 'new_m_norm_mix_post', 'new_m_norm_ffn_pre', 'new_m_norm_ffn_post', 'new_m_w_in', 'new_m_w_out', 'new_m_gla_gate_up', 'new_m_gla_gate_b', 'new_m_gla_norm', 'new_m_ssd_conv_w', 'new_m_ssd_conv_b', 'new_m_ssd_dt_bias', 'new_m_ssd_a_log', 'new_m_ssd_d', 'new_m_ssd_norm', 'new_m_ret_norm', 'new_m_ffn_w13', 'new_m_ffn_w2', 'new_v_c_ctx', 'new_v_ada_w', 'new_v_ada_b', 'new_v_norm_mix_pre', 'new_v_norm_mix_post', 'new_v_norm_ffn_pre', 'new_v_norm_ffn_post', 'new_v_w_in', 'new_v_w_out', 'new_v_gla_gate_up', 'new_v_gla_gate_b', 'new_v_gla_norm', 'new_v_ssd_conv_w', 'new_v_ssd_conv_b', 'new_v_ssd_dt_bias', 'new_v_ssd_a_log', 'new_v_ssd_d', 'new_v_ssd_norm', 'new_v_ret_norm', 'new_v_ffn_w13', 'new_v_ffn_w2']
TWIN_LEAF_KINDS = {'loss': 'loss', 'grad_x': 'grad_x', 'grad_c_ctx': 'grad_w', 'grad_ada_w': 'grad_w', 'grad_ada_b': 'grad_w', 'grad_norm_mix_pre': 'grad_w', 'grad_norm_mix_post': 'grad_w', 'grad_norm_ffn_pre': 'grad_w', 'grad_norm_ffn_post': 'grad_w', 'grad_w_in': 'grad_w', 'grad_w_out': 'grad_w', 'grad_gla_gate_up': 'grad_w', 'grad_gla_gate_b': 'grad_w', 'grad_gla_norm': 'grad_w', 'grad_ssd_conv_w': 'grad_w', 'grad_ssd_conv_b': 'grad_w', 'grad_ssd_dt_bias': 'grad_w', 'grad_ssd_a_log': 'grad_w', 'grad_ssd_d': 'grad_w', 'grad_ssd_norm': 'grad_w', 'grad_ret_norm': 'grad_w', 'grad_ffn_w13': 'grad_w', 'grad_ffn_w2': 'grad_w', 'delta_c_ctx': 'delta_w', 'delta_ada_w': 'delta_w', 'delta_ada_b': 'delta_w', 'delta_norm_mix_pre': 'delta_w', 'delta_norm_mix_post': 'delta_w', 'delta_norm_ffn_pre': 'delta_w', 'delta_norm_ffn_post': 'delta_w', 'delta_w_in': 'delta_w', 'delta_w_out': 'delta_w', 'delta_gla_gate_up': 'delta_w', 'delta_gla_gate_b': 'delta_w', 'delta_gla_norm': 'delta_w', 'delta_ssd_conv_w': 'delta_w', 'delta_ssd_conv_b': 'delta_w', 'delta_ssd_dt_bias': 'delta_w', 'delta_ssd_a_log': 'delta_w', 'delta_ssd_d': 'delta_w', 'delta_ssd_norm': 'delta_w', 'delta_ret_norm': 'delta_w', 'delta_ffn_w13': 'delta_w', 'delta_ffn_w2': 'delta_w', 'new_m_c_ctx': 'new_m', 'new_m_ada_w': 'new_m', 'new_m_ada_b': 'new_m', 'new_m_norm_mix_pre': 'new_m', 'new_m_norm_mix_post': 'new_m', 'new_m_norm_ffn_pre': 'new_m', 'new_m_norm_ffn_post': 'new_m', 'new_m_w_in': 'new_m', 'new_m_w_out': 'new_m', 'new_m_gla_gate_up': 'new_m', 'new_m_gla_gate_b': 'new_m', 'new_m_gla_norm': 'new_m', 'new_m_ssd_conv_w': 'new_m', 'new_m_ssd_conv_b': 'new_m', 'new_m_ssd_dt_bias': 'new_m', 'new_m_ssd_a_log': 'new_m', 'new_m_ssd_d': 'new_m', 'new_m_ssd_norm': 'new_m', 'new_m_ret_norm': 'new_m', 'new_m_ffn_w13': 'new_m', 'new_m_ffn_w2': 'new_m', 'new_v_c_ctx': 'new_v', 'new_v_ada_w': 'new_v', 'new_v_ada_b': 'new_v', 'new_v_norm_mix_pre': 'new_v', 'new_v_norm_mix_post': 'new_v', 'new_v_norm_ffn_pre': 'new_v', 'new_v_norm_ffn_post': 'new_v', 'new_v_w_in': 'new_v', 'new_v_w_out': 'new_v', 'new_v_gla_gate_up': 'new_v', 'new_v_gla_gate_b': 'new_v', 'new_v_gla_norm': 'new_v', 'new_v_ssd_conv_w': 'new_v', 'new_v_ssd_conv_b': 'new_v', 'new_v_ssd_dt_bias': 'new_v', 'new_v_ssd_a_log': 'new_v', 'new_v_ssd_d': 'new_v', 'new_v_ssd_norm': 'new_v', 'new_v_ret_norm': 'new_v', 'new_v_ffn_w13': 'new_v', 'new_v_ffn_w2': 'new_v'}


def _forward(args):
    return _fwd_reference(*[args[k] for k in FWD_PARAMS])


def _output_shape():
    def fwd():
        inp = _fwd_setup_inputs(0)
        return _fwd_reference(*[inp[k] for k in FWD_PARAMS])
    out = _jax.eval_shape(fwd)
    return out.shape, out.dtype

N_MICROBATCH = 1
ADAM_LR = 0.001
ADAM_B1 = 0.9
ADAM_B2 = 0.999
ADAM_EPS = 1e-08
ADAM_WD = 0.01
ADAM_STEP = 10
PER_EXAMPLE_BATCH_AXIS = {'x': 0, 'c': 0, 'ctx': 0, 'loss_target': 0}
SHARED_INPUTS = []
_WEIGHT_DTYPES = {'c_ctx': _jnp.float32, 'ada_w': _jnp.float32, 'ada_b': _jnp.float32, 'norm_mix_pre': _jnp.float32, 'norm_mix_post': _jnp.float32, 'norm_ffn_pre': _jnp.float32, 'norm_ffn_post': _jnp.float32, 'w_in': _jnp.float32, 'w_out': _jnp.float32, 'gla_gate_up': _jnp.float32, 'gla_gate_b': _jnp.float32, 'gla_norm': _jnp.float32, 'ssd_conv_w': _jnp.float32, 'ssd_conv_b': _jnp.float32, 'ssd_dt_bias': _jnp.float32, 'ssd_a_log': _jnp.float32, 'ssd_d': _jnp.float32, 'ssd_norm': _jnp.float32, 'ret_norm': _jnp.float32, 'ffn_w13': _jnp.float32, 'ffn_w2': _jnp.float32}
MOMENT_SCALE = {'c_ctx': 4.519539e-01, 'ada_w': 2.811844e+00, 'ada_b': 5.771958e+00, 'norm_mix_pre': 1.056718e+00, 'norm_mix_post': 1.327128e+01, 'norm_ffn_pre': 8.933357e-01, 'norm_ffn_post': 1.323488e+01, 'w_in': 1.149474e+00, 'w_out': 1.759459e+00, 'gla_gate_up': 2.700922e-01, 'gla_gate_b': 4.226749e-01, 'gla_norm': 9.706148e-01, 'ssd_conv_w': 1.313373e+00, 'ssd_conv_b': 1.825996e+00, 'ssd_dt_bias': 1.143121e+00, 'ssd_a_log': 3.513973e+00, 'ssd_d': 2.493189e+00, 'ssd_norm': 2.478404e+00, 'ret_norm': 9.236359e-01, 'ffn_w13': 7.090795e-01, 'ffn_w2': 1.326374e+00}


def _to_microbatches(a, axis):
    t = _jnp.moveaxis(a, axis, 0)
    t = t.reshape((N_MICROBATCH, t.shape[0] // N_MICROBATCH) + t.shape[1:])
    return _jnp.moveaxis(t, 1, axis + 1)


def setup_inputs(seed: int = 0) -> dict:
    inp = _fwd_setup_inputs(seed)
    key = _jax.random.fold_in(_jax.random.key(seed), 7919)
    shape, _ = _output_shape()
    out = dict(inp)
    out["loss_target"] = _jax.random.normal(_jax.random.fold_in(key, 0), shape, _jnp.float32)
    for i, name in enumerate(TWIN_WEIGHTS):
        w = inp[name].astype(_jnp.float32)
        if MOMENT_SCALE is None:
            s = _jnp.sqrt(_jnp.mean(_jnp.square(w)) + 1e-30)
        else:
            s = MOMENT_SCALE[name]
        km, kv = _jax.random.split(_jax.random.fold_in(key, i + 1))
        out[name] = w
        out["m_" + name] = s * _jax.random.normal(km, w.shape, _jnp.float32)
        out["v_" + name] = (s * s) * _jax.random.uniform(kv, w.shape, _jnp.float32, 0.5, 1.5)
    if N_MICROBATCH > 1:
        for name, axis in PER_EXAMPLE_BATCH_AXIS.items():
            out[name] = _to_microbatches(out[name], axis)
    return {'x': out['x'], 'c': out['c'], 'ctx': out['ctx'], 'c_ctx': out['c_ctx'], 'ada_w': out['ada_w'], 'ada_b': out['ada_b'], 'norm_mix_pre': out['norm_mix_pre'], 'norm_mix_post': out['norm_mix_post'], 'norm_ffn_pre': out['norm_ffn_pre'], 'norm_ffn_post': out['norm_ffn_post'], 'w_in': out['w_in'], 'w_out': out['w_out'], 'gla_gate_up': out['gla_gate_up'], 'gla_gate_b': out['gla_gate_b'], 'gla_norm': out['gla_norm'], 'ssd_conv_w': out['ssd_conv_w'], 'ssd_conv_b': out['ssd_conv_b'], 'ssd_dt_bias': out['ssd_dt_bias'], 'ssd_a_log': out['ssd_a_log'], 'ssd_d': out['ssd_d'], 'ssd_norm': out['ssd_norm'], 'ret_norm': out['ret_norm'], 'ffn_w13': out['ffn_w13'], 'ffn_w2': out['ffn_w2'], 'loss_target': out['loss_target'], 'm_c_ctx': out['m_c_ctx'], 'm_ada_w': out['m_ada_w'], 'm_ada_b': out['m_ada_b'], 'm_norm_mix_pre': out['m_norm_mix_pre'], 'm_norm_mix_post': out['m_norm_mix_post'], 'm_norm_ffn_pre': out['m_norm_ffn_pre'], 'm_norm_ffn_post': out['m_norm_ffn_post'], 'm_w_in': out['m_w_in'], 'm_w_out': out['m_w_out'], 'm_gla_gate_up': out['m_gla_gate_up'], 'm_gla_gate_b': out['m_gla_gate_b'], 'm_gla_norm': out['m_gla_norm'], 'm_ssd_conv_w': out['m_ssd_conv_w'], 'm_ssd_conv_b': out['m_ssd_conv_b'], 'm_ssd_dt_bias': out['m_ssd_dt_bias'], 'm_ssd_a_log': out['m_ssd_a_log'], 'm_ssd_d': out['m_ssd_d'], 'm_ssd_norm': out['m_ssd_norm'], 'm_ret_norm': out['m_ret_norm'], 'm_ffn_w13': out['m_ffn_w13'], 'm_ffn_w2': out['m_ffn_w2'], 'v_c_ctx': out['v_c_ctx'], 'v_ada_w': out['v_ada_w'], 'v_ada_b': out['v_ada_b'], 'v_norm_mix_pre': out['v_norm_mix_pre'], 'v_norm_mix_post': out['v_norm_mix_post'], 'v_norm_ffn_pre': out['v_norm_ffn_pre'], 'v_norm_ffn_post': out['v_norm_ffn_post'], 'v_w_in': out['v_w_in'], 'v_w_out': out['v_w_out'], 'v_gla_gate_up': out['v_gla_gate_up'], 'v_gla_gate_b': out['v_gla_gate_b'], 'v_gla_norm': out['v_gla_norm'], 'v_ssd_conv_w': out['v_ssd_conv_w'], 'v_ssd_conv_b': out['v_ssd_conv_b'], 'v_ssd_dt_bias': out['v_ssd_dt_bias'], 'v_ssd_a_log': out['v_ssd_a_log'], 'v_ssd_d': out['v_ssd_d'], 'v_ssd_norm': out['v_ssd_norm'], 'v_ret_norm': out['v_ret_norm'], 'v_ffn_w13': out['v_ffn_w13'], 'v_ffn_w2': out['v_ffn_w2']}


def _loss(weights, diff, rest, loss_target):
    with _jax.named_scope("forward"):
        args = {**rest, TWIN_DIFF_INPUT: diff, **{k: w.astype(_WEIGHT_DTYPES[k]) for k, w in weights.items()}}
        y = _forward(args)
    with _jax.named_scope("loss_head"):
        err = _jnp.square(y.astype(_jnp.float32) - loss_target)
        return 0.5 * _jnp.sum(_jnp.mean(err, axis=-1)) if err.ndim else 0.5 * err


def _adamw(w, g, m, v):
    m = ADAM_B1 * m + (1.0 - ADAM_B1) * g
    v = ADAM_B2 * v + (1.0 - ADAM_B2) * _jnp.square(g)
    m_hat = m / (1.0 - ADAM_B1 ** ADAM_STEP)
    v_hat = v / (1.0 - ADAM_B2 ** ADAM_STEP)
    delta = -ADAM_LR * (m_hat / (_jnp.sqrt(v_hat) + ADAM_EPS) + ADAM_WD * w)
    return delta, m, v


def reference(x, c, ctx, c_ctx, ada_w, ada_b, norm_mix_pre, norm_mix_post, norm_ffn_pre, norm_ffn_post, w_in, w_out, gla_gate_up, gla_gate_b, gla_norm, ssd_conv_w, ssd_conv_b, ssd_dt_bias, ssd_a_log, ssd_d, ssd_norm, ret_norm, ffn_w13, ffn_w2, loss_target, m_c_ctx, m_ada_w, m_ada_b, m_norm_mix_pre, m_norm_mix_post, m_norm_ffn_pre, m_norm_ffn_post, m_w_in, m_w_out, m_gla_gate_up, m_gla_gate_b, m_gla_norm, m_ssd_conv_w, m_ssd_conv_b, m_ssd_dt_bias, m_ssd_a_log, m_ssd_d, m_ssd_norm, m_ret_norm, m_ffn_w13, m_ffn_w2, v_c_ctx, v_ada_w, v_ada_b, v_norm_mix_pre, v_norm_mix_post, v_norm_ffn_pre, v_norm_ffn_post, v_w_in, v_w_out, v_gla_gate_up, v_gla_gate_b, v_gla_norm, v_ssd_conv_w, v_ssd_conv_b, v_ssd_dt_bias, v_ssd_a_log, v_ssd_d, v_ssd_norm, v_ret_norm, v_ffn_w13, v_ffn_w2):
    given = dict(x=x, c=c, ctx=ctx, c_ctx=c_ctx, ada_w=ada_w, ada_b=ada_b, norm_mix_pre=norm_mix_pre, norm_mix_post=norm_mix_post, norm_ffn_pre=norm_ffn_pre, norm_ffn_post=norm_ffn_post, w_in=w_in, w_out=w_out, gla_gate_up=gla_gate_up, gla_gate_b=gla_gate_b, gla_norm=gla_norm, ssd_conv_w=ssd_conv_w, ssd_conv_b=ssd_conv_b, ssd_dt_bias=ssd_dt_bias, ssd_a_log=ssd_a_log, ssd_d=ssd_d, ssd_norm=ssd_norm, ret_norm=ret_norm, ffn_w13=ffn_w13, ffn_w2=ffn_w2, loss_target=loss_target, m_c_ctx=m_c_ctx, m_ada_w=m_ada_w, m_ada_b=m_ada_b, m_norm_mix_pre=m_norm_mix_pre, m_norm_mix_post=m_norm_mix_post, m_norm_ffn_pre=m_norm_ffn_pre, m_norm_ffn_post=m_norm_ffn_post, m_w_in=m_w_in, m_w_out=m_w_out, m_gla_gate_up=m_gla_gate_up, m_gla_gate_b=m_gla_gate_b, m_gla_norm=m_gla_norm, m_ssd_conv_w=m_ssd_conv_w, m_ssd_conv_b=m_ssd_conv_b, m_ssd_dt_bias=m_ssd_dt_bias, m_ssd_a_log=m_ssd_a_log, m_ssd_d=m_ssd_d, m_ssd_norm=m_ssd_norm, m_ret_norm=m_ret_norm, m_ffn_w13=m_ffn_w13, m_ffn_w2=m_ffn_w2, v_c_ctx=v_c_ctx, v_ada_w=v_ada_w, v_ada_b=v_ada_b, v_norm_mix_pre=v_norm_mix_pre, v_norm_mix_post=v_norm_mix_post, v_norm_ffn_pre=v_norm_ffn_pre, v_norm_ffn_post=v_norm_ffn_post, v_w_in=v_w_in, v_w_out=v_w_out, v_gla_gate_up=v_gla_gate_up, v_gla_gate_b=v_gla_gate_b, v_gla_norm=v_gla_norm, v_ssd_conv_w=v_ssd_conv_w, v_ssd_conv_b=v_ssd_conv_b, v_ssd_dt_bias=v_ssd_dt_bias, v_ssd_a_log=v_ssd_a_log, v_ssd_d=v_ssd_d, v_ssd_norm=v_ssd_norm, v_ret_norm=v_ret_norm, v_ffn_w13=v_ffn_w13, v_ffn_w2=v_ffn_w2)
    weights = {n: given[n] for n in TWIN_WEIGHTS}
    shared = {n: given[n] for n in SHARED_INPUTS}
    per_example = {n: given[n] for n in ['x', 'c', 'ctx']}
    grad_fn = _jax.value_and_grad(_loss, argnums=(0, 1))

    def one_microbatch(ex, loss_target):
        ex = dict(ex)
        diff = ex.pop(TWIN_DIFF_INPUT)
        return grad_fn(weights, diff, {**shared, **ex}, loss_target)

    if N_MICROBATCH == 1:
        loss, (grad_w, grad_x) = one_microbatch(per_example, given["loss_target"])
    else:
        def body(carry, xs):
            loss_sum, grad_sum = carry
            l_k, (gw_k, gx_k) = one_microbatch(xs[0], xs[1])
            with _jax.named_scope("update"):
                return (loss_sum + l_k, _jax.tree.map(_jnp.add, grad_sum, gw_k)), gx_k

        init = (_jnp.zeros((), _jnp.float32), _jax.tree.map(_jnp.zeros_like, weights))
        (loss, grad_w), grad_x = _jax.lax.scan(body, init, (per_example, given["loss_target"]))
    with _jax.named_scope("update"):
        delta_w, new_m, new_v = {}, {}, {}
        for n in TWIN_WEIGHTS:
            delta_w[n], new_m[n], new_v[n] = _adamw(weights[n], grad_w[n], given["m_" + n], given["v_" + n])
    return (loss, grad_x, *[grad_w[n] for n in TWIN_WEIGHTS], *[delta_w[n] for n in TWIN_WEIGHTS],
            *[new_m[n] for n in TWIN_WEIGHTS], *[new_v[n] for n in TWIN_WEIGHTS])
```

```python
import functools
import math

import jax
import jax.numpy as jnp
from jax import lax
from jax.experimental import pallas as pl
from jax.experimental.pallas import tpu as pltpu

F32 = jnp.float32
BF16 = jnp.bfloat16
HI = lax.Precision.HIGHEST

D = 1024
DEPTH = 4
GRID_W = 64
RMS_EPS = 1e-6
GLA_TAU = 16.0
FFN_H = 2816
IN_COLS = 3376
IN_PAD = 3456
N_DEV = 8
ADAM_LR, ADAM_B1, ADAM_B2, ADAM_EPS, ADAM_WD, ADAM_STEP = 0.001, 0.9, 0.999, 1e-08, 0.01, 10

VMEM_LIMIT = 48 * 1024 * 1024
CHUNK = 64

_SEGS = [
    ("gla_q", 0, 128), ("gla_k", 128, 128), ("gla_v", 256, 256), ("gla_r", 512, 256),
    ("ssd_z", 800, 512), ("ssd_xbc", 1312, 1024),
    ("ret_q", 2352, 256), ("ret_k", 2608, 256), ("ret_v", 2864, 256), ("ret_g", 3120, 256),
    ("gla_lr", 768, 32), ("ssd_dt", 2336, 16),
]
_OFF = {}
_o = 0
for _n, _s, _z in _SEGS:
    _OFF[_n] = (_o, _z)
    _o += _z
assert _o == IN_COLS


def _permute_cols(w):
    parts = [w[..., s:s + z] for _, s, z in _SEGS]
    parts.append(jnp.zeros(w.shape[:-1] + (IN_PAD - IN_COLS,), w.dtype))
    return jnp.concatenate(parts, axis=-1)


def _unpermute_cols(w):
    order = sorted(_SEGS, key=lambda t: t[1])
    return jnp.concatenate([w[..., _OFF[n][0]:_OFF[n][0] + z] for n, _, z in order], axis=-1)


def _pick(n, cands):
    for c in cands:
        if n % c == 0:
            return c
    return n


def _params(sem=None):
    kw = dict(vmem_limit_bytes=VMEM_LIMIT)
    if sem is not None:
        kw["dimension_semantics"] = sem
    return pltpu.CompilerParams(**kw)


def _mm(a, b, *, trans_b=False, name):
    M, K = a.shape
    N = b.shape[0] if trans_b else b.shape[1]
    assert (b.shape[1] if trans_b else b.shape[0]) == K
    tm = _pick(M, (1408, 1088, 1024, 512, 256, 128, 64, 32, 16))
    tn = _pick(N, (512, 384, 256, 128))
    tk = _pick(K, (2176, 1408, 1024, 512, 384, 256, 128))
    nk = K // tk
    dims = (((1,), (1,)), ((), ())) if trans_b else (((1,), (0,)), ((), ()))

    def body(a_ref, b_ref, o_ref, acc_ref):
        k = pl.program_id(2)

        @pl.when(k == 0)
        def _():
            acc_ref[...] = jnp.zeros_like(acc_ref)

        acc_ref[...] += lax.dot_general(a_ref[...].astype(BF16), b_ref[...].astype(BF16), dims,
                                        preferred_element_type=F32)

        @pl.when(k == nk - 1)
        def _():
            o_ref[...] = acc_ref[...]

    b_spec = (pl.BlockSpec((tn, tk), lambda i, j, k: (j, k)) if trans_b
              else pl.BlockSpec((tk, tn), lambda i, j, k: (k, j)))
    return pl.pallas_call(
        body, name=name,
        out_shape=jax.ShapeDtypeStruct((M, N), F32),
        grid=(M // tm, N // tn, nk),
        in_specs=[pl.BlockSpec((tm, tk), lambda i, j, k: (i, k)), b_spec],
        out_specs=pl.BlockSpec((tm, tn), lambda i, j, k: (i, j)),
        scratch_shapes=[pltpu.VMEM((tm, tn), F32)],
        compiler_params=_params(("parallel", "parallel", "arbitrary")),
    )(a, b)


@jax.custom_vjp
def linear(a, w, gslot):
    return _mm(a, w, name="mm_fwd")


def _linear_fwd(a, w, gslot):
    return _mm(a, w, name="mm_fwd"), (a, w)


def _linear_bwd(res, g):
    a, w = res
    da = _mm(g, w, trans_b=True, name="mm_dx")
    dw = _mm(a.T.astype(BF16), g, name="mm_dw")
    return da, jnp.zeros_like(w), dw


linear.defvjp(_linear_fwd, _linear_bwd)


def _norm_fwd_call(x, w, a2, b2, tr):
    T, W = x.shape

    def body(x_ref, w_ref, a_ref, b_ref, y_ref):
        seg = jnp.minimum(pl.program_id(0), 1)
        xv = x_ref[...]
        rstd = lax.rsqrt(jnp.mean(xv * xv, axis=-1, keepdims=True) + RMS_EPS)
        y_ref[...] = a_ref[pl.ds(seg, 1), :] * (xv * rstd * w_ref[...]) + b_ref[pl.ds(seg, 1), :]

    return pl.pallas_call(
        body, name="norm_fwd",
        out_shape=jax.ShapeDtypeStruct((T, W), F32),
        grid=(T // tr,),
        in_specs=[pl.BlockSpec((tr, W), lambda i: (i, 0)), pl.BlockSpec((1, W), lambda i: (0, 0)),
                  pl.BlockSpec((8, W), lambda i: (0, 0)), pl.BlockSpec((8, W), lambda i: (0, 0))],
        out_specs=pl.BlockSpec((tr, W), lambda i: (i, 0)),
        compiler_params=_params(("parallel",)),
    )(x, w.reshape(1, W), a2, b2)


def _norm_bwd_call(x, w, a2, dy, tr):
    T, W = x.shape

    def body(x_ref, w_ref, a_ref, dy_ref, dx_ref, dw_ref, da_ref, db_ref):
        i = pl.program_id(0)
        seg = jnp.minimum(i, 1)

        @pl.when(i == 0)
        def _():
            dw_ref[...] = jnp.zeros_like(dw_ref)
            da_ref[...] = jnp.zeros_like(da_ref)
            db_ref[...] = jnp.zeros_like(db_ref)

        xv = x_ref[...]
        g = dy_ref[...]
        wv = w_ref[...]
        rstd = lax.rsqrt(jnp.mean(xv * xv, axis=-1, keepdims=True) + RMS_EPS)
        xh = xv * rstd
        da_ref[pl.ds(seg, 1), :] += jnp.sum(g * (xh * wv), axis=0, keepdims=True)
        db_ref[pl.ds(seg, 1), :] += jnp.sum(g, axis=0, keepdims=True)
        gy = g * a_ref[pl.ds(seg, 1), :]
        dw_ref[0:1, :] += jnp.sum(gy * xh, axis=0, keepdims=True)
        gx = gy * wv
        dx_ref[...] = rstd * (gx - xh * jnp.mean(gx * xh, axis=-1, keepdims=True))

    acc = jax.ShapeDtypeStruct((8, W), F32)
    acc_spec = pl.BlockSpec((8, W), lambda i: (0, 0))
    return pl.pallas_call(
        body, name="norm_bwd",
        out_shape=(jax.ShapeDtypeStruct((T, W), F32), acc, acc, acc),
        grid=(T // tr,),
        in_specs=[pl.BlockSpec((tr, W), lambda i: (i, 0)), pl.BlockSpec((1, W), lambda i: (0, 0)),
                  acc_spec, pl.BlockSpec((tr, W), lambda i: (i, 0))],
        out_specs=(pl.BlockSpec((tr, W), lambda i: (i, 0)), acc_spec, acc_spec, acc_spec),
        compiler_params=_params(("arbitrary",)),
    )(x, w.reshape(1, W), a2, dy)


@functools.lru_cache(maxsize=None)
def _make_norm(tr):
    @jax.custom_vjp
    def norm_affine(x, w, a2, b2):
        return _norm_fwd_call(x, w, a2, b2, tr)

    def fwd(x, w, a2, b2):
        return _norm_fwd_call(x, w, a2, b2, tr), (x, w, a2)

    def bwd(res, dy):
        x, w, a2 = res
        dx, dw, da, db = _norm_bwd_call(x, w, a2, dy, tr)
        return dx, dw[0], da, db

    norm_affine.defvjp(fwd, bwd)
    return norm_affine


_SCAN_CFG = {
    "gla": dict(H=4, Dk=32, Dv=64, nh=4, scalar=False),
    "ssd": dict(H=8, Dk=128, Dv=64, nh=2, scalar=True),
    "ret": dict(H=4, Dk=64, Dv=64, nh=4, scalar=True),
}
GPAD = 8


def _log2(n):
    r = int(math.log2(n))
    assert 1 << r == n
    return r


def _iota(shape, dim):
    return lax.broadcasted_iota(jnp.int32, shape, dim)


def _dot(a, b, dims, precision=None):
    return lax.dot_general(a, b, (dims, ((), ())), preferred_element_type=F32, precision=precision)


_NN = ((1,), (0,))
_NT = ((1,), (1,))
_TN = ((0,), (0,))


def _bf(x):
    return x.astype(BF16)


class _ScanMath:
    def __init__(self, cfg, C, reverse):
        self.C, self.reverse = C, reverse
        self.Dk, self.Dv, self.nh, self.scalar = cfg["Dk"], cfg["Dv"], cfg["nh"], cfg["scalar"]
        self.Wk, self.Wv = self.nh * self.Dk, self.nh * self.Dv
        self.nsg = cfg["H"] // self.nh
        C_, nh, Wk, Wv = C, self.nh, self.Wk, self.Wv
        lk, lv, lc = _log2(self.Dk), _log2(self.Dv), _log2(C_)
        r, c = _iota((C_, C_), 0), _iota((C_, C_), 1)
        self.L = ((c >= r) if reverse else (c <= r)).astype(F32)
        self.Lsuf = ((c <= r) if reverse else (c >= r)).astype(F32)
        i, j = _iota((C_, nh * C_), 0), _iota((C_, nh * C_), 1) & (C_ - 1)
        self.Mst = (j >= i) if reverse else (j <= i)
        self.LTt = ((i >= j) if reverse else (i <= j)).astype(F32)
        self.km = [((_iota((1, Wk), 1) >> lk) == h).astype(F32) for h in range(nh)]
        self.vm = [((_iota((1, Wv), 1) >> lv) == h).astype(F32) for h in range(nh)]
        self.BD = ((_iota((Wv, Wk), 0) >> lv) == (_iota((Wv, Wk), 1) >> lk)).astype(F32)
        self.last = 0 if reverse else C_ - 1
        self.lk, self.lc = lk, lc

    def Ek(self, s):
        return (_iota((GPAD, self.Wk), 0) == (_iota((GPAD, self.Wk), 1) >> self.lk) + s * self.nh).astype(F32)

    def Ec(self, s):
        n = self.nh * self.C
        return (_iota((GPAD, n), 0) == (_iota((GPAD, n), 1) >> self.lc) + s * self.nh).astype(F32)

    def kstack(self, x):
        return jnp.concatenate([x * self.km[h] for h in range(self.nh)], axis=0)

    def vstack(self, x):
        return jnp.concatenate([x * self.vm[h] for h in range(self.nh)], axis=0)

    def unstack(self, R, masks):
        C = self.C
        out = R[0:C] * masks[0]
        for h in range(1, self.nh):
            out = out + R[h * C:(h + 1) * C] * masks[h]
        return out

    def chunk(self, s, qs, ks, g):
        C = self.C
        if self.scalar:
            gk = _dot(g, self.Ek(s), _NN, HI)
        else:
            gk = g
        Gk = _dot(self.L, gk, _NN, HI)
        Glast = Gk[self.last:self.last + 1, :]
        out = dict(Gk=Gk, Glast=Glast, eG=jnp.exp(Gk), eGl=jnp.exp(Glast - Gk), eGlast=jnp.exp(Glast))
        if self.scalar:
            gc = _dot(g, self.Ec(s), _NN, HI)
            Gc = _dot(self.L, gc, _NN, HI)
            Gr = jnp.sum(gc * self.LTt, axis=0, keepdims=True)
            dec = jnp.where(self.Mst, jnp.exp(jnp.minimum(Gc - Gr, 0.0)), 0.0)
            qt, kt = qs, ks
            A = _dot(_bf(qt), _bf(self.kstack(kt)), _NT) * dec
            out.update(dec=dec, qt=qt, kt=kt, A=A)
        else:
            Gm = Gk[C // 2:C // 2 + 1, :]
            eq, ek = jnp.exp(Gk - Gm), jnp.exp(Gm - Gk)
            qt, kt = qs * eq, ks * ek
            A = jnp.where(self.Mst, _dot(_bf(qt), _bf(self.kstack(kt)), _NT), 0.0)
            out.update(eq=eq, ek=ek, qt=qt, kt=kt, A=A)
        return out


def _chunk_index(p, n, nc, reverse):
    if not reverse:
        return p
    return jnp.where(p < nc, nc - 1 - p, n - 1 + nc - p)


def _scan_fwd_call(kind, reverse, q, k, v, g, Tc):
    cfg = _SCAN_CFG[kind]
    C = CHUNK
    T, HK = q.shape
    HV = v.shape[1]
    n, nc = T // C, Tc // C
    GW = g.shape[1]

    def body(q_ref, k_ref, v_ref, g_ref, o_ref, st_ref, S_ref):
        m = _ScanMath(cfg, C, reverse)

        @pl.when(pl.program_id(0) == 0)
        def _():
            S_ref[...] = jnp.zeros_like(S_ref)

        for s in range(m.nsg):
            ksl, vsl = slice(s * m.Wk, (s + 1) * m.Wk), slice(s * m.Wv, (s + 1) * m.Wv)
            qs, ks, vs = q_ref[:, ksl], k_ref[:, ksl], v_ref[:, vsl]
            g_blk = g_ref[...] if m.scalar else g_ref[:, ksl]
            ch = m.chunk(s, qs, ks, g_blk)
            S = S_ref[vsl, :]
            o = _dot(_bf(ch["A"]), _bf(m.vstack(vs)), _NN) + _dot(_bf(qs * ch["eG"]), _bf(S), _NT)
            o_ref[:, vsl] = o
            st_ref[0, vsl, :] = S
            S_ref[vsl, :] = S * ch["eGlast"] + _dot(_bf(vs), _bf(ks * ch["eGl"]), _TN) * m.BD

    sg = cfg["H"] // cfg["nh"]
    Wk, Wv = cfg["nh"] * cfg["Dk"], cfg["nh"] * cfg["Dv"]
    idx = lambda p: (_chunk_index(p, n, nc, reverse), 0)
    return pl.pallas_call(
        body, name=f"scan_fwd_{kind}_{'r' if reverse else 'f'}",
        out_shape=(jax.ShapeDtypeStruct((T, HV), F32), jax.ShapeDtypeStruct((n, sg * Wv, Wk), F32)),
        grid=(n,),
        in_specs=[pl.BlockSpec((C, HK), idx), pl.BlockSpec((C, HK), idx), pl.BlockSpec((C, HV), idx),
                  pl.BlockSpec((C, GW), idx)],
        out_specs=(pl.BlockSpec((C, HV), idx),
                   pl.BlockSpec((1, sg * Wv, Wk), lambda p: (_chunk_index(p, n, nc, reverse), 0, 0))),
        scratch_shapes=[pltpu.VMEM((sg * Wv, Wk), F32)],
        compiler_params=_params(("arbitrary",)),
    )(q, k, v, g)


def _scan_bwd_call(kind, reverse, need_dg, q, k, v, g, st, do, Tc):
    cfg = _SCAN_CFG[kind]
    C = CHUNK
    T, HK = q.shape
    HV = v.shape[1]
    n, nc = T // C, Tc // C
    GW = g.shape[1]

    def body(q_ref, k_ref, v_ref, g_ref, st_ref, do_ref, dq_ref, dk_ref, dv_ref, dg_ref, dS_ref):
        m = _ScanMath(cfg, C, reverse)

        @pl.when(pl.program_id(0) == 0)
        def _():
            dS_ref[...] = jnp.zeros_like(dS_ref)

        dg_acc = jnp.zeros((C, GPAD), F32)
        for s in range(m.nsg):
            ksl, vsl = slice(s * m.Wk, (s + 1) * m.Wk), slice(s * m.Wv, (s + 1) * m.Wv)
            qs, ks, vs, dos = q_ref[:, ksl], k_ref[:, ksl], v_ref[:, vsl], do_ref[:, vsl]
            g_blk = g_ref[...] if m.scalar else g_ref[:, ksl]
            ch = m.chunk(s, qs, ks, g_blk)
            S = st_ref[0, vsl, :]
            dS = dS_ref[vsl, :]
            A, qt, kt = ch["A"], ch["qt"], ch["kt"]
            dA = _dot(_bf(dos), _bf(m.vstack(vs)), _NT)
            dAm = dA * ch["dec"] if m.scalar else jnp.where(m.Mst, dA, 0.0)
            kst = _bf(m.kstack(kt))
            dv = m.unstack(_dot(_bf(A), _bf(dos), _TN), m.vm) + _dot(_bf(ks * ch["eGl"]), _bf(dS), _NT)
            dv_ref[:, vsl] = dv
            dq_i = _dot(_bf(dAm), kst, _NN)
            dq_x = ch["eG"] * _dot(_bf(dos), _bf(S), _NN)
            dq_ref[:, ksl] = (dq_i if m.scalar else dq_i * ch["eq"]) + dq_x
            dk_i = m.unstack(_dot(_bf(dAm), _bf(qt), _TN), m.km)
            dk_x = ch["eGl"] * _dot(_bf(vs), _bf(dS), _NN)
            dk_ref[:, ksl] = (dk_i if m.scalar else dk_i * ch["ek"]) + dk_x
            if need_dg:
                bnd = (ch["eGlast"] * jnp.sum(dS * S, axis=0, keepdims=True)
                       + jnp.sum(ks * dk_x, axis=0, keepdims=True))
                X = (_bf(qt).astype(F32) * dq_i - _bf(kt).astype(F32) * dk_i) + (qs * dq_x - ks * dk_x)
                Dg = _dot(m.Lsuf, X, _NN, HI) + bnd
                if m.scalar:
                    dg_acc = dg_acc + _dot(Dg, m.Ek(s), _NT, HI)
                else:
                    dg_ref[:, ksl] = Dg
            dS_ref[vsl, :] = dS * ch["eGlast"] + _dot(_bf(dos), _bf(qs * ch["eG"]), _TN) * m.BD
        if m.scalar or not need_dg:
            dg_ref[...] = dg_acc if m.scalar else jnp.zeros_like(dg_ref)

    sg = cfg["H"] // cfg["nh"]
    Wk, Wv = cfg["nh"] * cfg["Dk"], cfg["nh"] * cfg["Dv"]
    idx = lambda p: (_chunk_index(n - 1 - p, n, nc, reverse), 0)
    bk, bv, bg = pl.BlockSpec((C, HK), idx), pl.BlockSpec((C, HV), idx), pl.BlockSpec((C, GW), idx)
    return pl.pallas_call(
        body, name=f"scan_bwd_{kind}_{'r' if reverse else 'f'}",
        out_shape=(jax.ShapeDtypeStruct((T, HK), F32), jax.ShapeDtypeStruct((T, HK), F32),
                   jax.ShapeDtypeStruct((T, HV), F32), jax.ShapeDtypeStruct((T, GW), F32)),
        grid=(n,),
        in_specs=[bk, bk, bv, bg,
                  pl.BlockSpec((1, sg * Wv, Wk), lambda p: (_chunk_index(n - 1 - p, n, nc, reverse), 0, 0)), bv],
        out_specs=(bk, bk, bv, bg),
        scratch_shapes=[pltpu.VMEM((sg * Wv, Wk), F32)],
        compiler_params=_params(("arbitrary",)),
    )(q, k, v, g, st, do)


@functools.lru_cache(maxsize=None)
def _make_scan(kind, reverse, need_dg, Tc):
    @jax.custom_vjp
    def scan(q, k, v, g):
        return _scan_fwd_call(kind, reverse, q, k, v, g, Tc)[0]

    def fwd(q, k, v, g):
        o, st = _scan_fwd_call(kind, reverse, q, k, v, g, Tc)
        return o, (q, k, v, g, st)

    def bwd(res, do):
        q, k, v, g, st = res
        return _scan_bwd_call(kind, reverse, need_dg, q, k, v, g, st, do, Tc)

    scan.defvjp(fwd, bwd)
    return scan


def _bidir(kind, need_dg, Tc, q, kf, kb, v, gf, gb):
    return (_make_scan(kind, False, need_dg, Tc)(q, kf, v, gf)
            + _make_scan(kind, True, need_dg, Tc)(q, kb, v, gb))


def _seg(P, name):
    o, z = _OFF[name]
    return P[:, o:o + z]


def _silu(x):
    return x * jax.nn.sigmoid(x)


def _gla_mixer(P, sp, l, Tc):
    T = P.shape[0]
    q = _seg(P, "gla_q") * (32 ** -0.5)
    k, v, r = _seg(P, "gla_k"), _seg(P, "gla_v"), _seg(P, "gla_r")
    lr = _seg(P, "gla_lr").reshape(T, 2, 16)
    z = jnp.einsum("tnr,nrk->tnk", lr, sp["gla_gate_up"][l]) + sp["gla_gate_b"][l]
    logg = jax.nn.log_sigmoid(z) / GLA_TAU
    o = _bidir("gla", True, Tc, q, k, k, v, logg[:, 0], logg[:, 1])
    o4 = o.reshape(T, 4, 64)
    o4 = o4 * lax.rsqrt(jnp.mean(o4 * o4, axis=-1, keepdims=True) + RMS_EPS) * sp["gla_norm"][l].reshape(4, 64)
    return o4.reshape(T, 256) * _silu(r)


def _dwconv(u, w, b, Tc):
    def one(seg):
        n = seg.shape[0]
        p = jnp.pad(seg, ((2, 2), (0, 0)))
        acc = p[0:n] * w[0]
        for t in range(1, 5):
            acc = acc + p[t:t + n] * w[t]
        return acc + b

    return jnp.concatenate([one(u[:Tc]), one(u[Tc:])], axis=0)


def _ssd_mixer(P, sp, l, Tc):
    T = P.shape[0]
    z, xbc = _seg(P, "ssd_z"), _seg(P, "ssd_xbc")
    dt = _seg(P, "ssd_dt").reshape(T, 2, 8)
    xbc = _silu(_dwconv(xbc, sp["ssd_conv_w"][l], sp["ssd_conv_b"][l], Tc))
    xs, bm, cm = xbc[:, :512], xbc[:, 512:768], xbc[:, 768:]
    dt = jax.nn.softplus(dt + sp["ssd_dt_bias"][l])
    logg = dt * (-jnp.exp(sp["ssd_a_log"][l]))

    def rep(a):
        return jnp.broadcast_to(a.reshape(T, 2, 1, 128), (T, 2, 4, 128)).reshape(T, 8, 128)

    q = rep(cm).reshape(T, 1024)
    bmr = rep(bm)
    kf = (bmr * dt[:, 0, :, None]).reshape(T, 1024)
    kb = (bmr * dt[:, 1, :, None]).reshape(T, 1024)
    y = _bidir("ssd", True, Tc, q, kf, kb, xs, logg[:, 0], logg[:, 1])
    y = y + (sp["ssd_d"][l][:, None] * xs.reshape(T, 8, 64)).reshape(T, 512)
    ones, zeros = jnp.ones((8, 512), F32), jnp.zeros((8, 512), F32)
    return _make_norm(Tc)(y * _silu(z), sp["ssd_norm"][l], ones, zeros)


def _rope_tables(Tl, Tc):
    rows = Tl // GRID_W
    row = jnp.repeat(jnp.arange(rows), GRID_W).astype(F32)
    col = jnp.tile(jnp.arange(GRID_W), rows).astype(F32)
    inv_freq = 10000.0 ** (-jnp.arange(16, dtype=F32) / 16)
    ang = jnp.concatenate([row[:, None] * inv_freq, col[:, None] * inv_freq], axis=-1)
    cos = jnp.concatenate([jnp.ones((Tc, 32), F32), jnp.cos(ang)], axis=0)
    sin = jnp.concatenate([jnp.zeros((Tc, 32), F32), jnp.sin(ang)], axis=0)
    return cos[:, None, :], sin[:, None, :]


def _rope(t, cos, sin):
    t1, t2 = t[..., :32], t[..., 32:]
    return jnp.concatenate([t1 * cos - t2 * sin, t2 * cos + t1 * sin], axis=-1)


def _ret_mixer(P, sp, l, Tc, cos, sin):
    T = P.shape[0]
    q = (_seg(P, "ret_q") * (64 ** -0.5)).reshape(T, 4, 64)
    k = _seg(P, "ret_k").reshape(T, 4, 64)
    v, g = _seg(P, "ret_v"), _seg(P, "ret_g")
    q = _rope(q, cos, sin).reshape(T, 256)
    k = _rope(k, cos, sin).reshape(T, 256)
    log_gamma = jnp.log1p(-jnp.exp2(-5.0 - jnp.arange(4, dtype=F32)))
    lg = jnp.broadcast_to(jnp.concatenate([log_gamma, jnp.zeros((GPAD - 4,), F32)])[None, :], (T, GPAD))
    o = _bidir("ret", False, Tc, q, k, k, v, lg, lg).reshape(T, 4, 64)
    mu = jnp.mean(o, axis=-1, keepdims=True)
    oc = o - mu
    o = oc * lax.rsqrt(jnp.mean(oc * oc, axis=-1, keepdims=True) + RMS_EPS) * sp["ret_norm"][l].reshape(4, 64)
    return o.reshape(T, 256) * _silu(g)


def _pad_gate(g):
    return jnp.pad(g, ((0, 0), (0, GPAD - g.shape[1]))) if g.shape[1] < GPAD else g


def _rows8(first, second):
    z = jnp.zeros((6,) + first.shape, F32)
    return jnp.concatenate([first[None], second[None], z], axis=0)


def _local_forward(xcat, mod_l, mod_c, sp, gs, W, Tc):
    Tt = xcat.shape[0]
    cos, sin = _rope_tables(Tt - Tc, Tc)
    norm = _make_norm(Tc)
    zero = jnp.zeros((8, D), F32)
    X = xcat
    for l in range(DEPTH):
        ml, mc = mod_l[l].reshape(6, D), mod_c[l].reshape(6, D)
        h = norm(X, sp["norm_mix_pre"][l], _rows8(1.0 + mc[1], 1.0 + ml[1]), _rows8(mc[0], ml[0]))
        P = linear(h, W["w_in"][l], gs["w_in"][l])
        mixed = jnp.concatenate([_gla_mixer(P, sp, l, Tc), _ssd_mixer(P, sp, l, Tc),
                                 _ret_mixer(P, sp, l, Tc, cos, sin)], axis=-1)
        M = linear(mixed, W["w_out"][l], gs["w_out"][l])
        X = X + norm(M, sp["norm_mix_post"][l], _rows8(mc[2], ml[2]), zero)
        h = norm(X, sp["norm_ffn_pre"][l], _rows8(1.0 + mc[4], 1.0 + ml[4]), _rows8(mc[3], ml[3]))
        U = linear(h, W["ffn_w13"][l], gs["ffn_w13"][l])
        act = _silu(U[:, :FFN_H]) * U[:, FFN_H:]
        Fo = linear(act, W["ffn_w2"][l], gs["ffn_w2"][l])
        X = X + norm(Fo, sp["norm_ffn_post"][l], _rows8(mc[5], ml[5]), zero)
    return X


def _loss_call(X, target, Tc):
    Tt, W = X.shape
    tr = Tc
    nt = Tt // tr

    def body(x_ref, t_ref, loss_ref, dx_ref, acc_ref):
        i = pl.program_id(0)

        @pl.when(i == 0)
        def _():
            acc_ref[...] = jnp.zeros_like(acc_ref)
            dx_ref[...] = jnp.zeros_like(dx_ref)

        @pl.when(i > 0)
        def _():
            e = x_ref[...] - t_ref[...]
            dx_ref[...] = e * (1.0 / W)
            acc_ref[...] += jnp.sum(e * e, axis=0, keepdims=True)

        @pl.when(i == nt - 1)
        def _():
            loss_ref[...] = jnp.full(loss_ref.shape, (0.5 / W) * jnp.sum(acc_ref[...]), F32)

    loss, dx = pl.pallas_call(
        body, name="loss",
        out_shape=(jax.ShapeDtypeStruct((8, 128), F32), jax.ShapeDtypeStruct((Tt, W), F32)),
        grid=(nt,),
        in_specs=[pl.BlockSpec((tr, W), lambda i: (i, 0)),
                  pl.BlockSpec((tr, W), lambda i: (jnp.maximum(i - 1, 0), 0))],
        out_specs=(pl.BlockSpec((8, 128), lambda i: (0, 0)), pl.BlockSpec((tr, W), lambda i: (i, 0))),
        scratch_shapes=[pltpu.VMEM((1, W), F32)],
        compiler_params=_params(("arbitrary",)),
    )(X, target)
    return loss[0, 0], dx


def _adamw_call(w, g, m, v, name):
    R, Cc = w.shape
    tr = _pick(R, (512, 352, 256, 128, 64, 32, 16, 8))
    c1 = 1.0 - ADAM_B1 ** ADAM_STEP
    c2 = 1.0 - ADAM_B2 ** ADAM_STEP

    def body(w_ref, g_ref, m_ref, v_ref, d_ref, nm_ref, nv_ref):
        gv = g_ref[...]
        nm = ADAM_B1 * m_ref[...] + (1.0 - ADAM_B1) * gv
        nv = ADAM_B2 * v_ref[...] + (1.0 - ADAM_B2) * (gv * gv)
        d_ref[...] = -ADAM_LR * ((nm / c1) / (jnp.sqrt(nv / c2) + ADAM_EPS) + ADAM_WD * w_ref[...])
        nm_ref[...] = nm
        nv_ref[...] = nv

    spec = pl.BlockSpec((tr, Cc), lambda i: (i, 0))
    sh = jax.ShapeDtypeStruct((R, Cc), F32)
    return pl.pallas_call(
        body, name=name, out_shape=(sh, sh, sh), grid=(R // tr,),
        in_specs=[spec] * 4, out_specs=(spec,) * 3, compiler_params=_params(("parallel",)),
    )(w, g, m, v)


def _sum_call(xs, name):
    R, Cc = xs[0].shape
    tr = _pick(R, (512, 352, 256, 128, 64, 32, 16, 8))
    k = len(xs)

    def body(*refs):
        acc = refs[0][...]
        for r in refs[1:k]:
            acc = acc + r[...]
        refs[k][...] = acc

    spec = pl.BlockSpec((tr, Cc), lambda i: (i, 0))
    return pl.pallas_call(
        body, name=name, out_shape=jax.ShapeDtypeStruct((R, Cc), F32), grid=(R // tr,),
        in_specs=[spec] * k, out_specs=spec, compiler_params=_params(("parallel",)),
    )(*xs)


MESH = pl.DeviceIdType.MESH
ANY = pl.BlockSpec(memory_space=pl.ANY)


def _me():
    return lax.axis_index("x"), lax.axis_index("y"), lax.axis_index("c")


def _two_level_gather_body(n_arr, x_refs, out_refs, send_sems, recv_sems, local_sems):
    x, y, c = _me()
    me, sibling = (x, y, c), (x, y, 1 - c)
    chips = [(1 - x, y), (x, 1 - y), (1 - x, 1 - y)]

    def slab(a, px, py, pc):
        return out_refs[a].at[4 * px + 2 * py + pc]

    def copy(a, k, block, to, src=None):
        return pltpu.make_async_remote_copy(
            src_ref=slab(a, *block) if src is None else src, dst_ref=slab(a, *block),
            send_sem=send_sems.at[a, k], recv_sem=recv_sems.at[a, k], device_id=to, device_id_type=MESH)

    mine = [pltpu.make_async_copy(x_refs[a], slab(a, *me), local_sems.at[a]) for a in range(n_arr)]
    for cp in mine:
        cp.start()
    first = []
    for a in range(n_arr):
        first.append(copy(a, 0, me, sibling, src=x_refs[a]))
        first += [copy(a, 1 + j, me, (*chip, c), src=x_refs[a]) for j, chip in enumerate(chips)]
    for cp in first:
        cp.start()
    passed = []
    for j, chip in enumerate(chips):
        for a in range(n_arr):
            copy(a, 1 + j, (*chip, c), me).wait_recv()
            fw = copy(a, 4 + j, (*chip, c), sibling)
            fw.start()
            passed.append(fw)
    for a in range(n_arr):
        copy(a, 0, sibling, me).wait_recv()
        for j, chip in enumerate(chips):
            copy(a, 4 + j, (*chip, 1 - c), me).wait_recv()
    for cp in first + passed:
        cp.wait_send()
    for cp in mine:
        cp.wait()


def _gather_big(xs, name):
    n_arr = len(xs)

    def body(*refs):
        _two_level_gather_body(n_arr, refs[:n_arr], refs[n_arr:2 * n_arr], *refs[2 * n_arr:])

    return pl.pallas_call(
        body, name=name,
        out_shape=tuple(jax.ShapeDtypeStruct((N_DEV,) + a.shape, a.dtype) for a in xs),
        in_specs=[ANY] * n_arr, out_specs=(ANY,) * n_arr,
        scratch_shapes=[pltpu.SemaphoreType.DMA((n_arr, 7)), pltpu.SemaphoreType.DMA((n_arr, 7)),
                        pltpu.SemaphoreType.DMA((n_arr,))],
    )(*xs)


def _gather_small(x, name):
    def body(x_ref, out_ref, send_sems, recv_sems, local_sems):
        _two_level_gather_body(1, [x_ref], [out_ref], send_sems, recv_sems, local_sems)

    vm = pl.BlockSpec(memory_space=pltpu.VMEM)
    return pl.pallas_call(
        body, name=name,
        out_shape=jax.ShapeDtypeStruct((N_DEV,) + x.shape, x.dtype),
        in_specs=[vm], out_specs=vm,
        scratch_shapes=[pltpu.SemaphoreType.DMA((1, 7)), pltpu.SemaphoreType.DMA((1, 7)),
                        pltpu.SemaphoreType.DMA((1,))],
    )(x)


def _exchange_sibling(gs_, name):
    n_arr = len(gs_)

    def body(*refs):
        g_refs, out_refs = refs[:n_arr], refs[n_arr:2 * n_arr]
        send_sems, recv_sems = refs[2 * n_arr:]
        x, y, c = _me()
        cps = []
        for a in range(n_arr):
            for px in range(2):
                for py in range(2):
                    i = 2 * px + py
                    cps.append(pltpu.make_async_remote_copy(
                        src_ref=g_refs[a].at[4 * px + 2 * py + (1 - c)], dst_ref=out_refs[a].at[i],
                        send_sem=send_sems.at[a, i], recv_sem=recv_sems.at[a, i],
                        device_id=(x, y, 1 - c), device_id_type=MESH))
        for cp in cps:
            cp.start()
        for cp in cps:
            cp.wait()

    return pl.pallas_call(
        body, name=name,
        out_shape=tuple(jax.ShapeDtypeStruct((4,) + a.shape[1:], a.dtype) for a in gs_),
        in_specs=[ANY] * n_arr, out_specs=(ANY,) * n_arr,
        scratch_shapes=[pltpu.SemaphoreType.DMA((n_arr, 4)), pltpu.SemaphoreType.DMA((n_arr, 4))],
    )(*gs_)


def _exchange_chips(ps, name):
    n_arr = len(ps)

    def body(*refs):
        p_refs, out_refs = refs[:n_arr], refs[n_arr:2 * n_arr]
        send_sems, recv_sems = refs[2 * n_arr:]
        x, y, c = _me()
        chips = [(1 - x, y), (x, 1 - y), (1 - x, 1 - y)]
        cps = []
        for a in range(n_arr):
            for j, (cx, cy) in enumerate(chips):
                cps.append(pltpu.make_async_remote_copy(
                    src_ref=p_refs[a].at[2 * cx + cy], dst_ref=out_refs[a].at[j],
                    send_sem=send_sems.at[a, j], recv_sem=recv_sems.at[a, j],
                    device_id=(cx, cy, c), device_id_type=MESH))
        for cp in cps:
            cp.start()
        for cp in cps:
            cp.wait()

    return pl.pallas_call(
        body, name=name,
        out_shape=tuple(jax.ShapeDtypeStruct((3,) + a.shape[1:], a.dtype) for a in ps),
        in_specs=[ANY] * n_arr, out_specs=(ANY,) * n_arr,
        scratch_shapes=[pltpu.SemaphoreType.DMA((n_arr, 3)), pltpu.SemaphoreType.DMA((n_arr, 3))],
    )(*ps)


def _reduce_scatter(gs_):
    x, y, c = _me()
    from_sib = _exchange_sibling(gs_, "rs_sibling")
    ps = []
    for a, g in enumerate(gs_):
        R, Cc = g.shape[1:]
        mine = lax.dynamic_index_in_dim(g.reshape(4, 2, R, Cc), c, axis=1, keepdims=False)
        ps.append(_sum_call([mine.reshape(4 * R, Cc), from_sib[a].reshape(4 * R, Cc)],
                            f"rs_add_sib{a}").reshape(4, R, Cc))
    from_chips = _exchange_chips(ps, "rs_chips")
    outs = []
    for a, p in enumerate(ps):
        mine = lax.dynamic_index_in_dim(p, 2 * x + y, axis=0, keepdims=False)
        outs.append(_sum_call([mine, from_chips[a][0], from_chips[a][1], from_chips[a][2]], f"rs_add_chips{a}"))
    return outs


_SMALL = ["norm_mix_pre", "norm_mix_post", "norm_ffn_pre", "norm_ffn_post", "gla_gate_up", "gla_gate_b",
          "gla_norm", "ssd_conv_w", "ssd_conv_b", "ssd_dt_bias", "ssd_a_log", "ssd_d", "ssd_norm", "ret_norm"]


def _pack(arrs):
    flat = jnp.concatenate([a.reshape(-1) for a in arrs])
    n = flat.shape[0]
    npad = -(-n // 1024) * 1024
    return jnp.pad(flat, (0, npad - n)).reshape(npad // 128, 128)


def _unpack(buf, shapes):
    flat = buf.reshape(-1)
    out, o = [], 0
    for s in shapes:
        n = math.prod(s)
        out.append(flat[o:o + n].reshape(s))
        o += n
    return out


def kernel(x, c, ctx, c_ctx, ada_w, ada_b, norm_mix_pre, norm_mix_post, norm_ffn_pre, norm_ffn_post, w_in, w_out, gla_gate_up, gla_gate_b, gla_norm, ssd_conv_w, ssd_conv_b, ssd_dt_bias, ssd_a_log, ssd_d, ssd_norm, ret_norm, ffn_w13, ffn_w2, loss_target, m_c_ctx, m_ada_w, m_ada_b, m_norm_mix_pre, m_norm_mix_post, m_norm_ffn_pre, m_norm_ffn_post, m_w_in, m_w_out, m_gla_gate_up, m_gla_gate_b, m_gla_norm, m_ssd_conv_w, m_ssd_conv_b, m_ssd_dt_bias, m_ssd_a_log, m_ssd_d, m_ssd_norm, m_ret_norm, m_ffn_w13, m_ffn_w2, v_c_ctx, v_ada_w, v_ada_b, v_norm_mix_pre, v_norm_mix_post, v_norm_ffn_pre, v_norm_ffn_post, v_w_in, v_w_out, v_gla_gate_up, v_gla_gate_b, v_gla_norm, v_ssd_conv_w, v_ssd_conv_b, v_ssd_dt_bias, v_ssd_a_log, v_ssd_d, v_ssd_norm, v_ret_norm, v_ffn_w13, v_ffn_w2):
    P_ = dict(c_ctx=c_ctx, ada_w=ada_w, ada_b=ada_b, norm_mix_pre=norm_mix_pre, norm_mix_post=norm_mix_post,
              norm_ffn_pre=norm_ffn_pre, norm_ffn_post=norm_ffn_post, w_in=w_in, w_out=w_out,
              gla_gate_up=gla_gate_up, gla_gate_b=gla_gate_b, gla_norm=gla_norm, ssd_conv_w=ssd_conv_w,
              ssd_conv_b=ssd_conv_b, ssd_dt_bias=ssd_dt_bias, ssd_a_log=ssd_a_log, ssd_d=ssd_d,
              ssd_norm=ssd_norm, ret_norm=ret_norm, ffn_w13=ffn_w13, ffn_w2=ffn_w2)
    M_ = dict(c_ctx=m_c_ctx, ada_w=m_ada_w, ada_b=m_ada_b, norm_mix_pre=m_norm_mix_pre,
              norm_mix_post=m_norm_mix_post, norm_ffn_pre=m_norm_ffn_pre, norm_ffn_post=m_norm_ffn_post,
              w_in=m_w_in, w_out=m_w_out, gla_gate_up=m_gla_gate_up, gla_gate_b=m_gla_gate_b,
              gla_norm=m_gla_norm, ssd_conv_w=m_ssd_conv_w, ssd_conv_b=m_ssd_conv_b, ssd_dt_bias=m_ssd_dt_bias,
              ssd_a_log=m_ssd_a_log, ssd_d=m_ssd_d, ssd_norm=m_ssd_norm, ret_norm=m_ret_norm,
              ffn_w13=m_ffn_w13, ffn_w2=m_ffn_w2)
    V_ = dict(c_ctx=v_c_ctx, ada_w=v_ada_w, ada_b=v_ada_b, norm_mix_pre=v_norm_mix_pre,
              norm_mix_post=v_norm_mix_post, norm_ffn_pre=v_norm_ffn_pre, norm_ffn_post=v_norm_ffn_post,
              w_in=v_w_in, w_out=v_w_out, gla_gate_up=v_gla_gate_up, gla_gate_b=v_gla_gate_b,
              gla_norm=v_gla_norm, ssd_conv_w=v_ssd_conv_w, ssd_conv_b=v_ssd_conv_b, ssd_dt_bias=v_ssd_dt_bias,
              ssd_a_log=v_ssd_a_log, ssd_d=v_ssd_d, ssd_norm=v_ssd_norm, ret_norm=v_ret_norm,
              ffn_w13=v_ffn_w13, ffn_w2=v_ffn_w2)
    order = ["c_ctx", "ada_w", "ada_b", "norm_mix_pre", "norm_mix_post", "norm_ffn_pre", "norm_ffn_post", "w_in",
             "w_out", "gla_gate_up", "gla_gate_b", "gla_norm", "ssd_conv_w", "ssd_conv_b", "ssd_dt_bias",
             "ssd_a_log", "ssd_d", "ssd_norm", "ret_norm", "ffn_w13", "ffn_w2"]

    mx, my, mc_ = _me()
    me = 4 * mx + 2 * my + mc_
    Tl, Tc = x.shape[1], ctx.shape[1]
    n_in, n_out, n_13, n_2 = w_in.shape[2], w_out.shape[1], ffn_w13.shape[2], ffn_w2.shape[1]
    n_ada = ada_w.shape[2]

    shards = [w_in.astype(BF16).reshape(DEPTH * D, n_in), w_out.astype(BF16).reshape(DEPTH * n_out, D),
              ffn_w13.astype(BF16).reshape(DEPTH * D, n_13), ffn_w2.astype(BF16).reshape(DEPTH * n_2, D)]
    g_in, g_out, g_13, g_2 = _gather_big(shards, "gather_weights")
    W = dict(
        w_in=_permute_cols(jnp.moveaxis(g_in.reshape(N_DEV, DEPTH, D, n_in), 0, 2).reshape(DEPTH, D, N_DEV * n_in)),
        w_out=jnp.moveaxis(g_out.reshape(N_DEV, DEPTH, n_out, D), 0, 1).reshape(DEPTH, N_DEV * n_out, D),
        ffn_w13=jnp.moveaxis(g_13.reshape(N_DEV, DEPTH, D, n_13), 0, 2).reshape(DEPTH, D, N_DEV * n_13),
        ffn_w2=jnp.moveaxis(g_2.reshape(N_DEV, DEPTH, n_2, D), 0, 1).reshape(DEPTH, N_DEV * n_2, D),
    )

    cw = ssd_conv_w.shape[2]
    small_in = jnp.concatenate([jnp.pad(c, ((0, 7), (0, 0))).reshape(-1),
                                ssd_conv_w.reshape(-1)]).reshape(-1, 128)
    n_c_rows = 8 * D // 128
    small_in = jnp.pad(small_in, ((0, -small_in.shape[0] % 8), (0, 0)))
    gathered = _gather_small(small_in, "gather_c_conv")
    c_all = gathered[:, :n_c_rows].reshape(N_DEV, 8, D)[:, 0]
    conv_rows = DEPTH * 5 * cw // 128
    conv_full = gathered[:, n_c_rows:n_c_rows + conv_rows].reshape(N_DEV, DEPTH, 5, cw)
    conv_full = jnp.moveaxis(conv_full, 0, 2).reshape(DEPTH, 5, N_DEV * cw)
    c9 = jnp.concatenate([c_all, c_ctx[None], jnp.zeros((7, D), F32)], axis=0)
    s9 = _silu(c9)
    mod_piece = jnp.concatenate([_mm(s9, ada_w[l], name="mm_mod") for l in range(DEPTH)], axis=0)
    mod_g = _gather_small(mod_piece, "gather_mod")
    mod_all = jnp.moveaxis(mod_g.reshape(N_DEV, DEPTH, 16, n_ada), 0, 2).reshape(DEPTH, 16, N_DEV * n_ada)
    mod_all = mod_all + ada_b[:, None, :]
    mod_l = lax.dynamic_index_in_dim(mod_all, me, axis=1, keepdims=False)
    mod_c = mod_all[:, 8]

    sp = {n: P_[n] for n in _SMALL}
    sp["ssd_conv_w"] = conv_full
    gs = dict(w_in=jnp.zeros((DEPTH, D, IN_PAD), F32), w_out=jnp.zeros((DEPTH, D, D), F32),
              ffn_w13=jnp.zeros((DEPTH, D, 2 * FFN_H), F32), ffn_w2=jnp.zeros((DEPTH, FFN_H, D), F32))
    xcat = jnp.concatenate([ctx[0], x[0]], axis=0)
    Xf, vjp = jax.vjp(lambda xc, ml, mc, sp_, gs_: _local_forward(xc, ml, mc, sp_, gs_, W, Tc),
                      xcat, mod_l, mod_c, sp, gs)
    loss_local, dX = _loss_call(Xf, loss_target[0], Tc)
    d_xcat, d_mod_l, d_mod_c, d_sp, d_gs = vjp(dX)
    loss = lax.psum(loss_local, ("x", "y", "c"))
    grad_x = d_xcat[Tc:][None]

    def dev_major_cols(g, n):
        K = g.shape[1]
        return jnp.moveaxis(g.reshape(DEPTH, K, N_DEV, n), 2, 0).reshape(N_DEV, DEPTH * K, n)

    def dev_major_rows(g, n):
        return jnp.moveaxis(g.reshape(DEPTH, N_DEV, n, D), 1, 0).reshape(N_DEV, DEPTH * n, D)

    big = [dev_major_cols(_unpermute_cols(d_gs["w_in"]), n_in), dev_major_rows(d_gs["w_out"], n_out),
           dev_major_cols(d_gs["ffn_w13"], n_13), dev_major_rows(d_gs["ffn_w2"], n_2)]
    r_in, r_out, r_13, r_2 = _reduce_scatter(big)
    G = dict(w_in=r_in.reshape(DEPTH, D, n_in), w_out=r_out.reshape(DEPTH, n_out, D),
             ffn_w13=r_13.reshape(DEPTH, D, n_13), ffn_w2=r_2.reshape(DEPTH, n_2, D))

    dmod_rows = jnp.concatenate([d_mod_l, d_mod_c], axis=0)
    dmod_g = _gather_small(dmod_rows, "gather_dmod").reshape(N_DEV, 2, DEPTH, 6 * D)
    dl = jnp.moveaxis(dmod_g[:, 0], 0, 1)
    dc = dmod_g[:, 1, :, :]
    dc_tot = dc[0]
    for d_ in range(1, N_DEV):
        dc_tot = dc_tot + dc[d_]
    dmod9 = jnp.concatenate([dl, dc_tot[:, None, :], jnp.zeros((DEPTH, 7, 6 * D), F32)], axis=1)
    g_ada_b = dmod9[:, 0]
    for r_ in range(1, 9):
        g_ada_b = g_ada_b + dmod9[:, r_]
    dmod9_mine = lax.dynamic_slice_in_dim(dmod9, me * n_ada, n_ada, axis=2)
    s9T = jnp.pad(s9.T, ((0, 0), (0, 112)))
    g_ada_w = jnp.stack([_mm(s9T, jnp.pad(dmod9_mine[l], ((0, 112), (0, 0))), name="mm_dada")
                         for l in range(DEPTH)])
    ds9 = _mm(dmod9_mine[0], ada_w[0], trans_b=True, name="mm_ds9")
    for l in range(1, DEPTH):
        ds9 = ds9 + _mm(dmod9_mine[l], ada_w[l], trans_b=True, name="mm_ds9")
    ds_ctx_part = ds9[8]

    small_names = [n for n in _SMALL]
    small_parts = [d_sp[n] for n in small_names] + [ds_ctx_part]
    packed = _pack(small_parts)
    allp = _gather_small(packed, "gather_small_grads")
    summed = _sum_call([allp[d_] for d_ in range(N_DEV)], "sum_small_grads")
    parts = _unpack(summed, [p.shape for p in small_parts])
    for n, p in zip(small_names, parts[:-1]):
        G[n] = p
    sig = jax.nn.sigmoid(c_ctx)
    G["c_ctx"] = parts[-1] * (sig * (1.0 + c_ctx * (1.0 - sig)))
    G["ssd_conv_w"] = lax.dynamic_slice_in_dim(G["ssd_conv_w"], me * cw, cw, axis=2)
    G["ada_w"] = g_ada_w
    G["ada_b"] = g_ada_b

    delta, new_m, new_v = {}, {}, {}
    for n in ["ada_w", "w_in", "w_out", "ffn_w13", "ffn_w2"]:
        sh = P_[n].shape
        f2 = lambda a: a.reshape(sh[0] * sh[1], sh[2])
        d_, m_, v_ = _adamw_call(f2(P_[n]), f2(G[n]), f2(M_[n]), f2(V_[n]), f"adamw_{n}")
        delta[n], new_m[n], new_v[n] = d_.reshape(sh), m_.reshape(sh), v_.reshape(sh)
    rest = [n for n in order if n not in delta]
    shapes = [P_[n].shape for n in rest]
    d_, m_, v_ = _adamw_call(_pack([P_[n] for n in rest]), _pack([G[n] for n in rest]),
                             _pack([M_[n] for n in rest]), _pack([V_[n] for n in rest]), "adamw_small")
    for n, a, b, e in zip(rest, _unpack(d_, shapes), _unpack(m_, shapes), _unpack(v_, shapes)):
        delta[n], new_m[n], new_v[n] = a, b, e

    return (loss, grad_x, *[G[n] for n in order], *[delta[n] for n in order],
            *[new_m[n] for n in order], *[new_v[n] for n in order])
```

```python
import functools
import math

import jax
import jax.numpy as jnp
from jax import lax
from jax.experimental import pallas as pl
from jax.experimental.pallas import tpu as pltpu

F32 = jnp.float32
BF16 = jnp.bfloat16

D = 1024
DEPTH = 4
GRID_W = 64
RMS_EPS = 1e-6
GLA_TAU = 16.0
FFN_H = 2816
IN_COLS = 3376
IN_PAD = 3456
N_DEV = 8
ADAM_LR, ADAM_B1, ADAM_B2, ADAM_EPS, ADAM_WD, ADAM_STEP = 0.001, 0.9, 0.999, 1e-08, 0.01, 10

VMEM_LIMIT = 48 * 1024 * 1024

_SEGS = [
    ("gla_q", 0, 128), ("gla_k", 128, 128), ("gla_v", 256, 256), ("gla_r", 512, 256),
    ("ssd_z", 800, 512), ("ssd_xbc", 1312, 1024),
    ("ret_q", 2352, 256), ("ret_k", 2608, 256), ("ret_v", 2864, 256), ("ret_g", 3120, 256),
    ("gla_lr", 768, 32), ("ssd_dt", 2336, 16),
]
_OFF = {}
_o = 0
for _n, _s, _z in _SEGS:
    _OFF[_n] = (_o, _z)
    _o += _z
assert _o == IN_COLS


def _permute_cols(w):
    parts = [w[..., s:s + z] for _, s, z in _SEGS]
    parts.append(jnp.zeros(w.shape[:-1] + (IN_PAD - IN_COLS,), w.dtype))
    return jnp.concatenate(parts, axis=-1)


def _unpermute_cols(w):
    order = sorted(_SEGS, key=lambda t: t[1])
    return jnp.concatenate([w[..., _OFF[n][0]:_OFF[n][0] + z] for n, _, z in order], axis=-1)


def _pick(n, cands):
    for c in cands:
        if n % c == 0:
            return c
    return n


def _params(sem=None):
    kw = dict(vmem_limit_bytes=VMEM_LIMIT)
    if sem is not None:
        kw["dimension_semantics"] = sem
    return pltpu.CompilerParams(**kw)


def _mm(a, b, *, trans_b=False, name):
    M, K = a.shape
    N = b.shape[0] if trans_b else b.shape[1]
    assert (b.shape[1] if trans_b else b.shape[0]) == K
    tm = _pick(M, (1408, 1088, 1024, 512, 256, 128, 64, 32, 16))
    tn = _pick(N, (512, 384, 256, 128))
    tk = _pick(K, (2176, 1408, 1024, 512, 384, 256, 128))
    nk = K // tk
    dims = (((1,), (1,)), ((), ())) if trans_b else (((1,), (0,)), ((), ()))

    def body(a_ref, b_ref, o_ref, acc_ref):
        k = pl.program_id(2)

        @pl.when(k == 0)
        def _():
            acc_ref[...] = jnp.zeros_like(acc_ref)

        acc_ref[...] += lax.dot_general(a_ref[...].astype(BF16), b_ref[...].astype(BF16), dims,
                                        preferred_element_type=F32)

        @pl.when(k == nk - 1)
        def _():
            o_ref[...] = acc_ref[...]

    b_spec = (pl.BlockSpec((tn, tk), lambda i, j, k: (j, k)) if trans_b
              else pl.BlockSpec((tk, tn), lambda i, j, k: (k, j)))
    return pl.pallas_call(
        body, name=name,
        out_shape=jax.ShapeDtypeStruct((M, N), F32),
        grid=(M // tm, N // tn, nk),
        in_specs=[pl.BlockSpec((tm, tk), lambda i, j, k: (i, k)), b_spec],
        out_specs=pl.BlockSpec((tm, tn), lambda i, j, k: (i, j)),
        scratch_shapes=[pltpu.VMEM((tm, tn), F32)],
        compiler_params=_params(("parallel", "parallel", "arbitrary")),
    )(a, b)


def _mm_tn(a, g, *, name):
    M, K = a.shape
    N = g.shape[1]
    tm = _pick(M, (1088, 512, 256, 128, 64, 32, 16))
    tk = _pick(K, (1408, 1024, 512, 256, 128))
    tn = _pick(N, (512, 384, 256, 128))
    nm = M // tm

    def body(a_ref, g_ref, o_ref, acc_ref):
        i = pl.program_id(2)

        @pl.when(i == 0)
        def _():
            acc_ref[...] = jnp.zeros_like(acc_ref)

        acc_ref[...] += lax.dot_general(a_ref[...].astype(BF16), g_ref[...].astype(BF16),
                                        (((0,), (0,)), ((), ())), preferred_element_type=F32)

        @pl.when(i == nm - 1)
        def _():
            o_ref[...] = acc_ref[...]

    return pl.pallas_call(
        body, name=name,
        out_shape=jax.ShapeDtypeStruct((K, N), F32),
        grid=(K // tk, N // tn, nm),
        in_specs=[pl.BlockSpec((tm, tk), lambda k, j, i: (i, k)), pl.BlockSpec((tm, tn), lambda k, j, i: (i, j))],
        out_specs=pl.BlockSpec((tk, tn), lambda k, j, i: (k, j)),
        scratch_shapes=[pltpu.VMEM((tk, tn), F32)],
        compiler_params=_params(("parallel", "parallel", "arbitrary")),
    )(a, g)


@jax.custom_vjp
def linear(a, w, gslot):
    return _mm(a, w, name="mm_fwd")


def _linear_fwd(a, w, gslot):
    return _mm(a, w, name="mm_fwd"), (a, w)


def _linear_bwd(res, g):
    a, w = res
    da = _mm(g, w, trans_b=True, name="mm_dx")
    dw = _mm_tn(a, g, name="mm_dw")
    return da, jnp.zeros_like(w), dw


linear.defvjp(_linear_fwd, _linear_bwd)


def _norm_fwd_call(x, w, a2, b2, tr):
    T, W = x.shape

    def body(x_ref, w_ref, a_ref, b_ref, y_ref):
        seg = jnp.minimum(pl.program_id(0), 1)
        xv = x_ref[...]
        rstd = lax.rsqrt(jnp.mean(xv * xv, axis=-1, keepdims=True) + RMS_EPS)
        y_ref[...] = a_ref[pl.ds(seg, 1), :] * (xv * rstd * w_ref[...]) + b_ref[pl.ds(seg, 1), :]

    return pl.pallas_call(
        body, name="norm_fwd",
        out_shape=jax.ShapeDtypeStruct((T, W), F32),
        grid=(T // tr,),
        in_specs=[pl.BlockSpec((tr, W), lambda i: (i, 0)), pl.BlockSpec((1, W), lambda i: (0, 0)),
                  pl.BlockSpec((8, W), lambda i: (0, 0)), pl.BlockSpec((8, W), lambda i: (0, 0))],
        out_specs=pl.BlockSpec((tr, W), lambda i: (i, 0)),
        compiler_params=_params(("parallel",)),
    )(x, w.reshape(1, W), a2, b2)


def _norm_bwd_call(x, w, a2, dy, tr):
    T, W = x.shape

    def body(x_ref, w_ref, a_ref, dy_ref, dx_ref, dw_ref, da_ref, db_ref):
        i = pl.program_id(0)
        seg = jnp.minimum(i, 1)

        @pl.when(i == 0)
        def _():
            dw_ref[...] = jnp.zeros_like(dw_ref)
            da_ref[...] = jnp.zeros_like(da_ref)
            db_ref[...] = jnp.zeros_like(db_ref)

        xv = x_ref[...]
        g = dy_ref[...]
        wv = w_ref[...]
        rstd = lax.rsqrt(jnp.mean(xv * xv, axis=-1, keepdims=True) + RMS_EPS)
        xh = xv * rstd
        da_ref[pl.ds(seg, 1), :] += jnp.sum(g * (xh * wv), axis=0, keepdims=True)
        db_ref[pl.ds(seg, 1), :] += jnp.sum(g, axis=0, keepdims=True)
        gy = g * a_ref[pl.ds(seg, 1), :]
        dw_ref[0:1, :] += jnp.sum(gy * xh, axis=0, keepdims=True)
        gx = gy * wv
        dx_ref[...] = rstd * (gx - xh * jnp.mean(gx * xh, axis=-1, keepdims=True))

    acc = jax.ShapeDtypeStruct((8, W), F32)
    acc_spec = pl.BlockSpec((8, W), lambda i: (0, 0))
    return pl.pallas_call(
        body, name="norm_bwd",
        out_shape=(jax.ShapeDtypeStruct((T, W), F32), acc, acc, acc),
        grid=(T // tr,),
        in_specs=[pl.BlockSpec((tr, W), lambda i: (i, 0)), pl.BlockSpec((1, W), lambda i: (0, 0)),
                  acc_spec, pl.BlockSpec((tr, W), lambda i: (i, 0))],
        out_specs=(pl.BlockSpec((tr, W), lambda i: (i, 0)), acc_spec, acc_spec, acc_spec),
        compiler_params=_params(("arbitrary",)),
    )(x, w.reshape(1, W), a2, dy)


@functools.lru_cache(maxsize=None)
def _make_norm(tr):
    @jax.custom_vjp
    def norm_affine(x, w, a2, b2):
        return _norm_fwd_call(x, w, a2, b2, tr)

    def fwd(x, w, a2, b2):
        return _norm_fwd_call(x, w, a2, b2, tr), (x, w, a2)

    def bwd(res, dy):
        x, w, a2 = res
        dx, dw, da, db = _norm_bwd_call(x, w, a2, dy, tr)
        return dx, dw[0], da, db

    norm_affine.defvjp(fwd, bwd)
    return norm_affine


_SCAN_CFG = {
    "gla": dict(H=4, Dk=32, Dv=64, nh=4, scalar=False, C=64),
    "ssd": dict(H=8, Dk=128, Dv=64, nh=2, scalar=True, C=128),
    "ret": dict(H=4, Dk=64, Dv=64, nh=4, scalar=True, C=128),
}
GPAD = 8


def _log2(n):
    r = int(math.log2(n))
    assert 1 << r == n
    return r


def _iota(shape, dim):
    return lax.broadcasted_iota(jnp.int32, shape, dim)


def _dot(a, b, dims, precision=None):
    return lax.dot_general(a, b, (dims, ((), ())), preferred_element_type=F32, precision=precision)


_NN = ((1,), (0,))
_NT = ((1,), (1,))
_TN = ((0,), (0,))


def _bf(x):
    return x.astype(BF16)


def _dot_sel(x, e, dims, x_left=True):
    eb = e.astype(BF16)
    hi = x.astype(BF16)
    r1 = x - hi.astype(F32)
    mid = r1.astype(BF16)
    lo = (r1 - mid.astype(F32)).astype(BF16)
    out = None
    for p in (hi, mid, lo):
        t = _dot(p, eb, dims) if x_left else _dot(eb, p, dims)
        out = t if out is None else out + t
    return out


class _ScanMath:
    def __init__(self, cfg, C, reverse):
        self.C, self.reverse = C, reverse
        self.Dk, self.Dv, self.nh, self.scalar = cfg["Dk"], cfg["Dv"], cfg["nh"], cfg["scalar"]
        self.Wk, self.Wv = self.nh * self.Dk, self.nh * self.Dv
        self.nsg = cfg["H"] // self.nh
        C_, nh, Wk, Wv = C, self.nh, self.Wk, self.Wv
        lk, lv, lc = _log2(self.Dk), _log2(self.Dv), _log2(C_)
        r, c = _iota((C_, C_), 0), _iota((C_, C_), 1)
        self.L = ((c >= r) if reverse else (c <= r)).astype(F32)
        self.Lsuf = ((c <= r) if reverse else (c >= r)).astype(F32)
        i, j = _iota((C_, nh * C_), 0), _iota((C_, nh * C_), 1) & (C_ - 1)
        self.Mst = (j >= i) if reverse else (j <= i)
        self.Dj = (i == j).astype(F32)
        self.km = [((_iota((1, Wk), 1) >> lk) == h).astype(F32) for h in range(nh)]
        self.vm = [((_iota((1, Wv), 1) >> lv) == h).astype(F32) for h in range(nh)]
        self.BD = ((_iota((Wv, Wk), 0) >> lv) == (_iota((Wv, Wk), 1) >> lk)).astype(F32)
        self.last = 0 if reverse else C_ - 1
        self.last_row = (_iota((C_, 1), 0) == self.last).astype(F32)
        self.lk, self.lc = lk, lc
        self.H = cfg["H"]

    def gates(self, g):
        if not self.scalar:
            return _dot_sel(g, self.L, _NN, x_left=False), None
        G8 = _dot_sel(g, self.L, _NN, x_left=False)
        nk, ncol = self.H * self.Dk, self.H * self.C
        ek = (_iota((GPAD, nk), 0) == (_iota((GPAD, nk), 1) >> self.lk)).astype(F32)
        ec = (_iota((GPAD, ncol), 0) == (_iota((GPAD, ncol), 1) >> self.lc)).astype(F32)
        return _dot_sel(G8, ek, _NN), _dot_sel(G8, ec, _NN)

    def Ek(self, s):
        return (_iota((GPAD, self.Wk), 0) == (_iota((GPAD, self.Wk), 1) >> self.lk) + s * self.nh).astype(F32)

    def kstack(self, x):
        return jnp.concatenate([x * self.km[h] for h in range(self.nh)], axis=0)

    def vstack(self, x):
        return jnp.concatenate([x * self.vm[h] for h in range(self.nh)], axis=0)

    def unstack(self, R, masks):
        C = self.C
        out = R[0:C] * masks[0]
        for h in range(1, self.nh):
            out = out + R[h * C:(h + 1) * C] * masks[h]
        return out

    def chunk(self, qs, ks, Gk, Gc):
        C = self.C
        Glast = Gk[self.last:self.last + 1, :]
        out = dict(Gk=Gk, Glast=Glast, eG=jnp.exp(Gk), eGl=jnp.exp(Glast - Gk), eGlast=jnp.exp(Glast))
        if self.scalar:
            Gr = jnp.sum(Gc * self.Dj, axis=0, keepdims=True)
            dec = jnp.where(self.Mst, jnp.exp(jnp.minimum(Gc - Gr, 0.0)), 0.0)
            qt, kt = qs, ks
            A = _dot(_bf(qt), _bf(self.kstack(kt)), _NT) * dec
            out.update(dec=dec, qt=qt, kt=kt, A=A)
        else:
            Gm = Gk[C // 2:C // 2 + 1, :]
            eq, ek = jnp.exp(Gk - Gm), jnp.exp(Gm - Gk)
            qt, kt = qs * eq, ks * ek
            A = jnp.where(self.Mst, _dot(_bf(qt), _bf(self.kstack(kt)), _NT), 0.0)
            out.update(eq=eq, ek=ek, qt=qt, kt=kt, A=A)
        return out


def _chunk_index(p, n, nc, reverse):
    if not reverse:
        return p
    return jnp.where(p < nc, nc - 1 - p, n - 1 + nc - p)


def _scan_fwd_call(kind, reverse, q, k, v, g, Tc):
    cfg = _SCAN_CFG[kind]
    C = cfg["C"]
    T, HK = q.shape
    HV = v.shape[1]
    n, nc = T // C, Tc // C
    GW = g.shape[1]

    def body(q_ref, k_ref, v_ref, g_ref, o_ref, st_ref, S_ref):
        m = _ScanMath(cfg, C, reverse)

        @pl.when(pl.program_id(0) == 0)
        def _():
            S_ref[...] = jnp.zeros_like(S_ref)

        Gk_all, Gc_all = m.gates(g_ref[...])
        for s in range(m.nsg):
            ksl, vsl = slice(s * m.Wk, (s + 1) * m.Wk), slice(s * m.Wv, (s + 1) * m.Wv)
            csl = slice(s * m.nh * C, (s + 1) * m.nh * C)
            qs, ks, vs = q_ref[:, ksl], k_ref[:, ksl], v_ref[:, vsl]
            ch = m.chunk(qs, ks, Gk_all[:, ksl], Gc_all[:, csl] if m.scalar else None)
            S = S_ref[vsl, :]
            o = _dot(_bf(ch["A"]), _bf(m.vstack(vs)), _NN) + _dot(_bf(qs * ch["eG"]), _bf(S), _NT)
            o_ref[:, vsl] = o
            st_ref[0, vsl, :] = S
            S_ref[vsl, :] = S * ch["eGlast"] + _dot(_bf(vs), _bf(ks * ch["eGl"]), _TN) * m.BD

    sg = cfg["H"] // cfg["nh"]
    Wk, Wv = cfg["nh"] * cfg["Dk"], cfg["nh"] * cfg["Dv"]
    idx = lambda p: (_chunk_index(p, n, nc, reverse), 0)
    return pl.pallas_call(
        body, name=f"scan_fwd_{kind}_{'r' if reverse else 'f'}",
        out_shape=(jax.ShapeDtypeStruct((T, HV), F32), jax.ShapeDtypeStruct((n, sg * Wv, Wk), F32)),
        grid=(n,),
        in_specs=[pl.BlockSpec((C, HK), idx), pl.BlockSpec((C, HK), idx), pl.BlockSpec((C, HV), idx),
                  pl.BlockSpec((C, GW), idx)],
        out_specs=(pl.BlockSpec((C, HV), idx),
                   pl.BlockSpec((1, sg * Wv, Wk), lambda p: (_chunk_index(p, n, nc, reverse), 0, 0))),
        scratch_shapes=[pltpu.VMEM((sg * Wv, Wk), F32)],
        compiler_params=_params(("arbitrary",)),
    )(q, k, v, g)


def _scan_bwd_call(kind, reverse, need_dg, q, k, v, g, st, do, Tc):
    cfg = _SCAN_CFG[kind]
    C = cfg["C"]
    T, HK = q.shape
    HV = v.shape[1]
    n, nc = T // C, Tc // C
    GW = g.shape[1]

    def body(q_ref, k_ref, v_ref, g_ref, st_ref, do_ref, dq_ref, dk_ref, dv_ref, dg_ref, dS_ref):
        m = _ScanMath(cfg, C, reverse)

        @pl.when(pl.program_id(0) == 0)
        def _():
            dS_ref[...] = jnp.zeros_like(dS_ref)

        x8 = jnp.zeros((C, GPAD), F32)
        Gk_all, Gc_all = m.gates(g_ref[...])
        for s in range(m.nsg):
            ksl, vsl = slice(s * m.Wk, (s + 1) * m.Wk), slice(s * m.Wv, (s + 1) * m.Wv)
            csl = slice(s * m.nh * C, (s + 1) * m.nh * C)
            qs, ks, vs, dos = q_ref[:, ksl], k_ref[:, ksl], v_ref[:, vsl], do_ref[:, vsl]
            ch = m.chunk(qs, ks, Gk_all[:, ksl], Gc_all[:, csl] if m.scalar else None)
            S = st_ref[0, vsl, :]
            dS = dS_ref[vsl, :]
            A, qt, kt = ch["A"], ch["qt"], ch["kt"]
            dA = _dot(_bf(dos), _bf(m.vstack(vs)), _NT)
            dAm = dA * ch["dec"] if m.scalar else jnp.where(m.Mst, dA, 0.0)
            kst = _bf(m.kstack(kt))
            dv = m.unstack(_dot(_bf(A), _bf(dos), _TN), m.vm) + _dot(_bf(ks * ch["eGl"]), _bf(dS), _NT)
            dv_ref[:, vsl] = dv
            dq_i = _dot(_bf(dAm), kst, _NN)
            dq_x = ch["eG"] * _dot(_bf(dos), _bf(S), _NN)
            dq_ref[:, ksl] = (dq_i if m.scalar else dq_i * ch["eq"]) + dq_x
            dk_i = m.unstack(_dot(_bf(dAm), _bf(qt), _TN), m.km)
            dk_x = ch["eGl"] * _dot(_bf(vs), _bf(dS), _NN)
            dk_ref[:, ksl] = (dk_i if m.scalar else dk_i * ch["ek"]) + dk_x
            if need_dg:
                bnd = (ch["eGlast"] * jnp.sum(dS * S, axis=0, keepdims=True)
                       + jnp.sum(ks * dk_x, axis=0, keepdims=True))
                X = (_bf(qt).astype(F32) * dq_i - _bf(kt).astype(F32) * dk_i) + (qs * dq_x - ks * dk_x)
                X = X + m.last_row * bnd
                if m.scalar:
                    x8 = x8 + _dot_sel(X, m.Ek(s), _NT)
                else:
                    dg_ref[:, ksl] = _dot_sel(X, m.Lsuf, _NN, x_left=False)
            dS_ref[vsl, :] = dS * ch["eGlast"] + _dot(_bf(dos), _bf(qs * ch["eG"]), _TN) * m.BD
        if m.scalar:
            dg_ref[...] = _dot_sel(x8, m.Lsuf, _NN, x_left=False)
        elif not need_dg:
            dg_ref[...] = jnp.zeros_like(dg_ref)

    sg = cfg["H"] // cfg["nh"]
    Wk, Wv = cfg["nh"] * cfg["Dk"], cfg["nh"] * cfg["Dv"]
    idx = lambda p: (_chunk_index(n - 1 - p, n, nc, reverse), 0)
    bk, bv, bg = pl.BlockSpec((C, HK), idx), pl.BlockSpec((C, HV), idx), pl.BlockSpec((C, GW), idx)
    return pl.pallas_call(
        body, name=f"scan_bwd_{kind}_{'r' if reverse else 'f'}",
        out_shape=(jax.ShapeDtypeStruct((T, HK), F32), jax.ShapeDtypeStruct((T, HK), F32),
                   jax.ShapeDtypeStruct((T, HV), F32), jax.ShapeDtypeStruct((T, GW), F32)),
        grid=(n,),
        in_specs=[bk, bk, bv, bg,
                  pl.BlockSpec((1, sg * Wv, Wk), lambda p: (_chunk_index(n - 1 - p, n, nc, reverse), 0, 0)), bv],
        out_specs=(bk, bk, bv, bg),
        scratch_shapes=[pltpu.VMEM((sg * Wv, Wk), F32)],
        compiler_params=_params(("arbitrary",)),
    )(q, k, v, g, st, do)


@functools.lru_cache(maxsize=None)
def _make_scan(kind, reverse, need_dg, Tc):
    @jax.custom_vjp
    def scan(q, k, v, g):
        return _scan_fwd_call(kind, reverse, q, k, v, g, Tc)[0]

    def fwd(q, k, v, g):
        o, st = _scan_fwd_call(kind, reverse, q, k, v, g, Tc)
        return o, (q, k, v, g, st)

    def bwd(res, do):
        q, k, v, g, st = res
        return _scan_bwd_call(kind, reverse, need_dg, q, k, v, g, st, do, Tc)

    scan.defvjp(fwd, bwd)
    return scan


def _bidir(kind, need_dg, Tc, q, kf, kb, v, gf, gb):
    return (_make_scan(kind, False, need_dg, Tc)(q, kf, v, gf)
            + _make_scan(kind, True, need_dg, Tc)(q, kb, v, gb))


def _seg(P, name):
    o, z = _OFF[name]
    return P[:, o:o + z]


def _silu(x):
    return x * jax.nn.sigmoid(x)


def _gla_mixer(P, sp, l, Tc):
    T = P.shape[0]
    q = _seg(P, "gla_q") * (32 ** -0.5)
    k, v, r = _seg(P, "gla_k"), _seg(P, "gla_v"), _seg(P, "gla_r")
    lr = _seg(P, "gla_lr").reshape(T, 2, 16)
    z = jnp.einsum("tnr,nrk->tnk", lr, sp["gla_gate_up"][l]) + sp["gla_gate_b"][l]
    logg = jax.nn.log_sigmoid(z) / GLA_TAU
    o = _bidir("gla", True, Tc, q, k, k, v, logg[:, 0], logg[:, 1])
    o4 = o.reshape(T, 4, 64)
    o4 = o4 * lax.rsqrt(jnp.mean(o4 * o4, axis=-1, keepdims=True) + RMS_EPS) * sp["gla_norm"][l].reshape(4, 64)
    return o4.reshape(T, 256) * _silu(r)


def _dwconv(u, w, b, Tc):
    def one(seg):
        n = seg.shape[0]
        p = jnp.pad(seg, ((2, 2), (0, 0)))
        acc = p[0:n] * w[0]
        for t in range(1, 5):
            acc = acc + p[t:t + n] * w[t]
        return acc + b

    return jnp.concatenate([one(u[:Tc]), one(u[Tc:])], axis=0)


def _ssd_mixer(P, sp, l, Tc):
    T = P.shape[0]
    z, xbc = _seg(P, "ssd_z"), _seg(P, "ssd_xbc")
    dt = _seg(P, "ssd_dt").reshape(T, 2, 8)
    xbc = _silu(_dwconv(xbc, sp["ssd_conv_w"][l], sp["ssd_conv_b"][l], Tc))
    xs, bm, cm = xbc[:, :512], xbc[:, 512:768], xbc[:, 768:]
    dt = jax.nn.softplus(dt + sp["ssd_dt_bias"][l])
    logg = dt * (-jnp.exp(sp["ssd_a_log"][l]))

    def rep(a):
        return jnp.broadcast_to(a.reshape(T, 2, 1, 128), (T, 2, 4, 128)).reshape(T, 8, 128)

    q = rep(cm).reshape(T, 1024)
    bmr = rep(bm)
    kf = (bmr * dt[:, 0, :, None]).reshape(T, 1024)
    kb = (bmr * dt[:, 1, :, None]).reshape(T, 1024)
    y = _bidir("ssd", True, Tc, q, kf, kb, xs, logg[:, 0], logg[:, 1])
    y = y + (sp["ssd_d"][l][:, None] * xs.reshape(T, 8, 64)).reshape(T, 512)
    ones, zeros = jnp.ones((8, 512), F32), jnp.zeros((8, 512), F32)
    return _make_norm(Tc)(y * _silu(z), sp["ssd_norm"][l], ones, zeros)


def _rope_tables(Tl, Tc):
    rows = Tl // GRID_W
    row = jnp.repeat(jnp.arange(rows), GRID_W).astype(F32)
    col = jnp.tile(jnp.arange(GRID_W), rows).astype(F32)
    inv_freq = 10000.0 ** (-jnp.arange(16, dtype=F32) / 16)
    ang = jnp.concatenate([row[:, None] * inv_freq, col[:, None] * inv_freq], axis=-1)
    cos = jnp.concatenate([jnp.ones((Tc, 32), F32), jnp.cos(ang)], axis=0)
    sin = jnp.concatenate([jnp.zeros((Tc, 32), F32), jnp.sin(ang)], axis=0)
    return cos[:, None, :], sin[:, None, :]


def _rope(t, cos, sin):
    t1, t2 = t[..., :32], t[..., 32:]
    return jnp.concatenate([t1 * cos - t2 * sin, t2 * cos + t1 * sin], axis=-1)


def _ret_mixer(P, sp, l, Tc, cos, sin):
    T = P.shape[0]
    q = (_seg(P, "ret_q") * (64 ** -0.5)).reshape(T, 4, 64)
    k = _seg(P, "ret_k").reshape(T, 4, 64)
    v, g = _seg(P, "ret_v"), _seg(P, "ret_g")
    q = _rope(q, cos, sin).reshape(T, 256)
    k = _rope(k, cos, sin).reshape(T, 256)
    log_gamma = jnp.log1p(-jnp.exp2(-5.0 - jnp.arange(4, dtype=F32)))
    lg = jnp.broadcast_to(jnp.concatenate([log_gamma, jnp.zeros((GPAD - 4,), F32)])[None, :], (T, GPAD))
    o = _bidir("ret", False, Tc, q, k, k, v, lg, lg).reshape(T, 4, 64)
    mu = jnp.mean(o, axis=-1, keepdims=True)
    oc = o - mu
    o = oc * lax.rsqrt(jnp.mean(oc * oc, axis=-1, keepdims=True) + RMS_EPS) * sp["ret_norm"][l].reshape(4, 64)
    return o.reshape(T, 256) * _silu(g)


def _pad_gate(g):
    return jnp.pad(g, ((0, 0), (0, GPAD - g.shape[1]))) if g.shape[1] < GPAD else g


def _rows8(first, second):
    z = jnp.zeros((6,) + first.shape, F32)
    return jnp.concatenate([first[None], second[None], z], axis=0)


def _local_forward(xcat, mod_l, mod_c, sp, gs, W, Tc):
    Tt = xcat.shape[0]
    cos, sin = _rope_tables(Tt - Tc, Tc)
    norm = _make_norm(Tc)
    zero = jnp.zeros((8, D), F32)
    X = xcat
    for l in range(DEPTH):
        ml, mc = mod_l[l].reshape(6, D), mod_c[l].reshape(6, D)
        h = norm(X, sp["norm_mix_pre"][l], _rows8(1.0 + mc[1], 1.0 + ml[1]), _rows8(mc[0], ml[0]))
        P = linear(h, W["w_in"][l], gs["w_in"][l])
        mixed = jnp.concatenate([_gla_mixer(P, sp, l, Tc), _ssd_mixer(P, sp, l, Tc),
                                 _ret_mixer(P, sp, l, Tc, cos, sin)], axis=-1)
        M = linear(mixed, W["w_out"][l], gs["w_out"][l])
        X = X + norm(M, sp["norm_mix_post"][l], _rows8(mc[2], ml[2]), zero)
        h = norm(X, sp["norm_ffn_pre"][l], _rows8(1.0 + mc[4], 1.0 + ml[4]), _rows8(mc[3], ml[3]))
        U = linear(h, W["ffn_w13"][l], gs["ffn_w13"][l])
        act = _silu(U[:, :FFN_H]) * U[:, FFN_H:]
        Fo = linear(act, W["ffn_w2"][l], gs["ffn_w2"][l])
        X = X + norm(Fo, sp["norm_ffn_post"][l], _rows8(mc[5], ml[5]), zero)
    return X


def _loss_call(X, target, Tc):
    Tt, W = X.shape
    tr = Tc
    nt = Tt // tr

    def body(x_ref, t_ref, loss_ref, dx_ref, acc_ref):
        i = pl.program_id(0)

        @pl.when(i == 0)
        def _():
            acc_ref[...] = jnp.zeros_like(acc_ref)
            dx_ref[...] = jnp.zeros_like(dx_ref)

        @pl.when(i > 0)
        def _():
            e = x_ref[...] - t_ref[...]
            dx_ref[...] = e * (1.0 / W)
            acc_ref[...] += jnp.sum(e * e, axis=0, keepdims=True)

        @pl.when(i == nt - 1)
        def _():
            loss_ref[...] = jnp.full(loss_ref.shape, (0.5 / W) * jnp.sum(acc_ref[...]), F32)

    loss, dx = pl.pallas_call(
        body, name="loss",
        out_shape=(jax.ShapeDtypeStruct((8, 128), F32), jax.ShapeDtypeStruct((Tt, W), F32)),
        grid=(nt,),
        in_specs=[pl.BlockSpec((tr, W), lambda i: (i, 0)),
                  pl.BlockSpec((tr, W), lambda i: (jnp.maximum(i - 1, 0), 0))],
        out_specs=(pl.BlockSpec((8, 128), lambda i: (0, 0)), pl.BlockSpec((tr, W), lambda i: (i, 0))),
        scratch_shapes=[pltpu.VMEM((1, W), F32)],
        compiler_params=_params(("arbitrary",)),
    )(X, target)
    return loss[0, 0], dx


def _adamw_call(w, g, m, v, name):
    R, Cc = w.shape
    tr = _pick(R, (512, 352, 256, 128, 64, 32, 16, 8))
    c1 = 1.0 - ADAM_B1 ** ADAM_STEP
    c2 = 1.0 - ADAM_B2 ** ADAM_STEP

    def body(w_ref, g_ref, m_ref, v_ref, d_ref, nm_ref, nv_ref):
        gv = g_ref[...]
        nm = ADAM_B1 * m_ref[...] + (1.0 - ADAM_B1) * gv
        nv = ADAM_B2 * v_ref[...] + (1.0 - ADAM_B2) * (gv * gv)
        d_ref[...] = -ADAM_LR * ((nm / c1) / (jnp.sqrt(nv / c2) + ADAM_EPS) + ADAM_WD * w_ref[...])
        nm_ref[...] = nm
        nv_ref[...] = nv

    spec = pl.BlockSpec((tr, Cc), lambda i: (i, 0))
    sh = jax.ShapeDtypeStruct((R, Cc), F32)
    return pl.pallas_call(
        body, name=name, out_shape=(sh, sh, sh), grid=(R // tr,),
        in_specs=[spec] * 4, out_specs=(spec,) * 3, compiler_params=_params(("parallel",)),
    )(w, g, m, v)


def _sum_call(xs, name, also_bf16=False):
    R, Cc = xs[0].shape
    tr = _pick(R, (512, 352, 256, 128, 64, 32, 16))
    k = len(xs)

    def body(*refs):
        acc = refs[0][...].astype(F32)
        for r in refs[1:k]:
            acc = acc + r[...].astype(F32)
        refs[k][...] = acc
        if also_bf16:
            refs[k + 1][...] = acc.astype(BF16)

    spec = pl.BlockSpec((tr, Cc), lambda i: (i, 0))
    sh = jax.ShapeDtypeStruct((R, Cc), F32)
    return pl.pallas_call(
        body, name=name, grid=(R // tr,), in_specs=[spec] * k,
        out_shape=(sh, jax.ShapeDtypeStruct((R, Cc), BF16)) if also_bf16 else sh,
        out_specs=(spec, spec) if also_bf16 else spec, compiler_params=_params(("parallel",)),
    )(*xs)


MESH = pl.DeviceIdType.MESH
ANY = pl.BlockSpec(memory_space=pl.ANY)


def _me():
    return lax.axis_index("x"), lax.axis_index("y"), lax.axis_index("c")


def _two_level_gather_body(n_arr, x_refs, out_refs, send_sems, recv_sems, local_sems):
    x, y, c = _me()
    me, sibling = (x, y, c), (x, y, 1 - c)
    chips = [(1 - x, y), (x, 1 - y), (1 - x, 1 - y)]

    def slab(a, px, py, pc):
        return out_refs[a].at[4 * px + 2 * py + pc]

    def copy(a, k, block, to, src=None):
        return pltpu.make_async_remote_copy(
            src_ref=slab(a, *block) if src is None else src, dst_ref=slab(a, *block),
            send_sem=send_sems.at[a, k], recv_sem=recv_sems.at[a, k], device_id=to, device_id_type=MESH)

    mine = [pltpu.make_async_copy(x_refs[a], slab(a, *me), local_sems.at[a]) for a in range(n_arr)]
    for cp in mine:
        cp.start()
    first = []
    for a in range(n_arr):
        first.append(copy(a, 0, me, sibling, src=x_refs[a]))
        first += [copy(a, 1 + j, me, (*chip, c), src=x_refs[a]) for j, chip in enumerate(chips)]
    for cp in first:
        cp.start()
    passed = []
    for j, chip in enumerate(chips):
        for a in range(n_arr):
            copy(a, 1 + j, (*chip, c), me).wait_recv()
            fw = copy(a, 4 + j, (*chip, c), sibling)
            fw.start()
            passed.append(fw)
    for a in range(n_arr):
        copy(a, 0, sibling, me).wait_recv()
        for j, chip in enumerate(chips):
            copy(a, 4 + j, (*chip, 1 - c), me).wait_recv()
    for cp in first + passed:
        cp.wait_send()
    for cp in mine:
        cp.wait()


def _gather_big(xs, name):
    n_arr = len(xs)

    def body(*refs):
        _two_level_gather_body(n_arr, refs[:n_arr], refs[n_arr:2 * n_arr], *refs[2 * n_arr:])

    return pl.pallas_call(
        body, name=name,
        out_shape=tuple(jax.ShapeDtypeStruct((N_DEV,) + a.shape, a.dtype) for a in xs),
        in_specs=[ANY] * n_arr, out_specs=(ANY,) * n_arr,
        scratch_shapes=[pltpu.SemaphoreType.DMA((n_arr, 7)), pltpu.SemaphoreType.DMA((n_arr, 7)),
                        pltpu.SemaphoreType.DMA((n_arr,))],
    )(*xs)


def _gather_small(x, name):
    def body(x_ref, out_ref, send_sems, recv_sems, local_sems):
        _two_level_gather_body(1, [x_ref], [out_ref], send_sems, recv_sems, local_sems)

    vm = pl.BlockSpec(memory_space=pltpu.VMEM)
    return pl.pallas_call(
        body, name=name,
        out_shape=jax.ShapeDtypeStruct((N_DEV,) + x.shape, x.dtype),
        in_specs=[vm], out_specs=vm,
        scratch_shapes=[pltpu.SemaphoreType.DMA((1, 7)), pltpu.SemaphoreType.DMA((1, 7)),
                        pltpu.SemaphoreType.DMA((1,))],
    )(x)


def _exchange_sibling(gs_, name):
    n_arr = len(gs_)

    def body(*refs):
        g_refs, out_refs = refs[:n_arr], refs[n_arr:2 * n_arr]
        send_sems, recv_sems = refs[2 * n_arr:]
        x, y, c = _me()
        cps = []
        for a in range(n_arr):
            for px in range(2):
                for py in range(2):
                    i = 2 * px + py
                    cps.append(pltpu.make_async_remote_copy(
                        src_ref=g_refs[a].at[4 * px + 2 * py + (1 - c)], dst_ref=out_refs[a].at[i],
                        send_sem=send_sems.at[a, i], recv_sem=recv_sems.at[a, i],
                        device_id=(x, y, 1 - c), device_id_type=MESH))
        for cp in cps:
            cp.start()
        for cp in cps:
            cp.wait()

    return pl.pallas_call(
        body, name=name,
        out_shape=tuple(jax.ShapeDtypeStruct((4,) + a.shape[1:], a.dtype) for a in gs_),
        in_specs=[ANY] * n_arr, out_specs=(ANY,) * n_arr,
        scratch_shapes=[pltpu.SemaphoreType.DMA((n_arr, 4)), pltpu.SemaphoreType.DMA((n_arr, 4))],
    )(*gs_)


def _exchange_chips(ps, name):
    n_arr = len(ps)

    def body(*refs):
        p_refs, out_refs = refs[:n_arr], refs[n_arr:2 * n_arr]
        send_sems, recv_sems = refs[2 * n_arr:]
        x, y, c = _me()
        chips = [(1 - x, y), (x, 1 - y), (1 - x, 1 - y)]
        cps = []
        for a in range(n_arr):
            for j, (cx, cy) in enumerate(chips):
                cps.append(pltpu.make_async_remote_copy(
                    src_ref=p_refs[a].at[2 * cx + cy], dst_ref=out_refs[a].at[j],
                    send_sem=send_sems.at[a, j], recv_sem=recv_sems.at[a, j],
                    device_id=(cx, cy, c), device_id_type=MESH))
        for cp in cps:
            cp.start()
        for cp in cps:
            cp.wait()

    return pl.pallas_call(
        body, name=name,
        out_shape=tuple(jax.ShapeDtypeStruct((3,) + a.shape[1:], a.dtype) for a in ps),
        in_specs=[ANY] * n_arr, out_specs=(ANY,) * n_arr,
        scratch_shapes=[pltpu.SemaphoreType.DMA((n_arr, 3)), pltpu.SemaphoreType.DMA((n_arr, 3))],
    )(*ps)


def _reduce_scatter(gs_):
    x, y, c = _me()
    from_sib = _exchange_sibling(gs_, "rs_sibling")
    ps, ps16 = [], []
    for a, g in enumerate(gs_):
        R, Cc = g.shape[1:]
        mine = lax.dynamic_index_in_dim(g.reshape(4, 2, R, Cc), c, axis=1, keepdims=False)
        p32, p16 = _sum_call([mine.reshape(4 * R, Cc), from_sib[a].reshape(4 * R, Cc)], f"rs_add_sib{a}",
                             also_bf16=True)
        ps.append(p32.reshape(4, R, Cc))
        ps16.append(p16.reshape(4, R, Cc))
    from_chips = _exchange_chips(ps16, "rs_chips")
    outs = []
    for a, p in enumerate(ps):
        mine = lax.dynamic_index_in_dim(p, 2 * x + y, axis=0, keepdims=False)
        outs.append(_sum_call([mine, from_chips[a][0], from_chips[a][1], from_chips[a][2]], f"rs_add_chips{a}"))
    return outs


_SMALL = ["norm_mix_pre", "norm_mix_post", "norm_ffn_pre", "norm_ffn_post", "gla_gate_up", "gla_gate_b",
          "gla_norm", "ssd_conv_w", "ssd_conv_b", "ssd_dt_bias", "ssd_a_log", "ssd_d", "ssd_norm", "ret_norm"]


def _pack(arrs):
    flat = jnp.concatenate([a.reshape(-1) for a in arrs])
    n = flat.shape[0]
    npad = -(-n // 1024) * 1024
    return jnp.pad(flat, (0, npad - n)).reshape(npad // 128, 128)


def _unpack(buf, shapes):
    flat = buf.reshape(-1)
    out, o = [], 0
    for s in shapes:
        n = math.prod(s)
        out.append(flat[o:o + n].reshape(s))
        o += n
    return out


def kernel(x, c, ctx, c_ctx, ada_w, ada_b, norm_mix_pre, norm_mix_post, norm_ffn_pre, norm_ffn_post, w_in, w_out, gla_gate_up, gla_gate_b, gla_norm, ssd_conv_w, ssd_conv_b, ssd_dt_bias, ssd_a_log, ssd_d, ssd_norm, ret_norm, ffn_w13, ffn_w2, loss_target, m_c_ctx, m_ada_w, m_ada_b, m_norm_mix_pre, m_norm_mix_post, m_norm_ffn_pre, m_norm_ffn_post, m_w_in, m_w_out, m_gla_gate_up, m_gla_gate_b, m_gla_norm, m_ssd_conv_w, m_ssd_conv_b, m_ssd_dt_bias, m_ssd_a_log, m_ssd_d, m_ssd_norm, m_ret_norm, m_ffn_w13, m_ffn_w2, v_c_ctx, v_ada_w, v_ada_b, v_norm_mix_pre, v_norm_mix_post, v_norm_ffn_pre, v_norm_ffn_post, v_w_in, v_w_out, v_gla_gate_up, v_gla_gate_b, v_gla_norm, v_ssd_conv_w, v_ssd_conv_b, v_ssd_dt_bias, v_ssd_a_log, v_ssd_d, v_ssd_norm, v_ret_norm, v_ffn_w13, v_ffn_w2):
    P_ = dict(c_ctx=c_ctx, ada_w=ada_w, ada_b=ada_b, norm_mix_pre=norm_mix_pre, norm_mix_post=norm_mix_post,
              norm_ffn_pre=norm_ffn_pre, norm_ffn_post=norm_ffn_post, w_in=w_in, w_out=w_out,
              gla_gate_up=gla_gate_up, gla_gate_b=gla_gate_b, gla_norm=gla_norm, ssd_conv_w=ssd_conv_w,
              ssd_conv_b=ssd_conv_b, ssd_dt_bias=ssd_dt_bias, ssd_a_log=ssd_a_log, ssd_d=ssd_d,
              ssd_norm=ssd_norm, ret_norm=ret_norm, ffn_w13=ffn_w13, ffn_w2=ffn_w2)
    M_ = dict(c_ctx=m_c_ctx, ada_w=m_ada_w, ada_b=m_ada_b, norm_mix_pre=m_norm_mix_pre,
              norm_mix_post=m_norm_mix_post, norm_ffn_pre=m_norm_ffn_pre, norm_ffn_post=m_norm_ffn_post,
              w_in=m_w_in, w_out=m_w_out, gla_gate_up=m_gla_gate_up, gla_gate_b=m_gla_gate_b,
              gla_norm=m_gla_norm, ssd_conv_w=m_ssd_conv_w, ssd_conv_b=m_ssd_conv_b, ssd_dt_bias=m_ssd_dt_bias,
              ssd_a_log=m_ssd_a_log, ssd_d=m_ssd_d, ssd_norm=m_ssd_norm, ret_norm=m_ret_norm,
              ffn_w13=m_ffn_w13, ffn_w2=m_ffn_w2)
    V_ = dict(c_ctx=v_c_ctx, ada_w=v_ada_w, ada_b=v_ada_b, norm_mix_pre=v_norm_mix_pre,
              norm_mix_post=v_norm_mix_post, norm_ffn_pre=v_norm_ffn_pre, norm_ffn_post=v_norm_ffn_post,
              w_in=v_w_in, w_out=v_w_out, gla_gate_up=v_gla_gate_up, gla_gate_b=v_gla_gate_b,
              gla_norm=v_gla_norm, ssd_conv_w=v_ssd_conv_w, ssd_conv_b=v_ssd_conv_b, ssd_dt_bias=v_ssd_dt_bias,
              ssd_a_log=v_ssd_a_log, ssd_d=v_ssd_d, ssd_norm=v_ssd_norm, ret_norm=v_ret_norm,
              ffn_w13=v_ffn_w13, ffn_w2=v_ffn_w2)
    order = ["c_ctx", "ada_w", "ada_b", "norm_mix_pre", "norm_mix_post", "norm_ffn_pre", "norm_ffn_post", "w_in",
             "w_out", "gla_gate_up", "gla_gate_b", "gla_norm", "ssd_conv_w", "ssd_conv_b", "ssd_dt_bias",
             "ssd_a_log", "ssd_d", "ssd_norm", "ret_norm", "ffn_w13", "ffn_w2"]

    mx, my, mc_ = _me()
    me = 4 * mx + 2 * my + mc_
    Tl, Tc = x.shape[1], ctx.shape[1]
    n_in, n_out, n_13, n_2 = w_in.shape[2], w_out.shape[1], ffn_w13.shape[2], ffn_w2.shape[1]
    n_ada = ada_w.shape[2]

    shards = [w_in.astype(BF16).reshape(DEPTH * D, n_in), w_out.astype(BF16).reshape(DEPTH * n_out, D),
              ffn_w13.astype(BF16).reshape(DEPTH * D, n_13), ffn_w2.astype(BF16).reshape(DEPTH * n_2, D)]
    g_in, g_out, g_13, g_2 = _gather_big(shards, "gather_weights")
    W = dict(
        w_in=_permute_cols(jnp.moveaxis(g_in.reshape(N_DEV, DEPTH, D, n_in), 0, 2).reshape(DEPTH, D, N_DEV * n_in)),
        w_out=jnp.moveaxis(g_out.reshape(N_DEV, DEPTH, n_out, D), 0, 1).reshape(DEPTH, N_DEV * n_out, D),
        ffn_w13=jnp.moveaxis(g_13.reshape(N_DEV, DEPTH, D, n_13), 0, 2).reshape(DEPTH, D, N_DEV * n_13),
        ffn_w2=jnp.moveaxis(g_2.reshape(N_DEV, DEPTH, n_2, D), 0, 1).reshape(DEPTH, N_DEV * n_2, D),
    )

    cw = ssd_conv_w.shape[2]
    small_in = jnp.concatenate([jnp.pad(c, ((0, 7), (0, 0))).reshape(-1),
                                ssd_conv_w.reshape(-1)]).reshape(-1, 128)
    n_c_rows = 8 * D // 128
    small_in = jnp.pad(small_in, ((0, -small_in.shape[0] % 8), (0, 0)))
    gathered = _gather_small(small_in, "gather_c_conv")
    c_all = gathered[:, :n_c_rows].reshape(N_DEV, 8, D)[:, 0]
    conv_rows = DEPTH * 5 * cw // 128
    conv_full = gathered[:, n_c_rows:n_c_rows + conv_rows].reshape(N_DEV, DEPTH, 5, cw)
    conv_full = jnp.moveaxis(conv_full, 0, 2).reshape(DEPTH, 5, N_DEV * cw)
    c9 = jnp.concatenate([c_all, c_ctx[None], jnp.zeros((7, D), F32)], axis=0)
    s9 = _silu(c9)
    mod_piece = jnp.concatenate([_mm(s9, ada_w[l], name="mm_mod") for l in range(DEPTH)], axis=0)
    mod_g = _gather_small(mod_piece, "gather_mod")
    mod_all = jnp.moveaxis(mod_g.reshape(N_DEV, DEPTH, 16, n_ada), 0, 2).reshape(DEPTH, 16, N_DEV * n_ada)
    mod_all = mod_all + ada_b[:, None, :]
    mod_l = lax.dynamic_index_in_dim(mod_all, me, axis=1, keepdims=False)
    mod_c = mod_all[:, 8]

    sp = {n: P_[n] for n in _SMALL}
    sp["ssd_conv_w"] = conv_full
    gs = dict(w_in=jnp.zeros((DEPTH, D, IN_PAD), F32), w_out=jnp.zeros((DEPTH, D, D), F32),
              ffn_w13=jnp.zeros((DEPTH, D, 2 * FFN_H), F32), ffn_w2=jnp.zeros((DEPTH, FFN_H, D), F32))
    xcat = jnp.concatenate([ctx[0], x[0]], axis=0)
    Xf, vjp = jax.vjp(lambda xc, ml, mc, sp_, gs_: _local_forward(xc, ml, mc, sp_, gs_, W, Tc),
                      xcat, mod_l, mod_c, sp, gs)
    loss_local, dX = _loss_call(Xf, loss_target[0], Tc)
    d_xcat, d_mod_l, d_mod_c, d_sp, d_gs = vjp(dX)
    loss = lax.psum(loss_local, ("x", "y", "c"))
    grad_x = d_xcat[Tc:][None]

    def dev_major_cols(g, n):
        K = g.shape[1]
        return jnp.moveaxis(g.reshape(DEPTH, K, N_DEV, n), 2, 0).reshape(N_DEV, DEPTH * K, n)

    def dev_major_rows(g, n):
        return jnp.moveaxis(g.reshape(DEPTH, N_DEV, n, D), 1, 0).reshape(N_DEV, DEPTH * n, D)

    big = [dev_major_cols(_unpermute_cols(d_gs["w_in"]), n_in), dev_major_rows(d_gs["w_out"], n_out),
           dev_major_cols(d_gs["ffn_w13"], n_13), dev_major_rows(d_gs["ffn_w2"], n_2)]
    r_in, r_out, r_13, r_2 = _reduce_scatter(big)
    G = dict(w_in=r_in.reshape(DEPTH, D, n_in), w_out=r_out.reshape(DEPTH, n_out, D),
             ffn_w13=r_13.reshape(DEPTH, D, n_13), ffn_w2=r_2.reshape(DEPTH, n_2, D))

    dmod_rows = jnp.concatenate([d_mod_l, d_mod_c], axis=0)
    dmod_g = _gather_small(dmod_rows, "gather_dmod").reshape(N_DEV, 2, DEPTH, 6 * D)
    dl = jnp.moveaxis(dmod_g[:, 0], 0, 1)
    dc = dmod_g[:, 1, :, :]
    dc_tot = dc[0]
    for d_ in range(1, N_DEV):
        dc_tot = dc_tot + dc[d_]
    dmod9 = jnp.concatenate([dl, dc_tot[:, None, :], jnp.zeros((DEPTH, 7, 6 * D), F32)], axis=1)
    g_ada_b = dmod9[:, 0]
    for r_ in range(1, 9):
        g_ada_b = g_ada_b + dmod9[:, r_]
    dmod9_mine = lax.dynamic_slice_in_dim(dmod9, me * n_ada, n_ada, axis=2)
    s9T = jnp.pad(s9.T, ((0, 0), (0, 112)))
    g_ada_w = jnp.stack([_mm(s9T, jnp.pad(dmod9_mine[l], ((0, 112), (0, 0))), name="mm_dada")
                         for l in range(DEPTH)])
    ds9 = _mm(dmod9_mine[0], ada_w[0], trans_b=True, name="mm_ds9")
    for l in range(1, DEPTH):
        ds9 = ds9 + _mm(dmod9_mine[l], ada_w[l], trans_b=True, name="mm_ds9")
    ds_ctx_part = ds9[8]

    small_names = [n for n in _SMALL]
    small_parts = [d_sp[n] for n in small_names] + [ds_ctx_part]
    packed = _pack(small_parts)
    allp = _gather_small(packed, "gather_small_grads")
    summed = _sum_call([allp[d_] for d_ in range(N_DEV)], "sum_small_grads")
    parts = _unpack(summed, [p.shape for p in small_parts])
    for n, p in zip(small_names, parts[:-1]):
        G[n] = p
    sig = jax.nn.sigmoid(c_ctx)
    G["c_ctx"] = parts[-1] * (sig * (1.0 + c_ctx * (1.0 - sig)))
    G["ssd_conv_w"] = lax.dynamic_slice_in_dim(G["ssd_conv_w"], me * cw, cw, axis=2)
    G["ada_w"] = g_ada_w
    G["ada_b"] = g_ada_b

    delta, new_m, new_v = {}, {}, {}
    for n in ["ada_w", "w_in", "w_out", "ffn_w13", "ffn_w2"]:
        sh = P_[n].shape
        f2 = lambda a: a.reshape(sh[0] * sh[1], sh[2])
        d_, m_, v_ = _adamw_call(f2(P_[n]), f2(G[n]), f2(M_[n]), f2(V_[n]), f"adamw_{n}")
        delta[n], new_m[n], new_v[n] = d_.reshape(sh), m_.reshape(sh), v_.reshape(sh)
    rest = [n for n in order if n not in delta]
    shapes = [P_[n].shape for n in rest]
    d_, m_, v_ = _adamw_call(_pack([P_[n] for n in rest]), _pack([G[n] for n in rest]),
                             _pack([M_[n] for n in rest]), _pack([V_[n] for n in rest]), "adamw_small")
    for n, a, b, e in zip(rest, _unpack(d_, shapes), _unpack(m_, shapes), _unpack(v_, shapes)):
        delta[n], new_m[n], new_v[n] = a, b, e

    return (loss, grad_x, *[G[n] for n in order], *[delta[n] for n in order],
            *[new_m[n] for n in order], *[new_v[n] for n in order])
```

```python
import functools
import math

import jax
import jax.numpy as jnp
from jax import lax
from jax.experimental import pallas as pl
from jax.experimental.pallas import tpu as pltpu

F32 = jnp.float32
BF16 = jnp.bfloat16

D = 1024
DEPTH = 4
GRID_W = 64
RMS_EPS = 1e-6
GLA_TAU = 16.0
FFN_H = 2816
IN_COLS = 3376
N_DEV = 8
ADAM_LR, ADAM_B1, ADAM_B2, ADAM_EPS, ADAM_WD, ADAM_STEP = 0.001, 0.9, 0.999, 1e-08, 0.01, 10

VMEM_LIMIT = 48 * 1024 * 1024

_ORIG = dict(gla_q=(0, 128), gla_k=(128, 128), gla_v=(256, 256), gla_r=(512, 256), gla_lr=(768, 32),
             ssd_z=(800, 512), ssd_xbc=(1312, 1024), ssd_dt=(2336, 16), ret_q=(2352, 256), ret_k=(2608, 256),
             ret_v=(2864, 256), ret_g=(3120, 256))
_R_ORDER = ["gla_v", "gla_r", "ret_q", "ret_k", "ret_v", "ret_g", "ssd_z", "gla_q", "gla_k", "gla_lr", "ssd_dt"]
R_W = 2432
_ROFF = {}
_o = 0
for _n in _R_ORDER:
    _ROFF[_n] = _o
    _o += _ORIG[_n][1]
MISC = _ROFF["gla_lr"]
assert MISC == 2304 and _o == 2352


def _split_w_in(w):
    xs, xz = _ORIG["ssd_xbc"]
    parts = [w[..., _ORIG[n][0]:_ORIG[n][0] + _ORIG[n][1]] for n in _R_ORDER]
    parts.append(jnp.zeros(w.shape[:-1] + (R_W - _o,), w.dtype))
    return w[..., xs:xs + xz], jnp.concatenate(parts, axis=-1)


def _merge_w_in(wx, wr):
    pieces = []
    for n, (s, z) in sorted(_ORIG.items(), key=lambda t: t[1][0]):
        pieces.append(wx if n == "ssd_xbc" else wr[..., _ROFF[n]:_ROFF[n] + z])
    return jnp.concatenate(pieces, axis=-1)


def _pick(n, cands):
    for c in cands:
        if n % c == 0:
            return c
    return n


def _params(sem=None):
    kw = dict(vmem_limit_bytes=VMEM_LIMIT)
    if sem is not None:
        kw["dimension_semantics"] = sem
    return pltpu.CompilerParams(**kw)


def _iota(shape, dim):
    return lax.broadcasted_iota(jnp.int32, shape, dim)


def _dot(a, b, dims):
    return lax.dot_general(a, b, (dims, ((), ())), preferred_element_type=F32)


_NN = ((1,), (0,))
_NT = ((1,), (1,))
_TN = ((0,), (0,))


def _bf(x):
    return x.astype(BF16)


def _dot_sel(x, e, dims, x_left=True):
    eb = e.astype(BF16)
    hi = x.astype(BF16)
    r1 = x - hi.astype(F32)
    mid = r1.astype(BF16)
    lo = (r1 - mid.astype(F32)).astype(BF16)
    out = None
    for p in (hi, mid, lo):
        t = _dot(p, eb, dims) if x_left else _dot(eb, p, dims)
        out = t if out is None else out + t
    return out


@jax.custom_vjp
def _sel(x, e):
    return _dot_sel(x, e, _NN)


_sel.defvjp(lambda x, e: (_dot_sel(x, e, _NN), e), lambda e, g: (_dot_sel(g, e, _NT), jnp.zeros_like(e)))


def _sig(x):
    e = jnp.exp(-jnp.abs(x))
    return jnp.where(x >= 0, 1.0 / (1.0 + e), e / (1.0 + e))


@jax.custom_vjp
def _sigmoid(x):
    return _sig(x)


def _sigmoid_fwd(x):
    s = _sig(x)
    return s, s


_sigmoid.defvjp(_sigmoid_fwd, lambda s, g: (g * s * (1.0 - s),))


def _silu(x):
    return x * _sigmoid(x)


@jax.custom_vjp
def _softplus(x):
    return jnp.maximum(x, 0.0) + jnp.log(1.0 + jnp.exp(-jnp.abs(x)))


_softplus.defvjp(lambda x: (jnp.maximum(x, 0.0) + jnp.log(1.0 + jnp.exp(-jnp.abs(x))), x),
                 lambda x, g: (g * _sig(x),))


def _log_sigmoid(x):
    return -_softplus(-x)


@jax.custom_vjp
def _mm_bf(x, w):
    return _dot(_bf(x), _bf(w), _NN)


_mm_bf.defvjp(lambda x, w: (_dot(_bf(x), _bf(w), _NN), (x, w)),
              lambda r, g: (_dot(_bf(g), _bf(r[1]), _NT), _dot(_bf(r[0]), _bf(g), _TN)))


def _mm(a, b, *, trans_b=False, name, add=None):
    M, K = a.shape
    N = b.shape[0] if trans_b else b.shape[1]
    assert (b.shape[1] if trans_b else b.shape[0]) == K
    tm = _pick(M, (1408, 1088, 1024, 512, 256, 128, 64, 32, 16))
    tn = _pick(N, (512, 384, 256, 128))
    tk = _pick(K, (2176, 1408, 1024, 512, 384, 256, 128))
    nk = K // tk
    dims = _NT if trans_b else _NN
    has_add = add is not None

    def body(*refs):
        a_ref, b_ref = refs[0], refs[1]
        o_ref, acc_ref = refs[-2], refs[-1]
        k = pl.program_id(2)

        @pl.when(k == 0)
        def _():
            acc_ref[...] = refs[2][...] if has_add else jnp.zeros_like(acc_ref)

        acc_ref[...] += _dot(a_ref[...].astype(BF16), b_ref[...].astype(BF16), dims)

        @pl.when(k == nk - 1)
        def _():
            o_ref[...] = acc_ref[...]

    b_spec = (pl.BlockSpec((tn, tk), lambda i, j, k: (j, k)) if trans_b
              else pl.BlockSpec((tk, tn), lambda i, j, k: (k, j)))
    o_spec = pl.BlockSpec((tm, tn), lambda i, j, k: (i, j))
    return pl.pallas_call(
        body, name=name,
        out_shape=jax.ShapeDtypeStruct((M, N), F32),
        grid=(M // tm, N // tn, nk),
        in_specs=[pl.BlockSpec((tm, tk), lambda i, j, k: (i, k)), b_spec] + ([o_spec] if has_add else []),
        out_specs=o_spec,
        scratch_shapes=[pltpu.VMEM((tm, tn), F32)],
        compiler_params=_params(("parallel", "parallel", "arbitrary")),
    )(*((a, b, add) if has_add else (a, b)))


def _mm_tn(a, g, *, name):
    M, K = a.shape
    N = g.shape[1]
    tm = _pick(M, (1088, 512, 256, 128, 64, 32, 16))
    tk = _pick(K, (1408, 1024, 512, 256, 128))
    tn = _pick(N, (512, 384, 256, 128))
    nm = M // tm

    def body(a_ref, g_ref, o_ref, acc_ref):
        i = pl.program_id(2)

        @pl.when(i == 0)
        def _():
            acc_ref[...] = jnp.zeros_like(acc_ref)

        acc_ref[...] += _dot(a_ref[...].astype(BF16), g_ref[...].astype(BF16), _TN)

        @pl.when(i == nm - 1)
        def _():
            o_ref[...] = acc_ref[...]

    return pl.pallas_call(
        body, name=name,
        out_shape=jax.ShapeDtypeStruct((K, N), F32),
        grid=(K // tk, N // tn, nm),
        in_specs=[pl.BlockSpec((tm, tk), lambda k, j, i: (i, k)), pl.BlockSpec((tm, tn), lambda k, j, i: (i, j))],
        out_specs=pl.BlockSpec((tk, tn), lambda k, j, i: (k, j)),
        scratch_shapes=[pltpu.VMEM((tk, tn), F32)],
        compiler_params=_params(("parallel", "parallel", "arbitrary")),
    )(a, g)


@jax.custom_vjp
def linear(a, w, gslot):
    return _mm(a, w, name="mm_fwd")


def _linear_fwd(a, w, gslot):
    return _mm(a, w, name="mm_fwd"), (a, w)


def _linear_bwd(res, g):
    a, w = res
    return _mm(g, w, trans_b=True, name="mm_dx"), jnp.zeros_like(w), _mm_tn(a, g, name="mm_dw")


linear.defvjp(_linear_fwd, _linear_bwd)


@jax.custom_vjp
def linear2(a, w1, w2, gs1, gs2):
    return _mm(a, w1, name="mm_fwd"), _mm(a, w2, name="mm_fwd")


def _linear2_fwd(a, w1, w2, gs1, gs2):
    return (_mm(a, w1, name="mm_fwd"), _mm(a, w2, name="mm_fwd")), (a, w1, w2)


def _linear2_bwd(res, gs):
    a, w1, w2 = res
    g1, g2 = gs
    da = _mm(g1, w1, trans_b=True, name="mm_dx")
    da = _mm(g2, w2, trans_b=True, name="mm_dx_acc", add=da)
    return da, jnp.zeros_like(w1), jnp.zeros_like(w2), _mm_tn(a, g1, name="mm_dw"), _mm_tn(a, g2, name="mm_dw")


linear2.defvjp(_linear2_fwd, _linear2_bwd)


def _norm_fwd_call(x, w, a2, b2, res, tr):
    T, W = x.shape
    has_res = res is not None

    def body(*refs):
        x_ref, w_ref, a_ref, b_ref = refs[:4]
        y_ref = refs[-1]
        seg = jnp.minimum(pl.program_id(0), 1)
        xv = x_ref[...]
        rstd = lax.rsqrt(jnp.mean(xv * xv, axis=-1, keepdims=True) + RMS_EPS)
        y = a_ref[pl.ds(seg, 1), :] * (xv * rstd * w_ref[...]) + b_ref[pl.ds(seg, 1), :]
        y_ref[...] = y + refs[4][...] if has_res else y

    row = pl.BlockSpec((tr, W), lambda i: (i, 0))
    small = pl.BlockSpec((8, W), lambda i: (0, 0))
    return pl.pallas_call(
        body, name="norm_fwd",
        out_shape=jax.ShapeDtypeStruct((T, W), F32),
        grid=(T // tr,),
        in_specs=[row, pl.BlockSpec((1, W), lambda i: (0, 0)), small, small] + ([row] if has_res else []),
        out_specs=row,
        compiler_params=_params(("parallel",)),
    )(*((x, w.reshape(1, W), a2, b2) + ((res,) if has_res else ())))


def _norm_bwd_call(x, w, a2, dy, tr):
    T, W = x.shape

    def body(x_ref, w_ref, a_ref, dy_ref, dx_ref, dw_ref, da_ref, db_ref):
        i = pl.program_id(0)
        seg = jnp.minimum(i, 1)

        @pl.when(i == 0)
        def _():
            dw_ref[...] = jnp.zeros_like(dw_ref)
            da_ref[...] = jnp.zeros_like(da_ref)
            db_ref[...] = jnp.zeros_like(db_ref)

        xv = x_ref[...]
        g = dy_ref[...]
        wv = w_ref[...]
        rstd = lax.rsqrt(jnp.mean(xv * xv, axis=-1, keepdims=True) + RMS_EPS)
        xh = xv * rstd
        da_ref[pl.ds(seg, 1), :] += jnp.sum(g * (xh * wv), axis=0, keepdims=True)
        db_ref[pl.ds(seg, 1), :] += jnp.sum(g, axis=0, keepdims=True)
        gy = g * a_ref[pl.ds(seg, 1), :]
        dw_ref[0:1, :] += jnp.sum(gy * xh, axis=0, keepdims=True)
        gx = gy * wv
        dx_ref[...] = rstd * (gx - xh * jnp.mean(gx * xh, axis=-1, keepdims=True))

    acc = jax.ShapeDtypeStruct((8, W), F32)
    acc_spec = pl.BlockSpec((8, W), lambda i: (0, 0))
    row = pl.BlockSpec((tr, W), lambda i: (i, 0))
    return pl.pallas_call(
        body, name="norm_bwd",
        out_shape=(jax.ShapeDtypeStruct((T, W), F32), acc, acc, acc),
        grid=(T // tr,),
        in_specs=[row, pl.BlockSpec((1, W), lambda i: (0, 0)), acc_spec, row],
        out_specs=(row, acc_spec, acc_spec, acc_spec),
        compiler_params=_params(("arbitrary",)),
    )(x, w.reshape(1, W), a2, dy)


@functools.lru_cache(maxsize=None)
def _make_norm(tr, has_res):
    if has_res:
        @jax.custom_vjp
        def norm_res(x, w, a2, b2, res):
            return _norm_fwd_call(x, w, a2, b2, res, tr)

        def fwd(x, w, a2, b2, res):
            return _norm_fwd_call(x, w, a2, b2, res, tr), (x, w, a2)

        def bwd(r, dy):
            dx, dw, da, db = _norm_bwd_call(*r, dy, tr)
            return dx, dw[0], da, db, dy

        norm_res.defvjp(fwd, bwd)
        return norm_res

    @jax.custom_vjp
    def norm_affine(x, w, a2, b2):
        return _norm_fwd_call(x, w, a2, b2, None, tr)

    def fwd(x, w, a2, b2):
        return _norm_fwd_call(x, w, a2, b2, None, tr), (x, w, a2)

    def bwd(r, dy):
        dx, dw, da, db = _norm_bwd_call(*r, dy, tr)
        return dx, dw[0], da, db

    norm_affine.defvjp(fwd, bwd)
    return norm_affine


def _act_call(u1, u2, dact=None):
    T, W = u1.shape
    tr = _pick(T, (512, 256, 128, 64))
    tn = _pick(W, (1408, 512, 256, 128))
    spec = pl.BlockSpec((tr, tn), lambda i, j: (i, j))
    sh = jax.ShapeDtypeStruct((T, W), F32)
    if dact is None:
        def body(a_ref, b_ref, o_ref):
            o_ref[...] = _silu(a_ref[...]) * b_ref[...]

        return pl.pallas_call(body, name="act_fwd", out_shape=sh, grid=(T // tr, W // tn), in_specs=[spec, spec],
                              out_specs=spec, compiler_params=_params(("parallel", "parallel")))(u1, u2)

    def body(a_ref, b_ref, g_ref, da_ref, db_ref):
        a, g = a_ref[...], g_ref[...]
        s = _sig(a)
        da_ref[...] = g * b_ref[...] * (s * (1.0 + a * (1.0 - s)))
        db_ref[...] = g * a * s

    return pl.pallas_call(body, name="act_bwd", out_shape=(sh, sh), grid=(T // tr, W // tn),
                          in_specs=[spec, spec, spec], out_specs=(spec, spec),
                          compiler_params=_params(("parallel", "parallel")))(u1, u2, dact)


@jax.custom_vjp
def swiglu_act(u1, u2):
    return _act_call(u1, u2)


swiglu_act.defvjp(lambda u1, u2: (_act_call(u1, u2), (u1, u2)), lambda r, g: _act_call(r[0], r[1], g))


def _conv_specs(T, Wc, tr):
    hb, nt = tr // 8, T // tr
    row = pl.BlockSpec((tr, Wc), lambda i: (i, 0))
    prev = pl.BlockSpec((8, Wc), lambda i: (jnp.maximum(i * hb - 1, 0), 0))
    nxt = pl.BlockSpec((8, Wc), lambda i: (jnp.minimum((i + 1) * hb, T // 8 - 1), 0))
    return row, prev, nxt, nt


def _fill_ext(dst_ref, cur_ref, prev_ref, next_ref, i, nt, tr):
    has_prev = (i > 1).astype(F32)
    has_next = jnp.logical_and(i > 0, i < nt - 1).astype(F32)
    dst_ref[8:16, :] = prev_ref[...] * has_prev
    dst_ref[16:16 + tr, :] = cur_ref[...]
    dst_ref[16 + tr:24 + tr, :] = next_ref[...] * has_next


def _conv_fwd_call(px, w8, b, tr):
    T, Wc = px.shape
    row, prev, nxt, nt = _conv_specs(T, Wc, tr)

    def body(x_ref, xp_ref, xn_ref, w_ref, b_ref, u_ref, xe_ref):
        i = pl.program_id(0)

        @pl.when(i == 0)
        def _():
            xe_ref[...] = jnp.zeros_like(xe_ref)

        _fill_ext(xe_ref, x_ref, xp_ref, xn_ref, i, nt, tr)
        y = b_ref[...] + w_ref[0:1, :] * xe_ref[pl.ds(14, tr), :]
        for k in range(1, 5):
            y = y + w_ref[k:k + 1, :] * xe_ref[pl.ds(14 + k, tr), :]
        u_ref[...] = y * _sig(y)

    return pl.pallas_call(
        body, name="conv_fwd", out_shape=jax.ShapeDtypeStruct((T, Wc), F32), grid=(nt,),
        in_specs=[row, prev, nxt, pl.BlockSpec((8, Wc), lambda i: (0, 0)), pl.BlockSpec((1, Wc), lambda i: (0, 0))],
        out_specs=row, scratch_shapes=[pltpu.VMEM((tr + 32, Wc), F32)],
        compiler_params=_params(("arbitrary",)),
    )(px, px, px, w8, b)


def _conv_bwd_call(px, w8, b, du, tr):
    T, Wc = px.shape
    row, prev, nxt, nt = _conv_specs(T, Wc, tr)
    E = tr + 16

    def body(x_ref, xp_ref, xn_ref, g_ref, gp_ref, gn_ref, w_ref, b_ref, dx_ref, dw_ref, db_ref,
             xe_ref, ge_ref, dy_ref):
        i = pl.program_id(0)

        @pl.when(i == 0)
        def _():
            xe_ref[...] = jnp.zeros_like(xe_ref)
            ge_ref[...] = jnp.zeros_like(ge_ref)
            dy_ref[...] = jnp.zeros_like(dy_ref)
            dw_ref[...] = jnp.zeros_like(dw_ref)
            db_ref[...] = jnp.zeros_like(db_ref)

        _fill_ext(xe_ref, x_ref, xp_ref, xn_ref, i, nt, tr)
        _fill_ext(ge_ref, g_ref, gp_ref, gn_ref, i, nt, tr)
        y = b_ref[...] + w_ref[0:1, :] * xe_ref[pl.ds(6, E), :]
        for k in range(1, 5):
            y = y + w_ref[k:k + 1, :] * xe_ref[pl.ds(6 + k, E), :]
        s = _sig(y)
        dy = ge_ref[pl.ds(8, E), :] * (s * (1.0 + y * (1.0 - s)))
        dy_ref[pl.ds(8, E), :] = dy
        dx = w_ref[0:1, :] * dy_ref[pl.ds(18, tr), :]
        for k in range(1, 5):
            dx = dx + w_ref[k:k + 1, :] * dy_ref[pl.ds(18 - k, tr), :]
        dx_ref[...] = dx
        dyt = dy_ref[pl.ds(16, tr), :]
        db_ref[0:1, :] += jnp.sum(dyt, axis=0, keepdims=True)
        for k in range(5):
            dw_ref[k:k + 1, :] += jnp.sum(dyt * xe_ref[pl.ds(14 + k, tr), :], axis=0, keepdims=True)

    acc = jax.ShapeDtypeStruct((8, Wc), F32)
    acc_spec = pl.BlockSpec((8, Wc), lambda i: (0, 0))
    ext = pltpu.VMEM((tr + 32, Wc), F32)
    return pl.pallas_call(
        body, name="conv_bwd", out_shape=(jax.ShapeDtypeStruct((T, Wc), F32), acc, acc), grid=(nt,),
        in_specs=[row, prev, nxt, row, prev, nxt, acc_spec, pl.BlockSpec((1, Wc), lambda i: (0, 0))],
        out_specs=(row, acc_spec, acc_spec), scratch_shapes=[ext, ext, ext],
        compiler_params=_params(("arbitrary",)),
    )(px, px, px, du, du, du, w8, b)


_SCAN_CFG = {
    "gla": dict(H=4, Dk=32, Dv=64, nh=4, scalar=False, C=64),
    "ssd": dict(H=8, Dk=128, Dv=64, nh=2, scalar=True, C=128),
    "ret": dict(H=4, Dk=64, Dv=64, nh=4, scalar=True, C=128),
}
GPAD = 8


def _log2(n):
    r = int(math.log2(n))
    assert 1 << r == n
    return r


class _ScanMath:
    def __init__(self, cfg, reverse):
        C = cfg["C"]
        self.C, self.reverse = C, reverse
        self.Dk, self.Dv, self.nh, self.scalar = cfg["Dk"], cfg["Dv"], cfg["nh"], cfg["scalar"]
        self.Wk, self.Wv = self.nh * self.Dk, self.nh * self.Dv
        self.nsg = cfg["H"] // self.nh
        nh, Wk, Wv = self.nh, self.Wk, self.Wv
        lk, lv, lc = _log2(self.Dk), _log2(self.Dv), _log2(C)
        r, c = _iota((C, C), 0), _iota((C, C), 1)
        self.L = ((c >= r) if reverse else (c <= r)).astype(F32)
        self.Lsuf = ((c <= r) if reverse else (c >= r)).astype(F32)
        i, j = _iota((C, nh * C), 0), _iota((C, nh * C), 1) & (C - 1)
        self.Mst = (j >= i) if reverse else (j <= i)
        self.Dj = (i == j).astype(F32)
        self.km = [((_iota((1, Wk), 1) >> lk) == h).astype(F32) for h in range(nh)]
        self.vm = [((_iota((1, Wv), 1) >> lv) == h).astype(F32) for h in range(nh)]
        self.BD = ((_iota((Wv, Wk), 0) >> lv) == (_iota((Wv, Wk), 1) >> lk)).astype(F32)
        self.last = 0 if reverse else C - 1
        self.last_row = (_iota((C, 1), 0) == self.last).astype(F32)
        self.lk, self.lc = lk, lc
        self.H = cfg["H"]

    def gates(self, g):
        if not self.scalar:
            return _dot_sel(g, self.L, _NN, x_left=False), None
        G8 = _dot_sel(g, self.L, _NN, x_left=False)
        nk, ncol = self.H * self.Dk, self.H * self.C
        ek = (_iota((GPAD, nk), 0) == (_iota((GPAD, nk), 1) >> self.lk)).astype(F32)
        ec = (_iota((GPAD, ncol), 0) == (_iota((GPAD, ncol), 1) >> self.lc)).astype(F32)
        return _dot_sel(G8, ek, _NN), _dot_sel(G8, ec, _NN)

    def Ek(self, s):
        return (_iota((GPAD, self.Wk), 0) == (_iota((GPAD, self.Wk), 1) >> self.lk) + s * self.nh).astype(F32)

    def kstack(self, x):
        return jnp.concatenate([x * self.km[h] for h in range(self.nh)], axis=0)

    def vstack(self, x):
        return jnp.concatenate([x * self.vm[h] for h in range(self.nh)], axis=0)

    def unstack(self, R, masks):
        C = self.C
        out = R[0:C] * masks[0]
        for h in range(1, self.nh):
            out = out + R[h * C:(h + 1) * C] * masks[h]
        return out

    def chunk(self, qs, ks, Gk, Gc):
        C = self.C
        Glast = Gk[self.last:self.last + 1, :]
        out = dict(Gk=Gk, Glast=Glast, eG=jnp.exp(Gk), eGl=jnp.exp(Glast - Gk), eGlast=jnp.exp(Glast))
        if self.scalar:
            Gr = jnp.sum(Gc * self.Dj, axis=0, keepdims=True)
            dec = jnp.where(self.Mst, jnp.exp(jnp.minimum(Gc - Gr, 0.0)), 0.0)
            qt, kt = qs, ks
            A = _dot(_bf(qt), _bf(self.kstack(kt)), _NT) * dec
            out.update(dec=dec, qt=qt, kt=kt, A=A)
        else:
            Gm = Gk[C // 2:C // 2 + 1, :]
            eq, ek = jnp.exp(Gk - Gm), jnp.exp(Gm - Gk)
            qt, kt = qs * eq, ks * ek
            A = jnp.where(self.Mst, _dot(_bf(qt), _bf(self.kstack(kt)), _NT), 0.0)
            out.update(eq=eq, ek=ek, qt=qt, kt=kt, A=A)
        return out


def _chunk_index(p, n, nc, reverse):
    if not reverse:
        return p
    return jnp.where(p < nc, nc - 1 - p, n - 1 + nc - p)


def _scan_dims(kind):
    cfg = _SCAN_CFG[kind]
    HK, HV = cfg["H"] * cfg["Dk"], cfg["H"] * cfg["Dv"]
    return cfg, cfg["C"], HK, HV, (GPAD if cfg["scalar"] else HK)


def _scan_fwd_call(kind, reverse, q, k, v, g, Tc):
    cfg, C, HK, HV, GW = _scan_dims(kind)
    T = q[0].shape[0]
    n, nc = T // C, Tc // C

    def body(q_ref, k_ref, v_ref, g_ref, o_ref, st_ref, S_ref):
        m = _ScanMath(cfg, reverse)

        @pl.when(pl.program_id(0) == 0)
        def _():
            S_ref[...] = jnp.zeros_like(S_ref)

        Gk_all, Gc_all = m.gates(g_ref[...])
        for s in range(m.nsg):
            ksl, vsl = slice(s * m.Wk, (s + 1) * m.Wk), slice(s * m.Wv, (s + 1) * m.Wv)
            csl = slice(s * m.nh * C, (s + 1) * m.nh * C)
            qs, ks, vs = q_ref[:, ksl], k_ref[:, ksl], v_ref[:, vsl]
            ch = m.chunk(qs, ks, Gk_all[:, ksl], Gc_all[:, csl] if m.scalar else None)
            S = S_ref[vsl, :]
            o = _dot(_bf(ch["A"]), _bf(m.vstack(vs)), _NN) + _dot(_bf(qs * ch["eG"]), _bf(S), _NT)
            o_ref[:, vsl] = o
            st_ref[0, vsl, :] = S
            S_ref[vsl, :] = S * ch["eGlast"] + _dot(_bf(vs), _bf(ks * ch["eGl"]), _TN) * m.BD

    sg = cfg["H"] // cfg["nh"]
    Wk, Wv = cfg["nh"] * cfg["Dk"], cfg["nh"] * cfg["Dv"]
    col = lambda w, j: pl.BlockSpec((C, w), lambda p: (_chunk_index(p, n, nc, reverse), j))
    return pl.pallas_call(
        body, name=f"scan_fwd_{kind}_{'r' if reverse else 'f'}",
        out_shape=(jax.ShapeDtypeStruct((T, HV), F32), jax.ShapeDtypeStruct((n, sg * Wv, Wk), F32)),
        grid=(n,),
        in_specs=[col(HK, q[1]), col(HK, k[1]), col(HV, v[1]), col(GW, g[1])],
        out_specs=(col(HV, 0),
                   pl.BlockSpec((1, sg * Wv, Wk), lambda p: (_chunk_index(p, n, nc, reverse), 0, 0))),
        scratch_shapes=[pltpu.VMEM((sg * Wv, Wk), F32)],
        compiler_params=_params(("arbitrary",)),
    )(q[0], k[0], v[0], g[0])


def _scan_bwd_call(kind, reverse, need_dg, q, k, v, g, st, do, Tc):
    cfg, C, HK, HV, GW = _scan_dims(kind)
    T = q[0].shape[0]
    n, nc = T // C, Tc // C

    def body(q_ref, k_ref, v_ref, g_ref, st_ref, do_ref, dq_ref, dk_ref, dv_ref, dg_ref, dS_ref):
        m = _ScanMath(cfg, reverse)

        @pl.when(pl.program_id(0) == 0)
        def _():
            dS_ref[...] = jnp.zeros_like(dS_ref)

        x8 = jnp.zeros((C, GPAD), F32)
        Gk_all, Gc_all = m.gates(g_ref[...])
        for s in range(m.nsg):
            ksl, vsl = slice(s * m.Wk, (s + 1) * m.Wk), slice(s * m.Wv, (s + 1) * m.Wv)
            csl = slice(s * m.nh * C, (s + 1) * m.nh * C)
            qs, ks, vs, dos = q_ref[:, ksl], k_ref[:, ksl], v_ref[:, vsl], do_ref[:, vsl]
            ch = m.chunk(qs, ks, Gk_all[:, ksl], Gc_all[:, csl] if m.scalar else None)
            S = st_ref[0, vsl, :]
            dS = dS_ref[vsl, :]
            A, qt, kt = ch["A"], ch["qt"], ch["kt"]
            dA = _dot(_bf(dos), _bf(m.vstack(vs)), _NT)
            dAm = dA * ch["dec"] if m.scalar else jnp.where(m.Mst, dA, 0.0)
            kst = _bf(m.kstack(kt))
            dv = m.unstack(_dot(_bf(A), _bf(dos), _TN), m.vm) + _dot(_bf(ks * ch["eGl"]), _bf(dS), _NT)
            dv_ref[:, vsl] = dv
            dq_i = _dot(_bf(dAm), kst, _NN)
            dq_x = ch["eG"] * _dot(_bf(dos), _bf(S), _NN)
            dq_ref[:, ksl] = (dq_i if m.scalar else dq_i * ch["eq"]) + dq_x
            dk_i = m.unstack(_dot(_bf(dAm), _bf(qt), _TN), m.km)
            dk_x = ch["eGl"] * _dot(_bf(vs), _bf(dS), _NN)
            dk_ref[:, ksl] = (dk_i if m.scalar else dk_i * ch["ek"]) + dk_x
            if need_dg:
                bnd = (ch["eGlast"] * jnp.sum(dS * S, axis=0, keepdims=True)
                       + jnp.sum(ks * dk_x, axis=0, keepdims=True))
                X = (_bf(qt).astype(F32) * dq_i - _bf(kt).astype(F32) * dk_i) + (qs * dq_x - ks * dk_x)
                X = X + m.last_row * bnd
                if m.scalar:
                    x8 = x8 + _dot_sel(X, m.Ek(s), _NT)
                else:
                    dg_ref[:, ksl] = _dot_sel(X, m.Lsuf, _NN, x_left=False)
            dS_ref[vsl, :] = dS * ch["eGlast"] + _dot(_bf(dos), _bf(qs * ch["eG"]), _TN) * m.BD
        if m.scalar:
            dg_ref[...] = _dot_sel(x8, m.Lsuf, _NN, x_left=False)
        elif not need_dg:
            dg_ref[...] = jnp.zeros_like(dg_ref)

    sg = cfg["H"] // cfg["nh"]
    Wk, Wv = cfg["nh"] * cfg["Dk"], cfg["nh"] * cfg["Dv"]
    col = lambda w, j: pl.BlockSpec((C, w), lambda p: (_chunk_index(n - 1 - p, n, nc, reverse), j))
    return pl.pallas_call(
        body, name=f"scan_bwd_{kind}_{'r' if reverse else 'f'}",
        out_shape=(jax.ShapeDtypeStruct((T, HK), F32), jax.ShapeDtypeStruct((T, HK), F32),
                   jax.ShapeDtypeStruct((T, HV), F32), jax.ShapeDtypeStruct((T, GW), F32)),
        grid=(n,),
        in_specs=[col(HK, q[1]), col(HK, k[1]), col(HV, v[1]), col(GW, g[1]),
                  pl.BlockSpec((1, sg * Wv, Wk), lambda p: (_chunk_index(n - 1 - p, n, nc, reverse), 0, 0)),
                  col(HV, 0)],
        out_specs=(col(HK, 0), col(HK, 0), col(HV, 0), col(GW, 0)),
        scratch_shapes=[pltpu.VMEM((sg * Wv, Wk), F32)],
        compiler_params=_params(("arbitrary",)),
    )(q[0], k[0], v[0], g[0], st, do)


def _prep_consts():
    r, c = _iota((256, 256), 0), _iota((256, 256), 1)
    first = (c & 63) < 32
    rope_perm = jnp.where(first, -(r == c + 32).astype(F32), (r == c - 32).astype(F32))
    sel_f = (_iota((128, GPAD), 0) == _iota((128, GPAD), 1) + 32).astype(F32)
    sel_b = (_iota((128, GPAD), 0) == _iota((128, GPAD), 1) + 40).astype(F32)
    ek = (_iota((GPAD, 1024), 0) == (_iota((GPAD, 1024), 1) >> 7)).astype(F32)
    return rope_perm, sel_f, sel_b, ek


def _prep_tile(misc, gq, rq, rk, bm, cm, cosE, sinE, Wg, gbias, dtbf, dtbb, nAf, nAb):
    rope_perm, sel_f, sel_b, ek = _prep_consts()
    logg = _log_sigmoid(_mm_bf(misc, Wg) + gbias) * (1.0 / GLA_TAU)
    a_gla = jnp.concatenate([gq * (32 ** -0.5), logg], axis=1)
    rot = lambda t: t * cosE + _sel(t, rope_perm) * sinE
    a_ret = jnp.concatenate([rot(rq * (64 ** -0.5)), rot(rk)], axis=1)
    dtf = _softplus(_sel(misc, sel_f) + dtbf)
    dtb = _softplus(_sel(misc, sel_b) + dtbb)
    rep = lambda t: jnp.concatenate([t[:, :128]] * 4 + [t[:, 128:]] * 4, axis=1)
    bmr = rep(bm)
    return a_gla, a_ret, rep(cm), bmr * _sel(dtf, ek), bmr * _sel(dtb, ek), dtf * nAf, dtb * nAb


def _prep_row_specs(tr):
    blk = lambda w, j: pl.BlockSpec((tr, w), lambda i: (i, j))
    return [blk(128, MISC // 128), blk(128, _ROFF["gla_q"] // 128), blk(256, _ROFF["ret_q"] // 256),
            blk(256, _ROFF["ret_k"] // 256), blk(256, 2), blk(256, 3), blk(256, 0), blk(256, 0)]


def _whole(a):
    return pl.BlockSpec(a.shape, lambda i: (0,) * a.ndim)


def _prep_fwd_call(Pr, u, cosE, sinE, pp, tr):
    T = Pr.shape[0]
    n_row = 8

    def body(*refs):
        outs = _prep_tile(*[r[...] for r in refs[:n_row + len(pp)]])
        for o_ref, o in zip(refs[n_row + len(pp):], outs):
            o_ref[...] = o

    widths = [384, 512, 1024, 1024, 1024, GPAD, GPAD]
    return pl.pallas_call(
        body, name="prep_fwd", grid=(T // tr,),
        out_shape=tuple(jax.ShapeDtypeStruct((T, w), F32) for w in widths),
        in_specs=_prep_row_specs(tr) + [_whole(p) for p in pp],
        out_specs=tuple(pl.BlockSpec((tr, w), lambda i: (i, 0)) for w in widths),
        compiler_params=_params(("parallel",)),
    )(Pr, Pr, Pr, Pr, u, u, cosE, sinE, *pp)


def _prep_bwd_call(Pr, u, cosE, sinE, pp, cts, tr):
    T = Pr.shape[0]
    n_row, n_p = 8, len(pp)
    names = ["gla_dq_f", "gla_dq_b", "gla_dg_f", "gla_dg_b", "gla_dk_f", "gla_dk_b", "gla_dv_f", "gla_dv_b",
             "ret_dq_f", "ret_dq_b", "ret_dk_f", "ret_dk_b", "ret_dv_f", "ret_dv_b",
             "ssd_dq_f", "ssd_dq_b", "ssd_dk_f", "ssd_dk_b", "ssd_dg_f", "ssd_dg_b", "ssd_dv_f", "ssd_dv_b",
             "d_r", "d_z", "d_gr", "d_xs"]
    ct_arrays = [cts[n] for n in names]

    def body(*refs):
        ins = [r[...] for r in refs[:n_row + n_p]]
        c = {n: r[...] for n, r in zip(names, refs[n_row + n_p:n_row + n_p + len(names)])}
        dPr_ref, du_ref = refs[n_row + n_p + len(names):n_row + n_p + len(names) + 2]
        dp_refs = refs[n_row + n_p + len(names) + 2:]
        _, vjp = jax.vjp(_prep_tile, *ins)
        ct_out = (jnp.concatenate([c["gla_dq_f"] + c["gla_dq_b"], c["gla_dg_f"], c["gla_dg_b"]], axis=1),
                  jnp.concatenate([c["ret_dq_f"] + c["ret_dq_b"], c["ret_dk_f"] + c["ret_dk_b"]], axis=1),
                  c["ssd_dq_f"] + c["ssd_dq_b"], c["ssd_dk_f"], c["ssd_dk_b"], c["ssd_dg_f"], c["ssd_dg_b"])
        d = vjp(ct_out)
        d_misc, d_gq, d_rq, d_rk, d_bm, d_cm = d[:6]
        dPr_ref[...] = jnp.concatenate(
            [c["gla_dv_f"] + c["gla_dv_b"], c["d_r"], d_rq, d_rk, c["ret_dv_f"] + c["ret_dv_b"], c["d_gr"],
             c["d_z"], d_gq, c["gla_dk_f"] + c["gla_dk_b"], d_misc], axis=1)
        du_ref[...] = jnp.concatenate([c["ssd_dv_f"] + c["ssd_dv_b"] + c["d_xs"], d_bm, d_cm], axis=1)

        @pl.when(pl.program_id(0) == 0)
        def _():
            for r in dp_refs:
                r[...] = jnp.zeros_like(r)

        for r, g in zip(dp_refs, d[n_row:]):
            r[...] += g

    row = lambda a: pl.BlockSpec((tr, a.shape[1]), lambda i: (i, 0))
    return pl.pallas_call(
        body, name="prep_bwd", grid=(T // tr,),
        out_shape=(jax.ShapeDtypeStruct((T, R_W), F32), jax.ShapeDtypeStruct((T, 1024), F32))
        + tuple(jax.ShapeDtypeStruct(p.shape, F32) for p in pp),
        in_specs=_prep_row_specs(tr) + [_whole(p) for p in pp] + [row(a) for a in ct_arrays],
        out_specs=(pl.BlockSpec((tr, R_W), lambda i: (i, 0)), pl.BlockSpec((tr, 1024), lambda i: (i, 0)))
        + tuple(_whole(p) for p in pp),
        compiler_params=_params(("arbitrary",)),
    )(Pr, Pr, Pr, Pr, u, u, cosE, sinE, *pp, *ct_arrays)


def _post_tile(ogf, ogb, r, ysf, ysb, xs, z, orf, orb, gr, gla_n, dexp, ssd_n, ret_n):
    bd = ((_iota((256, 256), 0) >> 6) == (_iota((256, 256), 1) >> 6)).astype(F32)
    og = ogf + ogb
    gla = og * lax.rsqrt(_sel(og * og, bd) * (1.0 / 64) + RMS_EPS) * gla_n * _silu(r)
    t = (ysf + ysb + dexp * xs) * _silu(z)
    ssd = t * lax.rsqrt(jnp.mean(t * t, axis=-1, keepdims=True) + RMS_EPS) * ssd_n
    o = orf + orb
    oc = o - _sel(o, bd) * (1.0 / 64)
    ret = oc * lax.rsqrt(_sel(oc * oc, bd) * (1.0 / 64) + RMS_EPS) * ret_n * _silu(gr)
    return jnp.concatenate([gla, ssd, ret], axis=1)


def _post_row_specs(tr):
    blk = lambda w, j: pl.BlockSpec((tr, w), lambda i: (i, j))
    return [blk(256, 0), blk(256, 0), blk(256, _ROFF["gla_r"] // 256), blk(512, 0), blk(512, 0), blk(512, 0),
            blk(512, _ROFF["ssd_z"] // 512), blk(256, 0), blk(256, 0), blk(256, _ROFF["ret_g"] // 256)]


def _post_fwd_call(rows, qp, tr):
    T = rows[0].shape[0]

    def body(*refs):
        refs[-1][...] = _post_tile(*[r[...] for r in refs[:-1]])

    return pl.pallas_call(
        body, name="post_fwd", grid=(T // tr,), out_shape=jax.ShapeDtypeStruct((T, D), F32),
        in_specs=_post_row_specs(tr) + [_whole(p) for p in qp],
        out_specs=pl.BlockSpec((tr, D), lambda i: (i, 0)), compiler_params=_params(("parallel",)),
    )(*rows, *qp)


def _post_bwd_call(rows, qp, dmixed, tr):
    T = rows[0].shape[0]
    n_in = 10 + len(qp)

    def body(*refs):
        ins = [r[...] for r in refs[:n_in]]
        _, vjp = jax.vjp(_post_tile, *ins)
        d = vjp(refs[n_in][...])
        outs = refs[n_in + 1:]
        for o_ref, g in zip(outs[:7], (d[0], d[3], d[7], d[2], d[6], d[9], d[5])):
            o_ref[...] = g

        @pl.when(pl.program_id(0) == 0)
        def _():
            for r in outs[7:]:
                r[...] = jnp.zeros_like(r)

        for r, g in zip(outs[7:], d[10:]):
            r[...] += g

    widths = [256, 512, 256, 256, 512, 256, 512]
    return pl.pallas_call(
        body, name="post_bwd", grid=(T // tr,),
        out_shape=tuple(jax.ShapeDtypeStruct((T, w), F32) for w in widths)
        + tuple(jax.ShapeDtypeStruct(p.shape, F32) for p in qp),
        in_specs=_post_row_specs(tr) + [_whole(p) for p in qp] + [pl.BlockSpec((tr, D), lambda i: (i, 0))],
        out_specs=tuple(pl.BlockSpec((tr, w), lambda i: (i, 0)) for w in widths) + tuple(_whole(p) for p in qp),
        compiler_params=_params(("arbitrary",)),
    )(*rows, *qp, dmixed)


def _mixer_scan_operands(Pr, u, a_gla, a_ret, cmr, kf, kb, g8f, g8b, lg):
    gk, gv = (Pr, _ROFF["gla_k"] // 128), (Pr, _ROFF["gla_v"] // 256)
    rv = (Pr, _ROFF["ret_v"] // 256)
    return {
        ("gla", False): ((a_gla, 0), gk, gv, (a_gla, 1)), ("gla", True): ((a_gla, 0), gk, gv, (a_gla, 2)),
        ("ret", False): ((a_ret, 0), (a_ret, 1), rv, (lg, 0)), ("ret", True): ((a_ret, 0), (a_ret, 1), rv, (lg, 0)),
        ("ssd", False): ((cmr, 0), (kf, 0), (u, 0), (g8f, 0)), ("ssd", True): ((cmr, 0), (kb, 0), (u, 0), (g8b, 0)),
    }


def _mixer_forward(Tc, Pr, Px, cosE, sinE, lg, pp, cw8, cb, qp):
    u = _conv_fwd_call(Px, cw8, cb, Tc)
    prep = _prep_fwd_call(Pr, u, cosE, sinE, pp, Tc)
    ops = _mixer_scan_operands(Pr, u, *prep, lg)
    o, st = {}, {}
    for key, (q, k, v, g) in ops.items():
        o[key], st[key] = _scan_fwd_call(key[0], key[1], q, k, v, g, Tc)
    rows = [o["gla", False], o["gla", True], Pr, o["ssd", False], o["ssd", True], u, Pr,
            o["ret", False], o["ret", True], Pr]
    mixed = _post_fwd_call(rows, qp, Tc)
    return mixed, (Pr, Px, cosE, sinE, lg, pp, cw8, cb, qp, u, prep, o, st)


@functools.lru_cache(maxsize=None)
def _make_mixer(Tc):
    @jax.custom_vjp
    def mixer(Pr, Px, cosE, sinE, lg, pp, cw8, cb, qp):
        return _mixer_forward(Tc, Pr, Px, cosE, sinE, lg, pp, cw8, cb, qp)[0]

    def fwd(Pr, Px, cosE, sinE, lg, pp, cw8, cb, qp):
        return _mixer_forward(Tc, Pr, Px, cosE, sinE, lg, pp, cw8, cb, qp)

    def bwd(res, dmixed):
        Pr, Px, cosE, sinE, lg, pp, cw8, cb, qp, u, prep, o, st = res
        rows = [o["gla", False], o["gla", True], Pr, o["ssd", False], o["ssd", True], u, Pr,
                o["ret", False], o["ret", True], Pr]
        post = _post_bwd_call(rows, qp, dmixed, Tc)
        d_o = dict(gla=post[0], ssd=post[1], ret=post[2])
        cts = dict(d_r=post[3], d_z=post[4], d_gr=post[5], d_xs=post[6])
        ops = _mixer_scan_operands(Pr, u, *prep, lg)
        for (kind, rev), (q, k, v, g) in ops.items():
            dq, dk, dv, dg = _scan_bwd_call(kind, rev, kind != "ret", q, k, v, g, st[kind, rev], d_o[kind], Tc)
            sfx = "_b" if rev else "_f"
            cts[kind + "_dq" + sfx], cts[kind + "_dk" + sfx], cts[kind + "_dv" + sfx] = dq, dk, dv
            cts[kind + "_dg" + sfx] = dg
        pb = _prep_bwd_call(Pr, u, cosE, sinE, pp, cts, Tc)
        dPr, du, dpp = pb[0], pb[1], tuple(pb[2:])
        dPx, dcw8, dcb = _conv_bwd_call(Px, cw8, cb, du, Tc)
        return (dPr, dPx, jnp.zeros_like(cosE), jnp.zeros_like(sinE), jnp.zeros_like(lg), dpp, dcw8,
                dcb[0:1], tuple(post[7:]))

    mixer.defvjp(fwd, bwd)
    return mixer


def _rope_tables(Tl, Tc):
    rows = Tl // GRID_W
    row = jnp.repeat(jnp.arange(rows), GRID_W).astype(F32)
    col = jnp.tile(jnp.arange(GRID_W), rows).astype(F32)
    inv_freq = 10000.0 ** (-jnp.arange(16, dtype=F32) / 16)
    ang = jnp.concatenate([row[:, None] * inv_freq, col[:, None] * inv_freq], axis=-1)
    cos = jnp.concatenate([jnp.ones((Tc, 32), F32), jnp.cos(ang)], axis=0)
    sin = jnp.concatenate([jnp.zeros((Tc, 32), F32), jnp.sin(ang)], axis=0)
    return jnp.tile(cos, (1, 8)), jnp.tile(sin, (1, 8))


def _rows8(first, second):
    z = jnp.zeros((6,) + first.shape, F32)
    return jnp.concatenate([first[None], second[None], z], axis=0)


def _local_forward(xcat, mod_l, mod_c, sp, gs, W, Tc):
    Tt = xcat.shape[0]
    cosE, sinE = _rope_tables(Tt - Tc, Tc)
    log_gamma = jnp.log1p(-jnp.exp2(-5.0 - jnp.arange(4, dtype=F32)))
    lg = jnp.broadcast_to(jnp.concatenate([log_gamma, jnp.zeros((GPAD - 4,), F32)])[None, :], (Tt, GPAD))
    norm, norm_res, mixer = _make_norm(Tc, False), _make_norm(Tc, True), _make_mixer(Tc)
    zero = jnp.zeros((8, D), F32)
    X = xcat
    for l in range(DEPTH):
        ml, mc = mod_l[l].reshape(6, D), mod_c[l].reshape(6, D)
        h = norm(X, sp["norm_mix_pre"][l], _rows8(1.0 + mc[1], 1.0 + ml[1]), _rows8(mc[0], ml[0]))
        Px, Pr = linear2(h, W["w_x"][l], W["w_r"][l], gs["w_x"][l], gs["w_r"][l])
        gu = sp["gla_gate_up"][l]
        Wg = jnp.zeros((128, 256), F32).at[0:16, 0:128].set(gu[0]).at[16:32, 128:256].set(gu[1])
        pp = (Wg, sp["gla_gate_b"][l].reshape(1, 256), sp["ssd_dt_bias"][l][0:1], sp["ssd_dt_bias"][l][1:2],
              -jnp.exp(sp["ssd_a_log"][l][0:1]), -jnp.exp(sp["ssd_a_log"][l][1:2]))
        qp = (sp["gla_norm"][l].reshape(1, 256), jnp.repeat(sp["ssd_d"][l], 64).reshape(1, 512),
              sp["ssd_norm"][l].reshape(1, 512), sp["ret_norm"][l].reshape(1, 256))
        cw8 = jnp.pad(sp["ssd_conv_w"][l], ((0, 3), (0, 0)))
        mixed = mixer(Pr, Px, cosE, sinE, lg, pp, cw8, sp["ssd_conv_b"][l].reshape(1, 1024), qp)
        M = linear(mixed, W["w_out"][l], gs["w_out"][l])
        X = norm_res(M, sp["norm_mix_post"][l], _rows8(mc[2], ml[2]), zero, X)
        h = norm(X, sp["norm_ffn_pre"][l], _rows8(1.0 + mc[4], 1.0 + ml[4]), _rows8(mc[3], ml[3]))
        U1, U2 = linear2(h, W["w_1"][l], W["w_3"][l], gs["w_1"][l], gs["w_3"][l])
        Fo = linear(swiglu_act(U1, U2), W["ffn_w2"][l], gs["ffn_w2"][l])
        X = norm_res(Fo, sp["norm_ffn_post"][l], _rows8(mc[5], ml[5]), zero, X)
    return X


def _local_weights(w_in, w_out, w13, w2):
    w_x, w_r = _split_w_in(w_in)
    W = dict(w_x=w_x, w_r=w_r, w_out=w_out, w_1=w13[..., :FFN_H], w_3=w13[..., FFN_H:], ffn_w2=w2)
    return W, {n: jnp.zeros(a.shape, F32) for n, a in W.items()}


def _local_weight_grads(d_gs):
    return (_merge_w_in(d_gs["w_x"], d_gs["w_r"]), d_gs["w_out"],
            jnp.concatenate([d_gs["w_1"], d_gs["w_3"]], axis=-1), d_gs["ffn_w2"])


def _loss_call(X, target, Tc):
    Tt, W = X.shape
    tr = Tc
    nt = Tt // tr

    def body(x_ref, t_ref, loss_ref, dx_ref, acc_ref):
        i = pl.program_id(0)

        @pl.when(i == 0)
        def _():
            acc_ref[...] = jnp.zeros_like(acc_ref)
            dx_ref[...] = jnp.zeros_like(dx_ref)

        @pl.when(i > 0)
        def _():
            e = x_ref[...] - t_ref[...]
            dx_ref[...] = e * (1.0 / W)
            acc_ref[...] += jnp.sum(e * e, axis=0, keepdims=True)

        @pl.when(i == nt - 1)
        def _():
            loss_ref[...] = jnp.full(loss_ref.shape, (0.5 / W) * jnp.sum(acc_ref[...]), F32)

    loss, dx = pl.pallas_call(
        body, name="loss",
        out_shape=(jax.ShapeDtypeStruct((8, 128), F32), jax.ShapeDtypeStruct((Tt, W), F32)),
        grid=(nt,),
        in_specs=[pl.BlockSpec((tr, W), lambda i: (i, 0)),
                  pl.BlockSpec((tr, W), lambda i: (jnp.maximum(i - 1, 0), 0))],
        out_specs=(pl.BlockSpec((8, 128), lambda i: (0, 0)), pl.BlockSpec((tr, W), lambda i: (i, 0))),
        scratch_shapes=[pltpu.VMEM((1, W), F32)],
        compiler_params=_params(("arbitrary",)),
    )(X, target)
    return loss[0, 0], dx


def _adamw_call(w, g, m, v, name):
    R, Cc = w.shape
    tr = _pick(R, (512, 352, 256, 128, 64, 32, 16, 8))
    c1 = 1.0 - ADAM_B1 ** ADAM_STEP
    c2 = 1.0 - ADAM_B2 ** ADAM_STEP

    def body(w_ref, g_ref, m_ref, v_ref, d_ref, nm_ref, nv_ref):
        gv = g_ref[...]
        nm = ADAM_B1 * m_ref[...] + (1.0 - ADAM_B1) * gv
        nv = ADAM_B2 * v_ref[...] + (1.0 - ADAM_B2) * (gv * gv)
        d_ref[...] = -ADAM_LR * ((nm / c1) / (jnp.sqrt(nv / c2) + ADAM_EPS) + ADAM_WD * w_ref[...])
        nm_ref[...] = nm
        nv_ref[...] = nv

    spec = pl.BlockSpec((tr, Cc), lambda i: (i, 0))
    sh = jax.ShapeDtypeStruct((R, Cc), F32)
    return pl.pallas_call(
        body, name=name, out_shape=(sh, sh, sh), grid=(R // tr,),
        in_specs=[spec] * 4, out_specs=(spec,) * 3, compiler_params=_params(("parallel",)),
    )(w, g, m, v)


def _sum_call(xs, name, also_bf16=False):
    R, Cc = xs[0].shape
    tr = _pick(R, (512, 352, 256, 128, 64, 32, 16))
    k = len(xs)

    def body(*refs):
        acc = refs[0][...].astype(F32)
        for r in refs[1:k]:
            acc = acc + r[...].astype(F32)
        refs[k][...] = acc
        if also_bf16:
            refs[k + 1][...] = acc.astype(BF16)

    spec = pl.BlockSpec((tr, Cc), lambda i: (i, 0))
    sh = jax.ShapeDtypeStruct((R, Cc), F32)
    return pl.pallas_call(
        body, name=name, grid=(R // tr,), in_specs=[spec] * k,
        out_shape=(sh, jax.ShapeDtypeStruct((R, Cc), BF16)) if also_bf16 else sh,
        out_specs=(spec, spec) if also_bf16 else spec, compiler_params=_params(("parallel",)),
    )(*xs)


MESH = pl.DeviceIdType.MESH
ANY = pl.BlockSpec(memory_space=pl.ANY)


def _me():
    return lax.axis_index("x"), lax.axis_index("y"), lax.axis_index("c")


def _two_level_gather_body(n_arr, x_refs, out_refs, send_sems, recv_sems, local_sems):
    x, y, c = _me()
    me, sibling = (x, y, c), (x, y, 1 - c)
    chips = [(1 - x, y), (x, 1 - y), (1 - x, 1 - y)]

    def slab(a, px, py, pc):
        return out_refs[a].at[4 * px + 2 * py + pc]

    def copy(a, k, block, to, src=None):
        return pltpu.make_async_remote_copy(
            src_ref=slab(a, *block) if src is None else src, dst_ref=slab(a, *block),
            send_sem=send_sems.at[a, k], recv_sem=recv_sems.at[a, k], device_id=to, device_id_type=MESH)

    mine = [pltpu.make_async_copy(x_refs[a], slab(a, *me), local_sems.at[a]) for a in range(n_arr)]
    for cp in mine:
        cp.start()
    first = []
    for a in range(n_arr):
        first.append(copy(a, 0, me, sibling, src=x_refs[a]))
        first += [copy(a, 1 + j, me, (*chip, c), src=x_refs[a]) for j, chip in enumerate(chips)]
    for cp in first:
        cp.start()
    passed = []
    for j, chip in enumerate(chips):
        for a in range(n_arr):
            copy(a, 1 + j, (*chip, c), me).wait_recv()
            fw = copy(a, 4 + j, (*chip, c), sibling)
            fw.start()
            passed.append(fw)
    for a in range(n_arr):
        copy(a, 0, sibling, me).wait_recv()
        for j, chip in enumerate(chips):
            copy(a, 4 + j, (*chip, 1 - c), me).wait_recv()
    for cp in first + passed:
        cp.wait_send()
    for cp in mine:
        cp.wait()


def _gather_big(xs, name):
    n_arr = len(xs)

    def body(*refs):
        _two_level_gather_body(n_arr, refs[:n_arr], refs[n_arr:2 * n_arr], *refs[2 * n_arr:])

    return pl.pallas_call(
        body, name=name,
        out_shape=tuple(jax.ShapeDtypeStruct((N_DEV,) + a.shape, a.dtype) for a in xs),
        in_specs=[ANY] * n_arr, out_specs=(ANY,) * n_arr,
        scratch_shapes=[pltpu.SemaphoreType.DMA((n_arr, 7)), pltpu.SemaphoreType.DMA((n_arr, 7)),
                        pltpu.SemaphoreType.DMA((n_arr,))],
    )(*xs)


def _gather_small(x, name):
    def body(x_ref, out_ref, send_sems, recv_sems, local_sems):
        _two_level_gather_body(1, [x_ref], [out_ref], send_sems, recv_sems, local_sems)

    vm = pl.BlockSpec(memory_space=pltpu.VMEM)
    return pl.pallas_call(
        body, name=name,
        out_shape=jax.ShapeDtypeStruct((N_DEV,) + x.shape, x.dtype),
        in_specs=[vm], out_specs=vm,
        scratch_shapes=[pltpu.SemaphoreType.DMA((1, 7)), pltpu.SemaphoreType.DMA((1, 7)),
                        pltpu.SemaphoreType.DMA((1,))],
    )(x)


def _exchange_sibling(gs_, name):
    n_arr = len(gs_)

    def body(*refs):
        g_refs, out_refs = refs[:n_arr], refs[n_arr:2 * n_arr]
        send_sems, recv_sems = refs[2 * n_arr:]
        x, y, c = _me()
        cps = []
        for a in range(n_arr):
            for px in range(2):
                for py in range(2):
                    i = 2 * px + py
                    cps.append(pltpu.make_async_remote_copy(
                        src_ref=g_refs[a].at[4 * px + 2 * py + (1 - c)], dst_ref=out_refs[a].at[i],
                        send_sem=send_sems.at[a, i], recv_sem=recv_sems.at[a, i],
                        device_id=(x, y, 1 - c), device_id_type=MESH))
        for cp in cps:
            cp.start()
        for cp in cps:
            cp.wait()

    return pl.pallas_call(
        body, name=name,
        out_shape=tuple(jax.ShapeDtypeStruct((4,) + a.shape[1:], a.dtype) for a in gs_),
        in_specs=[ANY] * n_arr, out_specs=(ANY,) * n_arr,
        scratch_shapes=[pltpu.SemaphoreType.DMA((n_arr, 4)), pltpu.SemaphoreType.DMA((n_arr, 4))],
    )(*gs_)


def _exchange_chips(ps, name):
    n_arr = len(ps)

    def body(*refs):
        p_refs, out_refs = refs[:n_arr], refs[n_arr:2 * n_arr]
        send_sems, recv_sems = refs[2 * n_arr:]
        x, y, c = _me()
        chips = [(1 - x, y), (x, 1 - y), (1 - x, 1 - y)]
        cps = []
        for a in range(n_arr):
            for j, (cx, cy) in enumerate(chips):
                cps.append(pltpu.make_async_remote_copy(
                    src_ref=p_refs[a].at[2 * cx + cy], dst_ref=out_refs[a].at[j],
                    send_sem=send_sems.at[a, j], recv_sem=recv_sems.at[a, j],
                    device_id=(cx, cy, c), device_id_type=MESH))
        for cp in cps:
            cp.start()
        for cp in cps:
            cp.wait()

    return pl.pallas_call(
        body, name=name,
        out_shape=tuple(jax.ShapeDtypeStruct((3,) + a.shape[1:], a.dtype) for a in ps),
        in_specs=[ANY] * n_arr, out_specs=(ANY,) * n_arr,
        scratch_shapes=[pltpu.SemaphoreType.DMA((n_arr, 3)), pltpu.SemaphoreType.DMA((n_arr, 3))],
    )(*ps)


def _reduce_scatter(gs_):
    x, y, c = _me()
    from_sib = _exchange_sibling(gs_, "rs_sibling")
    ps, ps16 = [], []
    for a, g in enumerate(gs_):
        R, Cc = g.shape[1:]
        mine = lax.dynamic_index_in_dim(g.reshape(4, 2, R, Cc), c, axis=1, keepdims=False)
        p32, p16 = _sum_call([mine.reshape(4 * R, Cc), from_sib[a].reshape(4 * R, Cc)], f"rs_add_sib{a}",
                             also_bf16=True)
        ps.append(p32.reshape(4, R, Cc))
        ps16.append(p16.reshape(4, R, Cc))
    from_chips = _exchange_chips(ps16, "rs_chips")
    outs = []
    for a, p in enumerate(ps):
        mine = lax.dynamic_index_in_dim(p, 2 * x + y, axis=0, keepdims=False)
        outs.append(_sum_call([mine, from_chips[a][0], from_chips[a][1], from_chips[a][2]], f"rs_add_chips{a}"))
    return outs


_SMALL = ["norm_mix_pre", "norm_mix_post", "norm_ffn_pre", "norm_ffn_post", "gla_gate_up", "gla_gate_b",
          "gla_norm", "ssd_conv_w", "ssd_conv_b", "ssd_dt_bias", "ssd_a_log", "ssd_d", "ssd_norm", "ret_norm"]


def _pack(arrs):
    flat = jnp.concatenate([a.reshape(-1) for a in arrs])
    n = flat.shape[0]
    npad = -(-n // 1024) * 1024
    return jnp.pad(flat, (0, npad - n)).reshape(npad // 128, 128)


def _unpack(buf, shapes):
    flat = buf.reshape(-1)
    out, o = [], 0
    for s in shapes:
        n = math.prod(s)
        out.append(flat[o:o + n].reshape(s))
        o += n
    return out


def kernel(x, c, ctx, c_ctx, ada_w, ada_b, norm_mix_pre, norm_mix_post, norm_ffn_pre, norm_ffn_post, w_in, w_out, gla_gate_up, gla_gate_b, gla_norm, ssd_conv_w, ssd_conv_b, ssd_dt_bias, ssd_a_log, ssd_d, ssd_norm, ret_norm, ffn_w13, ffn_w2, loss_target, m_c_ctx, m_ada_w, m_ada_b, m_norm_mix_pre, m_norm_mix_post, m_norm_ffn_pre, m_norm_ffn_post, m_w_in, m_w_out, m_gla_gate_up, m_gla_gate_b, m_gla_norm, m_ssd_conv_w, m_ssd_conv_b, m_ssd_dt_bias, m_ssd_a_log, m_ssd_d, m_ssd_norm, m_ret_norm, m_ffn_w13, m_ffn_w2, v_c_ctx, v_ada_w, v_ada_b, v_norm_mix_pre, v_norm_mix_post, v_norm_ffn_pre, v_norm_ffn_post, v_w_in, v_w_out, v_gla_gate_up, v_gla_gate_b, v_gla_norm, v_ssd_conv_w, v_ssd_conv_b, v_ssd_dt_bias, v_ssd_a_log, v_ssd_d, v_ssd_norm, v_ret_norm, v_ffn_w13, v_ffn_w2):
    P_ = dict(c_ctx=c_ctx, ada_w=ada_w, ada_b=ada_b, norm_mix_pre=norm_mix_pre, norm_mix_post=norm_mix_post,
              norm_ffn_pre=norm_ffn_pre, norm_ffn_post=norm_ffn_post, w_in=w_in, w_out=w_out,
              gla_gate_up=gla_gate_up, gla_gate_b=gla_gate_b, gla_norm=gla_norm, ssd_conv_w=ssd_conv_w,
              ssd_conv_b=ssd_conv_b, ssd_dt_bias=ssd_dt_bias, ssd_a_log=ssd_a_log, ssd_d=ssd_d,
              ssd_norm=ssd_norm, ret_norm=ret_norm, ffn_w13=ffn_w13, ffn_w2=ffn_w2)
    M_ = dict(c_ctx=m_c_ctx, ada_w=m_ada_w, ada_b=m_ada_b, norm_mix_pre=m_norm_mix_pre,
              norm_mix_post=m_norm_mix_post, norm_ffn_pre=m_norm_ffn_pre, norm_ffn_post=m_norm_ffn_post,
              w_in=m_w_in, w_out=m_w_out, gla_gate_up=m_gla_gate_up, gla_gate_b=m_gla_gate_b,
              gla_norm=m_gla_norm, ssd_conv_w=m_ssd_conv_w, ssd_conv_b=m_ssd_conv_b, ssd_dt_bias=m_ssd_dt_bias,
              ssd_a_log=m_ssd_a_log, ssd_d=m_ssd_d, ssd_norm=m_ssd_norm, ret_norm=m_ret_norm,
              ffn_w13=m_ffn_w13, ffn_w2=m_ffn_w2)
    V_ = dict(c_ctx=v_c_ctx, ada_w=v_ada_w, ada_b=v_ada_b, norm_mix_pre=v_norm_mix_pre,
              norm_mix_post=v_norm_mix_post, norm_ffn_pre=v_norm_ffn_pre, norm_ffn_post=v_norm_ffn_post,
              w_in=v_w_in, w_out=v_w_out, gla_gate_up=v_gla_gate_up, gla_gate_b=v_gla_gate_b,
              gla_norm=v_gla_norm, ssd_conv_w=v_ssd_conv_w, ssd_conv_b=v_ssd_conv_b, ssd_dt_bias=v_ssd_dt_bias,
              ssd_a_log=v_ssd_a_log, ssd_d=v_ssd_d, ssd_norm=v_ssd_norm, ret_norm=v_ret_norm,
              ffn_w13=v_ffn_w13, ffn_w2=v_ffn_w2)
    order = ["c_ctx", "ada_w", "ada_b", "norm_mix_pre", "norm_mix_post", "norm_ffn_pre", "norm_ffn_post", "w_in",
             "w_out", "gla_gate_up", "gla_gate_b", "gla_norm", "ssd_conv_w", "ssd_conv_b", "ssd_dt_bias",
             "ssd_a_log", "ssd_d", "ssd_norm", "ret_norm", "ffn_w13", "ffn_w2"]

    mx, my, mc_ = _me()
    me = 4 * mx + 2 * my + mc_
    Tl, Tc = x.shape[1], ctx.shape[1]
    n_in, n_out, n_13, n_2 = w_in.shape[2], w_out.shape[1], ffn_w13.shape[2], ffn_w2.shape[1]
    n_ada = ada_w.shape[2]

    shards = [w_in.astype(BF16).reshape(DEPTH * D, n_in), w_out.astype(BF16).reshape(DEPTH * n_out, D),
              ffn_w13.astype(BF16).reshape(DEPTH * D, n_13), ffn_w2.astype(BF16).reshape(DEPTH * n_2, D)]
    g_in, g_out, g_13, g_2 = _gather_big(shards, "gather_weights")
    W, gs = _local_weights(
        jnp.moveaxis(g_in.reshape(N_DEV, DEPTH, D, n_in), 0, 2).reshape(DEPTH, D, N_DEV * n_in),
        jnp.moveaxis(g_out.reshape(N_DEV, DEPTH, n_out, D), 0, 1).reshape(DEPTH, N_DEV * n_out, D),
        jnp.moveaxis(g_13.reshape(N_DEV, DEPTH, D, n_13), 0, 2).reshape(DEPTH, D, N_DEV * n_13),
        jnp.moveaxis(g_2.reshape(N_DEV, DEPTH, n_2, D), 0, 1).reshape(DEPTH, N_DEV * n_2, D))

    cw = ssd_conv_w.shape[2]
    small_in = jnp.concatenate([jnp.pad(c, ((0, 7), (0, 0))).reshape(-1),
                                ssd_conv_w.reshape(-1)]).reshape(-1, 128)
    n_c_rows = 8 * D // 128
    small_in = jnp.pad(small_in, ((0, -small_in.shape[0] % 8), (0, 0)))
    gathered = _gather_small(small_in, "gather_c_conv")
    c_all = gathered[:, :n_c_rows].reshape(N_DEV, 8, D)[:, 0]
    conv_rows = DEPTH * 5 * cw // 128
    conv_full = gathered[:, n_c_rows:n_c_rows + conv_rows].reshape(N_DEV, DEPTH, 5, cw)
    conv_full = jnp.moveaxis(conv_full, 0, 2).reshape(DEPTH, 5, N_DEV * cw)
    c9 = jnp.concatenate([c_all, c_ctx[None], jnp.zeros((7, D), F32)], axis=0)
    s9 = c9 * jax.nn.sigmoid(c9)
    mod_piece = jnp.concatenate([_mm(s9, ada_w[l], name="mm_mod") for l in range(DEPTH)], axis=0)
    mod_g = _gather_small(mod_piece, "gather_mod")
    mod_all = jnp.moveaxis(mod_g.reshape(N_DEV, DEPTH, 16, n_ada), 0, 2).reshape(DEPTH, 16, N_DEV * n_ada)
    mod_all = mod_all + ada_b[:, None, :]
    mod_l = lax.dynamic_index_in_dim(mod_all, me, axis=1, keepdims=False)
    mod_c = mod_all[:, 8]

    sp = {n: P_[n] for n in _SMALL}
    sp["ssd_conv_w"] = conv_full
    xcat = jnp.concatenate([ctx[0], x[0]], axis=0)
    Xf, vjp = jax.vjp(lambda xc, ml, mc, sp_, gs_: _local_forward(xc, ml, mc, sp_, gs_, W, Tc),
                      xcat, mod_l, mod_c, sp, gs)
    loss_local, dX = _loss_call(Xf, loss_target[0], Tc)
    d_xcat, d_mod_l, d_mod_c, d_sp, d_gs = vjp(dX)
    loss = lax.psum(loss_local, ("x", "y", "c"))
    grad_x = d_xcat[Tc:][None]

    def dev_major_cols(g, n):
        K = g.shape[1]
        return jnp.moveaxis(g.reshape(DEPTH, K, N_DEV, n), 2, 0).reshape(N_DEV, DEPTH * K, n)

    def dev_major_rows(g, n):
        return jnp.moveaxis(g.reshape(DEPTH, N_DEV, n, D), 1, 0).reshape(N_DEV, DEPTH * n, D)

    gw_in, gw_out, gw_13, gw_2 = _local_weight_grads(d_gs)
    big = [dev_major_cols(gw_in, n_in), dev_major_rows(gw_out, n_out),
           dev_major_cols(gw_13, n_13), dev_major_rows(gw_2, n_2)]
    r_in, r_out, r_13, r_2 = _reduce_scatter(big)
    G = dict(w_in=r_in.reshape(DEPTH, D, n_in), w_out=r_out.reshape(DEPTH, n_out, D),
             ffn_w13=r_13.reshape(DEPTH, D, n_13), ffn_w2=r_2.reshape(DEPTH, n_2, D))

    dmod_rows = jnp.concatenate([d_mod_l, d_mod_c], axis=0)
    dmod_g = _gather_small(dmod_rows, "gather_dmod").reshape(N_DEV, 2, DEPTH, 6 * D)
    dl = jnp.moveaxis(dmod_g[:, 0], 0, 1)
    dc = dmod_g[:, 1, :, :]
    dc_tot = dc[0]
    for d_ in range(1, N_DEV):
        dc_tot = dc_tot + dc[d_]
    dmod9 = jnp.concatenate([dl, dc_tot[:, None, :], jnp.zeros((DEPTH, 7, 6 * D), F32)], axis=1)
    g_ada_b = dmod9[:, 0]
    for r_ in range(1, 9):
        g_ada_b = g_ada_b + dmod9[:, r_]
    dmod9_mine = lax.dynamic_slice_in_dim(dmod9, me * n_ada, n_ada, axis=2)
    s9T = jnp.pad(s9.T, ((0, 0), (0, 112)))
    g_ada_w = jnp.stack([_mm(s9T, jnp.pad(dmod9_mine[l], ((0, 112), (0, 0))), name="mm_dada")
                         for l in range(DEPTH)])
    ds9 = _mm(dmod9_mine[0], ada_w[0], trans_b=True, name="mm_ds9")
    for l in range(1, DEPTH):
        ds9 = _mm(dmod9_mine[l], ada_w[l], trans_b=True, name="mm_ds9_acc", add=ds9)
    ds_ctx_part = ds9[8]

    small_names = [n for n in _SMALL]
    small_parts = [d_sp[n] for n in small_names] + [ds_ctx_part]
    packed = _pack(small_parts)
    allp = _gather_small(packed, "gather_small_grads")
    summed = _sum_call([allp[d_] for d_ in range(N_DEV)], "sum_small_grads")
    parts = _unpack(summed, [p.shape for p in small_parts])
    for n, p in zip(small_names, parts[:-1]):
        G[n] = p
    sig = jax.nn.sigmoid(c_ctx)
    G["c_ctx"] = parts[-1] * (sig * (1.0 + c_ctx * (1.0 - sig)))
    G["ssd_conv_w"] = lax.dynamic_slice_in_dim(G["ssd_conv_w"], me * cw, cw, axis=2)
    G["ada_w"] = g_ada_w
    G["ada_b"] = g_ada_b

    delta, new_m, new_v = {}, {}, {}
    for n in ["ada_w", "w_in", "w_out", "ffn_w13", "ffn_w2"]:
        sh = P_[n].shape
        f2 = lambda a: a.reshape(sh[0] * sh[1], sh[2])
        d_, m_, v_ = _adamw_call(f2(P_[n]), f2(G[n]), f2(M_[n]), f2(V_[n]), f"adamw_{n}")
        delta[n], new_m[n], new_v[n] = d_.reshape(sh), m_.reshape(sh), v_.reshape(sh)
    rest = [n for n in order if n not in delta]
    shapes = [P_[n].shape for n in rest]
    d_, m_, v_ = _adamw_call(_pack([P_[n] for n in rest]), _pack([G[n] for n in rest]),
                             _pack([M_[n] for n in rest]), _pack([V_[n] for n in rest]), "adamw_small")
    for n, a, b, e in zip(rest, _unpack(d_, shapes), _unpack(m_, shapes), _unpack(v_, shapes)):
        delta[n], new_m[n], new_v[n] = a, b, e

    return (loss, grad_x, *[G[n] for n in order], *[delta[n] for n in order],
            *[new_m[n] for n in order], *[new_v[n] for n in order])
```

```python
import functools
import math

import jax
import jax.numpy as jnp
from jax import lax
from jax.experimental import pallas as pl
from jax.experimental.pallas import tpu as pltpu

F32 = jnp.float32
BF16 = jnp.bfloat16

D = 1024
DEPTH = 4
GRID_W = 64
RMS_EPS = 1e-6
GLA_TAU = 16.0
FFN_H = 2816
IN_COLS = 3376
N_DEV = 8
ADAM_LR, ADAM_B1, ADAM_B2, ADAM_EPS, ADAM_WD, ADAM_STEP = 0.001, 0.9, 0.999, 1e-08, 0.01, 10

VMEM_LIMIT = 48 * 1024 * 1024

_ORIG = dict(gla_q=(0, 128), gla_k=(128, 128), gla_v=(256, 256), gla_r=(512, 256), gla_lr=(768, 32),
             ssd_z=(800, 512), ssd_xbc=(1312, 1024), ssd_dt=(2336, 16), ret_q=(2352, 256), ret_k=(2608, 256),
             ret_v=(2864, 256), ret_g=(3120, 256))
_R_ORDER = ["gla_v", "gla_r", "ret_q", "ret_k", "ret_v", "ret_g", "ssd_z", "gla_q", "gla_k", "gla_lr", "ssd_dt"]
R_W = 2560
_ROFF = {}
_o = 0
for _n in _R_ORDER:
    _ROFF[_n] = _o
    _o += _ORIG[_n][1]
MISC = _ROFF["gla_lr"]
assert MISC == 2304 and _o == 2352


def _split_w_in(w):
    xs, xz = _ORIG["ssd_xbc"]
    parts = [w[..., _ORIG[n][0]:_ORIG[n][0] + _ORIG[n][1]] for n in _R_ORDER]
    parts.append(jnp.zeros(w.shape[:-1] + (R_W - _o,), w.dtype))
    return w[..., xs:xs + xz], jnp.concatenate(parts, axis=-1)


def _merge_w_in(wx, wr):
    pieces = []
    for n, (s, z) in sorted(_ORIG.items(), key=lambda t: t[1][0]):
        pieces.append(wx if n == "ssd_xbc" else wr[..., _ROFF[n]:_ROFF[n] + z])
    return jnp.concatenate(pieces, axis=-1)


def _pick(n, cands):
    for c in cands:
        if n % c == 0:
            return c
    return n


def _params(sem=None):
    kw = dict(vmem_limit_bytes=VMEM_LIMIT)
    if sem is not None:
        kw["dimension_semantics"] = sem
    return pltpu.CompilerParams(**kw)


def _iota(shape, dim):
    return lax.broadcasted_iota(jnp.int32, shape, dim)


def _dot(a, b, dims):
    return lax.dot_general(a, b, (dims, ((), ())), preferred_element_type=F32)


_NN = ((1,), (0,))
_NT = ((1,), (1,))
_TN = ((0,), (0,))


def _bf(x):
    return x.astype(BF16)


def _dot_sel(x, e, dims, x_left=True):
    eb = e.astype(BF16)
    hi = x.astype(BF16)
    r1 = x - hi.astype(F32)
    mid = r1.astype(BF16)
    lo = (r1 - mid.astype(F32)).astype(BF16)
    out = None
    for p in (hi, mid, lo):
        t = _dot(p, eb, dims) if x_left else _dot(eb, p, dims)
        out = t if out is None else out + t
    return out


@jax.custom_vjp
def _sel(x, e):
    return _dot_sel(x, e, _NN)


_sel.defvjp(lambda x, e: (_dot_sel(x, e, _NN), e), lambda e, g: (_dot_sel(g, e, _NT), jnp.zeros_like(e)))


def _sig(x):
    e = jnp.exp(-jnp.abs(x))
    return jnp.where(x >= 0, 1.0 / (1.0 + e), e / (1.0 + e))


@jax.custom_vjp
def _sigmoid(x):
    return _sig(x)


def _sigmoid_fwd(x):
    s = _sig(x)
    return s, s


_sigmoid.defvjp(_sigmoid_fwd, lambda s, g: (g * s * (1.0 - s),))


def _silu(x):
    return x * _sigmoid(x)


@jax.custom_vjp
def _softplus(x):
    return jnp.maximum(x, 0.0) + jnp.log(1.0 + jnp.exp(-jnp.abs(x)))


_softplus.defvjp(lambda x: (jnp.maximum(x, 0.0) + jnp.log(1.0 + jnp.exp(-jnp.abs(x))), x),
                 lambda x, g: (g * _sig(x),))


def _log_sigmoid(x):
    return -_softplus(-x)


@jax.custom_vjp
def _mm_bf(x, w):
    return _dot(_bf(x), _bf(w), _NN)


_mm_bf.defvjp(lambda x, w: (_dot(_bf(x), _bf(w), _NN), (x, w)),
              lambda r, g: (_dot(_bf(g), _bf(r[1]), _NT), _dot(_bf(r[0]), _bf(g), _TN)))


_TILE_M = (1088, 1024, 512, 256, 128, 64, 32, 16)
_TILE_N = (1408, 1280, 1024, 768, 512, 384, 256, 128)
_TILE_K = (1408, 1280, 1024, 768, 512, 384, 256, 128)


def _mm(a, b, *, trans_b=False, name, add=None, out_dtype=F32):
    M, K = a.shape
    N = b.shape[0] if trans_b else b.shape[1]
    assert (b.shape[1] if trans_b else b.shape[0]) == K
    tm, tn, tk = _pick(M, _TILE_M), _pick(N, _TILE_N), _pick(K, _TILE_K)
    nk = K // tk
    dims = _NT if trans_b else _NN
    has_add = add is not None

    def body(*refs):
        a_ref, b_ref = refs[0], refs[1]
        o_ref, acc_ref = refs[-2], refs[-1]
        k = pl.program_id(2)

        @pl.when(k == 0)
        def _():
            acc_ref[...] = refs[2][...] if has_add else jnp.zeros_like(acc_ref)

        acc_ref[...] += _dot(a_ref[...].astype(BF16), b_ref[...].astype(BF16), dims)

        @pl.when(k == nk - 1)
        def _():
            o_ref[...] = acc_ref[...].astype(o_ref.dtype)

    b_spec = (pl.BlockSpec((tn, tk), lambda i, j, k: (j, k)) if trans_b
              else pl.BlockSpec((tk, tn), lambda i, j, k: (k, j)))
    o_spec = pl.BlockSpec((tm, tn), lambda i, j, k: (i, j))
    return pl.pallas_call(
        body, name=name,
        out_shape=jax.ShapeDtypeStruct((M, N), out_dtype),
        grid=(M // tm, N // tn, nk),
        in_specs=[pl.BlockSpec((tm, tk), lambda i, j, k: (i, k)), b_spec] + ([o_spec] if has_add else []),
        out_specs=o_spec,
        scratch_shapes=[pltpu.VMEM((tm, tn), F32)],
        compiler_params=_params(("parallel", "parallel", "arbitrary")),
    )(*((a, b, add) if has_add else (a, b)))


def _mm_tn(a, g, *, name):
    M, K = a.shape
    N = g.shape[1]
    tm, tk, tn = _pick(M, _TILE_M), _pick(K, _TILE_K), _pick(N, _TILE_N)
    nm = M // tm

    def body(a_ref, g_ref, o_ref, acc_ref):
        i = pl.program_id(2)

        @pl.when(i == 0)
        def _():
            acc_ref[...] = jnp.zeros_like(acc_ref)

        acc_ref[...] += _dot(a_ref[...].astype(BF16), g_ref[...].astype(BF16), _TN)

        @pl.when(i == nm - 1)
        def _():
            o_ref[...] = acc_ref[...]

    return pl.pallas_call(
        body, name=name,
        out_shape=jax.ShapeDtypeStruct((K, N), F32),
        grid=(K // tk, N // tn, nm),
        in_specs=[pl.BlockSpec((tm, tk), lambda k, j, i: (i, k)), pl.BlockSpec((tm, tn), lambda k, j, i: (i, j))],
        out_specs=pl.BlockSpec((tk, tn), lambda k, j, i: (k, j)),
        scratch_shapes=[pltpu.VMEM((tk, tn), F32)],
        compiler_params=_params(("parallel", "parallel", "arbitrary")),
    )(a, g)


def _norm_fwd_call(x, w, a2, b2, res, tr, out_dtype=F32):
    T, W = x.shape
    has_res = res is not None

    def body(*refs):
        x_ref, w_ref, a_ref, b_ref = refs[:4]
        y_ref = refs[-1]
        seg = jnp.minimum(pl.program_id(0), 1)
        xv = x_ref[...]
        rstd = lax.rsqrt(jnp.mean(xv * xv, axis=-1, keepdims=True) + RMS_EPS)
        y = a_ref[pl.ds(seg, 1), :] * (xv * rstd * w_ref[...]) + b_ref[pl.ds(seg, 1), :]
        y_ref[...] = (y + refs[4][...] if has_res else y).astype(y_ref.dtype)

    row = pl.BlockSpec((tr, W), lambda i: (i, 0))
    small = pl.BlockSpec((8, W), lambda i: (0, 0))
    return pl.pallas_call(
        body, name="norm_fwd",
        out_shape=jax.ShapeDtypeStruct((T, W), out_dtype),
        grid=(T // tr,),
        in_specs=[row, pl.BlockSpec((1, W), lambda i: (0, 0)), small, small] + ([row] if has_res else []),
        out_specs=row,
        compiler_params=_params(("parallel",)),
    )(*((x, w.reshape(1, W), a2, b2) + ((res,) if has_res else ())))


def _norm_bwd_call(x, w, a2, dy, tr, add=None, out_dtype=F32):
    T, W = x.shape
    has_add = add is not None

    def body(*refs):
        x_ref, w_ref, a_ref, dy_ref = refs[:4]
        dx_ref, dw_ref, da_ref, db_ref = refs[-4:]
        i = pl.program_id(0)
        seg = jnp.minimum(i, 1)

        @pl.when(i == 0)
        def _():
            dw_ref[...] = jnp.zeros_like(dw_ref)
            da_ref[...] = jnp.zeros_like(da_ref)
            db_ref[...] = jnp.zeros_like(db_ref)

        xv = x_ref[...]
        g = dy_ref[...]
        wv = w_ref[...]
        rstd = lax.rsqrt(jnp.mean(xv * xv, axis=-1, keepdims=True) + RMS_EPS)
        xh = xv * rstd
        da_ref[pl.ds(seg, 1), :] += jnp.sum(g * (xh * wv), axis=0, keepdims=True)
        db_ref[pl.ds(seg, 1), :] += jnp.sum(g, axis=0, keepdims=True)
        gy = g * a_ref[pl.ds(seg, 1), :]
        dw_ref[0:1, :] += jnp.sum(gy * xh, axis=0, keepdims=True)
        gx = gy * wv
        dx = rstd * (gx - xh * jnp.mean(gx * xh, axis=-1, keepdims=True))
        dx_ref[...] = (dx + refs[4][...] if has_add else dx).astype(dx_ref.dtype)

    acc = jax.ShapeDtypeStruct((8, W), F32)
    acc_spec = pl.BlockSpec((8, W), lambda i: (0, 0))
    row = pl.BlockSpec((tr, W), lambda i: (i, 0))
    return pl.pallas_call(
        body, name="norm_bwd",
        out_shape=(jax.ShapeDtypeStruct((T, W), out_dtype), acc, acc, acc),
        grid=(T // tr,),
        in_specs=[row, pl.BlockSpec((1, W), lambda i: (0, 0)), acc_spec, row] + ([row] if has_add else []),
        out_specs=(row, acc_spec, acc_spec, acc_spec),
        compiler_params=_params(("arbitrary",)),
    )(*((x, w.reshape(1, W), a2, dy) + ((add,) if has_add else ())))


def _act_call(u1, u2, dact=None):
    T, W = u1.shape
    tr = _pick(T, (512, 256, 128, 64))
    tn = _pick(W, (1408, 512, 256, 128))
    spec = pl.BlockSpec((tr, tn), lambda i, j: (i, j))
    sh = jax.ShapeDtypeStruct((T, W), BF16)
    if dact is None:
        def body(a_ref, b_ref, o_ref):
            a = a_ref[...]
            o_ref[...] = (a * _sig(a) * b_ref[...]).astype(o_ref.dtype)

        return pl.pallas_call(body, name="act_fwd", out_shape=sh, grid=(T // tr, W // tn), in_specs=[spec, spec],
                              out_specs=spec, compiler_params=_params(("parallel", "parallel")))(u1, u2)

    def body(a_ref, b_ref, g_ref, da_ref, db_ref):
        a, g = a_ref[...], g_ref[...]
        s = _sig(a)
        da_ref[...] = (g * b_ref[...] * (s * (1.0 + a * (1.0 - s)))).astype(da_ref.dtype)
        db_ref[...] = (g * a * s).astype(db_ref.dtype)

    return pl.pallas_call(body, name="act_bwd", out_shape=(sh, sh), grid=(T // tr, W // tn),
                          in_specs=[spec, spec, spec], out_specs=(spec, spec),
                          compiler_params=_params(("parallel", "parallel")))(u1, u2, dact)


def _conv_specs(T, Wc, tr):
    hb, nt = tr // 8, T // tr
    row = pl.BlockSpec((tr, Wc), lambda i: (i, 0))
    prev = pl.BlockSpec((8, Wc), lambda i: (jnp.maximum(i * hb - 1, 0), 0))
    nxt = pl.BlockSpec((8, Wc), lambda i: (jnp.minimum((i + 1) * hb, T // 8 - 1), 0))
    return row, prev, nxt, nt


def _fill_ext(dst_ref, cur_ref, prev_ref, next_ref, i, nt, tr):
    has_prev = (i > 1).astype(F32)
    has_next = jnp.logical_and(i > 0, i < nt - 1).astype(F32)
    dst_ref[8:16, :] = prev_ref[...] * has_prev
    dst_ref[16:16 + tr, :] = cur_ref[...]
    dst_ref[16 + tr:24 + tr, :] = next_ref[...] * has_next


def _conv_fwd_call(px, w8, b, tr):
    T, Wc = px.shape
    row, prev, nxt, nt = _conv_specs(T, Wc, tr)

    def body(x_ref, xp_ref, xn_ref, w_ref, b_ref, u_ref, xe_ref):
        i = pl.program_id(0)

        @pl.when(i == 0)
        def _():
            xe_ref[...] = jnp.zeros_like(xe_ref)

        _fill_ext(xe_ref, x_ref, xp_ref, xn_ref, i, nt, tr)
        y = b_ref[...] + w_ref[0:1, :] * xe_ref[pl.ds(14, tr), :]
        for k in range(1, 5):
            y = y + w_ref[k:k + 1, :] * xe_ref[pl.ds(14 + k, tr), :]
        u_ref[...] = y * _sig(y)

    return pl.pallas_call(
        body, name="conv_fwd", out_shape=jax.ShapeDtypeStruct((T, Wc), F32), grid=(nt,),
        in_specs=[row, prev, nxt, pl.BlockSpec((8, Wc), lambda i: (0, 0)), pl.BlockSpec((1, Wc), lambda i: (0, 0))],
        out_specs=row, scratch_shapes=[pltpu.VMEM((tr + 32, Wc), F32)],
        compiler_params=_params(("arbitrary",)),
    )(px, px, px, w8, b)


def _conv_bwd_call(px, w8, b, du, tr):
    T, Wc = px.shape
    row, prev, nxt, nt = _conv_specs(T, Wc, tr)
    E = tr + 16

    def body(x_ref, xp_ref, xn_ref, g_ref, gp_ref, gn_ref, w_ref, b_ref, dx_ref, dw_ref, db_ref,
             xe_ref, ge_ref, dy_ref):
        i = pl.program_id(0)

        @pl.when(i == 0)
        def _():
            xe_ref[...] = jnp.zeros_like(xe_ref)
            ge_ref[...] = jnp.zeros_like(ge_ref)
            dy_ref[...] = jnp.zeros_like(dy_ref)
            dw_ref[...] = jnp.zeros_like(dw_ref)
            db_ref[...] = jnp.zeros_like(db_ref)

        _fill_ext(xe_ref, x_ref, xp_ref, xn_ref, i, nt, tr)
        _fill_ext(ge_ref, g_ref, gp_ref, gn_ref, i, nt, tr)
        y = b_ref[...] + w_ref[0:1, :] * xe_ref[pl.ds(6, E), :]
        for k in range(1, 5):
            y = y + w_ref[k:k + 1, :] * xe_ref[pl.ds(6 + k, E), :]
        s = _sig(y)
        dy = ge_ref[pl.ds(8, E), :] * (s * (1.0 + y * (1.0 - s)))
        dy_ref[pl.ds(8, E), :] = dy
        dx = w_ref[0:1, :] * dy_ref[pl.ds(18, tr), :]
        for k in range(1, 5):
            dx = dx + w_ref[k:k + 1, :] * dy_ref[pl.ds(18 - k, tr), :]
        dx_ref[...] = dx.astype(dx_ref.dtype)
        dyt = dy_ref[pl.ds(16, tr), :]
        db_ref[0:1, :] += jnp.sum(dyt, axis=0, keepdims=True)
        for k in range(5):
            dw_ref[k:k + 1, :] += jnp.sum(dyt * xe_ref[pl.ds(14 + k, tr), :], axis=0, keepdims=True)

    acc = jax.ShapeDtypeStruct((8, Wc), F32)
    acc_spec = pl.BlockSpec((8, Wc), lambda i: (0, 0))
    ext = pltpu.VMEM((tr + 32, Wc), F32)
    return pl.pallas_call(
        body, name="conv_bwd", out_shape=(jax.ShapeDtypeStruct((T, Wc), BF16), acc, acc), grid=(nt,),
        in_specs=[row, prev, nxt, row, prev, nxt, acc_spec, pl.BlockSpec((1, Wc), lambda i: (0, 0))],
        out_specs=(row, acc_spec, acc_spec), scratch_shapes=[ext, ext, ext],
        compiler_params=_params(("arbitrary",)),
    )(px, px, px, du, du, du, w8, b)


_SCAN_CFG = {
    "gla": dict(H=4, Dk=32, Dv=64, nh=4, scalar=False, C=64),
    "ssd": dict(H=8, Dk=128, Dv=64, nh=2, scalar=True, C=128),
    "ret": dict(H=4, Dk=64, Dv=64, nh=4, scalar=True, C=128),
}
GPAD = 8


def _log2(n):
    r = int(math.log2(n))
    assert 1 << r == n
    return r


class _ScanMath:
    def __init__(self, cfg, reverse):
        C = cfg["C"]
        self.C, self.reverse = C, reverse
        self.Dk, self.Dv, self.nh, self.scalar = cfg["Dk"], cfg["Dv"], cfg["nh"], cfg["scalar"]
        self.Wk, self.Wv = self.nh * self.Dk, self.nh * self.Dv
        self.nsg = cfg["H"] // self.nh
        nh, Wk, Wv = self.nh, self.Wk, self.Wv
        lk, lv, lc = _log2(self.Dk), _log2(self.Dv), _log2(C)
        r, c = _iota((C, C), 0), _iota((C, C), 1)
        self.L = ((c >= r) if reverse else (c <= r)).astype(F32)
        self.Lsuf = ((c <= r) if reverse else (c >= r)).astype(F32)
        i, j = _iota((C, nh * C), 0), _iota((C, nh * C), 1) & (C - 1)
        self.Mst = (j >= i) if reverse else (j <= i)
        self.Dj = (i == j).astype(F32)
        self.km = [((_iota((1, Wk), 1) >> lk) == h).astype(F32) for h in range(nh)]
        self.vm = [((_iota((1, Wv), 1) >> lv) == h).astype(F32) for h in range(nh)]
        self.BD = ((_iota((Wv, Wk), 0) >> lv) == (_iota((Wv, Wk), 1) >> lk)).astype(F32)
        self.last = 0 if reverse else C - 1
        self.last_row = (_iota((C, 1), 0) == self.last).astype(F32)
        self.lk, self.lc = lk, lc
        self.H = cfg["H"]

    def gates(self, g):
        if not self.scalar:
            return _dot_sel(g, self.L, _NN, x_left=False), None
        G8 = _dot_sel(g, self.L, _NN, x_left=False)
        nk, ncol = self.H * self.Dk, self.H * self.C
        ek = (_iota((GPAD, nk), 0) == (_iota((GPAD, nk), 1) >> self.lk)).astype(F32)
        ec = (_iota((GPAD, ncol), 0) == (_iota((GPAD, ncol), 1) >> self.lc)).astype(F32)
        return _dot_sel(G8, ek, _NN), _dot_sel(G8, ec, _NN)

    def Ek(self, s):
        return (_iota((GPAD, self.Wk), 0) == (_iota((GPAD, self.Wk), 1) >> self.lk) + s * self.nh).astype(F32)

    def kstack(self, x):
        return jnp.concatenate([x * self.km[h] for h in range(self.nh)], axis=0)

    def vstack(self, x):
        return jnp.concatenate([x * self.vm[h] for h in range(self.nh)], axis=0)

    def unstack(self, R, masks):
        C = self.C
        out = R[0:C] * masks[0]
        for h in range(1, self.nh):
            out = out + R[h * C:(h + 1) * C] * masks[h]
        return out

    def chunk(self, qs, ks, Gk, Gc):
        C = self.C
        Glast = Gk[self.last:self.last + 1, :]
        out = dict(Gk=Gk, Glast=Glast, eG=jnp.exp(Gk), eGl=jnp.exp(Glast - Gk), eGlast=jnp.exp(Glast))
        if self.scalar:
            Gr = jnp.sum(Gc * self.Dj, axis=0, keepdims=True)
            dec = jnp.where(self.Mst, jnp.exp(jnp.minimum(Gc - Gr, 0.0)), 0.0)
            qt, kt = qs, ks
            A = _dot(_bf(qt), _bf(self.kstack(kt)), _NT) * dec
            out.update(dec=dec, qt=qt, kt=kt, A=A)
        else:
            Gm = Gk[C // 2:C // 2 + 1, :]
            eq, ek = jnp.exp(Gk - Gm), jnp.exp(Gm - Gk)
            qt, kt = qs * eq, ks * ek
            A = jnp.where(self.Mst, _dot(_bf(qt), _bf(self.kstack(kt)), _NT), 0.0)
            out.update(eq=eq, ek=ek, qt=qt, kt=kt, A=A)
        return out


def _chunk_index(p, n, nc, reverse):
    if not reverse:
        return p
    return jnp.where(p < nc, nc - 1 - p, n - 1 + nc - p)


def _scan_dims(kind):
    cfg = _SCAN_CFG[kind]
    HK, HV = cfg["H"] * cfg["Dk"], cfg["H"] * cfg["Dv"]
    return cfg, cfg["C"], HK, HV, (GPAD if cfg["scalar"] else HK)


def _scan_fwd_step(m, q_ref, k_ref, v_ref, g_ref, o_ref, st_ref, S_ref):
    C = m.C

    @pl.when(pl.program_id(0) == 0)
    def _():
        S_ref[...] = jnp.zeros_like(S_ref)

    Gk_all, Gc_all = m.gates(g_ref[...])
    for s in range(m.nsg):
        ksl, vsl = slice(s * m.Wk, (s + 1) * m.Wk), slice(s * m.Wv, (s + 1) * m.Wv)
        csl = slice(s * m.nh * C, (s + 1) * m.nh * C)
        qs, ks, vs = q_ref[:, ksl], k_ref[:, ksl], v_ref[:, vsl]
        ch = m.chunk(qs, ks, Gk_all[:, ksl], Gc_all[:, csl] if m.scalar else None)
        S = S_ref[vsl, :]
        o = _dot(_bf(ch["A"]), _bf(m.vstack(vs)), _NN) + _dot(_bf(qs * ch["eG"]), _bf(S), _NT)
        o_ref[:, vsl] = o
        st_ref[0, vsl, :] = S
        S_ref[vsl, :] = S * ch["eGlast"] + _dot(_bf(vs), _bf(ks * ch["eGl"]), _TN) * m.BD


def _scan_fwd_call(kind, ops, Tc):
    cfg, C, HK, HV, GW = _scan_dims(kind)
    T = ops[False][0][0].shape[0]
    n, nc = T // C, Tc // C

    def body(*refs):
        for d, rev in enumerate((False, True)):
            _scan_fwd_step(_ScanMath(cfg, rev), *refs[4 * d:4 * d + 4], *refs[8 + 2 * d:10 + 2 * d], refs[12 + d])

    sg = cfg["H"] // cfg["nh"]
    Wk, Wv = cfg["nh"] * cfg["Dk"], cfg["nh"] * cfg["Dv"]
    col = lambda rev, w, j: pl.BlockSpec((C, w), lambda p: (_chunk_index(p, n, nc, rev), j))
    st_spec = lambda rev: pl.BlockSpec((1, sg * Wv, Wk), lambda p: (_chunk_index(p, n, nc, rev), 0, 0))
    in_specs, args, out_specs, out_shape = [], [], [], []
    for rev in (False, True):
        q, k, v, g = ops[rev]
        in_specs += [col(rev, HK, q[1]), col(rev, HK, k[1]), col(rev, HV, v[1]), col(rev, GW, g[1])]
        args += [q[0], k[0], v[0], g[0]]
        out_specs += [col(rev, HV, 0), st_spec(rev)]
        out_shape += [jax.ShapeDtypeStruct((T, HV), F32), jax.ShapeDtypeStruct((n, sg * Wv, Wk), F32)]
    res = pl.pallas_call(
        body, name=f"scan_fwd_{kind}", out_shape=tuple(out_shape), grid=(n,),
        in_specs=in_specs, out_specs=tuple(out_specs),
        scratch_shapes=[pltpu.VMEM((sg * Wv, Wk), F32)] * 2,
        compiler_params=_params(("arbitrary",)),
    )(*args)
    return {False: (res[0], res[1]), True: (res[2], res[3])}


def _scan_bwd_step(m, need_dg, q_ref, k_ref, v_ref, g_ref, st_ref, do_ref, dq_ref, dk_ref, dv_ref, dg_ref, dS_ref):
    C = m.C

    @pl.when(pl.program_id(0) == 0)
    def _():
        dS_ref[...] = jnp.zeros_like(dS_ref)

    x8 = jnp.zeros((C, GPAD), F32)
    Gk_all, Gc_all = m.gates(g_ref[...])
    for s in range(m.nsg):
        ksl, vsl = slice(s * m.Wk, (s + 1) * m.Wk), slice(s * m.Wv, (s + 1) * m.Wv)
        csl = slice(s * m.nh * C, (s + 1) * m.nh * C)
        qs, ks, vs, dos = q_ref[:, ksl], k_ref[:, ksl], v_ref[:, vsl], do_ref[:, vsl]
        ch = m.chunk(qs, ks, Gk_all[:, ksl], Gc_all[:, csl] if m.scalar else None)
        S = st_ref[0, vsl, :]
        dS = dS_ref[vsl, :]
        A, qt, kt = ch["A"], ch["qt"], ch["kt"]
        dA = _dot(_bf(dos), _bf(m.vstack(vs)), _NT)
        dAm = dA * ch["dec"] if m.scalar else jnp.where(m.Mst, dA, 0.0)
        kst = _bf(m.kstack(kt))
        dv = m.unstack(_dot(_bf(A), _bf(dos), _TN), m.vm) + _dot(_bf(ks * ch["eGl"]), _bf(dS), _NT)
        dv_ref[:, vsl] = dv
        dq_i = _dot(_bf(dAm), kst, _NN)
        dq_x = ch["eG"] * _dot(_bf(dos), _bf(S), _NN)
        dq_ref[:, ksl] = (dq_i if m.scalar else dq_i * ch["eq"]) + dq_x
        dk_i = m.unstack(_dot(_bf(dAm), _bf(qt), _TN), m.km)
        dk_x = ch["eGl"] * _dot(_bf(vs), _bf(dS), _NN)
        dk_ref[:, ksl] = (dk_i if m.scalar else dk_i * ch["ek"]) + dk_x
        if need_dg:
            bnd = (ch["eGlast"] * jnp.sum(dS * S, axis=0, keepdims=True)
                   + jnp.sum(ks * dk_x, axis=0, keepdims=True))
            X = (_bf(qt).astype(F32) * dq_i - _bf(kt).astype(F32) * dk_i) + (qs * dq_x - ks * dk_x)
            X = X + m.last_row * bnd
            if m.scalar:
                x8 = x8 + _dot_sel(X, m.Ek(s), _NT)
            else:
                dg_ref[:, ksl] = _dot_sel(X, m.Lsuf, _NN, x_left=False)
        dS_ref[vsl, :] = dS * ch["eGlast"] + _dot(_bf(dos), _bf(qs * ch["eG"]), _TN) * m.BD
    if m.scalar:
        dg_ref[...] = _dot_sel(x8, m.Lsuf, _NN, x_left=False)
    elif not need_dg:
        dg_ref[...] = jnp.zeros_like(dg_ref)


def _scan_bwd_call(kind, need_dg, ops, st, do, Tc):
    cfg, C, HK, HV, GW = _scan_dims(kind)
    T = ops[False][0][0].shape[0]
    n, nc = T // C, Tc // C

    def body(*refs):
        for d, rev in enumerate((False, True)):
            _scan_bwd_step(_ScanMath(cfg, rev), need_dg, *refs[6 * d:6 * d + 6], *refs[12 + 4 * d:16 + 4 * d],
                           refs[20 + d])

    sg = cfg["H"] // cfg["nh"]
    Wk, Wv = cfg["nh"] * cfg["Dk"], cfg["nh"] * cfg["Dv"]
    col = lambda rev, w, j: pl.BlockSpec((C, w), lambda p: (_chunk_index(n - 1 - p, n, nc, rev), j))
    st_spec = lambda rev: pl.BlockSpec((1, sg * Wv, Wk), lambda p: (_chunk_index(n - 1 - p, n, nc, rev), 0, 0))
    in_specs, args, out_specs, out_shape = [], [], [], []
    for rev in (False, True):
        q, k, v, g = ops[rev]
        in_specs += [col(rev, HK, q[1]), col(rev, HK, k[1]), col(rev, HV, v[1]), col(rev, GW, g[1]),
                     st_spec(rev), col(rev, HV, 0)]
        args += [q[0], k[0], v[0], g[0], st[rev], do]
        out_specs += [col(rev, HK, 0), col(rev, HK, 0), col(rev, HV, 0), col(rev, GW, 0)]
        out_shape += [jax.ShapeDtypeStruct((T, w), F32) for w in (HK, HK, HV, GW)]
    res = pl.pallas_call(
        body, name=f"scan_bwd_{kind}", out_shape=tuple(out_shape), grid=(n,),
        in_specs=in_specs, out_specs=tuple(out_specs),
        scratch_shapes=[pltpu.VMEM((sg * Wv, Wk), F32)] * 2,
        compiler_params=_params(("arbitrary",)),
    )(*args)
    return {False: res[0:4], True: res[4:8]}


def _prep_consts():
    r, c = _iota((256, 256), 0), _iota((256, 256), 1)
    first = (c & 63) < 32
    rope_perm = jnp.where(first, -(r == c + 32).astype(F32), (r == c - 32).astype(F32))
    sel_f = (_iota((128, GPAD), 0) == _iota((128, GPAD), 1) + 32).astype(F32)
    sel_b = (_iota((128, GPAD), 0) == _iota((128, GPAD), 1) + 40).astype(F32)
    ek = (_iota((GPAD, 1024), 0) == (_iota((GPAD, 1024), 1) >> 7)).astype(F32)
    return rope_perm, sel_f, sel_b, ek


def _prep_tile(misc, gq, rq, rk, bm, cm, cosE, sinE, Wg, gbias, dtbf, dtbb, nAf, nAb):
    rope_perm, sel_f, sel_b, ek = _prep_consts()
    logg = _log_sigmoid(_mm_bf(misc, Wg) + gbias) * (1.0 / GLA_TAU)
    a_gla = jnp.concatenate([gq * (32 ** -0.5), logg], axis=1)
    rot = lambda t: t * cosE + _sel(t, rope_perm) * sinE
    a_ret = jnp.concatenate([rot(rq * (64 ** -0.5)), rot(rk)], axis=1)
    dtf = _softplus(_sel(misc, sel_f) + dtbf)
    dtb = _softplus(_sel(misc, sel_b) + dtbb)
    rep = lambda t: jnp.concatenate([t[:, :128]] * 4 + [t[:, 128:]] * 4, axis=1)
    bmr = rep(bm)
    return a_gla, a_ret, rep(cm), bmr * _sel(dtf, ek), bmr * _sel(dtb, ek), dtf * nAf, dtb * nAb


def _prep_row_specs(tr):
    blk = lambda w, j: pl.BlockSpec((tr, w), lambda i: (i, j))
    return [blk(128, MISC // 128), blk(128, _ROFF["gla_q"] // 128), blk(256, _ROFF["ret_q"] // 256),
            blk(256, _ROFF["ret_k"] // 256), blk(256, 2), blk(256, 3), blk(256, 0), blk(256, 0)]


def _whole(a):
    return pl.BlockSpec(a.shape, lambda i: (0,) * a.ndim)


def _prep_fwd_call(Pr, u, cosE, sinE, pp, tr):
    T = Pr.shape[0]
    n_row = 8

    def body(*refs):
        outs = _prep_tile(*[r[...] for r in refs[:n_row + len(pp)]])
        for o_ref, o in zip(refs[n_row + len(pp):], outs):
            o_ref[...] = o

    widths = [384, 512, 1024, 1024, 1024, GPAD, GPAD]
    return pl.pallas_call(
        body, name="prep_fwd", grid=(T // tr,),
        out_shape=tuple(jax.ShapeDtypeStruct((T, w), F32) for w in widths),
        in_specs=_prep_row_specs(tr) + [_whole(p) for p in pp],
        out_specs=tuple(pl.BlockSpec((tr, w), lambda i: (i, 0)) for w in widths),
        compiler_params=_params(("parallel",)),
    )(Pr, Pr, Pr, Pr, u, u, cosE, sinE, *pp)


def _prep_bwd_call(Pr, u, cosE, sinE, pp, cts, tr):
    T = Pr.shape[0]
    n_row, n_p = 8, len(pp)
    names = ["gla_dq_f", "gla_dq_b", "gla_dg_f", "gla_dg_b", "gla_dk_f", "gla_dk_b", "gla_dv_f", "gla_dv_b",
             "ret_dq_f", "ret_dq_b", "ret_dk_f", "ret_dk_b", "ret_dv_f", "ret_dv_b",
             "ssd_dq_f", "ssd_dq_b", "ssd_dk_f", "ssd_dk_b", "ssd_dg_f", "ssd_dg_b", "ssd_dv_f", "ssd_dv_b",
             "d_r", "d_z", "d_gr", "d_xs"]
    ct_arrays = [cts[n] for n in names]

    def body(*refs):
        ins = [r[...] for r in refs[:n_row + n_p]]
        c = {n: r[...] for n, r in zip(names, refs[n_row + n_p:n_row + n_p + len(names)])}
        dPr_ref, du_ref = refs[n_row + n_p + len(names):n_row + n_p + len(names) + 2]
        dp_refs = refs[n_row + n_p + len(names) + 2:]
        _, vjp = jax.vjp(_prep_tile, *ins)
        ct_out = (jnp.concatenate([c["gla_dq_f"] + c["gla_dq_b"], c["gla_dg_f"], c["gla_dg_b"]], axis=1),
                  jnp.concatenate([c["ret_dq_f"] + c["ret_dq_b"], c["ret_dk_f"] + c["ret_dk_b"]], axis=1),
                  c["ssd_dq_f"] + c["ssd_dq_b"], c["ssd_dk_f"], c["ssd_dk_b"], c["ssd_dg_f"], c["ssd_dg_b"])
        d = vjp(ct_out)
        d_misc, d_gq, d_rq, d_rk, d_bm, d_cm = d[:6]
        dPr_ref[...] = jnp.concatenate(
            [c["gla_dv_f"] + c["gla_dv_b"], c["d_r"], d_rq, d_rk, c["ret_dv_f"] + c["ret_dv_b"], c["d_gr"],
             c["d_z"], d_gq, c["gla_dk_f"] + c["gla_dk_b"], d_misc,
             jnp.zeros((d_misc.shape[0], R_W - MISC - 128), F32)], axis=1).astype(dPr_ref.dtype)
        du_ref[...] = jnp.concatenate([c["ssd_dv_f"] + c["ssd_dv_b"] + c["d_xs"], d_bm, d_cm], axis=1)

        @pl.when(pl.program_id(0) == 0)
        def _():
            for r in dp_refs:
                r[...] = jnp.zeros_like(r)

        for r, g in zip(dp_refs, d[n_row:]):
            r[...] += g

    row = lambda a: pl.BlockSpec((tr, a.shape[1]), lambda i: (i, 0))
    return pl.pallas_call(
        body, name="prep_bwd", grid=(T // tr,),
        out_shape=(jax.ShapeDtypeStruct((T, R_W), BF16), jax.ShapeDtypeStruct((T, 1024), F32))
        + tuple(jax.ShapeDtypeStruct(p.shape, F32) for p in pp),
        in_specs=_prep_row_specs(tr) + [_whole(p) for p in pp] + [row(a) for a in ct_arrays],
        out_specs=(pl.BlockSpec((tr, R_W), lambda i: (i, 0)), pl.BlockSpec((tr, 1024), lambda i: (i, 0)))
        + tuple(_whole(p) for p in pp),
        compiler_params=_params(("arbitrary",)),
    )(Pr, Pr, Pr, Pr, u, u, cosE, sinE, *pp, *ct_arrays)


def _post_tile(ogf, ogb, r, ysf, ysb, xs, z, orf, orb, gr, gla_n, dexp, ssd_n, ret_n):
    bd = ((_iota((256, 256), 0) >> 6) == (_iota((256, 256), 1) >> 6)).astype(F32)
    og = ogf + ogb
    gla = og * lax.rsqrt(_sel(og * og, bd) * (1.0 / 64) + RMS_EPS) * gla_n * _silu(r)
    t = (ysf + ysb + dexp * xs) * _silu(z)
    ssd = t * lax.rsqrt(jnp.mean(t * t, axis=-1, keepdims=True) + RMS_EPS) * ssd_n
    o = orf + orb
    oc = o - _sel(o, bd) * (1.0 / 64)
    ret = oc * lax.rsqrt(_sel(oc * oc, bd) * (1.0 / 64) + RMS_EPS) * ret_n * _silu(gr)
    return jnp.concatenate([gla, ssd, ret], axis=1)


def _post_row_specs(tr):
    blk = lambda w, j: pl.BlockSpec((tr, w), lambda i: (i, j))
    return [blk(256, 0), blk(256, 0), blk(256, _ROFF["gla_r"] // 256), blk(512, 0), blk(512, 0), blk(512, 0),
            blk(512, _ROFF["ssd_z"] // 512), blk(256, 0), blk(256, 0), blk(256, _ROFF["ret_g"] // 256)]


def _post_fwd_call(rows, qp, tr):
    T = rows[0].shape[0]

    def body(*refs):
        refs[-1][...] = _post_tile(*[r[...] for r in refs[:-1]]).astype(refs[-1].dtype)

    return pl.pallas_call(
        body, name="post_fwd", grid=(T // tr,), out_shape=jax.ShapeDtypeStruct((T, D), BF16),
        in_specs=_post_row_specs(tr) + [_whole(p) for p in qp],
        out_specs=pl.BlockSpec((tr, D), lambda i: (i, 0)), compiler_params=_params(("parallel",)),
    )(*rows, *qp)


def _post_bwd_call(rows, qp, dmixed, tr):
    T = rows[0].shape[0]
    n_in = 10 + len(qp)

    def body(*refs):
        ins = [r[...] for r in refs[:n_in]]
        _, vjp = jax.vjp(_post_tile, *ins)
        d = vjp(refs[n_in][...])
        outs = refs[n_in + 1:]
        for o_ref, g in zip(outs[:7], (d[0], d[3], d[7], d[2], d[6], d[9], d[5])):
            o_ref[...] = g.astype(o_ref.dtype)

        @pl.when(pl.program_id(0) == 0)
        def _():
            for r in outs[7:]:
                r[...] = jnp.zeros_like(r)

        for r, g in zip(outs[7:], d[10:]):
            r[...] += g

    widths = [256, 512, 256, 256, 512, 256, 512]
    dts = [BF16] * 3 + [F32] * 4
    return pl.pallas_call(
        body, name="post_bwd", grid=(T // tr,),
        out_shape=tuple(jax.ShapeDtypeStruct((T, w), dt) for w, dt in zip(widths, dts))
        + tuple(jax.ShapeDtypeStruct(p.shape, F32) for p in qp),
        in_specs=_post_row_specs(tr) + [_whole(p) for p in qp] + [pl.BlockSpec((tr, D), lambda i: (i, 0))],
        out_specs=tuple(pl.BlockSpec((tr, w), lambda i: (i, 0)) for w in widths) + tuple(_whole(p) for p in qp),
        compiler_params=_params(("arbitrary",)),
    )(*rows, *qp, dmixed)


def _mixer_scan_operands(Pr, u, a_gla, a_ret, cmr, kf, kb, g8f, g8b, lg):
    gk, gv = (Pr, _ROFF["gla_k"] // 128), (Pr, _ROFF["gla_v"] // 256)
    rv = (Pr, _ROFF["ret_v"] // 256)
    return {
        "gla": {False: ((a_gla, 0), gk, gv, (a_gla, 1)), True: ((a_gla, 0), gk, gv, (a_gla, 2))},
        "ret": {False: ((a_ret, 0), (a_ret, 1), rv, (lg, 0)), True: ((a_ret, 0), (a_ret, 1), rv, (lg, 0))},
        "ssd": {False: ((cmr, 0), (kf, 0), (u, 0), (g8f, 0)), True: ((cmr, 0), (kb, 0), (u, 0), (g8b, 0))},
    }


def _post_rows(o, Pr, u):
    return [o["gla"][False][0], o["gla"][True][0], Pr, o["ssd"][False][0], o["ssd"][True][0], u, Pr,
            o["ret"][False][0], o["ret"][True][0], Pr]


def _mixer_forward(Tc, Pr, Px, cn, pp, cw8, cb, qp):
    cosE, sinE, lg = cn
    u = _conv_fwd_call(Px, cw8, cb, Tc)
    prep = _prep_fwd_call(Pr, u, cosE, sinE, pp, Tc)
    ops = _mixer_scan_operands(Pr, u, *prep, lg)
    o = {kind: _scan_fwd_call(kind, ops[kind], Tc) for kind in ops}
    return _post_fwd_call(_post_rows(o, Pr, u), qp, Tc), (u, prep, o)


def _mixer_backward(Tc, Pr, Px, cn, pp, cw8, cb, qp, saved, dmixed):
    cosE, sinE, lg = cn
    u, prep, o = saved
    post = _post_bwd_call(_post_rows(o, Pr, u), qp, dmixed, Tc)
    d_o = dict(gla=post[0], ssd=post[1], ret=post[2])
    cts = dict(d_r=post[3], d_z=post[4], d_gr=post[5], d_xs=post[6])
    ops = _mixer_scan_operands(Pr, u, *prep, lg)
    for kind in ops:
        st = {rev: o[kind][rev][1] for rev in (False, True)}
        res = _scan_bwd_call(kind, kind != "ret", ops[kind], st, d_o[kind], Tc)
        for rev, sfx in ((False, "_f"), (True, "_b")):
            for nm, a in zip(("_dq", "_dk", "_dv", "_dg"), res[rev]):
                cts[kind + nm + sfx] = a
    pb = _prep_bwd_call(Pr, u, cosE, sinE, pp, cts, Tc)
    dPx, dcw8, dcb = _conv_bwd_call(Px, cw8, cb, pb[1], Tc)
    return pb[0], dPx, tuple(pb[2:]), dcw8, dcb[0:1], tuple(post[7:])


@functools.lru_cache(maxsize=None)
def _make_blocks(Tc):
    def zeros(t):
        return jax.tree.map(jnp.zeros_like, t)

    def mix_fwd(X, nw, mods, w, gs, cn, pp, cw8, cb, qp):
        h = _norm_fwd_call(X, nw[0], mods[0], mods[1], None, Tc, BF16)
        Px, Pr = _mm(h, w[0], name="mm_fwd"), _mm(h, w[1], name="mm_fwd")
        mixed, saved = _mixer_forward(Tc, Pr, Px, cn, pp, cw8, cb, qp)
        M = _mm(mixed, w[2], name="mm_fwd")
        Xn = _norm_fwd_call(M, nw[1], mods[2], jnp.zeros_like(mods[2]), X, Tc)
        return Xn, (X, nw, mods, w, cn, pp, cw8, cb, qp, h, Px, Pr, mixed, saved, M)

    def mix_bwd(res, dXn):
        X, nw, mods, w, cn, pp, cw8, cb, qp, h, Px, Pr, mixed, saved, M = res
        dM, dnw1, da_post, _ = _norm_bwd_call(M, nw[1], mods[2], dXn, Tc, out_dtype=BF16)
        dmixed = _mm(dM, w[2], trans_b=True, name="mm_dx")
        dPr, dPx, dpp, dcw8, dcb, dqp = _mixer_backward(Tc, Pr, Px, cn, pp, cw8, cb, qp, saved, dmixed)
        dh = _mm(dPx, w[0], trans_b=True, name="mm_dx")
        dh = _mm(dPr, w[1], trans_b=True, name="mm_dx_acc", add=dh)
        dX, dnw0, da_pre, db_pre = _norm_bwd_call(X, nw[0], mods[0], dh, Tc, add=dXn)
        dgs = (_mm_tn(h, dPx, name="mm_dw"), _mm_tn(h, dPr, name="mm_dw"), _mm_tn(mixed, dM, name="mm_dw"))
        return (dX, (dnw0[0], dnw1[0]), (da_pre, db_pre, da_post), zeros(w), dgs, zeros(cn), dpp, dcw8, dcb, dqp)

    @jax.custom_vjp
    def mix_block(X, nw, mods, w, gs, cn, pp, cw8, cb, qp):
        return mix_fwd(X, nw, mods, w, gs, cn, pp, cw8, cb, qp)[0]

    mix_block.defvjp(mix_fwd, mix_bwd)

    def ffn_fwd(X, nw, mods, w, gs):
        h = _norm_fwd_call(X, nw[0], mods[0], mods[1], None, Tc, BF16)
        U1, U2 = _mm(h, w[0], name="mm_fwd"), _mm(h, w[1], name="mm_fwd")
        act = _act_call(U1, U2)
        Fo = _mm(act, w[2], name="mm_fwd")
        Xn = _norm_fwd_call(Fo, nw[1], mods[2], jnp.zeros_like(mods[2]), X, Tc)
        return Xn, (X, nw, mods, w, h, U1, U2, act, Fo)

    def ffn_bwd(res, dXn):
        X, nw, mods, w, h, U1, U2, act, Fo = res
        dFo, dnw1, da_post, _ = _norm_bwd_call(Fo, nw[1], mods[2], dXn, Tc, out_dtype=BF16)
        dU1, dU2 = _act_call(U1, U2, _mm(dFo, w[2], trans_b=True, name="mm_dx"))
        dh = _mm(dU1, w[0], trans_b=True, name="mm_dx")
        dh = _mm(dU2, w[1], trans_b=True, name="mm_dx_acc", add=dh)
        dX, dnw0, da_pre, db_pre = _norm_bwd_call(X, nw[0], mods[0], dh, Tc, add=dXn)
        dgs = (_mm_tn(h, dU1, name="mm_dw"), _mm_tn(h, dU2, name="mm_dw"), _mm_tn(act, dFo, name="mm_dw"))
        return dX, (dnw0[0], dnw1[0]), (da_pre, db_pre, da_post), zeros(w), dgs

    @jax.custom_vjp
    def ffn_block(X, nw, mods, w, gs):
        return ffn_fwd(X, nw, mods, w, gs)[0]

    ffn_block.defvjp(ffn_fwd, ffn_bwd)
    return mix_block, ffn_block


def _rope_tables(Tl, Tc):
    rows = Tl // GRID_W
    row = jnp.repeat(jnp.arange(rows), GRID_W).astype(F32)
    col = jnp.tile(jnp.arange(GRID_W), rows).astype(F32)
    inv_freq = 10000.0 ** (-jnp.arange(16, dtype=F32) / 16)
    ang = jnp.concatenate([row[:, None] * inv_freq, col[:, None] * inv_freq], axis=-1)
    cos = jnp.concatenate([jnp.ones((Tc, 32), F32), jnp.cos(ang)], axis=0)
    sin = jnp.concatenate([jnp.zeros((Tc, 32), F32), jnp.sin(ang)], axis=0)
    return jnp.tile(cos, (1, 8)), jnp.tile(sin, (1, 8))


def _rows8(first, second):
    z = jnp.zeros((6,) + first.shape, F32)
    return jnp.concatenate([first[None], second[None], z], axis=0)


def _local_forward(xcat, mod_l, mod_c, sp, gs, W, Tc):
    Tt = xcat.shape[0]
    cosE, sinE = _rope_tables(Tt - Tc, Tc)
    log_gamma = jnp.log1p(-jnp.exp2(-5.0 - jnp.arange(4, dtype=F32)))
    lg = jnp.broadcast_to(jnp.concatenate([log_gamma, jnp.zeros((GPAD - 4,), F32)])[None, :], (Tt, GPAD))
    mix_block, ffn_block = _make_blocks(Tc)
    X = xcat
    for l in range(DEPTH):
        ml, mc = mod_l[l].reshape(6, D), mod_c[l].reshape(6, D)
        gu = sp["gla_gate_up"][l]
        Wg = jnp.zeros((128, 256), F32).at[0:16, 0:128].set(gu[0]).at[16:32, 128:256].set(gu[1])
        pp = (Wg, sp["gla_gate_b"][l].reshape(1, 256), sp["ssd_dt_bias"][l][0:1], sp["ssd_dt_bias"][l][1:2],
              -jnp.exp(sp["ssd_a_log"][l][0:1]), -jnp.exp(sp["ssd_a_log"][l][1:2]))
        qp = (sp["gla_norm"][l].reshape(1, 256), jnp.repeat(sp["ssd_d"][l], 64).reshape(1, 512),
              sp["ssd_norm"][l].reshape(1, 512), sp["ret_norm"][l].reshape(1, 256))
        cw8 = jnp.pad(sp["ssd_conv_w"][l], ((0, 3), (0, 0)))
        X = mix_block(X, (sp["norm_mix_pre"][l], sp["norm_mix_post"][l]),
                      (_rows8(1.0 + mc[1], 1.0 + ml[1]), _rows8(mc[0], ml[0]), _rows8(mc[2], ml[2])),
                      (W["w_x"][l], W["w_r"][l], W["w_out"][l]), (gs["w_x"][l], gs["w_r"][l], gs["w_out"][l]),
                      (cosE, sinE, lg), pp, cw8, sp["ssd_conv_b"][l].reshape(1, 1024), qp)
        X = ffn_block(X, (sp["norm_ffn_pre"][l], sp["norm_ffn_post"][l]),
                      (_rows8(1.0 + mc[4], 1.0 + ml[4]), _rows8(mc[3], ml[3]), _rows8(mc[5], ml[5])),
                      (W["w_1"][l], W["w_3"][l], W["ffn_w2"][l]), (gs["w_1"][l], gs["w_3"][l], gs["ffn_w2"][l]))
    return X


def _local_weights(w_in, w_out, w13, w2):
    w_x, w_r = _split_w_in(w_in)
    W = dict(w_x=w_x, w_r=w_r, w_out=w_out, w_1=w13[..., :FFN_H], w_3=w13[..., FFN_H:], ffn_w2=w2)
    return W, {n: jnp.zeros(a.shape, F32) for n, a in W.items()}


def _local_weight_grads(d_gs):
    return (_merge_w_in(d_gs["w_x"], d_gs["w_r"]), d_gs["w_out"],
            jnp.concatenate([d_gs["w_1"], d_gs["w_3"]], axis=-1), d_gs["ffn_w2"])


def _loss_call(X, target, Tc):
    Tt, W = X.shape
    tr = Tc
    nt = Tt // tr

    def body(x_ref, t_ref, loss_ref, dx_ref, acc_ref):
        i = pl.program_id(0)

        @pl.when(i == 0)
        def _():
            acc_ref[...] = jnp.zeros_like(acc_ref)
            dx_ref[...] = jnp.zeros_like(dx_ref)

        @pl.when(i > 0)
        def _():
            e = x_ref[...] - t_ref[...]
            dx_ref[...] = e * (1.0 / W)
            acc_ref[...] += jnp.sum(e * e, axis=0, keepdims=True)

        @pl.when(i == nt - 1)
        def _():
            loss_ref[...] = jnp.full(loss_ref.shape, (0.5 / W) * jnp.sum(acc_ref[...]), F32)

    loss, dx = pl.pallas_call(
        body, name="loss",
        out_shape=(jax.ShapeDtypeStruct((8, 128), F32), jax.ShapeDtypeStruct((Tt, W), F32)),
        grid=(nt,),
        in_specs=[pl.BlockSpec((tr, W), lambda i: (i, 0)),
                  pl.BlockSpec((tr, W), lambda i: (jnp.maximum(i - 1, 0), 0))],
        out_specs=(pl.BlockSpec((8, 128), lambda i: (0, 0)), pl.BlockSpec((tr, W), lambda i: (i, 0))),
        scratch_shapes=[pltpu.VMEM((1, W), F32)],
        compiler_params=_params(("arbitrary",)),
    )(X, target)
    return loss[0, 0], dx


def _adamw_call(w, g, m, v, name):
    R, Cc = w.shape
    tr = _pick(R, (512, 352, 256, 128, 64, 32, 16, 8))
    c1 = 1.0 - ADAM_B1 ** ADAM_STEP
    c2 = 1.0 - ADAM_B2 ** ADAM_STEP

    def body(w_ref, g_ref, m_ref, v_ref, d_ref, nm_ref, nv_ref):
        gv = g_ref[...]
        nm = ADAM_B1 * m_ref[...] + (1.0 - ADAM_B1) * gv
        nv = ADAM_B2 * v_ref[...] + (1.0 - ADAM_B2) * (gv * gv)
        d_ref[...] = -ADAM_LR * ((nm / c1) / (jnp.sqrt(nv / c2) + ADAM_EPS) + ADAM_WD * w_ref[...])
        nm_ref[...] = nm
        nv_ref[...] = nv

    spec = pl.BlockSpec((tr, Cc), lambda i: (i, 0))
    sh = jax.ShapeDtypeStruct((R, Cc), F32)
    return pl.pallas_call(
        body, name=name, out_shape=(sh, sh, sh), grid=(R // tr,),
        in_specs=[spec] * 4, out_specs=(spec,) * 3, compiler_params=_params(("parallel",)),
    )(w, g, m, v)


def _sum_call(xs, name, also_bf16=False):
    R, Cc = xs[0].shape
    tr = _pick(R, (512, 352, 256, 128, 64, 32, 16))
    k = len(xs)

    def body(*refs):
        acc = refs[0][...].astype(F32)
        for r in refs[1:k]:
            acc = acc + r[...].astype(F32)
        refs[k][...] = acc
        if also_bf16:
            refs[k + 1][...] = acc.astype(BF16)

    spec = pl.BlockSpec((tr, Cc), lambda i: (i, 0))
    sh = jax.ShapeDtypeStruct((R, Cc), F32)
    return pl.pallas_call(
        body, name=name, grid=(R // tr,), in_specs=[spec] * k,
        out_shape=(sh, jax.ShapeDtypeStruct((R, Cc), BF16)) if also_bf16 else sh,
        out_specs=(spec, spec) if also_bf16 else spec, compiler_params=_params(("parallel",)),
    )(*xs)


MESH = pl.DeviceIdType.MESH
ANY = pl.BlockSpec(memory_space=pl.ANY)


def _me():
    return lax.axis_index("x"), lax.axis_index("y"), lax.axis_index("c")


def _two_level_gather_body(n_arr, x_refs, out_refs, send_sems, recv_sems, local_sems):
    x, y, c = _me()
    me, sibling = (x, y, c), (x, y, 1 - c)
    chips = [(1 - x, y), (x, 1 - y), (1 - x, 1 - y)]

    def slab(a, px, py, pc):
        return out_refs[a].at[4 * px + 2 * py + pc]

    def copy(a, k, block, to, src=None):
        return pltpu.make_async_remote_copy(
            src_ref=slab(a, *block) if src is None else src, dst_ref=slab(a, *block),
            send_sem=send_sems.at[a, k], recv_sem=recv_sems.at[a, k], device_id=to, device_id_type=MESH)

    mine = [pltpu.make_async_copy(x_refs[a], slab(a, *me), local_sems.at[a]) for a in range(n_arr)]
    for cp in mine:
        cp.start()
    first = []
    for a in range(n_arr):
        first.append(copy(a, 0, me, sibling, src=x_refs[a]))
        first += [copy(a, 1 + j, me, (*chip, c), src=x_refs[a]) for j, chip in enumerate(chips)]
    for cp in first:
        cp.start()
    passed = []
    for j, chip in enumerate(chips):
        for a in range(n_arr):
            copy(a, 1 + j, (*chip, c), me).wait_recv()
            fw = copy(a, 4 + j, (*chip, c), sibling)
            fw.start()
            passed.append(fw)
    for a in range(n_arr):
        copy(a, 0, sibling, me).wait_recv()
        for j, chip in enumerate(chips):
            copy(a, 4 + j, (*chip, 1 - c), me).wait_recv()
    for cp in first + passed:
        cp.wait_send()
    for cp in mine:
        cp.wait()


def _gather_big(xs, name):
    n_arr = len(xs)

    def body(*refs):
        _two_level_gather_body(n_arr, refs[:n_arr], refs[n_arr:2 * n_arr], *refs[2 * n_arr:])

    return pl.pallas_call(
        body, name=name,
        out_shape=tuple(jax.ShapeDtypeStruct((N_DEV,) + a.shape, a.dtype) for a in xs),
        in_specs=[ANY] * n_arr, out_specs=(ANY,) * n_arr,
        scratch_shapes=[pltpu.SemaphoreType.DMA((n_arr, 7)), pltpu.SemaphoreType.DMA((n_arr, 7)),
                        pltpu.SemaphoreType.DMA((n_arr,))],
    )(*xs)


def _gather_small(x, name):
    def body(x_ref, out_ref, send_sems, recv_sems, local_sems):
        _two_level_gather_body(1, [x_ref], [out_ref], send_sems, recv_sems, local_sems)

    vm = pl.BlockSpec(memory_space=pltpu.VMEM)
    return pl.pallas_call(
        body, name=name,
        out_shape=jax.ShapeDtypeStruct((N_DEV,) + x.shape, x.dtype),
        in_specs=[vm], out_specs=vm,
        scratch_shapes=[pltpu.SemaphoreType.DMA((1, 7)), pltpu.SemaphoreType.DMA((1, 7)),
                        pltpu.SemaphoreType.DMA((1,))],
    )(x)


def _exchange_sibling(gs_, name):
    n_arr = len(gs_)

    def body(*refs):
        g_refs, out_refs = refs[:n_arr], refs[n_arr:2 * n_arr]
        send_sems, recv_sems = refs[2 * n_arr:]
        x, y, c = _me()
        cps = []
        for a in range(n_arr):
            for px in range(2):
                for py in range(2):
                    i = 2 * px + py
                    cps.append(pltpu.make_async_remote_copy(
                        src_ref=g_refs[a].at[4 * px + 2 * py + (1 - c)], dst_ref=out_refs[a].at[i],
                        send_sem=send_sems.at[a, i], recv_sem=recv_sems.at[a, i],
                        device_id=(x, y, 1 - c), device_id_type=MESH))
        for cp in cps:
            cp.start()
        for cp in cps:
            cp.wait()

    return pl.pallas_call(
        body, name=name,
        out_shape=tuple(jax.ShapeDtypeStruct((4,) + a.shape[1:], a.dtype) for a in gs_),
        in_specs=[ANY] * n_arr, out_specs=(ANY,) * n_arr,
        scratch_shapes=[pltpu.SemaphoreType.DMA((n_arr, 4)), pltpu.SemaphoreType.DMA((n_arr, 4))],
    )(*gs_)


def _exchange_chips(ps, name):
    n_arr = len(ps)

    def body(*refs):
        p_refs, out_refs = refs[:n_arr], refs[n_arr:2 * n_arr]
        send_sems, recv_sems = refs[2 * n_arr:]
        x, y, c = _me()
        chips = [(1 - x, y), (x, 1 - y), (1 - x, 1 - y)]
        cps = []
        for a in range(n_arr):
            for j, (cx, cy) in enumerate(chips):
                cps.append(pltpu.make_async_remote_copy(
                    src_ref=p_refs[a].at[2 * cx + cy], dst_ref=out_refs[a].at[j],
                    send_sem=send_sems.at[a, j], recv_sem=recv_sems.at[a, j],
                    device_id=(cx, cy, c), device_id_type=MESH))
        for cp in cps:
            cp.start()
        for cp in cps:
            cp.wait()

    return pl.pallas_call(
        body, name=name,
        out_shape=tuple(jax.ShapeDtypeStruct((3,) + a.shape[1:], a.dtype) for a in ps),
        in_specs=[ANY] * n_arr, out_specs=(ANY,) * n_arr,
        scratch_shapes=[pltpu.SemaphoreType.DMA((n_arr, 3)), pltpu.SemaphoreType.DMA((n_arr, 3))],
    )(*ps)


def _reduce_scatter(gs_):
    x, y, c = _me()
    from_sib = _exchange_sibling(gs_, "rs_sibling")
    ps, ps16 = [], []
    for a, g in enumerate(gs_):
        R, Cc = g.shape[1:]
        mine = lax.dynamic_index_in_dim(g.reshape(4, 2, R, Cc), c, axis=1, keepdims=False)
        p32, p16 = _sum_call([mine.reshape(4 * R, Cc), from_sib[a].reshape(4 * R, Cc)], f"rs_add_sib{a}",
                             also_bf16=True)
        ps.append(p32.reshape(4, R, Cc))
        ps16.append(p16.reshape(4, R, Cc))
    from_chips = _exchange_chips(ps16, "rs_chips")
    outs = []
    for a, p in enumerate(ps):
        mine = lax.dynamic_index_in_dim(p, 2 * x + y, axis=0, keepdims=False)
        outs.append(_sum_call([mine, from_chips[a][0], from_chips[a][1], from_chips[a][2]], f"rs_add_chips{a}"))
    return outs


_SMALL = ["norm_mix_pre", "norm_mix_post", "norm_ffn_pre", "norm_ffn_post", "gla_gate_up", "gla_gate_b",
          "gla_norm", "ssd_conv_w", "ssd_conv_b", "ssd_dt_bias", "ssd_a_log", "ssd_d", "ssd_norm", "ret_norm"]


def _pack(arrs):
    flat = jnp.concatenate([a.reshape(-1) for a in arrs])
    n = flat.shape[0]
    npad = -(-n // 1024) * 1024
    return jnp.pad(flat, (0, npad - n)).reshape(npad // 128, 128)


def _unpack(buf, shapes):
    flat = buf.reshape(-1)
    out, o = [], 0
    for s in shapes:
        n = math.prod(s)
        out.append(flat[o:o + n].reshape(s))
        o += n
    return out


def kernel(x, c, ctx, c_ctx, ada_w, ada_b, norm_mix_pre, norm_mix_post, norm_ffn_pre, norm_ffn_post, w_in, w_out, gla_gate_up, gla_gate_b, gla_norm, ssd_conv_w, ssd_conv_b, ssd_dt_bias, ssd_a_log, ssd_d, ssd_norm, ret_norm, ffn_w13, ffn_w2, loss_target, m_c_ctx, m_ada_w, m_ada_b, m_norm_mix_pre, m_norm_mix_post, m_norm_ffn_pre, m_norm_ffn_post, m_w_in, m_w_out, m_gla_gate_up, m_gla_gate_b, m_gla_norm, m_ssd_conv_w, m_ssd_conv_b, m_ssd_dt_bias, m_ssd_a_log, m_ssd_d, m_ssd_norm, m_ret_norm, m_ffn_w13, m_ffn_w2, v_c_ctx, v_ada_w, v_ada_b, v_norm_mix_pre, v_norm_mix_post, v_norm_ffn_pre, v_norm_ffn_post, v_w_in, v_w_out, v_gla_gate_up, v_gla_gate_b, v_gla_norm, v_ssd_conv_w, v_ssd_conv_b, v_ssd_dt_bias, v_ssd_a_log, v_ssd_d, v_ssd_norm, v_ret_norm, v_ffn_w13, v_ffn_w2):
    P_ = dict(c_ctx=c_ctx, ada_w=ada_w, ada_b=ada_b, norm_mix_pre=norm_mix_pre, norm_mix_post=norm_mix_post,
              norm_ffn_pre=norm_ffn_pre, norm_ffn_post=norm_ffn_post, w_in=w_in, w_out=w_out,
              gla_gate_up=gla_gate_up, gla_gate_b=gla_gate_b, gla_norm=gla_norm, ssd_conv_w=ssd_conv_w,
              ssd_conv_b=ssd_conv_b, ssd_dt_bias=ssd_dt_bias, ssd_a_log=ssd_a_log, ssd_d=ssd_d,
              ssd_norm=ssd_norm, ret_norm=ret_norm, ffn_w13=ffn_w13, ffn_w2=ffn_w2)
    M_ = dict(c_ctx=m_c_ctx, ada_w=m_ada_w, ada_b=m_ada_b, norm_mix_pre=m_norm_mix_pre,
              norm_mix_post=m_norm_mix_post, norm_ffn_pre=m_norm_ffn_pre, norm_ffn_post=m_norm_ffn_post,
              w_in=m_w_in, w_out=m_w_out, gla_gate_up=m_gla_gate_up, gla_gate_b=m_gla_gate_b,
              gla_norm=m_gla_norm, ssd_conv_w=m_ssd_conv_w, ssd_conv_b=m_ssd_conv_b, ssd_dt_bias=m_ssd_dt_bias,
              ssd_a_log=m_ssd_a_log, ssd_d=m_ssd_d, ssd_norm=m_ssd_norm, ret_norm=m_ret_norm,
              ffn_w13=m_ffn_w13, ffn_w2=m_ffn_w2)
    V_ = dict(c_ctx=v_c_ctx, ada_w=v_ada_w, ada_b=v_ada_b, norm_mix_pre=v_norm_mix_pre,
              norm_mix_post=v_norm_mix_post, norm_ffn_pre=v_norm_ffn_pre, norm_ffn_post=v_norm_ffn_post,
              w_in=v_w_in, w_out=v_w_out, gla_gate_up=v_gla_gate_up, gla_gate_b=v_gla_gate_b,
              gla_norm=v_gla_norm, ssd_conv_w=v_ssd_conv_w, ssd_conv_b=v_ssd_conv_b, ssd_dt_bias=v_ssd_dt_bias,
              ssd_a_log=v_ssd_a_log, ssd_d=v_ssd_d, ssd_norm=v_ssd_norm, ret_norm=v_ret_norm,
              ffn_w13=v_ffn_w13, ffn_w2=v_ffn_w2)
    order = ["c_ctx", "ada_w", "ada_b", "norm_mix_pre", "norm_mix_post", "norm_ffn_pre", "norm_ffn_post", "w_in",
             "w_out", "gla_gate_up", "gla_gate_b", "gla_norm", "ssd_conv_w", "ssd_conv_b", "ssd_dt_bias",
             "ssd_a_log", "ssd_d", "ssd_norm", "ret_norm", "ffn_w13", "ffn_w2"]

    mx, my, mc_ = _me()
    me = 4 * mx + 2 * my + mc_
    Tl, Tc = x.shape[1], ctx.shape[1]
    n_in, n_out, n_13, n_2 = w_in.shape[2], w_out.shape[1], ffn_w13.shape[2], ffn_w2.shape[1]
    n_ada = ada_w.shape[2]

    shards = [w_in.astype(BF16).reshape(DEPTH * D, n_in), w_out.astype(BF16).reshape(DEPTH * n_out, D),
              ffn_w13.astype(BF16).reshape(DEPTH * D, n_13), ffn_w2.astype(BF16).reshape(DEPTH * n_2, D)]
    g_in, g_out, g_13, g_2 = _gather_big(shards, "gather_weights")
    W, gs = _local_weights(
        jnp.moveaxis(g_in.reshape(N_DEV, DEPTH, D, n_in), 0, 2).reshape(DEPTH, D, N_DEV * n_in),
        jnp.moveaxis(g_out.reshape(N_DEV, DEPTH, n_out, D), 0, 1).reshape(DEPTH, N_DEV * n_out, D),
        jnp.moveaxis(g_13.reshape(N_DEV, DEPTH, D, n_13), 0, 2).reshape(DEPTH, D, N_DEV * n_13),
        jnp.moveaxis(g_2.reshape(N_DEV, DEPTH, n_2, D), 0, 1).reshape(DEPTH, N_DEV * n_2, D))

    cw = ssd_conv_w.shape[2]
    small_in = jnp.concatenate([jnp.pad(c, ((0, 7), (0, 0))).reshape(-1),
                                ssd_conv_w.reshape(-1)]).reshape(-1, 128)
    n_c_rows = 8 * D // 128
    small_in = jnp.pad(small_in, ((0, -small_in.shape[0] % 8), (0, 0)))
    gathered = _gather_small(small_in, "gather_c_conv")
    c_all = gathered[:, :n_c_rows].reshape(N_DEV, 8, D)[:, 0]
    conv_rows = DEPTH * 5 * cw // 128
    conv_full = gathered[:, n_c_rows:n_c_rows + conv_rows].reshape(N_DEV, DEPTH, 5, cw)
    conv_full = jnp.moveaxis(conv_full, 0, 2).reshape(DEPTH, 5, N_DEV * cw)
    c9 = jnp.concatenate([c_all, c_ctx[None], jnp.zeros((7, D), F32)], axis=0)
    s9 = c9 * jax.nn.sigmoid(c9)
    mod_piece = jnp.concatenate([_mm(s9, ada_w[l], name="mm_mod") for l in range(DEPTH)], axis=0)
    mod_g = _gather_small(mod_piece, "gather_mod")
    mod_all = jnp.moveaxis(mod_g.reshape(N_DEV, DEPTH, 16, n_ada), 0, 2).reshape(DEPTH, 16, N_DEV * n_ada)
    mod_all = mod_all + ada_b[:, None, :]
    mod_l = lax.dynamic_index_in_dim(mod_all, me, axis=1, keepdims=False)
    mod_c = mod_all[:, 8]

    sp = {n: P_[n] for n in _SMALL}
    sp["ssd_conv_w"] = conv_full
    xcat = jnp.concatenate([ctx[0], x[0]], axis=0)
    Xf, vjp = jax.vjp(lambda xc, ml, mc, sp_, gs_: _local_forward(xc, ml, mc, sp_, gs_, W, Tc),
                      xcat, mod_l, mod_c, sp, gs)
    loss_local, dX = _loss_call(Xf, loss_target[0], Tc)
    d_xcat, d_mod_l, d_mod_c, d_sp, d_gs = vjp(dX)
    loss = lax.psum(loss_local, ("x", "y", "c"))
    grad_x = d_xcat[Tc:][None]

    def dev_major_cols(g, n):
        K = g.shape[1]
        return jnp.moveaxis(g.reshape(DEPTH, K, N_DEV, n), 2, 0).reshape(N_DEV, DEPTH * K, n)

    def dev_major_rows(g, n):
        return jnp.moveaxis(g.reshape(DEPTH, N_DEV, n, D), 1, 0).reshape(N_DEV, DEPTH * n, D)

    gw_in, gw_out, gw_13, gw_2 = _local_weight_grads(d_gs)
    big = [dev_major_cols(gw_in, n_in), dev_major_rows(gw_out, n_out),
           dev_major_cols(gw_13, n_13), dev_major_rows(gw_2, n_2)]
    r_in, r_out, r_13, r_2 = _reduce_scatter(big)
    G = dict(w_in=r_in.reshape(DEPTH, D, n_in), w_out=r_out.reshape(DEPTH, n_out, D),
             ffn_w13=r_13.reshape(DEPTH, D, n_13), ffn_w2=r_2.reshape(DEPTH, n_2, D))

    dmod_rows = jnp.concatenate([d_mod_l, d_mod_c], axis=0)
    dmod_g = _gather_small(dmod_rows, "gather_dmod").reshape(N_DEV, 2, DEPTH, 6 * D)
    dl = jnp.moveaxis(dmod_g[:, 0], 0, 1)
    dc = dmod_g[:, 1, :, :]
    dc_tot = dc[0]
    for d_ in range(1, N_DEV):
        dc_tot = dc_tot + dc[d_]
    dmod9 = jnp.concatenate([dl, dc_tot[:, None, :], jnp.zeros((DEPTH, 7, 6 * D), F32)], axis=1)
    g_ada_b = dmod9[:, 0]
    for r_ in range(1, 9):
        g_ada_b = g_ada_b + dmod9[:, r_]
    dmod9_mine = lax.dynamic_slice_in_dim(dmod9, me * n_ada, n_ada, axis=2)
    s9T = jnp.pad(s9.T, ((0, 0), (0, 112)))
    g_ada_w = jnp.stack([_mm(s9T, jnp.pad(dmod9_mine[l], ((0, 112), (0, 0))), name="mm_dada")
                         for l in range(DEPTH)])
    ds9 = _mm(dmod9_mine[0], ada_w[0], trans_b=True, name="mm_ds9")
    for l in range(1, DEPTH):
        ds9 = _mm(dmod9_mine[l], ada_w[l], trans_b=True, name="mm_ds9_acc", add=ds9)
    ds_ctx_part = ds9[8]

    small_names = [n for n in _SMALL]
    small_parts = [d_sp[n] for n in small_names] + [ds_ctx_part]
    packed = _pack(small_parts)
    allp = _gather_small(packed, "gather_small_grads")
    summed = _sum_call([allp[d_] for d_ in range(N_DEV)], "sum_small_grads")
    parts = _unpack(summed, [p.shape for p in small_parts])
    for n, p in zip(small_names, parts[:-1]):
        G[n] = p
    sig = jax.nn.sigmoid(c_ctx)
    G["c_ctx"] = parts[-1] * (sig * (1.0 + c_ctx * (1.0 - sig)))
    G["ssd_conv_w"] = lax.dynamic_slice_in_dim(G["ssd_conv_w"], me * cw, cw, axis=2)
    G["ada_w"] = g_ada_w
    G["ada_b"] = g_ada_b

    delta, new_m, new_v = {}, {}, {}
    for n in ["ada_w", "w_in", "w_out", "ffn_w13", "ffn_w2"]:
        sh = P_[n].shape
        f2 = lambda a: a.reshape(sh[0] * sh[1], sh[2])
        d_, m_, v_ = _adamw_call(f2(P_[n]), f2(G[n]), f2(M_[n]), f2(V_[n]), f"adamw_{n}")
        delta[n], new_m[n], new_v[n] = d_.reshape(sh), m_.reshape(sh), v_.reshape(sh)
    rest = [n for n in order if n not in delta]
    shapes = [P_[n].shape for n in rest]
    d_, m_, v_ = _adamw_call(_pack([P_[n] for n in rest]), _pack([G[n] for n in rest]),
                             _pack([M_[n] for n in rest]), _pack([V_[n] for n in rest]), "adamw_small")
    for n, a, b, e in zip(rest, _unpack(d_, shapes), _unpack(m_, shapes), _unpack(v_, shapes)):
        delta[n], new_m[n], new_v[n] = a, b, e

    return (loss, grad_x, *[G[n] for n in order], *[delta[n] for n in order],
            *[new_m[n] for n in order], *[new_v[n] for n in order])
```

```python
import functools
import math

import jax
import jax.numpy as jnp
from jax import lax
from jax.experimental import pallas as pl
from jax.experimental.pallas import tpu as pltpu

F32 = jnp.float32
BF16 = jnp.bfloat16

D = 1024
DEPTH = 4
GRID_W = 64
RMS_EPS = 1e-6
GLA_TAU = 16.0
FFN_H = 2816
IN_COLS = 3376
N_DEV = 8
ADAM_LR, ADAM_B1, ADAM_B2, ADAM_EPS, ADAM_WD, ADAM_STEP = 0.001, 0.9, 0.999, 1e-08, 0.01, 10

VMEM_LIMIT = 48 * 1024 * 1024

_ORIG = dict(gla_q=(0, 128), gla_k=(128, 128), gla_v=(256, 256), gla_r=(512, 256), gla_lr=(768, 32),
             ssd_z=(800, 512), ssd_xbc=(1312, 1024), ssd_dt=(2336, 16), ret_q=(2352, 256), ret_k=(2608, 256),
             ret_v=(2864, 256), ret_g=(3120, 256))
_R_ORDER = ["gla_v", "gla_r", "ret_q", "ret_k", "ret_v", "ret_g", "ssd_z", "gla_q", "gla_k", "gla_lr", "ssd_dt"]
R_W = 2560
_ROFF = {}
_o = 0
for _n in _R_ORDER:
    _ROFF[_n] = _o
    _o += _ORIG[_n][1]
MISC = _ROFF["gla_lr"]
assert MISC == 2304 and _o == 2352


def _split_w_in(w):
    xs, xz = _ORIG["ssd_xbc"]
    parts = [w[..., _ORIG[n][0]:_ORIG[n][0] + _ORIG[n][1]] for n in _R_ORDER]
    parts.append(jnp.zeros(w.shape[:-1] + (R_W - _o,), w.dtype))
    return w[..., xs:xs + xz], jnp.concatenate(parts, axis=-1)


def _merge_w_in(wx, wr):
    pieces = []
    for n, (s, z) in sorted(_ORIG.items(), key=lambda t: t[1][0]):
        pieces.append(wx if n == "ssd_xbc" else wr[..., _ROFF[n]:_ROFF[n] + z])
    return jnp.concatenate(pieces, axis=-1)


def _pick(n, cands):
    for c in cands:
        if n % c == 0:
            return c
    return n


def _params(sem=None):
    kw = dict(vmem_limit_bytes=VMEM_LIMIT)
    if sem is not None:
        kw["dimension_semantics"] = sem
    return pltpu.CompilerParams(**kw)


def _iota(shape, dim):
    return lax.broadcasted_iota(jnp.int32, shape, dim)


def _dot(a, b, dims):
    return lax.dot_general(a, b, (dims, ((), ())), preferred_element_type=F32)


_NN = ((1,), (0,))
_NT = ((1,), (1,))
_TN = ((0,), (0,))


def _bf(x):
    return x.astype(BF16)


def _dot_sel(x, e, dims, x_left=True):
    eb = e.astype(BF16)
    hi = x.astype(BF16)
    r1 = x - hi.astype(F32)
    mid = r1.astype(BF16)
    lo = (r1 - mid.astype(F32)).astype(BF16)
    out = None
    for p in (hi, mid, lo):
        t = _dot(p, eb, dims) if x_left else _dot(eb, p, dims)
        out = t if out is None else out + t
    return out


@jax.custom_vjp
def _sel(x, e):
    return _dot_sel(x, e, _NN)


_sel.defvjp(lambda x, e: (_dot_sel(x, e, _NN), e), lambda e, g: (_dot_sel(g, e, _NT), jnp.zeros_like(e)))


def _sig(x):
    e = jnp.exp(-jnp.abs(x))
    return jnp.where(x >= 0, 1.0 / (1.0 + e), e / (1.0 + e))


@jax.custom_vjp
def _sigmoid(x):
    return _sig(x)


def _sigmoid_fwd(x):
    s = _sig(x)
    return s, s


_sigmoid.defvjp(_sigmoid_fwd, lambda s, g: (g * s * (1.0 - s),))


def _silu(x):
    return x * _sigmoid(x)


@jax.custom_vjp
def _softplus(x):
    return jnp.maximum(x, 0.0) + jnp.log(1.0 + jnp.exp(-jnp.abs(x)))


_softplus.defvjp(lambda x: (jnp.maximum(x, 0.0) + jnp.log(1.0 + jnp.exp(-jnp.abs(x))), x),
                 lambda x, g: (g * _sig(x),))


def _log_sigmoid(x):
    return -_softplus(-x)


@jax.custom_vjp
def _mm_bf(x, w):
    return _dot(_bf(x), _bf(w), _NN)


_mm_bf.defvjp(lambda x, w: (_dot(_bf(x), _bf(w), _NN), (x, w)),
              lambda r, g: (_dot(_bf(g), _bf(r[1]), _NT), _dot(_bf(r[0]), _bf(g), _TN)))


_TILE_M = (1088, 1024, 512, 256, 128, 64, 32, 16)
_TILE_N = (1408, 1280, 1024, 768, 512, 384, 256, 128)
_TILE_K = (1408, 1280, 1024, 768, 512, 384, 256, 128)


def _mm(a, b, *, trans_b=False, name, add=None, out_dtype=F32):
    M, K = a.shape
    N = b.shape[0] if trans_b else b.shape[1]
    assert (b.shape[1] if trans_b else b.shape[0]) == K
    tm, tn, tk = _pick(M, _TILE_M), _pick(N, _TILE_N), _pick(K, _TILE_K)
    nk = K // tk
    dims = _NT if trans_b else _NN
    has_add = add is not None

    def body(*refs):
        a_ref, b_ref = refs[0], refs[1]
        o_ref, acc_ref = refs[-2], refs[-1]
        k = pl.program_id(2)

        @pl.when(k == 0)
        def _():
            acc_ref[...] = refs[2][...] if has_add else jnp.zeros_like(acc_ref)

        acc_ref[...] += _dot(a_ref[...].astype(BF16), b_ref[...].astype(BF16), dims)

        @pl.when(k == nk - 1)
        def _():
            o_ref[...] = acc_ref[...].astype(o_ref.dtype)

    b_spec = (pl.BlockSpec((tn, tk), lambda i, j, k: (j, k)) if trans_b
              else pl.BlockSpec((tk, tn), lambda i, j, k: (k, j)))
    o_spec = pl.BlockSpec((tm, tn), lambda i, j, k: (i, j))
    return pl.pallas_call(
        body, name=name,
        out_shape=jax.ShapeDtypeStruct((M, N), out_dtype),
        grid=(M // tm, N // tn, nk),
        in_specs=[pl.BlockSpec((tm, tk), lambda i, j, k: (i, k)), b_spec] + ([o_spec] if has_add else []),
        out_specs=o_spec,
        scratch_shapes=[pltpu.VMEM((tm, tn), F32)],
        compiler_params=_params(("parallel", "parallel", "arbitrary")),
    )(*((a, b, add) if has_add else (a, b)))


def _mm_tn(a, g, *, name, out_dtype=F32):
    M, K = a.shape
    N = g.shape[1]
    tm, tk, tn = _pick(M, _TILE_M), _pick(K, _TILE_K), _pick(N, _TILE_N)
    nm = M // tm

    def body(a_ref, g_ref, o_ref, acc_ref):
        i = pl.program_id(2)

        @pl.when(i == 0)
        def _():
            acc_ref[...] = jnp.zeros_like(acc_ref)

        acc_ref[...] += _dot(a_ref[...].astype(BF16), g_ref[...].astype(BF16), _TN)

        @pl.when(i == nm - 1)
        def _():
            o_ref[...] = acc_ref[...].astype(o_ref.dtype)

    return pl.pallas_call(
        body, name=name,
        out_shape=jax.ShapeDtypeStruct((K, N), out_dtype),
        grid=(K // tk, N // tn, nm),
        in_specs=[pl.BlockSpec((tm, tk), lambda k, j, i: (i, k)), pl.BlockSpec((tm, tn), lambda k, j, i: (i, j))],
        out_specs=pl.BlockSpec((tk, tn), lambda k, j, i: (k, j)),
        scratch_shapes=[pltpu.VMEM((tk, tn), F32)],
        compiler_params=_params(("parallel", "parallel", "arbitrary")),
    )(a, g)


def _norm_fwd_call(x, w, a2, b2, res, tr, out_dtype=F32):
    T, W = x.shape
    has_res = res is not None

    def body(*refs):
        x_ref, w_ref, a_ref, b_ref = refs[:4]
        y_ref = refs[-1]
        seg = jnp.minimum(pl.program_id(0), 1)
        xv = x_ref[...]
        rstd = lax.rsqrt(jnp.mean(xv * xv, axis=-1, keepdims=True) + RMS_EPS)
        y = a_ref[pl.ds(seg, 1), :] * (xv * rstd * w_ref[...]) + b_ref[pl.ds(seg, 1), :]
        y_ref[...] = (y + refs[4][...] if has_res else y).astype(y_ref.dtype)

    row = pl.BlockSpec((tr, W), lambda i: (i, 0))
    small = pl.BlockSpec((8, W), lambda i: (0, 0))
    return pl.pallas_call(
        body, name="norm_fwd",
        out_shape=jax.ShapeDtypeStruct((T, W), out_dtype),
        grid=(T // tr,),
        in_specs=[row, pl.BlockSpec((1, W), lambda i: (0, 0)), small, small] + ([row] if has_res else []),
        out_specs=row,
        compiler_params=_params(("parallel",)),
    )(*((x, w.reshape(1, W), a2, b2) + ((res,) if has_res else ())))


def _norm_bwd_call(x, w, a2, dy, tr, add=None, out_dtype=F32):
    T, W = x.shape
    has_add = add is not None

    def body(*refs):
        x_ref, w_ref, a_ref, dy_ref = refs[:4]
        dx_ref, dw_ref, da_ref, db_ref = refs[-4:]
        i = pl.program_id(0)
        seg = jnp.minimum(i, 1)

        @pl.when(i == 0)
        def _():
            dw_ref[...] = jnp.zeros_like(dw_ref)
            da_ref[...] = jnp.zeros_like(da_ref)
            db_ref[...] = jnp.zeros_like(db_ref)

        xv = x_ref[...]
        g = dy_ref[...]
        wv = w_ref[...]
        rstd = lax.rsqrt(jnp.mean(xv * xv, axis=-1, keepdims=True) + RMS_EPS)
        xh = xv * rstd
        da_ref[pl.ds(seg, 1), :] += jnp.sum(g * (xh * wv), axis=0, keepdims=True)
        db_ref[pl.ds(seg, 1), :] += jnp.sum(g, axis=0, keepdims=True)
        gy = g * a_ref[pl.ds(seg, 1), :]
        dw_ref[0:1, :] += jnp.sum(gy * xh, axis=0, keepdims=True)
        gx = gy * wv
        dx = rstd * (gx - xh * jnp.mean(gx * xh, axis=-1, keepdims=True))
        dx_ref[...] = (dx + refs[4][...] if has_add else dx).astype(dx_ref.dtype)

    acc = jax.ShapeDtypeStruct((8, W), F32)
    acc_spec = pl.BlockSpec((8, W), lambda i: (0, 0))
    row = pl.BlockSpec((tr, W), lambda i: (i, 0))
    return pl.pallas_call(
        body, name="norm_bwd",
        out_shape=(jax.ShapeDtypeStruct((T, W), out_dtype), acc, acc, acc),
        grid=(T // tr,),
        in_specs=[row, pl.BlockSpec((1, W), lambda i: (0, 0)), acc_spec, row] + ([row] if has_add else []),
        out_specs=(row, acc_spec, acc_spec, acc_spec),
        compiler_params=_params(("arbitrary",)),
    )(*((x, w.reshape(1, W), a2, dy) + ((add,) if has_add else ())))


def _act_call(u1, u2, dact=None):
    T, W = u1.shape
    tr = _pick(T, (512, 256, 128, 64))
    tn = _pick(W, (1408, 512, 256, 128))
    spec = pl.BlockSpec((tr, tn), lambda i, j: (i, j))
    sh = jax.ShapeDtypeStruct((T, W), BF16)
    if dact is None:
        def body(a_ref, b_ref, o_ref):
            a = a_ref[...]
            o_ref[...] = (a * _sig(a) * b_ref[...]).astype(o_ref.dtype)

        return pl.pallas_call(body, name="act_fwd", out_shape=sh, grid=(T // tr, W // tn), in_specs=[spec, spec],
                              out_specs=spec, compiler_params=_params(("parallel", "parallel")))(u1, u2)

    def body(a_ref, b_ref, g_ref, da_ref, db_ref):
        a, g = a_ref[...], g_ref[...]
        s = _sig(a)
        da_ref[...] = (g * b_ref[...] * (s * (1.0 + a * (1.0 - s)))).astype(da_ref.dtype)
        db_ref[...] = (g * a * s).astype(db_ref.dtype)

    return pl.pallas_call(body, name="act_bwd", out_shape=(sh, sh), grid=(T // tr, W // tn),
                          in_specs=[spec, spec, spec], out_specs=(spec, spec),
                          compiler_params=_params(("parallel", "parallel")))(u1, u2, dact)


def _conv_specs(T, Wc, tr):
    hb, nt = tr // 8, T // tr
    row = pl.BlockSpec((tr, Wc), lambda i: (i, 0))
    prev = pl.BlockSpec((8, Wc), lambda i: (jnp.maximum(i * hb - 1, 0), 0))
    nxt = pl.BlockSpec((8, Wc), lambda i: (jnp.minimum((i + 1) * hb, T // 8 - 1), 0))
    return row, prev, nxt, nt


def _fill_ext(dst_ref, cur_ref, prev_ref, next_ref, i, nt, tr):
    has_prev = (i > 1).astype(F32)
    has_next = jnp.logical_and(i > 0, i < nt - 1).astype(F32)
    dst_ref[8:16, :] = prev_ref[...] * has_prev
    dst_ref[16:16 + tr, :] = cur_ref[...]
    dst_ref[16 + tr:24 + tr, :] = next_ref[...] * has_next


def _conv_fwd_call(px, w8, b, tr):
    T, Wc = px.shape
    row, prev, nxt, nt = _conv_specs(T, Wc, tr)

    def body(x_ref, xp_ref, xn_ref, w_ref, b_ref, u_ref, xe_ref):
        i = pl.program_id(0)

        @pl.when(i == 0)
        def _():
            xe_ref[...] = jnp.zeros_like(xe_ref)

        _fill_ext(xe_ref, x_ref, xp_ref, xn_ref, i, nt, tr)
        y = b_ref[...] + w_ref[0:1, :] * xe_ref[pl.ds(14, tr), :]
        for k in range(1, 5):
            y = y + w_ref[k:k + 1, :] * xe_ref[pl.ds(14 + k, tr), :]
        u_ref[...] = y * _sig(y)

    return pl.pallas_call(
        body, name="conv_fwd", out_shape=jax.ShapeDtypeStruct((T, Wc), F32), grid=(nt,),
        in_specs=[row, prev, nxt, pl.BlockSpec((8, Wc), lambda i: (0, 0)), pl.BlockSpec((1, Wc), lambda i: (0, 0))],
        out_specs=row, scratch_shapes=[pltpu.VMEM((tr + 32, Wc), F32)],
        compiler_params=_params(("arbitrary",)),
    )(px, px, px, w8, b)


def _conv_bwd_call(px, w8, b, du, tr):
    T, Wc = px.shape
    row, prev, nxt, nt = _conv_specs(T, Wc, tr)
    E = tr + 16

    def body(x_ref, xp_ref, xn_ref, g_ref, gp_ref, gn_ref, w_ref, b_ref, dx_ref, dw_ref, db_ref,
             xe_ref, ge_ref, dy_ref):
        i = pl.program_id(0)

        @pl.when(i == 0)
        def _():
            xe_ref[...] = jnp.zeros_like(xe_ref)
            ge_ref[...] = jnp.zeros_like(ge_ref)
            dy_ref[...] = jnp.zeros_like(dy_ref)
            dw_ref[...] = jnp.zeros_like(dw_ref)
            db_ref[...] = jnp.zeros_like(db_ref)

        _fill_ext(xe_ref, x_ref, xp_ref, xn_ref, i, nt, tr)
        _fill_ext(ge_ref, g_ref, gp_ref, gn_ref, i, nt, tr)
        y = b_ref[...] + w_ref[0:1, :] * xe_ref[pl.ds(6, E), :]
        for k in range(1, 5):
            y = y + w_ref[k:k + 1, :] * xe_ref[pl.ds(6 + k, E), :]
        s = _sig(y)
        dy = ge_ref[pl.ds(8, E), :] * (s * (1.0 + y * (1.0 - s)))
        dy_ref[pl.ds(8, E), :] = dy
        dx = w_ref[0:1, :] * dy_ref[pl.ds(18, tr), :]
        for k in range(1, 5):
            dx = dx + w_ref[k:k + 1, :] * dy_ref[pl.ds(18 - k, tr), :]
        dx_ref[...] = dx.astype(dx_ref.dtype)
        dyt = dy_ref[pl.ds(16, tr), :]
        db_ref[0:1, :] += jnp.sum(dyt, axis=0, keepdims=True)
        for k in range(5):
            dw_ref[k:k + 1, :] += jnp.sum(dyt * xe_ref[pl.ds(14 + k, tr), :], axis=0, keepdims=True)

    acc = jax.ShapeDtypeStruct((8, Wc), F32)
    acc_spec = pl.BlockSpec((8, Wc), lambda i: (0, 0))
    ext = pltpu.VMEM((tr + 32, Wc), F32)
    return pl.pallas_call(
        body, name="conv_bwd", out_shape=(jax.ShapeDtypeStruct((T, Wc), BF16), acc, acc), grid=(nt,),
        in_specs=[row, prev, nxt, row, prev, nxt, acc_spec, pl.BlockSpec((1, Wc), lambda i: (0, 0))],
        out_specs=(row, acc_spec, acc_spec), scratch_shapes=[ext, ext, ext],
        compiler_params=_params(("arbitrary",)),
    )(px, px, px, du, du, du, w8, b)


_SCAN_CFG = {
    "gla": dict(H=4, Dk=32, Dv=64, nh=4, scalar=False, C=64),
    "ssd": dict(H=8, Dk=128, Dv=64, nh=2, scalar=True, C=128),
    "ret": dict(H=4, Dk=64, Dv=64, nh=4, scalar=True, C=128),
}
GPAD = 8


def _log2(n):
    r = int(math.log2(n))
    assert 1 << r == n
    return r


class _ScanMath:
    def __init__(self, cfg, reverse):
        C = cfg["C"]
        self.C, self.reverse = C, reverse
        self.Dk, self.Dv, self.nh, self.scalar = cfg["Dk"], cfg["Dv"], cfg["nh"], cfg["scalar"]
        self.Wk, self.Wv = self.nh * self.Dk, self.nh * self.Dv
        self.nsg = cfg["H"] // self.nh
        nh, Wk, Wv = self.nh, self.Wk, self.Wv
        lk, lv, lc = _log2(self.Dk), _log2(self.Dv), _log2(C)
        r, c = _iota((C, C), 0), _iota((C, C), 1)
        self.L = ((c >= r) if reverse else (c <= r)).astype(F32)
        self.Lsuf = ((c <= r) if reverse else (c >= r)).astype(F32)
        i, j = _iota((C, nh * C), 0), _iota((C, nh * C), 1) & (C - 1)
        self.Mst = (j >= i) if reverse else (j <= i)
        self.Dj = (i == j).astype(F32)
        self.km = [((_iota((1, Wk), 1) >> lk) == h).astype(F32) for h in range(nh)]
        self.vm = [((_iota((1, Wv), 1) >> lv) == h).astype(F32) for h in range(nh)]
        self.BD = ((_iota((Wv, Wk), 0) >> lv) == (_iota((Wv, Wk), 1) >> lk)).astype(F32)
        self.last = 0 if reverse else C - 1
        self.last_row = (_iota((C, 1), 0) == self.last).astype(F32)
        self.lk, self.lc = lk, lc
        self.H = cfg["H"]

    def gates(self, g):
        if not self.scalar:
            return _dot_sel(g, self.L, _NN, x_left=False), None
        G8 = _dot_sel(g, self.L, _NN, x_left=False)
        nk, ncol = self.H * self.Dk, self.H * self.C
        ek = (_iota((GPAD, nk), 0) == (_iota((GPAD, nk), 1) >> self.lk)).astype(F32)
        ec = (_iota((GPAD, ncol), 0) == (_iota((GPAD, ncol), 1) >> self.lc)).astype(F32)
        return _dot_sel(G8, ek, _NN), _dot_sel(G8, ec, _NN)

    def Ek(self, s):
        return (_iota((GPAD, self.Wk), 0) == (_iota((GPAD, self.Wk), 1) >> self.lk) + s * self.nh).astype(F32)

    def kstack(self, x):
        return jnp.concatenate([x * self.km[h] for h in range(self.nh)], axis=0)

    def vstack(self, x):
        return jnp.concatenate([x * self.vm[h] for h in range(self.nh)], axis=0)

    def unstack(self, R, masks):
        C = self.C
        out = R[0:C] * masks[0]
        for h in range(1, self.nh):
            out = out + R[h * C:(h + 1) * C] * masks[h]
        return out

    def chunk(self, qs, ks, Gk, Gc):
        C = self.C
        Glast = Gk[self.last:self.last + 1, :]
        out = dict(Gk=Gk, Glast=Glast, eG=jnp.exp(Gk), eGl=jnp.exp(Glast - Gk), eGlast=jnp.exp(Glast))
        if self.scalar:
            Gr = jnp.sum(Gc * self.Dj, axis=0, keepdims=True)
            dec = jnp.where(self.Mst, jnp.exp(jnp.minimum(Gc - Gr, 0.0)), 0.0)
            qt, kt = qs, ks
            A = _dot(_bf(qt), _bf(self.kstack(kt)), _NT) * dec
            out.update(dec=dec, qt=qt, kt=kt, A=A)
        else:
            Gm = Gk[C // 2:C // 2 + 1, :]
            eq, ek = jnp.exp(Gk - Gm), jnp.exp(Gm - Gk)
            qt, kt = qs * eq, ks * ek
            A = jnp.where(self.Mst, _dot(_bf(qt), _bf(self.kstack(kt)), _NT), 0.0)
            out.update(eq=eq, ek=ek, qt=qt, kt=kt, A=A)
        return out


def _chunk_index(p, n, nc, reverse):
    if not reverse:
        return p
    return jnp.where(p < nc, nc - 1 - p, n - 1 + nc - p)


def _scan_dims(kind):
    cfg = _SCAN_CFG[kind]
    HK, HV = cfg["H"] * cfg["Dk"], cfg["H"] * cfg["Dv"]
    return cfg, cfg["C"], HK, HV, (GPAD if cfg["scalar"] else HK)


def _scan_fwd_step(m, q_ref, k_ref, v_ref, g_ref, o_ref, st_ref, S_ref):
    C = m.C

    @pl.when(pl.program_id(0) == 0)
    def _():
        S_ref[...] = jnp.zeros_like(S_ref)

    Gk_all, Gc_all = m.gates(g_ref[...])
    for s in range(m.nsg):
        ksl, vsl = slice(s * m.Wk, (s + 1) * m.Wk), slice(s * m.Wv, (s + 1) * m.Wv)
        csl = slice(s * m.nh * C, (s + 1) * m.nh * C)
        qs, ks, vs = q_ref[:, ksl], k_ref[:, ksl], v_ref[:, vsl]
        ch = m.chunk(qs, ks, Gk_all[:, ksl], Gc_all[:, csl] if m.scalar else None)
        S = S_ref[vsl, :]
        o = _dot(_bf(ch["A"]), _bf(m.vstack(vs)), _NN) + _dot(_bf(qs * ch["eG"]), _bf(S), _NT)
        o_ref[:, vsl] = o
        st_ref[0, vsl, :] = S
        S_ref[vsl, :] = S * ch["eGlast"] + _dot(_bf(vs), _bf(ks * ch["eGl"]), _TN) * m.BD


def _scan_fwd_call(kind, ops, Tc, comm=None):
    cfg, C, HK, HV, GW = _scan_dims(kind)
    T = ops[False][0][0].shape[0]
    n, nc = T // C, Tc // C

    def body(*refs):
        for d, rev in enumerate((False, True)):
            _scan_fwd_step(_ScanMath(cfg, rev), *refs[4 * d:4 * d + 4], *refs[8 + 2 * d:10 + 2 * d], refs[12 + d])

    sg = cfg["H"] // cfg["nh"]
    Wk, Wv = cfg["nh"] * cfg["Dk"], cfg["nh"] * cfg["Dv"]
    col = lambda rev, w, j: pl.BlockSpec((C, w), lambda p: (_chunk_index(p, n, nc, rev), j))
    st_spec = lambda rev: pl.BlockSpec((1, sg * Wv, Wk), lambda p: (_chunk_index(p, n, nc, rev), 0, 0))
    in_specs, args, out_specs, out_shape = [], [], [], []
    for rev in (False, True):
        q, k, v, g = ops[rev]
        in_specs += [col(rev, HK, q[1]), col(rev, HK, k[1]), col(rev, HV, v[1]), col(rev, GW, g[1])]
        args += [q[0], k[0], v[0], g[0]]
        out_specs += [col(rev, HV, 0), st_spec(rev)]
        out_shape += [jax.ShapeDtypeStruct((T, HV), F32), jax.ShapeDtypeStruct((n, sg * Wv, Wk), F32)]
    res, got = _pcall(body, name=f"scan_fwd_{kind}", out_shape=out_shape, grid=(n,), in_specs=in_specs,
                      out_specs=out_specs, scratch_shapes=[pltpu.VMEM((sg * Wv, Wk), F32)] * 2,
                      sem=("arbitrary",), args=args, comm=comm)
    return {False: (res[0], res[1]), True: (res[2], res[3])}, got


def _scan_bwd_step(m, need_dg, q_ref, k_ref, v_ref, g_ref, st_ref, do_ref, dq_ref, dk_ref, dv_ref, dg_ref, dS_ref):
    C = m.C

    @pl.when(pl.program_id(0) == 0)
    def _():
        dS_ref[...] = jnp.zeros_like(dS_ref)

    x8 = jnp.zeros((C, GPAD), F32)
    Gk_all, Gc_all = m.gates(g_ref[...])
    for s in range(m.nsg):
        ksl, vsl = slice(s * m.Wk, (s + 1) * m.Wk), slice(s * m.Wv, (s + 1) * m.Wv)
        csl = slice(s * m.nh * C, (s + 1) * m.nh * C)
        qs, ks, vs, dos = q_ref[:, ksl], k_ref[:, ksl], v_ref[:, vsl], do_ref[:, vsl]
        ch = m.chunk(qs, ks, Gk_all[:, ksl], Gc_all[:, csl] if m.scalar else None)
        S = st_ref[0, vsl, :]
        dS = dS_ref[vsl, :]
        A, qt, kt = ch["A"], ch["qt"], ch["kt"]
        dA = _dot(_bf(dos), _bf(m.vstack(vs)), _NT)
        dAm = dA * ch["dec"] if m.scalar else jnp.where(m.Mst, dA, 0.0)
        kst = _bf(m.kstack(kt))
        dv = m.unstack(_dot(_bf(A), _bf(dos), _TN), m.vm) + _dot(_bf(ks * ch["eGl"]), _bf(dS), _NT)
        dv_ref[:, vsl] = dv
        dq_i = _dot(_bf(dAm), kst, _NN)
        dq_x = ch["eG"] * _dot(_bf(dos), _bf(S), _NN)
        dq_ref[:, ksl] = (dq_i if m.scalar else dq_i * ch["eq"]) + dq_x
        dk_i = m.unstack(_dot(_bf(dAm), _bf(qt), _TN), m.km)
        dk_x = ch["eGl"] * _dot(_bf(vs), _bf(dS), _NN)
        dk_ref[:, ksl] = (dk_i if m.scalar else dk_i * ch["ek"]) + dk_x
        if need_dg:
            bnd = (ch["eGlast"] * jnp.sum(dS * S, axis=0, keepdims=True)
                   + jnp.sum(ks * dk_x, axis=0, keepdims=True))
            X = (_bf(qt).astype(F32) * dq_i - _bf(kt).astype(F32) * dk_i) + (qs * dq_x - ks * dk_x)
            X = X + m.last_row * bnd
            if m.scalar:
                x8 = x8 + _dot_sel(X, m.Ek(s), _NT)
            else:
                dg_ref[:, ksl] = _dot_sel(X, m.Lsuf, _NN, x_left=False)
        dS_ref[vsl, :] = dS * ch["eGlast"] + _dot(_bf(dos), _bf(qs * ch["eG"]), _TN) * m.BD
    if m.scalar:
        dg_ref[...] = _dot_sel(x8, m.Lsuf, _NN, x_left=False)
    elif not need_dg:
        dg_ref[...] = jnp.zeros_like(dg_ref)


def _scan_bwd_call(kind, need_dg, ops, st, do, Tc, comm=None):
    cfg, C, HK, HV, GW = _scan_dims(kind)
    T = ops[False][0][0].shape[0]
    n, nc = T // C, Tc // C

    def body(*refs):
        for d, rev in enumerate((False, True)):
            _scan_bwd_step(_ScanMath(cfg, rev), need_dg, *refs[6 * d:6 * d + 6], *refs[12 + 4 * d:16 + 4 * d],
                           refs[20 + d])

    sg = cfg["H"] // cfg["nh"]
    Wk, Wv = cfg["nh"] * cfg["Dk"], cfg["nh"] * cfg["Dv"]
    col = lambda rev, w, j: pl.BlockSpec((C, w), lambda p: (_chunk_index(n - 1 - p, n, nc, rev), j))
    st_spec = lambda rev: pl.BlockSpec((1, sg * Wv, Wk), lambda p: (_chunk_index(n - 1 - p, n, nc, rev), 0, 0))
    in_specs, args, out_specs, out_shape = [], [], [], []
    for rev in (False, True):
        q, k, v, g = ops[rev]
        in_specs += [col(rev, HK, q[1]), col(rev, HK, k[1]), col(rev, HV, v[1]), col(rev, GW, g[1]),
                     st_spec(rev), col(rev, HV, 0)]
        args += [q[0], k[0], v[0], g[0], st[rev], do]
        out_specs += [col(rev, HK, 0), col(rev, HK, 0), col(rev, HV, 0), col(rev, GW, 0)]
        out_shape += [jax.ShapeDtypeStruct((T, w), F32) for w in (HK, HK, HV, GW)]
    res, got = _pcall(body, name=f"scan_bwd_{kind}", out_shape=out_shape, grid=(n,), in_specs=in_specs,
                      out_specs=out_specs, scratch_shapes=[pltpu.VMEM((sg * Wv, Wk), F32)] * 2,
                      sem=("arbitrary",), args=args, comm=comm)
    return {False: res[0:4], True: res[4:8]}, got


def _prep_consts():
    r, c = _iota((256, 256), 0), _iota((256, 256), 1)
    first = (c & 63) < 32
    rope_perm = jnp.where(first, -(r == c + 32).astype(F32), (r == c - 32).astype(F32))
    sel_f = (_iota((128, GPAD), 0) == _iota((128, GPAD), 1) + 32).astype(F32)
    sel_b = (_iota((128, GPAD), 0) == _iota((128, GPAD), 1) + 40).astype(F32)
    ek = (_iota((GPAD, 1024), 0) == (_iota((GPAD, 1024), 1) >> 7)).astype(F32)
    return rope_perm, sel_f, sel_b, ek


def _prep_tile(misc, gq, rq, rk, bm, cm, cosE, sinE, Wg, gbias, dtbf, dtbb, nAf, nAb):
    rope_perm, sel_f, sel_b, ek = _prep_consts()
    logg = _log_sigmoid(_mm_bf(misc, Wg) + gbias) * (1.0 / GLA_TAU)
    a_gla = jnp.concatenate([gq * (32 ** -0.5), logg], axis=1)
    rot = lambda t: t * cosE + _sel(t, rope_perm) * sinE
    a_ret = jnp.concatenate([rot(rq * (64 ** -0.5)), rot(rk)], axis=1)
    dtf = _softplus(_sel(misc, sel_f) + dtbf)
    dtb = _softplus(_sel(misc, sel_b) + dtbb)
    rep = lambda t: jnp.concatenate([t[:, :128]] * 4 + [t[:, 128:]] * 4, axis=1)
    bmr = rep(bm)
    return a_gla, a_ret, rep(cm), bmr * _sel(dtf, ek), bmr * _sel(dtb, ek), dtf * nAf, dtb * nAb


def _prep_row_specs(tr):
    blk = lambda w, j: pl.BlockSpec((tr, w), lambda i: (i, j))
    return [blk(128, MISC // 128), blk(128, _ROFF["gla_q"] // 128), blk(256, _ROFF["ret_q"] // 256),
            blk(256, _ROFF["ret_k"] // 256), blk(256, 2), blk(256, 3), blk(256, 0), blk(256, 0)]


def _whole(a):
    return pl.BlockSpec(a.shape, lambda i: (0,) * a.ndim)


def _prep_fwd_call(Pr, u, cosE, sinE, pp, tr):
    T = Pr.shape[0]
    n_row = 8

    def body(*refs):
        outs = _prep_tile(*[r[...] for r in refs[:n_row + len(pp)]])
        for o_ref, o in zip(refs[n_row + len(pp):], outs):
            o_ref[...] = o

    widths = [384, 512, 1024, 1024, 1024, GPAD, GPAD]
    return pl.pallas_call(
        body, name="prep_fwd", grid=(T // tr,),
        out_shape=tuple(jax.ShapeDtypeStruct((T, w), F32) for w in widths),
        in_specs=_prep_row_specs(tr) + [_whole(p) for p in pp],
        out_specs=tuple(pl.BlockSpec((tr, w), lambda i: (i, 0)) for w in widths),
        compiler_params=_params(("parallel",)),
    )(Pr, Pr, Pr, Pr, u, u, cosE, sinE, *pp)


def _prep_bwd_call(Pr, u, cosE, sinE, pp, cts, tr, comm=None):
    T = Pr.shape[0]
    n_row, n_p = 8, len(pp)
    names = ["gla_dq_f", "gla_dq_b", "gla_dg_f", "gla_dg_b", "gla_dk_f", "gla_dk_b", "gla_dv_f", "gla_dv_b",
             "ret_dq_f", "ret_dq_b", "ret_dk_f", "ret_dk_b", "ret_dv_f", "ret_dv_b",
             "ssd_dq_f", "ssd_dq_b", "ssd_dk_f", "ssd_dk_b", "ssd_dg_f", "ssd_dg_b", "ssd_dv_f", "ssd_dv_b",
             "d_r", "d_z", "d_gr", "d_xs"]
    ct_arrays = [cts[n] for n in names]

    def body(*refs):
        ins = [r[...] for r in refs[:n_row + n_p]]
        c = {n: r[...] for n, r in zip(names, refs[n_row + n_p:n_row + n_p + len(names)])}
        dPr_ref, du_ref = refs[n_row + n_p + len(names):n_row + n_p + len(names) + 2]
        dp_refs = refs[n_row + n_p + len(names) + 2:]
        _, vjp = jax.vjp(_prep_tile, *ins)
        ct_out = (jnp.concatenate([c["gla_dq_f"] + c["gla_dq_b"], c["gla_dg_f"], c["gla_dg_b"]], axis=1),
                  jnp.concatenate([c["ret_dq_f"] + c["ret_dq_b"], c["ret_dk_f"] + c["ret_dk_b"]], axis=1),
                  c["ssd_dq_f"] + c["ssd_dq_b"], c["ssd_dk_f"], c["ssd_dk_b"], c["ssd_dg_f"], c["ssd_dg_b"])
        d = vjp(ct_out)
        d_misc, d_gq, d_rq, d_rk, d_bm, d_cm = d[:6]
        dPr_ref[...] = jnp.concatenate(
            [c["gla_dv_f"] + c["gla_dv_b"], c["d_r"], d_rq, d_rk, c["ret_dv_f"] + c["ret_dv_b"], c["d_gr"],
             c["d_z"], d_gq, c["gla_dk_f"] + c["gla_dk_b"], d_misc,
             jnp.zeros((d_misc.shape[0], R_W - MISC - 128), F32)], axis=1).astype(dPr_ref.dtype)
        du_ref[...] = jnp.concatenate([c["ssd_dv_f"] + c["ssd_dv_b"] + c["d_xs"], d_bm, d_cm], axis=1)

        @pl.when(pl.program_id(0) == 0)
        def _():
            for r in dp_refs:
                r[...] = jnp.zeros_like(r)

        for r, g in zip(dp_refs, d[n_row:]):
            r[...] += g

    row = lambda a: pl.BlockSpec((tr, a.shape[1]), lambda i: (i, 0))
    return _pcall(
        body, name="prep_bwd", grid=(T // tr,),
        out_shape=(jax.ShapeDtypeStruct((T, R_W), BF16), jax.ShapeDtypeStruct((T, 1024), F32))
        + tuple(jax.ShapeDtypeStruct(p.shape, F32) for p in pp),
        in_specs=_prep_row_specs(tr) + [_whole(p) for p in pp] + [row(a) for a in ct_arrays],
        out_specs=(pl.BlockSpec((tr, R_W), lambda i: (i, 0)), pl.BlockSpec((tr, 1024), lambda i: (i, 0)))
        + tuple(_whole(p) for p in pp),
        scratch_shapes=[], sem=("arbitrary",), args=(Pr, Pr, Pr, Pr, u, u, cosE, sinE, *pp, *ct_arrays), comm=comm)


def _post_tile(ogf, ogb, r, ysf, ysb, xs, z, orf, orb, gr, gla_n, dexp, ssd_n, ret_n):
    bd = ((_iota((256, 256), 0) >> 6) == (_iota((256, 256), 1) >> 6)).astype(F32)
    og = ogf + ogb
    gla = og * lax.rsqrt(_sel(og * og, bd) * (1.0 / 64) + RMS_EPS) * gla_n * _silu(r)
    t = (ysf + ysb + dexp * xs) * _silu(z)
    ssd = t * lax.rsqrt(jnp.mean(t * t, axis=-1, keepdims=True) + RMS_EPS) * ssd_n
    o = orf + orb
    oc = o - _sel(o, bd) * (1.0 / 64)
    ret = oc * lax.rsqrt(_sel(oc * oc, bd) * (1.0 / 64) + RMS_EPS) * ret_n * _silu(gr)
    return jnp.concatenate([gla, ssd, ret], axis=1)


def _post_row_specs(tr):
    blk = lambda w, j: pl.BlockSpec((tr, w), lambda i: (i, j))
    return [blk(256, 0), blk(256, 0), blk(256, _ROFF["gla_r"] // 256), blk(512, 0), blk(512, 0), blk(512, 0),
            blk(512, _ROFF["ssd_z"] // 512), blk(256, 0), blk(256, 0), blk(256, _ROFF["ret_g"] // 256)]


def _post_fwd_call(rows, qp, tr, comm=None):
    T = rows[0].shape[0]

    def body(*refs):
        refs[-1][...] = _post_tile(*[r[...] for r in refs[:-1]]).astype(refs[-1].dtype)

    res, got = _pcall(body, name="post_fwd", grid=(T // tr,), out_shape=[jax.ShapeDtypeStruct((T, D), BF16)],
                      in_specs=_post_row_specs(tr) + [_whole(p) for p in qp],
                      out_specs=[pl.BlockSpec((tr, D), lambda i: (i, 0))], scratch_shapes=[],
                      sem=("parallel",), args=(*rows, *qp), comm=comm)
    return res[0], got


def _post_bwd_call(rows, qp, dmixed, tr):
    T = rows[0].shape[0]
    n_in = 10 + len(qp)

    def body(*refs):
        ins = [r[...] for r in refs[:n_in]]
        _, vjp = jax.vjp(_post_tile, *ins)
        d = vjp(refs[n_in][...])
        outs = refs[n_in + 1:]
        for o_ref, g in zip(outs[:7], (d[0], d[3], d[7], d[2], d[6], d[9], d[5])):
            o_ref[...] = g.astype(o_ref.dtype)

        @pl.when(pl.program_id(0) == 0)
        def _():
            for r in outs[7:]:
                r[...] = jnp.zeros_like(r)

        for r, g in zip(outs[7:], d[10:]):
            r[...] += g

    widths = [256, 512, 256, 256, 512, 256, 512]
    dts = [BF16] * 3 + [F32] * 4
    return pl.pallas_call(
        body, name="post_bwd", grid=(T // tr,),
        out_shape=tuple(jax.ShapeDtypeStruct((T, w), dt) for w, dt in zip(widths, dts))
        + tuple(jax.ShapeDtypeStruct(p.shape, F32) for p in qp),
        in_specs=_post_row_specs(tr) + [_whole(p) for p in qp] + [pl.BlockSpec((tr, D), lambda i: (i, 0))],
        out_specs=tuple(pl.BlockSpec((tr, w), lambda i: (i, 0)) for w in widths) + tuple(_whole(p) for p in qp),
        compiler_params=_params(("arbitrary",)),
    )(*rows, *qp, dmixed)


def _mixer_scan_operands(Pr, u, a_gla, a_ret, cmr, kf, kb, g8f, g8b, lg):
    gk, gv = (Pr, _ROFF["gla_k"] // 128), (Pr, _ROFF["gla_v"] // 256)
    rv = (Pr, _ROFF["ret_v"] // 256)
    return {
        "gla": {False: ((a_gla, 0), gk, gv, (a_gla, 1)), True: ((a_gla, 0), gk, gv, (a_gla, 2))},
        "ret": {False: ((a_ret, 0), (a_ret, 1), rv, (lg, 0)), True: ((a_ret, 0), (a_ret, 1), rv, (lg, 0))},
        "ssd": {False: ((cmr, 0), (kf, 0), (u, 0), (g8f, 0)), True: ((cmr, 0), (kb, 0), (u, 0), (g8b, 0))},
    }


def _post_rows(o, Pr, u):
    return [o["gla"][False][0], o["gla"][True][0], Pr, o["ssd"][False][0], o["ssd"][True][0], u, Pr,
            o["ret"][False][0], o["ret"][True][0], Pr]


def _mixer_forward(Tc, Pr, Px, cn, pp, cw8, cb, qp, comm):
    cosE, sinE, lg = cn
    u = _conv_fwd_call(Px, cw8, cb, Tc)
    prep = _prep_fwd_call(Pr, u, cosE, sinE, pp, Tc)
    ops = _mixer_scan_operands(Pr, u, *prep, lg)
    o, got = {}, {}
    for kind in ops:
        o[kind], got[kind] = _scan_fwd_call(kind, ops[kind], Tc, comm.get(kind))
    mixed, got["post"] = _post_fwd_call(_post_rows(o, Pr, u), qp, Tc, comm.get("post"))
    return mixed, (u, prep, o), got


def _mixer_backward(Tc, Pr, Px, cn, pp, cw8, cb, qp, saved, dmixed, comm):
    cosE, sinE, lg = cn
    u, prep, o = saved
    post = _post_bwd_call(_post_rows(o, Pr, u), qp, dmixed, Tc)
    d_o = dict(gla=post[0], ssd=post[1], ret=post[2])
    cts = dict(d_r=post[3], d_z=post[4], d_gr=post[5], d_xs=post[6])
    ops = _mixer_scan_operands(Pr, u, *prep, lg)
    got = {}
    for kind in ops:
        st = {rev: o[kind][rev][1] for rev in (False, True)}
        res, got[kind] = _scan_bwd_call(kind, kind != "ret", ops[kind], st, d_o[kind], Tc, comm.get(kind))
        for rev, sfx in ((False, "_f"), (True, "_b")):
            for nm, a in zip(("_dq", "_dk", "_dv", "_dg"), res[rev]):
                cts[kind + nm + sfx] = a
    pb, got["prep"] = _prep_bwd_call(Pr, u, cosE, sinE, pp, cts, Tc, comm.get("prep"))
    dPx, dcw8, dcb = _conv_bwd_call(Px, cw8, cb, pb[1], Tc)
    return pb[0], dPx, tuple(pb[2:]), dcw8, dcb[0:1], tuple(post[7:]), got


def _mix_fwd(Tc, X, w, cn, nw, mods, pp, cw8, cb, qp, comm):
    h = _norm_fwd_call(X, nw[0], mods[0], mods[1], None, Tc, BF16)
    Px, Pr = _mm(h, w[0], name="mm_fwd"), _mm(h, w[1], name="mm_fwd")
    mixed, saved, got = _mixer_forward(Tc, Pr, Px, cn, pp, cw8, cb, qp, comm)
    M = _mm(mixed, w[2], name="mm_fwd")
    Xn = _norm_fwd_call(M, nw[1], mods[2], jnp.zeros_like(mods[2]), X, Tc)
    return Xn, (X, nw, mods, w, cn, pp, cw8, cb, qp, h, Px, Pr, mixed, saved, M), got


def _mix_bwd(Tc, res, dXn, comm):
    X, nw, mods, w, cn, pp, cw8, cb, qp, h, Px, Pr, mixed, saved, M = res
    dM, dnw1, da_post, _ = _norm_bwd_call(M, nw[1], mods[2], dXn, Tc, out_dtype=BF16)
    dmixed = _mm(dM, w[2], trans_b=True, name="mm_dx")
    dPr, dPx, dpp, dcw8, dcb, dqp, got = _mixer_backward(Tc, Pr, Px, cn, pp, cw8, cb, qp, saved, dmixed, comm)
    dh = _mm(dPx, w[0], trans_b=True, name="mm_dx")
    dh = _mm(dPr, w[1], trans_b=True, name="mm_dx_acc", add=dh)
    dX, dnw0, da_pre, db_pre = _norm_bwd_call(X, nw[0], mods[0], dh, Tc, add=dXn)
    dW = tuple(_mm_tn(a, g, name="mm_dw", out_dtype=BF16) for a, g in ((h, dPx), (h, dPr), (mixed, dM)))
    return dX, ((dnw0[0], dnw1[0]), (da_pre, db_pre, da_post), dpp, dcw8, dcb, dqp), dW, got


def _ffn_fwd(Tc, X, w, nw, mods):
    h = _norm_fwd_call(X, nw[0], mods[0], mods[1], None, Tc, BF16)
    U1, U2 = _mm(h, w[0], name="mm_fwd"), _mm(h, w[1], name="mm_fwd")
    act = _act_call(U1, U2)
    Fo = _mm(act, w[2], name="mm_fwd")
    Xn = _norm_fwd_call(Fo, nw[1], mods[2], jnp.zeros_like(mods[2]), X, Tc)
    return Xn, (X, nw, mods, w, h, U1, U2, act, Fo)


def _ffn_bwd(Tc, res, dXn):
    X, nw, mods, w, h, U1, U2, act, Fo = res
    dFo, dnw1, da_post, _ = _norm_bwd_call(Fo, nw[1], mods[2], dXn, Tc, out_dtype=BF16)
    dU1, dU2 = _act_call(U1, U2, _mm(dFo, w[2], trans_b=True, name="mm_dx"))
    dh = _mm(dU1, w[0], trans_b=True, name="mm_dx")
    dh = _mm(dU2, w[1], trans_b=True, name="mm_dx_acc", add=dh)
    dX, dnw0, da_pre, db_pre = _norm_bwd_call(X, nw[0], mods[0], dh, Tc, add=dXn)
    dW = tuple(_mm_tn(a, g, name="mm_dw", out_dtype=BF16) for a, g in ((h, dU1), (h, dU2), (act, dFo)))
    return dX, ((dnw0[0], dnw1[0]), (da_pre, db_pre, da_post)), dW


def _rope_tables(Tl, Tc):
    rows = Tl // GRID_W
    row = jnp.repeat(jnp.arange(rows), GRID_W).astype(F32)
    col = jnp.tile(jnp.arange(GRID_W), rows).astype(F32)
    inv_freq = 10000.0 ** (-jnp.arange(16, dtype=F32) / 16)
    ang = jnp.concatenate([row[:, None] * inv_freq, col[:, None] * inv_freq], axis=-1)
    cos = jnp.concatenate([jnp.ones((Tc, 32), F32), jnp.cos(ang)], axis=0)
    sin = jnp.concatenate([jnp.zeros((Tc, 32), F32), jnp.sin(ang)], axis=0)
    return jnp.tile(cos, (1, 8)), jnp.tile(sin, (1, 8))


def _rows8(first, second):
    z = jnp.zeros((6,) + first.shape, F32)
    return jnp.concatenate([first[None], second[None], z], axis=0)


def _layer_inputs(sp, ml, mc):
    gu = sp["gla_gate_up"]
    Wg = jnp.zeros((128, 256), F32).at[0:16, 0:128].set(gu[0]).at[16:32, 128:256].set(gu[1])
    pp = (Wg, sp["gla_gate_b"].reshape(1, 256), sp["ssd_dt_bias"][0:1], sp["ssd_dt_bias"][1:2],
          -jnp.exp(sp["ssd_a_log"][0:1]), -jnp.exp(sp["ssd_a_log"][1:2]))
    qp = (sp["gla_norm"].reshape(1, 256), jnp.repeat(sp["ssd_d"], 64).reshape(1, 512),
          sp["ssd_norm"].reshape(1, 512), sp["ret_norm"].reshape(1, 256))
    mix = ((sp["norm_mix_pre"], sp["norm_mix_post"]),
           (_rows8(1.0 + mc[1], 1.0 + ml[1]), _rows8(mc[0], ml[0]), _rows8(mc[2], ml[2])),
           pp, jnp.pad(sp["ssd_conv_w"], ((0, 3), (0, 0))), sp["ssd_conv_b"].reshape(1, 1024), qp)
    ffn = ((sp["norm_ffn_pre"], sp["norm_ffn_post"]),
           (_rows8(1.0 + mc[4], 1.0 + ml[4]), _rows8(mc[3], ml[3]), _rows8(mc[5], ml[5])))
    return mix, ffn


def _layer_weights(w_in, w_out, w13, w2):
    w_x, w_r = _split_w_in(w_in)
    return (w_x, w_r, w_out), (w13[:, :FFN_H], w13[:, FFN_H:], w2)


def _layer_weight_grads(dW_mix, dW_ffn):
    return (_merge_w_in(dW_mix[0], dW_mix[1]), dW_mix[2], jnp.concatenate([dW_ffn[0], dW_ffn[1]], axis=-1), dW_ffn[2])


def _from_slabs(got):
    g_in, g_out, g_13, g_2 = got
    cols = lambda g: jnp.moveaxis(g, 0, 1).reshape(g.shape[1], N_DEV * g.shape[2])
    rows = lambda g: g.reshape(N_DEV * g.shape[1], g.shape[2])
    return cols(g_in), rows(g_out), cols(g_13), rows(g_2)


def _to_slabs(full):
    f_in, f_out, f_13, f_2 = full
    cols = lambda f: jnp.moveaxis(f.reshape(f.shape[0], N_DEV, f.shape[1] // N_DEV), 1, 0)
    rows = lambda f: f.reshape(N_DEV, f.shape[0] // N_DEV, f.shape[1])
    return [cols(f_in), rows(f_out), cols(f_13), rows(f_2)]


def _local_step(xcat, target, mod_l, mod_c, sp, Tc, weights=None, shards=None):
    Tt = xcat.shape[0]
    cosE, sinE = _rope_tables(Tt - Tc, Tc)
    log_gamma = jnp.log1p(-jnp.exp2(-5.0 - jnp.arange(4, dtype=F32)))
    lg = jnp.broadcast_to(jnp.concatenate([log_gamma, jnp.zeros((GPAD - 4,), F32)])[None, :], (Tt, GPAD))
    cn = (cosE, sinE, lg)
    fwd_carriers, bwd_carriers = ("gla", "post", "ssd", "ret"), ("gla", "prep", "ssd", "ret")
    X, saved = xcat, []
    if shards is not None:
        full = _from_slabs(_exchange_call(False, shards[0], "gather_layer0"))
    for l in range(DEPTH):
        if shards is None:
            full = weights[l]
        w_mix, w_ffn = _layer_weights(*full)
        (a_mix, a_ffn), pull = jax.vjp(_layer_inputs, {n: sp[n][l] for n in _SMALL},
                                       mod_l[l].reshape(6, D), mod_c[l].reshape(6, D))
        comm = {}
        if shards is not None and l + 1 < DEPTH:
            comm = {c: (False, [s]) for c, s in zip(fwd_carriers, shards[l + 1])}
        X, r_mix, got = _mix_fwd(Tc, X, w_mix, cn, *a_mix, comm)
        X, r_ffn = _ffn_fwd(Tc, X, w_ffn, *a_ffn)
        saved.append((r_mix, r_ffn, pull))
        if comm:
            full = _from_slabs([got[c][0] for c in fwd_carriers])
    loss, dX = _loss_call(X, target, Tc)
    d_sp, d_ml, d_mc, gw = [None] * DEPTH, [None] * DEPTH, [None] * DEPTH, [None] * DEPTH
    pending = None
    for l in reversed(range(DEPTH)):
        r_mix, r_ffn, pull = saved[l]
        dX, c_ffn, dW_ffn = _ffn_bwd(Tc, r_ffn, dX)
        comm = {c: (True, [s]) for c, s in zip(bwd_carriers, pending)} if pending is not None else {}
        dX, c_mix, dW_mix, got = _mix_bwd(Tc, r_mix, dX, comm)
        if comm:
            gw[l + 1] = [got[c][0] for c in bwd_carriers]
        d_sp[l], d_ml[l], d_mc[l] = pull((c_mix, c_ffn))
        grads = _layer_weight_grads(dW_mix, dW_ffn)
        if shards is None:
            gw[l] = grads
        else:
            pending = _to_slabs(grads)
    if shards is not None:
        gw[0] = list(_exchange_call(True, pending, "scatter_layer0"))
    d_sp = {n: jnp.stack([d_sp[l][n] for l in range(DEPTH)]) for n in _SMALL}
    return (loss, dX, jnp.stack(d_ml).reshape(DEPTH, 6 * D), jnp.stack(d_mc).reshape(DEPTH, 6 * D), d_sp, gw)


def _sum8_call(slabs, name):
    _, R, Cc = slabs.shape
    tr = _pick(R, (512, 352, 256, 128, 64, 32, 16))

    def body(*refs):
        acc = refs[0][...].astype(F32)
        for r in refs[1:N_DEV]:
            acc = acc + r[...].astype(F32)
        refs[N_DEV][...] = acc

    return pl.pallas_call(
        body, name=name, grid=(R // tr,), out_shape=jax.ShapeDtypeStruct((R, Cc), F32),
        in_specs=[pl.BlockSpec((None, tr, Cc), lambda i, d=d: (d, i, 0)) for d in range(N_DEV)],
        out_specs=pl.BlockSpec((tr, Cc), lambda i: (i, 0)), compiler_params=_params(("parallel",)),
    )(*([slabs] * N_DEV))


def _loss_call(X, target, Tc):
    Tt, W = X.shape
    tr = Tc
    nt = Tt // tr

    def body(x_ref, t_ref, loss_ref, dx_ref, acc_ref):
        i = pl.program_id(0)

        @pl.when(i == 0)
        def _():
            acc_ref[...] = jnp.zeros_like(acc_ref)
            dx_ref[...] = jnp.zeros_like(dx_ref)

        @pl.when(i > 0)
        def _():
            e = x_ref[...] - t_ref[...]
            dx_ref[...] = e * (1.0 / W)
            acc_ref[...] += jnp.sum(e * e, axis=0, keepdims=True)

        @pl.when(i == nt - 1)
        def _():
            loss_ref[...] = jnp.full(loss_ref.shape, (0.5 / W) * jnp.sum(acc_ref[...]), F32)

    loss, dx = pl.pallas_call(
        body, name="loss",
        out_shape=(jax.ShapeDtypeStruct((8, 128), F32), jax.ShapeDtypeStruct((Tt, W), F32)),
        grid=(nt,),
        in_specs=[pl.BlockSpec((tr, W), lambda i: (i, 0)),
                  pl.BlockSpec((tr, W), lambda i: (jnp.maximum(i - 1, 0), 0))],
        out_specs=(pl.BlockSpec((8, 128), lambda i: (0, 0)), pl.BlockSpec((tr, W), lambda i: (i, 0))),
        scratch_shapes=[pltpu.VMEM((1, W), F32)],
        compiler_params=_params(("arbitrary",)),
    )(X, target)
    return loss[0, 0], dx


def _adamw_call(w, g, m, v, name):
    R, Cc = w.shape
    tr = _pick(R, (512, 352, 256, 128, 64, 32, 16, 8))
    c1 = 1.0 - ADAM_B1 ** ADAM_STEP
    c2 = 1.0 - ADAM_B2 ** ADAM_STEP

    def body(w_ref, g_ref, m_ref, v_ref, d_ref, nm_ref, nv_ref):
        gv = g_ref[...]
        nm = ADAM_B1 * m_ref[...] + (1.0 - ADAM_B1) * gv
        nv = ADAM_B2 * v_ref[...] + (1.0 - ADAM_B2) * (gv * gv)
        d_ref[...] = -ADAM_LR * ((nm / c1) / (jnp.sqrt(nv / c2) + ADAM_EPS) + ADAM_WD * w_ref[...])
        nm_ref[...] = nm
        nv_ref[...] = nv

    spec = pl.BlockSpec((tr, Cc), lambda i: (i, 0))
    sh = jax.ShapeDtypeStruct((R, Cc), F32)
    return pl.pallas_call(
        body, name=name, out_shape=(sh, sh, sh), grid=(R // tr,),
        in_specs=[spec] * 4, out_specs=(spec,) * 3, compiler_params=_params(("parallel",)),
    )(w, g, m, v)


def _sum_call(xs, name, also_bf16=False):
    R, Cc = xs[0].shape
    tr = _pick(R, (512, 352, 256, 128, 64, 32, 16))
    k = len(xs)

    def body(*refs):
        acc = refs[0][...].astype(F32)
        for r in refs[1:k]:
            acc = acc + r[...].astype(F32)
        refs[k][...] = acc
        if also_bf16:
            refs[k + 1][...] = acc.astype(BF16)

    spec = pl.BlockSpec((tr, Cc), lambda i: (i, 0))
    sh = jax.ShapeDtypeStruct((R, Cc), F32)
    return pl.pallas_call(
        body, name=name, grid=(R // tr,), in_specs=[spec] * k,
        out_shape=(sh, jax.ShapeDtypeStruct((R, Cc), BF16)) if also_bf16 else sh,
        out_specs=(spec, spec) if also_bf16 else spec, compiler_params=_params(("parallel",)),
    )(*xs)


MESH = pl.DeviceIdType.MESH
ANY = pl.BlockSpec(memory_space=pl.ANY)


def _me():
    return lax.axis_index("x"), lax.axis_index("y"), lax.axis_index("c")


_FLIPS = [(0, 0, 1), (1, 0, 0), (0, 1, 0), (1, 1, 0), (1, 0, 1), (0, 1, 1), (1, 1, 1)]


def _exchange_copies(scatter, srcs, dsts, send_sems, recv_sems, loc_sems):
    x, y, c = _me()
    me = 4 * x + 2 * y + c
    sends, recvs, local = [], [], []
    for a in range(len(srcs)):
        for k, (dx, dy, dc) in enumerate(_FLIPS):
            px, py, pc = (1 - x if dx else x), (1 - y if dy else y), (1 - c if dc else c)
            peer = 4 * px + 2 * py + pc
            src = srcs[a].at[peer] if scatter else srcs[a]
            for lst, slab in ((sends, me), (recvs, peer)):
                lst.append(pltpu.make_async_remote_copy(
                    src_ref=src, dst_ref=dsts[a].at[slab], send_sem=send_sems.at[a, k], recv_sem=recv_sems.at[a, k],
                    device_id=(px, py, pc), device_id_type=MESH))
        local.append(pltpu.make_async_copy(srcs[a].at[me] if scatter else srcs[a], dsts[a].at[me], loc_sems.at[a]))
    return sends, recvs, local


def _exchange_start(cps):
    for cp in cps[2] + cps[0]:
        cp.start()


def _exchange_wait(cps):
    for cp in cps[0]:
        cp.wait_send()
    for cp in cps[1]:
        cp.wait_recv()
    for cp in cps[2]:
        cp.wait()


def _exchange_shapes(scatter, srcs):
    return tuple(jax.ShapeDtypeStruct(((N_DEV,) + s.shape[-2:]), s.dtype) for s in srcs)


def _exchange_sems(n):
    return [pltpu.SemaphoreType.DMA((n, 7)), pltpu.SemaphoreType.DMA((n, 7)), pltpu.SemaphoreType.DMA((n,))]


def _exchange_call(scatter, srcs, name):
    n = len(srcs)

    def body(*refs):
        cps = _exchange_copies(scatter, refs[:n], refs[n:2 * n], *refs[2 * n:])
        _exchange_start(cps)
        _exchange_wait(cps)

    return pl.pallas_call(body, name=name, out_shape=_exchange_shapes(scatter, srcs), in_specs=[ANY] * n,
                          out_specs=(ANY,) * n, scratch_shapes=_exchange_sems(n))(*srcs)


def _pcall(body, *, name, grid, in_specs, out_specs, out_shape, scratch_shapes, sem, args, comm=None):
    if comm is None:
        res = pl.pallas_call(body, name=name, grid=grid, in_specs=list(in_specs), out_specs=tuple(out_specs),
                             out_shape=tuple(out_shape), scratch_shapes=list(scratch_shapes),
                             compiler_params=_params(sem))(*args)
        return tuple(res), ()
    scatter, srcs = comm
    n_in, n_out, n_c, n_s = len(in_specs), len(out_specs), len(srcs), len(scratch_shapes)

    def carrier(*refs):
        ins, c_src = refs[:n_in], refs[n_in:n_in + n_c]
        outs = refs[n_in + n_c:n_in + n_c + n_out]
        c_dst = refs[n_in + n_c + n_out:n_in + 2 * n_c + n_out]
        scr = refs[n_in + 2 * n_c + n_out:n_in + 2 * n_c + n_out + n_s]
        first = pl.program_id(0) == 0
        last = pl.program_id(0) == grid[0] - 1
        for ax in range(1, len(grid)):
            first = jnp.logical_and(first, pl.program_id(ax) == 0)
            last = jnp.logical_and(last, pl.program_id(ax) == grid[ax] - 1)

        @pl.when(first)
        def _():
            _exchange_start(_exchange_copies(scatter, c_src, c_dst, *refs[-3:]))

        body(*ins, *outs, *scr)

        @pl.when(last)
        def _():
            _exchange_wait(_exchange_copies(scatter, c_src, c_dst, *refs[-3:]))

    res = pl.pallas_call(
        carrier, name=name + "_x", grid=grid, in_specs=list(in_specs) + [ANY] * n_c,
        out_specs=tuple(out_specs) + (ANY,) * n_c, out_shape=tuple(out_shape) + _exchange_shapes(scatter, srcs),
        scratch_shapes=list(scratch_shapes) + _exchange_sems(n_c),
        compiler_params=_params(("arbitrary",) * len(grid)))(*args, *srcs)
    return tuple(res[:n_out]), tuple(res[n_out:])


def _two_level_gather_body(n_arr, x_refs, out_refs, send_sems, recv_sems, local_sems):
    x, y, c = _me()
    me, sibling = (x, y, c), (x, y, 1 - c)
    chips = [(1 - x, y), (x, 1 - y), (1 - x, 1 - y)]

    def slab(a, px, py, pc):
        return out_refs[a].at[4 * px + 2 * py + pc]

    def copy(a, k, block, to, src=None):
        return pltpu.make_async_remote_copy(
            src_ref=slab(a, *block) if src is None else src, dst_ref=slab(a, *block),
            send_sem=send_sems.at[a, k], recv_sem=recv_sems.at[a, k], device_id=to, device_id_type=MESH)

    mine = [pltpu.make_async_copy(x_refs[a], slab(a, *me), local_sems.at[a]) for a in range(n_arr)]
    for cp in mine:
        cp.start()
    first = []
    for a in range(n_arr):
        first.append(copy(a, 0, me, sibling, src=x_refs[a]))
        first += [copy(a, 1 + j, me, (*chip, c), src=x_refs[a]) for j, chip in enumerate(chips)]
    for cp in first:
        cp.start()
    passed = []
    for j, chip in enumerate(chips):
        for a in range(n_arr):
            copy(a, 1 + j, (*chip, c), me).wait_recv()
            fw = copy(a, 4 + j, (*chip, c), sibling)
            fw.start()
            passed.append(fw)
    for a in range(n_arr):
        copy(a, 0, sibling, me).wait_recv()
        for j, chip in enumerate(chips):
            copy(a, 4 + j, (*chip, 1 - c), me).wait_recv()
    for cp in first + passed:
        cp.wait_send()
    for cp in mine:
        cp.wait()


def _gather_small(x, name):
    def body(x_ref, out_ref, send_sems, recv_sems, local_sems):
        _two_level_gather_body(1, [x_ref], [out_ref], send_sems, recv_sems, local_sems)

    vm = pl.BlockSpec(memory_space=pltpu.VMEM)
    return pl.pallas_call(
        body, name=name,
        out_shape=jax.ShapeDtypeStruct((N_DEV,) + x.shape, x.dtype),
        in_specs=[vm], out_specs=vm,
        scratch_shapes=[pltpu.SemaphoreType.DMA((1, 7)), pltpu.SemaphoreType.DMA((1, 7)),
                        pltpu.SemaphoreType.DMA((1,))],
    )(x)


_SMALL = ["norm_mix_pre", "norm_mix_post", "norm_ffn_pre", "norm_ffn_post", "gla_gate_up", "gla_gate_b",
          "gla_norm", "ssd_conv_w", "ssd_conv_b", "ssd_dt_bias", "ssd_a_log", "ssd_d", "ssd_norm", "ret_norm"]


def _pack(arrs):
    flat = jnp.concatenate([a.reshape(-1) for a in arrs])
    n = flat.shape[0]
    npad = -(-n // 1024) * 1024
    return jnp.pad(flat, (0, npad - n)).reshape(npad // 128, 128)


def _unpack(buf, shapes):
    flat = buf.reshape(-1)
    out, o = [], 0
    for s in shapes:
        n = math.prod(s)
        out.append(flat[o:o + n].reshape(s))
        o += n
    return out


def kernel(x, c, ctx, c_ctx, ada_w, ada_b, norm_mix_pre, norm_mix_post, norm_ffn_pre, norm_ffn_post, w_in, w_out, gla_gate_up, gla_gate_b, gla_norm, ssd_conv_w, ssd_conv_b, ssd_dt_bias, ssd_a_log, ssd_d, ssd_norm, ret_norm, ffn_w13, ffn_w2, loss_target, m_c_ctx, m_ada_w, m_ada_b, m_norm_mix_pre, m_norm_mix_post, m_norm_ffn_pre, m_norm_ffn_post, m_w_in, m_w_out, m_gla_gate_up, m_gla_gate_b, m_gla_norm, m_ssd_conv_w, m_ssd_conv_b, m_ssd_dt_bias, m_ssd_a_log, m_ssd_d, m_ssd_norm, m_ret_norm, m_ffn_w13, m_ffn_w2, v_c_ctx, v_ada_w, v_ada_b, v_norm_mix_pre, v_norm_mix_post, v_norm_ffn_pre, v_norm_ffn_post, v_w_in, v_w_out, v_gla_gate_up, v_gla_gate_b, v_gla_norm, v_ssd_conv_w, v_ssd_conv_b, v_ssd_dt_bias, v_ssd_a_log, v_ssd_d, v_ssd_norm, v_ret_norm, v_ffn_w13, v_ffn_w2):
    P_ = dict(c_ctx=c_ctx, ada_w=ada_w, ada_b=ada_b, norm_mix_pre=norm_mix_pre, norm_mix_post=norm_mix_post,
              norm_ffn_pre=norm_ffn_pre, norm_ffn_post=norm_ffn_post, w_in=w_in, w_out=w_out,
              gla_gate_up=gla_gate_up, gla_gate_b=gla_gate_b, gla_norm=gla_norm, ssd_conv_w=ssd_conv_w,
              ssd_conv_b=ssd_conv_b, ssd_dt_bias=ssd_dt_bias, ssd_a_log=ssd_a_log, ssd_d=ssd_d,
              ssd_norm=ssd_norm, ret_norm=ret_norm, ffn_w13=ffn_w13, ffn_w2=ffn_w2)
    M_ = dict(c_ctx=m_c_ctx, ada_w=m_ada_w, ada_b=m_ada_b, norm_mix_pre=m_norm_mix_pre,
              norm_mix_post=m_norm_mix_post, norm_ffn_pre=m_norm_ffn_pre, norm_ffn_post=m_norm_ffn_post,
              w_in=m_w_in, w_out=m_w_out, gla_gate_up=m_gla_gate_up, gla_gate_b=m_gla_gate_b,
              gla_norm=m_gla_norm, ssd_conv_w=m_ssd_conv_w, ssd_conv_b=m_ssd_conv_b, ssd_dt_bias=m_ssd_dt_bias,
              ssd_a_log=m_ssd_a_log, ssd_d=m_ssd_d, ssd_norm=m_ssd_norm, ret_norm=m_ret_norm,
              ffn_w13=m_ffn_w13, ffn_w2=m_ffn_w2)
    V_ = dict(c_ctx=v_c_ctx, ada_w=v_ada_w, ada_b=v_ada_b, norm_mix_pre=v_norm_mix_pre,
              norm_mix_post=v_norm_mix_post, norm_ffn_pre=v_norm_ffn_pre, norm_ffn_post=v_norm_ffn_post,
              w_in=v_w_in, w_out=v_w_out, gla_gate_up=v_gla_gate_up, gla_gate_b=v_gla_gate_b,
              gla_norm=v_gla_norm, ssd_conv_w=v_ssd_conv_w, ssd_conv_b=v_ssd_conv_b, ssd_dt_bias=v_ssd_dt_bias,
              ssd_a_log=v_ssd_a_log, ssd_d=v_ssd_d, ssd_norm=v_ssd_norm, ret_norm=v_ret_norm,
              ffn_w13=v_ffn_w13, ffn_w2=v_ffn_w2)
    order = ["c_ctx", "ada_w", "ada_b", "norm_mix_pre", "norm_mix_post", "norm_ffn_pre", "norm_ffn_post", "w_in",
             "w_out", "gla_gate_up", "gla_gate_b", "gla_norm", "ssd_conv_w", "ssd_conv_b", "ssd_dt_bias",
             "ssd_a_log", "ssd_d", "ssd_norm", "ret_norm", "ffn_w13", "ffn_w2"]

    mx, my, mc_ = _me()
    me = 4 * mx + 2 * my + mc_
    Tl, Tc = x.shape[1], ctx.shape[1]
    n_in, n_out, n_13, n_2 = w_in.shape[2], w_out.shape[1], ffn_w13.shape[2], ffn_w2.shape[1]
    n_ada = ada_w.shape[2]

    shards = [[w_in[l].astype(BF16), w_out[l].astype(BF16), ffn_w13[l].astype(BF16), ffn_w2[l].astype(BF16)]
              for l in range(DEPTH)]

    cw = ssd_conv_w.shape[2]
    small_in = jnp.concatenate([jnp.pad(c, ((0, 7), (0, 0))).reshape(-1),
                                ssd_conv_w.reshape(-1)]).reshape(-1, 128)
    n_c_rows = 8 * D // 128
    small_in = jnp.pad(small_in, ((0, -small_in.shape[0] % 8), (0, 0)))
    gathered = _gather_small(small_in, "gather_c_conv")
    c_all = gathered[:, :n_c_rows].reshape(N_DEV, 8, D)[:, 0]
    conv_rows = DEPTH * 5 * cw // 128
    conv_full = gathered[:, n_c_rows:n_c_rows + conv_rows].reshape(N_DEV, DEPTH, 5, cw)
    conv_full = jnp.moveaxis(conv_full, 0, 2).reshape(DEPTH, 5, N_DEV * cw)
    c9 = jnp.concatenate([c_all, c_ctx[None], jnp.zeros((7, D), F32)], axis=0)
    s9 = c9 * jax.nn.sigmoid(c9)
    mod_piece = jnp.concatenate([_mm(s9, ada_w[l], name="mm_mod") for l in range(DEPTH)], axis=0)
    mod_g = _gather_small(mod_piece, "gather_mod")
    mod_all = jnp.moveaxis(mod_g.reshape(N_DEV, DEPTH, 16, n_ada), 0, 2).reshape(DEPTH, 16, N_DEV * n_ada)
    mod_all = mod_all + ada_b[:, None, :]
    mod_l = lax.dynamic_index_in_dim(mod_all, me, axis=1, keepdims=False)
    mod_c = mod_all[:, 8]

    sp = {n: P_[n] for n in _SMALL}
    sp["ssd_conv_w"] = conv_full
    xcat = jnp.concatenate([ctx[0], x[0]], axis=0)
    loss_local, d_xcat, d_mod_l, d_mod_c, d_sp, gw = _local_step(xcat, loss_target[0], mod_l, mod_c, sp, Tc,
                                                                 shards=shards)
    loss = lax.psum(loss_local, ("x", "y", "c"))
    grad_x = d_xcat[Tc:][None]

    G = {n: jnp.stack([_sum8_call(gw[l][a], f"sum_{n}") for l in range(DEPTH)])
         for a, n in enumerate(["w_in", "w_out", "ffn_w13", "ffn_w2"])}

    dmod_rows = jnp.concatenate([d_mod_l, d_mod_c], axis=0)
    dmod_g = _gather_small(dmod_rows, "gather_dmod").reshape(N_DEV, 2, DEPTH, 6 * D)
    dl = jnp.moveaxis(dmod_g[:, 0], 0, 1)
    dc = dmod_g[:, 1, :, :]
    dc_tot = dc[0]
    for d_ in range(1, N_DEV):
        dc_tot = dc_tot + dc[d_]
    dmod9 = jnp.concatenate([dl, dc_tot[:, None, :], jnp.zeros((DEPTH, 7, 6 * D), F32)], axis=1)
    g_ada_b = dmod9[:, 0]
    for r_ in range(1, 9):
        g_ada_b = g_ada_b + dmod9[:, r_]
    dmod9_mine = lax.dynamic_slice_in_dim(dmod9, me * n_ada, n_ada, axis=2)
    s9T = jnp.pad(s9.T, ((0, 0), (0, 112)))
    g_ada_w = jnp.stack([_mm(s9T, jnp.pad(dmod9_mine[l], ((0, 112), (0, 0))), name="mm_dada")
                         for l in range(DEPTH)])
    ds9 = _mm(dmod9_mine[0], ada_w[0], trans_b=True, name="mm_ds9")
    for l in range(1, DEPTH):
        ds9 = _mm(dmod9_mine[l], ada_w[l], trans_b=True, name="mm_ds9_acc", add=ds9)
    ds_ctx_part = ds9[8]

    small_names = [n for n in _SMALL]
    small_parts = [d_sp[n] for n in small_names] + [ds_ctx_part]
    packed = _pack(small_parts)
    allp = _gather_small(packed, "gather_small_grads")
    summed = _sum_call([allp[d_] for d_ in range(N_DEV)], "sum_small_grads")
    parts = _unpack(summed, [p.shape for p in small_parts])
    for n, p in zip(small_names, parts[:-1]):
        G[n] = p
    sig = jax.nn.sigmoid(c_ctx)
    G["c_ctx"] = parts[-1] * (sig * (1.0 + c_ctx * (1.0 - sig)))
    G["ssd_conv_w"] = lax.dynamic_slice_in_dim(G["ssd_conv_w"], me * cw, cw, axis=2)
    G["ada_w"] = g_ada_w
    G["ada_b"] = g_ada_b

    delta, new_m, new_v = {}, {}, {}
    for n in ["ada_w", "w_in", "w_out", "ffn_w13", "ffn_w2"]:
        sh = P_[n].shape
        f2 = lambda a: a.reshape(sh[0] * sh[1], sh[2])
        d_, m_, v_ = _adamw_call(f2(P_[n]), f2(G[n]), f2(M_[n]), f2(V_[n]), f"adamw_{n}")
        delta[n], new_m[n], new_v[n] = d_.reshape(sh), m_.reshape(sh), v_.reshape(sh)
    rest = [n for n in order if n not in delta]
    shapes = [P_[n].shape for n in rest]
    d_, m_, v_ = _adamw_call(_pack([P_[n] for n in rest]), _pack([G[n] for n in rest]),
                             _pack([M_[n] for n in rest]), _pack([V_[n] for n in rest]), "adamw_small")
    for n, a, b, e in zip(rest, _unpack(d_, shapes), _unpack(m_, shapes), _unpack(v_, shapes)):
        delta[n], new_m[n], new_v[n] = a, b, e

    return (loss, grad_x, *[G[n] for n in order], *[delta[n] for n in order],
            *[new_m[n] for n in order], *[new_v[n] for n in order])
```

```python
import functools
import math

import jax
import jax.numpy as jnp
from jax import lax
from jax.experimental import pallas as pl
from jax.experimental.pallas import tpu as pltpu

F32 = jnp.float32
BF16 = jnp.bfloat16

D = 1024
DEPTH = 4
GRID_W = 64
RMS_EPS = 1e-6
GLA_TAU = 16.0
FFN_H = 2816
IN_COLS = 3376
N_DEV = 8
ADAM_LR, ADAM_B1, ADAM_B2, ADAM_EPS, ADAM_WD, ADAM_STEP = 0.001, 0.9, 0.999, 1e-08, 0.01, 10

VMEM_LIMIT = 48 * 1024 * 1024

_ORIG = dict(gla_q=(0, 128), gla_k=(128, 128), gla_v=(256, 256), gla_r=(512, 256), gla_lr=(768, 32),
             ssd_z=(800, 512), ssd_xbc=(1312, 1024), ssd_dt=(2336, 16), ret_q=(2352, 256), ret_k=(2608, 256),
             ret_v=(2864, 256), ret_g=(3120, 256))
_R_ORDER = ["gla_v", "gla_r", "ret_q", "ret_k", "ret_v", "ret_g", "ssd_z", "gla_q", "gla_k", "gla_lr", "ssd_dt"]
R_W = 2560
_ROFF = {}
_o = 0
for _n in _R_ORDER:
    _ROFF[_n] = _o
    _o += _ORIG[_n][1]
MISC = _ROFF["gla_lr"]
assert MISC == 2304 and _o == 2352


def _split_w_in(wt):
    xs, xz = _ORIG["ssd_xbc"]
    parts = [wt[_ORIG[n][0]:_ORIG[n][0] + _ORIG[n][1]] for n in _R_ORDER]
    parts.append(jnp.zeros((R_W - _o,) + wt.shape[1:], wt.dtype))
    return wt[xs:xs + xz], jnp.concatenate(parts, axis=0)


def _merge_w_in(wx, wr):
    pieces = []
    for n, (s, z) in sorted(_ORIG.items(), key=lambda t: t[1][0]):
        pieces.append(wx if n == "ssd_xbc" else wr[_ROFF[n]:_ROFF[n] + z])
    return jnp.concatenate(pieces, axis=0)


def _pick(n, cands):
    for c in cands:
        if n % c == 0:
            return c
    return n


def _params(sem=None):
    kw = dict(vmem_limit_bytes=VMEM_LIMIT)
    if sem is not None:
        kw["dimension_semantics"] = sem
    return pltpu.CompilerParams(**kw)


def _iota(shape, dim):
    return lax.broadcasted_iota(jnp.int32, shape, dim)


def _dot(a, b, dims):
    return lax.dot_general(a, b, (dims, ((), ())), preferred_element_type=F32)


_NN = ((1,), (0,))
_NT = ((1,), (1,))
_TN = ((0,), (0,))


def _bf(x):
    return x.astype(BF16)


def _dot_sel(x, e, dims, x_left=True):
    eb = e.astype(BF16)
    hi = x.astype(BF16)
    r1 = x - hi.astype(F32)
    mid = r1.astype(BF16)
    lo = (r1 - mid.astype(F32)).astype(BF16)
    out = None
    for p in (hi, mid, lo):
        t = _dot(p, eb, dims) if x_left else _dot(eb, p, dims)
        out = t if out is None else out + t
    return out


@jax.custom_vjp
def _sel(x, e):
    return _dot_sel(x, e, _NN)


_sel.defvjp(lambda x, e: (_dot_sel(x, e, _NN), e), lambda e, g: (_dot_sel(g, e, _NT), jnp.zeros_like(e)))


def _sig(x):
    e = jnp.exp(-jnp.abs(x))
    return jnp.where(x >= 0, 1.0 / (1.0 + e), e / (1.0 + e))


@jax.custom_vjp
def _sigmoid(x):
    return _sig(x)


def _sigmoid_fwd(x):
    s = _sig(x)
    return s, s


_sigmoid.defvjp(_sigmoid_fwd, lambda s, g: (g * s * (1.0 - s),))


def _silu(x):
    return x * _sigmoid(x)


@jax.custom_vjp
def _softplus(x):
    return jnp.maximum(x, 0.0) + jnp.log(1.0 + jnp.exp(-jnp.abs(x)))


_softplus.defvjp(lambda x: (jnp.maximum(x, 0.0) + jnp.log(1.0 + jnp.exp(-jnp.abs(x))), x),
                 lambda x, g: (g * _sig(x),))


def _log_sigmoid(x):
    return -_softplus(-x)


@jax.custom_vjp
def _mm_bf(x, w):
    return _dot(_bf(x), _bf(w), _NN)


_mm_bf.defvjp(lambda x, w: (_dot(_bf(x), _bf(w), _NN), (x, w)),
              lambda r, g: (_dot(_bf(g), _bf(r[1]), _NT), _dot(_bf(r[0]), _bf(g), _TN)))


_TILE_M = (1088, 1024, 512, 256, 128, 64, 32, 16)
_TILE_N = (1408, 1280, 1024, 768, 512, 384, 256, 128)
_TILE_K = (1408, 1280, 1024, 768, 512, 384, 256, 128)


def _mm(a, b, *, trans_b=False, name, add=None, out_dtype=F32):
    M, K = a.shape
    N = b.shape[0] if trans_b else b.shape[1]
    assert (b.shape[1] if trans_b else b.shape[0]) == K
    tm, tn, tk = _pick(M, _TILE_M), _pick(N, _TILE_N), _pick(K, _TILE_K)
    nk = K // tk
    dims = _NT if trans_b else _NN
    has_add = add is not None

    def body(*refs):
        a_ref, b_ref = refs[0], refs[1]
        o_ref, acc_ref = refs[-2], refs[-1]
        k = pl.program_id(2)

        @pl.when(k == 0)
        def _():
            acc_ref[...] = refs[2][...] if has_add else jnp.zeros_like(acc_ref)

        acc_ref[...] += _dot(a_ref[...].astype(BF16), b_ref[...].astype(BF16), dims)

        @pl.when(k == nk - 1)
        def _():
            o_ref[...] = acc_ref[...].astype(o_ref.dtype)

    b_spec = (pl.BlockSpec((tn, tk), lambda i, j, k: (j, k)) if trans_b
              else pl.BlockSpec((tk, tn), lambda i, j, k: (k, j)))
    o_spec = pl.BlockSpec((tm, tn), lambda i, j, k: (i, j))
    return pl.pallas_call(
        body, name=name,
        out_shape=jax.ShapeDtypeStruct((M, N), out_dtype),
        grid=(M // tm, N // tn, nk),
        in_specs=[pl.BlockSpec((tm, tk), lambda i, j, k: (i, k)), b_spec] + ([o_spec] if has_add else []),
        out_specs=o_spec,
        scratch_shapes=[pltpu.VMEM((tm, tn), F32)],
        compiler_params=_params(("parallel", "parallel", "arbitrary")),
    )(*((a, b, add) if has_add else (a, b)))


def _mm_tn(a, g, *, name, out_dtype=F32):
    M, K = a.shape
    N = g.shape[1]
    tm, tk, tn = _pick(M, _TILE_M), _pick(K, _TILE_K), _pick(N, _TILE_N)
    nm = M // tm

    def body(a_ref, g_ref, o_ref, acc_ref):
        i = pl.program_id(2)

        @pl.when(i == 0)
        def _():
            acc_ref[...] = jnp.zeros_like(acc_ref)

        acc_ref[...] += _dot(a_ref[...].astype(BF16), g_ref[...].astype(BF16), _TN)

        @pl.when(i == nm - 1)
        def _():
            o_ref[...] = acc_ref[...].astype(o_ref.dtype)

    return pl.pallas_call(
        body, name=name,
        out_shape=jax.ShapeDtypeStruct((K, N), out_dtype),
        grid=(K // tk, N // tn, nm),
        in_specs=[pl.BlockSpec((tm, tk), lambda k, j, i: (i, k)), pl.BlockSpec((tm, tn), lambda k, j, i: (i, j))],
        out_specs=pl.BlockSpec((tk, tn), lambda k, j, i: (k, j)),
        scratch_shapes=[pltpu.VMEM((tk, tn), F32)],
        compiler_params=_params(("parallel", "parallel", "arbitrary")),
    )(a, g)


def _norm_fwd_call(x, w, a2, b2, res, tr, out_dtype=F32):
    T, W = x.shape
    has_res = res is not None

    def body(*refs):
        x_ref, w_ref, a_ref, b_ref = refs[:4]
        y_ref = refs[-1]
        seg = jnp.minimum(pl.program_id(0), 1)
        xv = x_ref[...]
        rstd = lax.rsqrt(jnp.mean(xv * xv, axis=-1, keepdims=True) + RMS_EPS)
        y = a_ref[pl.ds(seg, 1), :] * (xv * rstd * w_ref[...]) + b_ref[pl.ds(seg, 1), :]
        y_ref[...] = (y + refs[4][...] if has_res else y).astype(y_ref.dtype)

    row = pl.BlockSpec((tr, W), lambda i: (i, 0))
    small = pl.BlockSpec((8, W), lambda i: (0, 0))
    return pl.pallas_call(
        body, name="norm_fwd",
        out_shape=jax.ShapeDtypeStruct((T, W), out_dtype),
        grid=(T // tr,),
        in_specs=[row, pl.BlockSpec((1, W), lambda i: (0, 0)), small, small] + ([row] if has_res else []),
        out_specs=row,
        compiler_params=_params(("parallel",)),
    )(*((x, w.reshape(1, W), a2, b2) + ((res,) if has_res else ())))


def _norm_bwd_call(x, w, a2, dy, tr, add=None, out_dtype=F32):
    T, W = x.shape
    has_add = add is not None

    def body(*refs):
        x_ref, w_ref, a_ref, dy_ref = refs[:4]
        dx_ref, dw_ref, da_ref, db_ref = refs[-4:]
        i = pl.program_id(0)
        seg = jnp.minimum(i, 1)

        @pl.when(i == 0)
        def _():
            dw_ref[...] = jnp.zeros_like(dw_ref)
            da_ref[...] = jnp.zeros_like(da_ref)
            db_ref[...] = jnp.zeros_like(db_ref)

        xv = x_ref[...]
        g = dy_ref[...]
        wv = w_ref[...]
        rstd = lax.rsqrt(jnp.mean(xv * xv, axis=-1, keepdims=True) + RMS_EPS)
        xh = xv * rstd
        da_ref[pl.ds(seg, 1), :] += jnp.sum(g * (xh * wv), axis=0, keepdims=True)
        db_ref[pl.ds(seg, 1), :] += jnp.sum(g, axis=0, keepdims=True)
        gy = g * a_ref[pl.ds(seg, 1), :]
        dw_ref[0:1, :] += jnp.sum(gy * xh, axis=0, keepdims=True)
        gx = gy * wv
        dx = rstd * (gx - xh * jnp.mean(gx * xh, axis=-1, keepdims=True))
        dx_ref[...] = (dx + refs[4][...] if has_add else dx).astype(dx_ref.dtype)

    acc = jax.ShapeDtypeStruct((8, W), F32)
    acc_spec = pl.BlockSpec((8, W), lambda i: (0, 0))
    row = pl.BlockSpec((tr, W), lambda i: (i, 0))
    return pl.pallas_call(
        body, name="norm_bwd",
        out_shape=(jax.ShapeDtypeStruct((T, W), out_dtype), acc, acc, acc),
        grid=(T // tr,),
        in_specs=[row, pl.BlockSpec((1, W), lambda i: (0, 0)), acc_spec, row] + ([row] if has_add else []),
        out_specs=(row, acc_spec, acc_spec, acc_spec),
        compiler_params=_params(("arbitrary",)),
    )(*((x, w.reshape(1, W), a2, dy) + ((add,) if has_add else ())))


def _act_call(u1, u2, dact=None):
    T, W = u1.shape
    tr = _pick(T, (512, 256, 128, 64))
    tn = _pick(W, (1408, 512, 256, 128))
    spec = pl.BlockSpec((tr, tn), lambda i, j: (i, j))
    sh = jax.ShapeDtypeStruct((T, W), BF16)
    if dact is None:
        def body(a_ref, b_ref, o_ref):
            a = a_ref[...]
            o_ref[...] = (a * _sig(a) * b_ref[...]).astype(o_ref.dtype)

        return pl.pallas_call(body, name="act_fwd", out_shape=sh, grid=(T // tr, W // tn), in_specs=[spec, spec],
                              out_specs=spec, compiler_params=_params(("parallel", "parallel")))(u1, u2)

    def body(a_ref, b_ref, g_ref, da_ref, db_ref):
        a, g = a_ref[...], g_ref[...]
        s = _sig(a)
        da_ref[...] = (g * b_ref[...] * (s * (1.0 + a * (1.0 - s)))).astype(da_ref.dtype)
        db_ref[...] = (g * a * s).astype(db_ref.dtype)

    return pl.pallas_call(body, name="act_bwd", out_shape=(sh, sh), grid=(T // tr, W // tn),
                          in_specs=[spec, spec, spec], out_specs=(spec, spec),
                          compiler_params=_params(("parallel", "parallel")))(u1, u2, dact)


def _conv_specs(T, Wc, tr):
    hb, nt = tr // 8, T // tr
    row = pl.BlockSpec((tr, Wc), lambda i: (i, 0))
    prev = pl.BlockSpec((8, Wc), lambda i: (jnp.maximum(i * hb - 1, 0), 0))
    nxt = pl.BlockSpec((8, Wc), lambda i: (jnp.minimum((i + 1) * hb, T // 8 - 1), 0))
    return row, prev, nxt, nt


def _fill_ext(dst_ref, cur_ref, prev_ref, next_ref, i, nt, tr):
    has_prev = (i > 1).astype(F32)
    has_next = jnp.logical_and(i > 0, i < nt - 1).astype(F32)
    dst_ref[8:16, :] = prev_ref[...] * has_prev
    dst_ref[16:16 + tr, :] = cur_ref[...]
    dst_ref[16 + tr:24 + tr, :] = next_ref[...] * has_next


def _conv_fwd_call(px, w8, b, tr):
    T, Wc = px.shape
    row, prev, nxt, nt = _conv_specs(T, Wc, tr)

    def body(x_ref, xp_ref, xn_ref, w_ref, b_ref, u_ref, xe_ref):
        i = pl.program_id(0)

        @pl.when(i == 0)
        def _():
            xe_ref[...] = jnp.zeros_like(xe_ref)

        _fill_ext(xe_ref, x_ref, xp_ref, xn_ref, i, nt, tr)
        y = b_ref[...] + w_ref[0:1, :] * xe_ref[pl.ds(14, tr), :]
        for k in range(1, 5):
            y = y + w_ref[k:k + 1, :] * xe_ref[pl.ds(14 + k, tr), :]
        u_ref[...] = y * _sig(y)

    return pl.pallas_call(
        body, name="conv_fwd", out_shape=jax.ShapeDtypeStruct((T, Wc), F32), grid=(nt,),
        in_specs=[row, prev, nxt, pl.BlockSpec((8, Wc), lambda i: (0, 0)), pl.BlockSpec((1, Wc), lambda i: (0, 0))],
        out_specs=row, scratch_shapes=[pltpu.VMEM((tr + 32, Wc), F32)],
        compiler_params=_params(("arbitrary",)),
    )(px, px, px, w8, b)


def _conv_bwd_call(px, w8, b, du, tr):
    T, Wc = px.shape
    row, prev, nxt, nt = _conv_specs(T, Wc, tr)
    E = tr + 16

    def body(x_ref, xp_ref, xn_ref, g_ref, gp_ref, gn_ref, w_ref, b_ref, dx_ref, dw_ref, db_ref,
             xe_ref, ge_ref, dy_ref):
        i = pl.program_id(0)

        @pl.when(i == 0)
        def _():
            xe_ref[...] = jnp.zeros_like(xe_ref)
            ge_ref[...] = jnp.zeros_like(ge_ref)
            dy_ref[...] = jnp.zeros_like(dy_ref)
            dw_ref[...] = jnp.zeros_like(dw_ref)
            db_ref[...] = jnp.zeros_like(db_ref)

        _fill_ext(xe_ref, x_ref, xp_ref, xn_ref, i, nt, tr)
        _fill_ext(ge_ref, g_ref, gp_ref, gn_ref, i, nt, tr)
        y = b_ref[...] + w_ref[0:1, :] * xe_ref[pl.ds(6, E), :]
        for k in range(1, 5):
            y = y + w_ref[k:k + 1, :] * xe_ref[pl.ds(6 + k, E), :]
        s = _sig(y)
        dy = ge_ref[pl.ds(8, E), :] * (s * (1.0 + y * (1.0 - s)))
        dy_ref[pl.ds(8, E), :] = dy
        dx = w_ref[0:1, :] * dy_ref[pl.ds(18, tr), :]
        for k in range(1, 5):
            dx = dx + w_ref[k:k + 1, :] * dy_ref[pl.ds(18 - k, tr), :]
        dx_ref[...] = dx.astype(dx_ref.dtype)
        dyt = dy_ref[pl.ds(16, tr), :]
        db_ref[0:1, :] += jnp.sum(dyt, axis=0, keepdims=True)
        for k in range(5):
            dw_ref[k:k + 1, :] += jnp.sum(dyt * xe_ref[pl.ds(14 + k, tr), :], axis=0, keepdims=True)

    acc = jax.ShapeDtypeStruct((8, Wc), F32)
    acc_spec = pl.BlockSpec((8, Wc), lambda i: (0, 0))
    ext = pltpu.VMEM((tr + 32, Wc), F32)
    return pl.pallas_call(
        body, name="conv_bwd", out_shape=(jax.ShapeDtypeStruct((T, Wc), BF16), acc, acc), grid=(nt,),
        in_specs=[row, prev, nxt, row, prev, nxt, acc_spec, pl.BlockSpec((1, Wc), lambda i: (0, 0))],
        out_specs=(row, acc_spec, acc_spec), scratch_shapes=[ext, ext, ext],
        compiler_params=_params(("arbitrary",)),
    )(px, px, px, du, du, du, w8, b)


_SCAN_CFG = {
    "gla": dict(H=4, Dk=32, Dv=64, nh=4, scalar=False, C=64),
    "ssd": dict(H=8, Dk=128, Dv=64, nh=2, scalar=True, C=128),
    "ret": dict(H=4, Dk=64, Dv=64, nh=4, scalar=True, C=128),
}
GPAD = 8


def _log2(n):
    r = int(math.log2(n))
    assert 1 << r == n
    return r


class _ScanMath:
    def __init__(self, cfg, reverse):
        C = cfg["C"]
        self.C, self.reverse = C, reverse
        self.Dk, self.Dv, self.nh, self.scalar = cfg["Dk"], cfg["Dv"], cfg["nh"], cfg["scalar"]
        self.Wk, self.Wv = self.nh * self.Dk, self.nh * self.Dv
        self.nsg = cfg["H"] // self.nh
        nh, Wk, Wv = self.nh, self.Wk, self.Wv
        lk, lv, lc = _log2(self.Dk), _log2(self.Dv), _log2(C)
        r, c = _iota((C, C), 0), _iota((C, C), 1)
        self.L = ((c >= r) if reverse else (c <= r)).astype(F32)
        self.Lsuf = ((c <= r) if reverse else (c >= r)).astype(F32)
        i, j = _iota((C, nh * C), 0), _iota((C, nh * C), 1) & (C - 1)
        self.Mst = (j >= i) if reverse else (j <= i)
        self.Dj = (i == j).astype(F32)
        self.km = [((_iota((1, Wk), 1) >> lk) == h).astype(F32) for h in range(nh)]
        self.vm = [((_iota((1, Wv), 1) >> lv) == h).astype(F32) for h in range(nh)]
        self.BD = ((_iota((Wv, Wk), 0) >> lv) == (_iota((Wv, Wk), 1) >> lk)).astype(F32)
        self.last = 0 if reverse else C - 1
        self.last_row = (_iota((C, 1), 0) == self.last).astype(F32)
        self.lk, self.lc = lk, lc
        self.H = cfg["H"]

    def gates(self, g):
        if not self.scalar:
            return _dot_sel(g, self.L, _NN, x_left=False), None
        G8 = _dot_sel(g, self.L, _NN, x_left=False)
        nk, ncol = self.H * self.Dk, self.H * self.C
        ek = (_iota((GPAD, nk), 0) == (_iota((GPAD, nk), 1) >> self.lk)).astype(F32)
        ec = (_iota((GPAD, ncol), 0) == (_iota((GPAD, ncol), 1) >> self.lc)).astype(F32)
        return _dot_sel(G8, ek, _NN), _dot_sel(G8, ec, _NN)

    def Ek(self, s):
        return (_iota((GPAD, self.Wk), 0) == (_iota((GPAD, self.Wk), 1) >> self.lk) + s * self.nh).astype(F32)

    def kstack(self, x):
        return jnp.concatenate([x * self.km[h] for h in range(self.nh)], axis=0)

    def vstack(self, x):
        return jnp.concatenate([x * self.vm[h] for h in range(self.nh)], axis=0)

    def unstack(self, R, masks):
        C = self.C
        out = R[0:C] * masks[0]
        for h in range(1, self.nh):
            out = out + R[h * C:(h + 1) * C] * masks[h]
        return out

    def chunk(self, qs, ks, Gk, Gc):
        C = self.C
        Glast = Gk[self.last:self.last + 1, :]
        out = dict(Gk=Gk, Glast=Glast, eG=jnp.exp(Gk), eGl=jnp.exp(Glast - Gk), eGlast=jnp.exp(Glast))
        if self.scalar:
            Gr = jnp.sum(Gc * self.Dj, axis=0, keepdims=True)
            dec = jnp.where(self.Mst, jnp.exp(jnp.minimum(Gc - Gr, 0.0)), 0.0)
            qt, kt = qs, ks
            A = _dot(_bf(qt), _bf(self.kstack(kt)), _NT) * dec
            out.update(dec=dec, qt=qt, kt=kt, A=A)
        else:
            Gm = Gk[C // 2:C // 2 + 1, :]
            eq, ek = jnp.exp(Gk - Gm), jnp.exp(Gm - Gk)
            qt, kt = qs * eq, ks * ek
            A = jnp.where(self.Mst, _dot(_bf(qt), _bf(self.kstack(kt)), _NT), 0.0)
            out.update(eq=eq, ek=ek, qt=qt, kt=kt, A=A)
        return out


def _chunk_index(p, n, nc, reverse):
    if not reverse:
        return p
    return jnp.where(p < nc, nc - 1 - p, n - 1 + nc - p)


def _scan_dims(kind):
    cfg = _SCAN_CFG[kind]
    HK, HV = cfg["H"] * cfg["Dk"], cfg["H"] * cfg["Dv"]
    return cfg, cfg["C"], HK, HV, (GPAD if cfg["scalar"] else HK)


def _scan_fwd_step(m, q_ref, k_ref, v_ref, g_ref, o_ref, st_ref, S_ref):
    C = m.C

    @pl.when(pl.program_id(0) == 0)
    def _():
        S_ref[...] = jnp.zeros_like(S_ref)

    Gk_all, Gc_all = m.gates(g_ref[...])
    for s in range(m.nsg):
        ksl, vsl = slice(s * m.Wk, (s + 1) * m.Wk), slice(s * m.Wv, (s + 1) * m.Wv)
        csl = slice(s * m.nh * C, (s + 1) * m.nh * C)
        qs, ks, vs = q_ref[:, ksl], k_ref[:, ksl], v_ref[:, vsl]
        ch = m.chunk(qs, ks, Gk_all[:, ksl], Gc_all[:, csl] if m.scalar else None)
        S = S_ref[vsl, :]
        o = _dot(_bf(ch["A"]), _bf(m.vstack(vs)), _NN) + _dot(_bf(qs * ch["eG"]), _bf(S), _NT)
        o_ref[:, vsl] = o
        st_ref[0, vsl, :] = S
        S_ref[vsl, :] = S * ch["eGlast"] + _dot(_bf(vs), _bf(ks * ch["eGl"]), _TN) * m.BD


def _scan_fwd_call(kind, ops, Tc, comm=None):
    cfg, C, HK, HV, GW = _scan_dims(kind)
    T = ops[False][0][0].shape[0]
    n, nc = T // C, Tc // C

    def body(*refs):
        for d, rev in enumerate((False, True)):
            _scan_fwd_step(_ScanMath(cfg, rev), *refs[4 * d:4 * d + 4], *refs[8 + 2 * d:10 + 2 * d], refs[12 + d])

    sg = cfg["H"] // cfg["nh"]
    Wk, Wv = cfg["nh"] * cfg["Dk"], cfg["nh"] * cfg["Dv"]
    col = lambda rev, w, j: pl.BlockSpec((C, w), lambda p: (_chunk_index(p, n, nc, rev), j))
    st_spec = lambda rev: pl.BlockSpec((1, sg * Wv, Wk), lambda p: (_chunk_index(p, n, nc, rev), 0, 0))
    in_specs, args, out_specs, out_shape = [], [], [], []
    for rev in (False, True):
        q, k, v, g = ops[rev]
        in_specs += [col(rev, HK, q[1]), col(rev, HK, k[1]), col(rev, HV, v[1]), col(rev, GW, g[1])]
        args += [q[0], k[0], v[0], g[0]]
        out_specs += [col(rev, HV, 0), st_spec(rev)]
        out_shape += [jax.ShapeDtypeStruct((T, HV), F32), jax.ShapeDtypeStruct((n, sg * Wv, Wk), F32)]
    res, got = _pcall(body, name=f"scan_fwd_{kind}", out_shape=out_shape, grid=(n,), in_specs=in_specs,
                      out_specs=out_specs, scratch_shapes=[pltpu.VMEM((sg * Wv, Wk), F32)] * 2,
                      sem=("arbitrary",), args=args, comm=comm)
    return {False: (res[0], res[1]), True: (res[2], res[3])}, got


def _scan_bwd_step(m, need_dg, q_ref, k_ref, v_ref, g_ref, st_ref, do_ref, dq_ref, dk_ref, dv_ref, dg_ref, dS_ref):
    C = m.C

    @pl.when(pl.program_id(0) == 0)
    def _():
        dS_ref[...] = jnp.zeros_like(dS_ref)

    x8 = jnp.zeros((C, GPAD), F32)
    Gk_all, Gc_all = m.gates(g_ref[...])
    for s in range(m.nsg):
        ksl, vsl = slice(s * m.Wk, (s + 1) * m.Wk), slice(s * m.Wv, (s + 1) * m.Wv)
        csl = slice(s * m.nh * C, (s + 1) * m.nh * C)
        qs, ks, vs, dos = q_ref[:, ksl], k_ref[:, ksl], v_ref[:, vsl], do_ref[:, vsl]
        ch = m.chunk(qs, ks, Gk_all[:, ksl], Gc_all[:, csl] if m.scalar else None)
        S = st_ref[0, vsl, :]
        dS = dS_ref[vsl, :]
        A, qt, kt = ch["A"], ch["qt"], ch["kt"]
        dA = _dot(_bf(dos), _bf(m.vstack(vs)), _NT)
        dAm = dA * ch["dec"] if m.scalar else jnp.where(m.Mst, dA, 0.0)
        kst = _bf(m.kstack(kt))
        dv = m.unstack(_dot(_bf(A), _bf(dos), _TN), m.vm) + _dot(_bf(ks * ch["eGl"]), _bf(dS), _NT)
        dv_ref[:, vsl] = dv
        dq_i = _dot(_bf(dAm), kst, _NN)
        dq_x = ch["eG"] * _dot(_bf(dos), _bf(S), _NN)
        dq_ref[:, ksl] = (dq_i if m.scalar else dq_i * ch["eq"]) + dq_x
        dk_i = m.unstack(_dot(_bf(dAm), _bf(qt), _TN), m.km)
        dk_x = ch["eGl"] * _dot(_bf(vs), _bf(dS), _NN)
        dk_ref[:, ksl] = (dk_i if m.scalar else dk_i * ch["ek"]) + dk_x
        if need_dg:
            bnd = (ch["eGlast"] * jnp.sum(dS * S, axis=0, keepdims=True)
                   + jnp.sum(ks * dk_x, axis=0, keepdims=True))
            X = (_bf(qt).astype(F32) * dq_i - _bf(kt).astype(F32) * dk_i) + (qs * dq_x - ks * dk_x)
            X = X + m.last_row * bnd
            if m.scalar:
                x8 = x8 + _dot_sel(X, m.Ek(s), _NT)
            else:
                dg_ref[:, ksl] = _dot_sel(X, m.Lsuf, _NN, x_left=False)
        dS_ref[vsl, :] = dS * ch["eGlast"] + _dot(_bf(dos), _bf(qs * ch["eG"]), _TN) * m.BD
    if m.scalar:
        dg_ref[...] = _dot_sel(x8, m.Lsuf, _NN, x_left=False)
    elif not need_dg:
        dg_ref[...] = jnp.zeros_like(dg_ref)


def _scan_bwd_call(kind, need_dg, ops, st, do, Tc, comm=None):
    cfg, C, HK, HV, GW = _scan_dims(kind)
    T = ops[False][0][0].shape[0]
    n, nc = T // C, Tc // C

    def body(*refs):
        for d, rev in enumerate((False, True)):
            _scan_bwd_step(_ScanMath(cfg, rev), need_dg, *refs[6 * d:6 * d + 6], *refs[12 + 4 * d:16 + 4 * d],
                           refs[20 + d])

    sg = cfg["H"] // cfg["nh"]
    Wk, Wv = cfg["nh"] * cfg["Dk"], cfg["nh"] * cfg["Dv"]
    col = lambda rev, w, j: pl.BlockSpec((C, w), lambda p: (_chunk_index(n - 1 - p, n, nc, rev), j))
    st_spec = lambda rev: pl.BlockSpec((1, sg * Wv, Wk), lambda p: (_chunk_index(n - 1 - p, n, nc, rev), 0, 0))
    in_specs, args, out_specs, out_shape = [], [], [], []
    for rev in (False, True):
        q, k, v, g = ops[rev]
        in_specs += [col(rev, HK, q[1]), col(rev, HK, k[1]), col(rev, HV, v[1]), col(rev, GW, g[1]),
                     st_spec(rev), col(rev, HV, 0)]
        args += [q[0], k[0], v[0], g[0], st[rev], do]
        out_specs += [col(rev, HK, 0), col(rev, HK, 0), col(rev, HV, 0), col(rev, GW, 0)]
        out_shape += [jax.ShapeDtypeStruct((T, w), F32) for w in (HK, HK, HV, GW)]
    res, got = _pcall(body, name=f"scan_bwd_{kind}", out_shape=out_shape, grid=(n,), in_specs=in_specs,
                      out_specs=out_specs, scratch_shapes=[pltpu.VMEM((sg * Wv, Wk), F32)] * 2,
                      sem=("arbitrary",), args=args, comm=comm)
    return {False: res[0:4], True: res[4:8]}, got


def _prep_consts():
    r, c = _iota((256, 256), 0), _iota((256, 256), 1)
    first = (c & 63) < 32
    rope_perm = jnp.where(first, -(r == c + 32).astype(F32), (r == c - 32).astype(F32))
    sel_f = (_iota((128, GPAD), 0) == _iota((128, GPAD), 1) + 32).astype(F32)
    sel_b = (_iota((128, GPAD), 0) == _iota((128, GPAD), 1) + 40).astype(F32)
    ek = (_iota((GPAD, 1024), 0) == (_iota((GPAD, 1024), 1) >> 7)).astype(F32)
    return rope_perm, sel_f, sel_b, ek


def _prep_tile(misc, gq, rq, rk, bm, cm, cosE, sinE, Wg, gbias, dtbf, dtbb, nAf, nAb):
    rope_perm, sel_f, sel_b, ek = _prep_consts()
    logg = _log_sigmoid(_mm_bf(misc, Wg) + gbias) * (1.0 / GLA_TAU)
    a_gla = jnp.concatenate([gq * (32 ** -0.5), logg], axis=1)
    rot = lambda t: t * cosE + _sel(t, rope_perm) * sinE
    a_ret = jnp.concatenate([rot(rq * (64 ** -0.5)), rot(rk)], axis=1)
    dtf = _softplus(_sel(misc, sel_f) + dtbf)
    dtb = _softplus(_sel(misc, sel_b) + dtbb)
    rep = lambda t: jnp.concatenate([t[:, :128]] * 4 + [t[:, 128:]] * 4, axis=1)
    bmr = rep(bm)
    return a_gla, a_ret, rep(cm), bmr * _sel(dtf, ek), bmr * _sel(dtb, ek), dtf * nAf, dtb * nAb


def _prep_row_specs(tr):
    blk = lambda w, j: pl.BlockSpec((tr, w), lambda i: (i, j))
    return [blk(128, MISC // 128), blk(128, _ROFF["gla_q"] // 128), blk(256, _ROFF["ret_q"] // 256),
            blk(256, _ROFF["ret_k"] // 256), blk(256, 2), blk(256, 3), blk(256, 0), blk(256, 0)]


def _whole(a):
    return pl.BlockSpec(a.shape, lambda i: (0,) * a.ndim)


def _prep_fwd_call(Pr, u, cosE, sinE, pp, tr):
    T = Pr.shape[0]
    n_row = 8

    def body(*refs):
        outs = _prep_tile(*[r[...] for r in refs[:n_row + len(pp)]])
        for o_ref, o in zip(refs[n_row + len(pp):], outs):
            o_ref[...] = o

    widths = [384, 512, 1024, 1024, 1024, GPAD, GPAD]
    return pl.pallas_call(
        body, name="prep_fwd", grid=(T // tr,),
        out_shape=tuple(jax.ShapeDtypeStruct((T, w), F32) for w in widths),
        in_specs=_prep_row_specs(tr) + [_whole(p) for p in pp],
        out_specs=tuple(pl.BlockSpec((tr, w), lambda i: (i, 0)) for w in widths),
        compiler_params=_params(("parallel",)),
    )(Pr, Pr, Pr, Pr, u, u, cosE, sinE, *pp)


def _prep_bwd_call(Pr, u, cosE, sinE, pp, cts, tr, comm=None):
    T = Pr.shape[0]
    n_row, n_p = 8, len(pp)
    names = ["gla_dq_f", "gla_dq_b", "gla_dg_f", "gla_dg_b", "gla_dk_f", "gla_dk_b", "gla_dv_f", "gla_dv_b",
             "ret_dq_f", "ret_dq_b", "ret_dk_f", "ret_dk_b", "ret_dv_f", "ret_dv_b",
             "ssd_dq_f", "ssd_dq_b", "ssd_dk_f", "ssd_dk_b", "ssd_dg_f", "ssd_dg_b", "ssd_dv_f", "ssd_dv_b",
             "d_r", "d_z", "d_gr", "d_xs"]
    ct_arrays = [cts[n] for n in names]

    def body(*refs):
        ins = [r[...] for r in refs[:n_row + n_p]]
        c = {n: r[...] for n, r in zip(names, refs[n_row + n_p:n_row + n_p + len(names)])}
        dPr_ref, du_ref = refs[n_row + n_p + len(names):n_row + n_p + len(names) + 2]
        dp_refs = refs[n_row + n_p + len(names) + 2:]
        _, vjp = jax.vjp(_prep_tile, *ins)
        ct_out = (jnp.concatenate([c["gla_dq_f"] + c["gla_dq_b"], c["gla_dg_f"], c["gla_dg_b"]], axis=1),
                  jnp.concatenate([c["ret_dq_f"] + c["ret_dq_b"], c["ret_dk_f"] + c["ret_dk_b"]], axis=1),
                  c["ssd_dq_f"] + c["ssd_dq_b"], c["ssd_dk_f"], c["ssd_dk_b"], c["ssd_dg_f"], c["ssd_dg_b"])
        d = vjp(ct_out)
        d_misc, d_gq, d_rq, d_rk, d_bm, d_cm = d[:6]
        dPr_ref[...] = jnp.concatenate(
            [c["gla_dv_f"] + c["gla_dv_b"], c["d_r"], d_rq, d_rk, c["ret_dv_f"] + c["ret_dv_b"], c["d_gr"],
             c["d_z"], d_gq, c["gla_dk_f"] + c["gla_dk_b"], d_misc,
             jnp.zeros((d_misc.shape[0], R_W - MISC - 128), F32)], axis=1).astype(dPr_ref.dtype)
        du_ref[...] = jnp.concatenate([c["ssd_dv_f"] + c["ssd_dv_b"] + c["d_xs"], d_bm, d_cm], axis=1)

        @pl.when(pl.program_id(0) == 0)
        def _():
            for r in dp_refs:
                r[...] = jnp.zeros_like(r)

        for r, g in zip(dp_refs, d[n_row:]):
            r[...] += g

    row = lambda a: pl.BlockSpec((tr, a.shape[1]), lambda i: (i, 0))
    return _pcall(
        body, name="prep_bwd", grid=(T // tr,),
        out_shape=(jax.ShapeDtypeStruct((T, R_W), BF16), jax.ShapeDtypeStruct((T, 1024), F32))
        + tuple(jax.ShapeDtypeStruct(p.shape, F32) for p in pp),
        in_specs=_prep_row_specs(tr) + [_whole(p) for p in pp] + [row(a) for a in ct_arrays],
        out_specs=(pl.BlockSpec((tr, R_W), lambda i: (i, 0)), pl.BlockSpec((tr, 1024), lambda i: (i, 0)))
        + tuple(_whole(p) for p in pp),
        scratch_shapes=[], sem=("arbitrary",), args=(Pr, Pr, Pr, Pr, u, u, cosE, sinE, *pp, *ct_arrays), comm=comm)


def _post_tile(ogf, ogb, r, ysf, ysb, xs, z, orf, orb, gr, gla_n, dexp, ssd_n, ret_n):
    bd = ((_iota((256, 256), 0) >> 6) == (_iota((256, 256), 1) >> 6)).astype(F32)
    og = ogf + ogb
    gla = og * lax.rsqrt(_sel(og * og, bd) * (1.0 / 64) + RMS_EPS) * gla_n * _silu(r)
    t = (ysf + ysb + dexp * xs) * _silu(z)
    ssd = t * lax.rsqrt(jnp.mean(t * t, axis=-1, keepdims=True) + RMS_EPS) * ssd_n
    o = orf + orb
    oc = o - _sel(o, bd) * (1.0 / 64)
    ret = oc * lax.rsqrt(_sel(oc * oc, bd) * (1.0 / 64) + RMS_EPS) * ret_n * _silu(gr)
    return jnp.concatenate([gla, ssd, ret], axis=1)


def _post_row_specs(tr):
    blk = lambda w, j: pl.BlockSpec((tr, w), lambda i: (i, j))
    return [blk(256, 0), blk(256, 0), blk(256, _ROFF["gla_r"] // 256), blk(512, 0), blk(512, 0), blk(512, 0),
            blk(512, _ROFF["ssd_z"] // 512), blk(256, 0), blk(256, 0), blk(256, _ROFF["ret_g"] // 256)]


def _post_fwd_call(rows, qp, tr, comm=None):
    T = rows[0].shape[0]

    def body(*refs):
        refs[-1][...] = _post_tile(*[r[...] for r in refs[:-1]]).astype(refs[-1].dtype)

    res, got = _pcall(body, name="post_fwd", grid=(T // tr,), out_shape=[jax.ShapeDtypeStruct((T, D), BF16)],
                      in_specs=_post_row_specs(tr) + [_whole(p) for p in qp],
                      out_specs=[pl.BlockSpec((tr, D), lambda i: (i, 0))], scratch_shapes=[],
                      sem=("parallel",), args=(*rows, *qp), comm=comm)
    return res[0], got


def _post_bwd_call(rows, qp, dmixed, tr):
    T = rows[0].shape[0]
    n_in = 10 + len(qp)

    def body(*refs):
        ins = [r[...] for r in refs[:n_in]]
        _, vjp = jax.vjp(_post_tile, *ins)
        d = vjp(refs[n_in][...])
        outs = refs[n_in + 1:]
        for o_ref, g in zip(outs[:7], (d[0], d[3], d[7], d[2], d[6], d[9], d[5])):
            o_ref[...] = g.astype(o_ref.dtype)

        @pl.when(pl.program_id(0) == 0)
        def _():
            for r in outs[7:]:
                r[...] = jnp.zeros_like(r)

        for r, g in zip(outs[7:], d[10:]):
            r[...] += g

    widths = [256, 512, 256, 256, 512, 256, 512]
    dts = [BF16] * 3 + [F32] * 4
    return pl.pallas_call(
        body, name="post_bwd", grid=(T // tr,),
        out_shape=tuple(jax.ShapeDtypeStruct((T, w), dt) for w, dt in zip(widths, dts))
        + tuple(jax.ShapeDtypeStruct(p.shape, F32) for p in qp),
        in_specs=_post_row_specs(tr) + [_whole(p) for p in qp] + [pl.BlockSpec((tr, D), lambda i: (i, 0))],
        out_specs=tuple(pl.BlockSpec((tr, w), lambda i: (i, 0)) for w in widths) + tuple(_whole(p) for p in qp),
        compiler_params=_params(("arbitrary",)),
    )(*rows, *qp, dmixed)


def _mixer_scan_operands(Pr, u, a_gla, a_ret, cmr, kf, kb, g8f, g8b, lg):
    gk, gv = (Pr, _ROFF["gla_k"] // 128), (Pr, _ROFF["gla_v"] // 256)
    rv = (Pr, _ROFF["ret_v"] // 256)
    return {
        "gla": {False: ((a_gla, 0), gk, gv, (a_gla, 1)), True: ((a_gla, 0), gk, gv, (a_gla, 2))},
        "ret": {False: ((a_ret, 0), (a_ret, 1), rv, (lg, 0)), True: ((a_ret, 0), (a_ret, 1), rv, (lg, 0))},
        "ssd": {False: ((cmr, 0), (kf, 0), (u, 0), (g8f, 0)), True: ((cmr, 0), (kb, 0), (u, 0), (g8b, 0))},
    }


def _post_rows(o, Pr, u):
    return [o["gla"][False][0], o["gla"][True][0], Pr, o["ssd"][False][0], o["ssd"][True][0], u, Pr,
            o["ret"][False][0], o["ret"][True][0], Pr]


def _mixer_forward(Tc, Pr, Px, cn, pp, cw8, cb, qp, comm):
    cosE, sinE, lg = cn
    u = _conv_fwd_call(Px, cw8, cb, Tc)
    prep = _prep_fwd_call(Pr, u, cosE, sinE, pp, Tc)
    ops = _mixer_scan_operands(Pr, u, *prep, lg)
    o, got = {}, {}
    for kind in ops:
        o[kind], got[kind] = _scan_fwd_call(kind, ops[kind], Tc, comm.get(kind))
    mixed, got["post"] = _post_fwd_call(_post_rows(o, Pr, u), qp, Tc, comm.get("post"))
    return mixed, (u, prep, o), got


def _mixer_backward(Tc, Pr, Px, cn, pp, cw8, cb, qp, saved, dmixed, comm):
    cosE, sinE, lg = cn
    u, prep, o = saved
    post = _post_bwd_call(_post_rows(o, Pr, u), qp, dmixed, Tc)
    d_o = dict(gla=post[0], ssd=post[1], ret=post[2])
    cts = dict(d_r=post[3], d_z=post[4], d_gr=post[5], d_xs=post[6])
    ops = _mixer_scan_operands(Pr, u, *prep, lg)
    got = {}
    for kind in ops:
        st = {rev: o[kind][rev][1] for rev in (False, True)}
        res, got[kind] = _scan_bwd_call(kind, kind != "ret", ops[kind], st, d_o[kind], Tc, comm.get(kind))
        for rev, sfx in ((False, "_f"), (True, "_b")):
            for nm, a in zip(("_dq", "_dk", "_dv", "_dg"), res[rev]):
                cts[kind + nm + sfx] = a
    pb, got["prep"] = _prep_bwd_call(Pr, u, cosE, sinE, pp, cts, Tc, comm.get("prep"))
    dPx, dcw8, dcb = _conv_bwd_call(Px, cw8, cb, pb[1], Tc)
    return pb[0], dPx, tuple(pb[2:]), dcw8, dcb[0:1], tuple(post[7:]), got


def _mix_fwd(Tc, X, w, cn, nw, mods, pp, cw8, cb, qp, comm):
    h = _norm_fwd_call(X, nw[0], mods[0], mods[1], None, Tc, BF16)
    Px, Pr = _mm(h, w[0], trans_b=True, name="mm_fwd"), _mm(h, w[1], trans_b=True, name="mm_fwd")
    mixed, saved, got = _mixer_forward(Tc, Pr, Px, cn, pp, cw8, cb, qp, comm)
    M = _mm(mixed, w[2], name="mm_fwd")
    Xn = _norm_fwd_call(M, nw[1], mods[2], jnp.zeros_like(mods[2]), X, Tc)
    return Xn, (X, nw, mods, w, cn, pp, cw8, cb, qp, h, Px, Pr, mixed, saved, M), got


def _mix_bwd(Tc, res, dXn, comm):
    X, nw, mods, w, cn, pp, cw8, cb, qp, h, Px, Pr, mixed, saved, M = res
    dM, dnw1, da_post, _ = _norm_bwd_call(M, nw[1], mods[2], dXn, Tc, out_dtype=BF16)
    dmixed = _mm(dM, w[2], trans_b=True, name="mm_dx")
    dPr, dPx, dpp, dcw8, dcb, dqp, got = _mixer_backward(Tc, Pr, Px, cn, pp, cw8, cb, qp, saved, dmixed, comm)
    dh = _mm(dPx, w[0], name="mm_dx")
    dh = _mm(dPr, w[1], name="mm_dx_acc", add=dh)
    dX, dnw0, da_pre, db_pre = _norm_bwd_call(X, nw[0], mods[0], dh, Tc, add=dXn)
    dW = tuple(_mm_tn(a, g, name="mm_dw", out_dtype=BF16) for a, g in ((dPx, h), (dPr, h), (mixed, dM)))
    return dX, ((dnw0[0], dnw1[0]), (da_pre, db_pre, da_post), dpp, dcw8, dcb, dqp), dW, got


def _ffn_fwd(Tc, X, w, nw, mods):
    h = _norm_fwd_call(X, nw[0], mods[0], mods[1], None, Tc, BF16)
    U1, U2 = _mm(h, w[0], trans_b=True, name="mm_fwd"), _mm(h, w[1], trans_b=True, name="mm_fwd")
    act = _act_call(U1, U2)
    Fo = _mm(act, w[2], name="mm_fwd")
    Xn = _norm_fwd_call(Fo, nw[1], mods[2], jnp.zeros_like(mods[2]), X, Tc)
    return Xn, (X, nw, mods, w, h, U1, U2, act, Fo)


def _ffn_bwd(Tc, res, dXn):
    X, nw, mods, w, h, U1, U2, act, Fo = res
    dFo, dnw1, da_post, _ = _norm_bwd_call(Fo, nw[1], mods[2], dXn, Tc, out_dtype=BF16)
    dU1, dU2 = _act_call(U1, U2, _mm(dFo, w[2], trans_b=True, name="mm_dx"))
    dh = _mm(dU1, w[0], name="mm_dx")
    dh = _mm(dU2, w[1], name="mm_dx_acc", add=dh)
    dX, dnw0, da_pre, db_pre = _norm_bwd_call(X, nw[0], mods[0], dh, Tc, add=dXn)
    dW = tuple(_mm_tn(a, g, name="mm_dw", out_dtype=BF16) for a, g in ((dU1, h), (dU2, h), (act, dFo)))
    return dX, ((dnw0[0], dnw1[0]), (da_pre, db_pre, da_post)), dW


def _rope_tables(Tl, Tc):
    rows = Tl // GRID_W
    row = jnp.repeat(jnp.arange(rows), GRID_W).astype(F32)
    col = jnp.tile(jnp.arange(GRID_W), rows).astype(F32)
    inv_freq = 10000.0 ** (-jnp.arange(16, dtype=F32) / 16)
    ang = jnp.concatenate([row[:, None] * inv_freq, col[:, None] * inv_freq], axis=-1)
    cos = jnp.concatenate([jnp.ones((Tc, 32), F32), jnp.cos(ang)], axis=0)
    sin = jnp.concatenate([jnp.zeros((Tc, 32), F32), jnp.sin(ang)], axis=0)
    return jnp.tile(cos, (1, 8)), jnp.tile(sin, (1, 8))


def _rows8(first, second):
    z = jnp.zeros((6,) + first.shape, F32)
    return jnp.concatenate([first[None], second[None], z], axis=0)


def _layer_inputs(sp, ml, mc):
    gu = sp["gla_gate_up"]
    Wg = jnp.zeros((128, 256), F32).at[0:16, 0:128].set(gu[0]).at[16:32, 128:256].set(gu[1])
    pp = (Wg, sp["gla_gate_b"].reshape(1, 256), sp["ssd_dt_bias"][0:1], sp["ssd_dt_bias"][1:2],
          -jnp.exp(sp["ssd_a_log"][0:1]), -jnp.exp(sp["ssd_a_log"][1:2]))
    qp = (sp["gla_norm"].reshape(1, 256), jnp.repeat(sp["ssd_d"], 64).reshape(1, 512),
          sp["ssd_norm"].reshape(1, 512), sp["ret_norm"].reshape(1, 256))
    mix = ((sp["norm_mix_pre"], sp["norm_mix_post"]),
           (_rows8(1.0 + mc[1], 1.0 + ml[1]), _rows8(mc[0], ml[0]), _rows8(mc[2], ml[2])),
           pp, jnp.pad(sp["ssd_conv_w"], ((0, 3), (0, 0))), sp["ssd_conv_b"].reshape(1, 1024), qp)
    ffn = ((sp["norm_ffn_pre"], sp["norm_ffn_post"]),
           (_rows8(1.0 + mc[4], 1.0 + ml[4]), _rows8(mc[3], ml[3]), _rows8(mc[5], ml[5])))
    return mix, ffn


def _rows_from(g):
    return g.reshape(N_DEV * g.shape[1], g.shape[2])


def _rows_to(f):
    return f.reshape(N_DEV, f.shape[0] // N_DEV, f.shape[1])


def _local_step(xcat, target, mod_l, mod_c, sp, Tc, weights=None, shards=None):
    Tt = xcat.shape[0]
    cosE, sinE = _rope_tables(Tt - Tc, Tc)
    log_gamma = jnp.log1p(-jnp.exp2(-5.0 - jnp.arange(4, dtype=F32)))
    lg = jnp.broadcast_to(jnp.concatenate([log_gamma, jnp.zeros((GPAD - 4,), F32)])[None, :], (Tt, GPAD))
    cn = (cosE, sinE, lg)
    dist = shards is not None
    X, saved = xcat, []
    if dist:
        g_in, g_out = _exchange_call(False, shards[0][:2], "gather_mix0")
    for l in range(DEPTH):
        (a_mix, a_ffn), pull = jax.vjp(_layer_inputs, {n: sp[n][l] for n in _SMALL},
                                       mod_l[l].reshape(6, D), mod_c[l].reshape(6, D))
        comm = {}
        if dist:
            w_in, w_out = _rows_from(g_in), _rows_from(g_out)
            comm = dict(ssd=(False, [shards[l][2]]), ret=(False, [shards[l][3]]))
            if l + 1 < DEPTH:
                comm.update(gla=(False, [shards[l + 1][0]]), post=(False, [shards[l + 1][1]]))
        else:
            w_in, w_out, w13, w2 = weights[l]
        w_x, w_r = _split_w_in(w_in)
        X, r_mix, got = _mix_fwd(Tc, X, (w_x, w_r, w_out), cn, *a_mix, comm)
        if dist:
            w13, w2 = _rows_from(got["ssd"][0]), _rows_from(got["ret"][0])
            if l + 1 < DEPTH:
                g_in, g_out = got["gla"][0], got["post"][0]
        X, r_ffn = _ffn_fwd(Tc, X, (w13[:FFN_H], w13[FFN_H:], w2), *a_ffn)
        saved.append((r_mix, r_ffn, pull))
    loss, dX = _loss_call(X, target, Tc)
    d_sp, d_ml, d_mc = [None] * DEPTH, [None] * DEPTH, [None] * DEPTH
    gw = [[None] * 4 for _ in range(DEPTH)]
    nxt = None
    for l in reversed(range(DEPTH)):
        r_mix, r_ffn, pull = saved[l]
        dX, c_ffn, dW_ffn = _ffn_bwd(Tc, r_ffn, dX)
        g13, g2 = jnp.concatenate([dW_ffn[0], dW_ffn[1]], axis=0), dW_ffn[2]
        comm = {}
        if dist:
            comm = dict(ssd=(True, [_rows_to(g13)]), ret=(True, [_rows_to(g2)]))
            if nxt is not None:
                comm.update(gla=(True, [nxt[0]]), prep=(True, [nxt[1]]))
        dX, c_mix, dW_mix, got = _mix_bwd(Tc, r_mix, dX, comm)
        d_sp[l], d_ml[l], d_mc[l] = pull((c_mix, c_ffn))
        gin, gout = _merge_w_in(dW_mix[0], dW_mix[1]), dW_mix[2]
        if dist:
            gw[l][2], gw[l][3] = got["ssd"][0], got["ret"][0]
            if nxt is not None:
                gw[l + 1][0], gw[l + 1][1] = got["gla"][0], got["prep"][0]
            nxt = (_rows_to(gin), _rows_to(gout))
        else:
            gw[l] = [gin, gout, g13, g2]
    if dist:
        gw[0][0], gw[0][1] = _exchange_call(True, list(nxt), "scatter_mix0")
    d_sp = {n: jnp.stack([d_sp[l][n] for l in range(DEPTH)]) for n in _SMALL}
    return (loss, dX, jnp.stack(d_ml).reshape(DEPTH, 6 * D), jnp.stack(d_mc).reshape(DEPTH, 6 * D), d_sp, gw)


def _sum8_call(slabs, name):
    _, R, Cc = slabs.shape
    tr = _pick(R, (512, 352, 256, 128, 64, 32, 16))

    def body(*refs):
        acc = refs[0][...].astype(F32)
        for r in refs[1:N_DEV]:
            acc = acc + r[...].astype(F32)
        refs[N_DEV][...] = acc

    return pl.pallas_call(
        body, name=name, grid=(R // tr,), out_shape=jax.ShapeDtypeStruct((R, Cc), F32),
        in_specs=[pl.BlockSpec((None, tr, Cc), lambda i, d=d: (d, i, 0)) for d in range(N_DEV)],
        out_specs=pl.BlockSpec((tr, Cc), lambda i: (i, 0)), compiler_params=_params(("parallel",)),
    )(*([slabs] * N_DEV))


def _loss_call(X, target, Tc):
    Tt, W = X.shape
    tr = Tc
    nt = Tt // tr

    def body(x_ref, t_ref, loss_ref, dx_ref, acc_ref):
        i = pl.program_id(0)

        @pl.when(i == 0)
        def _():
            acc_ref[...] = jnp.zeros_like(acc_ref)
            dx_ref[...] = jnp.zeros_like(dx_ref)

        @pl.when(i > 0)
        def _():
            e = x_ref[...] - t_ref[...]
            dx_ref[...] = e * (1.0 / W)
            acc_ref[...] += jnp.sum(e * e, axis=0, keepdims=True)

        @pl.when(i == nt - 1)
        def _():
            loss_ref[...] = jnp.full(loss_ref.shape, (0.5 / W) * jnp.sum(acc_ref[...]), F32)

    loss, dx = pl.pallas_call(
        body, name="loss",
        out_shape=(jax.ShapeDtypeStruct((8, 128), F32), jax.ShapeDtypeStruct((Tt, W), F32)),
        grid=(nt,),
        in_specs=[pl.BlockSpec((tr, W), lambda i: (i, 0)),
                  pl.BlockSpec((tr, W), lambda i: (jnp.maximum(i - 1, 0), 0))],
        out_specs=(pl.BlockSpec((8, 128), lambda i: (0, 0)), pl.BlockSpec((tr, W), lambda i: (i, 0))),
        scratch_shapes=[pltpu.VMEM((1, W), F32)],
        compiler_params=_params(("arbitrary",)),
    )(X, target)
    return loss[0, 0], dx


def _adamw_call(w, g, m, v, name):
    R, Cc = w.shape
    tr = _pick(R, (512, 352, 256, 128, 64, 32, 16, 8))
    c1 = 1.0 - ADAM_B1 ** ADAM_STEP
    c2 = 1.0 - ADAM_B2 ** ADAM_STEP

    def body(w_ref, g_ref, m_ref, v_ref, d_ref, nm_ref, nv_ref):
        gv = g_ref[...]
        nm = ADAM_B1 * m_ref[...] + (1.0 - ADAM_B1) * gv
        nv = ADAM_B2 * v_ref[...] + (1.0 - ADAM_B2) * (gv * gv)
        d_ref[...] = -ADAM_LR * ((nm / c1) / (jnp.sqrt(nv / c2) + ADAM_EPS) + ADAM_WD * w_ref[...])
        nm_ref[...] = nm
        nv_ref[...] = nv

    spec = pl.BlockSpec((tr, Cc), lambda i: (i, 0))
    sh = jax.ShapeDtypeStruct((R, Cc), F32)
    return pl.pallas_call(
        body, name=name, out_shape=(sh, sh, sh), grid=(R // tr,),
        in_specs=[spec] * 4, out_specs=(spec,) * 3, compiler_params=_params(("parallel",)),
    )(w, g, m, v)


def _sum_call(xs, name, also_bf16=False):
    R, Cc = xs[0].shape
    tr = _pick(R, (512, 352, 256, 128, 64, 32, 16))
    k = len(xs)

    def body(*refs):
        acc = refs[0][...].astype(F32)
        for r in refs[1:k]:
            acc = acc + r[...].astype(F32)
        refs[k][...] = acc
        if also_bf16:
            refs[k + 1][...] = acc.astype(BF16)

    spec = pl.BlockSpec((tr, Cc), lambda i: (i, 0))
    sh = jax.ShapeDtypeStruct((R, Cc), F32)
    return pl.pallas_call(
        body, name=name, grid=(R // tr,), in_specs=[spec] * k,
        out_shape=(sh, jax.ShapeDtypeStruct((R, Cc), BF16)) if also_bf16 else sh,
        out_specs=(spec, spec) if also_bf16 else spec, compiler_params=_params(("parallel",)),
    )(*xs)


MESH = pl.DeviceIdType.MESH
ANY = pl.BlockSpec(memory_space=pl.ANY)


def _me():
    return lax.axis_index("x"), lax.axis_index("y"), lax.axis_index("c")


_FLIPS = [(0, 0, 1), (1, 0, 0), (0, 1, 0), (1, 1, 0), (1, 0, 1), (0, 1, 1), (1, 1, 1)]


def _exchange_copies(scatter, srcs, dsts, send_sems, recv_sems, loc_sems):
    x, y, c = _me()
    me = 4 * x + 2 * y + c
    sends, recvs, local = [], [], []
    for a in range(len(srcs)):
        for k, (dx, dy, dc) in enumerate(_FLIPS):
            px, py, pc = (1 - x if dx else x), (1 - y if dy else y), (1 - c if dc else c)
            peer = 4 * px + 2 * py + pc
            src = srcs[a].at[peer] if scatter else srcs[a]
            for lst, slab in ((sends, me), (recvs, peer)):
                lst.append(pltpu.make_async_remote_copy(
                    src_ref=src, dst_ref=dsts[a].at[slab], send_sem=send_sems.at[a, k], recv_sem=recv_sems.at[a, k],
                    device_id=(px, py, pc), device_id_type=MESH))
        local.append(pltpu.make_async_copy(srcs[a].at[me] if scatter else srcs[a], dsts[a].at[me], loc_sems.at[a]))
    return sends, recvs, local


def _exchange_start(cps):
    for cp in cps[2] + cps[0]:
        cp.start()


def _exchange_wait(cps):
    for cp in cps[0]:
        cp.wait_send()
    for cp in cps[1]:
        cp.wait_recv()
    for cp in cps[2]:
        cp.wait()


def _exchange_shapes(scatter, srcs):
    return tuple(jax.ShapeDtypeStruct(((N_DEV,) + s.shape[-2:]), s.dtype) for s in srcs)


def _exchange_sems(n):
    return [pltpu.SemaphoreType.DMA((n, 7)), pltpu.SemaphoreType.DMA((n, 7)), pltpu.SemaphoreType.DMA((n,))]


def _exchange_call(scatter, srcs, name):
    n = len(srcs)

    def body(*refs):
        cps = _exchange_copies(scatter, refs[:n], refs[n:2 * n], *refs[2 * n:])
        _exchange_start(cps)
        _exchange_wait(cps)

    return pl.pallas_call(body, name=name, out_shape=_exchange_shapes(scatter, srcs), in_specs=[ANY] * n,
                          out_specs=(ANY,) * n, scratch_shapes=_exchange_sems(n))(*srcs)


def _pcall(body, *, name, grid, in_specs, out_specs, out_shape, scratch_shapes, sem, args, comm=None):
    if comm is None:
        res = pl.pallas_call(body, name=name, grid=grid, in_specs=list(in_specs), out_specs=tuple(out_specs),
                             out_shape=tuple(out_shape), scratch_shapes=list(scratch_shapes),
                             compiler_params=_params(sem))(*args)
        return tuple(res), ()
    scatter, srcs = comm
    n_in, n_out, n_c, n_s = len(in_specs), len(out_specs), len(srcs), len(scratch_shapes)

    def carrier(*refs):
        ins, c_src = refs[:n_in], refs[n_in:n_in + n_c]
        outs = refs[n_in + n_c:n_in + n_c + n_out]
        c_dst = refs[n_in + n_c + n_out:n_in + 2 * n_c + n_out]
        scr = refs[n_in + 2 * n_c + n_out:n_in + 2 * n_c + n_out + n_s]
        first = pl.program_id(0) == 0
        last = pl.program_id(0) == grid[0] - 1
        for ax in range(1, len(grid)):
            first = jnp.logical_and(first, pl.program_id(ax) == 0)
            last = jnp.logical_and(last, pl.program_id(ax) == grid[ax] - 1)

        @pl.when(first)
        def _():
            _exchange_start(_exchange_copies(scatter, c_src, c_dst, *refs[-3:]))

        body(*ins, *outs, *scr)

        @pl.when(last)
        def _():
            _exchange_wait(_exchange_copies(scatter, c_src, c_dst, *refs[-3:]))

    res = pl.pallas_call(
        carrier, name=name + "_x", grid=grid, in_specs=list(in_specs) + [ANY] * n_c,
        out_specs=tuple(out_specs) + (ANY,) * n_c, out_shape=tuple(out_shape) + _exchange_shapes(scatter, srcs),
        scratch_shapes=list(scratch_shapes) + _exchange_sems(n_c),
        compiler_params=_params(("arbitrary",) * len(grid)))(*args, *srcs)
    return tuple(res[:n_out]), tuple(res[n_out:])


def _two_level_gather_body(n_arr, x_refs, out_refs, send_sems, recv_sems, local_sems):
    x, y, c = _me()
    me, sibling = (x, y, c), (x, y, 1 - c)
    chips = [(1 - x, y), (x, 1 - y), (1 - x, 1 - y)]

    def slab(a, px, py, pc):
        return out_refs[a].at[4 * px + 2 * py + pc]

    def copy(a, k, block, to, src=None):
        return pltpu.make_async_remote_copy(
            src_ref=slab(a, *block) if src is None else src, dst_ref=slab(a, *block),
            send_sem=send_sems.at[a, k], recv_sem=recv_sems.at[a, k], device_id=to, device_id_type=MESH)

    mine = [pltpu.make_async_copy(x_refs[a], slab(a, *me), local_sems.at[a]) for a in range(n_arr)]
    for cp in mine:
        cp.start()
    first = []
    for a in range(n_arr):
        first.append(copy(a, 0, me, sibling, src=x_refs[a]))
        first += [copy(a, 1 + j, me, (*chip, c), src=x_refs[a]) for j, chip in enumerate(chips)]
    for cp in first:
        cp.start()
    passed = []
    for j, chip in enumerate(chips):
        for a in range(n_arr):
            copy(a, 1 + j, (*chip, c), me).wait_recv()
            fw = copy(a, 4 + j, (*chip, c), sibling)
            fw.start()
            passed.append(fw)
    for a in range(n_arr):
        copy(a, 0, sibling, me).wait_recv()
        for j, chip in enumerate(chips):
            copy(a, 4 + j, (*chip, 1 - c), me).wait_recv()
    for cp in first + passed:
        cp.wait_send()
    for cp in mine:
        cp.wait()


def _gather_small(x, name):
    def body(x_ref, out_ref, send_sems, recv_sems, local_sems):
        _two_level_gather_body(1, [x_ref], [out_ref], send_sems, recv_sems, local_sems)

    vm = pl.BlockSpec(memory_space=pltpu.VMEM)
    return pl.pallas_call(
        body, name=name,
        out_shape=jax.ShapeDtypeStruct((N_DEV,) + x.shape, x.dtype),
        in_specs=[vm], out_specs=vm,
        scratch_shapes=[pltpu.SemaphoreType.DMA((1, 7)), pltpu.SemaphoreType.DMA((1, 7)),
                        pltpu.SemaphoreType.DMA((1,))],
    )(x)


_SMALL = ["norm_mix_pre", "norm_mix_post", "norm_ffn_pre", "norm_ffn_post", "gla_gate_up", "gla_gate_b",
          "gla_norm", "ssd_conv_w", "ssd_conv_b", "ssd_dt_bias", "ssd_a_log", "ssd_d", "ssd_norm", "ret_norm"]


def _pack(arrs):
    flat = jnp.concatenate([a.reshape(-1) for a in arrs])
    n = flat.shape[0]
    npad = -(-n // 1024) * 1024
    return jnp.pad(flat, (0, npad - n)).reshape(npad // 128, 128)


def _unpack(buf, shapes):
    flat = buf.reshape(-1)
    out, o = [], 0
    for s in shapes:
        n = math.prod(s)
        out.append(flat[o:o + n].reshape(s))
        o += n
    return out


def kernel(x, c, ctx, c_ctx, ada_w, ada_b, norm_mix_pre, norm_mix_post, norm_ffn_pre, norm_ffn_post, w_in, w_out, gla_gate_up, gla_gate_b, gla_norm, ssd_conv_w, ssd_conv_b, ssd_dt_bias, ssd_a_log, ssd_d, ssd_norm, ret_norm, ffn_w13, ffn_w2, loss_target, m_c_ctx, m_ada_w, m_ada_b, m_norm_mix_pre, m_norm_mix_post, m_norm_ffn_pre, m_norm_ffn_post, m_w_in, m_w_out, m_gla_gate_up, m_gla_gate_b, m_gla_norm, m_ssd_conv_w, m_ssd_conv_b, m_ssd_dt_bias, m_ssd_a_log, m_ssd_d, m_ssd_norm, m_ret_norm, m_ffn_w13, m_ffn_w2, v_c_ctx, v_ada_w, v_ada_b, v_norm_mix_pre, v_norm_mix_post, v_norm_ffn_pre, v_norm_ffn_post, v_w_in, v_w_out, v_gla_gate_up, v_gla_gate_b, v_gla_norm, v_ssd_conv_w, v_ssd_conv_b, v_ssd_dt_bias, v_ssd_a_log, v_ssd_d, v_ssd_norm, v_ret_norm, v_ffn_w13, v_ffn_w2):
    P_ = dict(c_ctx=c_ctx, ada_w=ada_w, ada_b=ada_b, norm_mix_pre=norm_mix_pre, norm_mix_post=norm_mix_post,
              norm_ffn_pre=norm_ffn_pre, norm_ffn_post=norm_ffn_post, w_in=w_in, w_out=w_out,
              gla_gate_up=gla_gate_up, gla_gate_b=gla_gate_b, gla_norm=gla_norm, ssd_conv_w=ssd_conv_w,
              ssd_conv_b=ssd_conv_b, ssd_dt_bias=ssd_dt_bias, ssd_a_log=ssd_a_log, ssd_d=ssd_d,
              ssd_norm=ssd_norm, ret_norm=ret_norm, ffn_w13=ffn_w13, ffn_w2=ffn_w2)
    M_ = dict(c_ctx=m_c_ctx, ada_w=m_ada_w, ada_b=m_ada_b, norm_mix_pre=m_norm_mix_pre,
              norm_mix_post=m_norm_mix_post, norm_ffn_pre=m_norm_ffn_pre, norm_ffn_post=m_norm_ffn_post,
              w_in=m_w_in, w_out=m_w_out, gla_gate_up=m_gla_gate_up, gla_gate_b=m_gla_gate_b,
              gla_norm=m_gla_norm, ssd_conv_w=m_ssd_conv_w, ssd_conv_b=m_ssd_conv_b, ssd_dt_bias=m_ssd_dt_bias,
              ssd_a_log=m_ssd_a_log, ssd_d=m_ssd_d, ssd_norm=m_ssd_norm, ret_norm=m_ret_norm,
              ffn_w13=m_ffn_w13, ffn_w2=m_ffn_w2)
    V_ = dict(c_ctx=v_c_ctx, ada_w=v_ada_w, ada_b=v_ada_b, norm_mix_pre=v_norm_mix_pre,
              norm_mix_post=v_norm_mix_post, norm_ffn_pre=v_norm_ffn_pre, norm_ffn_post=v_norm_ffn_post,
              w_in=v_w_in, w_out=v_w_out, gla_gate_up=v_gla_gate_up, gla_gate_b=v_gla_gate_b,
              gla_norm=v_gla_norm, ssd_conv_w=v_ssd_conv_w, ssd_conv_b=v_ssd_conv_b, ssd_dt_bias=v_ssd_dt_bias,
              ssd_a_log=v_ssd_a_log, ssd_d=v_ssd_d, ssd_norm=v_ssd_norm, ret_norm=v_ret_norm,
              ffn_w13=v_ffn_w13, ffn_w2=v_ffn_w2)
    order = ["c_ctx", "ada_w", "ada_b", "norm_mix_pre", "norm_mix_post", "norm_ffn_pre", "norm_ffn_post", "w_in",
             "w_out", "gla_gate_up", "gla_gate_b", "gla_norm", "ssd_conv_w", "ssd_conv_b", "ssd_dt_bias",
             "ssd_a_log", "ssd_d", "ssd_norm", "ret_norm", "ffn_w13", "ffn_w2"]

    mx, my, mc_ = _me()
    me = 4 * mx + 2 * my + mc_
    Tl, Tc = x.shape[1], ctx.shape[1]
    n_in, n_out, n_13, n_2 = w_in.shape[2], w_out.shape[1], ffn_w13.shape[2], ffn_w2.shape[1]
    n_ada = ada_w.shape[2]

    shards = [[w_in[l].T.astype(BF16), w_out[l].astype(BF16), ffn_w13[l].T.astype(BF16), ffn_w2[l].astype(BF16)]
              for l in range(DEPTH)]

    cw = ssd_conv_w.shape[2]
    small_in = jnp.concatenate([jnp.pad(c, ((0, 7), (0, 0))).reshape(-1),
                                ssd_conv_w.reshape(-1)]).reshape(-1, 128)
    n_c_rows = 8 * D // 128
    small_in = jnp.pad(small_in, ((0, -small_in.shape[0] % 8), (0, 0)))
    gathered = _gather_small(small_in, "gather_c_conv")
    c_all = gathered[:, :n_c_rows].reshape(N_DEV, 8, D)[:, 0]
    conv_rows = DEPTH * 5 * cw // 128
    conv_full = gathered[:, n_c_rows:n_c_rows + conv_rows].reshape(N_DEV, DEPTH, 5, cw)
    conv_full = jnp.moveaxis(conv_full, 0, 2).reshape(DEPTH, 5, N_DEV * cw)
    c9 = jnp.concatenate([c_all, c_ctx[None], jnp.zeros((7, D), F32)], axis=0)
    s9 = c9 * jax.nn.sigmoid(c9)
    mod_piece = jnp.concatenate([_mm(s9, ada_w[l], name="mm_mod") for l in range(DEPTH)], axis=0)
    mod_g = _gather_small(mod_piece, "gather_mod")
    mod_all = jnp.moveaxis(mod_g.reshape(N_DEV, DEPTH, 16, n_ada), 0, 2).reshape(DEPTH, 16, N_DEV * n_ada)
    mod_all = mod_all + ada_b[:, None, :]
    mod_l = lax.dynamic_index_in_dim(mod_all, me, axis=1, keepdims=False)
    mod_c = mod_all[:, 8]

    sp = {n: P_[n] for n in _SMALL}
    sp["ssd_conv_w"] = conv_full
    xcat = jnp.concatenate([ctx[0], x[0]], axis=0)
    loss_local, d_xcat, d_mod_l, d_mod_c, d_sp, gw = _local_step(xcat, loss_target[0], mod_l, mod_c, sp, Tc,
                                                                 shards=shards)
    loss = lax.psum(loss_local, ("x", "y", "c"))
    grad_x = d_xcat[Tc:][None]

    G = {n: jnp.stack([_sum8_call(gw[l][a], f"sum_{n}") for l in range(DEPTH)])
         for a, n in enumerate(["w_in", "w_out", "ffn_w13", "ffn_w2"])}
    G["w_in"], G["ffn_w13"] = jnp.swapaxes(G["w_in"], 1, 2), jnp.swapaxes(G["ffn_w13"], 1, 2)

    dmod_rows = jnp.concatenate([d_mod_l, d_mod_c], axis=0)
    dmod_g = _gather_small(dmod_rows, "gather_dmod").reshape(N_DEV, 2, DEPTH, 6 * D)
    dl = jnp.moveaxis(dmod_g[:, 0], 0, 1)
    dc = dmod_g[:, 1, :, :]
    dc_tot = dc[0]
    for d_ in range(1, N_DEV):
        dc_tot = dc_tot + dc[d_]
    dmod9 = jnp.concatenate([dl, dc_tot[:, None, :], jnp.zeros((DEPTH, 7, 6 * D), F32)], axis=1)
    g_ada_b = dmod9[:, 0]
    for r_ in range(1, 9):
        g_ada_b = g_ada_b + dmod9[:, r_]
    dmod9_mine = lax.dynamic_slice_in_dim(dmod9, me * n_ada, n_ada, axis=2)
    s9T = jnp.pad(s9.T, ((0, 0), (0, 112)))
    g_ada_w = jnp.stack([_mm(s9T, jnp.pad(dmod9_mine[l], ((0, 112), (0, 0))), name="mm_dada")
                         for l in range(DEPTH)])
    ds9 = _mm(dmod9_mine[0], ada_w[0], trans_b=True, name="mm_ds9")
    for l in range(1, DEPTH):
        ds9 = _mm(dmod9_mine[l], ada_w[l], trans_b=True, name="mm_ds9_acc", add=ds9)
    ds_ctx_part = ds9[8]

    small_names = [n for n in _SMALL]
    small_parts = [d_sp[n] for n in small_names] + [ds_ctx_part]
    packed = _pack(small_parts)
    allp = _gather_small(packed, "gather_small_grads")
    summed = _sum_call([allp[d_] for d_ in range(N_DEV)], "sum_small_grads")
    parts = _unpack(summed, [p.shape for p in small_parts])
    for n, p in zip(small_names, parts[:-1]):
        G[n] = p
    sig = jax.nn.sigmoid(c_ctx)
    G["c_ctx"] = parts[-1] * (sig * (1.0 + c_ctx * (1.0 - sig)))
    G["ssd_conv_w"] = lax.dynamic_slice_in_dim(G["ssd_conv_w"], me * cw, cw, axis=2)
    G["ada_w"] = g_ada_w
    G["ada_b"] = g_ada_b

    delta, new_m, new_v = {}, {}, {}
    for n in ["ada_w", "w_in", "w_out", "ffn_w13", "ffn_w2"]:
        sh = P_[n].shape
        f2 = lambda a: a.reshape(sh[0] * sh[1], sh[2])
        d_, m_, v_ = _adamw_call(f2(P_[n]), f2(G[n]), f2(M_[n]), f2(V_[n]), f"adamw_{n}")
        delta[n], new_m[n], new_v[n] = d_.reshape(sh), m_.reshape(sh), v_.reshape(sh)
    rest = [n for n in order if n not in delta]
    shapes = [P_[n].shape for n in rest]
    d_, m_, v_ = _adamw_call(_pack([P_[n] for n in rest]), _pack([G[n] for n in rest]),
                             _pack([M_[n] for n in rest]), _pack([V_[n] for n in rest]), "adamw_small")
    for n, a, b, e in zip(rest, _unpack(d_, shapes), _unpack(m_, shapes), _unpack(v_, shapes)):
        delta[n], new_m[n], new_v[n] = a, b, e

    return (loss, grad_x, *[G[n] for n in order], *[delta[n] for n in order],
            *[new_m[n] for n in order], *[new_v[n] for n in order])
```

```python
import functools
import math

import jax
import jax.numpy as jnp
from jax import lax
from jax.experimental import pallas as pl
from jax.experimental.pallas import tpu as pltpu

F32 = jnp.float32
BF16 = jnp.bfloat16

D = 1024
DEPTH = 4
GRID_W = 64
RMS_EPS = 1e-6
GLA_TAU = 16.0
FFN_H = 2816
IN_COLS = 3376
N_DEV = 8
ADAM_LR, ADAM_B1, ADAM_B2, ADAM_EPS, ADAM_WD, ADAM_STEP = 0.001, 0.9, 0.999, 1e-08, 0.01, 10

VMEM_LIMIT = 48 * 1024 * 1024

_ORIG = dict(gla_q=(0, 128), gla_k=(128, 128), gla_v=(256, 256), gla_r=(512, 256), gla_lr=(768, 32),
             ssd_z=(800, 512), ssd_xbc=(1312, 1024), ssd_dt=(2336, 16), ret_q=(2352, 256), ret_k=(2608, 256),
             ret_v=(2864, 256), ret_g=(3120, 256))
_R_ORDER = ["gla_v", "gla_r", "ret_q", "ret_k", "ret_v", "ret_g", "ssd_z", "gla_q", "gla_k", "gla_lr", "ssd_dt"]
R_W = 2560
_ROFF = {}
_o = 0
for _n in _R_ORDER:
    _ROFF[_n] = _o
    _o += _ORIG[_n][1]
MISC = _ROFF["gla_lr"]
assert MISC == 2304 and _o == 2352


def _split_w_in(wt):
    xs, xz = _ORIG["ssd_xbc"]
    parts = [wt[_ORIG[n][0]:_ORIG[n][0] + _ORIG[n][1]] for n in _R_ORDER]
    parts.append(jnp.zeros((R_W - _o,) + wt.shape[1:], wt.dtype))
    return wt[xs:xs + xz], jnp.concatenate(parts, axis=0)


def _merge_w_in(wx, wr):
    pieces = []
    for n, (s, z) in sorted(_ORIG.items(), key=lambda t: t[1][0]):
        pieces.append(wx if n == "ssd_xbc" else wr[_ROFF[n]:_ROFF[n] + z])
    return jnp.concatenate(pieces, axis=0)


def _pick(n, cands):
    for c in cands:
        if n % c == 0:
            return c
    return n


def _params(sem=None):
    kw = dict(vmem_limit_bytes=VMEM_LIMIT)
    if sem is not None:
        kw["dimension_semantics"] = sem
    return pltpu.CompilerParams(**kw)


def _iota(shape, dim):
    return lax.broadcasted_iota(jnp.int32, shape, dim)


def _dot(a, b, dims):
    return lax.dot_general(a, b, (dims, ((), ())), preferred_element_type=F32)


_NN = ((1,), (0,))
_NT = ((1,), (1,))
_TN = ((0,), (0,))


def _bf(x):
    return x.astype(BF16)


def _dot_sel(x, e, dims, x_left=True):
    eb = e.astype(BF16)
    hi = x.astype(BF16)
    r1 = x - hi.astype(F32)
    mid = r1.astype(BF16)
    lo = (r1 - mid.astype(F32)).astype(BF16)
    out = None
    for p in (hi, mid, lo):
        t = _dot(p, eb, dims) if x_left else _dot(eb, p, dims)
        out = t if out is None else out + t
    return out


@jax.custom_vjp
def _sel(x, e):
    return _dot_sel(x, e, _NN)


_sel.defvjp(lambda x, e: (_dot_sel(x, e, _NN), e), lambda e, g: (_dot_sel(g, e, _NT), jnp.zeros_like(e)))


def _sig(x):
    e = jnp.exp(-jnp.abs(x))
    return jnp.where(x >= 0, 1.0 / (1.0 + e), e / (1.0 + e))


@jax.custom_vjp
def _sigmoid(x):
    return _sig(x)


def _sigmoid_fwd(x):
    s = _sig(x)
    return s, s


_sigmoid.defvjp(_sigmoid_fwd, lambda s, g: (g * s * (1.0 - s),))


def _silu(x):
    return x * _sigmoid(x)


@jax.custom_vjp
def _softplus(x):
    return jnp.maximum(x, 0.0) + jnp.log(1.0 + jnp.exp(-jnp.abs(x)))


_softplus.defvjp(lambda x: (jnp.maximum(x, 0.0) + jnp.log(1.0 + jnp.exp(-jnp.abs(x))), x),
                 lambda x, g: (g * _sig(x),))


def _log_sigmoid(x):
    return -_softplus(-x)


@jax.custom_vjp
def _mm_bf(x, w):
    return _dot(_bf(x), _bf(w), _NN)


_mm_bf.defvjp(lambda x, w: (_dot(_bf(x), _bf(w), _NN), (x, w)),
              lambda r, g: (_dot(_bf(g), _bf(r[1]), _NT), _dot(_bf(r[0]), _bf(g), _TN)))


_TILE_M = (1088, 1024, 512, 256, 128, 64, 32, 16)
_TILE_N = (1408, 1280, 1024, 768, 512, 384, 256, 128)
_TILE_K = (1408, 1280, 1024, 768, 512, 384, 256, 128)


def _mm(a, b, *, trans_b=False, name, add=None, out_dtype=F32):
    M, K = a.shape
    N = b.shape[0] if trans_b else b.shape[1]
    assert (b.shape[1] if trans_b else b.shape[0]) == K
    tm, tn, tk = _pick(M, _TILE_M), _pick(N, _TILE_N), _pick(K, _TILE_K)
    nk = K // tk
    dims = _NT if trans_b else _NN
    has_add = add is not None

    def body(*refs):
        a_ref, b_ref = refs[0], refs[1]
        o_ref, acc_ref = refs[-2], refs[-1]
        k = pl.program_id(2)

        @pl.when(k == 0)
        def _():
            acc_ref[...] = refs[2][...] if has_add else jnp.zeros_like(acc_ref)

        acc_ref[...] += _dot(a_ref[...].astype(BF16), b_ref[...].astype(BF16), dims)

        @pl.when(k == nk - 1)
        def _():
            o_ref[...] = acc_ref[...].astype(o_ref.dtype)

    b_spec = (pl.BlockSpec((tn, tk), lambda i, j, k: (j, k)) if trans_b
              else pl.BlockSpec((tk, tn), lambda i, j, k: (k, j)))
    o_spec = pl.BlockSpec((tm, tn), lambda i, j, k: (i, j))
    return pl.pallas_call(
        body, name=name,
        out_shape=jax.ShapeDtypeStruct((M, N), out_dtype),
        grid=(M // tm, N // tn, nk),
        in_specs=[pl.BlockSpec((tm, tk), lambda i, j, k: (i, k)), b_spec] + ([o_spec] if has_add else []),
        out_specs=o_spec,
        scratch_shapes=[pltpu.VMEM((tm, tn), F32)],
        compiler_params=_params(("parallel", "parallel", "arbitrary")),
    )(*((a, b, add) if has_add else (a, b)))


def _mm_tn(a, g, *, name, out_dtype=F32):
    M, K = a.shape
    N = g.shape[1]
    tm, tk, tn = _pick(M, _TILE_M), _pick(K, _TILE_K), _pick(N, _TILE_N)
    nm = M // tm

    def body(a_ref, g_ref, o_ref, acc_ref):
        i = pl.program_id(2)

        @pl.when(i == 0)
        def _():
            acc_ref[...] = jnp.zeros_like(acc_ref)

        acc_ref[...] += _dot(a_ref[...].astype(BF16), g_ref[...].astype(BF16), _TN)

        @pl.when(i == nm - 1)
        def _():
            o_ref[...] = acc_ref[...].astype(o_ref.dtype)

    return pl.pallas_call(
        body, name=name,
        out_shape=jax.ShapeDtypeStruct((K, N), out_dtype),
        grid=(K // tk, N // tn, nm),
        in_specs=[pl.BlockSpec((tm, tk), lambda k, j, i: (i, k)), pl.BlockSpec((tm, tn), lambda k, j, i: (i, j))],
        out_specs=pl.BlockSpec((tk, tn), lambda k, j, i: (k, j)),
        scratch_shapes=[pltpu.VMEM((tk, tn), F32)],
        compiler_params=_params(("parallel", "parallel", "arbitrary")),
    )(a, g)


def _norm_fwd_call(x, w, a2, b2, res, tr, out_dtype=F32):
    T, W = x.shape
    has_res = res is not None

    def body(*refs):
        x_ref, w_ref, a_ref, b_ref = refs[:4]
        y_ref = refs[-1]
        seg = jnp.minimum(pl.program_id(0), 1)
        xv = x_ref[...]
        rstd = lax.rsqrt(jnp.mean(xv * xv, axis=-1, keepdims=True) + RMS_EPS)
        y = a_ref[pl.ds(seg, 1), :] * (xv * rstd * w_ref[...]) + b_ref[pl.ds(seg, 1), :]
        y_ref[...] = (y + refs[4][...] if has_res else y).astype(y_ref.dtype)

    row = pl.BlockSpec((tr, W), lambda i: (i, 0))
    small = pl.BlockSpec((8, W), lambda i: (0, 0))
    return pl.pallas_call(
        body, name="norm_fwd",
        out_shape=jax.ShapeDtypeStruct((T, W), out_dtype),
        grid=(T // tr,),
        in_specs=[row, pl.BlockSpec((1, W), lambda i: (0, 0)), small, small] + ([row] if has_res else []),
        out_specs=row,
        compiler_params=_params(("parallel",)),
    )(*((x, w.reshape(1, W), a2, b2) + ((res,) if has_res else ())))


def _norm_bwd_call(x, w, a2, dy, tr, add=None, out_dtype=F32):
    T, W = x.shape
    has_add = add is not None

    def body(*refs):
        x_ref, w_ref, a_ref, dy_ref = refs[:4]
        dx_ref, dw_ref, da_ref, db_ref = refs[-4:]
        i = pl.program_id(0)
        seg = jnp.minimum(i, 1)

        @pl.when(i == 0)
        def _():
            dw_ref[...] = jnp.zeros_like(dw_ref)
            da_ref[...] = jnp.zeros_like(da_ref)
            db_ref[...] = jnp.zeros_like(db_ref)

        xv = x_ref[...]
        g = dy_ref[...]
        wv = w_ref[...]
        rstd = lax.rsqrt(jnp.mean(xv * xv, axis=-1, keepdims=True) + RMS_EPS)
        xh = xv * rstd
        da_ref[pl.ds(seg, 1), :] += jnp.sum(g * (xh * wv), axis=0, keepdims=True)
        db_ref[pl.ds(seg, 1), :] += jnp.sum(g, axis=0, keepdims=True)
        gy = g * a_ref[pl.ds(seg, 1), :]
        dw_ref[0:1, :] += jnp.sum(gy * xh, axis=0, keepdims=True)
        gx = gy * wv
        dx = rstd * (gx - xh * jnp.mean(gx * xh, axis=-1, keepdims=True))
        dx_ref[...] = (dx + refs[4][...] if has_add else dx).astype(dx_ref.dtype)

    acc = jax.ShapeDtypeStruct((8, W), F32)
    acc_spec = pl.BlockSpec((8, W), lambda i: (0, 0))
    row = pl.BlockSpec((tr, W), lambda i: (i, 0))
    return pl.pallas_call(
        body, name="norm_bwd",
        out_shape=(jax.ShapeDtypeStruct((T, W), out_dtype), acc, acc, acc),
        grid=(T // tr,),
        in_specs=[row, pl.BlockSpec((1, W), lambda i: (0, 0)), acc_spec, row] + ([row] if has_add else []),
        out_specs=(row, acc_spec, acc_spec, acc_spec),
        compiler_params=_params(("arbitrary",)),
    )(*((x, w.reshape(1, W), a2, dy) + ((add,) if has_add else ())))


def _act_call(u1, u2, dact=None):
    T, W = u1.shape
    tr = _pick(T, (512, 256, 128, 64))
    tn = _pick(W, (1408, 512, 256, 128))
    spec = pl.BlockSpec((tr, tn), lambda i, j: (i, j))
    sh = jax.ShapeDtypeStruct((T, W), BF16)
    if dact is None:
        def body(a_ref, b_ref, o_ref):
            a = a_ref[...].astype(F32)
            o_ref[...] = (a * _sig(a) * b_ref[...].astype(F32)).astype(o_ref.dtype)

        return pl.pallas_call(body, name="act_fwd", out_shape=sh, grid=(T // tr, W // tn), in_specs=[spec, spec],
                              out_specs=spec, compiler_params=_params(("parallel", "parallel")))(u1, u2)

    def body(a_ref, b_ref, g_ref, da_ref, db_ref):
        a, g = a_ref[...].astype(F32), g_ref[...].astype(F32)
        s = _sig(a)
        da_ref[...] = (g * b_ref[...].astype(F32) * (s * (1.0 + a * (1.0 - s)))).astype(da_ref.dtype)
        db_ref[...] = (g * a * s).astype(db_ref.dtype)

    return pl.pallas_call(body, name="act_bwd", out_shape=(sh, sh), grid=(T // tr, W // tn),
                          in_specs=[spec, spec, spec], out_specs=(spec, spec),
                          compiler_params=_params(("parallel", "parallel")))(u1, u2, dact)


def _conv_specs(T, Wc, tr):
    hb, nt = tr // 8, T // tr
    row = pl.BlockSpec((tr, Wc), lambda i: (i, 0))
    prev = pl.BlockSpec((8, Wc), lambda i: (jnp.maximum(i * hb - 1, 0), 0))
    nxt = pl.BlockSpec((8, Wc), lambda i: (jnp.minimum((i + 1) * hb, T // 8 - 1), 0))
    return row, prev, nxt, nt


def _fill_ext(dst_ref, cur_ref, prev_ref, next_ref, i, nt, tr):
    has_prev = (i > 1).astype(F32)
    has_next = jnp.logical_and(i > 0, i < nt - 1).astype(F32)
    dst_ref[8:16, :] = prev_ref[...] * has_prev
    dst_ref[16:16 + tr, :] = cur_ref[...]
    dst_ref[16 + tr:24 + tr, :] = next_ref[...] * has_next


def _conv_fwd_call(px, w8, b, tr):
    T, Wc = px.shape
    row, prev, nxt, nt = _conv_specs(T, Wc, tr)

    def body(x_ref, xp_ref, xn_ref, w_ref, b_ref, u_ref, xe_ref):
        i = pl.program_id(0)

        @pl.when(i == 0)
        def _():
            xe_ref[...] = jnp.zeros_like(xe_ref)

        _fill_ext(xe_ref, x_ref, xp_ref, xn_ref, i, nt, tr)
        y = b_ref[...] + w_ref[0:1, :] * xe_ref[pl.ds(14, tr), :]
        for k in range(1, 5):
            y = y + w_ref[k:k + 1, :] * xe_ref[pl.ds(14 + k, tr), :]
        u_ref[...] = y * _sig(y)

    return pl.pallas_call(
        body, name="conv_fwd", out_shape=jax.ShapeDtypeStruct((T, Wc), F32), grid=(nt,),
        in_specs=[row, prev, nxt, pl.BlockSpec((8, Wc), lambda i: (0, 0)), pl.BlockSpec((1, Wc), lambda i: (0, 0))],
        out_specs=row, scratch_shapes=[pltpu.VMEM((tr + 32, Wc), F32)],
        compiler_params=_params(("arbitrary",)),
    )(px, px, px, w8, b)


def _conv_bwd_call(px, w8, b, du, tr):
    T, Wc = px.shape
    row, prev, nxt, nt = _conv_specs(T, Wc, tr)
    E = tr + 16

    def body(x_ref, xp_ref, xn_ref, g_ref, gp_ref, gn_ref, w_ref, b_ref, dx_ref, dw_ref, db_ref,
             xe_ref, ge_ref, dy_ref):
        i = pl.program_id(0)

        @pl.when(i == 0)
        def _():
            xe_ref[...] = jnp.zeros_like(xe_ref)
            ge_ref[...] = jnp.zeros_like(ge_ref)
            dy_ref[...] = jnp.zeros_like(dy_ref)
            dw_ref[...] = jnp.zeros_like(dw_ref)
            db_ref[...] = jnp.zeros_like(db_ref)

        _fill_ext(xe_ref, x_ref, xp_ref, xn_ref, i, nt, tr)
        _fill_ext(ge_ref, g_ref, gp_ref, gn_ref, i, nt, tr)
        y = b_ref[...] + w_ref[0:1, :] * xe_ref[pl.ds(6, E), :]
        for k in range(1, 5):
            y = y + w_ref[k:k + 1, :] * xe_ref[pl.ds(6 + k, E), :]
        s = _sig(y)
        dy = ge_ref[pl.ds(8, E), :] * (s * (1.0 + y * (1.0 - s)))
        dy_ref[pl.ds(8, E), :] = dy
        dx = w_ref[0:1, :] * dy_ref[pl.ds(18, tr), :]
        for k in range(1, 5):
            dx = dx + w_ref[k:k + 1, :] * dy_ref[pl.ds(18 - k, tr), :]
        dx_ref[...] = dx.astype(dx_ref.dtype)
        dyt = dy_ref[pl.ds(16, tr), :]
        db_ref[0:1, :] += jnp.sum(dyt, axis=0, keepdims=True)
        for k in range(5):
            dw_ref[k:k + 1, :] += jnp.sum(dyt * xe_ref[pl.ds(14 + k, tr), :], axis=0, keepdims=True)

    acc = jax.ShapeDtypeStruct((8, Wc), F32)
    acc_spec = pl.BlockSpec((8, Wc), lambda i: (0, 0))
    ext = pltpu.VMEM((tr + 32, Wc), F32)
    return pl.pallas_call(
        body, name="conv_bwd", out_shape=(jax.ShapeDtypeStruct((T, Wc), BF16), acc, acc), grid=(nt,),
        in_specs=[row, prev, nxt, row, prev, nxt, acc_spec, pl.BlockSpec((1, Wc), lambda i: (0, 0))],
        out_specs=(row, acc_spec, acc_spec), scratch_shapes=[ext, ext, ext],
        compiler_params=_params(("arbitrary",)),
    )(px, px, px, du, du, du, w8, b)


_SCAN_CFG = {
    "gla": dict(H=4, Dk=32, Dv=64, nh=4, scalar=False, C=128),
    "ssd": dict(H=8, Dk=128, Dv=64, nh=2, scalar=True, C=128),
    "ret": dict(H=4, Dk=64, Dv=64, nh=4, scalar=True, C=128),
}
GPAD = 8


def _log2(n):
    r = int(math.log2(n))
    assert 1 << r == n
    return r


class _ScanMath:
    def __init__(self, cfg, reverse):
        C = cfg["C"]
        self.C, self.reverse = C, reverse
        self.Dk, self.Dv, self.nh, self.scalar = cfg["Dk"], cfg["Dv"], cfg["nh"], cfg["scalar"]
        self.Wk, self.Wv = self.nh * self.Dk, self.nh * self.Dv
        self.nsg = cfg["H"] // self.nh
        nh, Wk, Wv = self.nh, self.Wk, self.Wv
        lk, lv, lc = _log2(self.Dk), _log2(self.Dv), _log2(C)
        r, c = _iota((C, C), 0), _iota((C, C), 1)
        self.L = ((c >= r) if reverse else (c <= r)).astype(F32)
        self.Lsuf = ((c <= r) if reverse else (c >= r)).astype(F32)
        i, j = _iota((C, nh * C), 0), _iota((C, nh * C), 1) & (C - 1)
        self.Mst = (j >= i) if reverse else (j <= i)
        self.Dj = (i == j).astype(F32)
        self.nb = 1 if (self.scalar or C == 64) else 3
        assert self.scalar or C in (64, 128)
        lanes = _iota((1, self.nb * Wk), 1) & (Wk - 1)
        self.km = [((lanes >> lk) == h).astype(F32) for h in range(nh)]
        self.vm = [((_iota((1, Wv), 1) >> lv) == h).astype(F32) for h in range(nh)]
        self.BD = ((_iota((Wv, Wk), 0) >> lv) == (_iota((Wv, Wk), 1) >> lk)).astype(F32)
        self.last = 0 if reverse else C - 1
        self.last_row = (_iota((C, 1), 0) == self.last).astype(F32)
        self.lk, self.lc = lk, lc
        self.H = cfg["H"]

    def gates(self, g):
        if not self.scalar:
            return _dot_sel(g, self.L, _NN, x_left=False), None
        G8 = _dot_sel(g, self.L, _NN, x_left=False)
        nk, ncol = self.H * self.Dk, self.H * self.C
        ek = (_iota((GPAD, nk), 0) == (_iota((GPAD, nk), 1) >> self.lk)).astype(F32)
        ec = (_iota((GPAD, ncol), 0) == (_iota((GPAD, ncol), 1) >> self.lc)).astype(F32)
        return _dot_sel(G8, ek, _NN), _dot_sel(G8, ec, _NN)

    def Ek(self, s):
        return (_iota((GPAD, self.Wk), 0) == (_iota((GPAD, self.Wk), 1) >> self.lk) + s * self.nh).astype(F32)

    def fold(self, x, factors=None):
        Wk = self.Wk
        out = None
        for b in range(self.nb):
            t = x[:, b * Wk:(b + 1) * Wk]
            t = t if factors is None or factors[b] is None else t * factors[b]
            out = t if out is None else out + t
        return out

    def kstack(self, x):
        return jnp.concatenate([x * self.km[h] for h in range(self.nh)], axis=0)

    def vstack(self, x):
        return jnp.concatenate([x * self.vm[h] for h in range(self.nh)], axis=0)

    def unstack(self, R, masks):
        C = self.C
        out = R[0:C] * masks[0]
        for h in range(1, self.nh):
            out = out + R[h * C:(h + 1) * C] * masks[h]
        return out

    def chunk(self, qs, ks, Gk, Gc):
        C = self.C
        Glast = Gk[self.last:self.last + 1, :]
        out = dict(Gk=Gk, Glast=Glast, eG=jnp.exp(Gk), eGl=jnp.exp(Glast - Gk), eGlast=jnp.exp(Glast))
        if self.scalar:
            Gr = jnp.sum(Gc * self.Dj, axis=0, keepdims=True)
            dec = jnp.where(self.Mst, jnp.exp(jnp.minimum(Gc - Gr, 0.0)), 0.0)
            qt, kt = qs, ks
            A = _dot(_bf(qt), _bf(self.kstack(kt)), _NT) * dec
            out.update(dec=dec, qt=qt, kt=kt, A=A, fq=[None], fk=[None])
        elif self.nb == 1:
            Gm = Gk[C // 2:C // 2 + 1, :]
            fq, fk = [jnp.exp(Gk - Gm)], [jnp.exp(Gm - Gk)]
            qt, kt = qs * fq[0], ks * fk[0]
            A = jnp.where(self.Mst, _dot(_bf(qt), _bf(self.kstack(kt)), _NT), 0.0)
            out.update(fq=fq, fk=fk, qt=qt, kt=kt, A=A)
        else:
            h = C // 2
            rows = _iota((C, 1), 0)
            early = (rows >= h) if self.reverse else (rows < h)
            late = jnp.logical_not(early)
            m_e, m_l, b = (h + h // 2, h // 2, h) if self.reverse else (h // 2, h + h // 2, h - 1)
            Ge, Gl, Gb = Gk[m_e:m_e + 1, :], Gk[m_l:m_l + 1, :], Gk[b:b + 1, :]

            def factor(mask, arg):
                return jnp.where(mask, jnp.exp(jnp.where(mask, arg, 0.0)), 0.0)

            fq = [factor(early, Gk - Ge), factor(late, Gk - Gl), factor(late, Gk - Gb)]
            fk = [factor(early, Ge - Gk), factor(late, Gl - Gk), factor(early, Gb - Gk)]
            qt = jnp.concatenate([qs * f for f in fq], axis=1)
            kt = jnp.concatenate([ks * f for f in fk], axis=1)
            A = jnp.where(self.Mst, _dot(_bf(qt), _bf(self.kstack(kt)), _NT), 0.0)
            out.update(fq=fq, fk=fk, qt=qt, kt=kt, A=A)
        return out


def _chunk_index(p, n, nc, reverse):
    if not reverse:
        return p
    return jnp.where(p < nc, nc - 1 - p, n - 1 + nc - p)


def _scan_dims(kind):
    cfg = _SCAN_CFG[kind]
    HK, HV = cfg["H"] * cfg["Dk"], cfg["H"] * cfg["Dv"]
    return cfg, cfg["C"], HK, HV, (GPAD if cfg["scalar"] else HK)


def _scan_fwd_step(m, q_ref, k_ref, v_ref, g_ref, o_ref, st_ref, S_ref):
    C = m.C

    @pl.when(pl.program_id(0) == 0)
    def _():
        S_ref[...] = jnp.zeros_like(S_ref)

    Gk_all, Gc_all = m.gates(g_ref[...])
    for s in range(m.nsg):
        ksl, vsl = slice(s * m.Wk, (s + 1) * m.Wk), slice(s * m.Wv, (s + 1) * m.Wv)
        csl = slice(s * m.nh * C, (s + 1) * m.nh * C)
        qs, ks, vs = q_ref[:, ksl], k_ref[:, ksl], v_ref[:, vsl]
        ch = m.chunk(qs, ks, Gk_all[:, ksl], Gc_all[:, csl] if m.scalar else None)
        S = S_ref[vsl, :]
        o = _dot(_bf(ch["A"]), _bf(m.vstack(vs)), _NN) + _dot(_bf(qs * ch["eG"]), _bf(S), _NT)
        o_ref[:, vsl] = o
        st_ref[0, vsl, :] = S
        S_ref[vsl, :] = S * ch["eGlast"] + _dot(_bf(vs), _bf(ks * ch["eGl"]), _TN) * m.BD


def _scan_fwd_call(kind, ops, Tc, comm=None):
    cfg, C, HK, HV, GW = _scan_dims(kind)
    T = ops[False][0][0].shape[0]
    n, nc = T // C, Tc // C

    def body(*refs):
        for d, rev in enumerate((False, True)):
            _scan_fwd_step(_ScanMath(cfg, rev), *refs[4 * d:4 * d + 4], *refs[8 + 2 * d:10 + 2 * d], refs[12 + d])

    sg = cfg["H"] // cfg["nh"]
    Wk, Wv = cfg["nh"] * cfg["Dk"], cfg["nh"] * cfg["Dv"]
    col = lambda rev, w, j: pl.BlockSpec((C, w), lambda p: (_chunk_index(p, n, nc, rev), j))
    st_spec = lambda rev: pl.BlockSpec((1, sg * Wv, Wk), lambda p: (_chunk_index(p, n, nc, rev), 0, 0))
    in_specs, args, out_specs, out_shape = [], [], [], []
    for rev in (False, True):
        q, k, v, g = ops[rev]
        in_specs += [col(rev, HK, q[1]), col(rev, HK, k[1]), col(rev, HV, v[1]), col(rev, GW, g[1])]
        args += [q[0], k[0], v[0], g[0]]
        out_specs += [col(rev, HV, 0), st_spec(rev)]
        out_shape += [jax.ShapeDtypeStruct((T, HV), F32), jax.ShapeDtypeStruct((n, sg * Wv, Wk), F32)]
    res, got = _pcall(body, name=f"scan_fwd_{kind}", out_shape=out_shape, grid=(n,), in_specs=in_specs,
                      out_specs=out_specs, scratch_shapes=[pltpu.VMEM((sg * Wv, Wk), F32)] * 2,
                      sem=("arbitrary",), args=args, comm=comm)
    return {False: (res[0], res[1]), True: (res[2], res[3])}, got


def _scan_bwd_step(m, need_dg, q_ref, k_ref, v_ref, g_ref, st_ref, do_ref, dq_ref, dk_ref, dv_ref, dg_ref, dS_ref):
    C = m.C

    @pl.when(pl.program_id(0) == 0)
    def _():
        dS_ref[...] = jnp.zeros_like(dS_ref)

    x8 = jnp.zeros((C, GPAD), F32)
    Gk_all, Gc_all = m.gates(g_ref[...])
    for s in range(m.nsg):
        ksl, vsl = slice(s * m.Wk, (s + 1) * m.Wk), slice(s * m.Wv, (s + 1) * m.Wv)
        csl = slice(s * m.nh * C, (s + 1) * m.nh * C)
        qs, ks, vs, dos = q_ref[:, ksl], k_ref[:, ksl], v_ref[:, vsl], do_ref[:, vsl]
        ch = m.chunk(qs, ks, Gk_all[:, ksl], Gc_all[:, csl] if m.scalar else None)
        S = st_ref[0, vsl, :]
        dS = dS_ref[vsl, :]
        A, qt, kt = ch["A"], ch["qt"], ch["kt"]
        dA = _dot(_bf(dos), _bf(m.vstack(vs)), _NT)
        dAm = dA * ch["dec"] if m.scalar else jnp.where(m.Mst, dA, 0.0)
        kst = _bf(m.kstack(kt))
        dv = m.unstack(_dot(_bf(A), _bf(dos), _TN), m.vm) + _dot(_bf(ks * ch["eGl"]), _bf(dS), _NT)
        dv_ref[:, vsl] = dv
        dq_i = _dot(_bf(dAm), kst, _NN)
        dq_x = ch["eG"] * _dot(_bf(dos), _bf(S), _NN)
        dq_ref[:, ksl] = m.fold(dq_i, ch["fq"]) + dq_x
        dk_i = m.unstack(_dot(_bf(dAm), _bf(qt), _TN), m.km)
        dk_x = ch["eGl"] * _dot(_bf(vs), _bf(dS), _NN)
        dk_ref[:, ksl] = m.fold(dk_i, ch["fk"]) + dk_x
        if need_dg:
            bnd = (ch["eGlast"] * jnp.sum(dS * S, axis=0, keepdims=True)
                   + jnp.sum(ks * dk_x, axis=0, keepdims=True))
            X = m.fold(_bf(qt).astype(F32) * dq_i - _bf(kt).astype(F32) * dk_i) + (qs * dq_x - ks * dk_x)
            X = X + m.last_row * bnd
            if m.scalar:
                x8 = x8 + _dot_sel(X, m.Ek(s), _NT)
            else:
                dg_ref[:, ksl] = _dot_sel(X, m.Lsuf, _NN, x_left=False)
        dS_ref[vsl, :] = dS * ch["eGlast"] + _dot(_bf(dos), _bf(qs * ch["eG"]), _TN) * m.BD
    if m.scalar:
        dg_ref[...] = _dot_sel(x8, m.Lsuf, _NN, x_left=False)
    elif not need_dg:
        dg_ref[...] = jnp.zeros_like(dg_ref)


def _scan_bwd_call(kind, need_dg, ops, st, do, Tc, comm=None):
    cfg, C, HK, HV, GW = _scan_dims(kind)
    T = ops[False][0][0].shape[0]
    n, nc = T // C, Tc // C

    def body(*refs):
        for d, rev in enumerate((False, True)):
            _scan_bwd_step(_ScanMath(cfg, rev), need_dg, *refs[6 * d:6 * d + 6], *refs[12 + 4 * d:16 + 4 * d],
                           refs[20 + d])

    sg = cfg["H"] // cfg["nh"]
    Wk, Wv = cfg["nh"] * cfg["Dk"], cfg["nh"] * cfg["Dv"]
    col = lambda rev, w, j: pl.BlockSpec((C, w), lambda p: (_chunk_index(n - 1 - p, n, nc, rev), j))
    st_spec = lambda rev: pl.BlockSpec((1, sg * Wv, Wk), lambda p: (_chunk_index(n - 1 - p, n, nc, rev), 0, 0))
    in_specs, args, out_specs, out_shape = [], [], [], []
    for rev in (False, True):
        q, k, v, g = ops[rev]
        in_specs += [col(rev, HK, q[1]), col(rev, HK, k[1]), col(rev, HV, v[1]), col(rev, GW, g[1]),
                     st_spec(rev), col(rev, HV, 0)]
        args += [q[0], k[0], v[0], g[0], st[rev], do]
        out_specs += [col(rev, HK, 0), col(rev, HK, 0), col(rev, HV, 0), col(rev, GW, 0)]
        out_shape += [jax.ShapeDtypeStruct((T, w), F32) for w in (HK, HK, HV, GW)]
    res, got = _pcall(body, name=f"scan_bwd_{kind}", out_shape=out_shape, grid=(n,), in_specs=in_specs,
                      out_specs=out_specs, scratch_shapes=[pltpu.VMEM((sg * Wv, Wk), F32)] * 2,
                      sem=("arbitrary",), args=args, comm=comm)
    return {False: res[0:4], True: res[4:8]}, got


def _prep_consts():
    r, c = _iota((256, 256), 0), _iota((256, 256), 1)
    first = (c & 63) < 32
    rope_perm = jnp.where(first, -(r == c + 32).astype(F32), (r == c - 32).astype(F32))
    sel_f = (_iota((128, GPAD), 0) == _iota((128, GPAD), 1) + 32).astype(F32)
    sel_b = (_iota((128, GPAD), 0) == _iota((128, GPAD), 1) + 40).astype(F32)
    ek = (_iota((GPAD, 1024), 0) == (_iota((GPAD, 1024), 1) >> 7)).astype(F32)
    return rope_perm, sel_f, sel_b, ek


def _prep_tile(misc, gq, rq, rk, bm, cm, cosE, sinE, Wg, gbias, dtbf, dtbb, nAf, nAb):
    rope_perm, sel_f, sel_b, ek = _prep_consts()
    logg = _log_sigmoid(_mm_bf(misc, Wg) + gbias) * (1.0 / GLA_TAU)
    a_gla = jnp.concatenate([gq * (32 ** -0.5), logg], axis=1)
    rot = lambda t: t * cosE + _sel(t, rope_perm) * sinE
    a_ret = jnp.concatenate([rot(rq * (64 ** -0.5)), rot(rk)], axis=1)
    dtf = _softplus(_sel(misc, sel_f) + dtbf)
    dtb = _softplus(_sel(misc, sel_b) + dtbb)
    rep = lambda t: jnp.concatenate([t[:, :128]] * 4 + [t[:, 128:]] * 4, axis=1)
    bmr = rep(bm)
    return a_gla, a_ret, rep(cm), bmr * _sel(dtf, ek), bmr * _sel(dtb, ek), dtf * nAf, dtb * nAb


def _prep_row_specs(tr):
    blk = lambda w, j: pl.BlockSpec((tr, w), lambda i: (i, j))
    return [blk(128, MISC // 128), blk(128, _ROFF["gla_q"] // 128), blk(256, _ROFF["ret_q"] // 256),
            blk(256, _ROFF["ret_k"] // 256), blk(256, 2), blk(256, 3), blk(256, 0), blk(256, 0)]


def _whole(a):
    return pl.BlockSpec(a.shape, lambda i: (0,) * a.ndim)


def _prep_fwd_call(Pr, u, cosE, sinE, pp, tr):
    T = Pr.shape[0]
    n_row = 8

    def body(*refs):
        outs = _prep_tile(*[r[...] for r in refs[:n_row + len(pp)]])
        for o_ref, o in zip(refs[n_row + len(pp):], outs):
            o_ref[...] = o

    widths = [384, 512, 1024, 1024, 1024, GPAD, GPAD]
    return pl.pallas_call(
        body, name="prep_fwd", grid=(T // tr,),
        out_shape=tuple(jax.ShapeDtypeStruct((T, w), F32) for w in widths),
        in_specs=_prep_row_specs(tr) + [_whole(p) for p in pp],
        out_specs=tuple(pl.BlockSpec((tr, w), lambda i: (i, 0)) for w in widths),
        compiler_params=_params(("parallel",)),
    )(Pr, Pr, Pr, Pr, u, u, cosE, sinE, *pp)


def _prep_bwd_call(Pr, u, cosE, sinE, pp, cts, tr, comm=None):
    T = Pr.shape[0]
    n_row, n_p = 8, len(pp)
    names = ["gla_dq_f", "gla_dq_b", "gla_dg_f", "gla_dg_b", "gla_dk_f", "gla_dk_b", "gla_dv_f", "gla_dv_b",
             "ret_dq_f", "ret_dq_b", "ret_dk_f", "ret_dk_b", "ret_dv_f", "ret_dv_b",
             "ssd_dq_f", "ssd_dq_b", "ssd_dk_f", "ssd_dk_b", "ssd_dg_f", "ssd_dg_b", "ssd_dv_f", "ssd_dv_b",
             "d_r", "d_z", "d_gr", "d_xs"]
    ct_arrays = [cts[n] for n in names]

    def body(*refs):
        ins = [r[...] for r in refs[:n_row + n_p]]
        c = {n: r[...] for n, r in zip(names, refs[n_row + n_p:n_row + n_p + len(names)])}
        dPr_ref, du_ref = refs[n_row + n_p + len(names):n_row + n_p + len(names) + 2]
        dp_refs = refs[n_row + n_p + len(names) + 2:]
        _, vjp = jax.vjp(_prep_tile, *ins)
        ct_out = (jnp.concatenate([c["gla_dq_f"] + c["gla_dq_b"], c["gla_dg_f"], c["gla_dg_b"]], axis=1),
                  jnp.concatenate([c["ret_dq_f"] + c["ret_dq_b"], c["ret_dk_f"] + c["ret_dk_b"]], axis=1),
                  c["ssd_dq_f"] + c["ssd_dq_b"], c["ssd_dk_f"], c["ssd_dk_b"], c["ssd_dg_f"], c["ssd_dg_b"])
        d = vjp(ct_out)
        d_misc, d_gq, d_rq, d_rk, d_bm, d_cm = d[:6]
        dPr_ref[...] = jnp.concatenate(
            [c["gla_dv_f"] + c["gla_dv_b"], c["d_r"], d_rq, d_rk, c["ret_dv_f"] + c["ret_dv_b"], c["d_gr"],
             c["d_z"], d_gq, c["gla_dk_f"] + c["gla_dk_b"], d_misc,
             jnp.zeros((d_misc.shape[0], R_W - MISC - 128), F32)], axis=1).astype(dPr_ref.dtype)
        du_ref[...] = jnp.concatenate([c["ssd_dv_f"] + c["ssd_dv_b"] + c["d_xs"], d_bm, d_cm], axis=1)

        @pl.when(pl.program_id(0) == 0)
        def _():
            for r in dp_refs:
                r[...] = jnp.zeros_like(r)

        for r, g in zip(dp_refs, d[n_row:]):
            r[...] += g

    row = lambda a: pl.BlockSpec((tr, a.shape[1]), lambda i: (i, 0))
    return _pcall(
        body, name="prep_bwd", grid=(T // tr,),
        out_shape=(jax.ShapeDtypeStruct((T, R_W), BF16), jax.ShapeDtypeStruct((T, 1024), F32))
        + tuple(jax.ShapeDtypeStruct(p.shape, F32) for p in pp),
        in_specs=_prep_row_specs(tr) + [_whole(p) for p in pp] + [row(a) for a in ct_arrays],
        out_specs=(pl.BlockSpec((tr, R_W), lambda i: (i, 0)), pl.BlockSpec((tr, 1024), lambda i: (i, 0)))
        + tuple(_whole(p) for p in pp),
        scratch_shapes=[], sem=("arbitrary",), args=(Pr, Pr, Pr, Pr, u, u, cosE, sinE, *pp, *ct_arrays), comm=comm)


def _post_tile(ogf, ogb, r, ysf, ysb, xs, z, orf, orb, gr, gla_n, dexp, ssd_n, ret_n):
    bd = ((_iota((256, 256), 0) >> 6) == (_iota((256, 256), 1) >> 6)).astype(F32)
    og = ogf + ogb
    gla = og * lax.rsqrt(_sel(og * og, bd) * (1.0 / 64) + RMS_EPS) * gla_n * _silu(r)
    t = (ysf + ysb + dexp * xs) * _silu(z)
    ssd = t * lax.rsqrt(jnp.mean(t * t, axis=-1, keepdims=True) + RMS_EPS) * ssd_n
    o = orf + orb
    oc = o - _sel(o, bd) * (1.0 / 64)
    ret = oc * lax.rsqrt(_sel(oc * oc, bd) * (1.0 / 64) + RMS_EPS) * ret_n * _silu(gr)
    return jnp.concatenate([gla, ssd, ret], axis=1)


def _post_row_specs(tr):
    blk = lambda w, j: pl.BlockSpec((tr, w), lambda i: (i, j))
    return [blk(256, 0), blk(256, 0), blk(256, _ROFF["gla_r"] // 256), blk(512, 0), blk(512, 0), blk(512, 0),
            blk(512, _ROFF["ssd_z"] // 512), blk(256, 0), blk(256, 0), blk(256, _ROFF["ret_g"] // 256)]


def _post_fwd_call(rows, qp, tr, comm=None):
    T = rows[0].shape[0]

    def body(*refs):
        refs[-1][...] = _post_tile(*[r[...] for r in refs[:-1]]).astype(refs[-1].dtype)

    res, got = _pcall(body, name="post_fwd", grid=(T // tr,), out_shape=[jax.ShapeDtypeStruct((T, D), BF16)],
                      in_specs=_post_row_specs(tr) + [_whole(p) for p in qp],
                      out_specs=[pl.BlockSpec((tr, D), lambda i: (i, 0))], scratch_shapes=[],
                      sem=("parallel",), args=(*rows, *qp), comm=comm)
    return res[0], got


def _post_bwd_call(rows, qp, dmixed, tr):
    T = rows[0].shape[0]
    n_in = 10 + len(qp)

    def body(*refs):
        ins = [r[...] for r in refs[:n_in]]
        _, vjp = jax.vjp(_post_tile, *ins)
        d = vjp(refs[n_in][...])
        outs = refs[n_in + 1:]
        for o_ref, g in zip(outs[:7], (d[0], d[3], d[7], d[2], d[6], d[9], d[5])):
            o_ref[...] = g.astype(o_ref.dtype)

        @pl.when(pl.program_id(0) == 0)
        def _():
            for r in outs[7:]:
                r[...] = jnp.zeros_like(r)

        for r, g in zip(outs[7:], d[10:]):
            r[...] += g

    widths = [256, 512, 256, 256, 512, 256, 512]
    dts = [BF16] * 3 + [F32] * 4
    return pl.pallas_call(
        body, name="post_bwd", grid=(T // tr,),
        out_shape=tuple(jax.ShapeDtypeStruct((T, w), dt) for w, dt in zip(widths, dts))
        + tuple(jax.ShapeDtypeStruct(p.shape, F32) for p in qp),
        in_specs=_post_row_specs(tr) + [_whole(p) for p in qp] + [pl.BlockSpec((tr, D), lambda i: (i, 0))],
        out_specs=tuple(pl.BlockSpec((tr, w), lambda i: (i, 0)) for w in widths) + tuple(_whole(p) for p in qp),
        compiler_params=_params(("arbitrary",)),
    )(*rows, *qp, dmixed)


def _mixer_scan_operands(Pr, u, a_gla, a_ret, cmr, kf, kb, g8f, g8b, lg):
    gk, gv = (Pr, _ROFF["gla_k"] // 128), (Pr, _ROFF["gla_v"] // 256)
    rv = (Pr, _ROFF["ret_v"] // 256)
    return {
        "gla": {False: ((a_gla, 0), gk, gv, (a_gla, 1)), True: ((a_gla, 0), gk, gv, (a_gla, 2))},
        "ret": {False: ((a_ret, 0), (a_ret, 1), rv, (lg, 0)), True: ((a_ret, 0), (a_ret, 1), rv, (lg, 0))},
        "ssd": {False: ((cmr, 0), (kf, 0), (u, 0), (g8f, 0)), True: ((cmr, 0), (kb, 0), (u, 0), (g8b, 0))},
    }


def _post_rows(o, Pr, u):
    return [o["gla"][False][0], o["gla"][True][0], Pr, o["ssd"][False][0], o["ssd"][True][0], u, Pr,
            o["ret"][False][0], o["ret"][True][0], Pr]


def _mixer_forward(Tc, Pr, Px, cn, pp, cw8, cb, qp, comm):
    cosE, sinE, lg = cn
    u = _conv_fwd_call(Px, cw8, cb, Tc)
    prep = _prep_fwd_call(Pr, u, cosE, sinE, pp, Tc)
    ops = _mixer_scan_operands(Pr, u, *prep, lg)
    o, got = {}, {}
    for kind in ops:
        o[kind], got[kind] = _scan_fwd_call(kind, ops[kind], Tc, comm.get(kind))
    mixed, got["post"] = _post_fwd_call(_post_rows(o, Pr, u), qp, Tc, comm.get("post"))
    return mixed, (u, prep, o), got


def _mixer_backward(Tc, Pr, Px, cn, pp, cw8, cb, qp, saved, dmixed, comm):
    cosE, sinE, lg = cn
    u, prep, o = saved
    post = _post_bwd_call(_post_rows(o, Pr, u), qp, dmixed, Tc)
    d_o = dict(gla=post[0], ssd=post[1], ret=post[2])
    cts = dict(d_r=post[3], d_z=post[4], d_gr=post[5], d_xs=post[6])
    ops = _mixer_scan_operands(Pr, u, *prep, lg)
    got = {}
    for kind in ops:
        st = {rev: o[kind][rev][1] for rev in (False, True)}
        res, got[kind] = _scan_bwd_call(kind, kind != "ret", ops[kind], st, d_o[kind], Tc, comm.get(kind))
        for rev, sfx in ((False, "_f"), (True, "_b")):
            for nm, a in zip(("_dq", "_dk", "_dv", "_dg"), res[rev]):
                cts[kind + nm + sfx] = a
    pb, got["prep"] = _prep_bwd_call(Pr, u, cosE, sinE, pp, cts, Tc, comm.get("prep"))
    dPx, dcw8, dcb = _conv_bwd_call(Px, cw8, cb, pb[1], Tc)
    return pb[0], dPx, tuple(pb[2:]), dcw8, dcb[0:1], tuple(post[7:]), got


def _mix_fwd(Tc, X, w, cn, nw, mods, pp, cw8, cb, qp, comm):
    h = _norm_fwd_call(X, nw[0], mods[0], mods[1], None, Tc, BF16)
    Px, Pr = _mm(h, w[0], trans_b=True, name="mm_fwd"), _mm(h, w[1], trans_b=True, name="mm_fwd")
    mixed, saved, got = _mixer_forward(Tc, Pr, Px, cn, pp, cw8, cb, qp, comm)
    M = _mm(mixed, w[2], name="mm_fwd")
    Xn = _norm_fwd_call(M, nw[1], mods[2], jnp.zeros_like(mods[2]), X, Tc)
    return Xn, (X, nw, mods, w, cn, pp, cw8, cb, qp, h, Px, Pr, mixed, saved, M), got


def _mix_bwd(Tc, res, dXn, comm):
    X, nw, mods, w, cn, pp, cw8, cb, qp, h, Px, Pr, mixed, saved, M = res
    dM, dnw1, da_post, _ = _norm_bwd_call(M, nw[1], mods[2], dXn, Tc, out_dtype=BF16)
    dmixed = _mm(dM, w[2], trans_b=True, name="mm_dx")
    dPr, dPx, dpp, dcw8, dcb, dqp, got = _mixer_backward(Tc, Pr, Px, cn, pp, cw8, cb, qp, saved, dmixed, comm)
    dh = _mm(dPx, w[0], name="mm_dx")
    dh = _mm(dPr, w[1], name="mm_dx_acc", add=dh)
    dX, dnw0, da_pre, db_pre = _norm_bwd_call(X, nw[0], mods[0], dh, Tc, add=dXn)
    dW = tuple(_mm_tn(a, g, name="mm_dw", out_dtype=BF16) for a, g in ((dPx, h), (dPr, h), (mixed, dM)))
    return dX, ((dnw0[0], dnw1[0]), (da_pre, db_pre, da_post), dpp, dcw8, dcb, dqp), dW, got


def _ffn_fwd(Tc, X, w, nw, mods):
    h = _norm_fwd_call(X, nw[0], mods[0], mods[1], None, Tc, BF16)
    U1 = _mm(h, w[0], trans_b=True, name="mm_fwd", out_dtype=BF16)
    U2 = _mm(h, w[1], trans_b=True, name="mm_fwd", out_dtype=BF16)
    act = _act_call(U1, U2)
    Fo = _mm(act, w[2], name="mm_fwd")
    Xn = _norm_fwd_call(Fo, nw[1], mods[2], jnp.zeros_like(mods[2]), X, Tc)
    return Xn, (X, nw, mods, w, h, U1, U2, act, Fo)


def _ffn_bwd(Tc, res, dXn):
    X, nw, mods, w, h, U1, U2, act, Fo = res
    dFo, dnw1, da_post, _ = _norm_bwd_call(Fo, nw[1], mods[2], dXn, Tc, out_dtype=BF16)
    dU1, dU2 = _act_call(U1, U2, _mm(dFo, w[2], trans_b=True, name="mm_dx", out_dtype=BF16))
    dh = _mm(dU1, w[0], name="mm_dx")
    dh = _mm(dU2, w[1], name="mm_dx_acc", add=dh)
    dX, dnw0, da_pre, db_pre = _norm_bwd_call(X, nw[0], mods[0], dh, Tc, add=dXn)
    dW = tuple(_mm_tn(a, g, name="mm_dw", out_dtype=BF16) for a, g in ((dU1, h), (dU2, h), (act, dFo)))
    return dX, ((dnw0[0], dnw1[0]), (da_pre, db_pre, da_post)), dW


def _rope_tables(Tl, Tc):
    rows = Tl // GRID_W
    row = jnp.repeat(jnp.arange(rows), GRID_W).astype(F32)
    col = jnp.tile(jnp.arange(GRID_W), rows).astype(F32)
    inv_freq = 10000.0 ** (-jnp.arange(16, dtype=F32) / 16)
    ang = jnp.concatenate([row[:, None] * inv_freq, col[:, None] * inv_freq], axis=-1)
    cos = jnp.concatenate([jnp.ones((Tc, 32), F32), jnp.cos(ang)], axis=0)
    sin = jnp.concatenate([jnp.zeros((Tc, 32), F32), jnp.sin(ang)], axis=0)
    return jnp.tile(cos, (1, 8)), jnp.tile(sin, (1, 8))


def _rows8(first, second):
    z = jnp.zeros((6,) + first.shape, F32)
    return jnp.concatenate([first[None], second[None], z], axis=0)


def _layer_inputs(sp, ml, mc):
    gu = sp["gla_gate_up"]
    Wg = jnp.zeros((128, 256), F32).at[0:16, 0:128].set(gu[0]).at[16:32, 128:256].set(gu[1])
    pp = (Wg, sp["gla_gate_b"].reshape(1, 256), sp["ssd_dt_bias"][0:1], sp["ssd_dt_bias"][1:2],
          -jnp.exp(sp["ssd_a_log"][0:1]), -jnp.exp(sp["ssd_a_log"][1:2]))
    qp = (sp["gla_norm"].reshape(1, 256), jnp.repeat(sp["ssd_d"], 64).reshape(1, 512),
          sp["ssd_norm"].reshape(1, 512), sp["ret_norm"].reshape(1, 256))
    mix = ((sp["norm_mix_pre"], sp["norm_mix_post"]),
           (_rows8(1.0 + mc[1], 1.0 + ml[1]), _rows8(mc[0], ml[0]), _rows8(mc[2], ml[2])),
           pp, jnp.pad(sp["ssd_conv_w"], ((0, 3), (0, 0))), sp["ssd_conv_b"].reshape(1, 1024), qp)
    ffn = ((sp["norm_ffn_pre"], sp["norm_ffn_post"]),
           (_rows8(1.0 + mc[4], 1.0 + ml[4]), _rows8(mc[3], ml[3]), _rows8(mc[5], ml[5])))
    return mix, ffn


def _rows_from(g):
    return g.reshape(N_DEV * g.shape[1], g.shape[2])


def _rows_to(f):
    return f.reshape(N_DEV, f.shape[0] // N_DEV, f.shape[1])


def _local_step(xcat, target, mod_l, mod_c, sp, Tc, weights=None, shards=None):
    Tt = xcat.shape[0]
    cosE, sinE = _rope_tables(Tt - Tc, Tc)
    log_gamma = jnp.log1p(-jnp.exp2(-5.0 - jnp.arange(4, dtype=F32)))
    lg = jnp.broadcast_to(jnp.concatenate([log_gamma, jnp.zeros((GPAD - 4,), F32)])[None, :], (Tt, GPAD))
    cn = (cosE, sinE, lg)
    dist = shards is not None
    X, saved = xcat, []
    if dist:
        g_in, g_out = _exchange_call(False, shards[0][:2], "gather_mix0")
    for l in range(DEPTH):
        (a_mix, a_ffn), pull = jax.vjp(_layer_inputs, {n: sp[n][l] for n in _SMALL},
                                       mod_l[l].reshape(6, D), mod_c[l].reshape(6, D))
        comm = {}
        if dist:
            w_in, w_out = _rows_from(g_in), _rows_from(g_out)
            comm = dict(ssd=(False, [shards[l][2]]), ret=(False, [shards[l][3]]))
            if l + 1 < DEPTH:
                comm.update(gla=(False, [shards[l + 1][0]]), post=(False, [shards[l + 1][1]]))
        else:
            w_in, w_out, w13, w2 = weights[l]
        w_x, w_r = _split_w_in(w_in)
        X, r_mix, got = _mix_fwd(Tc, X, (w_x, w_r, w_out), cn, *a_mix, comm)
        if dist:
            w13, w2 = _rows_from(got["ssd"][0]), _rows_from(got["ret"][0])
            if l + 1 < DEPTH:
                g_in, g_out = got["gla"][0], got["post"][0]
        X, r_ffn = _ffn_fwd(Tc, X, (w13[:FFN_H], w13[FFN_H:], w2), *a_ffn)
        saved.append((r_mix, r_ffn, pull))
    loss, dX = _loss_call(X, target, Tc)
    d_sp, d_ml, d_mc = [None] * DEPTH, [None] * DEPTH, [None] * DEPTH
    gw = [[None] * 4 for _ in range(DEPTH)]
    nxt = None
    for l in reversed(range(DEPTH)):
        r_mix, r_ffn, pull = saved[l]
        dX, c_ffn, dW_ffn = _ffn_bwd(Tc, r_ffn, dX)
        g13, g2 = jnp.concatenate([dW_ffn[0], dW_ffn[1]], axis=0), dW_ffn[2]
        comm = {}
        if dist:
            comm = dict(ssd=(True, [_rows_to(g13)]), ret=(True, [_rows_to(g2)]))
            if nxt is not None:
                comm.update(gla=(True, [nxt[0]]), prep=(True, [nxt[1]]))
        dX, c_mix, dW_mix, got = _mix_bwd(Tc, r_mix, dX, comm)
        d_sp[l], d_ml[l], d_mc[l] = pull((c_mix, c_ffn))
        gin, gout = _merge_w_in(dW_mix[0], dW_mix[1]), dW_mix[2]
        if dist:
            gw[l][2], gw[l][3] = got["ssd"][0], got["ret"][0]
            if nxt is not None:
                gw[l + 1][0], gw[l + 1][1] = got["gla"][0], got["prep"][0]
            nxt = (_rows_to(gin), _rows_to(gout))
        else:
            gw[l] = [gin, gout, g13, g2]
    if dist:
        gw[0][0], gw[0][1] = _exchange_call(True, list(nxt), "scatter_mix0")
    d_sp = {n: jnp.stack([d_sp[l][n] for l in range(DEPTH)]) for n in _SMALL}
    return (loss, dX, jnp.stack(d_ml).reshape(DEPTH, 6 * D), jnp.stack(d_mc).reshape(DEPTH, 6 * D), d_sp, gw)


def _sum8_call(slabs, name):
    _, R, Cc = slabs.shape
    tr = _pick(R, (512, 352, 256, 128, 64, 32, 16))

    def body(*refs):
        acc = refs[0][...].astype(F32)
        for r in refs[1:N_DEV]:
            acc = acc + r[...].astype(F32)
        refs[N_DEV][...] = acc

    return pl.pallas_call(
        body, name=name, grid=(R // tr,), out_shape=jax.ShapeDtypeStruct((R, Cc), F32),
        in_specs=[pl.BlockSpec((None, tr, Cc), lambda i, d=d: (d, i, 0)) for d in range(N_DEV)],
        out_specs=pl.BlockSpec((tr, Cc), lambda i: (i, 0)), compiler_params=_params(("parallel",)),
    )(*([slabs] * N_DEV))


def _loss_call(X, target, Tc):
    Tt, W = X.shape
    tr = Tc
    nt = Tt // tr

    def body(x_ref, t_ref, loss_ref, dx_ref, acc_ref):
        i = pl.program_id(0)

        @pl.when(i == 0)
        def _():
            acc_ref[...] = jnp.zeros_like(acc_ref)
            dx_ref[...] = jnp.zeros_like(dx_ref)

        @pl.when(i > 0)
        def _():
            e = x_ref[...] - t_ref[...]
            dx_ref[...] = e * (1.0 / W)
            acc_ref[...] += jnp.sum(e * e, axis=0, keepdims=True)

        @pl.when(i == nt - 1)
        def _():
            loss_ref[...] = jnp.full(loss_ref.shape, (0.5 / W) * jnp.sum(acc_ref[...]), F32)

    loss, dx = pl.pallas_call(
        body, name="loss",
        out_shape=(jax.ShapeDtypeStruct((8, 128), F32), jax.ShapeDtypeStruct((Tt, W), F32)),
        grid=(nt,),
        in_specs=[pl.BlockSpec((tr, W), lambda i: (i, 0)),
                  pl.BlockSpec((tr, W), lambda i: (jnp.maximum(i - 1, 0), 0))],
        out_specs=(pl.BlockSpec((8, 128), lambda i: (0, 0)), pl.BlockSpec((tr, W), lambda i: (i, 0))),
        scratch_shapes=[pltpu.VMEM((1, W), F32)],
        compiler_params=_params(("arbitrary",)),
    )(X, target)
    return loss[0, 0], dx


def _adamw_call(w, g, m, v, name):
    R, Cc = w.shape
    tr = _pick(R, (512, 352, 256, 128, 64, 32, 16, 8))
    c1 = 1.0 - ADAM_B1 ** ADAM_STEP
    c2 = 1.0 - ADAM_B2 ** ADAM_STEP

    def body(w_ref, g_ref, m_ref, v_ref, d_ref, nm_ref, nv_ref):
        gv = g_ref[...]
        nm = ADAM_B1 * m_ref[...] + (1.0 - ADAM_B1) * gv
        nv = ADAM_B2 * v_ref[...] + (1.0 - ADAM_B2) * (gv * gv)
        d_ref[...] = -ADAM_LR * ((nm / c1) / (jnp.sqrt(nv / c2) + ADAM_EPS) + ADAM_WD * w_ref[...])
        nm_ref[...] = nm
        nv_ref[...] = nv

    spec = pl.BlockSpec((tr, Cc), lambda i: (i, 0))
    sh = jax.ShapeDtypeStruct((R, Cc), F32)
    return pl.pallas_call(
        body, name=name, out_shape=(sh, sh, sh), grid=(R // tr,),
        in_specs=[spec] * 4, out_specs=(spec,) * 3, compiler_params=_params(("parallel",)),
    )(w, g, m, v)


def _sum_call(xs, name, also_bf16=False):
    R, Cc = xs[0].shape
    tr = _pick(R, (512, 352, 256, 128, 64, 32, 16))
    k = len(xs)

    def body(*refs):
        acc = refs[0][...].astype(F32)
        for r in refs[1:k]:
            acc = acc + r[...].astype(F32)
        refs[k][...] = acc
        if also_bf16:
            refs[k + 1][...] = acc.astype(BF16)

    spec = pl.BlockSpec((tr, Cc), lambda i: (i, 0))
    sh = jax.ShapeDtypeStruct((R, Cc), F32)
    return pl.pallas_call(
        body, name=name, grid=(R // tr,), in_specs=[spec] * k,
        out_shape=(sh, jax.ShapeDtypeStruct((R, Cc), BF16)) if also_bf16 else sh,
        out_specs=(spec, spec) if also_bf16 else spec, compiler_params=_params(("parallel",)),
    )(*xs)


MESH = pl.DeviceIdType.MESH
ANY = pl.BlockSpec(memory_space=pl.ANY)


def _me():
    return lax.axis_index("x"), lax.axis_index("y"), lax.axis_index("c")


_FLIPS = [(0, 0, 1), (1, 0, 0), (0, 1, 0), (1, 1, 0), (1, 0, 1), (0, 1, 1), (1, 1, 1)]


def _exchange_copies(scatter, srcs, dsts, send_sems, recv_sems, loc_sems, arrivals):
    x, y, c = _me()
    me = 4 * x + 2 * y + c
    sends, recvs, local = [], [], []
    for a in range(len(srcs)):
        for k, (dx, dy, dc) in enumerate(_FLIPS):
            px, py, pc = (1 - x if dx else x), (1 - y if dy else y), (1 - c if dc else c)
            peer = 4 * px + 2 * py + pc
            src = srcs[a].at[peer] if scatter else srcs[a]
            for lst, slab in ((sends, me), (recvs, peer)) if arrivals else ((sends, me),):
                lst.append(pltpu.make_async_remote_copy(
                    src_ref=src, dst_ref=dsts[a].at[slab], send_sem=send_sems.at[a, k], recv_sem=recv_sems.at[a, k],
                    device_id=(px, py, pc), device_id_type=MESH))
        local.append(pltpu.make_async_copy(srcs[a].at[me] if scatter else srcs[a], dsts[a].at[me], loc_sems.at[a]))
    return sends, recvs, local


def _exchange_start(scatter, srcs, dsts, sems):
    sends, _, local = _exchange_copies(scatter, srcs, dsts, *sems, arrivals=False)
    for cp in local + sends:
        cp.start()


def _exchange_wait(scatter, srcs, dsts, sems):
    sends, recvs, local = _exchange_copies(scatter, srcs, dsts, *sems, arrivals=True)
    for cp in sends:
        cp.wait_send()
    for cp in recvs:
        cp.wait_recv()
    for cp in local:
        cp.wait()


def _exchange_shapes(scatter, srcs):
    return tuple(jax.ShapeDtypeStruct(((N_DEV,) + s.shape[-2:]), s.dtype) for s in srcs)


def _exchange_sems(n):
    return [pltpu.SemaphoreType.DMA((n, 7)), pltpu.SemaphoreType.DMA((n, 7)), pltpu.SemaphoreType.DMA((n,))]


def _exchange_call(scatter, srcs, name):
    n = len(srcs)

    def body(*refs):
        _exchange_start(scatter, refs[:n], refs[n:2 * n], refs[2 * n:])
        _exchange_wait(scatter, refs[:n], refs[n:2 * n], refs[2 * n:])

    return pl.pallas_call(body, name=name, out_shape=_exchange_shapes(scatter, srcs), in_specs=[ANY] * n,
                          out_specs=(ANY,) * n, scratch_shapes=_exchange_sems(n))(*srcs)


def _pcall(body, *, name, grid, in_specs, out_specs, out_shape, scratch_shapes, sem, args, comm=None):
    if comm is None:
        res = pl.pallas_call(body, name=name, grid=grid, in_specs=list(in_specs), out_specs=tuple(out_specs),
                             out_shape=tuple(out_shape), scratch_shapes=list(scratch_shapes),
                             compiler_params=_params(sem))(*args)
        return tuple(res), ()
    scatter, srcs = comm
    n_in, n_out, n_c, n_s = len(in_specs), len(out_specs), len(srcs), len(scratch_shapes)

    def carrier(*refs):
        ins, c_src = refs[:n_in], refs[n_in:n_in + n_c]
        outs = refs[n_in + n_c:n_in + n_c + n_out]
        c_dst = refs[n_in + n_c + n_out:n_in + 2 * n_c + n_out]
        scr = refs[n_in + 2 * n_c + n_out:n_in + 2 * n_c + n_out + n_s]
        first = pl.program_id(0) == 0
        last = pl.program_id(0) == grid[0] - 1
        for ax in range(1, len(grid)):
            first = jnp.logical_and(first, pl.program_id(ax) == 0)
            last = jnp.logical_and(last, pl.program_id(ax) == grid[ax] - 1)

        @pl.when(first)
        def _():
            _exchange_start(scatter, c_src, c_dst, refs[-3:])

        body(*ins, *outs, *scr)

        @pl.when(last)
        def _():
            _exchange_wait(scatter, c_src, c_dst, refs[-3:])

    res = pl.pallas_call(
        carrier, name=name + "_x", grid=grid, in_specs=list(in_specs) + [ANY] * n_c,
        out_specs=tuple(out_specs) + (ANY,) * n_c, out_shape=tuple(out_shape) + _exchange_shapes(scatter, srcs),
        scratch_shapes=list(scratch_shapes) + _exchange_sems(n_c),
        compiler_params=_params(("arbitrary",) * len(grid)))(*args, *srcs)
    return tuple(res[:n_out]), tuple(res[n_out:])


def _two_level_gather_body(n_arr, x_refs, out_refs, send_sems, recv_sems, local_sems):
    x, y, c = _me()
    me, sibling = (x, y, c), (x, y, 1 - c)
    chips = [(1 - x, y), (x, 1 - y), (1 - x, 1 - y)]

    def slab(a, px, py, pc):
        return out_refs[a].at[4 * px + 2 * py + pc]

    def copy(a, k, block, to, src=None):
        return pltpu.make_async_remote_copy(
            src_ref=slab(a, *block) if src is None else src, dst_ref=slab(a, *block),
            send_sem=send_sems.at[a, k], recv_sem=recv_sems.at[a, k], device_id=to, device_id_type=MESH)

    mine = [pltpu.make_async_copy(x_refs[a], slab(a, *me), local_sems.at[a]) for a in range(n_arr)]
    for cp in mine:
        cp.start()
    first = []
    for a in range(n_arr):
        first.append(copy(a, 0, me, sibling, src=x_refs[a]))
        first += [copy(a, 1 + j, me, (*chip, c), src=x_refs[a]) for j, chip in enumerate(chips)]
    for cp in first:
        cp.start()
    passed = []
    for j, chip in enumerate(chips):
        for a in range(n_arr):
            copy(a, 1 + j, (*chip, c), me).wait_recv()
            fw = copy(a, 4 + j, (*chip, c), sibling)
            fw.start()
            passed.append(fw)
    for a in range(n_arr):
        copy(a, 0, sibling, me).wait_recv()
        for j, chip in enumerate(chips):
            copy(a, 4 + j, (*chip, 1 - c), me).wait_recv()
    for cp in first + passed:
        cp.wait_send()
    for cp in mine:
        cp.wait()


def _gather_small(x, name):
    def body(x_ref, out_ref, send_sems, recv_sems, local_sems):
        _two_level_gather_body(1, [x_ref], [out_ref], send_sems, recv_sems, local_sems)

    vm = pl.BlockSpec(memory_space=pltpu.VMEM)
    return pl.pallas_call(
        body, name=name,
        out_shape=jax.ShapeDtypeStruct((N_DEV,) + x.shape, x.dtype),
        in_specs=[vm], out_specs=vm,
        scratch_shapes=[pltpu.SemaphoreType.DMA((1, 7)), pltpu.SemaphoreType.DMA((1, 7)),
                        pltpu.SemaphoreType.DMA((1,))],
    )(x)


_SMALL = ["norm_mix_pre", "norm_mix_post", "norm_ffn_pre", "norm_ffn_post", "gla_gate_up", "gla_gate_b",
          "gla_norm", "ssd_conv_w", "ssd_conv_b", "ssd_dt_bias", "ssd_a_log", "ssd_d", "ssd_norm", "ret_norm"]


def _pack(arrs):
    flat = jnp.concatenate([a.reshape(-1) for a in arrs])
    n = flat.shape[0]
    npad = -(-n // 1024) * 1024
    return jnp.pad(flat, (0, npad - n)).reshape(npad // 128, 128)


def _unpack(buf, shapes):
    flat = buf.reshape(-1)
    out, o = [], 0
    for s in shapes:
        n = math.prod(s)
        out.append(flat[o:o + n].reshape(s))
        o += n
    return out


def kernel(x, c, ctx, c_ctx, ada_w, ada_b, norm_mix_pre, norm_mix_post, norm_ffn_pre, norm_ffn_post, w_in, w_out, gla_gate_up, gla_gate_b, gla_norm, ssd_conv_w, ssd_conv_b, ssd_dt_bias, ssd_a_log, ssd_d, ssd_norm, ret_norm, ffn_w13, ffn_w2, loss_target, m_c_ctx, m_ada_w, m_ada_b, m_norm_mix_pre, m_norm_mix_post, m_norm_ffn_pre, m_norm_ffn_post, m_w_in, m_w_out, m_gla_gate_up, m_gla_gate_b, m_gla_norm, m_ssd_conv_w, m_ssd_conv_b, m_ssd_dt_bias, m_ssd_a_log, m_ssd_d, m_ssd_norm, m_ret_norm, m_ffn_w13, m_ffn_w2, v_c_ctx, v_ada_w, v_ada_b, v_norm_mix_pre, v_norm_mix_post, v_norm_ffn_pre, v_norm_ffn_post, v_w_in, v_w_out, v_gla_gate_up, v_gla_gate_b, v_gla_norm, v_ssd_conv_w, v_ssd_conv_b, v_ssd_dt_bias, v_ssd_a_log, v_ssd_d, v_ssd_norm, v_ret_norm, v_ffn_w13, v_ffn_w2):
    P_ = dict(c_ctx=c_ctx, ada_w=ada_w, ada_b=ada_b, norm_mix_pre=norm_mix_pre, norm_mix_post=norm_mix_post,
              norm_ffn_pre=norm_ffn_pre, norm_ffn_post=norm_ffn_post, w_in=w_in, w_out=w_out,
              gla_gate_up=gla_gate_up, gla_gate_b=gla_gate_b, gla_norm=gla_norm, ssd_conv_w=ssd_conv_w,
              ssd_conv_b=ssd_conv_b, ssd_dt_bias=ssd_dt_bias, ssd_a_log=ssd_a_log, ssd_d=ssd_d,
              ssd_norm=ssd_norm, ret_norm=ret_norm, ffn_w13=ffn_w13, ffn_w2=ffn_w2)
    M_ = dict(c_ctx=m_c_ctx, ada_w=m_ada_w, ada_b=m_ada_b, norm_mix_pre=m_norm_mix_pre,
              norm_mix_post=m_norm_mix_post, norm_ffn_pre=m_norm_ffn_pre, norm_ffn_post=m_norm_ffn_post,
              w_in=m_w_in, w_out=m_w_out, gla_gate_up=m_gla_gate_up, gla_gate_b=m_gla_gate_b,
              gla_norm=m_gla_norm, ssd_conv_w=m_ssd_conv_w, ssd_conv_b=m_ssd_conv_b, ssd_dt_bias=m_ssd_dt_bias,
              ssd_a_log=m_ssd_a_log, ssd_d=m_ssd_d, ssd_norm=m_ssd_norm, ret_norm=m_ret_norm,
              ffn_w13=m_ffn_w13, ffn_w2=m_ffn_w2)
    V_ = dict(c_ctx=v_c_ctx, ada_w=v_ada_w, ada_b=v_ada_b, norm_mix_pre=v_norm_mix_pre,
              norm_mix_post=v_norm_mix_post, norm_ffn_pre=v_norm_ffn_pre, norm_ffn_post=v_norm_ffn_post,
              w_in=v_w_in, w_out=v_w_out, gla_gate_up=v_gla_gate_up, gla_gate_b=v_gla_gate_b,
              gla_norm=v_gla_norm, ssd_conv_w=v_ssd_conv_w, ssd_conv_b=v_ssd_conv_b, ssd_dt_bias=v_ssd_dt_bias,
              ssd_a_log=v_ssd_a_log, ssd_d=v_ssd_d, ssd_norm=v_ssd_norm, ret_norm=v_ret_norm,
              ffn_w13=v_ffn_w13, ffn_w2=v_ffn_w2)
    order = ["c_ctx", "ada_w", "ada_b", "norm_mix_pre", "norm_mix_post", "norm_ffn_pre", "norm_ffn_post", "w_in",
             "w_out", "gla_gate_up", "gla_gate_b", "gla_norm", "ssd_conv_w", "ssd_conv_b", "ssd_dt_bias",
             "ssd_a_log", "ssd_d", "ssd_norm", "ret_norm", "ffn_w13", "ffn_w2"]

    mx, my, mc_ = _me()
    me = 4 * mx + 2 * my + mc_
    Tl, Tc = x.shape[1], ctx.shape[1]
    n_in, n_out, n_13, n_2 = w_in.shape[2], w_out.shape[1], ffn_w13.shape[2], ffn_w2.shape[1]
    n_ada = ada_w.shape[2]

    shards = [[w_in[l].T.astype(BF16), w_out[l].astype(BF16), ffn_w13[l].T.astype(BF16), ffn_w2[l].astype(BF16)]
              for l in range(DEPTH)]

    cw = ssd_conv_w.shape[2]
    small_in = jnp.concatenate([jnp.pad(c, ((0, 7), (0, 0))).reshape(-1),
                                ssd_conv_w.reshape(-1)]).reshape(-1, 128)
    n_c_rows = 8 * D // 128
    small_in = jnp.pad(small_in, ((0, -small_in.shape[0] % 8), (0, 0)))
    gathered = _gather_small(small_in, "gather_c_conv")
    c_all = gathered[:, :n_c_rows].reshape(N_DEV, 8, D)[:, 0]
    conv_rows = DEPTH * 5 * cw // 128
    conv_full = gathered[:, n_c_rows:n_c_rows + conv_rows].reshape(N_DEV, DEPTH, 5, cw)
    conv_full = jnp.moveaxis(conv_full, 0, 2).reshape(DEPTH, 5, N_DEV * cw)
    c9 = jnp.concatenate([c_all, c_ctx[None], jnp.zeros((7, D), F32)], axis=0)
    s9 = c9 * jax.nn.sigmoid(c9)
    mod_piece = jnp.concatenate([_mm(s9, ada_w[l], name="mm_mod") for l in range(DEPTH)], axis=0)
    mod_g = _gather_small(mod_piece, "gather_mod")
    mod_all = jnp.moveaxis(mod_g.reshape(N_DEV, DEPTH, 16, n_ada), 0, 2).reshape(DEPTH, 16, N_DEV * n_ada)
    mod_all = mod_all + ada_b[:, None, :]
    mod_l = lax.dynamic_index_in_dim(mod_all, me, axis=1, keepdims=False)
    mod_c = mod_all[:, 8]

    sp = {n: P_[n] for n in _SMALL}
    sp["ssd_conv_w"] = conv_full
    xcat = jnp.concatenate([ctx[0], x[0]], axis=0)
    loss_local, d_xcat, d_mod_l, d_mod_c, d_sp, gw = _local_step(xcat, loss_target[0], mod_l, mod_c, sp, Tc,
                                                                 shards=shards)
    loss = lax.psum(loss_local, ("x", "y", "c"))
    grad_x = d_xcat[Tc:][None]

    G = {n: jnp.stack([_sum8_call(gw[l][a], f"sum_{n}") for l in range(DEPTH)])
         for a, n in enumerate(["w_in", "w_out", "ffn_w13", "ffn_w2"])}
    G["w_in"], G["ffn_w13"] = jnp.swapaxes(G["w_in"], 1, 2), jnp.swapaxes(G["ffn_w13"], 1, 2)

    dmod_rows = jnp.concatenate([d_mod_l, d_mod_c], axis=0)
    dmod_g = _gather_small(dmod_rows, "gather_dmod").reshape(N_DEV, 2, DEPTH, 6 * D)
    dl = jnp.moveaxis(dmod_g[:, 0], 0, 1)
    dc = dmod_g[:, 1, :, :]
    dc_tot = dc[0]
    for d_ in range(1, N_DEV):
        dc_tot = dc_tot + dc[d_]
    dmod9 = jnp.concatenate([dl, dc_tot[:, None, :], jnp.zeros((DEPTH, 7, 6 * D), F32)], axis=1)
    g_ada_b = dmod9[:, 0]
    for r_ in range(1, 9):
        g_ada_b = g_ada_b + dmod9[:, r_]
    dmod9_mine = lax.dynamic_slice_in_dim(dmod9, me * n_ada, n_ada, axis=2)
    s9T = jnp.pad(s9.T, ((0, 0), (0, 112)))
    g_ada_w = jnp.stack([_mm(s9T, jnp.pad(dmod9_mine[l], ((0, 112), (0, 0))), name="mm_dada")
                         for l in range(DEPTH)])
    ds9 = _mm(dmod9_mine[0], ada_w[0], trans_b=True, name="mm_ds9")
    for l in range(1, DEPTH):
        ds9 = _mm(dmod9_mine[l], ada_w[l], trans_b=True, name="mm_ds9_acc", add=ds9)
    ds_ctx_part = ds9[8]

    small_names = [n for n in _SMALL]
    small_parts = [d_sp[n] for n in small_names] + [ds_ctx_part]
    packed = _pack(small_parts)
    allp = _gather_small(packed, "gather_small_grads")
    summed = _sum_call([allp[d_] for d_ in range(N_DEV)], "sum_small_grads")
    parts = _unpack(summed, [p.shape for p in small_parts])
    for n, p in zip(small_names, parts[:-1]):
        G[n] = p
    sig = jax.nn.sigmoid(c_ctx)
    G["c_ctx"] = parts[-1] * (sig * (1.0 + c_ctx * (1.0 - sig)))
    G["ssd_conv_w"] = lax.dynamic_slice_in_dim(G["ssd_conv_w"], me * cw, cw, axis=2)
    G["ada_w"] = g_ada_w
    G["ada_b"] = g_ada_b

    delta, new_m, new_v = {}, {}, {}
    for n in ["ada_w", "w_in", "w_out", "ffn_w13", "ffn_w2"]:
        sh = P_[n].shape
        f2 = lambda a: a.reshape(sh[0] * sh[1], sh[2])
        d_, m_, v_ = _adamw_call(f2(P_[n]), f2(G[n]), f2(M_[n]), f2(V_[n]), f"adamw_{n}")
        delta[n], new_m[n], new_v[n] = d_.reshape(sh), m_.reshape(sh), v_.reshape(sh)
    rest = [n for n in order if n not in delta]
    shapes = [P_[n].shape for n in rest]
    d_, m_, v_ = _adamw_call(_pack([P_[n] for n in rest]), _pack([G[n] for n in rest]),
                             _pack([M_[n] for n in rest]), _pack([V_[n] for n in rest]), "adamw_small")
    for n, a, b, e in zip(rest, _unpack(d_, shapes), _unpack(m_, shapes), _unpack(v_, shapes)):
        delta[n], new_m[n], new_v[n] = a, b, e

    return (loss, grad_x, *[G[n] for n in order], *[delta[n] for n in order],
            *[new_m[n] for n in order], *[new_v[n] for n in order])
```

```python
import math

import jax
import jax.numpy as jnp
from jax import lax
from jax.experimental import pallas as pl
from jax.experimental.pallas import tpu as pltpu

F32 = jnp.float32
BF16 = jnp.bfloat16

D = 1024
DEPTH = 4
GRID_W = 64
RMS_EPS = 1e-6
GLA_TAU = 16.0
FFN_H = 2816
IN_COLS = 3376
N_DEV = 8
ADAM_LR, ADAM_B1, ADAM_B2, ADAM_EPS, ADAM_WD, ADAM_STEP = 0.001, 0.9, 0.999, 1e-08, 0.01, 10

VMEM_LIMIT = 48 * 1024 * 1024

_ORIG = dict(gla_q=(0, 128), gla_k=(128, 128), gla_v=(256, 256), gla_r=(512, 256), gla_lr=(768, 32),
             ssd_z=(800, 512), ssd_xbc=(1312, 1024), ssd_dt=(2336, 16), ret_q=(2352, 256), ret_k=(2608, 256),
             ret_v=(2864, 256), ret_g=(3120, 256))
_R_ORDER = ["gla_v", "gla_r", "ret_q", "ret_k", "ret_v", "ret_g", "ssd_z", "gla_q", "gla_k", "gla_lr", "ssd_dt"]
R_W = 2560
_ROFF = {}
_o = 0
for _n in _R_ORDER:
    _ROFF[_n] = _o
    _o += _ORIG[_n][1]
MISC = _ROFF["gla_lr"]
assert MISC == 2304 and _o == 2352


def _split_w_in(wt):
    xs, xz = _ORIG["ssd_xbc"]
    parts = [wt[_ORIG[n][0]:_ORIG[n][0] + _ORIG[n][1]] for n in _R_ORDER]
    parts.append(jnp.zeros((R_W - _o,) + wt.shape[1:], wt.dtype))
    return wt[xs:xs + xz], jnp.concatenate(parts, axis=0)


def _merge_w_in(wx, wr):
    pieces = []
    for n, (s, z) in sorted(_ORIG.items(), key=lambda t: t[1][0]):
        pieces.append(wx if n == "ssd_xbc" else wr[_ROFF[n]:_ROFF[n] + z])
    return jnp.concatenate(pieces, axis=0)


def _pick(n, cands):
    for c in cands:
        if n % c == 0:
            return c
    return n


def _params(sem=None):
    kw = dict(vmem_limit_bytes=VMEM_LIMIT)
    if sem is not None:
        kw["dimension_semantics"] = sem
    return pltpu.CompilerParams(**kw)


def _iota(shape, dim):
    return lax.broadcasted_iota(jnp.int32, shape, dim)


def _dot(a, b, dims):
    return lax.dot_general(a, b, (dims, ((), ())), preferred_element_type=F32)


_NN = ((1,), (0,))
_NT = ((1,), (1,))
_TN = ((0,), (0,))


def _bf(x):
    return x.astype(BF16)


def _dot_sel(x, e, dims, x_left=True):
    eb = e.astype(BF16)
    hi = x.astype(BF16)
    r1 = x - hi.astype(F32)
    mid = r1.astype(BF16)
    lo = (r1 - mid.astype(F32)).astype(BF16)
    out = None
    for p in (hi, mid, lo):
        t = _dot(p, eb, dims) if x_left else _dot(eb, p, dims)
        out = t if out is None else out + t
    return out


@jax.custom_vjp
def _sel(x, e):
    return _dot_sel(x, e, _NN)


_sel.defvjp(lambda x, e: (_dot_sel(x, e, _NN), e), lambda e, g: (_dot_sel(g, e, _NT), jnp.zeros_like(e)))


def _sig(x):
    e = jnp.exp(-jnp.abs(x))
    return jnp.where(x >= 0, 1.0 / (1.0 + e), e / (1.0 + e))


@jax.custom_vjp
def _sigmoid(x):
    return _sig(x)


def _sigmoid_fwd(x):
    s = _sig(x)
    return s, s


_sigmoid.defvjp(_sigmoid_fwd, lambda s, g: (g * s * (1.0 - s),))


def _silu(x):
    return x * _sigmoid(x)


@jax.custom_vjp
def _softplus(x):
    return jnp.maximum(x, 0.0) + jnp.log(1.0 + jnp.exp(-jnp.abs(x)))


_softplus.defvjp(lambda x: (jnp.maximum(x, 0.0) + jnp.log(1.0 + jnp.exp(-jnp.abs(x))), x),
                 lambda x, g: (g * _sig(x),))


def _log_sigmoid(x):
    return -_softplus(-x)


@jax.custom_vjp
def _mm_bf(x, w):
    return _dot(_bf(x), _bf(w), _NN)


_mm_bf.defvjp(lambda x, w: (_dot(_bf(x), _bf(w), _NN), (x, w)),
              lambda r, g: (_dot(_bf(g), _bf(r[1]), _NT), _dot(_bf(r[0]), _bf(g), _TN)))


_TILE_M = (1088, 1024, 512, 256, 128, 64, 32, 16)
_TILE_N = (1408, 1280, 1024, 768, 512, 384, 256, 128)
_TILE_K = (1408, 1280, 1024, 768, 512, 384, 256, 128)


def _mm(a, b, *, trans_b=False, name, add=None, out_dtype=F32):
    M, K = a.shape
    N = b.shape[0] if trans_b else b.shape[1]
    assert (b.shape[1] if trans_b else b.shape[0]) == K
    tm, tn, tk = _pick(M, _TILE_M), _pick(N, _TILE_N), _pick(K, _TILE_K)
    nk = K // tk
    dims = _NT if trans_b else _NN
    has_add = add is not None

    def body(*refs):
        a_ref, b_ref = refs[0], refs[1]
        o_ref, acc_ref = refs[-2], refs[-1]
        k = pl.program_id(2)

        @pl.when(k == 0)
        def _():
            acc_ref[...] = refs[2][...] if has_add else jnp.zeros_like(acc_ref)

        acc_ref[...] += _dot(a_ref[...].astype(BF16), b_ref[...].astype(BF16), dims)

        @pl.when(k == nk - 1)
        def _():
            o_ref[...] = acc_ref[...].astype(o_ref.dtype)

    b_spec = (pl.BlockSpec((tn, tk), lambda i, j, k: (j, k)) if trans_b
              else pl.BlockSpec((tk, tn), lambda i, j, k: (k, j)))
    o_spec = pl.BlockSpec((tm, tn), lambda i, j, k: (i, j))
    return pl.pallas_call(
        body, name=name,
        out_shape=jax.ShapeDtypeStruct((M, N), out_dtype),
        grid=(M // tm, N // tn, nk),
        in_specs=[pl.BlockSpec((tm, tk), lambda i, j, k: (i, k)), b_spec] + ([o_spec] if has_add else []),
        out_specs=o_spec,
        scratch_shapes=[pltpu.VMEM((tm, tn), F32)],
        compiler_params=_params(("parallel", "parallel", "arbitrary")),
    )(*((a, b, add) if has_add else (a, b)))


def _mm_tn(a, g, *, name, out_dtype=F32):
    M, K = a.shape
    N = g.shape[1]
    tm, tk, tn = _pick(M, _TILE_M), _pick(K, _TILE_K), _pick(N, _TILE_N)
    nm = M // tm

    def body(a_ref, g_ref, o_ref, acc_ref):
        i = pl.program_id(2)

        @pl.when(i == 0)
        def _():
            acc_ref[...] = jnp.zeros_like(acc_ref)

        acc_ref[...] += _dot(a_ref[...].astype(BF16), g_ref[...].astype(BF16), _TN)

        @pl.when(i == nm - 1)
        def _():
            o_ref[...] = acc_ref[...].astype(o_ref.dtype)

    return pl.pallas_call(
        body, name=name,
        out_shape=jax.ShapeDtypeStruct((K, N), out_dtype),
        grid=(K // tk, N // tn, nm),
        in_specs=[pl.BlockSpec((tm, tk), lambda k, j, i: (i, k)), pl.BlockSpec((tm, tn), lambda k, j, i: (i, j))],
        out_specs=pl.BlockSpec((tk, tn), lambda k, j, i: (k, j)),
        scratch_shapes=[pltpu.VMEM((tk, tn), F32)],
        compiler_params=_params(("parallel", "parallel", "arbitrary")),
    )(a, g)


def _norm_fwd_call(x, w, a2, b2, res, tr, out_dtype=F32):
    T, W = x.shape
    has_res = res is not None

    def body(*refs):
        x_ref, w_ref, a_ref, b_ref = refs[:4]
        y_ref = refs[-1]
        seg = jnp.minimum(pl.program_id(0), 1)
        xv = x_ref[...]
        rstd = lax.rsqrt(jnp.mean(xv * xv, axis=-1, keepdims=True) + RMS_EPS)
        y = a_ref[pl.ds(seg, 1), :] * (xv * rstd * w_ref[...]) + b_ref[pl.ds(seg, 1), :]
        y_ref[...] = (y + refs[4][...] if has_res else y).astype(y_ref.dtype)

    row = pl.BlockSpec((tr, W), lambda i: (i, 0))
    small = pl.BlockSpec((8, W), lambda i: (0, 0))
    return pl.pallas_call(
        body, name="norm_fwd",
        out_shape=jax.ShapeDtypeStruct((T, W), out_dtype),
        grid=(T // tr,),
        in_specs=[row, pl.BlockSpec((1, W), lambda i: (0, 0)), small, small] + ([row] if has_res else []),
        out_specs=row,
        compiler_params=_params(("parallel",)),
    )(*((x, w.reshape(1, W), a2, b2) + ((res,) if has_res else ())))


def _norm_bwd_call(x, w, a2, dy, tr, add=None, out_dtype=F32):
    T, W = x.shape
    has_add = add is not None

    def body(*refs):
        x_ref, w_ref, a_ref, dy_ref = refs[:4]
        dx_ref, dw_ref, da_ref, db_ref = refs[-4:]
        i = pl.program_id(0)
        seg = jnp.minimum(i, 1)

        @pl.when(i == 0)
        def _():
            dw_ref[...] = jnp.zeros_like(dw_ref)
            da_ref[...] = jnp.zeros_like(da_ref)
            db_ref[...] = jnp.zeros_like(db_ref)

        xv = x_ref[...]
        g = dy_ref[...]
        wv = w_ref[...]
        rstd = lax.rsqrt(jnp.mean(xv * xv, axis=-1, keepdims=True) + RMS_EPS)
        xh = xv * rstd
        da_ref[pl.ds(seg, 1), :] += jnp.sum(g * (xh * wv), axis=0, keepdims=True)
        db_ref[pl.ds(seg, 1), :] += jnp.sum(g, axis=0, keepdims=True)
        gy = g * a_ref[pl.ds(seg, 1), :]
        dw_ref[0:1, :] += jnp.sum(gy * xh, axis=0, keepdims=True)
        gx = gy * wv
        dx = rstd * (gx - xh * jnp.mean(gx * xh, axis=-1, keepdims=True))
        dx_ref[...] = (dx + refs[4][...] if has_add else dx).astype(dx_ref.dtype)

    acc = jax.ShapeDtypeStruct((8, W), F32)
    acc_spec = pl.BlockSpec((8, W), lambda i: (0, 0))
    row = pl.BlockSpec((tr, W), lambda i: (i, 0))
    return pl.pallas_call(
        body, name="norm_bwd",
        out_shape=(jax.ShapeDtypeStruct((T, W), out_dtype), acc, acc, acc),
        grid=(T // tr,),
        in_specs=[row, pl.BlockSpec((1, W), lambda i: (0, 0)), acc_spec, row] + ([row] if has_add else []),
        out_specs=(row, acc_spec, acc_spec, acc_spec),
        compiler_params=_params(("arbitrary",)),
    )(*((x, w.reshape(1, W), a2, dy) + ((add,) if has_add else ())))


def _act_call(u1, u2, dact=None):
    T, W = u1.shape
    tr = _pick(T, (512, 256, 128, 64))
    tn = _pick(W, (1408, 512, 256, 128))
    spec = pl.BlockSpec((tr, tn), lambda i, j: (i, j))
    sh = jax.ShapeDtypeStruct((T, W), BF16)
    if dact is None:
        def body(a_ref, b_ref, o_ref):
            a = a_ref[...].astype(F32)
            o_ref[...] = (a * _sig(a) * b_ref[...].astype(F32)).astype(o_ref.dtype)

        return pl.pallas_call(body, name="act_fwd", out_shape=sh, grid=(T // tr, W // tn), in_specs=[spec, spec],
                              out_specs=spec, compiler_params=_params(("parallel", "parallel")))(u1, u2)

    def body(a_ref, b_ref, g_ref, da_ref, db_ref):
        a, g = a_ref[...].astype(F32), g_ref[...].astype(F32)
        s = _sig(a)
        da_ref[...] = (g * b_ref[...].astype(F32) * (s * (1.0 + a * (1.0 - s)))).astype(da_ref.dtype)
        db_ref[...] = (g * a * s).astype(db_ref.dtype)

    return pl.pallas_call(body, name="act_bwd", out_shape=(sh, sh), grid=(T // tr, W // tn),
                          in_specs=[spec, spec, spec], out_specs=(spec, spec),
                          compiler_params=_params(("parallel", "parallel")))(u1, u2, dact)


def _conv_specs(T, Wc, tr):
    hb, nt = tr // 8, T // tr
    row = pl.BlockSpec((tr, Wc), lambda i: (i, 0))
    prev = pl.BlockSpec((8, Wc), lambda i: (jnp.maximum(i * hb - 1, 0), 0))
    nxt = pl.BlockSpec((8, Wc), lambda i: (jnp.minimum((i + 1) * hb, T // 8 - 1), 0))
    return row, prev, nxt, nt


def _fill_ext(dst_ref, cur_ref, prev_ref, next_ref, i, nt, tr):
    has_prev = (i > 1).astype(F32)
    has_next = jnp.logical_and(i > 0, i < nt - 1).astype(F32)
    dst_ref[8:16, :] = prev_ref[...] * has_prev
    dst_ref[16:16 + tr, :] = cur_ref[...]
    dst_ref[16 + tr:24 + tr, :] = next_ref[...] * has_next


def _conv_fwd_call(px, w8, b, tr):
    T, Wc = px.shape
    row, prev, nxt, nt = _conv_specs(T, Wc, tr)

    def body(x_ref, xp_ref, xn_ref, w_ref, b_ref, u_ref, xe_ref):
        i = pl.program_id(0)

        @pl.when(i == 0)
        def _():
            xe_ref[...] = jnp.zeros_like(xe_ref)

        _fill_ext(xe_ref, x_ref, xp_ref, xn_ref, i, nt, tr)
        y = b_ref[...] + w_ref[0:1, :] * xe_ref[pl.ds(14, tr), :]
        for k in range(1, 5):
            y = y + w_ref[k:k + 1, :] * xe_ref[pl.ds(14 + k, tr), :]
        u_ref[...] = y * _sig(y)

    return pl.pallas_call(
        body, name="conv_fwd", out_shape=jax.ShapeDtypeStruct((T, Wc), F32), grid=(nt,),
        in_specs=[row, prev, nxt, pl.BlockSpec((8, Wc), lambda i: (0, 0)), pl.BlockSpec((1, Wc), lambda i: (0, 0))],
        out_specs=row, scratch_shapes=[pltpu.VMEM((tr + 32, Wc), F32)],
        compiler_params=_params(("arbitrary",)),
    )(px, px, px, w8, b)


def _conv_bwd_call(px, w8, b, du, tr):
    T, Wc = px.shape
    row, prev, nxt, nt = _conv_specs(T, Wc, tr)
    E = tr + 16

    def body(x_ref, xp_ref, xn_ref, g_ref, gp_ref, gn_ref, w_ref, b_ref, dx_ref, dw_ref, db_ref,
             xe_ref, ge_ref, dy_ref):
        i = pl.program_id(0)

        @pl.when(i == 0)
        def _():
            xe_ref[...] = jnp.zeros_like(xe_ref)
            ge_ref[...] = jnp.zeros_like(ge_ref)
            dy_ref[...] = jnp.zeros_like(dy_ref)
            dw_ref[...] = jnp.zeros_like(dw_ref)
            db_ref[...] = jnp.zeros_like(db_ref)

        _fill_ext(xe_ref, x_ref, xp_ref, xn_ref, i, nt, tr)
        _fill_ext(ge_ref, g_ref, gp_ref, gn_ref, i, nt, tr)
        y = b_ref[...] + w_ref[0:1, :] * xe_ref[pl.ds(6, E), :]
        for k in range(1, 5):
            y = y + w_ref[k:k + 1, :] * xe_ref[pl.ds(6 + k, E), :]
        s = _sig(y)
        dy = ge_ref[pl.ds(8, E), :] * (s * (1.0 + y * (1.0 - s)))
        dy_ref[pl.ds(8, E), :] = dy
        dx = w_ref[0:1, :] * dy_ref[pl.ds(18, tr), :]
        for k in range(1, 5):
            dx = dx + w_ref[k:k + 1, :] * dy_ref[pl.ds(18 - k, tr), :]
        dx_ref[...] = dx.astype(dx_ref.dtype)
        dyt = dy_ref[pl.ds(16, tr), :]
        db_ref[0:1, :] += jnp.sum(dyt, axis=0, keepdims=True)
        for k in range(5):
            dw_ref[k:k + 1, :] += jnp.sum(dyt * xe_ref[pl.ds(14 + k, tr), :], axis=0, keepdims=True)

    acc = jax.ShapeDtypeStruct((8, Wc), F32)
    acc_spec = pl.BlockSpec((8, Wc), lambda i: (0, 0))
    ext = pltpu.VMEM((tr + 32, Wc), F32)
    return pl.pallas_call(
        body, name="conv_bwd", out_shape=(jax.ShapeDtypeStruct((T, Wc), BF16), acc, acc), grid=(nt,),
        in_specs=[row, prev, nxt, row, prev, nxt, acc_spec, pl.BlockSpec((1, Wc), lambda i: (0, 0))],
        out_specs=(row, acc_spec, acc_spec), scratch_shapes=[ext, ext, ext],
        compiler_params=_params(("arbitrary",)),
    )(px, px, px, du, du, du, w8, b)


_SCAN_CFG = {
    "gla": dict(H=4, Dk=32, Dv=64, nh=4, scalar=False, C=128),
    "ssd": dict(H=8, Dk=128, Dv=64, nh=2, scalar=True, C=128),
    "ret": dict(H=4, Dk=64, Dv=64, nh=4, scalar=True, C=128),
}
GPAD = 8


def _log2(n):
    r = int(math.log2(n))
    assert 1 << r == n
    return r


class _ScanMath:
    def __init__(self, cfg, reverse):
        C = cfg["C"]
        self.C, self.reverse = C, reverse
        self.Dk, self.Dv, self.nh, self.scalar = cfg["Dk"], cfg["Dv"], cfg["nh"], cfg["scalar"]
        self.Wk, self.Wv = self.nh * self.Dk, self.nh * self.Dv
        self.nsg = cfg["H"] // self.nh
        nh, Wk, Wv = self.nh, self.Wk, self.Wv
        lk, lv, lc = _log2(self.Dk), _log2(self.Dv), _log2(C)
        r, c = _iota((C, C), 0), _iota((C, C), 1)
        self.L = ((c >= r) if reverse else (c <= r)).astype(F32)
        self.Lsuf = ((c <= r) if reverse else (c >= r)).astype(F32)
        i, j = _iota((C, nh * C), 0), _iota((C, nh * C), 1) & (C - 1)
        self.Mst = (j >= i) if reverse else (j <= i)
        self.Dj = (i == j).astype(F32)
        self.nb = 1 if (self.scalar or C == 64) else 3
        assert self.scalar or C in (64, 128)
        lanes = _iota((1, self.nb * Wk), 1) & (Wk - 1)
        self.km = [((lanes >> lk) == h).astype(F32) for h in range(nh)]
        self.vm = [((_iota((1, Wv), 1) >> lv) == h).astype(F32) for h in range(nh)]
        self.BD = ((_iota((Wv, Wk), 0) >> lv) == (_iota((Wv, Wk), 1) >> lk)).astype(F32)
        self.last = 0 if reverse else C - 1
        self.last_row = (_iota((C, 1), 0) == self.last).astype(F32)
        self.lk, self.lc = lk, lc
        self.H = cfg["H"]

    def gates(self, g):
        if not self.scalar:
            return _dot_sel(g, self.L, _NN, x_left=False), None
        G8 = _dot_sel(g, self.L, _NN, x_left=False)
        nk, ncol = self.H * self.Dk, self.H * self.C
        ek = (_iota((GPAD, nk), 0) == (_iota((GPAD, nk), 1) >> self.lk)).astype(F32)
        ec = (_iota((GPAD, ncol), 0) == (_iota((GPAD, ncol), 1) >> self.lc)).astype(F32)
        return _dot_sel(G8, ek, _NN), _dot_sel(G8, ec, _NN)

    def Ek(self, s):
        return (_iota((GPAD, self.Wk), 0) == (_iota((GPAD, self.Wk), 1) >> self.lk) + s * self.nh).astype(F32)

    def fold(self, x, factors=None):
        Wk = self.Wk
        out = None
        for b in range(self.nb):
            t = x[:, b * Wk:(b + 1) * Wk]
            t = t if factors is None or factors[b] is None else t * factors[b]
            out = t if out is None else out + t
        return out

    def kstack(self, x):
        return jnp.concatenate([x * self.km[h] for h in range(self.nh)], axis=0)

    def vstack(self, x):
        return jnp.concatenate([x * self.vm[h] for h in range(self.nh)], axis=0)

    def unstack(self, R, masks):
        C = self.C
        out = R[0:C] * masks[0]
        for h in range(1, self.nh):
            out = out + R[h * C:(h + 1) * C] * masks[h]
        return out

    def chunk(self, qs, ks, Gk, Gc):
        C = self.C
        Glast = Gk[self.last:self.last + 1, :]
        out = dict(Gk=Gk, Glast=Glast, eG=jnp.exp(Gk), eGl=jnp.exp(Glast - Gk), eGlast=jnp.exp(Glast))
        if self.scalar:
            Gr = jnp.sum(Gc * self.Dj, axis=0, keepdims=True)
            dec = jnp.where(self.Mst, jnp.exp(jnp.minimum(Gc - Gr, 0.0)), 0.0)
            qt, kt = qs, ks
            A = _dot(_bf(qt), _bf(self.kstack(kt)), _NT) * dec
            out.update(dec=dec, qt=qt, kt=kt, A=A, fq=[None], fk=[None])
        elif self.nb == 1:
            Gm = Gk[C // 2:C // 2 + 1, :]
            fq, fk = [jnp.exp(Gk - Gm)], [jnp.exp(Gm - Gk)]
            qt, kt = qs * fq[0], ks * fk[0]
            A = jnp.where(self.Mst, _dot(_bf(qt), _bf(self.kstack(kt)), _NT), 0.0)
            out.update(fq=fq, fk=fk, qt=qt, kt=kt, A=A)
        else:
            h = C // 2
            rows = _iota((C, 1), 0)
            early = (rows >= h) if self.reverse else (rows < h)
            late = jnp.logical_not(early)
            m_e, m_l, b = (h + h // 2, h // 2, h) if self.reverse else (h // 2, h + h // 2, h - 1)
            Ge, Gl, Gb = Gk[m_e:m_e + 1, :], Gk[m_l:m_l + 1, :], Gk[b:b + 1, :]

            def factor(mask, arg):
                return jnp.where(mask, jnp.exp(jnp.where(mask, arg, 0.0)), 0.0)

            fq = [factor(early, Gk - Ge), factor(late, Gk - Gl), factor(late, Gk - Gb)]
            fk = [factor(early, Ge - Gk), factor(late, Gl - Gk), factor(early, Gb - Gk)]
            qt = jnp.concatenate([qs * f for f in fq], axis=1)
            kt = jnp.concatenate([ks * f for f in fk], axis=1)
            A = jnp.where(self.Mst, _dot(_bf(qt), _bf(self.kstack(kt)), _NT), 0.0)
            out.update(fq=fq, fk=fk, qt=qt, kt=kt, A=A)
        return out


def _chunk_index(p, n, nc, reverse):
    if not reverse:
        return p
    return jnp.where(p < nc, nc - 1 - p, n - 1 + nc - p)


def _scan_dims(kind):
    cfg = _SCAN_CFG[kind]
    HK, HV = cfg["H"] * cfg["Dk"], cfg["H"] * cfg["Dv"]
    return cfg, cfg["C"], HK, HV, (GPAD if cfg["scalar"] else HK)


def _scan_fwd_step(m, q_ref, k_ref, v_ref, g_ref, o_ref, st_ref, S_ref):
    C = m.C

    @pl.when(pl.program_id(0) == 0)
    def _():
        S_ref[...] = jnp.zeros_like(S_ref)

    Gk_all, Gc_all = m.gates(g_ref[...])
    for s in range(m.nsg):
        ksl, vsl = slice(s * m.Wk, (s + 1) * m.Wk), slice(s * m.Wv, (s + 1) * m.Wv)
        csl = slice(s * m.nh * C, (s + 1) * m.nh * C)
        qs, ks, vs = q_ref[:, ksl], k_ref[:, ksl], v_ref[:, vsl]
        ch = m.chunk(qs, ks, Gk_all[:, ksl], Gc_all[:, csl] if m.scalar else None)
        S = S_ref[vsl, :]
        o = _dot(_bf(ch["A"]), _bf(m.vstack(vs)), _NN) + _dot(_bf(qs * ch["eG"]), _bf(S), _NT)
        o_ref[:, vsl] = o
        st_ref[0, vsl, :] = S
        S_ref[vsl, :] = S * ch["eGlast"] + _dot(_bf(vs), _bf(ks * ch["eGl"]), _TN) * m.BD


def _scan_fwd_call(kind, ops, Tc, comm=None):
    cfg, C, HK, HV, GW = _scan_dims(kind)
    T = ops[False][0][0].shape[0]
    n, nc = T // C, Tc // C

    def body(*refs):
        for d, rev in enumerate((False, True)):
            _scan_fwd_step(_ScanMath(cfg, rev), *refs[4 * d:4 * d + 4], *refs[8 + 2 * d:10 + 2 * d], refs[12 + d])

    sg = cfg["H"] // cfg["nh"]
    Wk, Wv = cfg["nh"] * cfg["Dk"], cfg["nh"] * cfg["Dv"]
    col = lambda rev, w, j: pl.BlockSpec((C, w), lambda p: (_chunk_index(p, n, nc, rev), j))
    st_spec = lambda rev: pl.BlockSpec((1, sg * Wv, Wk), lambda p: (_chunk_index(p, n, nc, rev), 0, 0))
    in_specs, args, out_specs, out_shape = [], [], [], []
    for rev in (False, True):
        q, k, v, g = ops[rev]
        in_specs += [col(rev, HK, q[1]), col(rev, HK, k[1]), col(rev, HV, v[1]), col(rev, GW, g[1])]
        args += [q[0], k[0], v[0], g[0]]
        out_specs += [col(rev, HV, 0), st_spec(rev)]
        out_shape += [jax.ShapeDtypeStruct((T, HV), F32), jax.ShapeDtypeStruct((n, sg * Wv, Wk), F32)]
    res, got = _pcall(body, name=f"scan_fwd_{kind}", out_shape=out_shape, grid=(n,), in_specs=in_specs,
                      out_specs=out_specs, scratch_shapes=[pltpu.VMEM((sg * Wv, Wk), F32)] * 2,
                      sem=("arbitrary",), args=args, comm=comm)
    return {False: (res[0], res[1]), True: (res[2], res[3])}, got


def _scan_bwd_step(m, need_dg, q_ref, k_ref, v_ref, g_ref, st_ref, do_ref, dq_ref, dk_ref, dv_ref, dg_ref, dS_ref):
    C = m.C

    @pl.when(pl.program_id(0) == 0)
    def _():
        dS_ref[...] = jnp.zeros_like(dS_ref)

    x8 = jnp.zeros((C, GPAD), F32)
    Gk_all, Gc_all = m.gates(g_ref[...])
    for s in range(m.nsg):
        ksl, vsl = slice(s * m.Wk, (s + 1) * m.Wk), slice(s * m.Wv, (s + 1) * m.Wv)
        csl = slice(s * m.nh * C, (s + 1) * m.nh * C)
        qs, ks, vs, dos = q_ref[:, ksl], k_ref[:, ksl], v_ref[:, vsl], do_ref[:, vsl]
        ch = m.chunk(qs, ks, Gk_all[:, ksl], Gc_all[:, csl] if m.scalar else None)
        S = st_ref[0, vsl, :]
        dS = dS_ref[vsl, :]
        A, qt, kt = ch["A"], ch["qt"], ch["kt"]
        dA = _dot(_bf(dos), _bf(m.vstack(vs)), _NT)
        dAm = dA * ch["dec"] if m.scalar else jnp.where(m.Mst, dA, 0.0)
        kst = _bf(m.kstack(kt))
        dv = m.unstack(_dot(_bf(A), _bf(dos), _TN), m.vm) + _dot(_bf(ks * ch["eGl"]), _bf(dS), _NT)
        dv_ref[:, vsl] = dv
        dq_i = _dot(_bf(dAm), kst, _NN)
        dq_x = ch["eG"] * _dot(_bf(dos), _bf(S), _NN)
        dq_ref[:, ksl] = m.fold(dq_i, ch["fq"]) + dq_x
        dk_i = m.unstack(_dot(_bf(dAm), _bf(qt), _TN), m.km)
        dk_x = ch["eGl"] * _dot(_bf(vs), _bf(dS), _NN)
        dk_ref[:, ksl] = m.fold(dk_i, ch["fk"]) + dk_x
        if need_dg:
            bnd = (ch["eGlast"] * jnp.sum(dS * S, axis=0, keepdims=True)
                   + jnp.sum(ks * dk_x, axis=0, keepdims=True))
            X = m.fold(_bf(qt).astype(F32) * dq_i - _bf(kt).astype(F32) * dk_i) + (qs * dq_x - ks * dk_x)
            X = X + m.last_row * bnd
            if m.scalar:
                x8 = x8 + _dot_sel(X, m.Ek(s), _NT)
            else:
                dg_ref[:, ksl] = _dot_sel(X, m.Lsuf, _NN, x_left=False)
        dS_ref[vsl, :] = dS * ch["eGlast"] + _dot(_bf(dos), _bf(qs * ch["eG"]), _TN) * m.BD
    if m.scalar:
        dg_ref[...] = _dot_sel(x8, m.Lsuf, _NN, x_left=False)
    elif not need_dg:
        dg_ref[...] = jnp.zeros_like(dg_ref)


def _scan_bwd_call(kind, need_dg, ops, st, do, Tc, comm=None):
    cfg, C, HK, HV, GW = _scan_dims(kind)
    T = ops[False][0][0].shape[0]
    n, nc = T // C, Tc // C

    def body(*refs):
        for d, rev in enumerate((False, True)):
            _scan_bwd_step(_ScanMath(cfg, rev), need_dg, *refs[6 * d:6 * d + 6], *refs[12 + 4 * d:16 + 4 * d],
                           refs[20 + d])

    sg = cfg["H"] // cfg["nh"]
    Wk, Wv = cfg["nh"] * cfg["Dk"], cfg["nh"] * cfg["Dv"]
    col = lambda rev, w, j: pl.BlockSpec((C, w), lambda p: (_chunk_index(n - 1 - p, n, nc, rev), j))
    st_spec = lambda rev: pl.BlockSpec((1, sg * Wv, Wk), lambda p: (_chunk_index(n - 1 - p, n, nc, rev), 0, 0))
    in_specs, args, out_specs, out_shape = [], [], [], []
    for rev in (False, True):
        q, k, v, g = ops[rev]
        in_specs += [col(rev, HK, q[1]), col(rev, HK, k[1]), col(rev, HV, v[1]), col(rev, GW, g[1]),
                     st_spec(rev), col(rev, HV, 0)]
        args += [q[0], k[0], v[0], g[0], st[rev], do]
        out_specs += [col(rev, HK, 0), col(rev, HK, 0), col(rev, HV, 0), col(rev, GW, 0)]
        out_shape += [jax.ShapeDtypeStruct((T, w), F32) for w in (HK, HK, HV, GW)]
    res, got = _pcall(body, name=f"scan_bwd_{kind}", out_shape=out_shape, grid=(n,), in_specs=in_specs,
                      out_specs=out_specs, scratch_shapes=[pltpu.VMEM((sg * Wv, Wk), F32)] * 2,
                      sem=("arbitrary",), args=args, comm=comm)
    return {False: res[0:4], True: res[4:8]}, got


def _prep_consts():
    r, c = _iota((256, 256), 0), _iota((256, 256), 1)
    first = (c & 63) < 32
    rope_perm = jnp.where(first, -(r == c + 32).astype(F32), (r == c - 32).astype(F32))
    sel_f = (_iota((128, GPAD), 0) == _iota((128, GPAD), 1) + 32).astype(F32)
    sel_b = (_iota((128, GPAD), 0) == _iota((128, GPAD), 1) + 40).astype(F32)
    ek = (_iota((GPAD, 1024), 0) == (_iota((GPAD, 1024), 1) >> 7)).astype(F32)
    return rope_perm, sel_f, sel_b, ek


def _prep_tile(misc, gq, rq, rk, bm, cm, cosE, sinE, Wg, gbias, dtbf, dtbb, nAf, nAb):
    rope_perm, sel_f, sel_b, ek = _prep_consts()
    logg = _log_sigmoid(_mm_bf(misc, Wg) + gbias) * (1.0 / GLA_TAU)
    a_gla = jnp.concatenate([gq * (32 ** -0.5), logg], axis=1)
    rot = lambda t: t * cosE + _sel(t, rope_perm) * sinE
    a_ret = jnp.concatenate([rot(rq * (64 ** -0.5)), rot(rk)], axis=1)
    dtf = _softplus(_sel(misc, sel_f) + dtbf)
    dtb = _softplus(_sel(misc, sel_b) + dtbb)
    rep = lambda t: jnp.concatenate([t[:, :128]] * 4 + [t[:, 128:]] * 4, axis=1)
    bmr = rep(bm)
    return a_gla, a_ret, rep(cm), bmr * _sel(dtf, ek), bmr * _sel(dtb, ek), dtf * nAf, dtb * nAb


def _prep_row_specs(tr):
    blk = lambda w, j: pl.BlockSpec((tr, w), lambda i: (i, j))
    return [blk(128, MISC // 128), blk(128, _ROFF["gla_q"] // 128), blk(256, _ROFF["ret_q"] // 256),
            blk(256, _ROFF["ret_k"] // 256), blk(256, 2), blk(256, 3), blk(256, 0), blk(256, 0)]


def _whole(a):
    return pl.BlockSpec(a.shape, lambda i: (0,) * a.ndim)


def _prep_fwd_call(Pr, u, cosE, sinE, pp, tr):
    T = Pr.shape[0]
    n_row = 8

    def body(*refs):
        outs = _prep_tile(*[r[...] for r in refs[:n_row + len(pp)]])
        for o_ref, o in zip(refs[n_row + len(pp):], outs):
            o_ref[...] = o

    widths = [384, 512, 1024, 1024, 1024, GPAD, GPAD]
    return pl.pallas_call(
        body, name="prep_fwd", grid=(T // tr,),
        out_shape=tuple(jax.ShapeDtypeStruct((T, w), F32) for w in widths),
        in_specs=_prep_row_specs(tr) + [_whole(p) for p in pp],
        out_specs=tuple(pl.BlockSpec((tr, w), lambda i: (i, 0)) for w in widths),
        compiler_params=_params(("parallel",)),
    )(Pr, Pr, Pr, Pr, u, u, cosE, sinE, *pp)


def _prep_bwd_call(Pr, u, cosE, sinE, pp, cts, tr, comm=None):
    T = Pr.shape[0]
    n_row, n_p = 8, len(pp)
    names = ["gla_dq_f", "gla_dq_b", "gla_dg_f", "gla_dg_b", "gla_dk_f", "gla_dk_b", "gla_dv_f", "gla_dv_b",
             "ret_dq_f", "ret_dq_b", "ret_dk_f", "ret_dk_b", "ret_dv_f", "ret_dv_b",
             "ssd_dq_f", "ssd_dq_b", "ssd_dk_f", "ssd_dk_b", "ssd_dg_f", "ssd_dg_b", "ssd_dv_f", "ssd_dv_b",
             "d_r", "d_z", "d_gr", "d_xs"]
    ct_arrays = [cts[n] for n in names]

    def body(*refs):
        ins = [r[...] for r in refs[:n_row + n_p]]
        c = {n: r[...] for n, r in zip(names, refs[n_row + n_p:n_row + n_p + len(names)])}
        dPr_ref, du_ref = refs[n_row + n_p + len(names):n_row + n_p + len(names) + 2]
        dp_refs = refs[n_row + n_p + len(names) + 2:]
        _, vjp = jax.vjp(_prep_tile, *ins)
        ct_out = (jnp.concatenate([c["gla_dq_f"] + c["gla_dq_b"], c["gla_dg_f"], c["gla_dg_b"]], axis=1),
                  jnp.concatenate([c["ret_dq_f"] + c["ret_dq_b"], c["ret_dk_f"] + c["ret_dk_b"]], axis=1),
                  c["ssd_dq_f"] + c["ssd_dq_b"], c["ssd_dk_f"], c["ssd_dk_b"], c["ssd_dg_f"], c["ssd_dg_b"])
        d = vjp(ct_out)
        d_misc, d_gq, d_rq, d_rk, d_bm, d_cm = d[:6]
        dPr_ref[...] = jnp.concatenate(
            [c["gla_dv_f"] + c["gla_dv_b"], c["d_r"], d_rq, d_rk, c["ret_dv_f"] + c["ret_dv_b"], c["d_gr"],
             c["d_z"], d_gq, c["gla_dk_f"] + c["gla_dk_b"], d_misc,
             jnp.zeros((d_misc.shape[0], R_W - MISC - 128), F32)], axis=1).astype(dPr_ref.dtype)
        du_ref[...] = jnp.concatenate([c["ssd_dv_f"] + c["ssd_dv_b"] + c["d_xs"], d_bm, d_cm], axis=1)

        @pl.when(pl.program_id(0) == 0)
        def _():
            for r in dp_refs:
                r[...] = jnp.zeros_like(r)

        for r, g in zip(dp_refs, d[n_row:]):
            r[...] += g

    row = lambda a: pl.BlockSpec((tr, a.shape[1]), lambda i: (i, 0))
    return _pcall(
        body, name="prep_bwd", grid=(T // tr,),
        out_shape=(jax.ShapeDtypeStruct((T, R_W), BF16), jax.ShapeDtypeStruct((T, 1024), F32))
        + tuple(jax.ShapeDtypeStruct(p.shape, F32) for p in pp),
        in_specs=_prep_row_specs(tr) + [_whole(p) for p in pp] + [row(a) for a in ct_arrays],
        out_specs=(pl.BlockSpec((tr, R_W), lambda i: (i, 0)), pl.BlockSpec((tr, 1024), lambda i: (i, 0)))
        + tuple(_whole(p) for p in pp),
        scratch_shapes=[], sem=("arbitrary",), args=(Pr, Pr, Pr, Pr, u, u, cosE, sinE, *pp, *ct_arrays), comm=comm)


def _post_tile(ogf, ogb, r, ysf, ysb, xs, z, orf, orb, gr, gla_n, dexp, ssd_n, ret_n):
    bd = ((_iota((256, 256), 0) >> 6) == (_iota((256, 256), 1) >> 6)).astype(F32)
    og = ogf + ogb
    gla = og * lax.rsqrt(_sel(og * og, bd) * (1.0 / 64) + RMS_EPS) * gla_n * _silu(r)
    t = (ysf + ysb + dexp * xs) * _silu(z)
    ssd = t * lax.rsqrt(jnp.mean(t * t, axis=-1, keepdims=True) + RMS_EPS) * ssd_n
    o = orf + orb
    oc = o - _sel(o, bd) * (1.0 / 64)
    ret = oc * lax.rsqrt(_sel(oc * oc, bd) * (1.0 / 64) + RMS_EPS) * ret_n * _silu(gr)
    return jnp.concatenate([gla, ssd, ret], axis=1)


def _post_row_specs(tr):
    blk = lambda w, j: pl.BlockSpec((tr, w), lambda i: (i, j))
    return [blk(256, 0), blk(256, 0), blk(256, _ROFF["gla_r"] // 256), blk(512, 0), blk(512, 0), blk(512, 0),
            blk(512, _ROFF["ssd_z"] // 512), blk(256, 0), blk(256, 0), blk(256, _ROFF["ret_g"] // 256)]


def _post_fwd_call(rows, qp, tr, comm=None):
    T = rows[0].shape[0]

    def body(*refs):
        refs[-1][...] = _post_tile(*[r[...] for r in refs[:-1]]).astype(refs[-1].dtype)

    res, got = _pcall(body, name="post_fwd", grid=(T // tr,), out_shape=[jax.ShapeDtypeStruct((T, D), BF16)],
                      in_specs=_post_row_specs(tr) + [_whole(p) for p in qp],
                      out_specs=[pl.BlockSpec((tr, D), lambda i: (i, 0))], scratch_shapes=[],
                      sem=("parallel",), args=(*rows, *qp), comm=comm)
    return res[0], got


def _post_bwd_call(rows, qp, dmixed, tr):
    T = rows[0].shape[0]
    n_in = 10 + len(qp)

    def body(*refs):
        ins = [r[...] for r in refs[:n_in]]
        _, vjp = jax.vjp(_post_tile, *ins)
        d = vjp(refs[n_in][...])
        outs = refs[n_in + 1:]
        for o_ref, g in zip(outs[:7], (d[0], d[3], d[7], d[2], d[6], d[9], d[5])):
            o_ref[...] = g.astype(o_ref.dtype)

        @pl.when(pl.program_id(0) == 0)
        def _():
            for r in outs[7:]:
                r[...] = jnp.zeros_like(r)

        for r, g in zip(outs[7:], d[10:]):
            r[...] += g

    widths = [256, 512, 256, 256, 512, 256, 512]
    dts = [BF16] * 3 + [F32] * 4
    return pl.pallas_call(
        body, name="post_bwd", grid=(T // tr,),
        out_shape=tuple(jax.ShapeDtypeStruct((T, w), dt) for w, dt in zip(widths, dts))
        + tuple(jax.ShapeDtypeStruct(p.shape, F32) for p in qp),
        in_specs=_post_row_specs(tr) + [_whole(p) for p in qp] + [pl.BlockSpec((tr, D), lambda i: (i, 0))],
        out_specs=tuple(pl.BlockSpec((tr, w), lambda i: (i, 0)) for w in widths) + tuple(_whole(p) for p in qp),
        compiler_params=_params(("arbitrary",)),
    )(*rows, *qp, dmixed)


def _mixer_scan_operands(Pr, u, a_gla, a_ret, cmr, kf, kb, g8f, g8b, lg):
    gk, gv = (Pr, _ROFF["gla_k"] // 128), (Pr, _ROFF["gla_v"] // 256)
    rv = (Pr, _ROFF["ret_v"] // 256)
    return {
        "gla": {False: ((a_gla, 0), gk, gv, (a_gla, 1)), True: ((a_gla, 0), gk, gv, (a_gla, 2))},
        "ret": {False: ((a_ret, 0), (a_ret, 1), rv, (lg, 0)), True: ((a_ret, 0), (a_ret, 1), rv, (lg, 0))},
        "ssd": {False: ((cmr, 0), (kf, 0), (u, 0), (g8f, 0)), True: ((cmr, 0), (kb, 0), (u, 0), (g8b, 0))},
    }


def _post_rows(o, Pr, u):
    return [o["gla"][False][0], o["gla"][True][0], Pr, o["ssd"][False][0], o["ssd"][True][0], u, Pr,
            o["ret"][False][0], o["ret"][True][0], Pr]


def _mixer_forward(Tc, Pr, Px, cn, pp, cw8, cb, qp, comm):
    cosE, sinE, lg = cn
    u = _conv_fwd_call(Px, cw8, cb, Tc)
    prep = _prep_fwd_call(Pr, u, cosE, sinE, pp, Tc)
    ops = _mixer_scan_operands(Pr, u, *prep, lg)
    o, got = {}, {}
    for kind in ops:
        o[kind], got[kind] = _scan_fwd_call(kind, ops[kind], Tc, comm.get(kind))
    mixed, got["post"] = _post_fwd_call(_post_rows(o, Pr, u), qp, Tc, comm.get("post"))
    return mixed, (u, prep, o), got


def _mixer_backward(Tc, Pr, Px, cn, pp, cw8, cb, qp, saved, dmixed, comm):
    cosE, sinE, lg = cn
    u, prep, o = saved
    post = _post_bwd_call(_post_rows(o, Pr, u), qp, dmixed, Tc)
    d_o = dict(gla=post[0], ssd=post[1], ret=post[2])
    cts = dict(d_r=post[3], d_z=post[4], d_gr=post[5], d_xs=post[6])
    ops = _mixer_scan_operands(Pr, u, *prep, lg)
    got = {}
    for kind in ops:
        st = {rev: o[kind][rev][1] for rev in (False, True)}
        res, got[kind] = _scan_bwd_call(kind, kind != "ret", ops[kind], st, d_o[kind], Tc, comm.get(kind))
        for rev, sfx in ((False, "_f"), (True, "_b")):
            for nm, a in zip(("_dq", "_dk", "_dv", "_dg"), res[rev]):
                cts[kind + nm + sfx] = a
    pb, got["prep"] = _prep_bwd_call(Pr, u, cosE, sinE, pp, cts, Tc, comm.get("prep"))
    dPx, dcw8, dcb = _conv_bwd_call(Px, cw8, cb, pb[1], Tc)
    return pb[0], dPx, tuple(pb[2:]), dcw8, dcb[0:1], tuple(post[7:]), got


def _mix_fwd(Tc, X, w, cn, nw, mods, pp, cw8, cb, qp, comm):
    h = _norm_fwd_call(X, nw[0], mods[0], mods[1], None, Tc, BF16)
    Px, Pr = _mm(h, w[0], trans_b=True, name="mm_fwd"), _mm(h, w[1], trans_b=True, name="mm_fwd")
    mixed, saved, got = _mixer_forward(Tc, Pr, Px, cn, pp, cw8, cb, qp, comm)
    M = _mm(mixed, w[2], name="mm_fwd")
    Xn = _norm_fwd_call(M, nw[1], mods[2], jnp.zeros_like(mods[2]), X, Tc)
    return Xn, (X, nw, mods, w, cn, pp, cw8, cb, qp, h, Px, Pr, mixed, saved, M), got


def _mix_bwd(Tc, res, dXn, comm):
    X, nw, mods, w, cn, pp, cw8, cb, qp, h, Px, Pr, mixed, saved, M = res
    dM, dnw1, da_post, _ = _norm_bwd_call(M, nw[1], mods[2], dXn, Tc, out_dtype=BF16)
    dmixed = _mm(dM, w[2], trans_b=True, name="mm_dx")
    dPr, dPx, dpp, dcw8, dcb, dqp, got = _mixer_backward(Tc, Pr, Px, cn, pp, cw8, cb, qp, saved, dmixed, comm)
    dh = _mm(dPx, w[0], name="mm_dx")
    dh = _mm(dPr, w[1], name="mm_dx_acc", add=dh)
    dX, dnw0, da_pre, db_pre = _norm_bwd_call(X, nw[0], mods[0], dh, Tc, add=dXn)
    dW = tuple(_mm_tn(a, g, name="mm_dw", out_dtype=BF16) for a, g in ((dPx, h), (dPr, h), (mixed, dM)))
    return dX, ((dnw0[0], dnw1[0]), (da_pre, db_pre, da_post), dpp, dcw8, dcb, dqp), dW, got


def _ffn_fwd(Tc, X, w, nw, mods):
    h = _norm_fwd_call(X, nw[0], mods[0], mods[1], None, Tc, BF16)
    U1 = _mm(h, w[0], trans_b=True, name="mm_fwd", out_dtype=BF16)
    U2 = _mm(h, w[1], trans_b=True, name="mm_fwd", out_dtype=BF16)
    act = _act_call(U1, U2)
    Fo = _mm(act, w[2], name="mm_fwd")
    Xn = _norm_fwd_call(Fo, nw[1], mods[2], jnp.zeros_like(mods[2]), X, Tc)
    return Xn, (X, nw, mods, w, h, U1, U2, act, Fo)


def _ffn_bwd(Tc, res, dXn):
    X, nw, mods, w, h, U1, U2, act, Fo = res
    dFo, dnw1, da_post, _ = _norm_bwd_call(Fo, nw[1], mods[2], dXn, Tc, out_dtype=BF16)
    dU1, dU2 = _act_call(U1, U2, _mm(dFo, w[2], trans_b=True, name="mm_dx", out_dtype=BF16))
    dh = _mm(dU1, w[0], name="mm_dx")
    dh = _mm(dU2, w[1], name="mm_dx_acc", add=dh)
    dX, dnw0, da_pre, db_pre = _norm_bwd_call(X, nw[0], mods[0], dh, Tc, add=dXn)
    dW = tuple(_mm_tn(a, g, name="mm_dw", out_dtype=BF16) for a, g in ((dU1, h), (dU2, h), (act, dFo)))
    return dX, ((dnw0[0], dnw1[0]), (da_pre, db_pre, da_post)), dW


def _rope_tables(Tl, Tc):
    rows = Tl // GRID_W
    row = jnp.repeat(jnp.arange(rows), GRID_W).astype(F32)
    col = jnp.tile(jnp.arange(GRID_W), rows).astype(F32)
    inv_freq = 10000.0 ** (-jnp.arange(16, dtype=F32) / 16)
    ang = jnp.concatenate([row[:, None] * inv_freq, col[:, None] * inv_freq], axis=-1)
    cos = jnp.concatenate([jnp.ones((Tc, 32), F32), jnp.cos(ang)], axis=0)
    sin = jnp.concatenate([jnp.zeros((Tc, 32), F32), jnp.sin(ang)], axis=0)
    return jnp.tile(cos, (1, 8)), jnp.tile(sin, (1, 8))


def _rows8(first, second):
    z = jnp.zeros((6,) + first.shape, F32)
    return jnp.concatenate([first[None], second[None], z], axis=0)


def _layer_inputs(sp, ml, mc):
    gu = sp["gla_gate_up"]
    Wg = jnp.zeros((128, 256), F32).at[0:16, 0:128].set(gu[0]).at[16:32, 128:256].set(gu[1])
    pp = (Wg, sp["gla_gate_b"].reshape(1, 256), sp["ssd_dt_bias"][0:1], sp["ssd_dt_bias"][1:2],
          -jnp.exp(sp["ssd_a_log"][0:1]), -jnp.exp(sp["ssd_a_log"][1:2]))
    qp = (sp["gla_norm"].reshape(1, 256), jnp.repeat(sp["ssd_d"], 64).reshape(1, 512),
          sp["ssd_norm"].reshape(1, 512), sp["ret_norm"].reshape(1, 256))
    mix = ((sp["norm_mix_pre"], sp["norm_mix_post"]),
           (_rows8(1.0 + mc[1], 1.0 + ml[1]), _rows8(mc[0], ml[0]), _rows8(mc[2], ml[2])),
           pp, jnp.pad(sp["ssd_conv_w"], ((0, 3), (0, 0))), sp["ssd_conv_b"].reshape(1, 1024), qp)
    ffn = ((sp["norm_ffn_pre"], sp["norm_ffn_post"]),
           (_rows8(1.0 + mc[4], 1.0 + ml[4]), _rows8(mc[3], ml[3]), _rows8(mc[5], ml[5])))
    return mix, ffn


def _rows_from(g):
    return g.reshape(N_DEV * g.shape[1], g.shape[2])


def _rows_to(f):
    return f.reshape(N_DEV, f.shape[0] // N_DEV, f.shape[1])


def _local_step(xcat, target, mod_l, mod_c, sp, Tc, weights=None, shards=None):
    Tt = xcat.shape[0]
    cosE, sinE = _rope_tables(Tt - Tc, Tc)
    log_gamma = jnp.log1p(-jnp.exp2(-5.0 - jnp.arange(4, dtype=F32)))
    lg = jnp.broadcast_to(jnp.concatenate([log_gamma, jnp.zeros((GPAD - 4,), F32)])[None, :], (Tt, GPAD))
    cn = (cosE, sinE, lg)
    dist = shards is not None
    X, saved = xcat, []
    if dist:
        g_in, g_out = _exchange_call("two", shards[0][:2], "gather_mix0")
    for l in range(DEPTH):
        (a_mix, a_ffn), pull = jax.vjp(_layer_inputs, {n: sp[n][l] for n in _SMALL},
                                       mod_l[l].reshape(6, D), mod_c[l].reshape(6, D))
        comm = {}
        if dist:
            w_in, w_out = _rows_from(g_in), _rows_from(g_out)
            comm = dict(ssd=("two", [shards[l][2]]), ret=("two", [shards[l][3]]))
            if l + 1 < DEPTH:
                comm.update(gla=("two", [shards[l + 1][0]]), post=("two", [shards[l + 1][1]]))
        else:
            w_in, w_out, w13, w2 = weights[l]
        w_x, w_r = _split_w_in(w_in)
        X, r_mix, got = _mix_fwd(Tc, X, (w_x, w_r, w_out), cn, *a_mix, comm)
        if dist:
            w13, w2 = _rows_from(got["ssd"][0]), _rows_from(got["ret"][0])
            if l + 1 < DEPTH:
                g_in, g_out = got["gla"][0], got["post"][0]
        X, r_ffn = _ffn_fwd(Tc, X, (w13[:FFN_H], w13[FFN_H:], w2), *a_ffn)
        saved.append((r_mix, r_ffn, pull))
    loss, dX = _loss_call(X, target, Tc)
    d_sp, d_ml, d_mc = [None] * DEPTH, [None] * DEPTH, [None] * DEPTH
    gw = [[None] * 4 for _ in range(DEPTH)]
    nxt = None
    for l in reversed(range(DEPTH)):
        r_mix, r_ffn, pull = saved[l]
        dX, c_ffn, dW_ffn = _ffn_bwd(Tc, r_ffn, dX)
        g13, g2 = jnp.concatenate([dW_ffn[0], dW_ffn[1]], axis=0), dW_ffn[2]
        comm = {}
        if dist:
            comm = dict(ssd=(True, [_rows_to(g13)]), ret=(True, [_rows_to(g2)]))
            if nxt is not None:
                comm.update(gla=(True, [nxt[0]]), prep=(True, [nxt[1]]))
        dX, c_mix, dW_mix, got = _mix_bwd(Tc, r_mix, dX, comm)
        d_sp[l], d_ml[l], d_mc[l] = pull((c_mix, c_ffn))
        gin, gout = _merge_w_in(dW_mix[0], dW_mix[1]), dW_mix[2]
        if dist:
            gw[l][2], gw[l][3] = got["ssd"][0], got["ret"][0]
            if nxt is not None:
                gw[l + 1][0], gw[l + 1][1] = got["gla"][0], got["prep"][0]
            nxt = (_rows_to(gin), _rows_to(gout))
        else:
            gw[l] = [gin, gout, g13, g2]
    if dist:
        gw[0][0], gw[0][1] = _exchange_call(True, list(nxt), "scatter_mix0")
    d_sp = {n: jnp.stack([d_sp[l][n] for l in range(DEPTH)]) for n in _SMALL}
    return (loss, dX, jnp.stack(d_ml).reshape(DEPTH, 6 * D), jnp.stack(d_mc).reshape(DEPTH, 6 * D), d_sp, gw)


def _sum8_call(slabs, name):
    _, R, Cc = slabs.shape
    tr = _pick(R, (512, 352, 256, 128, 64, 32, 16))

    def body(*refs):
        acc = refs[0][...].astype(F32)
        for r in refs[1:N_DEV]:
            acc = acc + r[...].astype(F32)
        refs[N_DEV][...] = acc

    return pl.pallas_call(
        body, name=name, grid=(R // tr,), out_shape=jax.ShapeDtypeStruct((R, Cc), F32),
        in_specs=[pl.BlockSpec((None, tr, Cc), lambda i, d=d: (d, i, 0)) for d in range(N_DEV)],
        out_specs=pl.BlockSpec((tr, Cc), lambda i: (i, 0)), compiler_params=_params(("parallel",)),
    )(*([slabs] * N_DEV))


def _loss_call(X, target, Tc):
    Tt, W = X.shape
    tr = Tc
    nt = Tt // tr

    def body(x_ref, t_ref, loss_ref, dx_ref, acc_ref):
        i = pl.program_id(0)

        @pl.when(i == 0)
        def _():
            acc_ref[...] = jnp.zeros_like(acc_ref)
            dx_ref[...] = jnp.zeros_like(dx_ref)

        @pl.when(i > 0)
        def _():
            e = x_ref[...] - t_ref[...]
            dx_ref[...] = e * (1.0 / W)
            acc_ref[...] += jnp.sum(e * e, axis=0, keepdims=True)

        @pl.when(i == nt - 1)
        def _():
            loss_ref[...] = jnp.full(loss_ref.shape, (0.5 / W) * jnp.sum(acc_ref[...]), F32)

    loss, dx = pl.pallas_call(
        body, name="loss",
        out_shape=(jax.ShapeDtypeStruct((8, 128), F32), jax.ShapeDtypeStruct((Tt, W), F32)),
        grid=(nt,),
        in_specs=[pl.BlockSpec((tr, W), lambda i: (i, 0)),
                  pl.BlockSpec((tr, W), lambda i: (jnp.maximum(i - 1, 0), 0))],
        out_specs=(pl.BlockSpec((8, 128), lambda i: (0, 0)), pl.BlockSpec((tr, W), lambda i: (i, 0))),
        scratch_shapes=[pltpu.VMEM((1, W), F32)],
        compiler_params=_params(("arbitrary",)),
    )(X, target)
    return loss[0, 0], dx


def _adamw_call(w, g, m, v, name):
    R, Cc = w.shape
    tr = _pick(R, (512, 352, 256, 128, 64, 32, 16, 8))
    c1 = 1.0 - ADAM_B1 ** ADAM_STEP
    c2 = 1.0 - ADAM_B2 ** ADAM_STEP

    def body(w_ref, g_ref, m_ref, v_ref, d_ref, nm_ref, nv_ref):
        gv = g_ref[...]
        nm = ADAM_B1 * m_ref[...] + (1.0 - ADAM_B1) * gv
        nv = ADAM_B2 * v_ref[...] + (1.0 - ADAM_B2) * (gv * gv)
        d_ref[...] = -ADAM_LR * ((nm / c1) / (jnp.sqrt(nv / c2) + ADAM_EPS) + ADAM_WD * w_ref[...])
        nm_ref[...] = nm
        nv_ref[...] = nv

    spec = pl.BlockSpec((tr, Cc), lambda i: (i, 0))
    sh = jax.ShapeDtypeStruct((R, Cc), F32)
    return pl.pallas_call(
        body, name=name, out_shape=(sh, sh, sh), grid=(R // tr,),
        in_specs=[spec] * 4, out_specs=(spec,) * 3, compiler_params=_params(("parallel",)),
    )(w, g, m, v)


def _sum_call(xs, name, also_bf16=False):
    R, Cc = xs[0].shape
    tr = _pick(R, (512, 352, 256, 128, 64, 32, 16))
    k = len(xs)

    def body(*refs):
        acc = refs[0][...].astype(F32)
        for r in refs[1:k]:
            acc = acc + r[...].astype(F32)
        refs[k][...] = acc
        if also_bf16:
            refs[k + 1][...] = acc.astype(BF16)

    spec = pl.BlockSpec((tr, Cc), lambda i: (i, 0))
    sh = jax.ShapeDtypeStruct((R, Cc), F32)
    return pl.pallas_call(
        body, name=name, grid=(R // tr,), in_specs=[spec] * k,
        out_shape=(sh, jax.ShapeDtypeStruct((R, Cc), BF16)) if also_bf16 else sh,
        out_specs=(spec, spec) if also_bf16 else spec, compiler_params=_params(("parallel",)),
    )(*xs)


MESH = pl.DeviceIdType.MESH
ANY = pl.BlockSpec(memory_space=pl.ANY)


def _me():
    return lax.axis_index("x"), lax.axis_index("y"), lax.axis_index("c")


_FLIPS = [(0, 0, 1), (1, 0, 0), (0, 1, 0), (1, 1, 0), (1, 0, 1), (0, 1, 1), (1, 1, 1)]


def _exchange_copies(scatter, srcs, dsts, send_sems, recv_sems, loc_sems, arrivals):
    x, y, c = _me()
    me = 4 * x + 2 * y + c
    sends, recvs, local = [], [], []
    for a in range(len(srcs)):
        for k, (dx, dy, dc) in enumerate(_FLIPS):
            px, py, pc = (1 - x if dx else x), (1 - y if dy else y), (1 - c if dc else c)
            peer = 4 * px + 2 * py + pc
            src = srcs[a].at[peer] if scatter else srcs[a]
            for lst, slab in ((sends, me), (recvs, peer)) if arrivals else ((sends, me),):
                lst.append(pltpu.make_async_remote_copy(
                    src_ref=src, dst_ref=dsts[a].at[slab], send_sem=send_sems.at[a, k], recv_sem=recv_sems.at[a, k],
                    device_id=(px, py, pc), device_id_type=MESH))
        local.append(pltpu.make_async_copy(srcs[a].at[me] if scatter else srcs[a], dsts[a].at[me], loc_sems.at[a]))
    return sends, recvs, local


def _exchange_start(scatter, srcs, dsts, sems):
    sends, _, local = _exchange_copies(scatter, srcs, dsts, *sems, arrivals=False)
    for cp in local + sends:
        cp.start()


def _exchange_wait(scatter, srcs, dsts, sems):
    sends, recvs, local = _exchange_copies(scatter, srcs, dsts, *sems, arrivals=True)
    for cp in sends:
        cp.wait_send()
    for cp in recvs:
        cp.wait_recv()
    for cp in local:
        cp.wait()


def _exchange_shapes(scatter, srcs):
    return tuple(jax.ShapeDtypeStruct(((N_DEV,) + s.shape[-2:]), s.dtype) for s in srcs)


def _exchange_sems(n):
    return [pltpu.SemaphoreType.DMA((n, 7)), pltpu.SemaphoreType.DMA((n, 7)), pltpu.SemaphoreType.DMA((n,))]


def _exchange_call(scatter, srcs, name):
    n = len(srcs)

    def body(*refs):
        if scatter == "two":
            _two_level_gather_body(n, refs[:n], refs[n:2 * n], *refs[2 * n:])
        else:
            _exchange_start(scatter, refs[:n], refs[n:2 * n], refs[2 * n:])
            _exchange_wait(scatter, refs[:n], refs[n:2 * n], refs[2 * n:])

    return pl.pallas_call(body, name=name, out_shape=_exchange_shapes(scatter, srcs), in_specs=[ANY] * n,
                          out_specs=(ANY,) * n, scratch_shapes=_exchange_sems(n))(*srcs)


def _pcall(body, *, name, grid, in_specs, out_specs, out_shape, scratch_shapes, sem, args, comm=None):
    if comm is None:
        res = pl.pallas_call(body, name=name, grid=grid, in_specs=list(in_specs), out_specs=tuple(out_specs),
                             out_shape=tuple(out_shape), scratch_shapes=list(scratch_shapes),
                             compiler_params=_params(sem))(*args)
        return tuple(res), ()
    scatter, srcs = comm
    n_in, n_out, n_c, n_s = len(in_specs), len(out_specs), len(srcs), len(scratch_shapes)

    def carrier(*refs):
        ins, c_src = refs[:n_in], refs[n_in:n_in + n_c]
        outs = refs[n_in + n_c:n_in + n_c + n_out]
        c_dst = refs[n_in + n_c + n_out:n_in + 2 * n_c + n_out]
        scr = refs[n_in + 2 * n_c + n_out:n_in + 2 * n_c + n_out + n_s]
        first = pl.program_id(0) == 0
        last = pl.program_id(0) == grid[0] - 1
        for ax in range(1, len(grid)):
            first = jnp.logical_and(first, pl.program_id(ax) == 0)
            last = jnp.logical_and(last, pl.program_id(ax) == grid[ax] - 1)

        two_level = scatter == "two"

        @pl.when(first)
        def _():
            if two_level:
                _two_level_gather("start", n_c, c_src, c_dst, *refs[-3:])
            else:
                _exchange_start(scatter, c_src, c_dst, refs[-3:])

        body(*ins, *outs, *scr)

        if two_level:
            @pl.when(pl.program_id(0) == grid[0] // 2)
            def _():
                _two_level_gather("pass", n_c, c_src, c_dst, *refs[-3:])

        @pl.when(last)
        def _():
            if two_level:
                _two_level_gather("finish", n_c, c_src, c_dst, *refs[-3:])
            else:
                _exchange_wait(scatter, c_src, c_dst, refs[-3:])

    res = pl.pallas_call(
        carrier, name=name + "_x", grid=grid, in_specs=list(in_specs) + [ANY] * n_c,
        out_specs=tuple(out_specs) + (ANY,) * n_c, out_shape=tuple(out_shape) + _exchange_shapes(scatter, srcs),
        scratch_shapes=list(scratch_shapes) + _exchange_sems(n_c),
        compiler_params=_params(("arbitrary",) * len(grid)))(*args, *srcs)
    return tuple(res[:n_out]), tuple(res[n_out:])


def _two_level_gather(phase, n_arr, x_refs, out_refs, send_sems, recv_sems, local_sems):
    x, y, c = _me()
    me, sibling = (x, y, c), (x, y, 1 - c)
    chips = [(1 - x, y), (x, 1 - y), (1 - x, 1 - y)]

    def slab(a, px, py, pc):
        return out_refs[a].at[4 * px + 2 * py + pc]

    def copy(a, k, block, to, src=None):
        return pltpu.make_async_remote_copy(
            src_ref=slab(a, *block) if src is None else src, dst_ref=slab(a, *block),
            send_sem=send_sems.at[a, k], recv_sem=recv_sems.at[a, k], device_id=to, device_id_type=MESH)

    def first(a):
        return [copy(a, 0, me, sibling, src=x_refs[a])] + [copy(a, 1 + j, me, (*chip, c), src=x_refs[a])
                                                            for j, chip in enumerate(chips)]

    if phase == "start":
        for a in range(n_arr):
            pltpu.make_async_copy(x_refs[a], slab(a, *me), local_sems.at[a]).start()
        for a in range(n_arr):
            for cp in first(a):
                cp.start()
    elif phase == "pass":
        for j, chip in enumerate(chips):
            for a in range(n_arr):
                copy(a, 1 + j, (*chip, c), me).wait_recv()
                copy(a, 4 + j, (*chip, c), sibling).start()
    else:
        for a in range(n_arr):
            copy(a, 0, sibling, me).wait_recv()
            for j, chip in enumerate(chips):
                copy(a, 4 + j, (*chip, 1 - c), me).wait_recv()
        for a in range(n_arr):
            for cp in first(a) + [copy(a, 4 + j, (*chip, c), sibling) for j, chip in enumerate(chips)]:
                cp.wait_send()
            pltpu.make_async_copy(x_refs[a], slab(a, *me), local_sems.at[a]).wait()


def _two_level_gather_body(n_arr, x_refs, out_refs, send_sems, recv_sems, local_sems):
    for phase in ("start", "pass", "finish"):
        _two_level_gather(phase, n_arr, x_refs, out_refs, send_sems, recv_sems, local_sems)


def _gather_small(x, name):
    def body(x_ref, out_ref, send_sems, recv_sems, local_sems):
        _two_level_gather_body(1, [x_ref], [out_ref], send_sems, recv_sems, local_sems)

    vm = pl.BlockSpec(memory_space=pltpu.VMEM)
    return pl.pallas_call(
        body, name=name,
        out_shape=jax.ShapeDtypeStruct((N_DEV,) + x.shape, x.dtype),
        in_specs=[vm], out_specs=vm,
        scratch_shapes=[pltpu.SemaphoreType.DMA((1, 7)), pltpu.SemaphoreType.DMA((1, 7)),
                        pltpu.SemaphoreType.DMA((1,))],
    )(x)


_SMALL = ["norm_mix_pre", "norm_mix_post", "norm_ffn_pre", "norm_ffn_post", "gla_gate_up", "gla_gate_b",
          "gla_norm", "ssd_conv_w", "ssd_conv_b", "ssd_dt_bias", "ssd_a_log", "ssd_d", "ssd_norm", "ret_norm"]


def _pack(arrs):
    flat = jnp.concatenate([a.reshape(-1) for a in arrs])
    n = flat.shape[0]
    npad = -(-n // 1024) * 1024
    return jnp.pad(flat, (0, npad - n)).reshape(npad // 128, 128)


def _unpack(buf, shapes):
    flat = buf.reshape(-1)
    out, o = [], 0
    for s in shapes:
        n = math.prod(s)
        out.append(flat[o:o + n].reshape(s))
        o += n
    return out


def kernel(x, c, ctx, c_ctx, ada_w, ada_b, norm_mix_pre, norm_mix_post, norm_ffn_pre, norm_ffn_post, w_in, w_out, gla_gate_up, gla_gate_b, gla_norm, ssd_conv_w, ssd_conv_b, ssd_dt_bias, ssd_a_log, ssd_d, ssd_norm, ret_norm, ffn_w13, ffn_w2, loss_target, m_c_ctx, m_ada_w, m_ada_b, m_norm_mix_pre, m_norm_mix_post, m_norm_ffn_pre, m_norm_ffn_post, m_w_in, m_w_out, m_gla_gate_up, m_gla_gate_b, m_gla_norm, m_ssd_conv_w, m_ssd_conv_b, m_ssd_dt_bias, m_ssd_a_log, m_ssd_d, m_ssd_norm, m_ret_norm, m_ffn_w13, m_ffn_w2, v_c_ctx, v_ada_w, v_ada_b, v_norm_mix_pre, v_norm_mix_post, v_norm_ffn_pre, v_norm_ffn_post, v_w_in, v_w_out, v_gla_gate_up, v_gla_gate_b, v_gla_norm, v_ssd_conv_w, v_ssd_conv_b, v_ssd_dt_bias, v_ssd_a_log, v_ssd_d, v_ssd_norm, v_ret_norm, v_ffn_w13, v_ffn_w2):
    P_ = dict(c_ctx=c_ctx, ada_w=ada_w, ada_b=ada_b, norm_mix_pre=norm_mix_pre, norm_mix_post=norm_mix_post,
              norm_ffn_pre=norm_ffn_pre, norm_ffn_post=norm_ffn_post, w_in=w_in, w_out=w_out,
              gla_gate_up=gla_gate_up, gla_gate_b=gla_gate_b, gla_norm=gla_norm, ssd_conv_w=ssd_conv_w,
              ssd_conv_b=ssd_conv_b, ssd_dt_bias=ssd_dt_bias, ssd_a_log=ssd_a_log, ssd_d=ssd_d,
              ssd_norm=ssd_norm, ret_norm=ret_norm, ffn_w13=ffn_w13, ffn_w2=ffn_w2)
    M_ = dict(c_ctx=m_c_ctx, ada_w=m_ada_w, ada_b=m_ada_b, norm_mix_pre=m_norm_mix_pre,
              norm_mix_post=m_norm_mix_post, norm_ffn_pre=m_norm_ffn_pre, norm_ffn_post=m_norm_ffn_post,
              w_in=m_w_in, w_out=m_w_out, gla_gate_up=m_gla_gate_up, gla_gate_b=m_gla_gate_b,
              gla_norm=m_gla_norm, ssd_conv_w=m_ssd_conv_w, ssd_conv_b=m_ssd_conv_b, ssd_dt_bias=m_ssd_dt_bias,
              ssd_a_log=m_ssd_a_log, ssd_d=m_ssd_d, ssd_norm=m_ssd_norm, ret_norm=m_ret_norm,
              ffn_w13=m_ffn_w13, ffn_w2=m_ffn_w2)
    V_ = dict(c_ctx=v_c_ctx, ada_w=v_ada_w, ada_b=v_ada_b, norm_mix_pre=v_norm_mix_pre,
              norm_mix_post=v_norm_mix_post, norm_ffn_pre=v_norm_ffn_pre, norm_ffn_post=v_norm_ffn_post,
              w_in=v_w_in, w_out=v_w_out, gla_gate_up=v_gla_gate_up, gla_gate_b=v_gla_gate_b,
              gla_norm=v_gla_norm, ssd_conv_w=v_ssd_conv_w, ssd_conv_b=v_ssd_conv_b, ssd_dt_bias=v_ssd_dt_bias,
              ssd_a_log=v_ssd_a_log, ssd_d=v_ssd_d, ssd_norm=v_ssd_norm, ret_norm=v_ret_norm,
              ffn_w13=v_ffn_w13, ffn_w2=v_ffn_w2)
    order = ["c_ctx", "ada_w", "ada_b", "norm_mix_pre", "norm_mix_post", "norm_ffn_pre", "norm_ffn_post", "w_in",
             "w_out", "gla_gate_up", "gla_gate_b", "gla_norm", "ssd_conv_w", "ssd_conv_b", "ssd_dt_bias",
             "ssd_a_log", "ssd_d", "ssd_norm", "ret_norm", "ffn_w13", "ffn_w2"]

    mx, my, mc_ = _me()
    me = 4 * mx + 2 * my + mc_
    Tl, Tc = x.shape[1], ctx.shape[1]
    n_in, n_out, n_13, n_2 = w_in.shape[2], w_out.shape[1], ffn_w13.shape[2], ffn_w2.shape[1]
    n_ada = ada_w.shape[2]

    shards = [[w_in[l].T.astype(BF16), w_out[l].astype(BF16), ffn_w13[l].T.astype(BF16), ffn_w2[l].astype(BF16)]
              for l in range(DEPTH)]

    cw = ssd_conv_w.shape[2]
    small_in = jnp.concatenate([jnp.pad(c, ((0, 7), (0, 0))).reshape(-1),
                                ssd_conv_w.reshape(-1)]).reshape(-1, 128)
    n_c_rows = 8 * D // 128
    small_in = jnp.pad(small_in, ((0, -small_in.shape[0] % 8), (0, 0)))
    gathered = _gather_small(small_in, "gather_c_conv")
    c_all = gathered[:, :n_c_rows].reshape(N_DEV, 8, D)[:, 0]
    conv_rows = DEPTH * 5 * cw // 128
    conv_full = gathered[:, n_c_rows:n_c_rows + conv_rows].reshape(N_DEV, DEPTH, 5, cw)
    conv_full = jnp.moveaxis(conv_full, 0, 2).reshape(DEPTH, 5, N_DEV * cw)
    c9 = jnp.concatenate([c_all, c_ctx[None], jnp.zeros((7, D), F32)], axis=0)
    s9 = c9 * jax.nn.sigmoid(c9)
    mod_piece = jnp.concatenate([_mm(s9, ada_w[l], name="mm_mod") for l in range(DEPTH)], axis=0)
    mod_g = _gather_small(mod_piece, "gather_mod")
    mod_all = jnp.moveaxis(mod_g.reshape(N_DEV, DEPTH, 16, n_ada), 0, 2).reshape(DEPTH, 16, N_DEV * n_ada)
    mod_all = mod_all + ada_b[:, None, :]
    mod_l = lax.dynamic_index_in_dim(mod_all, me, axis=1, keepdims=False)
    mod_c = mod_all[:, 8]

    sp = {n: P_[n] for n in _SMALL}
    sp["ssd_conv_w"] = conv_full
    xcat = jnp.concatenate([ctx[0], x[0]], axis=0)
    loss_local, d_xcat, d_mod_l, d_mod_c, d_sp, gw = _local_step(xcat, loss_target[0], mod_l, mod_c, sp, Tc,
                                                                 shards=shards)
    loss = lax.psum(loss_local, ("x", "y", "c"))
    grad_x = d_xcat[Tc:][None]

    G = {n: jnp.stack([_sum8_call(gw[l][a], f"sum_{n}") for l in range(DEPTH)])
         for a, n in enumerate(["w_in", "w_out", "ffn_w13", "ffn_w2"])}
    G["w_in"], G["ffn_w13"] = jnp.swapaxes(G["w_in"], 1, 2), jnp.swapaxes(G["ffn_w13"], 1, 2)

    dmod_rows = jnp.concatenate([d_mod_l, d_mod_c], axis=0)
    dmod_g = _gather_small(dmod_rows, "gather_dmod").reshape(N_DEV, 2, DEPTH, 6 * D)
    dl = jnp.moveaxis(dmod_g[:, 0], 0, 1)
    dc = dmod_g[:, 1, :, :]
    dc_tot = dc[0]
    for d_ in range(1, N_DEV):
        dc_tot = dc_tot + dc[d_]
    dmod9 = jnp.concatenate([dl, dc_tot[:, None, :], jnp.zeros((DEPTH, 7, 6 * D), F32)], axis=1)
    g_ada_b = dmod9[:, 0]
    for r_ in range(1, 9):
        g_ada_b = g_ada_b + dmod9[:, r_]
    dmod9_mine = lax.dynamic_slice_in_dim(dmod9, me * n_ada, n_ada, axis=2)
    s9T = jnp.pad(s9.T, ((0, 0), (0, 112)))
    g_ada_w = jnp.stack([_mm(s9T, jnp.pad(dmod9_mine[l], ((0, 112), (0, 0))), name="mm_dada")
                         for l in range(DEPTH)])
    ds9 = _mm(dmod9_mine[0], ada_w[0], trans_b=True, name="mm_ds9")
    for l in range(1, DEPTH):
        ds9 = _mm(dmod9_mine[l], ada_w[l], trans_b=True, name="mm_ds9_acc", add=ds9)
    ds_ctx_part = ds9[8]

    small_names = [n for n in _SMALL]
    small_parts = [d_sp[n] for n in small_names] + [ds_ctx_part]
    packed = _pack(small_parts)
    allp = _gather_small(packed, "gather_small_grads")
    summed = _sum_call([allp[d_] for d_ in range(N_DEV)], "sum_small_grads")
    parts = _unpack(summed, [p.shape for p in small_parts])
    for n, p in zip(small_names, parts[:-1]):
        G[n] = p
    sig = jax.nn.sigmoid(c_ctx)
    G["c_ctx"] = parts[-1] * (sig * (1.0 + c_ctx * (1.0 - sig)))
    G["ssd_conv_w"] = lax.dynamic_slice_in_dim(G["ssd_conv_w"], me * cw, cw, axis=2)
    G["ada_w"] = g_ada_w
    G["ada_b"] = g_ada_b

    delta, new_m, new_v = {}, {}, {}
    for n in ["ada_w", "w_in", "w_out", "ffn_w13", "ffn_w2"]:
        sh = P_[n].shape
        f2 = lambda a: a.reshape(sh[0] * sh[1], sh[2])
        d_, m_, v_ = _adamw_call(f2(P_[n]), f2(G[n]), f2(M_[n]), f2(V_[n]), f"adamw_{n}")
        delta[n], new_m[n], new_v[n] = d_.reshape(sh), m_.reshape(sh), v_.reshape(sh)
    rest = [n for n in order if n not in delta]
    shapes = [P_[n].shape for n in rest]
    d_, m_, v_ = _adamw_call(_pack([P_[n] for n in rest]), _pack([G[n] for n in rest]),
                             _pack([M_[n] for n in rest]), _pack([V_[n] for n in rest]), "adamw_small")
    for n, a, b, e in zip(rest, _unpack(d_, shapes), _unpack(m_, shapes), _unpack(v_, shapes)):
        delta[n], new_m[n], new_v[n] = a, b, e

    return (loss, grad_x, *[G[n] for n in order], *[delta[n] for n in order],
            *[new_m[n] for n in order], *[new_v[n] for n in order])
```

```python
import math

import jax
import jax.numpy as jnp
from jax import lax
from jax.experimental import pallas as pl
from jax.experimental.pallas import tpu as pltpu

F32 = jnp.float32
BF16 = jnp.bfloat16

D = 1024
DEPTH = 4
GRID_W = 64
RMS_EPS = 1e-6
GLA_TAU = 16.0
FFN_H = 2816
N_DEV = 8
ADAM_LR, ADAM_B1, ADAM_B2, ADAM_EPS, ADAM_WD, ADAM_STEP = 0.001, 0.9, 0.999, 1e-08, 0.01, 10

VMEM_LIMIT = 48 * 1024 * 1024

_ORIG = dict(gla_q=(0, 128), gla_k=(128, 128), gla_v=(256, 256), gla_r=(512, 256), gla_lr=(768, 32),
             ssd_z=(800, 512), ssd_xbc=(1312, 1024), ssd_dt=(2336, 16), ret_q=(2352, 256), ret_k=(2608, 256),
             ret_v=(2864, 256), ret_g=(3120, 256))
_R_ORDER = ["gla_v", "gla_r", "ret_q", "ret_k", "ret_v", "ret_g", "ssd_z", "gla_q", "gla_k", "gla_lr", "ssd_dt"]
R_W = 2560
_ROFF = {}
_o = 0
for _n in _R_ORDER:
    _ROFF[_n] = _o
    _o += _ORIG[_n][1]
MISC = _ROFF["gla_lr"]
assert MISC == 2304 and _o == 2352


def _split_w_in(wt):
    xs, xz = _ORIG["ssd_xbc"]
    parts = [wt[_ORIG[n][0]:_ORIG[n][0] + _ORIG[n][1]] for n in _R_ORDER]
    parts.append(jnp.zeros((R_W - _o,) + wt.shape[1:], wt.dtype))
    return wt[xs:xs + xz], jnp.concatenate(parts, axis=0)


def _merge_w_in(wx, wr):
    pieces = []
    for n, (s, z) in sorted(_ORIG.items(), key=lambda t: t[1][0]):
        pieces.append(wx if n == "ssd_xbc" else wr[_ROFF[n]:_ROFF[n] + z])
    return jnp.concatenate(pieces, axis=0)


def _pick(n, cands):
    for c in cands:
        if n % c == 0:
            return c
    return n


def _params(sem=None):
    kw = dict(vmem_limit_bytes=VMEM_LIMIT)
    if sem is not None:
        kw["dimension_semantics"] = sem
    return pltpu.CompilerParams(**kw)


def _iota(shape, dim):
    return lax.broadcasted_iota(jnp.int32, shape, dim)


def _dot(a, b, dims):
    return lax.dot_general(a, b, (dims, ((), ())), preferred_element_type=F32)


_NN = ((1,), (0,))
_NT = ((1,), (1,))
_TN = ((0,), (0,))


def _bf(x):
    return x.astype(BF16)


def _dot_sel(x, e, dims, x_left=True):
    eb = e.astype(BF16)
    hi = x.astype(BF16)
    r1 = x - hi.astype(F32)
    mid = r1.astype(BF16)
    lo = (r1 - mid.astype(F32)).astype(BF16)
    out = None
    for p in (hi, mid, lo):
        t = _dot(p, eb, dims) if x_left else _dot(eb, p, dims)
        out = t if out is None else out + t
    return out


@jax.custom_vjp
def _sel(x, e):
    return _dot_sel(x, e, _NN)


_sel.defvjp(lambda x, e: (_dot_sel(x, e, _NN), e), lambda e, g: (_dot_sel(g, e, _NT), jnp.zeros_like(e)))


def _sig(x):
    e = jnp.exp(-jnp.abs(x))
    return jnp.where(x >= 0, 1.0 / (1.0 + e), e / (1.0 + e))


@jax.custom_vjp
def _sigmoid(x):
    return _sig(x)


def _sigmoid_fwd(x):
    s = _sig(x)
    return s, s


_sigmoid.defvjp(_sigmoid_fwd, lambda s, g: (g * s * (1.0 - s),))


def _silu(x):
    return x * _sigmoid(x)


@jax.custom_vjp
def _softplus(x):
    return jnp.maximum(x, 0.0) + jnp.log(1.0 + jnp.exp(-jnp.abs(x)))


_softplus.defvjp(lambda x: (jnp.maximum(x, 0.0) + jnp.log(1.0 + jnp.exp(-jnp.abs(x))), x),
                 lambda x, g: (g * _sig(x),))


def _log_sigmoid(x):
    return -_softplus(-x)


@jax.custom_vjp
def _mm_bf(x, w):
    return _dot(_bf(x), _bf(w), _NN)


_mm_bf.defvjp(lambda x, w: (_dot(_bf(x), _bf(w), _NN), (x, w)),
              lambda r, g: (_dot(_bf(g), _bf(r[1]), _NT), _dot(_bf(r[0]), _bf(g), _TN)))


_TILE_M = (1088, 1024, 512, 256, 128, 64, 32, 16)
_TILE_N = (1408, 1280, 1024, 768, 512, 384, 256, 128)
_TILE_K = (1408, 1280, 1024, 768, 512, 384, 256, 128)


def _mm(a, b, *, trans_b=False, name, add=None, out_dtype=F32):
    M, K = a.shape
    N = b.shape[0] if trans_b else b.shape[1]
    assert (b.shape[1] if trans_b else b.shape[0]) == K
    tm, tn, tk = _pick(M, _TILE_M), _pick(N, _TILE_N), _pick(K, _TILE_K)
    nk = K // tk
    dims = _NT if trans_b else _NN
    has_add = add is not None

    def body(*refs):
        a_ref, b_ref = refs[0], refs[1]
        o_ref, acc_ref = refs[-2], refs[-1]
        k = pl.program_id(2)

        @pl.when(k == 0)
        def _():
            acc_ref[...] = refs[2][...] if has_add else jnp.zeros_like(acc_ref)

        acc_ref[...] += _dot(a_ref[...].astype(BF16), b_ref[...].astype(BF16), dims)

        @pl.when(k == nk - 1)
        def _():
            o_ref[...] = acc_ref[...].astype(o_ref.dtype)

    b_spec = (pl.BlockSpec((tn, tk), lambda i, j, k: (j, k)) if trans_b
              else pl.BlockSpec((tk, tn), lambda i, j, k: (k, j)))
    o_spec = pl.BlockSpec((tm, tn), lambda i, j, k: (i, j))
    return pl.pallas_call(
        body, name=name,
        out_shape=jax.ShapeDtypeStruct((M, N), out_dtype),
        grid=(M // tm, N // tn, nk),
        in_specs=[pl.BlockSpec((tm, tk), lambda i, j, k: (i, k)), b_spec] + ([o_spec] if has_add else []),
        out_specs=o_spec,
        scratch_shapes=[pltpu.VMEM((tm, tn), F32)],
        compiler_params=_params(("parallel", "parallel", "arbitrary")),
    )(*((a, b, add) if has_add else (a, b)))


def _mm_tn(a, g, *, name, out_dtype=F32):
    M, K = a.shape
    N = g.shape[1]
    tm, tk, tn = _pick(M, _TILE_M), _pick(K, _TILE_K), _pick(N, _TILE_N)
    nm = M // tm

    def body(a_ref, g_ref, o_ref, acc_ref):
        i = pl.program_id(2)

        @pl.when(i == 0)
        def _():
            acc_ref[...] = jnp.zeros_like(acc_ref)

        acc_ref[...] += _dot(a_ref[...].astype(BF16), g_ref[...].astype(BF16), _TN)

        @pl.when(i == nm - 1)
        def _():
            o_ref[...] = acc_ref[...].astype(o_ref.dtype)

    return pl.pallas_call(
        body, name=name,
        out_shape=jax.ShapeDtypeStruct((K, N), out_dtype),
        grid=(K // tk, N // tn, nm),
        in_specs=[pl.BlockSpec((tm, tk), lambda k, j, i: (i, k)), pl.BlockSpec((tm, tn), lambda k, j, i: (i, j))],
        out_specs=pl.BlockSpec((tk, tn), lambda k, j, i: (k, j)),
        scratch_shapes=[pltpu.VMEM((tk, tn), F32)],
        compiler_params=_params(("parallel", "parallel", "arbitrary")),
    )(a, g)


def _norm_fwd_call(x, w, a2, b2, res, tr, out_dtype=F32):
    T, W = x.shape
    has_res = res is not None

    def body(*refs):
        x_ref, w_ref, a_ref, b_ref = refs[:4]
        y_ref = refs[-1]
        seg = jnp.minimum(pl.program_id(0), 1)
        xv = x_ref[...]
        rstd = lax.rsqrt(jnp.mean(xv * xv, axis=-1, keepdims=True) + RMS_EPS)
        y = a_ref[pl.ds(seg, 1), :] * (xv * rstd * w_ref[...]) + b_ref[pl.ds(seg, 1), :]
        y_ref[...] = (y + refs[4][...] if has_res else y).astype(y_ref.dtype)

    row = pl.BlockSpec((tr, W), lambda i: (i, 0))
    small = pl.BlockSpec((8, W), lambda i: (0, 0))
    return pl.pallas_call(
        body, name="norm_fwd",
        out_shape=jax.ShapeDtypeStruct((T, W), out_dtype),
        grid=(T // tr,),
        in_specs=[row, pl.BlockSpec((1, W), lambda i: (0, 0)), small, small] + ([row] if has_res else []),
        out_specs=row,
        compiler_params=_params(("parallel",)),
    )(*((x, w.reshape(1, W), a2, b2) + ((res,) if has_res else ())))


def _norm_bwd_call(x, w, a2, dy, tr, add=None, out_dtype=F32):
    T, W = x.shape
    has_add = add is not None

    def body(*refs):
        x_ref, w_ref, a_ref, dy_ref = refs[:4]
        dx_ref, dw_ref, da_ref, db_ref = refs[-4:]
        i = pl.program_id(0)
        seg = jnp.minimum(i, 1)

        @pl.when(i == 0)
        def _():
            dw_ref[...] = jnp.zeros_like(dw_ref)
            da_ref[...] = jnp.zeros_like(da_ref)
            db_ref[...] = jnp.zeros_like(db_ref)

        xv = x_ref[...]
        g = dy_ref[...]
        wv = w_ref[...]
        rstd = lax.rsqrt(jnp.mean(xv * xv, axis=-1, keepdims=True) + RMS_EPS)
        xh = xv * rstd
        da_ref[pl.ds(seg, 1), :] += jnp.sum(g * (xh * wv), axis=0, keepdims=True)
        db_ref[pl.ds(seg, 1), :] += jnp.sum(g, axis=0, keepdims=True)
        gy = g * a_ref[pl.ds(seg, 1), :]
        dw_ref[0:1, :] += jnp.sum(gy * xh, axis=0, keepdims=True)
        gx = gy * wv
        dx = rstd * (gx - xh * jnp.mean(gx * xh, axis=-1, keepdims=True))
        dx_ref[...] = (dx + refs[4][...] if has_add else dx).astype(dx_ref.dtype)

    acc = jax.ShapeDtypeStruct((8, W), F32)
    acc_spec = pl.BlockSpec((8, W), lambda i: (0, 0))
    row = pl.BlockSpec((tr, W), lambda i: (i, 0))
    return pl.pallas_call(
        body, name="norm_bwd",
        out_shape=(jax.ShapeDtypeStruct((T, W), out_dtype), acc, acc, acc),
        grid=(T // tr,),
        in_specs=[row, pl.BlockSpec((1, W), lambda i: (0, 0)), acc_spec, row] + ([row] if has_add else []),
        out_specs=(row, acc_spec, acc_spec, acc_spec),
        compiler_params=_params(("arbitrary",)),
    )(*((x, w.reshape(1, W), a2, dy) + ((add,) if has_add else ())))


def _act_call(u1, u2, dact=None):
    T, W = u1.shape
    tr = _pick(T, (512, 256, 128, 64))
    tn = _pick(W, (1408, 512, 256, 128))
    spec = pl.BlockSpec((tr, tn), lambda i, j: (i, j))
    sh = jax.ShapeDtypeStruct((T, W), BF16)
    if dact is None:
        def body(a_ref, b_ref, o_ref):
            a = a_ref[...].astype(F32)
            o_ref[...] = (a * _sig(a) * b_ref[...].astype(F32)).astype(o_ref.dtype)

        return pl.pallas_call(body, name="act_fwd", out_shape=sh, grid=(T // tr, W // tn), in_specs=[spec, spec],
                              out_specs=spec, compiler_params=_params(("parallel", "parallel")))(u1, u2)

    def body(a_ref, b_ref, g_ref, da_ref, db_ref):
        a, g = a_ref[...].astype(F32), g_ref[...].astype(F32)
        s = _sig(a)
        da_ref[...] = (g * b_ref[...].astype(F32) * (s * (1.0 + a * (1.0 - s)))).astype(da_ref.dtype)
        db_ref[...] = (g * a * s).astype(db_ref.dtype)

    return pl.pallas_call(body, name="act_bwd", out_shape=(sh, sh), grid=(T // tr, W // tn),
                          in_specs=[spec, spec, spec], out_specs=(spec, spec),
                          compiler_params=_params(("parallel", "parallel")))(u1, u2, dact)


def _conv_specs(T, Wc, tr):
    hb, nt = tr // 8, T // tr
    row = pl.BlockSpec((tr, Wc), lambda i: (i, 0))
    prev = pl.BlockSpec((8, Wc), lambda i: (jnp.maximum(i * hb - 1, 0), 0))
    nxt = pl.BlockSpec((8, Wc), lambda i: (jnp.minimum((i + 1) * hb, T // 8 - 1), 0))
    return row, prev, nxt, nt


def _fill_ext(dst_ref, cur_ref, prev_ref, next_ref, i, nt, tr):
    has_prev = (i > 1).astype(F32)
    has_next = jnp.logical_and(i > 0, i < nt - 1).astype(F32)
    dst_ref[8:16, :] = prev_ref[...] * has_prev
    dst_ref[16:16 + tr, :] = cur_ref[...]
    dst_ref[16 + tr:24 + tr, :] = next_ref[...] * has_next


def _conv_fwd_call(px, w8, b, tr):
    T, Wc = px.shape
    row, prev, nxt, nt = _conv_specs(T, Wc, tr)

    def body(x_ref, xp_ref, xn_ref, w_ref, b_ref, u_ref, xe_ref):
        i = pl.program_id(0)

        @pl.when(i == 0)
        def _():
            xe_ref[...] = jnp.zeros_like(xe_ref)

        _fill_ext(xe_ref, x_ref, xp_ref, xn_ref, i, nt, tr)
        y = b_ref[...] + w_ref[0:1, :] * xe_ref[pl.ds(14, tr), :]
        for k in range(1, 5):
            y = y + w_ref[k:k + 1, :] * xe_ref[pl.ds(14 + k, tr), :]
        u_ref[...] = y * _sig(y)

    return pl.pallas_call(
        body, name="conv_fwd", out_shape=jax.ShapeDtypeStruct((T, Wc), F32), grid=(nt,),
        in_specs=[row, prev, nxt, pl.BlockSpec((8, Wc), lambda i: (0, 0)), pl.BlockSpec((1, Wc), lambda i: (0, 0))],
        out_specs=row, scratch_shapes=[pltpu.VMEM((tr + 32, Wc), F32)],
        compiler_params=_params(("arbitrary",)),
    )(px, px, px, w8, b)


def _conv_bwd_call(px, w8, b, du, tr):
    T, Wc = px.shape
    row, prev, nxt, nt = _conv_specs(T, Wc, tr)
    E = tr + 16

    def body(x_ref, xp_ref, xn_ref, g_ref, gp_ref, gn_ref, w_ref, b_ref, dx_ref, dw_ref, db_ref,
             xe_ref, ge_ref, dy_ref):
        i = pl.program_id(0)

        @pl.when(i == 0)
        def _():
            xe_ref[...] = jnp.zeros_like(xe_ref)
            ge_ref[...] = jnp.zeros_like(ge_ref)
            dy_ref[...] = jnp.zeros_like(dy_ref)
            dw_ref[...] = jnp.zeros_like(dw_ref)
            db_ref[...] = jnp.zeros_like(db_ref)

        _fill_ext(xe_ref, x_ref, xp_ref, xn_ref, i, nt, tr)
        _fill_ext(ge_ref, g_ref, gp_ref, gn_ref, i, nt, tr)
        y = b_ref[...] + w_ref[0:1, :] * xe_ref[pl.ds(6, E), :]
        for k in range(1, 5):
            y = y + w_ref[k:k + 1, :] * xe_ref[pl.ds(6 + k, E), :]
        s = _sig(y)
        dy = ge_ref[pl.ds(8, E), :] * (s * (1.0 + y * (1.0 - s)))
        dy_ref[pl.ds(8, E), :] = dy
        dx = w_ref[0:1, :] * dy_ref[pl.ds(18, tr), :]
        for k in range(1, 5):
            dx = dx + w_ref[k:k + 1, :] * dy_ref[pl.ds(18 - k, tr), :]
        dx_ref[...] = dx.astype(dx_ref.dtype)
        dyt = dy_ref[pl.ds(16, tr), :]
        db_ref[0:1, :] += jnp.sum(dyt, axis=0, keepdims=True)
        for k in range(5):
            dw_ref[k:k + 1, :] += jnp.sum(dyt * xe_ref[pl.ds(14 + k, tr), :], axis=0, keepdims=True)

    acc = jax.ShapeDtypeStruct((8, Wc), F32)
    acc_spec = pl.BlockSpec((8, Wc), lambda i: (0, 0))
    ext = pltpu.VMEM((tr + 32, Wc), F32)
    return pl.pallas_call(
        body, name="conv_bwd", out_shape=(jax.ShapeDtypeStruct((T, Wc), BF16), acc, acc), grid=(nt,),
        in_specs=[row, prev, nxt, row, prev, nxt, acc_spec, pl.BlockSpec((1, Wc), lambda i: (0, 0))],
        out_specs=(row, acc_spec, acc_spec), scratch_shapes=[ext, ext, ext],
        compiler_params=_params(("arbitrary",)),
    )(px, px, px, du, du, du, w8, b)


_SCAN_CFG = {
    "gla": dict(H=4, Dk=32, Dv=64, nh=4, scalar=False, C=128),
    "ssd": dict(H=8, Dk=128, Dv=64, nh=2, scalar=True, C=128),
    "ret": dict(H=4, Dk=64, Dv=64, nh=4, scalar=True, C=128),
}
GPAD = 8


def _log2(n):
    r = int(math.log2(n))
    assert 1 << r == n
    return r


class _ScanMath:
    def __init__(self, cfg, reverse):
        C = cfg["C"]
        self.C, self.reverse = C, reverse
        self.Dk, self.Dv, self.nh, self.scalar = cfg["Dk"], cfg["Dv"], cfg["nh"], cfg["scalar"]
        self.Wk, self.Wv = self.nh * self.Dk, self.nh * self.Dv
        self.nsg = cfg["H"] // self.nh
        nh, Wk, Wv = self.nh, self.Wk, self.Wv
        lk, lv, lc = _log2(self.Dk), _log2(self.Dv), _log2(C)
        r, c = _iota((C, C), 0), _iota((C, C), 1)
        self.L = ((c >= r) if reverse else (c <= r)).astype(F32)
        self.Lsuf = ((c <= r) if reverse else (c >= r)).astype(F32)
        i, j = _iota((C, nh * C), 0), _iota((C, nh * C), 1) & (C - 1)
        self.Mst = (j >= i) if reverse else (j <= i)
        self.Dj = (i == j).astype(F32)
        self.nb = 1 if (self.scalar or C == 64) else 3
        assert self.scalar or C in (64, 128)
        lanes = _iota((1, self.nb * Wk), 1) & (Wk - 1)
        self.km = [((lanes >> lk) == h).astype(F32) for h in range(nh)]
        self.vm = [((_iota((1, Wv), 1) >> lv) == h).astype(F32) for h in range(nh)]
        self.BD = ((_iota((Wv, Wk), 0) >> lv) == (_iota((Wv, Wk), 1) >> lk)).astype(F32)
        self.last = 0 if reverse else C - 1
        self.last_row = (_iota((C, 1), 0) == self.last).astype(F32)
        self.lk, self.lc = lk, lc
        self.H = cfg["H"]

    def gates(self, g):
        if not self.scalar:
            return _dot_sel(g, self.L, _NN, x_left=False), None
        G8 = _dot_sel(g, self.L, _NN, x_left=False)
        nk, ncol = self.H * self.Dk, self.H * self.C
        ek = (_iota((GPAD, nk), 0) == (_iota((GPAD, nk), 1) >> self.lk)).astype(F32)
        ec = (_iota((GPAD, ncol), 0) == (_iota((GPAD, ncol), 1) >> self.lc)).astype(F32)
        return _dot_sel(G8, ek, _NN), _dot_sel(G8, ec, _NN)

    def Ek(self, s):
        return (_iota((GPAD, self.Wk), 0) == (_iota((GPAD, self.Wk), 1) >> self.lk) + s * self.nh).astype(F32)

    def fold(self, x, factors=None):
        Wk = self.Wk
        out = None
        for b in range(self.nb):
            t = x[:, b * Wk:(b + 1) * Wk]
            t = t if factors is None or factors[b] is None else t * factors[b]
            out = t if out is None else out + t
        return out

    def kstack(self, x):
        return jnp.concatenate([x * self.km[h] for h in range(self.nh)], axis=0)

    def vstack(self, x):
        return jnp.concatenate([x * self.vm[h] for h in range(self.nh)], axis=0)

    def unstack(self, R, masks):
        C = self.C
        out = R[0:C] * masks[0]
        for h in range(1, self.nh):
            out = out + R[h * C:(h + 1) * C] * masks[h]
        return out

    def chunk(self, qs, ks, Gk, Gc):
        C = self.C
        Glast = Gk[self.last:self.last + 1, :]
        out = dict(Gk=Gk, Glast=Glast, eG=jnp.exp(Gk), eGl=jnp.exp(Glast - Gk), eGlast=jnp.exp(Glast))
        if self.scalar:
            Gr = jnp.sum(Gc * self.Dj, axis=0, keepdims=True)
            dec = jnp.where(self.Mst, jnp.exp(jnp.minimum(Gc - Gr, 0.0)), 0.0)
            qt, kt = qs, ks
            A = _dot(_bf(qt), _bf(self.kstack(kt)), _NT) * dec
            out.update(dec=dec, qt=qt, kt=kt, A=A, fq=[None], fk=[None])
        elif self.nb == 1:
            Gm = Gk[C // 2:C // 2 + 1, :]
            fq, fk = [jnp.exp(Gk - Gm)], [jnp.exp(Gm - Gk)]
            qt, kt = qs * fq[0], ks * fk[0]
            A = jnp.where(self.Mst, _dot(_bf(qt), _bf(self.kstack(kt)), _NT), 0.0)
            out.update(fq=fq, fk=fk, qt=qt, kt=kt, A=A)
        else:
            h = C // 2
            rows = _iota((C, 1), 0)
            early = (rows >= h) if self.reverse else (rows < h)
            late = jnp.logical_not(early)
            m_e, m_l, b = (h + h // 2, h // 2, h) if self.reverse else (h // 2, h + h // 2, h - 1)
            Ge, Gl, Gb = Gk[m_e:m_e + 1, :], Gk[m_l:m_l + 1, :], Gk[b:b + 1, :]

            def factor(mask, arg):
                return jnp.where(mask, jnp.exp(jnp.where(mask, arg, 0.0)), 0.0)

            fq = [factor(early, Gk - Ge), factor(late, Gk - Gl), factor(late, Gk - Gb)]
            fk = [factor(early, Ge - Gk), factor(late, Gl - Gk), factor(early, Gb - Gk)]
            qt = jnp.concatenate([qs * f for f in fq], axis=1)
            kt = jnp.concatenate([ks * f for f in fk], axis=1)
            A = jnp.where(self.Mst, _dot(_bf(qt), _bf(self.kstack(kt)), _NT), 0.0)
            out.update(fq=fq, fk=fk, qt=qt, kt=kt, A=A)
        return out


def _chunk_index(p, n, nc, reverse):
    if not reverse:
        return p
    return jnp.where(p < nc, nc - 1 - p, n - 1 + nc - p)


def _scan_dims(kind):
    cfg = _SCAN_CFG[kind]
    HK, HV = cfg["H"] * cfg["Dk"], cfg["H"] * cfg["Dv"]
    return cfg, cfg["C"], HK, HV, (GPAD if cfg["scalar"] else HK)


def _scan_fwd_step(m, q_ref, k_ref, v_ref, g_ref, o_ref, st_ref, S_ref):
    C = m.C

    @pl.when(pl.program_id(0) == 0)
    def _():
        S_ref[...] = jnp.zeros_like(S_ref)

    Gk_all, Gc_all = m.gates(g_ref[...])
    for s in range(m.nsg):
        ksl, vsl = slice(s * m.Wk, (s + 1) * m.Wk), slice(s * m.Wv, (s + 1) * m.Wv)
        csl = slice(s * m.nh * C, (s + 1) * m.nh * C)
        qs, ks, vs = q_ref[:, ksl], k_ref[:, ksl], v_ref[:, vsl]
        ch = m.chunk(qs, ks, Gk_all[:, ksl], Gc_all[:, csl] if m.scalar else None)
        S = S_ref[vsl, :]
        o = _dot(_bf(ch["A"]), _bf(m.vstack(vs)), _NN) + _dot(_bf(qs * ch["eG"]), _bf(S), _NT)
        o_ref[:, vsl] = o
        st_ref[0, vsl, :] = S
        S_ref[vsl, :] = S * ch["eGlast"] + _dot(_bf(vs), _bf(ks * ch["eGl"]), _TN) * m.BD


def _scan_fwd_call(kind, ops, Tc, comm=None):
    cfg, C, HK, HV, GW = _scan_dims(kind)
    T = ops[False][0][0].shape[0]
    n, nc = T // C, Tc // C

    def body(*refs):
        for d, rev in enumerate((False, True)):
            _scan_fwd_step(_ScanMath(cfg, rev), *refs[4 * d:4 * d + 4], *refs[8 + 2 * d:10 + 2 * d], refs[12 + d])

    sg = cfg["H"] // cfg["nh"]
    Wk, Wv = cfg["nh"] * cfg["Dk"], cfg["nh"] * cfg["Dv"]
    col = lambda rev, w, j: pl.BlockSpec((C, w), lambda p: (_chunk_index(p, n, nc, rev), j))
    st_spec = lambda rev: pl.BlockSpec((1, sg * Wv, Wk), lambda p: (_chunk_index(p, n, nc, rev), 0, 0))
    in_specs, args, out_specs, out_shape = [], [], [], []
    for rev in (False, True):
        q, k, v, g = ops[rev]
        in_specs += [col(rev, HK, q[1]), col(rev, HK, k[1]), col(rev, HV, v[1]), col(rev, GW, g[1])]
        args += [q[0], k[0], v[0], g[0]]
        out_specs += [col(rev, HV, 0), st_spec(rev)]
        out_shape += [jax.ShapeDtypeStruct((T, HV), F32), jax.ShapeDtypeStruct((n, sg * Wv, Wk), F32)]
    res, got = _pcall(body, name=f"scan_fwd_{kind}", out_shape=out_shape, grid=(n,), in_specs=in_specs,
                      out_specs=out_specs, scratch_shapes=[pltpu.VMEM((sg * Wv, Wk), F32)] * 2,
                      sem=("arbitrary",), args=args, comm=comm)
    return {False: (res[0], res[1]), True: (res[2], res[3])}, got


def _scan_bwd_step(m, need_dg, q_ref, k_ref, v_ref, g_ref, st_ref, do_ref, dq_ref, dk_ref, dv_ref, dg_ref, dS_ref):
    C = m.C

    @pl.when(pl.program_id(0) == 0)
    def _():
        dS_ref[...] = jnp.zeros_like(dS_ref)

    x8 = jnp.zeros((C, GPAD), F32)
    Gk_all, Gc_all = m.gates(g_ref[...])
    for s in range(m.nsg):
        ksl, vsl = slice(s * m.Wk, (s + 1) * m.Wk), slice(s * m.Wv, (s + 1) * m.Wv)
        csl = slice(s * m.nh * C, (s + 1) * m.nh * C)
        qs, ks, vs, dos = q_ref[:, ksl], k_ref[:, ksl], v_ref[:, vsl], do_ref[:, vsl]
        ch = m.chunk(qs, ks, Gk_all[:, ksl], Gc_all[:, csl] if m.scalar else None)
        S = st_ref[0, vsl, :]
        dS = dS_ref[vsl, :]
        A, qt, kt = ch["A"], ch["qt"], ch["kt"]
        dA = _dot(_bf(dos), _bf(m.vstack(vs)), _NT)
        dAm = dA * ch["dec"] if m.scalar else jnp.where(m.Mst, dA, 0.0)
        kst = _bf(m.kstack(kt))
        dv = m.unstack(_dot(_bf(A), _bf(dos), _TN), m.vm) + _dot(_bf(ks * ch["eGl"]), _bf(dS), _NT)
        dv_ref[:, vsl] = dv
        dq_i = _dot(_bf(dAm), kst, _NN)
        dq_x = ch["eG"] * _dot(_bf(dos), _bf(S), _NN)
        dq_ref[:, ksl] = m.fold(dq_i, ch["fq"]) + dq_x
        dk_i = m.unstack(_dot(_bf(dAm), _bf(qt), _TN), m.km)
        dk_x = ch["eGl"] * _dot(_bf(vs), _bf(dS), _NN)
        dk_ref[:, ksl] = m.fold(dk_i, ch["fk"]) + dk_x
        if need_dg:
            bnd = (ch["eGlast"] * jnp.sum(dS * S, axis=0, keepdims=True)
                   + jnp.sum(ks * dk_x, axis=0, keepdims=True))
            X = m.fold(_bf(qt).astype(F32) * dq_i - _bf(kt).astype(F32) * dk_i) + (qs * dq_x - ks * dk_x)
            X = X + m.last_row * bnd
            if m.scalar:
                x8 = x8 + _dot_sel(X, m.Ek(s), _NT)
            else:
                dg_ref[:, ksl] = _dot_sel(X, m.Lsuf, _NN, x_left=False)
        dS_ref[vsl, :] = dS * ch["eGlast"] + _dot(_bf(dos), _bf(qs * ch["eG"]), _TN) * m.BD
    if m.scalar:
        dg_ref[...] = _dot_sel(x8, m.Lsuf, _NN, x_left=False)
    elif not need_dg:
        dg_ref[...] = jnp.zeros_like(dg_ref)


def _scan_bwd_call(kind, need_dg, ops, st, do, Tc, comm=None):
    cfg, C, HK, HV, GW = _scan_dims(kind)
    T = ops[False][0][0].shape[0]
    n, nc = T // C, Tc // C

    def body(*refs):
        for d, rev in enumerate((False, True)):
            _scan_bwd_step(_ScanMath(cfg, rev), need_dg, *refs[6 * d:6 * d + 6], *refs[12 + 4 * d:16 + 4 * d],
                           refs[20 + d])

    sg = cfg["H"] // cfg["nh"]
    Wk, Wv = cfg["nh"] * cfg["Dk"], cfg["nh"] * cfg["Dv"]
    col = lambda rev, w, j: pl.BlockSpec((C, w), lambda p: (_chunk_index(n - 1 - p, n, nc, rev), j))
    st_spec = lambda rev: pl.BlockSpec((1, sg * Wv, Wk), lambda p: (_chunk_index(n - 1 - p, n, nc, rev), 0, 0))
    in_specs, args, out_specs, out_shape = [], [], [], []
    for rev in (False, True):
        q, k, v, g = ops[rev]
        in_specs += [col(rev, HK, q[1]), col(rev, HK, k[1]), col(rev, HV, v[1]), col(rev, GW, g[1]),
                     st_spec(rev), col(rev, HV, 0)]
        args += [q[0], k[0], v[0], g[0], st[rev], do]
        out_specs += [col(rev, HK, 0), col(rev, HK, 0), col(rev, HV, 0), col(rev, GW, 0)]
        out_shape += [jax.ShapeDtypeStruct((T, w), F32) for w in (HK, HK, HV, GW)]
    res, got = _pcall(body, name=f"scan_bwd_{kind}", out_shape=out_shape, grid=(n,), in_specs=in_specs,
                      out_specs=out_specs, scratch_shapes=[pltpu.VMEM((sg * Wv, Wk), F32)] * 2,
                      sem=("arbitrary",), args=args, comm=comm)
    return {False: res[0:4], True: res[4:8]}, got


def _prep_consts():
    r, c = _iota((256, 256), 0), _iota((256, 256), 1)
    first = (c & 63) < 32
    rope_perm = jnp.where(first, -(r == c + 32).astype(F32), (r == c - 32).astype(F32))
    sel_f = (_iota((128, GPAD), 0) == _iota((128, GPAD), 1) + 32).astype(F32)
    sel_b = (_iota((128, GPAD), 0) == _iota((128, GPAD), 1) + 40).astype(F32)
    ek = (_iota((GPAD, 1024), 0) == (_iota((GPAD, 1024), 1) >> 7)).astype(F32)
    return rope_perm, sel_f, sel_b, ek


def _prep_tile(misc, gq, rq, rk, bm, cm, cosE, sinE, Wg, gbias, dtbf, dtbb, nAf, nAb):
    rope_perm, sel_f, sel_b, ek = _prep_consts()
    logg = _log_sigmoid(_mm_bf(misc, Wg) + gbias) * (1.0 / GLA_TAU)
    a_gla = jnp.concatenate([gq * (32 ** -0.5), logg], axis=1)
    rot = lambda t: t * cosE + _sel(t, rope_perm) * sinE
    a_ret = jnp.concatenate([rot(rq * (64 ** -0.5)), rot(rk)], axis=1)
    dtf = _softplus(_sel(misc, sel_f) + dtbf)
    dtb = _softplus(_sel(misc, sel_b) + dtbb)
    rep = lambda t: jnp.concatenate([t[:, :128]] * 4 + [t[:, 128:]] * 4, axis=1)
    bmr = rep(bm)
    return a_gla, a_ret, rep(cm), bmr * _sel(dtf, ek), bmr * _sel(dtb, ek), dtf * nAf, dtb * nAb


def _prep_row_specs(tr):
    blk = lambda w, j: pl.BlockSpec((tr, w), lambda i: (i, j))
    return [blk(128, MISC // 128), blk(128, _ROFF["gla_q"] // 128), blk(256, _ROFF["ret_q"] // 256),
            blk(256, _ROFF["ret_k"] // 256), blk(256, 2), blk(256, 3), blk(256, 0), blk(256, 0)]


def _whole(a):
    return pl.BlockSpec(a.shape, lambda i: (0,) * a.ndim)


def _prep_fwd_call(Pr, u, cosE, sinE, pp, tr):
    T = Pr.shape[0]
    n_row = 8

    def body(*refs):
        outs = _prep_tile(*[r[...] for r in refs[:n_row + len(pp)]])
        for o_ref, o in zip(refs[n_row + len(pp):], outs):
            o_ref[...] = o

    widths = [384, 512, 1024, 1024, 1024, GPAD, GPAD]
    return pl.pallas_call(
        body, name="prep_fwd", grid=(T // tr,),
        out_shape=tuple(jax.ShapeDtypeStruct((T, w), F32) for w in widths),
        in_specs=_prep_row_specs(tr) + [_whole(p) for p in pp],
        out_specs=tuple(pl.BlockSpec((tr, w), lambda i: (i, 0)) for w in widths),
        compiler_params=_params(("parallel",)),
    )(Pr, Pr, Pr, Pr, u, u, cosE, sinE, *pp)


def _prep_bwd_call(Pr, u, cosE, sinE, pp, cts, tr, comm=None):
    T = Pr.shape[0]
    n_row, n_p = 8, len(pp)
    names = ["gla_dq_f", "gla_dq_b", "gla_dg_f", "gla_dg_b", "gla_dk_f", "gla_dk_b", "gla_dv_f", "gla_dv_b",
             "ret_dq_f", "ret_dq_b", "ret_dk_f", "ret_dk_b", "ret_dv_f", "ret_dv_b",
             "ssd_dq_f", "ssd_dq_b", "ssd_dk_f", "ssd_dk_b", "ssd_dg_f", "ssd_dg_b", "ssd_dv_f", "ssd_dv_b",
             "d_r", "d_z", "d_gr", "d_xs"]
    ct_arrays = [cts[n] for n in names]

    def body(*refs):
        ins = [r[...] for r in refs[:n_row + n_p]]
        c = {n: r[...] for n, r in zip(names, refs[n_row + n_p:n_row + n_p + len(names)])}
        dPr_ref, du_ref = refs[n_row + n_p + len(names):n_row + n_p + len(names) + 2]
        dp_refs = refs[n_row + n_p + len(names) + 2:]
        _, vjp = jax.vjp(_prep_tile, *ins)
        ct_out = (jnp.concatenate([c["gla_dq_f"] + c["gla_dq_b"], c["gla_dg_f"], c["gla_dg_b"]], axis=1),
                  jnp.concatenate([c["ret_dq_f"] + c["ret_dq_b"], c["ret_dk_f"] + c["ret_dk_b"]], axis=1),
                  c["ssd_dq_f"] + c["ssd_dq_b"], c["ssd_dk_f"], c["ssd_dk_b"], c["ssd_dg_f"], c["ssd_dg_b"])
        d = vjp(ct_out)
        d_misc, d_gq, d_rq, d_rk, d_bm, d_cm = d[:6]
        dPr_ref[...] = jnp.concatenate(
            [c["gla_dv_f"] + c["gla_dv_b"], c["d_r"], d_rq, d_rk, c["ret_dv_f"] + c["ret_dv_b"], c["d_gr"],
             c["d_z"], d_gq, c["gla_dk_f"] + c["gla_dk_b"], d_misc,
             jnp.zeros((d_misc.shape[0], R_W - MISC - 128), F32)], axis=1).astype(dPr_ref.dtype)
        du_ref[...] = jnp.concatenate([c["ssd_dv_f"] + c["ssd_dv_b"] + c["d_xs"], d_bm, d_cm], axis=1)

        @pl.when(pl.program_id(0) == 0)
        def _():
            for r in dp_refs:
                r[...] = jnp.zeros_like(r)

        for r, g in zip(dp_refs, d[n_row:]):
            r[...] += g

    row = lambda a: pl.BlockSpec((tr, a.shape[1]), lambda i: (i, 0))
    return _pcall(
        body, name="prep_bwd", grid=(T // tr,),
        out_shape=(jax.ShapeDtypeStruct((T, R_W), BF16), jax.ShapeDtypeStruct((T, 1024), F32))
        + tuple(jax.ShapeDtypeStruct(p.shape, F32) for p in pp),
        in_specs=_prep_row_specs(tr) + [_whole(p) for p in pp] + [row(a) for a in ct_arrays],
        out_specs=(pl.BlockSpec((tr, R_W), lambda i: (i, 0)), pl.BlockSpec((tr, 1024), lambda i: (i, 0)))
        + tuple(_whole(p) for p in pp),
        scratch_shapes=[], sem=("arbitrary",), args=(Pr, Pr, Pr, Pr, u, u, cosE, sinE, *pp, *ct_arrays), comm=comm)


def _post_tile(ogf, ogb, r, ysf, ysb, xs, z, orf, orb, gr, gla_n, dexp, ssd_n, ret_n):
    bd = ((_iota((256, 256), 0) >> 6) == (_iota((256, 256), 1) >> 6)).astype(F32)
    og = ogf + ogb
    gla = og * lax.rsqrt(_sel(og * og, bd) * (1.0 / 64) + RMS_EPS) * gla_n * _silu(r)
    t = (ysf + ysb + dexp * xs) * _silu(z)
    ssd = t * lax.rsqrt(jnp.mean(t * t, axis=-1, keepdims=True) + RMS_EPS) * ssd_n
    o = orf + orb
    oc = o - _sel(o, bd) * (1.0 / 64)
    ret = oc * lax.rsqrt(_sel(oc * oc, bd) * (1.0 / 64) + RMS_EPS) * ret_n * _silu(gr)
    return jnp.concatenate([gla, ssd, ret], axis=1)


def _post_row_specs(tr):
    blk = lambda w, j: pl.BlockSpec((tr, w), lambda i: (i, j))
    return [blk(256, 0), blk(256, 0), blk(256, _ROFF["gla_r"] // 256), blk(512, 0), blk(512, 0), blk(512, 0),
            blk(512, _ROFF["ssd_z"] // 512), blk(256, 0), blk(256, 0), blk(256, _ROFF["ret_g"] // 256)]


def _post_fwd_call(rows, qp, tr, comm=None):
    T = rows[0].shape[0]

    def body(*refs):
        refs[-1][...] = _post_tile(*[r[...] for r in refs[:-1]]).astype(refs[-1].dtype)

    res, got = _pcall(body, name="post_fwd", grid=(T // tr,), out_shape=[jax.ShapeDtypeStruct((T, D), BF16)],
                      in_specs=_post_row_specs(tr) + [_whole(p) for p in qp],
                      out_specs=[pl.BlockSpec((tr, D), lambda i: (i, 0))], scratch_shapes=[],
                      sem=("parallel",), args=(*rows, *qp), comm=comm)
    return res[0], got


def _post_bwd_call(rows, qp, dmixed, tr):
    T = rows[0].shape[0]
    n_in = 10 + len(qp)

    def body(*refs):
        ins = [r[...] for r in refs[:n_in]]
        _, vjp = jax.vjp(_post_tile, *ins)
        d = vjp(refs[n_in][...])
        outs = refs[n_in + 1:]
        for o_ref, g in zip(outs[:7], (d[0], d[3], d[7], d[2], d[6], d[9], d[5])):
            o_ref[...] = g.astype(o_ref.dtype)

        @pl.when(pl.program_id(0) == 0)
        def _():
            for r in outs[7:]:
                r[...] = jnp.zeros_like(r)

        for r, g in zip(outs[7:], d[10:]):
            r[...] += g

    widths = [256, 512, 256, 256, 512, 256, 512]
    dts = [BF16] * 3 + [F32] * 4
    return pl.pallas_call(
        body, name="post_bwd", grid=(T // tr,),
        out_shape=tuple(jax.ShapeDtypeStruct((T, w), dt) for w, dt in zip(widths, dts))
        + tuple(jax.ShapeDtypeStruct(p.shape, F32) for p in qp),
        in_specs=_post_row_specs(tr) + [_whole(p) for p in qp] + [pl.BlockSpec((tr, D), lambda i: (i, 0))],
        out_specs=tuple(pl.BlockSpec((tr, w), lambda i: (i, 0)) for w in widths) + tuple(_whole(p) for p in qp),
        compiler_params=_params(("arbitrary",)),
    )(*rows, *qp, dmixed)


def _mixer_scan_operands(Pr, u, a_gla, a_ret, cmr, kf, kb, g8f, g8b, lg):
    gk, gv = (Pr, _ROFF["gla_k"] // 128), (Pr, _ROFF["gla_v"] // 256)
    rv = (Pr, _ROFF["ret_v"] // 256)
    return {
        "gla": {False: ((a_gla, 0), gk, gv, (a_gla, 1)), True: ((a_gla, 0), gk, gv, (a_gla, 2))},
        "ret": {False: ((a_ret, 0), (a_ret, 1), rv, (lg, 0)), True: ((a_ret, 0), (a_ret, 1), rv, (lg, 0))},
        "ssd": {False: ((cmr, 0), (kf, 0), (u, 0), (g8f, 0)), True: ((cmr, 0), (kb, 0), (u, 0), (g8b, 0))},
    }


def _post_rows(o, Pr, u):
    return [o["gla"][False][0], o["gla"][True][0], Pr, o["ssd"][False][0], o["ssd"][True][0], u, Pr,
            o["ret"][False][0], o["ret"][True][0], Pr]


def _mixer_forward(Tc, Pr, Px, cn, pp, cw8, cb, qp, comm):
    cosE, sinE, lg = cn
    u = _conv_fwd_call(Px, cw8, cb, Tc)
    prep = _prep_fwd_call(Pr, u, cosE, sinE, pp, Tc)
    ops = _mixer_scan_operands(Pr, u, *prep, lg)
    o, got = {}, {}
    for kind in ops:
        o[kind], got[kind] = _scan_fwd_call(kind, ops[kind], Tc, comm.get(kind))
    mixed, got["post"] = _post_fwd_call(_post_rows(o, Pr, u), qp, Tc, comm.get("post"))
    return mixed, (u, prep, o), got


def _mixer_backward(Tc, Pr, Px, cn, pp, cw8, cb, qp, saved, dmixed, comm):
    cosE, sinE, lg = cn
    u, prep, o = saved
    post = _post_bwd_call(_post_rows(o, Pr, u), qp, dmixed, Tc)
    d_o = dict(gla=post[0], ssd=post[1], ret=post[2])
    cts = dict(d_r=post[3], d_z=post[4], d_gr=post[5], d_xs=post[6])
    ops = _mixer_scan_operands(Pr, u, *prep, lg)
    got = {}
    for kind in ops:
        st = {rev: o[kind][rev][1] for rev in (False, True)}
        res, got[kind] = _scan_bwd_call(kind, kind != "ret", ops[kind], st, d_o[kind], Tc, comm.get(kind))
        for rev, sfx in ((False, "_f"), (True, "_b")):
            for nm, a in zip(("_dq", "_dk", "_dv", "_dg"), res[rev]):
                cts[kind + nm + sfx] = a
    pb, got["prep"] = _prep_bwd_call(Pr, u, cosE, sinE, pp, cts, Tc, comm.get("prep"))
    dPx, dcw8, dcb = _conv_bwd_call(Px, cw8, cb, pb[1], Tc)
    return pb[0], dPx, tuple(pb[2:]), dcw8, dcb[0:1], tuple(post[7:]), got


def _mix_fwd(Tc, X, w, cn, nw, mods, pp, cw8, cb, qp, comm):
    h = _norm_fwd_call(X, nw[0], mods[0], mods[1], None, Tc, BF16)
    Px, Pr = _mm(h, w[0], trans_b=True, name="mm_fwd"), _mm(h, w[1], trans_b=True, name="mm_fwd")
    mixed, saved, got = _mixer_forward(Tc, Pr, Px, cn, pp, cw8, cb, qp, comm)
    M = _mm(mixed, w[2], name="mm_fwd")
    Xn = _norm_fwd_call(M, nw[1], mods[2], jnp.zeros_like(mods[2]), X, Tc)
    return Xn, (X, nw, mods, w, cn, pp, cw8, cb, qp, h, Px, Pr, mixed, saved, M), got


def _mix_bwd(Tc, res, dXn, comm):
    X, nw, mods, w, cn, pp, cw8, cb, qp, h, Px, Pr, mixed, saved, M = res
    dM, dnw1, da_post, _ = _norm_bwd_call(M, nw[1], mods[2], dXn, Tc, out_dtype=BF16)
    dmixed = _mm(dM, w[2], trans_b=True, name="mm_dx")
    dPr, dPx, dpp, dcw8, dcb, dqp, got = _mixer_backward(Tc, Pr, Px, cn, pp, cw8, cb, qp, saved, dmixed, comm)
    dh = _mm(dPx, w[0], name="mm_dx")
    dh = _mm(dPr, w[1], name="mm_dx_acc", add=dh)
    dX, dnw0, da_pre, db_pre = _norm_bwd_call(X, nw[0], mods[0], dh, Tc, add=dXn)
    dW = tuple(_mm_tn(a, g, name="mm_dw", out_dtype=BF16) for a, g in ((dPx, h), (dPr, h), (mixed, dM)))
    return dX, ((dnw0[0], dnw1[0]), (da_pre, db_pre, da_post), dpp, dcw8, dcb, dqp), dW, got


def _ffn_fwd(Tc, X, w, nw, mods):
    h = _norm_fwd_call(X, nw[0], mods[0], mods[1], None, Tc, BF16)
    U1 = _mm(h, w[0], trans_b=True, name="mm_fwd", out_dtype=BF16)
    U2 = _mm(h, w[1], trans_b=True, name="mm_fwd", out_dtype=BF16)
    act = _act_call(U1, U2)
    Fo = _mm(act, w[2], name="mm_fwd")
    Xn = _norm_fwd_call(Fo, nw[1], mods[2], jnp.zeros_like(mods[2]), X, Tc)
    return Xn, (X, nw, mods, w, h, U1, U2, act, Fo)


def _ffn_bwd(Tc, res, dXn):
    X, nw, mods, w, h, U1, U2, act, Fo = res
    dFo, dnw1, da_post, _ = _norm_bwd_call(Fo, nw[1], mods[2], dXn, Tc, out_dtype=BF16)
    dU1, dU2 = _act_call(U1, U2, _mm(dFo, w[2], trans_b=True, name="mm_dx", out_dtype=BF16))
    dh = _mm(dU1, w[0], name="mm_dx")
    dh = _mm(dU2, w[1], name="mm_dx_acc", add=dh)
    dX, dnw0, da_pre, db_pre = _norm_bwd_call(X, nw[0], mods[0], dh, Tc, add=dXn)
    dW = tuple(_mm_tn(a, g, name="mm_dw", out_dtype=BF16) for a, g in ((dU1, h), (dU2, h), (act, dFo)))
    return dX, ((dnw0[0], dnw1[0]), (da_pre, db_pre, da_post)), dW


def _rope_tables(Tl, Tc):
    rows = Tl // GRID_W
    row = jnp.repeat(jnp.arange(rows), GRID_W).astype(F32)
    col = jnp.tile(jnp.arange(GRID_W), rows).astype(F32)
    inv_freq = 10000.0 ** (-jnp.arange(16, dtype=F32) / 16)
    ang = jnp.concatenate([row[:, None] * inv_freq, col[:, None] * inv_freq], axis=-1)
    cos = jnp.concatenate([jnp.ones((Tc, 32), F32), jnp.cos(ang)], axis=0)
    sin = jnp.concatenate([jnp.zeros((Tc, 32), F32), jnp.sin(ang)], axis=0)
    return jnp.tile(cos, (1, 8)), jnp.tile(sin, (1, 8))


def _rows8(first, second):
    z = jnp.zeros((6,) + first.shape, F32)
    return jnp.concatenate([first[None], second[None], z], axis=0)


def _layer_inputs(sp, ml, mc):
    gu = sp["gla_gate_up"]
    Wg = jnp.zeros((128, 256), F32).at[0:16, 0:128].set(gu[0]).at[16:32, 128:256].set(gu[1])
    pp = (Wg, sp["gla_gate_b"].reshape(1, 256), sp["ssd_dt_bias"][0:1], sp["ssd_dt_bias"][1:2],
          -jnp.exp(sp["ssd_a_log"][0:1]), -jnp.exp(sp["ssd_a_log"][1:2]))
    qp = (sp["gla_norm"].reshape(1, 256), jnp.repeat(sp["ssd_d"], 64).reshape(1, 512),
          sp["ssd_norm"].reshape(1, 512), sp["ret_norm"].reshape(1, 256))
    mix = ((sp["norm_mix_pre"], sp["norm_mix_post"]),
           (_rows8(1.0 + mc[1], 1.0 + ml[1]), _rows8(mc[0], ml[0]), _rows8(mc[2], ml[2])),
           pp, jnp.pad(sp["ssd_conv_w"], ((0, 3), (0, 0))), sp["ssd_conv_b"].reshape(1, 1024), qp)
    ffn = ((sp["norm_ffn_pre"], sp["norm_ffn_post"]),
           (_rows8(1.0 + mc[4], 1.0 + ml[4]), _rows8(mc[3], ml[3]), _rows8(mc[5], ml[5])))
    return mix, ffn


def _rows_from(g):
    return g.reshape(N_DEV * g.shape[1], g.shape[2])


def _rows_to(f):
    return f.reshape(N_DEV, f.shape[0] // N_DEV, f.shape[1])


def _local_step(xcat, target, mod_l, mod_c, sp, Tc, weights=None, shards=None):
    Tt = xcat.shape[0]
    cosE, sinE = _rope_tables(Tt - Tc, Tc)
    log_gamma = jnp.log1p(-jnp.exp2(-5.0 - jnp.arange(4, dtype=F32)))
    lg = jnp.broadcast_to(jnp.concatenate([log_gamma, jnp.zeros((GPAD - 4,), F32)])[None, :], (Tt, GPAD))
    cn = (cosE, sinE, lg)
    dist = shards is not None
    X, saved = xcat, []
    if dist:
        g_in, g_out = _exchange_call("two", shards[0][:2], "gather_mix0")
    for l in range(DEPTH):
        (a_mix, a_ffn), pull = jax.vjp(_layer_inputs, {n: sp[n][l] for n in _SMALL},
                                       mod_l[l].reshape(6, D), mod_c[l].reshape(6, D))
        comm = {}
        if dist:
            w_in, w_out = _rows_from(g_in), _rows_from(g_out)
            comm = dict(ssd=("two", [shards[l][2]]), ret=("two", [shards[l][3]]))
            if l + 1 < DEPTH:
                comm.update(gla=("two", [shards[l + 1][0]]), post=("two", [shards[l + 1][1]]))
        else:
            w_in, w_out, w13, w2 = weights[l]
        w_x, w_r = _split_w_in(w_in)
        X, r_mix, got = _mix_fwd(Tc, X, (w_x, w_r, w_out), cn, *a_mix, comm)
        if dist:
            w13, w2 = _rows_from(got["ssd"][0]), _rows_from(got["ret"][0])
            if l + 1 < DEPTH:
                g_in, g_out = got["gla"][0], got["post"][0]
        X, r_ffn = _ffn_fwd(Tc, X, (w13[:FFN_H], w13[FFN_H:], w2), *a_ffn)
        saved.append((r_mix, r_ffn, pull))
    loss, dX = _loss_call(X, target, Tc)
    d_sp, d_ml, d_mc = [None] * DEPTH, [None] * DEPTH, [None] * DEPTH
    gw = [[None] * 4 for _ in range(DEPTH)]
    nxt = None
    for l in reversed(range(DEPTH)):
        r_mix, r_ffn, pull = saved[l]
        dX, c_ffn, dW_ffn = _ffn_bwd(Tc, r_ffn, dX)
        g13, g2 = jnp.concatenate([dW_ffn[0], dW_ffn[1]], axis=0), dW_ffn[2]
        comm = {}
        if dist:
            comm = dict(ssd=(True, [_rows_to(g13)]), ret=(True, [_rows_to(g2)]))
            if nxt is not None:
                comm.update(gla=(True, [nxt[0]]), prep=(True, [nxt[1]]))
        dX, c_mix, dW_mix, got = _mix_bwd(Tc, r_mix, dX, comm)
        d_sp[l], d_ml[l], d_mc[l] = pull((c_mix, c_ffn))
        gin, gout = _merge_w_in(dW_mix[0], dW_mix[1]), dW_mix[2]
        if dist:
            gw[l][2], gw[l][3] = got["ssd"][0], got["ret"][0]
            if nxt is not None:
                gw[l + 1][0], gw[l + 1][1] = got["gla"][0], got["prep"][0]
            nxt = (_rows_to(gin), _rows_to(gout))
        else:
            gw[l] = [gin, gout, g13, g2]
    if dist:
        gw[0][0], gw[0][1] = _exchange_call(True, list(nxt), "scatter_mix0")
    d_sp = {n: jnp.stack([d_sp[l][n] for l in range(DEPTH)]) for n in _SMALL}
    return (loss, dX, jnp.stack(d_ml).reshape(DEPTH, 6 * D), jnp.stack(d_mc).reshape(DEPTH, 6 * D), d_sp, gw)


def _sum8_call(slabs, name):
    _, R, Cc = slabs.shape
    tr = _pick(R, (512, 352, 256, 128, 64, 32, 16))

    def body(*refs):
        acc = refs[0][...].astype(F32)
        for r in refs[1:N_DEV]:
            acc = acc + r[...].astype(F32)
        refs[N_DEV][...] = acc

    return pl.pallas_call(
        body, name=name, grid=(R // tr,), out_shape=jax.ShapeDtypeStruct((R, Cc), F32),
        in_specs=[pl.BlockSpec((None, tr, Cc), lambda i, d=d: (d, i, 0)) for d in range(N_DEV)],
        out_specs=pl.BlockSpec((tr, Cc), lambda i: (i, 0)), compiler_params=_params(("parallel",)),
    )(*([slabs] * N_DEV))


def _loss_call(X, target, Tc):
    Tt, W = X.shape
    tr = Tc
    nt = Tt // tr

    def body(x_ref, t_ref, loss_ref, dx_ref, acc_ref):
        i = pl.program_id(0)

        @pl.when(i == 0)
        def _():
            acc_ref[...] = jnp.zeros_like(acc_ref)
            dx_ref[...] = jnp.zeros_like(dx_ref)

        @pl.when(i > 0)
        def _():
            e = x_ref[...] - t_ref[...]
            dx_ref[...] = e * (1.0 / W)
            acc_ref[...] += jnp.sum(e * e, axis=0, keepdims=True)

        @pl.when(i == nt - 1)
        def _():
            loss_ref[...] = jnp.full(loss_ref.shape, (0.5 / W) * jnp.sum(acc_ref[...]), F32)

    loss, dx = pl.pallas_call(
        body, name="loss",
        out_shape=(jax.ShapeDtypeStruct((8, 128), F32), jax.ShapeDtypeStruct((Tt, W), F32)),
        grid=(nt,),
        in_specs=[pl.BlockSpec((tr, W), lambda i: (i, 0)),
                  pl.BlockSpec((tr, W), lambda i: (jnp.maximum(i - 1, 0), 0))],
        out_specs=(pl.BlockSpec((8, 128), lambda i: (0, 0)), pl.BlockSpec((tr, W), lambda i: (i, 0))),
        scratch_shapes=[pltpu.VMEM((1, W), F32)],
        compiler_params=_params(("arbitrary",)),
    )(X, target)
    return loss[0, 0], dx


def _adamw_call(w, g, m, v, name):
    R, Cc = w.shape
    tr = _pick(R, (512, 352, 256, 128, 64, 32, 16, 8))
    c1 = 1.0 - ADAM_B1 ** ADAM_STEP
    c2 = 1.0 - ADAM_B2 ** ADAM_STEP

    def body(w_ref, g_ref, m_ref, v_ref, d_ref, nm_ref, nv_ref):
        gv = g_ref[...]
        nm = ADAM_B1 * m_ref[...] + (1.0 - ADAM_B1) * gv
        nv = ADAM_B2 * v_ref[...] + (1.0 - ADAM_B2) * (gv * gv)
        d_ref[...] = -ADAM_LR * ((nm / c1) / (jnp.sqrt(nv / c2) + ADAM_EPS) + ADAM_WD * w_ref[...])
        nm_ref[...] = nm
        nv_ref[...] = nv

    spec = pl.BlockSpec((tr, Cc), lambda i: (i, 0))
    sh = jax.ShapeDtypeStruct((R, Cc), F32)
    return pl.pallas_call(
        body, name=name, out_shape=(sh, sh, sh), grid=(R // tr,),
        in_specs=[spec] * 4, out_specs=(spec,) * 3, compiler_params=_params(("parallel",)),
    )(w, g, m, v)


def _sum_call(xs, name):
    R, Cc = xs[0].shape
    tr = _pick(R, (512, 352, 256, 128, 64, 32, 16))
    k = len(xs)

    def body(*refs):
        acc = refs[0][...]
        for r in refs[1:k]:
            acc = acc + r[...]
        refs[k][...] = acc

    spec = pl.BlockSpec((tr, Cc), lambda i: (i, 0))
    return pl.pallas_call(
        body, name=name, grid=(R // tr,), in_specs=[spec] * k, out_shape=jax.ShapeDtypeStruct((R, Cc), F32),
        out_specs=spec, compiler_params=_params(("parallel",)),
    )(*xs)


MESH = pl.DeviceIdType.MESH
ANY = pl.BlockSpec(memory_space=pl.ANY)


def _me():
    return lax.axis_index("x"), lax.axis_index("y"), lax.axis_index("c")


_FLIPS = [(0, 0, 1), (1, 0, 0), (0, 1, 0), (1, 1, 0), (1, 0, 1), (0, 1, 1), (1, 1, 1)]


def _exchange_copies(scatter, srcs, dsts, send_sems, recv_sems, loc_sems, arrivals):
    x, y, c = _me()
    me = 4 * x + 2 * y + c
    sends, recvs, local = [], [], []
    for a in range(len(srcs)):
        for k, (dx, dy, dc) in enumerate(_FLIPS):
            px, py, pc = (1 - x if dx else x), (1 - y if dy else y), (1 - c if dc else c)
            peer = 4 * px + 2 * py + pc
            src = srcs[a].at[peer] if scatter else srcs[a]
            for lst, slab in ((sends, me), (recvs, peer)) if arrivals else ((sends, me),):
                lst.append(pltpu.make_async_remote_copy(
                    src_ref=src, dst_ref=dsts[a].at[slab], send_sem=send_sems.at[a, k], recv_sem=recv_sems.at[a, k],
                    device_id=(px, py, pc), device_id_type=MESH))
        local.append(pltpu.make_async_copy(srcs[a].at[me] if scatter else srcs[a], dsts[a].at[me], loc_sems.at[a]))
    return sends, recvs, local


def _exchange_start(scatter, srcs, dsts, sems):
    sends, _, local = _exchange_copies(scatter, srcs, dsts, *sems, arrivals=False)
    for cp in local + sends:
        cp.start()


def _exchange_wait(scatter, srcs, dsts, sems):
    sends, recvs, local = _exchange_copies(scatter, srcs, dsts, *sems, arrivals=True)
    for cp in sends:
        cp.wait_send()
    for cp in recvs:
        cp.wait_recv()
    for cp in local:
        cp.wait()


def _exchange_shapes(scatter, srcs):
    return tuple(jax.ShapeDtypeStruct(((N_DEV,) + s.shape[-2:]), s.dtype) for s in srcs)


def _exchange_sems(n):
    return [pltpu.SemaphoreType.DMA((n, 7)), pltpu.SemaphoreType.DMA((n, 7)), pltpu.SemaphoreType.DMA((n,))]


def _exchange_call(scatter, srcs, name):
    n = len(srcs)

    def body(*refs):
        if scatter == "two":
            _two_level_gather_body(n, refs[:n], refs[n:2 * n], *refs[2 * n:])
        else:
            _exchange_start(scatter, refs[:n], refs[n:2 * n], refs[2 * n:])
            _exchange_wait(scatter, refs[:n], refs[n:2 * n], refs[2 * n:])

    return pl.pallas_call(body, name=name, out_shape=_exchange_shapes(scatter, srcs), in_specs=[ANY] * n,
                          out_specs=(ANY,) * n, scratch_shapes=_exchange_sems(n))(*srcs)


def _pcall(body, *, name, grid, in_specs, out_specs, out_shape, scratch_shapes, sem, args, comm=None):
    if comm is None:
        res = pl.pallas_call(body, name=name, grid=grid, in_specs=list(in_specs), out_specs=tuple(out_specs),
                             out_shape=tuple(out_shape), scratch_shapes=list(scratch_shapes),
                             compiler_params=_params(sem))(*args)
        return tuple(res), ()
    scatter, srcs = comm
    n_in, n_out, n_c, n_s = len(in_specs), len(out_specs), len(srcs), len(scratch_shapes)

    def carrier(*refs):
        ins, c_src = refs[:n_in], refs[n_in:n_in + n_c]
        outs = refs[n_in + n_c:n_in + n_c + n_out]
        c_dst = refs[n_in + n_c + n_out:n_in + 2 * n_c + n_out]
        scr = refs[n_in + 2 * n_c + n_out:n_in + 2 * n_c + n_out + n_s]
        first = pl.program_id(0) == 0
        last = pl.program_id(0) == grid[0] - 1
        for ax in range(1, len(grid)):
            first = jnp.logical_and(first, pl.program_id(ax) == 0)
            last = jnp.logical_and(last, pl.program_id(ax) == grid[ax] - 1)

        two_level = scatter == "two"

        @pl.when(first)
        def _():
            if two_level:
                _two_level_gather("start", n_c, c_src, c_dst, *refs[-3:])
            else:
                _exchange_start(scatter, c_src, c_dst, refs[-3:])

        body(*ins, *outs, *scr)

        if two_level:
            @pl.when(pl.program_id(0) == (3 * grid[0]) // 4)
            def _():
                _two_level_gather("pass", n_c, c_src, c_dst, *refs[-3:])

        @pl.when(last)
        def _():
            if two_level:
                _two_level_gather("finish", n_c, c_src, c_dst, *refs[-3:])
            else:
                _exchange_wait(scatter, c_src, c_dst, refs[-3:])

    res = pl.pallas_call(
        carrier, name=name + "_x", grid=grid, in_specs=list(in_specs) + [ANY] * n_c,
        out_specs=tuple(out_specs) + (ANY,) * n_c, out_shape=tuple(out_shape) + _exchange_shapes(scatter, srcs),
        scratch_shapes=list(scratch_shapes) + _exchange_sems(n_c),
        compiler_params=_params(("arbitrary",) * len(grid)))(*args, *srcs)
    return tuple(res[:n_out]), tuple(res[n_out:])


def _two_level_gather(phase, n_arr, x_refs, out_refs, send_sems, recv_sems, local_sems):
    x, y, c = _me()
    me, sibling = (x, y, c), (x, y, 1 - c)
    chips = [(1 - x, y), (x, 1 - y), (1 - x, 1 - y)]

    def slab(a, px, py, pc):
        return out_refs[a].at[4 * px + 2 * py + pc]

    def copy(a, k, block, to, src=None):
        return pltpu.make_async_remote_copy(
            src_ref=slab(a, *block) if src is None else src, dst_ref=slab(a, *block),
            send_sem=send_sems.at[a, k], recv_sem=recv_sems.at[a, k], device_id=to, device_id_type=MESH)

    def first(a):
        return [copy(a, 0, me, sibling, src=x_refs[a])] + [copy(a, 1 + j, me, (*chip, c), src=x_refs[a])
                                                            for j, chip in enumerate(chips)]

    if phase == "start":
        for a in range(n_arr):
            pltpu.make_async_copy(x_refs[a], slab(a, *me), local_sems.at[a]).start()
        for a in range(n_arr):
            for cp in first(a):
                cp.start()
    elif phase == "pass":
        for j, chip in enumerate(chips):
            for a in range(n_arr):
                copy(a, 1 + j, (*chip, c), me).wait_recv()
                copy(a, 4 + j, (*chip, c), sibling).start()
    else:
        for a in range(n_arr):
            copy(a, 0, sibling, me).wait_recv()
            for j, chip in enumerate(chips):
                copy(a, 4 + j, (*chip, 1 - c), me).wait_recv()
        for a in range(n_arr):
            for cp in first(a) + [copy(a, 4 + j, (*chip, c), sibling) for j, chip in enumerate(chips)]:
                cp.wait_send()
            pltpu.make_async_copy(x_refs[a], slab(a, *me), local_sems.at[a]).wait()


def _two_level_gather_body(n_arr, x_refs, out_refs, send_sems, recv_sems, local_sems):
    for phase in ("start", "pass", "finish"):
        _two_level_gather(phase, n_arr, x_refs, out_refs, send_sems, recv_sems, local_sems)


def _gather_small(x, name):
    def body(x_ref, out_ref, send_sems, recv_sems, local_sems):
        _two_level_gather_body(1, [x_ref], [out_ref], send_sems, recv_sems, local_sems)

    vm = pl.BlockSpec(memory_space=pltpu.VMEM)
    return pl.pallas_call(
        body, name=name,
        out_shape=jax.ShapeDtypeStruct((N_DEV,) + x.shape, x.dtype),
        in_specs=[vm], out_specs=vm,
        scratch_shapes=[pltpu.SemaphoreType.DMA((1, 7)), pltpu.SemaphoreType.DMA((1, 7)),
                        pltpu.SemaphoreType.DMA((1,))],
    )(x)


_SMALL = ["norm_mix_pre", "norm_mix_post", "norm_ffn_pre", "norm_ffn_post", "gla_gate_up", "gla_gate_b",
          "gla_norm", "ssd_conv_w", "ssd_conv_b", "ssd_dt_bias", "ssd_a_log", "ssd_d", "ssd_norm", "ret_norm"]


def _pack(arrs):
    flat = jnp.concatenate([a.reshape(-1) for a in arrs])
    n = flat.shape[0]
    npad = -(-n // 1024) * 1024
    return jnp.pad(flat, (0, npad - n)).reshape(npad // 128, 128)


def _unpack(buf, shapes):
    flat = buf.reshape(-1)
    out, o = [], 0
    for s in shapes:
        n = math.prod(s)
        out.append(flat[o:o + n].reshape(s))
        o += n
    return out


def kernel(x, c, ctx, c_ctx, ada_w, ada_b, norm_mix_pre, norm_mix_post, norm_ffn_pre, norm_ffn_post, w_in, w_out, gla_gate_up, gla_gate_b, gla_norm, ssd_conv_w, ssd_conv_b, ssd_dt_bias, ssd_a_log, ssd_d, ssd_norm, ret_norm, ffn_w13, ffn_w2, loss_target, m_c_ctx, m_ada_w, m_ada_b, m_norm_mix_pre, m_norm_mix_post, m_norm_ffn_pre, m_norm_ffn_post, m_w_in, m_w_out, m_gla_gate_up, m_gla_gate_b, m_gla_norm, m_ssd_conv_w, m_ssd_conv_b, m_ssd_dt_bias, m_ssd_a_log, m_ssd_d, m_ssd_norm, m_ret_norm, m_ffn_w13, m_ffn_w2, v_c_ctx, v_ada_w, v_ada_b, v_norm_mix_pre, v_norm_mix_post, v_norm_ffn_pre, v_norm_ffn_post, v_w_in, v_w_out, v_gla_gate_up, v_gla_gate_b, v_gla_norm, v_ssd_conv_w, v_ssd_conv_b, v_ssd_dt_bias, v_ssd_a_log, v_ssd_d, v_ssd_norm, v_ret_norm, v_ffn_w13, v_ffn_w2):
    P_ = dict(c_ctx=c_ctx, ada_w=ada_w, ada_b=ada_b, norm_mix_pre=norm_mix_pre, norm_mix_post=norm_mix_post,
              norm_ffn_pre=norm_ffn_pre, norm_ffn_post=norm_ffn_post, w_in=w_in, w_out=w_out,
              gla_gate_up=gla_gate_up, gla_gate_b=gla_gate_b, gla_norm=gla_norm, ssd_conv_w=ssd_conv_w,
              ssd_conv_b=ssd_conv_b, ssd_dt_bias=ssd_dt_bias, ssd_a_log=ssd_a_log, ssd_d=ssd_d,
              ssd_norm=ssd_norm, ret_norm=ret_norm, ffn_w13=ffn_w13, ffn_w2=ffn_w2)
    M_ = dict(c_ctx=m_c_ctx, ada_w=m_ada_w, ada_b=m_ada_b, norm_mix_pre=m_norm_mix_pre,
              norm_mix_post=m_norm_mix_post, norm_ffn_pre=m_norm_ffn_pre, norm_ffn_post=m_norm_ffn_post,
              w_in=m_w_in, w_out=m_w_out, gla_gate_up=m_gla_gate_up, gla_gate_b=m_gla_gate_b,
              gla_norm=m_gla_norm, ssd_conv_w=m_ssd_conv_w, ssd_conv_b=m_ssd_conv_b, ssd_dt_bias=m_ssd_dt_bias,
              ssd_a_log=m_ssd_a_log, ssd_d=m_ssd_d, ssd_norm=m_ssd_norm, ret_norm=m_ret_norm,
              ffn_w13=m_ffn_w13, ffn_w2=m_ffn_w2)
    V_ = dict(c_ctx=v_c_ctx, ada_w=v_ada_w, ada_b=v_ada_b, norm_mix_pre=v_norm_mix_pre,
              norm_mix_post=v_norm_mix_post, norm_ffn_pre=v_norm_ffn_pre, norm_ffn_post=v_norm_ffn_post,
              w_in=v_w_in, w_out=v_w_out, gla_gate_up=v_gla_gate_up, gla_gate_b=v_gla_gate_b,
              gla_norm=v_gla_norm, ssd_conv_w=v_ssd_conv_w, ssd_conv_b=v_ssd_conv_b, ssd_dt_bias=v_ssd_dt_bias,
              ssd_a_log=v_ssd_a_log, ssd_d=v_ssd_d, ssd_norm=v_ssd_norm, ret_norm=v_ret_norm,
              ffn_w13=v_ffn_w13, ffn_w2=v_ffn_w2)
    order = ["c_ctx", "ada_w", "ada_b", "norm_mix_pre", "norm_mix_post", "norm_ffn_pre", "norm_ffn_post", "w_in",
             "w_out", "gla_gate_up", "gla_gate_b", "gla_norm", "ssd_conv_w", "ssd_conv_b", "ssd_dt_bias",
             "ssd_a_log", "ssd_d", "ssd_norm", "ret_norm", "ffn_w13", "ffn_w2"]

    mx, my, mc_ = _me()
    me = 4 * mx + 2 * my + mc_
    Tl, Tc = x.shape[1], ctx.shape[1]
    n_in, n_out, n_13, n_2 = w_in.shape[2], w_out.shape[1], ffn_w13.shape[2], ffn_w2.shape[1]
    n_ada = ada_w.shape[2]

    shards = [[w_in[l].T.astype(BF16), w_out[l].astype(BF16), ffn_w13[l].T.astype(BF16), ffn_w2[l].astype(BF16)]
              for l in range(DEPTH)]

    cw = ssd_conv_w.shape[2]
    small_in = jnp.concatenate([jnp.pad(c, ((0, 7), (0, 0))).reshape(-1),
                                ssd_conv_w.reshape(-1)]).reshape(-1, 128)
    n_c_rows = 8 * D // 128
    small_in = jnp.pad(small_in, ((0, -small_in.shape[0] % 8), (0, 0)))
    gathered = _gather_small(small_in, "gather_c_conv")
    c_all = gathered[:, :n_c_rows].reshape(N_DEV, 8, D)[:, 0]
    conv_rows = DEPTH * 5 * cw // 128
    conv_full = gathered[:, n_c_rows:n_c_rows + conv_rows].reshape(N_DEV, DEPTH, 5, cw)
    conv_full = jnp.moveaxis(conv_full, 0, 2).reshape(DEPTH, 5, N_DEV * cw)
    c9 = jnp.concatenate([c_all, c_ctx[None], jnp.zeros((7, D), F32)], axis=0)
    s9 = c9 * jax.nn.sigmoid(c9)
    mod_piece = jnp.concatenate([_mm(s9, ada_w[l], name="mm_mod") for l in range(DEPTH)], axis=0)
    mod_g = _gather_small(mod_piece, "gather_mod")
    mod_all = jnp.moveaxis(mod_g.reshape(N_DEV, DEPTH, 16, n_ada), 0, 2).reshape(DEPTH, 16, N_DEV * n_ada)
    mod_all = mod_all + ada_b[:, None, :]
    mod_l = lax.dynamic_index_in_dim(mod_all, me, axis=1, keepdims=False)
    mod_c = mod_all[:, 8]

    sp = {n: P_[n] for n in _SMALL}
    sp["ssd_conv_w"] = conv_full
    xcat = jnp.concatenate([ctx[0], x[0]], axis=0)
    loss_local, d_xcat, d_mod_l, d_mod_c, d_sp, gw = _local_step(xcat, loss_target[0], mod_l, mod_c, sp, Tc,
                                                                 shards=shards)
    loss = lax.psum(loss_local, ("x", "y", "c"))
    grad_x = d_xcat[Tc:][None]

    G = {n: jnp.stack([_sum8_call(gw[l][a], f"sum_{n}") for l in range(DEPTH)])
         for a, n in enumerate(["w_in", "w_out", "ffn_w13", "ffn_w2"])}
    G["w_in"], G["ffn_w13"] = jnp.swapaxes(G["w_in"], 1, 2), jnp.swapaxes(G["ffn_w13"], 1, 2)

    dmod_rows = jnp.concatenate([d_mod_l, d_mod_c], axis=0)
    dmod_g = _gather_small(dmod_rows, "gather_dmod").reshape(N_DEV, 2, DEPTH, 6 * D)
    dl = jnp.moveaxis(dmod_g[:, 0], 0, 1)
    dc = dmod_g[:, 1, :, :]
    dc_tot = dc[0]
    for d_ in range(1, N_DEV):
        dc_tot = dc_tot + dc[d_]
    dmod9 = jnp.concatenate([dl, dc_tot[:, None, :], jnp.zeros((DEPTH, 7, 6 * D), F32)], axis=1)
    g_ada_b = dmod9[:, 0]
    for r_ in range(1, 9):
        g_ada_b = g_ada_b + dmod9[:, r_]
    dmod9_mine = lax.dynamic_slice_in_dim(dmod9, me * n_ada, n_ada, axis=2)
    s9T = jnp.pad(s9.T, ((0, 0), (0, 112)))
    g_ada_w = jnp.stack([_mm(s9T, jnp.pad(dmod9_mine[l], ((0, 112), (0, 0))), name="mm_dada")
                         for l in range(DEPTH)])
    ds9 = _mm(dmod9_mine[0], ada_w[0], trans_b=True, name="mm_ds9")
    for l in range(1, DEPTH):
        ds9 = _mm(dmod9_mine[l], ada_w[l], trans_b=True, name="mm_ds9_acc", add=ds9)
    ds_ctx_part = ds9[8]

    small_names = [n for n in _SMALL]
    small_parts = [d_sp[n] for n in small_names] + [ds_ctx_part]
    packed = _pack(small_parts)
    allp = _gather_small(packed, "gather_small_grads")
    summed = _sum_call([allp[d_] for d_ in range(N_DEV)], "sum_small_grads")
    parts = _unpack(summed, [p.shape for p in small_parts])
    for n, p in zip(small_names, parts[:-1]):
        G[n] = p
    sig = jax.nn.sigmoid(c_ctx)
    G["c_ctx"] = parts[-1] * (sig * (1.0 + c_ctx * (1.0 - sig)))
    G["ssd_conv_w"] = lax.dynamic_slice_in_dim(G["ssd_conv_w"], me * cw, cw, axis=2)
    G["ada_w"] = g_ada_w
    G["ada_b"] = g_ada_b

    delta, new_m, new_v = {}, {}, {}
    for n in ["ada_w", "w_in", "w_out", "ffn_w13", "ffn_w2"]:
        sh = P_[n].shape
        f2 = lambda a: a.reshape(sh[0] * sh[1], sh[2])
        d_, m_, v_ = _adamw_call(f2(P_[n]), f2(G[n]), f2(M_[n]), f2(V_[n]), f"adamw_{n}")
        delta[n], new_m[n], new_v[n] = d_.reshape(sh), m_.reshape(sh), v_.reshape(sh)
    rest = [n for n in order if n not in delta]
    shapes = [P_[n].shape for n in rest]
    d_, m_, v_ = _adamw_call(_pack([P_[n] for n in rest]), _pack([G[n] for n in rest]),
                             _pack([M_[n] for n in rest]), _pack([V_[n] for n in rest]), "adamw_small")
    for n, a, b, e in zip(rest, _unpack(d_, shapes), _unpack(m_, shapes), _unpack(v_, shapes)):
        delta[n], new_m[n], new_v[n] = a, b, e

    return (loss, grad_x, *[G[n] for n in order], *[delta[n] for n in order],
            *[new_m[n] for n in order], *[new_v[n] for n in order])
```

```python
import math

import jax
import jax.numpy as jnp
from jax import lax
from jax.experimental import pallas as pl
from jax.experimental.pallas import tpu as pltpu

F32 = jnp.float32
BF16 = jnp.bfloat16

D = 1024
DEPTH = 4
GRID_W = 64
RMS_EPS = 1e-6
GLA_TAU = 16.0
FFN_H = 2816
N_DEV = 8
ADAM_LR, ADAM_B1, ADAM_B2, ADAM_EPS, ADAM_WD, ADAM_STEP = 0.001, 0.9, 0.999, 1e-08, 0.01, 10

VMEM_LIMIT = 48 * 1024 * 1024

_ORIG = dict(gla_q=(0, 128), gla_k=(128, 128), gla_v=(256, 256), gla_r=(512, 256), gla_lr=(768, 32),
             ssd_z=(800, 512), ssd_xbc=(1312, 1024), ssd_dt=(2336, 16), ret_q=(2352, 256), ret_k=(2608, 256),
             ret_v=(2864, 256), ret_g=(3120, 256))
_R_ORDER = ["gla_v", "gla_r", "ret_q", "ret_k", "ret_v", "ret_g", "ssd_z", "gla_q", "gla_k", "gla_lr", "ssd_dt"]
R_W = 2560
_ROFF = {}
_o = 0
for _n in _R_ORDER:
    _ROFF[_n] = _o
    _o += _ORIG[_n][1]
MISC = _ROFF["gla_lr"]
assert MISC == 2304 and _o == 2352


def _split_w_in(wt):
    xs, xz = _ORIG["ssd_xbc"]
    parts = [wt[_ORIG[n][0]:_ORIG[n][0] + _ORIG[n][1]] for n in _R_ORDER]
    parts.append(jnp.zeros((R_W - _o,) + wt.shape[1:], wt.dtype))
    return wt[xs:xs + xz], jnp.concatenate(parts, axis=0)


def _merge_w_in(wx, wr):
    pieces = []
    for n, (s, z) in sorted(_ORIG.items(), key=lambda t: t[1][0]):
        pieces.append(wx if n == "ssd_xbc" else wr[_ROFF[n]:_ROFF[n] + z])
    return jnp.concatenate(pieces, axis=0)


def _pick(n, cands):
    for c in cands:
        if n % c == 0:
            return c
    return n


def _params(sem=None):
    kw = dict(vmem_limit_bytes=VMEM_LIMIT)
    if sem is not None:
        kw["dimension_semantics"] = sem
    return pltpu.CompilerParams(**kw)


def _iota(shape, dim):
    return lax.broadcasted_iota(jnp.int32, shape, dim)


def _dot(a, b, dims):
    return lax.dot_general(a, b, (dims, ((), ())), preferred_element_type=F32)


_NN = ((1,), (0,))
_NT = ((1,), (1,))
_TN = ((0,), (0,))


def _bf(x):
    return x.astype(BF16)


def _dot_sel(x, e, dims, x_left=True):
    eb = e.astype(BF16)
    hi = x.astype(BF16)
    r1 = x - hi.astype(F32)
    mid = r1.astype(BF16)
    lo = (r1 - mid.astype(F32)).astype(BF16)
    out = None
    for p in (hi, mid, lo):
        t = _dot(p, eb, dims) if x_left else _dot(eb, p, dims)
        out = t if out is None else out + t
    return out


@jax.custom_vjp
def _sel(x, e):
    return _dot_sel(x, e, _NN)


_sel.defvjp(lambda x, e: (_dot_sel(x, e, _NN), e), lambda e, g: (_dot_sel(g, e, _NT), jnp.zeros_like(e)))


def _sig(x):
    e = jnp.exp(-jnp.abs(x))
    return jnp.where(x >= 0, 1.0 / (1.0 + e), e / (1.0 + e))


@jax.custom_vjp
def _sigmoid(x):
    return _sig(x)


def _sigmoid_fwd(x):
    s = _sig(x)
    return s, s


_sigmoid.defvjp(_sigmoid_fwd, lambda s, g: (g * s * (1.0 - s),))


def _silu(x):
    return x * _sigmoid(x)


@jax.custom_vjp
def _softplus(x):
    return jnp.maximum(x, 0.0) + jnp.log(1.0 + jnp.exp(-jnp.abs(x)))


_softplus.defvjp(lambda x: (jnp.maximum(x, 0.0) + jnp.log(1.0 + jnp.exp(-jnp.abs(x))), x),
                 lambda x, g: (g * _sig(x),))


def _log_sigmoid(x):
    return -_softplus(-x)


@jax.custom_vjp
def _mm_bf(x, w):
    return _dot(_bf(x), _bf(w), _NN)


_mm_bf.defvjp(lambda x, w: (_dot(_bf(x), _bf(w), _NN), (x, w)),
              lambda r, g: (_dot(_bf(g), _bf(r[1]), _NT), _dot(_bf(r[0]), _bf(g), _TN)))


_TILE_M = (1088, 1024, 512, 256, 128, 64, 32, 16)
_TILE_N = (1408, 1280, 1024, 768, 512, 384, 256, 128)
_TILE_K = (1408, 1280, 1024, 768, 512, 384, 256, 128)


def _mm(a, b, *, trans_b=False, name, add=None, out_dtype=F32):
    M, K = a.shape
    N = b.shape[0] if trans_b else b.shape[1]
    assert (b.shape[1] if trans_b else b.shape[0]) == K
    tm, tn, tk = _pick(M, _TILE_M), _pick(N, _TILE_N), _pick(K, _TILE_K)
    nk = K // tk
    dims = _NT if trans_b else _NN
    has_add = add is not None

    def body(*refs):
        a_ref, b_ref = refs[0], refs[1]
        o_ref, acc_ref = refs[-2], refs[-1]
        k = pl.program_id(2)

        @pl.when(k == 0)
        def _():
            acc_ref[...] = refs[2][...] if has_add else jnp.zeros_like(acc_ref)

        acc_ref[...] += _dot(a_ref[...].astype(BF16), b_ref[...].astype(BF16), dims)

        @pl.when(k == nk - 1)
        def _():
            o_ref[...] = acc_ref[...].astype(o_ref.dtype)

    b_spec = (pl.BlockSpec((tn, tk), lambda i, j, k: (j, k)) if trans_b
              else pl.BlockSpec((tk, tn), lambda i, j, k: (k, j)))
    o_spec = pl.BlockSpec((tm, tn), lambda i, j, k: (i, j))
    return pl.pallas_call(
        body, name=name,
        out_shape=jax.ShapeDtypeStruct((M, N), out_dtype),
        grid=(M // tm, N // tn, nk),
        in_specs=[pl.BlockSpec((tm, tk), lambda i, j, k: (i, k)), b_spec] + ([o_spec] if has_add else []),
        out_specs=o_spec,
        scratch_shapes=[pltpu.VMEM((tm, tn), F32)],
        compiler_params=_params(("parallel", "parallel", "arbitrary")),
    )(*((a, b, add) if has_add else (a, b)))


def _mm_tn(a, g, *, name, out_dtype=F32):
    M, K = a.shape
    N = g.shape[1]
    tm, tk, tn = _pick(M, _TILE_M), _pick(K, _TILE_K), _pick(N, _TILE_N)
    nm = M // tm

    def body(a_ref, g_ref, o_ref, acc_ref):
        i = pl.program_id(2)

        @pl.when(i == 0)
        def _():
            acc_ref[...] = jnp.zeros_like(acc_ref)

        acc_ref[...] += _dot(a_ref[...].astype(BF16), g_ref[...].astype(BF16), _TN)

        @pl.when(i == nm - 1)
        def _():
            o_ref[...] = acc_ref[...].astype(o_ref.dtype)

    return pl.pallas_call(
        body, name=name,
        out_shape=jax.ShapeDtypeStruct((K, N), out_dtype),
        grid=(K // tk, N // tn, nm),
        in_specs=[pl.BlockSpec((tm, tk), lambda k, j, i: (i, k)), pl.BlockSpec((tm, tn), lambda k, j, i: (i, j))],
        out_specs=pl.BlockSpec((tk, tn), lambda k, j, i: (k, j)),
        scratch_shapes=[pltpu.VMEM((tk, tn), F32)],
        compiler_params=_params(("parallel", "parallel", "arbitrary")),
    )(a, g)


def _norm_fwd_call(x, w, a2, b2, res, tr, out_dtype=F32):
    T, W = x.shape
    has_res = res is not None

    def body(*refs):
        x_ref, w_ref, a_ref, b_ref = refs[:4]
        y_ref = refs[-1]
        seg = jnp.minimum(pl.program_id(0), 1)
        xv = x_ref[...]
        rstd = lax.rsqrt(jnp.mean(xv * xv, axis=-1, keepdims=True) + RMS_EPS)
        y = a_ref[pl.ds(seg, 1), :] * (xv * rstd * w_ref[...]) + b_ref[pl.ds(seg, 1), :]
        y_ref[...] = (y + refs[4][...] if has_res else y).astype(y_ref.dtype)

    row = pl.BlockSpec((tr, W), lambda i: (i, 0))
    small = pl.BlockSpec((8, W), lambda i: (0, 0))
    return pl.pallas_call(
        body, name="norm_fwd",
        out_shape=jax.ShapeDtypeStruct((T, W), out_dtype),
        grid=(T // tr,),
        in_specs=[row, pl.BlockSpec((1, W), lambda i: (0, 0)), small, small] + ([row] if has_res else []),
        out_specs=row,
        compiler_params=_params(("parallel",)),
    )(*((x, w.reshape(1, W), a2, b2) + ((res,) if has_res else ())))


def _norm_bwd_call(x, w, a2, dy, tr, add=None, out_dtype=F32):
    T, W = x.shape
    has_add = add is not None

    def body(*refs):
        x_ref, w_ref, a_ref, dy_ref = refs[:4]
        dx_ref, dw_ref, da_ref, db_ref = refs[-4:]
        i = pl.program_id(0)
        seg = jnp.minimum(i, 1)

        @pl.when(i == 0)
        def _():
            dw_ref[...] = jnp.zeros_like(dw_ref)
            da_ref[...] = jnp.zeros_like(da_ref)
            db_ref[...] = jnp.zeros_like(db_ref)

        xv = x_ref[...]
        g = dy_ref[...]
        wv = w_ref[...]
        rstd = lax.rsqrt(jnp.mean(xv * xv, axis=-1, keepdims=True) + RMS_EPS)
        xh = xv * rstd
        da_ref[pl.ds(seg, 1), :] += jnp.sum(g * (xh * wv), axis=0, keepdims=True)
        db_ref[pl.ds(seg, 1), :] += jnp.sum(g, axis=0, keepdims=True)
        gy = g * a_ref[pl.ds(seg, 1), :]
        dw_ref[0:1, :] += jnp.sum(gy * xh, axis=0, keepdims=True)
        gx = gy * wv
        dx = rstd * (gx - xh * jnp.mean(gx * xh, axis=-1, keepdims=True))
        dx_ref[...] = (dx + refs[4][...] if has_add else dx).astype(dx_ref.dtype)

    acc = jax.ShapeDtypeStruct((8, W), F32)
    acc_spec = pl.BlockSpec((8, W), lambda i: (0, 0))
    row = pl.BlockSpec((tr, W), lambda i: (i, 0))
    return pl.pallas_call(
        body, name="norm_bwd",
        out_shape=(jax.ShapeDtypeStruct((T, W), out_dtype), acc, acc, acc),
        grid=(T // tr,),
        in_specs=[row, pl.BlockSpec((1, W), lambda i: (0, 0)), acc_spec, row] + ([row] if has_add else []),
        out_specs=(row, acc_spec, acc_spec, acc_spec),
        compiler_params=_params(("arbitrary",)),
    )(*((x, w.reshape(1, W), a2, dy) + ((add,) if has_add else ())))


def _act_call(u1, u2, dact=None):
    T, W = u1.shape
    tr = _pick(T, (512, 256, 128, 64))
    tn = _pick(W, (1408, 512, 256, 128))
    spec = pl.BlockSpec((tr, tn), lambda i, j: (i, j))
    sh = jax.ShapeDtypeStruct((T, W), BF16)
    if dact is None:
        def body(a_ref, b_ref, o_ref):
            a = a_ref[...].astype(F32)
            o_ref[...] = (a * _sig(a) * b_ref[...].astype(F32)).astype(o_ref.dtype)

        return pl.pallas_call(body, name="act_fwd", out_shape=sh, grid=(T // tr, W // tn), in_specs=[spec, spec],
                              out_specs=spec, compiler_params=_params(("parallel", "parallel")))(u1, u2)

    def body(a_ref, b_ref, g_ref, da_ref, db_ref):
        a, g = a_ref[...].astype(F32), g_ref[...].astype(F32)
        s = _sig(a)
        da_ref[...] = (g * b_ref[...].astype(F32) * (s * (1.0 + a * (1.0 - s)))).astype(da_ref.dtype)
        db_ref[...] = (g * a * s).astype(db_ref.dtype)

    return pl.pallas_call(body, name="act_bwd", out_shape=(sh, sh), grid=(T // tr, W // tn),
                          in_specs=[spec, spec, spec], out_specs=(spec, spec),
                          compiler_params=_params(("parallel", "parallel")))(u1, u2, dact)


def _conv_specs(T, Wc, tr):
    hb, nt = tr // 8, T // tr
    row = pl.BlockSpec((tr, Wc), lambda i: (i, 0))
    prev = pl.BlockSpec((8, Wc), lambda i: (jnp.maximum(i * hb - 1, 0), 0))
    nxt = pl.BlockSpec((8, Wc), lambda i: (jnp.minimum((i + 1) * hb, T // 8 - 1), 0))
    return row, prev, nxt, nt


def _fill_ext(dst_ref, cur_ref, prev_ref, next_ref, i, nt, tr):
    has_prev = (i > 1).astype(F32)
    has_next = jnp.logical_and(i > 0, i < nt - 1).astype(F32)
    dst_ref[8:16, :] = prev_ref[...] * has_prev
    dst_ref[16:16 + tr, :] = cur_ref[...]
    dst_ref[16 + tr:24 + tr, :] = next_ref[...] * has_next


def _conv_fwd_call(px, w8, b, tr):
    T, Wc = px.shape
    row, prev, nxt, nt = _conv_specs(T, Wc, tr)

    def body(x_ref, xp_ref, xn_ref, w_ref, b_ref, u_ref, xe_ref):
        i = pl.program_id(0)

        @pl.when(i == 0)
        def _():
            xe_ref[...] = jnp.zeros_like(xe_ref)

        _fill_ext(xe_ref, x_ref, xp_ref, xn_ref, i, nt, tr)
        y = b_ref[...] + w_ref[0:1, :] * xe_ref[pl.ds(14, tr), :]
        for k in range(1, 5):
            y = y + w_ref[k:k + 1, :] * xe_ref[pl.ds(14 + k, tr), :]
        u_ref[...] = y * _sig(y)

    return pl.pallas_call(
        body, name="conv_fwd", out_shape=jax.ShapeDtypeStruct((T, Wc), F32), grid=(nt,),
        in_specs=[row, prev, nxt, pl.BlockSpec((8, Wc), lambda i: (0, 0)), pl.BlockSpec((1, Wc), lambda i: (0, 0))],
        out_specs=row, scratch_shapes=[pltpu.VMEM((tr + 32, Wc), F32)],
        compiler_params=_params(("arbitrary",)),
    )(px, px, px, w8, b)


def _conv_bwd_call(px, w8, b, du, tr):
    T, Wc = px.shape
    row, prev, nxt, nt = _conv_specs(T, Wc, tr)
    E = tr + 16

    def body(x_ref, xp_ref, xn_ref, g_ref, gp_ref, gn_ref, w_ref, b_ref, dx_ref, dw_ref, db_ref,
             xe_ref, ge_ref, dy_ref):
        i = pl.program_id(0)

        @pl.when(i == 0)
        def _():
            xe_ref[...] = jnp.zeros_like(xe_ref)
            ge_ref[...] = jnp.zeros_like(ge_ref)
            dy_ref[...] = jnp.zeros_like(dy_ref)
            dw_ref[...] = jnp.zeros_like(dw_ref)
            db_ref[...] = jnp.zeros_like(db_ref)

        _fill_ext(xe_ref, x_ref, xp_ref, xn_ref, i, nt, tr)
        _fill_ext(ge_ref, g_ref, gp_ref, gn_ref, i, nt, tr)
        y = b_ref[...] + w_ref[0:1, :] * xe_ref[pl.ds(6, E), :]
        for k in range(1, 5):
            y = y + w_ref[k:k + 1, :] * xe_ref[pl.ds(6 + k, E), :]
        s = _sig(y)
        dy = ge_ref[pl.ds(8, E), :] * (s * (1.0 + y * (1.0 - s)))
        dy_ref[pl.ds(8, E), :] = dy
        dx = w_ref[0:1, :] * dy_ref[pl.ds(18, tr), :]
        for k in range(1, 5):
            dx = dx + w_ref[k:k + 1, :] * dy_ref[pl.ds(18 - k, tr), :]
        dx_ref[...] = dx.astype(dx_ref.dtype)
        dyt = dy_ref[pl.ds(16, tr), :]
        db_ref[0:1, :] += jnp.sum(dyt, axis=0, keepdims=True)
        for k in range(5):
            dw_ref[k:k + 1, :] += jnp.sum(dyt * xe_ref[pl.ds(14 + k, tr), :], axis=0, keepdims=True)

    acc = jax.ShapeDtypeStruct((8, Wc), F32)
    acc_spec = pl.BlockSpec((8, Wc), lambda i: (0, 0))
    ext = pltpu.VMEM((tr + 32, Wc), F32)
    return pl.pallas_call(
        body, name="conv_bwd", out_shape=(jax.ShapeDtypeStruct((T, Wc), BF16), acc, acc), grid=(nt,),
        in_specs=[row, prev, nxt, row, prev, nxt, acc_spec, pl.BlockSpec((1, Wc), lambda i: (0, 0))],
        out_specs=(row, acc_spec, acc_spec), scratch_shapes=[ext, ext, ext],
        compiler_params=_params(("arbitrary",)),
    )(px, px, px, du, du, du, w8, b)


_SCAN_CFG = {
    "gla": dict(H=4, Dk=32, Dv=64, nh=4, scalar=False, C=128),
    "ssd": dict(H=8, Dk=128, Dv=64, nh=2, scalar=True, C=128),
    "ret": dict(H=4, Dk=64, Dv=64, nh=4, scalar=True, C=128),
}
GPAD = 8


def _log2(n):
    r = int(math.log2(n))
    assert 1 << r == n
    return r


class _ScanMath:
    def __init__(self, cfg, reverse):
        C = cfg["C"]
        self.C, self.reverse = C, reverse
        self.Dk, self.Dv, self.nh, self.scalar = cfg["Dk"], cfg["Dv"], cfg["nh"], cfg["scalar"]
        self.Wk, self.Wv = self.nh * self.Dk, self.nh * self.Dv
        self.nsg = cfg["H"] // self.nh
        nh, Wk, Wv = self.nh, self.Wk, self.Wv
        lk, lv, lc = _log2(self.Dk), _log2(self.Dv), _log2(C)
        r, c = _iota((C, C), 0), _iota((C, C), 1)
        self.L = ((c >= r) if reverse else (c <= r)).astype(F32)
        self.Lsuf = ((c <= r) if reverse else (c >= r)).astype(F32)
        i, j = _iota((C, nh * C), 0), _iota((C, nh * C), 1) & (C - 1)
        self.Mst = (j >= i) if reverse else (j <= i)
        self.Dj = (i == j).astype(F32)
        self.nb = 1 if (self.scalar or C == 64) else 3
        assert self.scalar or C in (64, 128)
        lanes = _iota((1, self.nb * Wk), 1) & (Wk - 1)
        self.km = [((lanes >> lk) == h).astype(F32) for h in range(nh)]
        self.vm = [((_iota((1, Wv), 1) >> lv) == h).astype(F32) for h in range(nh)]
        self.BD = ((_iota((Wv, Wk), 0) >> lv) == (_iota((Wv, Wk), 1) >> lk)).astype(F32)
        self.last = 0 if reverse else C - 1
        self.last_row = (_iota((C, 1), 0) == self.last).astype(F32)
        self.lk, self.lc = lk, lc
        self.H = cfg["H"]

    def gates(self, g):
        if not self.scalar:
            return _dot_sel(g, self.L, _NN, x_left=False), None
        G8 = _dot_sel(g, self.L, _NN, x_left=False)
        nk, ncol = self.H * self.Dk, self.H * self.C
        ek = (_iota((GPAD, nk), 0) == (_iota((GPAD, nk), 1) >> self.lk)).astype(F32)
        ec = (_iota((GPAD, ncol), 0) == (_iota((GPAD, ncol), 1) >> self.lc)).astype(F32)
        return _dot_sel(G8, ek, _NN), _dot_sel(G8, ec, _NN)

    def Ek(self, s):
        return (_iota((GPAD, self.Wk), 0) == (_iota((GPAD, self.Wk), 1) >> self.lk) + s * self.nh).astype(F32)

    def fold(self, x, factors=None):
        Wk = self.Wk
        out = None
        for b in range(self.nb):
            t = x[:, b * Wk:(b + 1) * Wk]
            t = t if factors is None or factors[b] is None else t * factors[b]
            out = t if out is None else out + t
        return out

    def kstack(self, x):
        return jnp.concatenate([x * self.km[h] for h in range(self.nh)], axis=0)

    def vstack(self, x):
        return jnp.concatenate([x * self.vm[h] for h in range(self.nh)], axis=0)

    def unstack(self, R, masks):
        C = self.C
        out = R[0:C] * masks[0]
        for h in range(1, self.nh):
            out = out + R[h * C:(h + 1) * C] * masks[h]
        return out

    def chunk(self, qs, ks, Gk, Gc):
        C = self.C
        Glast = Gk[self.last:self.last + 1, :]
        out = dict(Gk=Gk, Glast=Glast, eG=jnp.exp(Gk), eGl=jnp.exp(Glast - Gk), eGlast=jnp.exp(Glast))
        if self.scalar:
            Gr = jnp.sum(Gc * self.Dj, axis=0, keepdims=True)
            dec = jnp.where(self.Mst, jnp.exp(jnp.minimum(Gc - Gr, 0.0)), 0.0)
            qt, kt = qs, ks
            A = _dot(_bf(qt), _bf(self.kstack(kt)), _NT) * dec
            out.update(dec=dec, qt=qt, kt=kt, A=A, fq=[None], fk=[None])
        elif self.nb == 1:
            Gm = Gk[C // 2:C // 2 + 1, :]
            fq, fk = [jnp.exp(Gk - Gm)], [jnp.exp(Gm - Gk)]
            qt, kt = qs * fq[0], ks * fk[0]
            A = jnp.where(self.Mst, _dot(_bf(qt), _bf(self.kstack(kt)), _NT), 0.0)
            out.update(fq=fq, fk=fk, qt=qt, kt=kt, A=A)
        else:
            h = C // 2
            rows = _iota((C, 1), 0)
            early = (rows >= h) if self.reverse else (rows < h)
            late = jnp.logical_not(early)
            m_e, m_l, b = (h + h // 2, h // 2, h) if self.reverse else (h // 2, h + h // 2, h - 1)
            Ge, Gl, Gb = Gk[m_e:m_e + 1, :], Gk[m_l:m_l + 1, :], Gk[b:b + 1, :]

            def factor(mask, arg):
                return jnp.where(mask, jnp.exp(jnp.where(mask, arg, 0.0)), 0.0)

            fq = [factor(early, Gk - Ge), factor(late, Gk - Gl), factor(late, Gk - Gb)]
            fk = [factor(early, Ge - Gk), factor(late, Gl - Gk), factor(early, Gb - Gk)]
            qt = jnp.concatenate([qs * f for f in fq], axis=1)
            kt = jnp.concatenate([ks * f for f in fk], axis=1)
            A = jnp.where(self.Mst, _dot(_bf(qt), _bf(self.kstack(kt)), _NT), 0.0)
            out.update(fq=fq, fk=fk, qt=qt, kt=kt, A=A)
        return out


def _chunk_index(p, n, nc, reverse):
    if not reverse:
        return p
    return jnp.where(p < nc, nc - 1 - p, n - 1 + nc - p)


def _scan_dims(kind):
    cfg = _SCAN_CFG[kind]
    HK, HV = cfg["H"] * cfg["Dk"], cfg["H"] * cfg["Dv"]
    return cfg, cfg["C"], HK, HV, (GPAD if cfg["scalar"] else HK)


def _scan_fwd_step(m, q_ref, k_ref, v_ref, g_ref, o_ref, st_ref, S_ref):
    C = m.C

    @pl.when(pl.program_id(0) == 0)
    def _():
        S_ref[...] = jnp.zeros_like(S_ref)

    Gk_all, Gc_all = m.gates(g_ref[...])
    for s in range(m.nsg):
        ksl, vsl = slice(s * m.Wk, (s + 1) * m.Wk), slice(s * m.Wv, (s + 1) * m.Wv)
        csl = slice(s * m.nh * C, (s + 1) * m.nh * C)
        qs, ks, vs = q_ref[:, ksl], k_ref[:, ksl], v_ref[:, vsl]
        ch = m.chunk(qs, ks, Gk_all[:, ksl], Gc_all[:, csl] if m.scalar else None)
        S = S_ref[vsl, :]
        o = _dot(_bf(ch["A"]), _bf(m.vstack(vs)), _NN) + _dot(_bf(qs * ch["eG"]), _bf(S), _NT)
        o_ref[:, vsl] = o
        st_ref[0, vsl, :] = S
        S_ref[vsl, :] = S * ch["eGlast"] + _dot(_bf(vs), _bf(ks * ch["eGl"]), _TN) * m.BD


def _scan_fwd_call(kind, ops, Tc, comm=None):
    cfg, C, HK, HV, GW = _scan_dims(kind)
    T = ops[False][0][0].shape[0]
    n, nc = T // C, Tc // C

    def body(*refs):
        for d, rev in enumerate((False, True)):
            _scan_fwd_step(_ScanMath(cfg, rev), *refs[4 * d:4 * d + 4], *refs[8 + 2 * d:10 + 2 * d], refs[12 + d])

    sg = cfg["H"] // cfg["nh"]
    Wk, Wv = cfg["nh"] * cfg["Dk"], cfg["nh"] * cfg["Dv"]
    col = lambda rev, w, j: pl.BlockSpec((C, w), lambda p: (_chunk_index(p, n, nc, rev), j))
    st_spec = lambda rev: pl.BlockSpec((1, sg * Wv, Wk), lambda p: (_chunk_index(p, n, nc, rev), 0, 0))
    in_specs, args, out_specs, out_shape = [], [], [], []
    for rev in (False, True):
        q, k, v, g = ops[rev]
        in_specs += [col(rev, HK, q[1]), col(rev, HK, k[1]), col(rev, HV, v[1]), col(rev, GW, g[1])]
        args += [q[0], k[0], v[0], g[0]]
        out_specs += [col(rev, HV, 0), st_spec(rev)]
        out_shape += [jax.ShapeDtypeStruct((T, HV), F32), jax.ShapeDtypeStruct((n, sg * Wv, Wk), F32)]
    res, got = _pcall(body, name=f"scan_fwd_{kind}", out_shape=out_shape, grid=(n,), in_specs=in_specs,
                      out_specs=out_specs, scratch_shapes=[pltpu.VMEM((sg * Wv, Wk), F32)] * 2,
                      sem=("arbitrary",), args=args, comm=comm)
    return {False: (res[0], res[1]), True: (res[2], res[3])}, got


def _scan_bwd_step(m, need_dg, q_ref, k_ref, v_ref, g_ref, st_ref, do_ref, dq_ref, dk_ref, dv_ref, dg_ref, dS_ref):
    C = m.C

    @pl.when(pl.program_id(0) == 0)
    def _():
        dS_ref[...] = jnp.zeros_like(dS_ref)

    x8 = jnp.zeros((C, GPAD), F32)
    Gk_all, Gc_all = m.gates(g_ref[...])
    for s in range(m.nsg):
        ksl, vsl = slice(s * m.Wk, (s + 1) * m.Wk), slice(s * m.Wv, (s + 1) * m.Wv)
        csl = slice(s * m.nh * C, (s + 1) * m.nh * C)
        qs, ks, vs, dos = q_ref[:, ksl], k_ref[:, ksl], v_ref[:, vsl], do_ref[:, vsl]
        ch = m.chunk(qs, ks, Gk_all[:, ksl], Gc_all[:, csl] if m.scalar else None)
        S = st_ref[0, vsl, :]
        dS = dS_ref[vsl, :]
        A, qt, kt = ch["A"], ch["qt"], ch["kt"]
        dA = _dot(_bf(dos), _bf(m.vstack(vs)), _NT)
        dAm = dA * ch["dec"] if m.scalar else jnp.where(m.Mst, dA, 0.0)
        kst = _bf(m.kstack(kt))
        dv = m.unstack(_dot(_bf(A), _bf(dos), _TN), m.vm) + _dot(_bf(ks * ch["eGl"]), _bf(dS), _NT)
        dv_ref[:, vsl] = dv
        dq_i = _dot(_bf(dAm), kst, _NN)
        dq_x = ch["eG"] * _dot(_bf(dos), _bf(S), _NN)
        dq_ref[:, ksl] = m.fold(dq_i, ch["fq"]) + dq_x
        dk_i = m.unstack(_dot(_bf(dAm), _bf(qt), _TN), m.km)
        dk_x = ch["eGl"] * _dot(_bf(vs), _bf(dS), _NN)
        dk_ref[:, ksl] = m.fold(dk_i, ch["fk"]) + dk_x
        if need_dg:
            bnd = (ch["eGlast"] * jnp.sum(dS * S, axis=0, keepdims=True)
                   + jnp.sum(ks * dk_x, axis=0, keepdims=True))
            X = m.fold(_bf(qt).astype(F32) * dq_i - _bf(kt).astype(F32) * dk_i) + (qs * dq_x - ks * dk_x)
            X = X + m.last_row * bnd
            if m.scalar:
                x8 = x8 + _dot_sel(X, m.Ek(s), _NT)
            else:
                dg_ref[:, ksl] = _dot_sel(X, m.Lsuf, _NN, x_left=False)
        dS_ref[vsl, :] = dS * ch["eGlast"] + _dot(_bf(dos), _bf(qs * ch["eG"]), _TN) * m.BD
    if m.scalar:
        dg_ref[...] = _dot_sel(x8, m.Lsuf, _NN, x_left=False)
    elif not need_dg:
        dg_ref[...] = jnp.zeros_like(dg_ref)


def _scan_bwd_call(kind, need_dg, ops, st, do, Tc, comm=None):
    cfg, C, HK, HV, GW = _scan_dims(kind)
    T = ops[False][0][0].shape[0]
    n, nc = T // C, Tc // C

    def body(*refs):
        for d, rev in enumerate((False, True)):
            _scan_bwd_step(_ScanMath(cfg, rev), need_dg, *refs[6 * d:6 * d + 6], *refs[12 + 4 * d:16 + 4 * d],
                           refs[20 + d])

    sg = cfg["H"] // cfg["nh"]
    Wk, Wv = cfg["nh"] * cfg["Dk"], cfg["nh"] * cfg["Dv"]
    col = lambda rev, w, j: pl.BlockSpec((C, w), lambda p: (_chunk_index(n - 1 - p, n, nc, rev), j))
    st_spec = lambda rev: pl.BlockSpec((1, sg * Wv, Wk), lambda p: (_chunk_index(n - 1 - p, n, nc, rev), 0, 0))
    in_specs, args, out_specs, out_shape = [], [], [], []
    for rev in (False, True):
        q, k, v, g = ops[rev]
        in_specs += [col(rev, HK, q[1]), col(rev, HK, k[1]), col(rev, HV, v[1]), col(rev, GW, g[1]),
                     st_spec(rev), col(rev, HV, 0)]
        args += [q[0], k[0], v[0], g[0], st[rev], do]
        out_specs += [col(rev, HK, 0), col(rev, HK, 0), col(rev, HV, 0), col(rev, GW, 0)]
        out_shape += [jax.ShapeDtypeStruct((T, w), F32) for w in (HK, HK, HV, GW)]
    res, got = _pcall(body, name=f"scan_bwd_{kind}", out_shape=out_shape, grid=(n,), in_specs=in_specs,
                      out_specs=out_specs, scratch_shapes=[pltpu.VMEM((sg * Wv, Wk), F32)] * 2,
                      sem=("arbitrary",), args=args, comm=comm)
    return {False: res[0:4], True: res[4:8]}, got


def _prep_consts():
    r, c = _iota((256, 256), 0), _iota((256, 256), 1)
    first = (c & 63) < 32
    rope_perm = jnp.where(first, -(r == c + 32).astype(F32), (r == c - 32).astype(F32))
    sel_f = (_iota((128, GPAD), 0) == _iota((128, GPAD), 1) + 32).astype(F32)
    sel_b = (_iota((128, GPAD), 0) == _iota((128, GPAD), 1) + 40).astype(F32)
    ek = (_iota((GPAD, 1024), 0) == (_iota((GPAD, 1024), 1) >> 7)).astype(F32)
    return rope_perm, sel_f, sel_b, ek


def _prep_tile(misc, gq, rq, rk, bm, cm, cosE, sinE, Wg, gbias, dtbf, dtbb, nAf, nAb):
    rope_perm, sel_f, sel_b, ek = _prep_consts()
    logg = _log_sigmoid(_mm_bf(misc, Wg) + gbias) * (1.0 / GLA_TAU)
    a_gla = jnp.concatenate([gq * (32 ** -0.5), logg], axis=1)
    rot = lambda t: t * cosE + _sel(t, rope_perm) * sinE
    a_ret = jnp.concatenate([rot(rq * (64 ** -0.5)), rot(rk)], axis=1)
    dtf = _softplus(_sel(misc, sel_f) + dtbf)
    dtb = _softplus(_sel(misc, sel_b) + dtbb)
    rep = lambda t: jnp.concatenate([t[:, :128]] * 4 + [t[:, 128:]] * 4, axis=1)
    bmr = rep(bm)
    return a_gla, a_ret, rep(cm), bmr * _sel(dtf, ek), bmr * _sel(dtb, ek), dtf * nAf, dtb * nAb


def _prep_row_specs(tr):
    blk = lambda w, j: pl.BlockSpec((tr, w), lambda i: (i, j))
    return [blk(128, MISC // 128), blk(128, _ROFF["gla_q"] // 128), blk(256, _ROFF["ret_q"] // 256),
            blk(256, _ROFF["ret_k"] // 256), blk(256, 2), blk(256, 3), blk(256, 0), blk(256, 0)]


def _whole(a):
    return pl.BlockSpec(a.shape, lambda i: (0,) * a.ndim)


def _prep_fwd_call(Pr, u, cosE, sinE, pp, tr):
    T = Pr.shape[0]
    n_row = 8

    def body(*refs):
        outs = _prep_tile(*[r[...] for r in refs[:n_row + len(pp)]])
        for o_ref, o in zip(refs[n_row + len(pp):], outs):
            o_ref[...] = o

    widths = [384, 512, 1024, 1024, 1024, GPAD, GPAD]
    return pl.pallas_call(
        body, name="prep_fwd", grid=(T // tr,),
        out_shape=tuple(jax.ShapeDtypeStruct((T, w), F32) for w in widths),
        in_specs=_prep_row_specs(tr) + [_whole(p) for p in pp],
        out_specs=tuple(pl.BlockSpec((tr, w), lambda i: (i, 0)) for w in widths),
        compiler_params=_params(("parallel",)),
    )(Pr, Pr, Pr, Pr, u, u, cosE, sinE, *pp)


def _prep_bwd_call(Pr, u, cosE, sinE, pp, cts, tr, comm=None):
    T = Pr.shape[0]
    n_row, n_p = 8, len(pp)
    names = ["gla_dq_f", "gla_dq_b", "gla_dg_f", "gla_dg_b", "gla_dk_f", "gla_dk_b", "gla_dv_f", "gla_dv_b",
             "ret_dq_f", "ret_dq_b", "ret_dk_f", "ret_dk_b", "ret_dv_f", "ret_dv_b",
             "ssd_dq_f", "ssd_dq_b", "ssd_dk_f", "ssd_dk_b", "ssd_dg_f", "ssd_dg_b", "ssd_dv_f", "ssd_dv_b",
             "d_r", "d_z", "d_gr", "d_xs"]
    ct_arrays = [cts[n] for n in names]

    def body(*refs):
        ins = [r[...] for r in refs[:n_row + n_p]]
        c = {n: r[...] for n, r in zip(names, refs[n_row + n_p:n_row + n_p + len(names)])}
        dPr_ref, du_ref = refs[n_row + n_p + len(names):n_row + n_p + len(names) + 2]
        dp_refs = refs[n_row + n_p + len(names) + 2:]
        _, vjp = jax.vjp(_prep_tile, *ins)
        ct_out = (jnp.concatenate([c["gla_dq_f"] + c["gla_dq_b"], c["gla_dg_f"], c["gla_dg_b"]], axis=1),
                  jnp.concatenate([c["ret_dq_f"] + c["ret_dq_b"], c["ret_dk_f"] + c["ret_dk_b"]], axis=1),
                  c["ssd_dq_f"] + c["ssd_dq_b"], c["ssd_dk_f"], c["ssd_dk_b"], c["ssd_dg_f"], c["ssd_dg_b"])
        d = vjp(ct_out)
        d_misc, d_gq, d_rq, d_rk, d_bm, d_cm = d[:6]
        dPr_ref[...] = jnp.concatenate(
            [c["gla_dv_f"] + c["gla_dv_b"], c["d_r"], d_rq, d_rk, c["ret_dv_f"] + c["ret_dv_b"], c["d_gr"],
             c["d_z"], d_gq, c["gla_dk_f"] + c["gla_dk_b"], d_misc,
             jnp.zeros((d_misc.shape[0], R_W - MISC - 128), F32)], axis=1).astype(dPr_ref.dtype)
        du_ref[...] = jnp.concatenate([c["ssd_dv_f"] + c["ssd_dv_b"] + c["d_xs"], d_bm, d_cm], axis=1)

        @pl.when(pl.program_id(0) == 0)
        def _():
            for r in dp_refs:
                r[...] = jnp.zeros_like(r)

        for r, g in zip(dp_refs, d[n_row:]):
            r[...] += g

    row = lambda a: pl.BlockSpec((tr, a.shape[1]), lambda i: (i, 0))
    return _pcall(
        body, name="prep_bwd", grid=(T // tr,),
        out_shape=(jax.ShapeDtypeStruct((T, R_W), BF16), jax.ShapeDtypeStruct((T, 1024), F32))
        + tuple(jax.ShapeDtypeStruct(p.shape, F32) for p in pp),
        in_specs=_prep_row_specs(tr) + [_whole(p) for p in pp] + [row(a) for a in ct_arrays],
        out_specs=(pl.BlockSpec((tr, R_W), lambda i: (i, 0)), pl.BlockSpec((tr, 1024), lambda i: (i, 0)))
        + tuple(_whole(p) for p in pp),
        scratch_shapes=[], sem=("arbitrary",), args=(Pr, Pr, Pr, Pr, u, u, cosE, sinE, *pp, *ct_arrays), comm=comm)


def _post_tile(ogf, ogb, r, ysf, ysb, xs, z, orf, orb, gr, gla_n, dexp, ssd_n, ret_n):
    bd = ((_iota((256, 256), 0) >> 6) == (_iota((256, 256), 1) >> 6)).astype(F32)
    og = ogf + ogb
    gla = og * lax.rsqrt(_sel(og * og, bd) * (1.0 / 64) + RMS_EPS) * gla_n * _silu(r)
    t = (ysf + ysb + dexp * xs) * _silu(z)
    ssd = t * lax.rsqrt(jnp.mean(t * t, axis=-1, keepdims=True) + RMS_EPS) * ssd_n
    o = orf + orb
    oc = o - _sel(o, bd) * (1.0 / 64)
    ret = oc * lax.rsqrt(_sel(oc * oc, bd) * (1.0 / 64) + RMS_EPS) * ret_n * _silu(gr)
    return jnp.concatenate([gla, ssd, ret], axis=1)


def _post_row_specs(tr):
    blk = lambda w, j: pl.BlockSpec((tr, w), lambda i: (i, j))
    return [blk(256, 0), blk(256, 0), blk(256, _ROFF["gla_r"] // 256), blk(512, 0), blk(512, 0), blk(512, 0),
            blk(512, _ROFF["ssd_z"] // 512), blk(256, 0), blk(256, 0), blk(256, _ROFF["ret_g"] // 256)]


def _post_fwd_call(rows, qp, tr, comm=None):
    T = rows[0].shape[0]

    def body(*refs):
        refs[-1][...] = _post_tile(*[r[...] for r in refs[:-1]]).astype(refs[-1].dtype)

    res, got = _pcall(body, name="post_fwd", grid=(T // tr,), out_shape=[jax.ShapeDtypeStruct((T, D), BF16)],
                      in_specs=_post_row_specs(tr) + [_whole(p) for p in qp],
                      out_specs=[pl.BlockSpec((tr, D), lambda i: (i, 0))], scratch_shapes=[],
                      sem=("parallel",), args=(*rows, *qp), comm=comm)
    return res[0], got


def _post_bwd_call(rows, qp, dmixed, tr):
    T = rows[0].shape[0]
    n_in = 10 + len(qp)

    def body(*refs):
        ins = [r[...] for r in refs[:n_in]]
        _, vjp = jax.vjp(_post_tile, *ins)
        d = vjp(refs[n_in][...])
        outs = refs[n_in + 1:]
        for o_ref, g in zip(outs[:7], (d[0], d[3], d[7], d[2], d[6], d[9], d[5])):
            o_ref[...] = g.astype(o_ref.dtype)

        @pl.when(pl.program_id(0) == 0)
        def _():
            for r in outs[7:]:
                r[...] = jnp.zeros_like(r)

        for r, g in zip(outs[7:], d[10:]):
            r[...] += g

    widths = [256, 512, 256, 256, 512, 256, 512]
    dts = [BF16] * 3 + [F32] * 4
    return pl.pallas_call(
        body, name="post_bwd", grid=(T // tr,),
        out_shape=tuple(jax.ShapeDtypeStruct((T, w), dt) for w, dt in zip(widths, dts))
        + tuple(jax.ShapeDtypeStruct(p.shape, F32) for p in qp),
        in_specs=_post_row_specs(tr) + [_whole(p) for p in qp] + [pl.BlockSpec((tr, D), lambda i: (i, 0))],
        out_specs=tuple(pl.BlockSpec((tr, w), lambda i: (i, 0)) for w in widths) + tuple(_whole(p) for p in qp),
        compiler_params=_params(("arbitrary",)),
    )(*rows, *qp, dmixed)


def _mixer_scan_operands(Pr, u, a_gla, a_ret, cmr, kf, kb, g8f, g8b, lg):
    gk, gv = (Pr, _ROFF["gla_k"] // 128), (Pr, _ROFF["gla_v"] // 256)
    rv = (Pr, _ROFF["ret_v"] // 256)
    return {
        "gla": {False: ((a_gla, 0), gk, gv, (a_gla, 1)), True: ((a_gla, 0), gk, gv, (a_gla, 2))},
        "ret": {False: ((a_ret, 0), (a_ret, 1), rv, (lg, 0)), True: ((a_ret, 0), (a_ret, 1), rv, (lg, 0))},
        "ssd": {False: ((cmr, 0), (kf, 0), (u, 0), (g8f, 0)), True: ((cmr, 0), (kb, 0), (u, 0), (g8b, 0))},
    }


def _post_rows(o, Pr, u):
    return [o["gla"][False][0], o["gla"][True][0], Pr, o["ssd"][False][0], o["ssd"][True][0], u, Pr,
            o["ret"][False][0], o["ret"][True][0], Pr]


def _mixer_forward(Tc, Pr, Px, cn, pp, cw8, cb, qp, comm):
    cosE, sinE, lg = cn
    u = _conv_fwd_call(Px, cw8, cb, Tc)
    prep = _prep_fwd_call(Pr, u, cosE, sinE, pp, Tc)
    ops = _mixer_scan_operands(Pr, u, *prep, lg)
    o, got = {}, {}
    for kind in ops:
        o[kind], got[kind] = _scan_fwd_call(kind, ops[kind], Tc, comm.get(kind))
    mixed, got["post"] = _post_fwd_call(_post_rows(o, Pr, u), qp, Tc, comm.get("post"))
    return mixed, (u, prep, o), got


def _mixer_backward(Tc, Pr, Px, cn, pp, cw8, cb, qp, saved, dmixed, comm):
    cosE, sinE, lg = cn
    u, prep, o = saved
    post = _post_bwd_call(_post_rows(o, Pr, u), qp, dmixed, Tc)
    d_o = dict(gla=post[0], ssd=post[1], ret=post[2])
    cts = dict(d_r=post[3], d_z=post[4], d_gr=post[5], d_xs=post[6])
    ops = _mixer_scan_operands(Pr, u, *prep, lg)
    got = {}
    for kind in ops:
        st = {rev: o[kind][rev][1] for rev in (False, True)}
        res, got[kind] = _scan_bwd_call(kind, kind != "ret", ops[kind], st, d_o[kind], Tc, comm.get(kind))
        for rev, sfx in ((False, "_f"), (True, "_b")):
            for nm, a in zip(("_dq", "_dk", "_dv", "_dg"), res[rev]):
                cts[kind + nm + sfx] = a
    pb, got["prep"] = _prep_bwd_call(Pr, u, cosE, sinE, pp, cts, Tc, comm.get("prep"))
    dPx, dcw8, dcb = _conv_bwd_call(Px, cw8, cb, pb[1], Tc)
    return pb[0], dPx, tuple(pb[2:]), dcw8, dcb[0:1], tuple(post[7:]), got


def _mix_fwd(Tc, X, w, cn, nw, mods, pp, cw8, cb, qp, comm):
    h = _norm_fwd_call(X, nw[0], mods[0], mods[1], None, Tc, BF16)
    Px, Pr = _mm(h, w[0], trans_b=True, name="mm_fwd"), _mm(h, w[1], trans_b=True, name="mm_fwd")
    mixed, saved, got = _mixer_forward(Tc, Pr, Px, cn, pp, cw8, cb, qp, comm)
    M = _mm(mixed, w[2], name="mm_fwd")
    Xn = _norm_fwd_call(M, nw[1], mods[2], jnp.zeros_like(mods[2]), X, Tc)
    return Xn, (X, nw, mods, w, cn, pp, cw8, cb, qp, h, Px, Pr, mixed, saved, M), got


def _mix_bwd(Tc, res, dXn, comm):
    X, nw, mods, w, cn, pp, cw8, cb, qp, h, Px, Pr, mixed, saved, M = res
    dM, dnw1, da_post, _ = _norm_bwd_call(M, nw[1], mods[2], dXn, Tc, out_dtype=BF16)
    dmixed = _mm(dM, w[2], trans_b=True, name="mm_dx")
    dPr, dPx, dpp, dcw8, dcb, dqp, got = _mixer_backward(Tc, Pr, Px, cn, pp, cw8, cb, qp, saved, dmixed, comm)
    dh = _mm(dPx, w[0], name="mm_dx")
    dh = _mm(dPr, w[1], name="mm_dx_acc", add=dh)
    dX, dnw0, da_pre, db_pre = _norm_bwd_call(X, nw[0], mods[0], dh, Tc, add=dXn)
    dW = tuple(_mm_tn(a, g, name="mm_dw", out_dtype=BF16) for a, g in ((dPx, h), (dPr, h), (mixed, dM)))
    return dX, ((dnw0[0], dnw1[0]), (da_pre, db_pre, da_post), dpp, dcw8, dcb, dqp), dW, got


def _ffn_fwd(Tc, X, w, nw, mods):
    h = _norm_fwd_call(X, nw[0], mods[0], mods[1], None, Tc, BF16)
    U1 = _mm(h, w[0], trans_b=True, name="mm_fwd", out_dtype=BF16)
    U2 = _mm(h, w[1], trans_b=True, name="mm_fwd", out_dtype=BF16)
    act = _act_call(U1, U2)
    Fo = _mm(act, w[2], name="mm_fwd")
    Xn = _norm_fwd_call(Fo, nw[1], mods[2], jnp.zeros_like(mods[2]), X, Tc)
    return Xn, (X, nw, mods, w, h, U1, U2, act, Fo)


def _ffn_bwd(Tc, res, dXn):
    X, nw, mods, w, h, U1, U2, act, Fo = res
    dFo, dnw1, da_post, _ = _norm_bwd_call(Fo, nw[1], mods[2], dXn, Tc, out_dtype=BF16)
    dU1, dU2 = _act_call(U1, U2, _mm(dFo, w[2], trans_b=True, name="mm_dx", out_dtype=BF16))
    dh = _mm(dU1, w[0], name="mm_dx")
    dh = _mm(dU2, w[1], name="mm_dx_acc", add=dh)
    dX, dnw0, da_pre, db_pre = _norm_bwd_call(X, nw[0], mods[0], dh, Tc, add=dXn)
    dW = tuple(_mm_tn(a, g, name="mm_dw", out_dtype=BF16) for a, g in ((dU1, h), (dU2, h), (act, dFo)))
    return dX, ((dnw0[0], dnw1[0]), (da_pre, db_pre, da_post)), dW


def _rope_tables(Tl, Tc):
    rows = Tl // GRID_W
    row = jnp.repeat(jnp.arange(rows), GRID_W).astype(F32)
    col = jnp.tile(jnp.arange(GRID_W), rows).astype(F32)
    inv_freq = 10000.0 ** (-jnp.arange(16, dtype=F32) / 16)
    ang = jnp.concatenate([row[:, None] * inv_freq, col[:, None] * inv_freq], axis=-1)
    cos = jnp.concatenate([jnp.ones((Tc, 32), F32), jnp.cos(ang)], axis=0)
    sin = jnp.concatenate([jnp.zeros((Tc, 32), F32), jnp.sin(ang)], axis=0)
    return jnp.tile(cos, (1, 8)), jnp.tile(sin, (1, 8))


def _rows8(first, second):
    z = jnp.zeros((6,) + first.shape, F32)
    return jnp.concatenate([first[None], second[None], z], axis=0)


def _layer_inputs(sp, ml, mc):
    gu = sp["gla_gate_up"]
    Wg = jnp.zeros((128, 256), F32).at[0:16, 0:128].set(gu[0]).at[16:32, 128:256].set(gu[1])
    pp = (Wg, sp["gla_gate_b"].reshape(1, 256), sp["ssd_dt_bias"][0:1], sp["ssd_dt_bias"][1:2],
          -jnp.exp(sp["ssd_a_log"][0:1]), -jnp.exp(sp["ssd_a_log"][1:2]))
    qp = (sp["gla_norm"].reshape(1, 256), jnp.repeat(sp["ssd_d"], 64).reshape(1, 512),
          sp["ssd_norm"].reshape(1, 512), sp["ret_norm"].reshape(1, 256))
    mix = ((sp["norm_mix_pre"], sp["norm_mix_post"]),
           (_rows8(1.0 + mc[1], 1.0 + ml[1]), _rows8(mc[0], ml[0]), _rows8(mc[2], ml[2])),
           pp, jnp.pad(sp["ssd_conv_w"], ((0, 3), (0, 0))), sp["ssd_conv_b"].reshape(1, 1024), qp)
    ffn = ((sp["norm_ffn_pre"], sp["norm_ffn_post"]),
           (_rows8(1.0 + mc[4], 1.0 + ml[4]), _rows8(mc[3], ml[3]), _rows8(mc[5], ml[5])))
    return mix, ffn


def _rows_from(g):
    return g.reshape(N_DEV * g.shape[1], g.shape[2])


def _rows_to(f):
    return f.reshape(N_DEV, f.shape[0] // N_DEV, f.shape[1])


def _local_step(xcat, target, mod_l, mod_c, sp, Tc, weights=None, shards=None):
    Tt = xcat.shape[0]
    cosE, sinE = _rope_tables(Tt - Tc, Tc)
    log_gamma = jnp.log1p(-jnp.exp2(-5.0 - jnp.arange(4, dtype=F32)))
    lg = jnp.broadcast_to(jnp.concatenate([log_gamma, jnp.zeros((GPAD - 4,), F32)])[None, :], (Tt, GPAD))
    cn = (cosE, sinE, lg)
    dist = shards is not None
    X, saved = xcat, []
    if dist:
        g_in, g_out = _exchange_call("two", shards[0][:2], "gather_mix0")
    for l in range(DEPTH):
        (a_mix, a_ffn), pull = jax.vjp(_layer_inputs, {n: sp[n][l] for n in _SMALL},
                                       mod_l[l].reshape(6, D), mod_c[l].reshape(6, D))
        comm = {}
        if dist:
            w_in, w_out = _rows_from(g_in), _rows_from(g_out)
            more = l + 1 < DEPTH
            comm = dict(ssd=("two", [shards[l][2]] + ([shards[l + 1][1]] if more else [])),
                        ret=("two", [shards[l][3]]))
            if more:
                comm.update(gla=("two", [shards[l + 1][0]]))
        else:
            w_in, w_out, w13, w2 = weights[l]
        w_x, w_r = _split_w_in(w_in)
        X, r_mix, got = _mix_fwd(Tc, X, (w_x, w_r, w_out), cn, *a_mix, comm)
        if dist:
            w13, w2 = _rows_from(got["ssd"][0]), _rows_from(got["ret"][0])
            if more:
                g_in, g_out = got["gla"][0], got["ssd"][1]
        X, r_ffn = _ffn_fwd(Tc, X, (w13[:FFN_H], w13[FFN_H:], w2), *a_ffn)
        saved.append((r_mix, r_ffn, pull))
    loss, dX = _loss_call(X, target, Tc)
    d_sp, d_ml, d_mc = [None] * DEPTH, [None] * DEPTH, [None] * DEPTH
    gw = [[None] * 4 for _ in range(DEPTH)]
    nxt = None
    for l in reversed(range(DEPTH)):
        r_mix, r_ffn, pull = saved[l]
        dX, c_ffn, dW_ffn = _ffn_bwd(Tc, r_ffn, dX)
        g13, g2 = jnp.concatenate([dW_ffn[0], dW_ffn[1]], axis=0), dW_ffn[2]
        comm = {}
        if dist:
            comm = dict(ssd=(True, [_rows_to(g13)] + ([nxt[1]] if nxt is not None else [])),
                        ret=(True, [_rows_to(g2)]))
            if nxt is not None:
                comm.update(gla=(True, [nxt[0]]))
        dX, c_mix, dW_mix, got = _mix_bwd(Tc, r_mix, dX, comm)
        d_sp[l], d_ml[l], d_mc[l] = pull((c_mix, c_ffn))
        gin, gout = _merge_w_in(dW_mix[0], dW_mix[1]), dW_mix[2]
        if dist:
            gw[l][2], gw[l][3] = got["ssd"][0], got["ret"][0]
            if nxt is not None:
                gw[l + 1][0], gw[l + 1][1] = got["gla"][0], got["ssd"][1]
            nxt = (_rows_to(gin), _rows_to(gout))
        else:
            gw[l] = [gin, gout, g13, g2]
    if dist:
        gw[0][0], gw[0][1] = _exchange_call(True, list(nxt), "scatter_mix0")
    d_sp = {n: jnp.stack([d_sp[l][n] for l in range(DEPTH)]) for n in _SMALL}
    return (loss, dX, jnp.stack(d_ml).reshape(DEPTH, 6 * D), jnp.stack(d_mc).reshape(DEPTH, 6 * D), d_sp, gw)


def _sum8_call(slabs, name):
    _, R, Cc = slabs.shape
    tr = _pick(R, (512, 352, 256, 128, 64, 32, 16))

    def body(*refs):
        acc = refs[0][...].astype(F32)
        for r in refs[1:N_DEV]:
            acc = acc + r[...].astype(F32)
        refs[N_DEV][...] = acc

    return pl.pallas_call(
        body, name=name, grid=(R // tr,), out_shape=jax.ShapeDtypeStruct((R, Cc), F32),
        in_specs=[pl.BlockSpec((None, tr, Cc), lambda i, d=d: (d, i, 0)) for d in range(N_DEV)],
        out_specs=pl.BlockSpec((tr, Cc), lambda i: (i, 0)), compiler_params=_params(("parallel",)),
    )(*([slabs] * N_DEV))


def _loss_call(X, target, Tc):
    Tt, W = X.shape
    tr = Tc
    nt = Tt // tr

    def body(x_ref, t_ref, loss_ref, dx_ref, acc_ref):
        i = pl.program_id(0)

        @pl.when(i == 0)
        def _():
            acc_ref[...] = jnp.zeros_like(acc_ref)
            dx_ref[...] = jnp.zeros_like(dx_ref)

        @pl.when(i > 0)
        def _():
            e = x_ref[...] - t_ref[...]
            dx_ref[...] = e * (1.0 / W)
            acc_ref[...] += jnp.sum(e * e, axis=0, keepdims=True)

        @pl.when(i == nt - 1)
        def _():
            loss_ref[...] = jnp.full(loss_ref.shape, (0.5 / W) * jnp.sum(acc_ref[...]), F32)

    loss, dx = pl.pallas_call(
        body, name="loss",
        out_shape=(jax.ShapeDtypeStruct((8, 128), F32), jax.ShapeDtypeStruct((Tt, W), F32)),
        grid=(nt,),
        in_specs=[pl.BlockSpec((tr, W), lambda i: (i, 0)),
                  pl.BlockSpec((tr, W), lambda i: (jnp.maximum(i - 1, 0), 0))],
        out_specs=(pl.BlockSpec((8, 128), lambda i: (0, 0)), pl.BlockSpec((tr, W), lambda i: (i, 0))),
        scratch_shapes=[pltpu.VMEM((1, W), F32)],
        compiler_params=_params(("arbitrary",)),
    )(X, target)
    return loss[0, 0], dx


def _adamw_call(w, g, m, v, name):
    R, Cc = w.shape
    tr = _pick(R, (512, 352, 256, 128, 64, 32, 16, 8))
    c1 = 1.0 - ADAM_B1 ** ADAM_STEP
    c2 = 1.0 - ADAM_B2 ** ADAM_STEP

    def body(w_ref, g_ref, m_ref, v_ref, d_ref, nm_ref, nv_ref):
        gv = g_ref[...]
        nm = ADAM_B1 * m_ref[...] + (1.0 - ADAM_B1) * gv
        nv = ADAM_B2 * v_ref[...] + (1.0 - ADAM_B2) * (gv * gv)
        d_ref[...] = -ADAM_LR * ((nm / c1) / (jnp.sqrt(nv / c2) + ADAM_EPS) + ADAM_WD * w_ref[...])
        nm_ref[...] = nm
        nv_ref[...] = nv

    spec = pl.BlockSpec((tr, Cc), lambda i: (i, 0))
    sh = jax.ShapeDtypeStruct((R, Cc), F32)
    return pl.pallas_call(
        body, name=name, out_shape=(sh, sh, sh), grid=(R // tr,),
        in_specs=[spec] * 4, out_specs=(spec,) * 3, compiler_params=_params(("parallel",)),
    )(w, g, m, v)


def _sum_call(xs, name):
    R, Cc = xs[0].shape
    tr = _pick(R, (512, 352, 256, 128, 64, 32, 16))
    k = len(xs)

    def body(*refs):
        acc = refs[0][...]
        for r in refs[1:k]:
            acc = acc + r[...]
        refs[k][...] = acc

    spec = pl.BlockSpec((tr, Cc), lambda i: (i, 0))
    return pl.pallas_call(
        body, name=name, grid=(R // tr,), in_specs=[spec] * k, out_shape=jax.ShapeDtypeStruct((R, Cc), F32),
        out_specs=spec, compiler_params=_params(("parallel",)),
    )(*xs)


MESH = pl.DeviceIdType.MESH
ANY = pl.BlockSpec(memory_space=pl.ANY)


def _me():
    return lax.axis_index("x"), lax.axis_index("y"), lax.axis_index("c")


_FLIPS = [(0, 0, 1), (1, 0, 0), (0, 1, 0), (1, 1, 0), (1, 0, 1), (0, 1, 1), (1, 1, 1)]


def _exchange_copies(scatter, srcs, dsts, send_sems, recv_sems, loc_sems, arrivals):
    x, y, c = _me()
    me = 4 * x + 2 * y + c
    sends, recvs, local = [], [], []
    for a in range(len(srcs)):
        for k, (dx, dy, dc) in enumerate(_FLIPS):
            px, py, pc = (1 - x if dx else x), (1 - y if dy else y), (1 - c if dc else c)
            peer = 4 * px + 2 * py + pc
            src = srcs[a].at[peer] if scatter else srcs[a]
            for lst, slab in ((sends, me), (recvs, peer)) if arrivals else ((sends, me),):
                lst.append(pltpu.make_async_remote_copy(
                    src_ref=src, dst_ref=dsts[a].at[slab], send_sem=send_sems.at[a, k], recv_sem=recv_sems.at[a, k],
                    device_id=(px, py, pc), device_id_type=MESH))
        local.append(pltpu.make_async_copy(srcs[a].at[me] if scatter else srcs[a], dsts[a].at[me], loc_sems.at[a]))
    return sends, recvs, local


def _exchange_start(scatter, srcs, dsts, sems):
    sends, _, local = _exchange_copies(scatter, srcs, dsts, *sems, arrivals=False)
    for cp in local + sends:
        cp.start()


def _exchange_wait(scatter, srcs, dsts, sems):
    sends, recvs, local = _exchange_copies(scatter, srcs, dsts, *sems, arrivals=True)
    for cp in sends:
        cp.wait_send()
    for cp in recvs:
        cp.wait_recv()
    for cp in local:
        cp.wait()


def _exchange_shapes(scatter, srcs):
    return tuple(jax.ShapeDtypeStruct(((N_DEV,) + s.shape[-2:]), s.dtype) for s in srcs)


def _exchange_sems(n):
    return [pltpu.SemaphoreType.DMA((n, 7)), pltpu.SemaphoreType.DMA((n, 7)), pltpu.SemaphoreType.DMA((n,))]


def _exchange_call(scatter, srcs, name):
    n = len(srcs)

    def body(*refs):
        if scatter == "two":
            _two_level_gather_body(n, refs[:n], refs[n:2 * n], *refs[2 * n:])
        else:
            _exchange_start(scatter, refs[:n], refs[n:2 * n], refs[2 * n:])
            _exchange_wait(scatter, refs[:n], refs[n:2 * n], refs[2 * n:])

    return pl.pallas_call(body, name=name, out_shape=_exchange_shapes(scatter, srcs), in_specs=[ANY] * n,
                          out_specs=(ANY,) * n, scratch_shapes=_exchange_sems(n))(*srcs)


def _pcall(body, *, name, grid, in_specs, out_specs, out_shape, scratch_shapes, sem, args, comm=None):
    if comm is None:
        res = pl.pallas_call(body, name=name, grid=grid, in_specs=list(in_specs), out_specs=tuple(out_specs),
                             out_shape=tuple(out_shape), scratch_shapes=list(scratch_shapes),
                             compiler_params=_params(sem))(*args)
        return tuple(res), ()
    scatter, srcs = comm
    n_in, n_out, n_c, n_s = len(in_specs), len(out_specs), len(srcs), len(scratch_shapes)

    def carrier(*refs):
        ins, c_src = refs[:n_in], refs[n_in:n_in + n_c]
        outs = refs[n_in + n_c:n_in + n_c + n_out]
        c_dst = refs[n_in + n_c + n_out:n_in + 2 * n_c + n_out]
        scr = refs[n_in + 2 * n_c + n_out:n_in + 2 * n_c + n_out + n_s]
        first = pl.program_id(0) == 0
        last = pl.program_id(0) == grid[0] - 1
        for ax in range(1, len(grid)):
            first = jnp.logical_and(first, pl.program_id(ax) == 0)
            last = jnp.logical_and(last, pl.program_id(ax) == grid[ax] - 1)

        two_level = scatter == "two"

        @pl.when(first)
        def _():
            if two_level:
                _two_level_gather("start", n_c, c_src, c_dst, *refs[-3:])
            else:
                _exchange_start(scatter, c_src, c_dst, refs[-3:])

        body(*ins, *outs, *scr)

        if two_level:
            @pl.when(pl.program_id(0) == (3 * grid[0]) // 4)
            def _():
                _two_level_gather("pass", n_c, c_src, c_dst, *refs[-3:])

        @pl.when(last)
        def _():
            if two_level:
                _two_level_gather("finish", n_c, c_src, c_dst, *refs[-3:])
            else:
                _exchange_wait(scatter, c_src, c_dst, refs[-3:])

    res = pl.pallas_call(
        carrier, name=name + "_x", grid=grid, in_specs=list(in_specs) + [ANY] * n_c,
        out_specs=tuple(out_specs) + (ANY,) * n_c, out_shape=tuple(out_shape) + _exchange_shapes(scatter, srcs),
        scratch_shapes=list(scratch_shapes) + _exchange_sems(n_c),
        compiler_params=_params(("arbitrary",) * len(grid)))(*args, *srcs)
    return tuple(res[:n_out]), tuple(res[n_out:])


def _two_level_gather(phase, n_arr, x_refs, out_refs, send_sems, recv_sems, local_sems):
    x, y, c = _me()
    me, sibling = (x, y, c), (x, y, 1 - c)
    chips = [(1 - x, y), (x, 1 - y), (1 - x, 1 - y)]

    def slab(a, px, py, pc):
        return out_refs[a].at[4 * px + 2 * py + pc]

    def copy(a, k, block, to, src=None):
        return pltpu.make_async_remote_copy(
            src_ref=slab(a, *block) if src is None else src, dst_ref=slab(a, *block),
            send_sem=send_sems.at[a, k], recv_sem=recv_sems.at[a, k], device_id=to, device_id_type=MESH)

    def first(a):
        return [copy(a, 0, me, sibling, src=x_refs[a])] + [copy(a, 1 + j, me, (*chip, c), src=x_refs[a])
                                                            for j, chip in enumerate(chips)]

    if phase == "start":
        for a in range(n_arr):
            pltpu.make_async_copy(x_refs[a], slab(a, *me), local_sems.at[a]).start()
        for a in range(n_arr):
            for cp in first(a):
                cp.start()
    elif phase == "pass":
        for j, chip in enumerate(chips):
            for a in range(n_arr):
                copy(a, 1 + j, (*chip, c), me).wait_recv()
                copy(a, 4 + j, (*chip, c), sibling).start()
    else:
        for a in range(n_arr):
            copy(a, 0, sibling, me).wait_recv()
            for j, chip in enumerate(chips):
                copy(a, 4 + j, (*chip, 1 - c), me).wait_recv()
        for a in range(n_arr):
            for cp in first(a) + [copy(a, 4 + j, (*chip, c), sibling) for j, chip in enumerate(chips)]:
                cp.wait_send()
            pltpu.make_async_copy(x_refs[a], slab(a, *me), local_sems.at[a]).wait()


def _two_level_gather_body(n_arr, x_refs, out_refs, send_sems, recv_sems, local_sems):
    for phase in ("start", "pass", "finish"):
        _two_level_gather(phase, n_arr, x_refs, out_refs, send_sems, recv_sems, local_sems)


def _gather_small(x, name):
    def body(x_ref, out_ref, send_sems, recv_sems, local_sems):
        _two_level_gather_body(1, [x_ref], [out_ref], send_sems, recv_sems, local_sems)

    vm = pl.BlockSpec(memory_space=pltpu.VMEM)
    return pl.pallas_call(
        body, name=name,
        out_shape=jax.ShapeDtypeStruct((N_DEV,) + x.shape, x.dtype),
        in_specs=[vm], out_specs=vm,
        scratch_shapes=[pltpu.SemaphoreType.DMA((1, 7)), pltpu.SemaphoreType.DMA((1, 7)),
                        pltpu.SemaphoreType.DMA((1,))],
    )(x)


_SMALL = ["norm_mix_pre", "norm_mix_post", "norm_ffn_pre", "norm_ffn_post", "gla_gate_up", "gla_gate_b",
          "gla_norm", "ssd_conv_w", "ssd_conv_b", "ssd_dt_bias", "ssd_a_log", "ssd_d", "ssd_norm", "ret_norm"]


def _pack(arrs):
    flat = jnp.concatenate([a.reshape(-1) for a in arrs])
    n = flat.shape[0]
    npad = -(-n // 1024) * 1024
    return jnp.pad(flat, (0, npad - n)).reshape(npad // 128, 128)


def _unpack(buf, shapes):
    flat = buf.reshape(-1)
    out, o = [], 0
    for s in shapes:
        n = math.prod(s)
        out.append(flat[o:o + n].reshape(s))
        o += n
    return out


def kernel(x, c, ctx, c_ctx, ada_w, ada_b, norm_mix_pre, norm_mix_post, norm_ffn_pre, norm_ffn_post, w_in, w_out, gla_gate_up, gla_gate_b, gla_norm, ssd_conv_w, ssd_conv_b, ssd_dt_bias, ssd_a_log, ssd_d, ssd_norm, ret_norm, ffn_w13, ffn_w2, loss_target, m_c_ctx, m_ada_w, m_ada_b, m_norm_mix_pre, m_norm_mix_post, m_norm_ffn_pre, m_norm_ffn_post, m_w_in, m_w_out, m_gla_gate_up, m_gla_gate_b, m_gla_norm, m_ssd_conv_w, m_ssd_conv_b, m_ssd_dt_bias, m_ssd_a_log, m_ssd_d, m_ssd_norm, m_ret_norm, m_ffn_w13, m_ffn_w2, v_c_ctx, v_ada_w, v_ada_b, v_norm_mix_pre, v_norm_mix_post, v_norm_ffn_pre, v_norm_ffn_post, v_w_in, v_w_out, v_gla_gate_up, v_gla_gate_b, v_gla_norm, v_ssd_conv_w, v_ssd_conv_b, v_ssd_dt_bias, v_ssd_a_log, v_ssd_d, v_ssd_norm, v_ret_norm, v_ffn_w13, v_ffn_w2):
    P_ = dict(c_ctx=c_ctx, ada_w=ada_w, ada_b=ada_b, norm_mix_pre=norm_mix_pre, norm_mix_post=norm_mix_post,
              norm_ffn_pre=norm_ffn_pre, norm_ffn_post=norm_ffn_post, w_in=w_in, w_out=w_out,
              gla_gate_up=gla_gate_up, gla_gate_b=gla_gate_b, gla_norm=gla_norm, ssd_conv_w=ssd_conv_w,
              ssd_conv_b=ssd_conv_b, ssd_dt_bias=ssd_dt_bias, ssd_a_log=ssd_a_log, ssd_d=ssd_d,
              ssd_norm=ssd_norm, ret_norm=ret_norm, ffn_w13=ffn_w13, ffn_w2=ffn_w2)
    M_ = dict(c_ctx=m_c_ctx, ada_w=m_ada_w, ada_b=m_ada_b, norm_mix_pre=m_norm_mix_pre,
              norm_mix_post=m_norm_mix_post, norm_ffn_pre=m_norm_ffn_pre, norm_ffn_post=m_norm_ffn_post,
              w_in=m_w_in, w_out=m_w_out, gla_gate_up=m_gla_gate_up, gla_gate_b=m_gla_gate_b,
              gla_norm=m_gla_norm, ssd_conv_w=m_ssd_conv_w, ssd_conv_b=m_ssd_conv_b, ssd_dt_bias=m_ssd_dt_bias,
              ssd_a_log=m_ssd_a_log, ssd_d=m_ssd_d, ssd_norm=m_ssd_norm, ret_norm=m_ret_norm,
              ffn_w13=m_ffn_w13, ffn_w2=m_ffn_w2)
    V_ = dict(c_ctx=v_c_ctx, ada_w=v_ada_w, ada_b=v_ada_b, norm_mix_pre=v_norm_mix_pre,
              norm_mix_post=v_norm_mix_post, norm_ffn_pre=v_norm_ffn_pre, norm_ffn_post=v_norm_ffn_post,
              w_in=v_w_in, w_out=v_w_out, gla_gate_up=v_gla_gate_up, gla_gate_b=v_gla_gate_b,
              gla_norm=v_gla_norm, ssd_conv_w=v_ssd_conv_w, ssd_conv_b=v_ssd_conv_b, ssd_dt_bias=v_ssd_dt_bias,
              ssd_a_log=v_ssd_a_log, ssd_d=v_ssd_d, ssd_norm=v_ssd_norm, ret_norm=v_ret_norm,
              ffn_w13=v_ffn_w13, ffn_w2=v_ffn_w2)
    order = ["c_ctx", "ada_w", "ada_b", "norm_mix_pre", "norm_mix_post", "norm_ffn_pre", "norm_ffn_post", "w_in",
             "w_out", "gla_gate_up", "gla_gate_b", "gla_norm", "ssd_conv_w", "ssd_conv_b", "ssd_dt_bias",
             "ssd_a_log", "ssd_d", "ssd_norm", "ret_norm", "ffn_w13", "ffn_w2"]

    mx, my, mc_ = _me()
    me = 4 * mx + 2 * my + mc_
    Tl, Tc = x.shape[1], ctx.shape[1]
    n_in, n_out, n_13, n_2 = w_in.shape[2], w_out.shape[1], ffn_w13.shape[2], ffn_w2.shape[1]
    n_ada = ada_w.shape[2]

    shards = [[w_in[l].T.astype(BF16), w_out[l].astype(BF16), ffn_w13[l].T.astype(BF16), ffn_w2[l].astype(BF16)]
              for l in range(DEPTH)]

    cw = ssd_conv_w.shape[2]
    small_in = jnp.concatenate([jnp.pad(c, ((0, 7), (0, 0))).reshape(-1),
                                ssd_conv_w.reshape(-1)]).reshape(-1, 128)
    n_c_rows = 8 * D // 128
    small_in = jnp.pad(small_in, ((0, -small_in.shape[0] % 8), (0, 0)))
    gathered = _gather_small(small_in, "gather_c_conv")
    c_all = gathered[:, :n_c_rows].reshape(N_DEV, 8, D)[:, 0]
    conv_rows = DEPTH * 5 * cw // 128
    conv_full = gathered[:, n_c_rows:n_c_rows + conv_rows].reshape(N_DEV, DEPTH, 5, cw)
    conv_full = jnp.moveaxis(conv_full, 0, 2).reshape(DEPTH, 5, N_DEV * cw)
    c9 = jnp.concatenate([c_all, c_ctx[None], jnp.zeros((7, D), F32)], axis=0)
    s9 = c9 * jax.nn.sigmoid(c9)
    mod_piece = jnp.concatenate([_mm(s9, ada_w[l], name="mm_mod") for l in range(DEPTH)], axis=0)
    mod_g = _gather_small(mod_piece, "gather_mod")
    mod_all = jnp.moveaxis(mod_g.reshape(N_DEV, DEPTH, 16, n_ada), 0, 2).reshape(DEPTH, 16, N_DEV * n_ada)
    mod_all = mod_all + ada_b[:, None, :]
    mod_l = lax.dynamic_index_in_dim(mod_all, me, axis=1, keepdims=False)
    mod_c = mod_all[:, 8]

    sp = {n: P_[n] for n in _SMALL}
    sp["ssd_conv_w"] = conv_full
    xcat = jnp.concatenate([ctx[0], x[0]], axis=0)
    loss_local, d_xcat, d_mod_l, d_mod_c, d_sp, gw = _local_step(xcat, loss_target[0], mod_l, mod_c, sp, Tc,
                                                                 shards=shards)
    loss = lax.psum(loss_local, ("x", "y", "c"))
    grad_x = d_xcat[Tc:][None]

    G = {n: jnp.stack([_sum8_call(gw[l][a], f"sum_{n}") for l in range(DEPTH)])
         for a, n in enumerate(["w_in", "w_out", "ffn_w13", "ffn_w2"])}
    G["w_in"], G["ffn_w13"] = jnp.swapaxes(G["w_in"], 1, 2), jnp.swapaxes(G["ffn_w13"], 1, 2)

    dmod_rows = jnp.concatenate([d_mod_l, d_mod_c], axis=0)
    dmod_g = _gather_small(dmod_rows, "gather_dmod").reshape(N_DEV, 2, DEPTH, 6 * D)
    dl = jnp.moveaxis(dmod_g[:, 0], 0, 1)
    dc = dmod_g[:, 1, :, :]
    dc_tot = dc[0]
    for d_ in range(1, N_DEV):
        dc_tot = dc_tot + dc[d_]
    dmod9 = jnp.concatenate([dl, dc_tot[:, None, :], jnp.zeros((DEPTH, 7, 6 * D), F32)], axis=1)
    g_ada_b = dmod9[:, 0]
    for r_ in range(1, 9):
        g_ada_b = g_ada_b + dmod9[:, r_]
    dmod9_mine = lax.dynamic_slice_in_dim(dmod9, me * n_ada, n_ada, axis=2)
    s9T = jnp.pad(s9.T, ((0, 0), (0, 112)))
    g_ada_w = jnp.stack([_mm(s9T, jnp.pad(dmod9_mine[l], ((0, 112), (0, 0))), name="mm_dada")
                         for l in range(DEPTH)])
    ds9 = _mm(dmod9_mine[0], ada_w[0], trans_b=True, name="mm_ds9")
    for l in range(1, DEPTH):
        ds9 = _mm(dmod9_mine[l], ada_w[l], trans_b=True, name="mm_ds9_acc", add=ds9)
    ds_ctx_part = ds9[8]

    small_names = [n for n in _SMALL]
    small_parts = [d_sp[n] for n in small_names] + [ds_ctx_part]
    packed = _pack(small_parts)
    allp = _gather_small(packed, "gather_small_grads")
    summed = _sum_call([allp[d_] for d_ in range(N_DEV)], "sum_small_grads")
    parts = _unpack(summed, [p.shape for p in small_parts])
    for n, p in zip(small_names, parts[:-1]):
        G[n] = p
    sig = jax.nn.sigmoid(c_ctx)
    G["c_ctx"] = parts[-1] * (sig * (1.0 + c_ctx * (1.0 - sig)))
    G["ssd_conv_w"] = lax.dynamic_slice_in_dim(G["ssd_conv_w"], me * cw, cw, axis=2)
    G["ada_w"] = g_ada_w
    G["ada_b"] = g_ada_b

    delta, new_m, new_v = {}, {}, {}
    for n in ["ada_w", "w_in", "w_out", "ffn_w13", "ffn_w2"]:
        sh = P_[n].shape
        f2 = lambda a: a.reshape(sh[0] * sh[1], sh[2])
        d_, m_, v_ = _adamw_call(f2(P_[n]), f2(G[n]), f2(M_[n]), f2(V_[n]), f"adamw_{n}")
        delta[n], new_m[n], new_v[n] = d_.reshape(sh), m_.reshape(sh), v_.reshape(sh)
    rest = [n for n in order if n not in delta]
    shapes = [P_[n].shape for n in rest]
    d_, m_, v_ = _adamw_call(_pack([P_[n] for n in rest]), _pack([G[n] for n in rest]),
                             _pack([M_[n] for n in rest]), _pack([V_[n] for n in rest]), "adamw_small")
    for n, a, b, e in zip(rest, _unpack(d_, shapes), _unpack(m_, shapes), _unpack(v_, shapes)):
        delta[n], new_m[n], new_v[n] = a, b, e

    return (loss, grad_x, *[G[n] for n in order], *[delta[n] for n in order],
            *[new_m[n] for n in order], *[new_v[n] for n in order])
```

```python
import math

import jax
import jax.numpy as jnp
from jax import lax
from jax.experimental import pallas as pl
from jax.experimental.pallas import tpu as pltpu

F32 = jnp.float32
BF16 = jnp.bfloat16

D = 1024
DEPTH = 4
GRID_W = 64
RMS_EPS = 1e-6
GLA_TAU = 16.0
FFN_H = 2816
N_DEV = 8
ADAM_LR, ADAM_B1, ADAM_B2, ADAM_EPS, ADAM_WD, ADAM_STEP = 0.001, 0.9, 0.999, 1e-08, 0.01, 10

VMEM_LIMIT = 48 * 1024 * 1024

_ORIG = dict(gla_q=(0, 128), gla_k=(128, 128), gla_v=(256, 256), gla_r=(512, 256), gla_lr=(768, 32),
             ssd_z=(800, 512), ssd_xbc=(1312, 1024), ssd_dt=(2336, 16), ret_q=(2352, 256), ret_k=(2608, 256),
             ret_v=(2864, 256), ret_g=(3120, 256))
_R_ORDER = ["gla_v", "gla_r", "ret_q", "ret_k", "ret_v", "ret_g", "ssd_z", "gla_q", "gla_k", "gla_lr", "ssd_dt"]
R_W = 2560
_ROFF = {}
_o = 0
for _n in _R_ORDER:
    _ROFF[_n] = _o
    _o += _ORIG[_n][1]
MISC = _ROFF["gla_lr"]
assert MISC == 2304 and _o == 2352


def _split_w_in(wt):
    xs, xz = _ORIG["ssd_xbc"]
    parts = [wt[_ORIG[n][0]:_ORIG[n][0] + _ORIG[n][1]] for n in _R_ORDER]
    parts.append(jnp.zeros((R_W - _o,) + wt.shape[1:], wt.dtype))
    return wt[xs:xs + xz], jnp.concatenate(parts, axis=0)


def _merge_w_in(wx, wr):
    pieces = []
    for n, (s, z) in sorted(_ORIG.items(), key=lambda t: t[1][0]):
        pieces.append(wx if n == "ssd_xbc" else wr[_ROFF[n]:_ROFF[n] + z])
    return jnp.concatenate(pieces, axis=0)


def _pick(n, cands):
    for c in cands:
        if n % c == 0:
            return c
    return n


def _params(sem=None):
    kw = dict(vmem_limit_bytes=VMEM_LIMIT)
    if sem is not None:
        kw["dimension_semantics"] = sem
    return pltpu.CompilerParams(**kw)


def _iota(shape, dim):
    return lax.broadcasted_iota(jnp.int32, shape, dim)


def _dot(a, b, dims):
    return lax.dot_general(a, b, (dims, ((), ())), preferred_element_type=F32)


_NN = ((1,), (0,))
_NT = ((1,), (1,))
_TN = ((0,), (0,))


def _bf(x):
    return x.astype(BF16)


def _dot_sel(x, e, dims, x_left=True):
    eb = e.astype(BF16)
    hi = x.astype(BF16)
    r1 = x - hi.astype(F32)
    mid = r1.astype(BF16)
    lo = (r1 - mid.astype(F32)).astype(BF16)
    out = None
    for p in (hi, mid, lo):
        t = _dot(p, eb, dims) if x_left else _dot(eb, p, dims)
        out = t if out is None else out + t
    return out


@jax.custom_vjp
def _sel(x, e):
    return _dot_sel(x, e, _NN)


_sel.defvjp(lambda x, e: (_dot_sel(x, e, _NN), e), lambda e, g: (_dot_sel(g, e, _NT), jnp.zeros_like(e)))


def _sig(x):
    e = jnp.exp(-jnp.abs(x))
    r = 1.0 / (1.0 + e)
    return jnp.where(x >= 0, r, e * r)


@jax.custom_vjp
def _sigmoid(x):
    return _sig(x)


def _sigmoid_fwd(x):
    s = _sig(x)
    return s, s


_sigmoid.defvjp(_sigmoid_fwd, lambda s, g: (g * s * (1.0 - s),))


def _silu(x):
    return x * _sigmoid(x)


@jax.custom_vjp
def _softplus(x):
    return jnp.maximum(x, 0.0) + jnp.log(1.0 + jnp.exp(-jnp.abs(x)))


_softplus.defvjp(lambda x: (jnp.maximum(x, 0.0) + jnp.log(1.0 + jnp.exp(-jnp.abs(x))), x),
                 lambda x, g: (g * _sig(x),))


def _log_sigmoid(x):
    return -_softplus(-x)


@jax.custom_vjp
def _mm_bf(x, w):
    return _dot(_bf(x), _bf(w), _NN)


_mm_bf.defvjp(lambda x, w: (_dot(_bf(x), _bf(w), _NN), (x, w)),
              lambda r, g: (_dot(_bf(g), _bf(r[1]), _NT), _dot(_bf(r[0]), _bf(g), _TN)))


_TILE_M = (1088, 1024, 512, 256, 128, 64, 32, 16)
_TILE_N = (1408, 1280, 1024, 768, 512, 384, 256, 128)
_TILE_K = (1408, 1280, 1024, 768, 512, 384, 256, 128)


def _mm(a, b, *, trans_b=False, name, add=None, out_dtype=F32):
    M, K = a.shape
    N = b.shape[0] if trans_b else b.shape[1]
    assert (b.shape[1] if trans_b else b.shape[0]) == K
    tm, tn, tk = _pick(M, _TILE_M), _pick(N, _TILE_N), _pick(K, _TILE_K)
    nk = K // tk
    dims = _NT if trans_b else _NN
    has_add = add is not None

    def body(*refs):
        a_ref, b_ref = refs[0], refs[1]
        o_ref, acc_ref = refs[-2], refs[-1]
        k = pl.program_id(2)

        @pl.when(k == 0)
        def _():
            acc_ref[...] = refs[2][...] if has_add else jnp.zeros_like(acc_ref)

        acc_ref[...] += _dot(a_ref[...].astype(BF16), b_ref[...].astype(BF16), dims)

        @pl.when(k == nk - 1)
        def _():
            o_ref[...] = acc_ref[...].astype(o_ref.dtype)

    b_spec = (pl.BlockSpec((tn, tk), lambda i, j, k: (j, k)) if trans_b
              else pl.BlockSpec((tk, tn), lambda i, j, k: (k, j)))
    o_spec = pl.BlockSpec((tm, tn), lambda i, j, k: (i, j))
    return pl.pallas_call(
        body, name=name,
        out_shape=jax.ShapeDtypeStruct((M, N), out_dtype),
        grid=(M // tm, N // tn, nk),
        in_specs=[pl.BlockSpec((tm, tk), lambda i, j, k: (i, k)), b_spec] + ([o_spec] if has_add else []),
        out_specs=o_spec,
        scratch_shapes=[pltpu.VMEM((tm, tn), F32)],
        compiler_params=_params(("parallel", "parallel", "arbitrary")),
    )(*((a, b, add) if has_add else (a, b)))


def _mm_tn(a, g, *, name, out_dtype=F32):
    M, K = a.shape
    N = g.shape[1]
    tm, tk, tn = _pick(M, _TILE_M), _pick(K, _TILE_K), _pick(N, _TILE_N)
    nm = M // tm

    def body(a_ref, g_ref, o_ref, acc_ref):
        i = pl.program_id(2)

        @pl.when(i == 0)
        def _():
            acc_ref[...] = jnp.zeros_like(acc_ref)

        acc_ref[...] += _dot(a_ref[...].astype(BF16), g_ref[...].astype(BF16), _TN)

        @pl.when(i == nm - 1)
        def _():
            o_ref[...] = acc_ref[...].astype(o_ref.dtype)

    return pl.pallas_call(
        body, name=name,
        out_shape=jax.ShapeDtypeStruct((K, N), out_dtype),
        grid=(K // tk, N // tn, nm),
        in_specs=[pl.BlockSpec((tm, tk), lambda k, j, i: (i, k)), pl.BlockSpec((tm, tn), lambda k, j, i: (i, j))],
        out_specs=pl.BlockSpec((tk, tn), lambda k, j, i: (k, j)),
        scratch_shapes=[pltpu.VMEM((tk, tn), F32)],
        compiler_params=_params(("parallel", "parallel", "arbitrary")),
    )(a, g)


def _norm_fwd_call(x, w, a2, b2, res, tr, out_dtype=F32):
    T, W = x.shape
    has_res = res is not None

    def body(*refs):
        x_ref, w_ref, a_ref, b_ref = refs[:4]
        y_ref = refs[-1]
        seg = jnp.minimum(pl.program_id(0), 1)
        xv = x_ref[...]
        rstd = lax.rsqrt(jnp.mean(xv * xv, axis=-1, keepdims=True) + RMS_EPS)
        y = a_ref[pl.ds(seg, 1), :] * (xv * rstd * w_ref[...]) + b_ref[pl.ds(seg, 1), :]
        y_ref[...] = (y + refs[4][...] if has_res else y).astype(y_ref.dtype)

    row = pl.BlockSpec((tr, W), lambda i: (i, 0))
    small = pl.BlockSpec((8, W), lambda i: (0, 0))
    return pl.pallas_call(
        body, name="norm_fwd",
        out_shape=jax.ShapeDtypeStruct((T, W), out_dtype),
        grid=(T // tr,),
        in_specs=[row, pl.BlockSpec((1, W), lambda i: (0, 0)), small, small] + ([row] if has_res else []),
        out_specs=row,
        compiler_params=_params(("parallel",)),
    )(*((x, w.reshape(1, W), a2, b2) + ((res,) if has_res else ())))


def _norm_bwd_call(x, w, a2, dy, tr, add=None, out_dtype=F32):
    T, W = x.shape
    has_add = add is not None

    def body(*refs):
        x_ref, w_ref, a_ref, dy_ref = refs[:4]
        dx_ref, dw_ref, da_ref, db_ref = refs[-4:]
        i = pl.program_id(0)
        seg = jnp.minimum(i, 1)

        @pl.when(i == 0)
        def _():
            dw_ref[...] = jnp.zeros_like(dw_ref)
            da_ref[...] = jnp.zeros_like(da_ref)
            db_ref[...] = jnp.zeros_like(db_ref)

        xv = x_ref[...]
        g = dy_ref[...]
        wv = w_ref[...]
        rstd = lax.rsqrt(jnp.mean(xv * xv, axis=-1, keepdims=True) + RMS_EPS)
        xh = xv * rstd
        da_ref[pl.ds(seg, 1), :] += jnp.sum(g * (xh * wv), axis=0, keepdims=True)
        db_ref[pl.ds(seg, 1), :] += jnp.sum(g, axis=0, keepdims=True)
        gy = g * a_ref[pl.ds(seg, 1), :]
        dw_ref[0:1, :] += jnp.sum(gy * xh, axis=0, keepdims=True)
        gx = gy * wv
        dx = rstd * (gx - xh * jnp.mean(gx * xh, axis=-1, keepdims=True))
        dx_ref[...] = (dx + refs[4][...] if has_add else dx).astype(dx_ref.dtype)

    acc = jax.ShapeDtypeStruct((8, W), F32)
    acc_spec = pl.BlockSpec((8, W), lambda i: (0, 0))
    row = pl.BlockSpec((tr, W), lambda i: (i, 0))
    return pl.pallas_call(
        body, name="norm_bwd",
        out_shape=(jax.ShapeDtypeStruct((T, W), out_dtype), acc, acc, acc),
        grid=(T // tr,),
        in_specs=[row, pl.BlockSpec((1, W), lambda i: (0, 0)), acc_spec, row] + ([row] if has_add else []),
        out_specs=(row, acc_spec, acc_spec, acc_spec),
        compiler_params=_params(("arbitrary",)),
    )(*((x, w.reshape(1, W), a2, dy) + ((add,) if has_add else ())))


def _act_call(u1, u2, dact=None):
    T, W = u1.shape
    tr = _pick(T, (512, 256, 128, 64))
    tn = _pick(W, (1408, 512, 256, 128))
    spec = pl.BlockSpec((tr, tn), lambda i, j: (i, j))
    sh = jax.ShapeDtypeStruct((T, W), BF16)
    if dact is None:
        def body(a_ref, b_ref, o_ref):
            a = a_ref[...].astype(F32)
            o_ref[...] = (a * _sig(a) * b_ref[...].astype(F32)).astype(o_ref.dtype)

        return pl.pallas_call(body, name="act_fwd", out_shape=sh, grid=(T // tr, W // tn), in_specs=[spec, spec],
                              out_specs=spec, compiler_params=_params(("parallel", "parallel")))(u1, u2)

    def body(a_ref, b_ref, g_ref, da_ref, db_ref):
        a, g = a_ref[...].astype(F32), g_ref[...].astype(F32)
        s = _sig(a)
        da_ref[...] = (g * b_ref[...].astype(F32) * (s * (1.0 + a * (1.0 - s)))).astype(da_ref.dtype)
        db_ref[...] = (g * a * s).astype(db_ref.dtype)

    return pl.pallas_call(body, name="act_bwd", out_shape=(sh, sh), grid=(T // tr, W // tn),
                          in_specs=[spec, spec, spec], out_specs=(spec, spec),
                          compiler_params=_params(("parallel", "parallel")))(u1, u2, dact)


def _conv_specs(T, Wc, tr):
    hb, nt = tr // 8, T // tr
    row = pl.BlockSpec((tr, Wc), lambda i: (i, 0))
    prev = pl.BlockSpec((8, Wc), lambda i: (jnp.maximum(i * hb - 1, 0), 0))
    nxt = pl.BlockSpec((8, Wc), lambda i: (jnp.minimum((i + 1) * hb, T // 8 - 1), 0))
    return row, prev, nxt, nt


def _fill_ext(dst_ref, cur_ref, prev_ref, next_ref, i, nt, tr):
    has_prev = (i > 1).astype(F32)
    has_next = jnp.logical_and(i > 0, i < nt - 1).astype(F32)
    dst_ref[8:16, :] = prev_ref[...] * has_prev
    dst_ref[16:16 + tr, :] = cur_ref[...]
    dst_ref[16 + tr:24 + tr, :] = next_ref[...] * has_next


def _conv_fwd_call(px, w8, b, tr):
    T, Wc = px.shape
    row, prev, nxt, nt = _conv_specs(T, Wc, tr)

    def body(x_ref, xp_ref, xn_ref, w_ref, b_ref, u_ref, xe_ref):
        i = pl.program_id(0)

        @pl.when(i == 0)
        def _():
            xe_ref[...] = jnp.zeros_like(xe_ref)

        _fill_ext(xe_ref, x_ref, xp_ref, xn_ref, i, nt, tr)
        y = b_ref[...] + w_ref[0:1, :] * xe_ref[pl.ds(14, tr), :]
        for k in range(1, 5):
            y = y + w_ref[k:k + 1, :] * xe_ref[pl.ds(14 + k, tr), :]
        u_ref[...] = y * _sig(y)

    return pl.pallas_call(
        body, name="conv_fwd", out_shape=jax.ShapeDtypeStruct((T, Wc), F32), grid=(nt,),
        in_specs=[row, prev, nxt, pl.BlockSpec((8, Wc), lambda i: (0, 0)), pl.BlockSpec((1, Wc), lambda i: (0, 0))],
        out_specs=row, scratch_shapes=[pltpu.VMEM((tr + 32, Wc), F32)],
        compiler_params=_params(("arbitrary",)),
    )(px, px, px, w8, b)


def _conv_bwd_call(px, w8, b, du, tr):
    T, Wc = px.shape
    row, prev, nxt, nt = _conv_specs(T, Wc, tr)
    E = tr + 16

    def body(x_ref, xp_ref, xn_ref, g_ref, gp_ref, gn_ref, w_ref, b_ref, dx_ref, dw_ref, db_ref,
             xe_ref, ge_ref, dy_ref):
        i = pl.program_id(0)

        @pl.when(i == 0)
        def _():
            xe_ref[...] = jnp.zeros_like(xe_ref)
            ge_ref[...] = jnp.zeros_like(ge_ref)
            dy_ref[...] = jnp.zeros_like(dy_ref)
            dw_ref[...] = jnp.zeros_like(dw_ref)
            db_ref[...] = jnp.zeros_like(db_ref)

        _fill_ext(xe_ref, x_ref, xp_ref, xn_ref, i, nt, tr)
        _fill_ext(ge_ref, g_ref, gp_ref, gn_ref, i, nt, tr)
        y = b_ref[...] + w_ref[0:1, :] * xe_ref[pl.ds(6, E), :]
        for k in range(1, 5):
            y = y + w_ref[k:k + 1, :] * xe_ref[pl.ds(6 + k, E), :]
        s = _sig(y)
        dy = ge_ref[pl.ds(8, E), :] * (s * (1.0 + y * (1.0 - s)))
        dy_ref[pl.ds(8, E), :] = dy
        dx = w_ref[0:1, :] * dy_ref[pl.ds(18, tr), :]
        for k in range(1, 5):
            dx = dx + w_ref[k:k + 1, :] * dy_ref[pl.ds(18 - k, tr), :]
        dx_ref[...] = dx.astype(dx_ref.dtype)
        dyt = dy_ref[pl.ds(16, tr), :]
        db_ref[0:1, :] += jnp.sum(dyt, axis=0, keepdims=True)
        for k in range(5):
            dw_ref[k:k + 1, :] += jnp.sum(dyt * xe_ref[pl.ds(14 + k, tr), :], axis=0, keepdims=True)

    acc = jax.ShapeDtypeStruct((8, Wc), F32)
    acc_spec = pl.BlockSpec((8, Wc), lambda i: (0, 0))
    ext = pltpu.VMEM((tr + 32, Wc), F32)
    return pl.pallas_call(
        body, name="conv_bwd", out_shape=(jax.ShapeDtypeStruct((T, Wc), BF16), acc, acc), grid=(nt,),
        in_specs=[row, prev, nxt, row, prev, nxt, acc_spec, pl.BlockSpec((1, Wc), lambda i: (0, 0))],
        out_specs=(row, acc_spec, acc_spec), scratch_shapes=[ext, ext, ext],
        compiler_params=_params(("arbitrary",)),
    )(px, px, px, du, du, du, w8, b)


_SCAN_CFG = {
    "gla": dict(H=4, Dk=32, Dv=64, nh=4, scalar=False, C=128),
    "ssd": dict(H=8, Dk=128, Dv=64, nh=2, scalar=True, C=128),
    "ret": dict(H=4, Dk=64, Dv=64, nh=4, scalar=True, C=128),
}
GPAD = 8


def _log2(n):
    r = int(math.log2(n))
    assert 1 << r == n
    return r


class _ScanMath:
    def __init__(self, cfg, reverse):
        C = cfg["C"]
        self.C, self.reverse = C, reverse
        self.Dk, self.Dv, self.nh, self.scalar = cfg["Dk"], cfg["Dv"], cfg["nh"], cfg["scalar"]
        self.Wk, self.Wv = self.nh * self.Dk, self.nh * self.Dv
        self.nsg = cfg["H"] // self.nh
        nh, Wk, Wv = self.nh, self.Wk, self.Wv
        lk, lv, lc = _log2(self.Dk), _log2(self.Dv), _log2(C)
        r, c = _iota((C, C), 0), _iota((C, C), 1)
        self.L = ((c >= r) if reverse else (c <= r)).astype(F32)
        self.Lsuf = ((c <= r) if reverse else (c >= r)).astype(F32)
        i, j = _iota((C, nh * C), 0), _iota((C, nh * C), 1) & (C - 1)
        self.Mst = (j >= i) if reverse else (j <= i)
        self.Dj = (i == j).astype(F32)
        self.nb = 1 if (self.scalar or C == 64) else 3
        assert self.scalar or C in (64, 128)
        lanes = _iota((1, self.nb * Wk), 1) & (Wk - 1)
        self.km = [((lanes >> lk) == h).astype(F32) for h in range(nh)]
        self.vm = [((_iota((1, Wv), 1) >> lv) == h).astype(F32) for h in range(nh)]
        self.BD = ((_iota((Wv, Wk), 0) >> lv) == (_iota((Wv, Wk), 1) >> lk)).astype(F32)
        self.last = 0 if reverse else C - 1
        self.last_row = (_iota((C, 1), 0) == self.last).astype(F32)
        self.lk, self.lc = lk, lc
        self.H = cfg["H"]

    def gates(self, g):
        if not self.scalar:
            return _dot_sel(g, self.L, _NN, x_left=False), None
        G8 = _dot_sel(g, self.L, _NN, x_left=False)
        nk, ncol = self.H * self.Dk, self.H * self.C
        ek = (_iota((GPAD, nk), 0) == (_iota((GPAD, nk), 1) >> self.lk)).astype(F32)
        ec = (_iota((GPAD, ncol), 0) == (_iota((GPAD, ncol), 1) >> self.lc)).astype(F32)
        return _dot_sel(G8, ek, _NN), _dot_sel(G8, ec, _NN)

    def Ek(self, s):
        return (_iota((GPAD, self.Wk), 0) == (_iota((GPAD, self.Wk), 1) >> self.lk) + s * self.nh).astype(F32)

    def fold(self, x, factors=None):
        Wk = self.Wk
        out = None
        for b in range(self.nb):
            t = x[:, b * Wk:(b + 1) * Wk]
            t = t if factors is None or factors[b] is None else t * factors[b]
            out = t if out is None else out + t
        return out

    def kstack(self, x):
        return jnp.concatenate([x * self.km[h] for h in range(self.nh)], axis=0)

    def vstack(self, x):
        return jnp.concatenate([x * self.vm[h] for h in range(self.nh)], axis=0)

    def unstack(self, R, masks):
        C = self.C
        out = R[0:C] * masks[0]
        for h in range(1, self.nh):
            out = out + R[h * C:(h + 1) * C] * masks[h]
        return out

    def chunk(self, qs, ks, Gk, Gc):
        C = self.C
        Glast = Gk[self.last:self.last + 1, :]
        out = dict(Gk=Gk, Glast=Glast, eG=jnp.exp(Gk), eGl=jnp.exp(Glast - Gk), eGlast=jnp.exp(Glast))
        if self.scalar:
            Gr = jnp.sum(Gc * self.Dj, axis=0, keepdims=True)
            dec = jnp.where(self.Mst, jnp.exp(jnp.minimum(Gc - Gr, 0.0)), 0.0)
            qt, kt = qs, ks
            A = _dot(_bf(qt), _bf(self.kstack(kt)), _NT) * dec
            out.update(dec=dec, qt=qt, kt=kt, A=A, fq=[None], fk=[None])
        elif self.nb == 1:
            Gm = Gk[C // 2:C // 2 + 1, :]
            fq, fk = [jnp.exp(Gk - Gm)], [jnp.exp(Gm - Gk)]
            qt, kt = qs * fq[0], ks * fk[0]
            A = jnp.where(self.Mst, _dot(_bf(qt), _bf(self.kstack(kt)), _NT), 0.0)
            out.update(fq=fq, fk=fk, qt=qt, kt=kt, A=A)
        else:
            h = C // 2
            rows = _iota((C, 1), 0)
            early = (rows >= h) if self.reverse else (rows < h)
            late = jnp.logical_not(early)
            m_e, m_l, b = (h + h // 2, h // 2, h) if self.reverse else (h // 2, h + h // 2, h - 1)
            Ge, Gl, Gb = Gk[m_e:m_e + 1, :], Gk[m_l:m_l + 1, :], Gk[b:b + 1, :]

            def factor(mask, arg):
                return jnp.where(mask, jnp.exp(jnp.where(mask, arg, 0.0)), 0.0)

            fq = [factor(early, Gk - Ge), factor(late, Gk - Gl), factor(late, Gk - Gb)]
            fk = [factor(early, Ge - Gk), factor(late, Gl - Gk), factor(early, Gb - Gk)]
            qt = jnp.concatenate([qs * f for f in fq], axis=1)
            kt = jnp.concatenate([ks * f for f in fk], axis=1)
            A = jnp.where(self.Mst, _dot(_bf(qt), _bf(self.kstack(kt)), _NT), 0.0)
            out.update(fq=fq, fk=fk, qt=qt, kt=kt, A=A)
        return out


def _chunk_index(p, n, nc, reverse):
    if not reverse:
        return p
    return jnp.where(p < nc, nc - 1 - p, n - 1 + nc - p)


def _scan_dims(kind):
    cfg = _SCAN_CFG[kind]
    HK, HV = cfg["H"] * cfg["Dk"], cfg["H"] * cfg["Dv"]
    return cfg, cfg["C"], HK, HV, (GPAD if cfg["scalar"] else HK)


def _scan_fwd_step(m, q_ref, k_ref, v_ref, g_ref, o_ref, st_ref, S_ref):
    C = m.C

    @pl.when(pl.program_id(0) == 0)
    def _():
        S_ref[...] = jnp.zeros_like(S_ref)

    Gk_all, Gc_all = m.gates(g_ref[...])
    for s in range(m.nsg):
        ksl, vsl = slice(s * m.Wk, (s + 1) * m.Wk), slice(s * m.Wv, (s + 1) * m.Wv)
        csl = slice(s * m.nh * C, (s + 1) * m.nh * C)
        qs, ks, vs = q_ref[:, ksl], k_ref[:, ksl], v_ref[:, vsl]
        ch = m.chunk(qs, ks, Gk_all[:, ksl], Gc_all[:, csl] if m.scalar else None)
        S = S_ref[vsl, :]
        o = _dot(_bf(ch["A"]), _bf(m.vstack(vs)), _NN) + _dot(_bf(qs * ch["eG"]), _bf(S), _NT)
        o_ref[:, vsl] = o
        st_ref[0, vsl, :] = S
        S_ref[vsl, :] = S * ch["eGlast"] + _dot(_bf(vs), _bf(ks * ch["eGl"]), _TN) * m.BD


def _scan_fwd_call(kind, ops, Tc, comm=None):
    cfg, C, HK, HV, GW = _scan_dims(kind)
    T = ops[False][0][0].shape[0]
    n, nc = T // C, Tc // C

    def body(*refs):
        for d, rev in enumerate((False, True)):
            _scan_fwd_step(_ScanMath(cfg, rev), *refs[4 * d:4 * d + 4], *refs[8 + 2 * d:10 + 2 * d], refs[12 + d])

    sg = cfg["H"] // cfg["nh"]
    Wk, Wv = cfg["nh"] * cfg["Dk"], cfg["nh"] * cfg["Dv"]
    col = lambda rev, w, j: pl.BlockSpec((C, w), lambda p: (_chunk_index(p, n, nc, rev), j))
    st_spec = lambda rev: pl.BlockSpec((1, sg * Wv, Wk), lambda p: (_chunk_index(p, n, nc, rev), 0, 0))
    in_specs, args, out_specs, out_shape = [], [], [], []
    for rev in (False, True):
        q, k, v, g = ops[rev]
        in_specs += [col(rev, HK, q[1]), col(rev, HK, k[1]), col(rev, HV, v[1]), col(rev, GW, g[1])]
        args += [q[0], k[0], v[0], g[0]]
        out_specs += [col(rev, HV, 0), st_spec(rev)]
        out_shape += [jax.ShapeDtypeStruct((T, HV), F32), jax.ShapeDtypeStruct((n, sg * Wv, Wk), F32)]
    res, got = _pcall(body, name=f"scan_fwd_{kind}", out_shape=out_shape, grid=(n,), in_specs=in_specs,
                      out_specs=out_specs, scratch_shapes=[pltpu.VMEM((sg * Wv, Wk), F32)] * 2,
                      sem=("arbitrary",), args=args, comm=comm)
    return {False: (res[0], res[1]), True: (res[2], res[3])}, got


def _scan_bwd_step(m, need_dg, q_ref, k_ref, v_ref, g_ref, st_ref, do_ref, dq_ref, dk_ref, dv_ref, dg_ref, dS_ref):
    C = m.C

    @pl.when(pl.program_id(0) == 0)
    def _():
        dS_ref[...] = jnp.zeros_like(dS_ref)

    x8 = jnp.zeros((C, GPAD), F32)
    Gk_all, Gc_all = m.gates(g_ref[...])
    for s in range(m.nsg):
        ksl, vsl = slice(s * m.Wk, (s + 1) * m.Wk), slice(s * m.Wv, (s + 1) * m.Wv)
        csl = slice(s * m.nh * C, (s + 1) * m.nh * C)
        qs, ks, vs, dos = q_ref[:, ksl], k_ref[:, ksl], v_ref[:, vsl], do_ref[:, vsl]
        ch = m.chunk(qs, ks, Gk_all[:, ksl], Gc_all[:, csl] if m.scalar else None)
        S = st_ref[0, vsl, :]
        dS = dS_ref[vsl, :]
        A, qt, kt = ch["A"], ch["qt"], ch["kt"]
        dA = _dot(_bf(dos), _bf(m.vstack(vs)), _NT)
        dAm = dA * ch["dec"] if m.scalar else jnp.where(m.Mst, dA, 0.0)
        kst = _bf(m.kstack(kt))
        dv = m.unstack(_dot(_bf(A), _bf(dos), _TN), m.vm) + _dot(_bf(ks * ch["eGl"]), _bf(dS), _NT)
        dv_ref[:, vsl] = dv
        dq_i = _dot(_bf(dAm), kst, _NN)
        dq_x = ch["eG"] * _dot(_bf(dos), _bf(S), _NN)
        dq_ref[:, ksl] = m.fold(dq_i, ch["fq"]) + dq_x
        dk_i = m.unstack(_dot(_bf(dAm), _bf(qt), _TN), m.km)
        dk_x = ch["eGl"] * _dot(_bf(vs), _bf(dS), _NN)
        dk_ref[:, ksl] = m.fold(dk_i, ch["fk"]) + dk_x
        if need_dg:
            bnd = (ch["eGlast"] * jnp.sum(dS * S, axis=0, keepdims=True)
                   + jnp.sum(ks * dk_x, axis=0, keepdims=True))
            X = m.fold(_bf(qt).astype(F32) * dq_i - _bf(kt).astype(F32) * dk_i) + (qs * dq_x - ks * dk_x)
            X = X + m.last_row * bnd
            if m.scalar:
                x8 = x8 + _dot_sel(X, m.Ek(s), _NT)
            else:
                dg_ref[:, ksl] = _dot_sel(X, m.Lsuf, _NN, x_left=False)
        dS_ref[vsl, :] = dS * ch["eGlast"] + _dot(_bf(dos), _bf(qs * ch["eG"]), _TN) * m.BD
    if m.scalar:
        dg_ref[...] = _dot_sel(x8, m.Lsuf, _NN, x_left=False)
    elif not need_dg:
        dg_ref[...] = jnp.zeros_like(dg_ref)


def _scan_bwd_call(kind, need_dg, ops, st, do, Tc, comm=None):
    cfg, C, HK, HV, GW = _scan_dims(kind)
    T = ops[False][0][0].shape[0]
    n, nc = T // C, Tc // C

    def body(*refs):
        for d, rev in enumerate((False, True)):
            _scan_bwd_step(_ScanMath(cfg, rev), need_dg, *refs[6 * d:6 * d + 6], *refs[12 + 4 * d:16 + 4 * d],
                           refs[20 + d])

    sg = cfg["H"] // cfg["nh"]
    Wk, Wv = cfg["nh"] * cfg["Dk"], cfg["nh"] * cfg["Dv"]
    col = lambda rev, w, j: pl.BlockSpec((C, w), lambda p: (_chunk_index(n - 1 - p, n, nc, rev), j))
    st_spec = lambda rev: pl.BlockSpec((1, sg * Wv, Wk), lambda p: (_chunk_index(n - 1 - p, n, nc, rev), 0, 0))
    in_specs, args, out_specs, out_shape = [], [], [], []
    for rev in (False, True):
        q, k, v, g = ops[rev]
        in_specs += [col(rev, HK, q[1]), col(rev, HK, k[1]), col(rev, HV, v[1]), col(rev, GW, g[1]),
                     st_spec(rev), col(rev, HV, 0)]
        args += [q[0], k[0], v[0], g[0], st[rev], do]
        out_specs += [col(rev, HK, 0), col(rev, HK, 0), col(rev, HV, 0), col(rev, GW, 0)]
        out_shape += [jax.ShapeDtypeStruct((T, w), F32) for w in (HK, HK, HV, GW)]
    res, got = _pcall(body, name=f"scan_bwd_{kind}", out_shape=out_shape, grid=(n,), in_specs=in_specs,
                      out_specs=out_specs, scratch_shapes=[pltpu.VMEM((sg * Wv, Wk), F32)] * 2,
                      sem=("arbitrary",), args=args, comm=comm)
    return {False: res[0:4], True: res[4:8]}, got


def _prep_consts():
    r, c = _iota((256, 256), 0), _iota((256, 256), 1)
    first = (c & 63) < 32
    rope_perm = jnp.where(first, -(r == c + 32).astype(F32), (r == c - 32).astype(F32))
    sel_f = (_iota((128, GPAD), 0) == _iota((128, GPAD), 1) + 32).astype(F32)
    sel_b = (_iota((128, GPAD), 0) == _iota((128, GPAD), 1) + 40).astype(F32)
    ek = (_iota((GPAD, 1024), 0) == (_iota((GPAD, 1024), 1) >> 7)).astype(F32)
    return rope_perm, sel_f, sel_b, ek


def _prep_tile(misc, gq, rq, rk, bm, cm, cosE, sinE, Wg, gbias, dtbf, dtbb, nAf, nAb):
    rope_perm, sel_f, sel_b, ek = _prep_consts()
    logg = _log_sigmoid(_mm_bf(misc, Wg) + gbias) * (1.0 / GLA_TAU)
    a_gla = jnp.concatenate([gq * (32 ** -0.5), logg], axis=1)
    rot = lambda t: t * cosE + _sel(t, rope_perm) * sinE
    a_ret = jnp.concatenate([rot(rq * (64 ** -0.5)), rot(rk)], axis=1)
    dtf = _softplus(_sel(misc, sel_f) + dtbf)
    dtb = _softplus(_sel(misc, sel_b) + dtbb)
    rep = lambda t: jnp.concatenate([t[:, :128]] * 4 + [t[:, 128:]] * 4, axis=1)
    bmr = rep(bm)
    return a_gla, a_ret, rep(cm), bmr * _sel(dtf, ek), bmr * _sel(dtb, ek), dtf * nAf, dtb * nAb


def _prep_row_specs(tr):
    blk = lambda w, j: pl.BlockSpec((tr, w), lambda i: (i, j))
    return [blk(128, MISC // 128), blk(128, _ROFF["gla_q"] // 128), blk(256, _ROFF["ret_q"] // 256),
            blk(256, _ROFF["ret_k"] // 256), blk(256, 2), blk(256, 3), blk(256, 0), blk(256, 0)]


def _whole(a):
    return pl.BlockSpec(a.shape, lambda i: (0,) * a.ndim)


def _prep_fwd_call(Pr, u, cosE, sinE, pp, tr):
    T = Pr.shape[0]
    n_row = 8

    def body(*refs):
        outs = _prep_tile(*[r[...] for r in refs[:n_row + len(pp)]])
        for o_ref, o in zip(refs[n_row + len(pp):], outs):
            o_ref[...] = o

    widths = [384, 512, 1024, 1024, 1024, GPAD, GPAD]
    return pl.pallas_call(
        body, name="prep_fwd", grid=(T // tr,),
        out_shape=tuple(jax.ShapeDtypeStruct((T, w), F32) for w in widths),
        in_specs=_prep_row_specs(tr) + [_whole(p) for p in pp],
        out_specs=tuple(pl.BlockSpec((tr, w), lambda i: (i, 0)) for w in widths),
        compiler_params=_params(("parallel",)),
    )(Pr, Pr, Pr, Pr, u, u, cosE, sinE, *pp)


def _prep_bwd_call(Pr, u, cosE, sinE, pp, cts, tr, comm=None):
    T = Pr.shape[0]
    n_row, n_p = 8, len(pp)
    names = ["gla_dq_f", "gla_dq_b", "gla_dg_f", "gla_dg_b", "gla_dk_f", "gla_dk_b", "gla_dv_f", "gla_dv_b",
             "ret_dq_f", "ret_dq_b", "ret_dk_f", "ret_dk_b", "ret_dv_f", "ret_dv_b",
             "ssd_dq_f", "ssd_dq_b", "ssd_dk_f", "ssd_dk_b", "ssd_dg_f", "ssd_dg_b", "ssd_dv_f", "ssd_dv_b",
             "d_r", "d_z", "d_gr", "d_xs"]
    ct_arrays = [cts[n] for n in names]

    def body(*refs):
        ins = [r[...] for r in refs[:n_row + n_p]]
        c = {n: r[...] for n, r in zip(names, refs[n_row + n_p:n_row + n_p + len(names)])}
        dPr_ref, du_ref = refs[n_row + n_p + len(names):n_row + n_p + len(names) + 2]
        dp_refs = refs[n_row + n_p + len(names) + 2:]
        _, vjp = jax.vjp(_prep_tile, *ins)
        ct_out = (jnp.concatenate([c["gla_dq_f"] + c["gla_dq_b"], c["gla_dg_f"], c["gla_dg_b"]], axis=1),
                  jnp.concatenate([c["ret_dq_f"] + c["ret_dq_b"], c["ret_dk_f"] + c["ret_dk_b"]], axis=1),
                  c["ssd_dq_f"] + c["ssd_dq_b"], c["ssd_dk_f"], c["ssd_dk_b"], c["ssd_dg_f"], c["ssd_dg_b"])
        d = vjp(ct_out)
        d_misc, d_gq, d_rq, d_rk, d_bm, d_cm = d[:6]
        dPr_ref[...] = jnp.concatenate(
            [c["gla_dv_f"] + c["gla_dv_b"], c["d_r"], d_rq, d_rk, c["ret_dv_f"] + c["ret_dv_b"], c["d_gr"],
             c["d_z"], d_gq, c["gla_dk_f"] + c["gla_dk_b"], d_misc,
             jnp.zeros((d_misc.shape[0], R_W - MISC - 128), F32)], axis=1).astype(dPr_ref.dtype)
        du_ref[...] = jnp.concatenate([c["ssd_dv_f"] + c["ssd_dv_b"] + c["d_xs"], d_bm, d_cm], axis=1)

        @pl.when(pl.program_id(0) == 0)
        def _():
            for r in dp_refs:
                r[...] = jnp.zeros_like(r)

        for r, g in zip(dp_refs, d[n_row:]):
            r[...] += g

    row = lambda a: pl.BlockSpec((tr, a.shape[1]), lambda i: (i, 0))
    return _pcall(
        body, name="prep_bwd", grid=(T // tr,),
        out_shape=(jax.ShapeDtypeStruct((T, R_W), BF16), jax.ShapeDtypeStruct((T, 1024), F32))
        + tuple(jax.ShapeDtypeStruct(p.shape, F32) for p in pp),
        in_specs=_prep_row_specs(tr) + [_whole(p) for p in pp] + [row(a) for a in ct_arrays],
        out_specs=(pl.BlockSpec((tr, R_W), lambda i: (i, 0)), pl.BlockSpec((tr, 1024), lambda i: (i, 0)))
        + tuple(_whole(p) for p in pp),
        scratch_shapes=[], sem=("arbitrary",), args=(Pr, Pr, Pr, Pr, u, u, cosE, sinE, *pp, *ct_arrays), comm=comm)


def _post_tile(ogf, ogb, r, ysf, ysb, xs, z, orf, orb, gr, gla_n, dexp, ssd_n, ret_n):
    bd = ((_iota((256, 256), 0) >> 6) == (_iota((256, 256), 1) >> 6)).astype(F32)
    og = ogf + ogb
    gla = og * lax.rsqrt(_sel(og * og, bd) * (1.0 / 64) + RMS_EPS) * gla_n * _silu(r)
    t = (ysf + ysb + dexp * xs) * _silu(z)
    ssd = t * lax.rsqrt(jnp.mean(t * t, axis=-1, keepdims=True) + RMS_EPS) * ssd_n
    o = orf + orb
    oc = o - _sel(o, bd) * (1.0 / 64)
    ret = oc * lax.rsqrt(_sel(oc * oc, bd) * (1.0 / 64) + RMS_EPS) * ret_n * _silu(gr)
    return jnp.concatenate([gla, ssd, ret], axis=1)


def _post_row_specs(tr):
    blk = lambda w, j: pl.BlockSpec((tr, w), lambda i: (i, j))
    return [blk(256, 0), blk(256, 0), blk(256, _ROFF["gla_r"] // 256), blk(512, 0), blk(512, 0), blk(512, 0),
            blk(512, _ROFF["ssd_z"] // 512), blk(256, 0), blk(256, 0), blk(256, _ROFF["ret_g"] // 256)]


def _post_fwd_call(rows, qp, tr, comm=None):
    T = rows[0].shape[0]

    def body(*refs):
        refs[-1][...] = _post_tile(*[r[...] for r in refs[:-1]]).astype(refs[-1].dtype)

    res, got = _pcall(body, name="post_fwd", grid=(T // tr,), out_shape=[jax.ShapeDtypeStruct((T, D), BF16)],
                      in_specs=_post_row_specs(tr) + [_whole(p) for p in qp],
                      out_specs=[pl.BlockSpec((tr, D), lambda i: (i, 0))], scratch_shapes=[],
                      sem=("parallel",), args=(*rows, *qp), comm=comm)
    return res[0], got


def _post_bwd_call(rows, qp, dmixed, tr):
    T = rows[0].shape[0]
    n_in = 10 + len(qp)

    def body(*refs):
        ins = [r[...] for r in refs[:n_in]]
        _, vjp = jax.vjp(_post_tile, *ins)
        d = vjp(refs[n_in][...])
        outs = refs[n_in + 1:]
        for o_ref, g in zip(outs[:7], (d[0], d[3], d[7], d[2], d[6], d[9], d[5])):
            o_ref[...] = g.astype(o_ref.dtype)

        @pl.when(pl.program_id(0) == 0)
        def _():
            for r in outs[7:]:
                r[...] = jnp.zeros_like(r)

        for r, g in zip(outs[7:], d[10:]):
            r[...] += g

    widths = [256, 512, 256, 256, 512, 256, 512]
    dts = [BF16] * 3 + [F32] * 4
    return pl.pallas_call(
        body, name="post_bwd", grid=(T // tr,),
        out_shape=tuple(jax.ShapeDtypeStruct((T, w), dt) for w, dt in zip(widths, dts))
        + tuple(jax.ShapeDtypeStruct(p.shape, F32) for p in qp),
        in_specs=_post_row_specs(tr) + [_whole(p) for p in qp] + [pl.BlockSpec((tr, D), lambda i: (i, 0))],
        out_specs=tuple(pl.BlockSpec((tr, w), lambda i: (i, 0)) for w in widths) + tuple(_whole(p) for p in qp),
        compiler_params=_params(("arbitrary",)),
    )(*rows, *qp, dmixed)


def _mixer_scan_operands(Pr, u, a_gla, a_ret, cmr, kf, kb, g8f, g8b, lg):
    gk, gv = (Pr, _ROFF["gla_k"] // 128), (Pr, _ROFF["gla_v"] // 256)
    rv = (Pr, _ROFF["ret_v"] // 256)
    return {
        "gla": {False: ((a_gla, 0), gk, gv, (a_gla, 1)), True: ((a_gla, 0), gk, gv, (a_gla, 2))},
        "ret": {False: ((a_ret, 0), (a_ret, 1), rv, (lg, 0)), True: ((a_ret, 0), (a_ret, 1), rv, (lg, 0))},
        "ssd": {False: ((cmr, 0), (kf, 0), (u, 0), (g8f, 0)), True: ((cmr, 0), (kb, 0), (u, 0), (g8b, 0))},
    }


def _post_rows(o, Pr, u):
    return [o["gla"][False][0], o["gla"][True][0], Pr, o["ssd"][False][0], o["ssd"][True][0], u, Pr,
            o["ret"][False][0], o["ret"][True][0], Pr]


def _mixer_forward(Tc, Pr, Px, cn, pp, cw8, cb, qp, comm):
    cosE, sinE, lg = cn
    u = _conv_fwd_call(Px, cw8, cb, Tc)
    prep = _prep_fwd_call(Pr, u, cosE, sinE, pp, Tc)
    ops = _mixer_scan_operands(Pr, u, *prep, lg)
    o, got = {}, {}
    for kind in ops:
        o[kind], got[kind] = _scan_fwd_call(kind, ops[kind], Tc, comm.get(kind))
    mixed, got["post"] = _post_fwd_call(_post_rows(o, Pr, u), qp, Tc, comm.get("post"))
    return mixed, (u, prep, o), got


def _mixer_backward(Tc, Pr, Px, cn, pp, cw8, cb, qp, saved, dmixed, comm):
    cosE, sinE, lg = cn
    u, prep, o = saved
    post = _post_bwd_call(_post_rows(o, Pr, u), qp, dmixed, Tc)
    d_o = dict(gla=post[0], ssd=post[1], ret=post[2])
    cts = dict(d_r=post[3], d_z=post[4], d_gr=post[5], d_xs=post[6])
    ops = _mixer_scan_operands(Pr, u, *prep, lg)
    got = {}
    for kind in ops:
        st = {rev: o[kind][rev][1] for rev in (False, True)}
        res, got[kind] = _scan_bwd_call(kind, kind != "ret", ops[kind], st, d_o[kind], Tc, comm.get(kind))
        for rev, sfx in ((False, "_f"), (True, "_b")):
            for nm, a in zip(("_dq", "_dk", "_dv", "_dg"), res[rev]):
                cts[kind + nm + sfx] = a
    pb, got["prep"] = _prep_bwd_call(Pr, u, cosE, sinE, pp, cts, Tc, comm.get("prep"))
    dPx, dcw8, dcb = _conv_bwd_call(Px, cw8, cb, pb[1], Tc)
    return pb[0], dPx, tuple(pb[2:]), dcw8, dcb[0:1], tuple(post[7:]), got


def _mix_fwd(Tc, X, w, cn, nw, mods, pp, cw8, cb, qp, comm):
    h = _norm_fwd_call(X, nw[0], mods[0], mods[1], None, Tc, BF16)
    Px, Pr = _mm(h, w[0], trans_b=True, name="mm_fwd"), _mm(h, w[1], trans_b=True, name="mm_fwd")
    mixed, saved, got = _mixer_forward(Tc, Pr, Px, cn, pp, cw8, cb, qp, comm)
    M = _mm(mixed, w[2], name="mm_fwd")
    Xn = _norm_fwd_call(M, nw[1], mods[2], jnp.zeros_like(mods[2]), X, Tc)
    return Xn, (X, nw, mods, w, cn, pp, cw8, cb, qp, h, Px, Pr, mixed, saved, M), got


def _mix_bwd(Tc, res, dXn, comm):
    X, nw, mods, w, cn, pp, cw8, cb, qp, h, Px, Pr, mixed, saved, M = res
    dM, dnw1, da_post, _ = _norm_bwd_call(M, nw[1], mods[2], dXn, Tc, out_dtype=BF16)
    dmixed = _mm(dM, w[2], trans_b=True, name="mm_dx")
    dPr, dPx, dpp, dcw8, dcb, dqp, got = _mixer_backward(Tc, Pr, Px, cn, pp, cw8, cb, qp, saved, dmixed, comm)
    dh = _mm(dPx, w[0], name="mm_dx")
    dh = _mm(dPr, w[1], name="mm_dx_acc", add=dh)
    dX, dnw0, da_pre, db_pre = _norm_bwd_call(X, nw[0], mods[0], dh, Tc, add=dXn)
    dW = tuple(_mm_tn(a, g, name="mm_dw", out_dtype=BF16) for a, g in ((dPx, h), (dPr, h), (mixed, dM)))
    return dX, ((dnw0[0], dnw1[0]), (da_pre, db_pre, da_post), dpp, dcw8, dcb, dqp), dW, got


def _ffn_fwd(Tc, X, w, nw, mods):
    h = _norm_fwd_call(X, nw[0], mods[0], mods[1], None, Tc, BF16)
    U1 = _mm(h, w[0], trans_b=True, name="mm_fwd", out_dtype=BF16)
    U2 = _mm(h, w[1], trans_b=True, name="mm_fwd", out_dtype=BF16)
    act = _act_call(U1, U2)
    Fo = _mm(act, w[2], name="mm_fwd")
    Xn = _norm_fwd_call(Fo, nw[1], mods[2], jnp.zeros_like(mods[2]), X, Tc)
    return Xn, (X, nw, mods, w, h, U1, U2, act, Fo)


def _ffn_bwd(Tc, res, dXn):
    X, nw, mods, w, h, U1, U2, act, Fo = res
    dFo, dnw1, da_post, _ = _norm_bwd_call(Fo, nw[1], mods[2], dXn, Tc, out_dtype=BF16)
    dU1, dU2 = _act_call(U1, U2, _mm(dFo, w[2], trans_b=True, name="mm_dx", out_dtype=BF16))
    dh = _mm(dU1, w[0], name="mm_dx")
    dh = _mm(dU2, w[1], name="mm_dx_acc", add=dh)
    dX, dnw0, da_pre, db_pre = _norm_bwd_call(X, nw[0], mods[0], dh, Tc, add=dXn)
    dW = tuple(_mm_tn(a, g, name="mm_dw", out_dtype=BF16) for a, g in ((dU1, h), (dU2, h), (act, dFo)))
    return dX, ((dnw0[0], dnw1[0]), (da_pre, db_pre, da_post)), dW


def _rope_tables(Tl, Tc):
    rows = Tl // GRID_W
    row = jnp.repeat(jnp.arange(rows), GRID_W).astype(F32)
    col = jnp.tile(jnp.arange(GRID_W), rows).astype(F32)
    inv_freq = 10000.0 ** (-jnp.arange(16, dtype=F32) / 16)
    ang = jnp.concatenate([row[:, None] * inv_freq, col[:, None] * inv_freq], axis=-1)
    cos = jnp.concatenate([jnp.ones((Tc, 32), F32), jnp.cos(ang)], axis=0)
    sin = jnp.concatenate([jnp.zeros((Tc, 32), F32), jnp.sin(ang)], axis=0)
    return jnp.tile(cos, (1, 8)), jnp.tile(sin, (1, 8))


def _rows8(first, second):
    z = jnp.zeros((6,) + first.shape, F32)
    return jnp.concatenate([first[None], second[None], z], axis=0)


def _layer_inputs(sp, ml, mc):
    gu = sp["gla_gate_up"]
    Wg = jnp.zeros((128, 256), F32).at[0:16, 0:128].set(gu[0]).at[16:32, 128:256].set(gu[1])
    pp = (Wg, sp["gla_gate_b"].reshape(1, 256), sp["ssd_dt_bias"][0:1], sp["ssd_dt_bias"][1:2],
          -jnp.exp(sp["ssd_a_log"][0:1]), -jnp.exp(sp["ssd_a_log"][1:2]))
    qp = (sp["gla_norm"].reshape(1, 256), jnp.repeat(sp["ssd_d"], 64).reshape(1, 512),
          sp["ssd_norm"].reshape(1, 512), sp["ret_norm"].reshape(1, 256))
    mix = ((sp["norm_mix_pre"], sp["norm_mix_post"]),
           (_rows8(1.0 + mc[1], 1.0 + ml[1]), _rows8(mc[0], ml[0]), _rows8(mc[2], ml[2])),
           pp, jnp.pad(sp["ssd_conv_w"], ((0, 3), (0, 0))), sp["ssd_conv_b"].reshape(1, 1024), qp)
    ffn = ((sp["norm_ffn_pre"], sp["norm_ffn_post"]),
           (_rows8(1.0 + mc[4], 1.0 + ml[4]), _rows8(mc[3], ml[3]), _rows8(mc[5], ml[5])))
    return mix, ffn


def _rows_from(g):
    return g.reshape(N_DEV * g.shape[1], g.shape[2])


def _rows_to(f):
    return f.reshape(N_DEV, f.shape[0] // N_DEV, f.shape[1])


def _local_step(xcat, target, mod_l, mod_c, sp, Tc, weights=None, shards=None):
    Tt = xcat.shape[0]
    cosE, sinE = _rope_tables(Tt - Tc, Tc)
    log_gamma = jnp.log1p(-jnp.exp2(-5.0 - jnp.arange(4, dtype=F32)))
    lg = jnp.broadcast_to(jnp.concatenate([log_gamma, jnp.zeros((GPAD - 4,), F32)])[None, :], (Tt, GPAD))
    cn = (cosE, sinE, lg)
    dist = shards is not None
    X, saved = xcat, []
    if dist:
        g_in, g_out = _exchange_call("two", shards[0][:2], "gather_mix0")
    for l in range(DEPTH):
        (a_mix, a_ffn), pull = jax.vjp(_layer_inputs, {n: sp[n][l] for n in _SMALL},
                                       mod_l[l].reshape(6, D), mod_c[l].reshape(6, D))
        comm = {}
        if dist:
            w_in, w_out = _rows_from(g_in), _rows_from(g_out)
            more = l + 1 < DEPTH
            comm = dict(ssd=("two", [shards[l][2]] + ([shards[l + 1][1]] if more else [])),
                        ret=("two", [shards[l][3]]))
            if more:
                comm.update(gla=("two", [shards[l + 1][0]]))
        else:
            w_in, w_out, w13, w2 = weights[l]
        w_x, w_r = _split_w_in(w_in)
        X, r_mix, got = _mix_fwd(Tc, X, (w_x, w_r, w_out), cn, *a_mix, comm)
        if dist:
            w13, w2 = _rows_from(got["ssd"][0]), _rows_from(got["ret"][0])
            if more:
                g_in, g_out = got["gla"][0], got["ssd"][1]
        X, r_ffn = _ffn_fwd(Tc, X, (w13[:FFN_H], w13[FFN_H:], w2), *a_ffn)
        saved.append((r_mix, r_ffn, pull))
    loss, dX = _loss_call(X, target, Tc)
    d_sp, d_ml, d_mc = [None] * DEPTH, [None] * DEPTH, [None] * DEPTH
    gw = [[None] * 4 for _ in range(DEPTH)]
    nxt = None
    for l in reversed(range(DEPTH)):
        r_mix, r_ffn, pull = saved[l]
        dX, c_ffn, dW_ffn = _ffn_bwd(Tc, r_ffn, dX)
        g13, g2 = jnp.concatenate([dW_ffn[0], dW_ffn[1]], axis=0), dW_ffn[2]
        comm = {}
        if dist:
            comm = dict(ssd=(True, [_rows_to(g13)] + ([nxt[1]] if nxt is not None else [])),
                        ret=(True, [_rows_to(g2)]))
            if nxt is not None:
                comm.update(gla=(True, [nxt[0]]))
        dX, c_mix, dW_mix, got = _mix_bwd(Tc, r_mix, dX, comm)
        d_sp[l], d_ml[l], d_mc[l] = pull((c_mix, c_ffn))
        gin, gout = _merge_w_in(dW_mix[0], dW_mix[1]), dW_mix[2]
        if dist:
            gw[l][2], gw[l][3] = got["ssd"][0], got["ret"][0]
            if nxt is not None:
                gw[l + 1][0], gw[l + 1][1] = got["gla"][0], got["ssd"][1]
            nxt = (_rows_to(gin), _rows_to(gout))
        else:
            gw[l] = [gin, gout, g13, g2]
    if dist:
        gw[0][0], gw[0][1] = _exchange_call(True, list(nxt), "scatter_mix0")
    d_sp = {n: jnp.stack([d_sp[l][n] for l in range(DEPTH)]) for n in _SMALL}
    return (loss, dX, jnp.stack(d_ml).reshape(DEPTH, 6 * D), jnp.stack(d_mc).reshape(DEPTH, 6 * D), d_sp, gw)


def _sum8_call(slabs, name):
    _, R, Cc = slabs.shape
    tr = _pick(R, (512, 352, 256, 128, 64, 32, 16))

    def body(*refs):
        acc = refs[0][...].astype(F32)
        for r in refs[1:N_DEV]:
            acc = acc + r[...].astype(F32)
        refs[N_DEV][...] = acc

    return pl.pallas_call(
        body, name=name, grid=(R // tr,), out_shape=jax.ShapeDtypeStruct((R, Cc), F32),
        in_specs=[pl.BlockSpec((None, tr, Cc), lambda i, d=d: (d, i, 0)) for d in range(N_DEV)],
        out_specs=pl.BlockSpec((tr, Cc), lambda i: (i, 0)), compiler_params=_params(("parallel",)),
    )(*([slabs] * N_DEV))


def _loss_call(X, target, Tc):
    Tt, W = X.shape
    tr = Tc
    nt = Tt // tr

    def body(x_ref, t_ref, loss_ref, dx_ref, acc_ref):
        i = pl.program_id(0)

        @pl.when(i == 0)
        def _():
            acc_ref[...] = jnp.zeros_like(acc_ref)
            dx_ref[...] = jnp.zeros_like(dx_ref)

        @pl.when(i > 0)
        def _():
            e = x_ref[...] - t_ref[...]
            dx_ref[...] = e * (1.0 / W)
            acc_ref[...] += jnp.sum(e * e, axis=0, keepdims=True)

        @pl.when(i == nt - 1)
        def _():
            loss_ref[...] = jnp.full(loss_ref.shape, (0.5 / W) * jnp.sum(acc_ref[...]), F32)

    loss, dx = pl.pallas_call(
        body, name="loss",
        out_shape=(jax.ShapeDtypeStruct((8, 128), F32), jax.ShapeDtypeStruct((Tt, W), F32)),
        grid=(nt,),
        in_specs=[pl.BlockSpec((tr, W), lambda i: (i, 0)),
                  pl.BlockSpec((tr, W), lambda i: (jnp.maximum(i - 1, 0), 0))],
        out_specs=(pl.BlockSpec((8, 128), lambda i: (0, 0)), pl.BlockSpec((tr, W), lambda i: (i, 0))),
        scratch_shapes=[pltpu.VMEM((1, W), F32)],
        compiler_params=_params(("arbitrary",)),
    )(X, target)
    return loss[0, 0], dx


def _adamw_call(w, g, m, v, name):
    R, Cc = w.shape
    tr = _pick(R, (512, 352, 256, 128, 64, 32, 16, 8))
    c1 = 1.0 - ADAM_B1 ** ADAM_STEP
    c2 = 1.0 - ADAM_B2 ** ADAM_STEP

    def body(w_ref, g_ref, m_ref, v_ref, d_ref, nm_ref, nv_ref):
        gv = g_ref[...]
        nm = ADAM_B1 * m_ref[...] + (1.0 - ADAM_B1) * gv
        nv = ADAM_B2 * v_ref[...] + (1.0 - ADAM_B2) * (gv * gv)
        d_ref[...] = -ADAM_LR * ((nm / c1) / (jnp.sqrt(nv / c2) + ADAM_EPS) + ADAM_WD * w_ref[...])
        nm_ref[...] = nm
        nv_ref[...] = nv

    spec = pl.BlockSpec((tr, Cc), lambda i: (i, 0))
    sh = jax.ShapeDtypeStruct((R, Cc), F32)
    return pl.pallas_call(
        body, name=name, out_shape=(sh, sh, sh), grid=(R // tr,),
        in_specs=[spec] * 4, out_specs=(spec,) * 3, compiler_params=_params(("parallel",)),
    )(w, g, m, v)


def _sum_call(xs, name):
    R, Cc = xs[0].shape
    tr = _pick(R, (512, 352, 256, 128, 64, 32, 16))
    k = len(xs)

    def body(*refs):
        acc = refs[0][...]
        for r in refs[1:k]:
            acc = acc + r[...]
        refs[k][...] = acc

    spec = pl.BlockSpec((tr, Cc), lambda i: (i, 0))
    return pl.pallas_call(
        body, name=name, grid=(R // tr,), in_specs=[spec] * k, out_shape=jax.ShapeDtypeStruct((R, Cc), F32),
        out_specs=spec, compiler_params=_params(("parallel",)),
    )(*xs)


MESH = pl.DeviceIdType.MESH
ANY = pl.BlockSpec(memory_space=pl.ANY)


def _me():
    return lax.axis_index("x"), lax.axis_index("y"), lax.axis_index("c")


_FLIPS = [(0, 0, 1), (1, 0, 0), (0, 1, 0), (1, 1, 0), (1, 0, 1), (0, 1, 1), (1, 1, 1)]


def _exchange_copies(scatter, srcs, dsts, send_sems, recv_sems, loc_sems, arrivals):
    x, y, c = _me()
    me = 4 * x + 2 * y + c
    sends, recvs, local = [], [], []
    for a in range(len(srcs)):
        for k, (dx, dy, dc) in enumerate(_FLIPS):
            px, py, pc = (1 - x if dx else x), (1 - y if dy else y), (1 - c if dc else c)
            peer = 4 * px + 2 * py + pc
            src = srcs[a].at[peer] if scatter else srcs[a]
            for lst, slab in ((sends, me), (recvs, peer)) if arrivals else ((sends, me),):
                lst.append(pltpu.make_async_remote_copy(
                    src_ref=src, dst_ref=dsts[a].at[slab], send_sem=send_sems.at[a, k], recv_sem=recv_sems.at[a, k],
                    device_id=(px, py, pc), device_id_type=MESH))
        local.append(pltpu.make_async_copy(srcs[a].at[me] if scatter else srcs[a], dsts[a].at[me], loc_sems.at[a]))
    return sends, recvs, local


def _exchange_start(scatter, srcs, dsts, sems):
    sends, _, local = _exchange_copies(scatter, srcs, dsts, *sems, arrivals=False)
    for cp in local + sends:
        cp.start()


def _exchange_wait(scatter, srcs, dsts, sems):
    sends, recvs, local = _exchange_copies(scatter, srcs, dsts, *sems, arrivals=True)
    for cp in sends:
        cp.wait_send()
    for cp in recvs:
        cp.wait_recv()
    for cp in local:
        cp.wait()


def _exchange_shapes(scatter, srcs):
    return tuple(jax.ShapeDtypeStruct(((N_DEV,) + s.shape[-2:]), s.dtype) for s in srcs)


def _exchange_sems(n):
    return [pltpu.SemaphoreType.DMA((n, 7)), pltpu.SemaphoreType.DMA((n, 7)), pltpu.SemaphoreType.DMA((n,))]


def _exchange_call(scatter, srcs, name):
    n = len(srcs)

    def body(*refs):
        if scatter == "two":
            _two_level_gather_body(n, refs[:n], refs[n:2 * n], *refs[2 * n:])
        else:
            _exchange_start(scatter, refs[:n], refs[n:2 * n], refs[2 * n:])
            _exchange_wait(scatter, refs[:n], refs[n:2 * n], refs[2 * n:])

    return pl.pallas_call(body, name=name, out_shape=_exchange_shapes(scatter, srcs), in_specs=[ANY] * n,
                          out_specs=(ANY,) * n, scratch_shapes=_exchange_sems(n))(*srcs)


def _pcall(body, *, name, grid, in_specs, out_specs, out_shape, scratch_shapes, sem, args, comm=None):
    if comm is None:
        res = pl.pallas_call(body, name=name, grid=grid, in_specs=list(in_specs), out_specs=tuple(out_specs),
                             out_shape=tuple(out_shape), scratch_shapes=list(scratch_shapes),
                             compiler_params=_params(sem))(*args)
        return tuple(res), ()
    scatter, srcs = comm
    n_in, n_out, n_c, n_s = len(in_specs), len(out_specs), len(srcs), len(scratch_shapes)

    def carrier(*refs):
        ins, c_src = refs[:n_in], refs[n_in:n_in + n_c]
        outs = refs[n_in + n_c:n_in + n_c + n_out]
        c_dst = refs[n_in + n_c + n_out:n_in + 2 * n_c + n_out]
        scr = refs[n_in + 2 * n_c + n_out:n_in + 2 * n_c + n_out + n_s]
        first = pl.program_id(0) == 0
        last = pl.program_id(0) == grid[0] - 1
        for ax in range(1, len(grid)):
            first = jnp.logical_and(first, pl.program_id(ax) == 0)
            last = jnp.logical_and(last, pl.program_id(ax) == grid[ax] - 1)

        two_level = scatter == "two"

        @pl.when(first)
        def _():
            if two_level:
                _two_level_gather("start", n_c, c_src, c_dst, *refs[-3:])
            else:
                _exchange_start(scatter, c_src, c_dst, refs[-3:])

        body(*ins, *outs, *scr)

        if two_level:
            @pl.when(pl.program_id(0) == (3 * grid[0]) // 4)
            def _():
                _two_level_gather("pass", n_c, c_src, c_dst, *refs[-3:])

        @pl.when(last)
        def _():
            if two_level:
                _two_level_gather("finish", n_c, c_src, c_dst, *refs[-3:])
            else:
                _exchange_wait(scatter, c_src, c_dst, refs[-3:])

    res = pl.pallas_call(
        carrier, name=name + "_x", grid=grid, in_specs=list(in_specs) + [ANY] * n_c,
        out_specs=tuple(out_specs) + (ANY,) * n_c, out_shape=tuple(out_shape) + _exchange_shapes(scatter, srcs),
        scratch_shapes=list(scratch_shapes) + _exchange_sems(n_c),
        compiler_params=_params(("arbitrary",) * len(grid)))(*args, *srcs)
    return tuple(res[:n_out]), tuple(res[n_out:])


def _two_level_gather(phase, n_arr, x_refs, out_refs, send_sems, recv_sems, local_sems):
    x, y, c = _me()
    me, sibling = (x, y, c), (x, y, 1 - c)
    chips = [(1 - x, y), (x, 1 - y), (1 - x, 1 - y)]

    def slab(a, px, py, pc):
        return out_refs[a].at[4 * px + 2 * py + pc]

    def copy(a, k, block, to, src=None):
        return pltpu.make_async_remote_copy(
            src_ref=slab(a, *block) if src is None else src, dst_ref=slab(a, *block),
            send_sem=send_sems.at[a, k], recv_sem=recv_sems.at[a, k], device_id=to, device_id_type=MESH)

    def first(a):
        return [copy(a, 0, me, sibling, src=x_refs[a])] + [copy(a, 1 + j, me, (*chip, c), src=x_refs[a])
                                                            for j, chip in enumerate(chips)]

    if phase == "start":
        for a in range(n_arr):
            pltpu.make_async_copy(x_refs[a], slab(a, *me), local_sems.at[a]).start()
        for a in range(n_arr):
            for cp in first(a):
                cp.start()
    elif phase == "pass":
        for j, chip in enumerate(chips):
            for a in range(n_arr):
                copy(a, 1 + j, (*chip, c), me).wait_recv()
                copy(a, 4 + j, (*chip, c), sibling).start()
    else:
        for a in range(n_arr):
            copy(a, 0, sibling, me).wait_recv()
            for j, chip in enumerate(chips):
                copy(a, 4 + j, (*chip, 1 - c), me).wait_recv()
        for a in range(n_arr):
            for cp in first(a) + [copy(a, 4 + j, (*chip, c), sibling) for j, chip in enumerate(chips)]:
                cp.wait_send()
            pltpu.make_async_copy(x_refs[a], slab(a, *me), local_sems.at[a]).wait()


def _two_level_gather_body(n_arr, x_refs, out_refs, send_sems, recv_sems, local_sems):
    for phase in ("start", "pass", "finish"):
        _two_level_gather(phase, n_arr, x_refs, out_refs, send_sems, recv_sems, local_sems)


def _gather_small(x, name):
    def body(x_ref, out_ref, send_sems, recv_sems, local_sems):
        _two_level_gather_body(1, [x_ref], [out_ref], send_sems, recv_sems, local_sems)

    vm = pl.BlockSpec(memory_space=pltpu.VMEM)
    return pl.pallas_call(
        body, name=name,
        out_shape=jax.ShapeDtypeStruct((N_DEV,) + x.shape, x.dtype),
        in_specs=[vm], out_specs=vm,
        scratch_shapes=[pltpu.SemaphoreType.DMA((1, 7)), pltpu.SemaphoreType.DMA((1, 7)),
                        pltpu.SemaphoreType.DMA((1,))],
    )(x)


_SMALL = ["norm_mix_pre", "norm_mix_post", "norm_ffn_pre", "norm_ffn_post", "gla_gate_up", "gla_gate_b",
          "gla_norm", "ssd_conv_w", "ssd_conv_b", "ssd_dt_bias", "ssd_a_log", "ssd_d", "ssd_norm", "ret_norm"]


def _pack(arrs):
    flat = jnp.concatenate([a.reshape(-1) for a in arrs])
    n = flat.shape[0]
    npad = -(-n // 1024) * 1024
    return jnp.pad(flat, (0, npad - n)).reshape(npad // 128, 128)


def _unpack(buf, shapes):
    flat = buf.reshape(-1)
    out, o = [], 0
    for s in shapes:
        n = math.prod(s)
        out.append(flat[o:o + n].reshape(s))
        o += n
    return out


def kernel(x, c, ctx, c_ctx, ada_w, ada_b, norm_mix_pre, norm_mix_post, norm_ffn_pre, norm_ffn_post, w_in, w_out, gla_gate_up, gla_gate_b, gla_norm, ssd_conv_w, ssd_conv_b, ssd_dt_bias, ssd_a_log, ssd_d, ssd_norm, ret_norm, ffn_w13, ffn_w2, loss_target, m_c_ctx, m_ada_w, m_ada_b, m_norm_mix_pre, m_norm_mix_post, m_norm_ffn_pre, m_norm_ffn_post, m_w_in, m_w_out, m_gla_gate_up, m_gla_gate_b, m_gla_norm, m_ssd_conv_w, m_ssd_conv_b, m_ssd_dt_bias, m_ssd_a_log, m_ssd_d, m_ssd_norm, m_ret_norm, m_ffn_w13, m_ffn_w2, v_c_ctx, v_ada_w, v_ada_b, v_norm_mix_pre, v_norm_mix_post, v_norm_ffn_pre, v_norm_ffn_post, v_w_in, v_w_out, v_gla_gate_up, v_gla_gate_b, v_gla_norm, v_ssd_conv_w, v_ssd_conv_b, v_ssd_dt_bias, v_ssd_a_log, v_ssd_d, v_ssd_norm, v_ret_norm, v_ffn_w13, v_ffn_w2):
    P_ = dict(c_ctx=c_ctx, ada_w=ada_w, ada_b=ada_b, norm_mix_pre=norm_mix_pre, norm_mix_post=norm_mix_post,
              norm_ffn_pre=norm_ffn_pre, norm_ffn_post=norm_ffn_post, w_in=w_in, w_out=w_out,
              gla_gate_up=gla_gate_up, gla_gate_b=gla_gate_b, gla_norm=gla_norm, ssd_conv_w=ssd_conv_w,
              ssd_conv_b=ssd_conv_b, ssd_dt_bias=ssd_dt_bias, ssd_a_log=ssd_a_log, ssd_d=ssd_d,
              ssd_norm=ssd_norm, ret_norm=ret_norm, ffn_w13=ffn_w13, ffn_w2=ffn_w2)
    M_ = dict(c_ctx=m_c_ctx, ada_w=m_ada_w, ada_b=m_ada_b, norm_mix_pre=m_norm_mix_pre,
              norm_mix_post=m_norm_mix_post, norm_ffn_pre=m_norm_ffn_pre, norm_ffn_post=m_norm_ffn_post,
              w_in=m_w_in, w_out=m_w_out, gla_gate_up=m_gla_gate_up, gla_gate_b=m_gla_gate_b,
              gla_norm=m_gla_norm, ssd_conv_w=m_ssd_conv_w, ssd_conv_b=m_ssd_conv_b, ssd_dt_bias=m_ssd_dt_bias,
              ssd_a_log=m_ssd_a_log, ssd_d=m_ssd_d, ssd_norm=m_ssd_norm, ret_norm=m_ret_norm,
              ffn_w13=m_ffn_w13, ffn_w2=m_ffn_w2)
    V_ = dict(c_ctx=v_c_ctx, ada_w=v_ada_w, ada_b=v_ada_b, norm_mix_pre=v_norm_mix_pre,
              norm_mix_post=v_norm_mix_post, norm_ffn_pre=v_norm_ffn_pre, norm_ffn_post=v_norm_ffn_post,
              w_in=v_w_in, w_out=v_w_out, gla_gate_up=v_gla_gate_up, gla_gate_b=v_gla_gate_b,
              gla_norm=v_gla_norm, ssd_conv_w=v_ssd_conv_w, ssd_conv_b=v_ssd_conv_b, ssd_dt_bias=v_ssd_dt_bias,
              ssd_a_log=v_ssd_a_log, ssd_d=v_ssd_d, ssd_norm=v_ssd_norm, ret_norm=v_ret_norm,
              ffn_w13=v_ffn_w13, ffn_w2=v_ffn_w2)
    order = ["c_ctx", "ada_w", "ada_b", "norm_mix_pre", "norm_mix_post", "norm_ffn_pre", "norm_ffn_post", "w_in",
             "w_out", "gla_gate_up", "gla_gate_b", "gla_norm", "ssd_conv_w", "ssd_conv_b", "ssd_dt_bias",
             "ssd_a_log", "ssd_d", "ssd_norm", "ret_norm", "ffn_w13", "ffn_w2"]

    mx, my, mc_ = _me()
    me = 4 * mx + 2 * my + mc_
    Tl, Tc = x.shape[1], ctx.shape[1]
    n_in, n_out, n_13, n_2 = w_in.shape[2], w_out.shape[1], ffn_w13.shape[2], ffn_w2.shape[1]
    n_ada = ada_w.shape[2]

    shards = [[w_in[l].T.astype(BF16), w_out[l].astype(BF16), ffn_w13[l].T.astype(BF16), ffn_w2[l].astype(BF16)]
              for l in range(DEPTH)]

    cw = ssd_conv_w.shape[2]
    small_in = jnp.concatenate([jnp.pad(c, ((0, 7), (0, 0))).reshape(-1),
                                ssd_conv_w.reshape(-1)]).reshape(-1, 128)
    n_c_rows = 8 * D // 128
    small_in = jnp.pad(small_in, ((0, -small_in.shape[0] % 8), (0, 0)))
    gathered = _gather_small(small_in, "gather_c_conv")
    c_all = gathered[:, :n_c_rows].reshape(N_DEV, 8, D)[:, 0]
    conv_rows = DEPTH * 5 * cw // 128
    conv_full = gathered[:, n_c_rows:n_c_rows + conv_rows].reshape(N_DEV, DEPTH, 5, cw)
    conv_full = jnp.moveaxis(conv_full, 0, 2).reshape(DEPTH, 5, N_DEV * cw)
    c9 = jnp.concatenate([c_all, c_ctx[None], jnp.zeros((7, D), F32)], axis=0)
    s9 = c9 * jax.nn.sigmoid(c9)
    mod_piece = jnp.concatenate([_mm(s9, ada_w[l], name="mm_mod") for l in range(DEPTH)], axis=0)
    mod_g = _gather_small(mod_piece, "gather_mod")
    mod_all = jnp.moveaxis(mod_g.reshape(N_DEV, DEPTH, 16, n_ada), 0, 2).reshape(DEPTH, 16, N_DEV * n_ada)
    mod_all = mod_all + ada_b[:, None, :]
    mod_l = lax.dynamic_index_in_dim(mod_all, me, axis=1, keepdims=False)
    mod_c = mod_all[:, 8]

    sp = {n: P_[n] for n in _SMALL}
    sp["ssd_conv_w"] = conv_full
    xcat = jnp.concatenate([ctx[0], x[0]], axis=0)
    loss_local, d_xcat, d_mod_l, d_mod_c, d_sp, gw = _local_step(xcat, loss_target[0], mod_l, mod_c, sp, Tc,
                                                                 shards=shards)
    loss = lax.psum(loss_local, ("x", "y", "c"))
    grad_x = d_xcat[Tc:][None]

    G = {n: jnp.stack([_sum8_call(gw[l][a], f"sum_{n}") for l in range(DEPTH)])
         for a, n in enumerate(["w_in", "w_out", "ffn_w13", "ffn_w2"])}
    G["w_in"], G["ffn_w13"] = jnp.swapaxes(G["w_in"], 1, 2), jnp.swapaxes(G["ffn_w13"], 1, 2)

    dmod_rows = jnp.concatenate([d_mod_l, d_mod_c], axis=0)
    dmod_g = _gather_small(dmod_rows, "gather_dmod").reshape(N_DEV, 2, DEPTH, 6 * D)
    dl = jnp.moveaxis(dmod_g[:, 0], 0, 1)
    dc = dmod_g[:, 1, :, :]
    dc_tot = dc[0]
    for d_ in range(1, N_DEV):
        dc_tot = dc_tot + dc[d_]
    dmod9 = jnp.concatenate([dl, dc_tot[:, None, :], jnp.zeros((DEPTH, 7, 6 * D), F32)], axis=1)
    g_ada_b = dmod9[:, 0]
    for r_ in range(1, 9):
        g_ada_b = g_ada_b + dmod9[:, r_]
    dmod9_mine = lax.dynamic_slice_in_dim(dmod9, me * n_ada, n_ada, axis=2)
    s9T = jnp.pad(s9.T, ((0, 0), (0, 112)))
    g_ada_w = jnp.stack([_mm(s9T, jnp.pad(dmod9_mine[l], ((0, 112), (0, 0))), name="mm_dada")
                         for l in range(DEPTH)])
    ds9 = _mm(dmod9_mine[0], ada_w[0], trans_b=True, name="mm_ds9")
    for l in range(1, DEPTH):
        ds9 = _mm(dmod9_mine[l], ada_w[l], trans_b=True, name="mm_ds9_acc", add=ds9)
    ds_ctx_part = ds9[8]

    small_names = [n for n in _SMALL]
    small_parts = [d_sp[n] for n in small_names] + [ds_ctx_part]
    packed = _pack(small_parts)
    allp = _gather_small(packed, "gather_small_grads")
    summed = _sum_call([allp[d_] for d_ in range(N_DEV)], "sum_small_grads")
    parts = _unpack(summed, [p.shape for p in small_parts])
    for n, p in zip(small_names, parts[:-1]):
        G[n] = p
    sig = jax.nn.sigmoid(c_ctx)
    G["c_ctx"] = parts[-1] * (sig * (1.0 + c_ctx * (1.0 - sig)))
    G["ssd_conv_w"] = lax.dynamic_slice_in_dim(G["ssd_conv_w"], me * cw, cw, axis=2)
    G["ada_w"] = g_ada_w
    G["ada_b"] = g_ada_b

    delta, new_m, new_v = {}, {}, {}
    for n in ["ada_w", "w_in", "w_out", "ffn_w13", "ffn_w2"]:
        sh = P_[n].shape
        f2 = lambda a: a.reshape(sh[0] * sh[1], sh[2])
        d_, m_, v_ = _adamw_call(f2(P_[n]), f2(G[n]), f2(M_[n]), f2(V_[n]), f"adamw_{n}")
        delta[n], new_m[n], new_v[n] = d_.reshape(sh), m_.reshape(sh), v_.reshape(sh)
    rest = [n for n in order if n not in delta]
    shapes = [P_[n].shape for n in rest]
    d_, m_, v_ = _adamw_call(_pack([P_[n] for n in rest]), _pack([G[n] for n in rest]),
                             _pack([M_[n] for n in rest]), _pack([V_[n] for n in rest]), "adamw_small")
    for n, a, b, e in zip(rest, _unpack(d_, shapes), _unpack(m_, shapes), _unpack(v_, shapes)):
        delta[n], new_m[n], new_v[n] = a, b, e

    return (loss, grad_x, *[G[n] for n in order], *[delta[n] for n in order],
            *[new_m[n] for n in order], *[new_v[n] for n in order])
```

```python
import math

import jax
import jax.numpy as jnp
from jax import lax
from jax.experimental import pallas as pl
from jax.experimental.pallas import tpu as pltpu

F32 = jnp.float32
BF16 = jnp.bfloat16

D = 1024
DEPTH = 4
GRID_W = 64
RMS_EPS = 1e-6
GLA_TAU = 16.0
FFN_H = 2816
N_DEV = 8
ADAM_LR, ADAM_B1, ADAM_B2, ADAM_EPS, ADAM_WD, ADAM_STEP = 0.001, 0.9, 0.999, 1e-08, 0.01, 10

VMEM_LIMIT = 48 * 1024 * 1024

_ORIG = dict(gla_q=(0, 128), gla_k=(128, 128), gla_v=(256, 256), gla_r=(512, 256), gla_lr=(768, 32),
             ssd_z=(800, 512), ssd_xbc=(1312, 1024), ssd_dt=(2336, 16), ret_q=(2352, 256), ret_k=(2608, 256),
             ret_v=(2864, 256), ret_g=(3120, 256))
_R_ORDER = ["gla_v", "gla_r", "ret_q", "ret_k", "ret_v", "ret_g", "ssd_z", "gla_q", "gla_k", "gla_lr", "ssd_dt"]
R_W = 2560
_ROFF = {}
_o = 0
for _n in _R_ORDER:
    _ROFF[_n] = _o
    _o += _ORIG[_n][1]
MISC = _ROFF["gla_lr"]
assert MISC == 2304 and _o == 2352


def _split_w_in(wt):
    xs, xz = _ORIG["ssd_xbc"]
    parts = [wt[_ORIG[n][0]:_ORIG[n][0] + _ORIG[n][1]] for n in _R_ORDER]
    parts.append(jnp.zeros((R_W - _o,) + wt.shape[1:], wt.dtype))
    return wt[xs:xs + xz], jnp.concatenate(parts, axis=0)


def _merge_w_in(wx, wr):
    pieces = []
    for n, (s, z) in sorted(_ORIG.items(), key=lambda t: t[1][0]):
        pieces.append(wx if n == "ssd_xbc" else wr[_ROFF[n]:_ROFF[n] + z])
    return jnp.concatenate(pieces, axis=0)


def _pick(n, cands):
    for c in cands:
        if n % c == 0:
            return c
    return n


def _params(sem=None):
    kw = dict(vmem_limit_bytes=VMEM_LIMIT)
    if sem is not None:
        kw["dimension_semantics"] = sem
    return pltpu.CompilerParams(**kw)


def _iota(shape, dim):
    return lax.broadcasted_iota(jnp.int32, shape, dim)


def _dot(a, b, dims):
    return lax.dot_general(a, b, (dims, ((), ())), preferred_element_type=F32)


_NN = ((1,), (0,))
_NT = ((1,), (1,))
_TN = ((0,), (0,))


def _bf(x):
    return x.astype(BF16)


def _dot_sel(x, e, dims, x_left=True):
    eb = e.astype(BF16)
    hi = x.astype(BF16)
    r1 = x - hi.astype(F32)
    mid = r1.astype(BF16)
    lo = (r1 - mid.astype(F32)).astype(BF16)
    out = None
    for p in (hi, mid, lo):
        t = _dot(p, eb, dims) if x_left else _dot(eb, p, dims)
        out = t if out is None else out + t
    return out


@jax.custom_vjp
def _sel(x, e):
    return _dot_sel(x, e, _NN)


_sel.defvjp(lambda x, e: (_dot_sel(x, e, _NN), e), lambda e, g: (_dot_sel(g, e, _NT), jnp.zeros_like(e)))


def _sig(x):
    e = jnp.exp(-jnp.abs(x))
    return jnp.where(x >= 0, 1.0 / (1.0 + e), e / (1.0 + e))


@jax.custom_vjp
def _sigmoid(x):
    return _sig(x)


def _sigmoid_fwd(x):
    s = _sig(x)
    return s, s


_sigmoid.defvjp(_sigmoid_fwd, lambda s, g: (g * s * (1.0 - s),))


def _silu(x):
    return x * _sigmoid(x)


@jax.custom_vjp
def _softplus(x):
    return jnp.maximum(x, 0.0) + jnp.log(1.0 + jnp.exp(-jnp.abs(x)))


_softplus.defvjp(lambda x: (jnp.maximum(x, 0.0) + jnp.log(1.0 + jnp.exp(-jnp.abs(x))), x),
                 lambda x, g: (g * _sig(x),))


def _log_sigmoid(x):
    return -_softplus(-x)


@jax.custom_vjp
def _mm_bf(x, w):
    return _dot(_bf(x), _bf(w), _NN)


_mm_bf.defvjp(lambda x, w: (_dot(_bf(x), _bf(w), _NN), (x, w)),
              lambda r, g: (_dot(_bf(g), _bf(r[1]), _NT), _dot(_bf(r[0]), _bf(g), _TN)))


_TILE_M = (1088, 1024, 512, 256, 128, 64, 32, 16)
_TILE_N = (1408, 1280, 1024, 768, 512, 384, 256, 128)
_TILE_K = (1408, 1280, 1024, 768, 512, 384, 256, 128)


def _mm(a, b, *, trans_b=False, name, add=None, out_dtype=F32):
    M, K = a.shape
    N = b.shape[0] if trans_b else b.shape[1]
    assert (b.shape[1] if trans_b else b.shape[0]) == K
    tm, tn, tk = _pick(M, _TILE_M), _pick(N, _TILE_N), _pick(K, _TILE_K)
    nk = K // tk
    dims = _NT if trans_b else _NN
    has_add = add is not None

    def body(*refs):
        a_ref, b_ref = refs[0], refs[1]
        o_ref, acc_ref = refs[-2], refs[-1]
        k = pl.program_id(2)

        @pl.when(k == 0)
        def _():
            acc_ref[...] = refs[2][...] if has_add else jnp.zeros_like(acc_ref)

        acc_ref[...] += _dot(a_ref[...].astype(BF16), b_ref[...].astype(BF16), dims)

        @pl.when(k == nk - 1)
        def _():
            o_ref[...] = acc_ref[...].astype(o_ref.dtype)

    b_spec = (pl.BlockSpec((tn, tk), lambda i, j, k: (j, k)) if trans_b
              else pl.BlockSpec((tk, tn), lambda i, j, k: (k, j)))
    o_spec = pl.BlockSpec((tm, tn), lambda i, j, k: (i, j))
    return pl.pallas_call(
        body, name=name,
        out_shape=jax.ShapeDtypeStruct((M, N), out_dtype),
        grid=(M // tm, N // tn, nk),
        in_specs=[pl.BlockSpec((tm, tk), lambda i, j, k: (i, k)), b_spec] + ([o_spec] if has_add else []),
        out_specs=o_spec,
        scratch_shapes=[pltpu.VMEM((tm, tn), F32)],
        compiler_params=_params(("parallel", "parallel", "arbitrary")),
    )(*((a, b, add) if has_add else (a, b)))


def _mm_tn(a, g, *, name, out_dtype=F32):
    M, K = a.shape
    N = g.shape[1]
    tm, tk, tn = _pick(M, _TILE_M), _pick(K, _TILE_K), _pick(N, _TILE_N)
    nm = M // tm

    def body(a_ref, g_ref, o_ref, acc_ref):
        i = pl.program_id(2)

        @pl.when(i == 0)
        def _():
            acc_ref[...] = jnp.zeros_like(acc_ref)

        acc_ref[...] += _dot(a_ref[...].astype(BF16), g_ref[...].astype(BF16), _TN)

        @pl.when(i == nm - 1)
        def _():
            o_ref[...] = acc_ref[...].astype(o_ref.dtype)

    return pl.pallas_call(
        body, name=name,
        out_shape=jax.ShapeDtypeStruct((K, N), out_dtype),
        grid=(K // tk, N // tn, nm),
        in_specs=[pl.BlockSpec((tm, tk), lambda k, j, i: (i, k)), pl.BlockSpec((tm, tn), lambda k, j, i: (i, j))],
        out_specs=pl.BlockSpec((tk, tn), lambda k, j, i: (k, j)),
        scratch_shapes=[pltpu.VMEM((tk, tn), F32)],
        compiler_params=_params(("parallel", "parallel", "arbitrary")),
    )(a, g)


def _norm_fwd_call(x, w, a2, b2, res, tr, out_dtype=F32):
    T, W = x.shape
    has_res = res is not None

    def body(*refs):
        x_ref, w_ref, a_ref, b_ref = refs[:4]
        y_ref = refs[-1]
        seg = jnp.minimum(pl.program_id(0), 1)
        xv = x_ref[...]
        rstd = lax.rsqrt(jnp.mean(xv * xv, axis=-1, keepdims=True) + RMS_EPS)
        y = a_ref[pl.ds(seg, 1), :] * (xv * rstd * w_ref[...]) + b_ref[pl.ds(seg, 1), :]
        y_ref[...] = (y + refs[4][...] if has_res else y).astype(y_ref.dtype)

    row = pl.BlockSpec((tr, W), lambda i: (i, 0))
    small = pl.BlockSpec((8, W), lambda i: (0, 0))
    return pl.pallas_call(
        body, name="norm_fwd",
        out_shape=jax.ShapeDtypeStruct((T, W), out_dtype),
        grid=(T // tr,),
        in_specs=[row, pl.BlockSpec((1, W), lambda i: (0, 0)), small, small] + ([row] if has_res else []),
        out_specs=row,
        compiler_params=_params(("parallel",)),
    )(*((x, w.reshape(1, W), a2, b2) + ((res,) if has_res else ())))


def _norm_bwd_call(x, w, a2, dy, tr, add=None, out_dtype=F32):
    T, W = x.shape
    has_add = add is not None

    def body(*refs):
        x_ref, w_ref, a_ref, dy_ref = refs[:4]
        dx_ref, dw_ref, da_ref, db_ref = refs[-4:]
        i = pl.program_id(0)
        seg = jnp.minimum(i, 1)

        @pl.when(i == 0)
        def _():
            dw_ref[...] = jnp.zeros_like(dw_ref)
            da_ref[...] = jnp.zeros_like(da_ref)
            db_ref[...] = jnp.zeros_like(db_ref)

        xv = x_ref[...]
        g = dy_ref[...]
        wv = w_ref[...]
        rstd = lax.rsqrt(jnp.mean(xv * xv, axis=-1, keepdims=True) + RMS_EPS)
        xh = xv * rstd
        da_ref[pl.ds(seg, 1), :] += jnp.sum(g * (xh * wv), axis=0, keepdims=True)
        db_ref[pl.ds(seg, 1), :] += jnp.sum(g, axis=0, keepdims=True)
        gy = g * a_ref[pl.ds(seg, 1), :]
        dw_ref[0:1, :] += jnp.sum(gy * xh, axis=0, keepdims=True)
        gx = gy * wv
        dx = rstd * (gx - xh * jnp.mean(gx * xh, axis=-1, keepdims=True))
        dx_ref[...] = (dx + refs[4][...] if has_add else dx).astype(dx_ref.dtype)

    acc = jax.ShapeDtypeStruct((8, W), F32)
    acc_spec = pl.BlockSpec((8, W), lambda i: (0, 0))
    row = pl.BlockSpec((tr, W), lambda i: (i, 0))
    return pl.pallas_call(
        body, name="norm_bwd",
        out_shape=(jax.ShapeDtypeStruct((T, W), out_dtype), acc, acc, acc),
        grid=(T // tr,),
        in_specs=[row, pl.BlockSpec((1, W), lambda i: (0, 0)), acc_spec, row] + ([row] if has_add else []),
        out_specs=(row, acc_spec, acc_spec, acc_spec),
        compiler_params=_params(("arbitrary",)),
    )(*((x, w.reshape(1, W), a2, dy) + ((add,) if has_add else ())))


def _act_call(u1, u2, dact=None):
    T, W = u1.shape
    tr = _pick(T, (512, 256, 128, 64))
    tn = _pick(W, (1408, 512, 256, 128))
    spec = pl.BlockSpec((tr, tn), lambda i, j: (i, j))
    sh = jax.ShapeDtypeStruct((T, W), BF16)
    if dact is None:
        def body(a_ref, b_ref, o_ref):
            a = a_ref[...].astype(F32)
            o_ref[...] = (a * _sig(a) * b_ref[...].astype(F32)).astype(o_ref.dtype)

        return pl.pallas_call(body, name="act_fwd", out_shape=sh, grid=(T // tr, W // tn), in_specs=[spec, spec],
                              out_specs=spec, compiler_params=_params(("parallel", "parallel")))(u1, u2)

    def body(a_ref, b_ref, g_ref, da_ref, db_ref):
        a, g = a_ref[...].astype(F32), g_ref[...].astype(F32)
        s = _sig(a)
        da_ref[...] = (g * b_ref[...].astype(F32) * (s * (1.0 + a * (1.0 - s)))).astype(da_ref.dtype)
        db_ref[...] = (g * a * s).astype(db_ref.dtype)

    return pl.pallas_call(body, name="act_bwd", out_shape=(sh, sh), grid=(T // tr, W // tn),
                          in_specs=[spec, spec, spec], out_specs=(spec, spec),
                          compiler_params=_params(("parallel", "parallel")))(u1, u2, dact)


def _conv_specs(T, Wc, tr):
    hb, nt = tr // 8, T // tr
    row = pl.BlockSpec((tr, Wc), lambda i: (i, 0))
    prev = pl.BlockSpec((8, Wc), lambda i: (jnp.maximum(i * hb - 1, 0), 0))
    nxt = pl.BlockSpec((8, Wc), lambda i: (jnp.minimum((i + 1) * hb, T // 8 - 1), 0))
    return row, prev, nxt, nt


def _fill_ext(dst_ref, cur_ref, prev_ref, next_ref, i, nt, tr):
    has_prev = (i > 1).astype(F32)
    has_next = jnp.logical_and(i > 0, i < nt - 1).astype(F32)
    dst_ref[8:16, :] = prev_ref[...] * has_prev
    dst_ref[16:16 + tr, :] = cur_ref[...]
    dst_ref[16 + tr:24 + tr, :] = next_ref[...] * has_next


def _conv_fwd_call(px, w8, b, tr):
    T, Wc = px.shape
    row, prev, nxt, nt = _conv_specs(T, Wc, tr)

    def body(x_ref, xp_ref, xn_ref, w_ref, b_ref, u_ref, xe_ref):
        i = pl.program_id(0)

        @pl.when(i == 0)
        def _():
            xe_ref[...] = jnp.zeros_like(xe_ref)

        _fill_ext(xe_ref, x_ref, xp_ref, xn_ref, i, nt, tr)
        y = b_ref[...] + w_ref[0:1, :] * xe_ref[pl.ds(14, tr), :]
        for k in range(1, 5):
            y = y + w_ref[k:k + 1, :] * xe_ref[pl.ds(14 + k, tr), :]
        u_ref[...] = y * _sig(y)

    return pl.pallas_call(
        body, name="conv_fwd", out_shape=jax.ShapeDtypeStruct((T, Wc), F32), grid=(nt,),
        in_specs=[row, prev, nxt, pl.BlockSpec((8, Wc), lambda i: (0, 0)), pl.BlockSpec((1, Wc), lambda i: (0, 0))],
        out_specs=row, scratch_shapes=[pltpu.VMEM((tr + 32, Wc), F32)],
        compiler_params=_params(("arbitrary",)),
    )(px, px, px, w8, b)


def _conv_bwd_call(px, w8, b, du, tr):
    T, Wc = px.shape
    row, prev, nxt, nt = _conv_specs(T, Wc, tr)
    E = tr + 16

    def body(x_ref, xp_ref, xn_ref, g_ref, gp_ref, gn_ref, w_ref, b_ref, dx_ref, dw_ref, db_ref,
             xe_ref, ge_ref, dy_ref):
        i = pl.program_id(0)

        @pl.when(i == 0)
        def _():
            xe_ref[...] = jnp.zeros_like(xe_ref)
            ge_ref[...] = jnp.zeros_like(ge_ref)
            dy_ref[...] = jnp.zeros_like(dy_ref)
            dw_ref[...] = jnp.zeros_like(dw_ref)
            db_ref[...] = jnp.zeros_like(db_ref)

        _fill_ext(xe_ref, x_ref, xp_ref, xn_ref, i, nt, tr)
        _fill_ext(ge_ref, g_ref, gp_ref, gn_ref, i, nt, tr)
        y = b_ref[...] + w_ref[0:1, :] * xe_ref[pl.ds(6, E), :]
        for k in range(1, 5):
            y = y + w_ref[k:k + 1, :] * xe_ref[pl.ds(6 + k, E), :]
        s = _sig(y)
        dy = ge_ref[pl.ds(8, E), :] * (s * (1.0 + y * (1.0 - s)))
        dy_ref[pl.ds(8, E), :] = dy
        dx = w_ref[0:1, :] * dy_ref[pl.ds(18, tr), :]
        for k in range(1, 5):
            dx = dx + w_ref[k:k + 1, :] * dy_ref[pl.ds(18 - k, tr), :]
        dx_ref[...] = dx.astype(dx_ref.dtype)
        dyt = dy_ref[pl.ds(16, tr), :]
        db_ref[0:1, :] += jnp.sum(dyt, axis=0, keepdims=True)
        for k in range(5):
            dw_ref[k:k + 1, :] += jnp.sum(dyt * xe_ref[pl.ds(14 + k, tr), :], axis=0, keepdims=True)

    acc = jax.ShapeDtypeStruct((8, Wc), F32)
    acc_spec = pl.BlockSpec((8, Wc), lambda i: (0, 0))
    ext = pltpu.VMEM((tr + 32, Wc), F32)
    return pl.pallas_call(
        body, name="conv_bwd", out_shape=(jax.ShapeDtypeStruct((T, Wc), BF16), acc, acc), grid=(nt,),
        in_specs=[row, prev, nxt, row, prev, nxt, acc_spec, pl.BlockSpec((1, Wc), lambda i: (0, 0))],
        out_specs=(row, acc_spec, acc_spec), scratch_shapes=[ext, ext, ext],
        compiler_params=_params(("arbitrary",)),
    )(px, px, px, du, du, du, w8, b)


_SCAN_CFG = {
    "gla": dict(H=4, Dk=32, Dv=64, nh=4, scalar=False, C=128),
    "ssd": dict(H=8, Dk=128, Dv=64, nh=2, scalar=True, C=128),
    "ret": dict(H=4, Dk=64, Dv=64, nh=4, scalar=True, C=128),
}
GPAD = 8


def _log2(n):
    r = int(math.log2(n))
    assert 1 << r == n
    return r


class _ScanMath:
    def __init__(self, cfg, reverse):
        C = cfg["C"]
        self.C, self.reverse = C, reverse
        self.Dk, self.Dv, self.nh, self.scalar = cfg["Dk"], cfg["Dv"], cfg["nh"], cfg["scalar"]
        self.Wk, self.Wv = self.nh * self.Dk, self.nh * self.Dv
        self.nsg = cfg["H"] // self.nh
        nh, Wk, Wv = self.nh, self.Wk, self.Wv
        lk, lv, lc = _log2(self.Dk), _log2(self.Dv), _log2(C)
        r, c = _iota((C, C), 0), _iota((C, C), 1)
        self.L = ((c >= r) if reverse else (c <= r)).astype(F32)
        self.Lsuf = ((c <= r) if reverse else (c >= r)).astype(F32)
        i, j = _iota((C, nh * C), 0), _iota((C, nh * C), 1) & (C - 1)
        self.Mst = (j >= i) if reverse else (j <= i)
        self.Dj = (i == j).astype(F32)
        self.nb = 1 if (self.scalar or C == 64) else 3
        assert self.scalar or C in (64, 128)
        lanes = _iota((1, self.nb * Wk), 1) & (Wk - 1)
        self.km = [((lanes >> lk) == h).astype(F32) for h in range(nh)]
        self.vm = [((_iota((1, Wv), 1) >> lv) == h).astype(F32) for h in range(nh)]
        self.BD = ((_iota((Wv, Wk), 0) >> lv) == (_iota((Wv, Wk), 1) >> lk)).astype(F32)
        self.last = 0 if reverse else C - 1
        self.last_row = (_iota((C, 1), 0) == self.last).astype(F32)
        self.lk, self.lc = lk, lc
        self.H = cfg["H"]

    def gates(self, g):
        if not self.scalar:
            return _dot_sel(g, self.L, _NN, x_left=False), None
        G8 = _dot_sel(g, self.L, _NN, x_left=False)
        nk, ncol = self.H * self.Dk, self.H * self.C
        ek = (_iota((GPAD, nk), 0) == (_iota((GPAD, nk), 1) >> self.lk)).astype(F32)
        ec = (_iota((GPAD, ncol), 0) == (_iota((GPAD, ncol), 1) >> self.lc)).astype(F32)
        return _dot_sel(G8, ek, _NN), _dot_sel(G8, ec, _NN)

    def Ek(self, s):
        return (_iota((GPAD, self.Wk), 0) == (_iota((GPAD, self.Wk), 1) >> self.lk) + s * self.nh).astype(F32)

    def fold(self, x, factors=None):
        Wk = self.Wk
        out = None
        for b in range(self.nb):
            t = x[:, b * Wk:(b + 1) * Wk]
            t = t if factors is None or factors[b] is None else t * factors[b]
            out = t if out is None else out + t
        return out

    def kstack(self, x):
        return jnp.concatenate([x * self.km[h] for h in range(self.nh)], axis=0)

    def vstack(self, x):
        return jnp.concatenate([x * self.vm[h] for h in range(self.nh)], axis=0)

    def unstack(self, R, masks):
        C = self.C
        out = R[0:C] * masks[0]
        for h in range(1, self.nh):
            out = out + R[h * C:(h + 1) * C] * masks[h]
        return out

    def chunk(self, qs, ks, Gk, Gc):
        C = self.C
        Glast = Gk[self.last:self.last + 1, :]
        out = dict(Gk=Gk, Glast=Glast, eG=jnp.exp(Gk), eGl=jnp.exp(Glast - Gk), eGlast=jnp.exp(Glast))
        if self.scalar:
            Gr = jnp.sum(Gc * self.Dj, axis=0, keepdims=True)
            dec = jnp.where(self.Mst, jnp.exp(jnp.minimum(Gc - Gr, 0.0)), 0.0)
            qt, kt = qs, ks
            A = _dot(_bf(qt), _bf(self.kstack(kt)), _NT) * dec
            out.update(dec=dec, qt=qt, kt=kt, A=A, fq=[None], fk=[None])
        elif self.nb == 1:
            Gm = Gk[C // 2:C // 2 + 1, :]
            fq, fk = [jnp.exp(Gk - Gm)], [jnp.exp(Gm - Gk)]
            qt, kt = qs * fq[0], ks * fk[0]
            A = jnp.where(self.Mst, _dot(_bf(qt), _bf(self.kstack(kt)), _NT), 0.0)
            out.update(fq=fq, fk=fk, qt=qt, kt=kt, A=A)
        else:
            h = C // 2
            rows = _iota((C, 1), 0)
            early = (rows >= h) if self.reverse else (rows < h)
            late = jnp.logical_not(early)
            m_e, m_l, b = (h + h // 2, h // 2, h) if self.reverse else (h // 2, h + h // 2, h - 1)
            Ge, Gl, Gb = Gk[m_e:m_e + 1, :], Gk[m_l:m_l + 1, :], Gk[b:b + 1, :]

            def factor(mask, arg):
                return jnp.where(mask, jnp.exp(jnp.where(mask, arg, 0.0)), 0.0)

            fq = [factor(early, Gk - Ge), factor(late, Gk - Gl), factor(late, Gk - Gb)]
            fk = [factor(early, Ge - Gk), factor(late, Gl - Gk), factor(early, Gb - Gk)]
            qt = jnp.concatenate([qs * f for f in fq], axis=1)
            kt = jnp.concatenate([ks * f for f in fk], axis=1)
            A = jnp.where(self.Mst, _dot(_bf(qt), _bf(self.kstack(kt)), _NT), 0.0)
            out.update(fq=fq, fk=fk, qt=qt, kt=kt, A=A)
        return out


def _chunk_index(p, n, nc, reverse):
    if not reverse:
        return p
    return jnp.where(p < nc, nc - 1 - p, n - 1 + nc - p)


def _scan_dims(kind):
    cfg = _SCAN_CFG[kind]
    HK, HV = cfg["H"] * cfg["Dk"], cfg["H"] * cfg["Dv"]
    return cfg, cfg["C"], HK, HV, (GPAD if cfg["scalar"] else HK)


def _scan_fwd_step(m, q_ref, k_ref, v_ref, g_ref, o_ref, st_ref, S_ref):
    C = m.C

    @pl.when(pl.program_id(0) == 0)
    def _():
        S_ref[...] = jnp.zeros_like(S_ref)

    Gk_all, Gc_all = m.gates(g_ref[...])
    for s in range(m.nsg):
        ksl, vsl = slice(s * m.Wk, (s + 1) * m.Wk), slice(s * m.Wv, (s + 1) * m.Wv)
        csl = slice(s * m.nh * C, (s + 1) * m.nh * C)
        qs, ks, vs = q_ref[:, ksl], k_ref[:, ksl], v_ref[:, vsl]
        ch = m.chunk(qs, ks, Gk_all[:, ksl], Gc_all[:, csl] if m.scalar else None)
        S = S_ref[vsl, :]
        o = _dot(_bf(ch["A"]), _bf(m.vstack(vs)), _NN) + _dot(_bf(qs * ch["eG"]), _bf(S), _NT)
        o_ref[:, vsl] = o
        st_ref[0, vsl, :] = S
        S_ref[vsl, :] = S * ch["eGlast"] + _dot(_bf(vs), _bf(ks * ch["eGl"]), _TN) * m.BD


def _scan_fwd_call(kind, ops, Tc, comm=None):
    cfg, C, HK, HV, GW = _scan_dims(kind)
    T = ops[False][0][0].shape[0]
    n, nc = T // C, Tc // C

    def body(*refs):
        for d, rev in enumerate((False, True)):
            _scan_fwd_step(_ScanMath(cfg, rev), *refs[4 * d:4 * d + 4], *refs[8 + 2 * d:10 + 2 * d], refs[12 + d])

    sg = cfg["H"] // cfg["nh"]
    Wk, Wv = cfg["nh"] * cfg["Dk"], cfg["nh"] * cfg["Dv"]
    col = lambda rev, w, j: pl.BlockSpec((C, w), lambda p: (_chunk_index(p, n, nc, rev), j))
    st_spec = lambda rev: pl.BlockSpec((1, sg * Wv, Wk), lambda p: (_chunk_index(p, n, nc, rev), 0, 0))
    in_specs, args, out_specs, out_shape = [], [], [], []
    for rev in (False, True):
        q, k, v, g = ops[rev]
        in_specs += [col(rev, HK, q[1]), col(rev, HK, k[1]), col(rev, HV, v[1]), col(rev, GW, g[1])]
        args += [q[0], k[0], v[0], g[0]]
        out_specs += [col(rev, HV, 0), st_spec(rev)]
        out_shape += [jax.ShapeDtypeStruct((T, HV), F32), jax.ShapeDtypeStruct((n, sg * Wv, Wk), F32)]
    res, got = _pcall(body, name=f"scan_fwd_{kind}", out_shape=out_shape, grid=(n,), in_specs=in_specs,
                      out_specs=out_specs, scratch_shapes=[pltpu.VMEM((sg * Wv, Wk), F32)] * 2,
                      sem=("arbitrary",), args=args, comm=comm)
    return {False: (res[0], res[1]), True: (res[2], res[3])}, got


def _scan_bwd_step(m, need_dg, q_ref, k_ref, v_ref, g_ref, st_ref, do_ref, dq_ref, dk_ref, dv_ref, dg_ref, dS_ref):
    C = m.C

    @pl.when(pl.program_id(0) == 0)
    def _():
        dS_ref[...] = jnp.zeros_like(dS_ref)

    x8 = jnp.zeros((C, GPAD), F32)
    Gk_all, Gc_all = m.gates(g_ref[...])
    for s in range(m.nsg):
        ksl, vsl = slice(s * m.Wk, (s + 1) * m.Wk), slice(s * m.Wv, (s + 1) * m.Wv)
        csl = slice(s * m.nh * C, (s + 1) * m.nh * C)
        qs, ks, vs, dos = q_ref[:, ksl], k_ref[:, ksl], v_ref[:, vsl], do_ref[:, vsl]
        ch = m.chunk(qs, ks, Gk_all[:, ksl], Gc_all[:, csl] if m.scalar else None)
        S = st_ref[0, vsl, :]
        dS = dS_ref[vsl, :]
        A, qt, kt = ch["A"], ch["qt"], ch["kt"]
        dA = _dot(_bf(dos), _bf(m.vstack(vs)), _NT)
        dAm = dA * ch["dec"] if m.scalar else jnp.where(m.Mst, dA, 0.0)
        kst = _bf(m.kstack(kt))
        dv = m.unstack(_dot(_bf(A), _bf(dos), _TN), m.vm) + _dot(_bf(ks * ch["eGl"]), _bf(dS), _NT)
        dv_ref[:, vsl] = dv
        dq_i = _dot(_bf(dAm), kst, _NN)
        dq_x = ch["eG"] * _dot(_bf(dos), _bf(S), _NN)
        dq_ref[:, ksl] = m.fold(dq_i, ch["fq"]) + dq_x
        dk_i = m.unstack(_dot(_bf(dAm), _bf(qt), _TN), m.km)
        dk_x = ch["eGl"] * _dot(_bf(vs), _bf(dS), _NN)
        dk_ref[:, ksl] = m.fold(dk_i, ch["fk"]) + dk_x
        if need_dg:
            bnd = (ch["eGlast"] * jnp.sum(dS * S, axis=0, keepdims=True)
                   + jnp.sum(ks * dk_x, axis=0, keepdims=True))
            X = m.fold(_bf(qt).astype(F32) * dq_i - _bf(kt).astype(F32) * dk_i) + (qs * dq_x - ks * dk_x)
            X = X + m.last_row * bnd
            if m.scalar:
                x8 = x8 + _dot_sel(X, m.Ek(s), _NT)
            else:
                dg_ref[:, ksl] = _dot_sel(X, m.Lsuf, _NN, x_left=False)
        dS_ref[vsl, :] = dS * ch["eGlast"] + _dot(_bf(dos), _bf(qs * ch["eG"]), _TN) * m.BD
    if m.scalar:
        dg_ref[...] = _dot_sel(x8, m.Lsuf, _NN, x_left=False)
    elif not need_dg:
        dg_ref[...] = jnp.zeros_like(dg_ref)


def _scan_bwd_call(kind, need_dg, ops, st, do, Tc, comm=None):
    cfg, C, HK, HV, GW = _scan_dims(kind)
    T = ops[False][0][0].shape[0]
    n, nc = T // C, Tc // C

    def body(*refs):
        for d, rev in enumerate((False, True)):
            _scan_bwd_step(_ScanMath(cfg, rev), need_dg, *refs[6 * d:6 * d + 6], *refs[12 + 4 * d:16 + 4 * d],
                           refs[20 + d])

    sg = cfg["H"] // cfg["nh"]
    Wk, Wv = cfg["nh"] * cfg["Dk"], cfg["nh"] * cfg["Dv"]
    col = lambda rev, w, j: pl.BlockSpec((C, w), lambda p: (_chunk_index(n - 1 - p, n, nc, rev), j))
    st_spec = lambda rev: pl.BlockSpec((1, sg * Wv, Wk), lambda p: (_chunk_index(n - 1 - p, n, nc, rev), 0, 0))
    in_specs, args, out_specs, out_shape = [], [], [], []
    for rev in (False, True):
        q, k, v, g = ops[rev]
        in_specs += [col(rev, HK, q[1]), col(rev, HK, k[1]), col(rev, HV, v[1]), col(rev, GW, g[1]),
                     st_spec(rev), col(rev, HV, 0)]
        args += [q[0], k[0], v[0], g[0], st[rev], do]
        out_specs += [col(rev, HK, 0), col(rev, HK, 0), col(rev, HV, 0), col(rev, GW, 0)]
        out_shape += [jax.ShapeDtypeStruct((T, w), F32) for w in (HK, HK, HV, GW)]
    res, got = _pcall(body, name=f"scan_bwd_{kind}", out_shape=out_shape, grid=(n,), in_specs=in_specs,
                      out_specs=out_specs, scratch_shapes=[pltpu.VMEM((sg * Wv, Wk), F32)] * 2,
                      sem=("arbitrary",), args=args, comm=comm)
    return {False: res[0:4], True: res[4:8]}, got


def _prep_consts():
    r, c = _iota((256, 256), 0), _iota((256, 256), 1)
    first = (c & 63) < 32
    rope_perm = jnp.where(first, -(r == c + 32).astype(F32), (r == c - 32).astype(F32))
    sel_f = (_iota((128, GPAD), 0) == _iota((128, GPAD), 1) + 32).astype(F32)
    sel_b = (_iota((128, GPAD), 0) == _iota((128, GPAD), 1) + 40).astype(F32)
    ek = (_iota((GPAD, 1024), 0) == (_iota((GPAD, 1024), 1) >> 7)).astype(F32)
    return rope_perm, sel_f, sel_b, ek


def _prep_tile(misc, gq, rq, rk, bm, cm, cosE, sinE, Wg, gbias, dtbf, dtbb, nAf, nAb):
    rope_perm, sel_f, sel_b, ek = _prep_consts()
    logg = _log_sigmoid(_mm_bf(misc, Wg) + gbias) * (1.0 / GLA_TAU)
    a_gla = jnp.concatenate([gq * (32 ** -0.5), logg], axis=1)
    rot = lambda t: t * cosE + _sel(t, rope_perm) * sinE
    a_ret = jnp.concatenate([rot(rq * (64 ** -0.5)), rot(rk)], axis=1)
    dtf = _softplus(_sel(misc, sel_f) + dtbf)
    dtb = _softplus(_sel(misc, sel_b) + dtbb)
    rep = lambda t: jnp.concatenate([t[:, :128]] * 4 + [t[:, 128:]] * 4, axis=1)
    bmr = rep(bm)
    return a_gla, a_ret, rep(cm), bmr * _sel(dtf, ek), bmr * _sel(dtb, ek), dtf * nAf, dtb * nAb


def _prep_row_specs(tr):
    blk = lambda w, j: pl.BlockSpec((tr, w), lambda i: (i, j))
    return [blk(128, MISC // 128), blk(128, _ROFF["gla_q"] // 128), blk(256, _ROFF["ret_q"] // 256),
            blk(256, _ROFF["ret_k"] // 256), blk(256, 2), blk(256, 3), blk(256, 0), blk(256, 0)]


def _whole(a):
    return pl.BlockSpec(a.shape, lambda i: (0,) * a.ndim)


def _prep_fwd_call(Pr, u, cosE, sinE, pp, tr):
    T = Pr.shape[0]
    n_row = 8

    def body(*refs):
        outs = _prep_tile(*[r[...] for r in refs[:n_row + len(pp)]])
        for o_ref, o in zip(refs[n_row + len(pp):], outs):
            o_ref[...] = o

    widths = [384, 512, 1024, 1024, 1024, GPAD, GPAD]
    return pl.pallas_call(
        body, name="prep_fwd", grid=(T // tr,),
        out_shape=tuple(jax.ShapeDtypeStruct((T, w), F32) for w in widths),
        in_specs=_prep_row_specs(tr) + [_whole(p) for p in pp],
        out_specs=tuple(pl.BlockSpec((tr, w), lambda i: (i, 0)) for w in widths),
        compiler_params=_params(("parallel",)),
    )(Pr, Pr, Pr, Pr, u, u, cosE, sinE, *pp)


def _prep_bwd_call(Pr, u, cosE, sinE, pp, cts, tr, comm=None):
    T = Pr.shape[0]
    n_row, n_p = 8, len(pp)
    names = ["gla_dq_f", "gla_dq_b", "gla_dg_f", "gla_dg_b", "gla_dk_f", "gla_dk_b", "gla_dv_f", "gla_dv_b",
             "ret_dq_f", "ret_dq_b", "ret_dk_f", "ret_dk_b", "ret_dv_f", "ret_dv_b",
             "ssd_dq_f", "ssd_dq_b", "ssd_dk_f", "ssd_dk_b", "ssd_dg_f", "ssd_dg_b", "ssd_dv_f", "ssd_dv_b",
             "d_r", "d_z", "d_gr", "d_xs"]
    ct_arrays = [cts[n] for n in names]

    def body(*refs):
        ins = [r[...] for r in refs[:n_row + n_p]]
        c = {n: r[...] for n, r in zip(names, refs[n_row + n_p:n_row + n_p + len(names)])}
        dPr_ref, du_ref = refs[n_row + n_p + len(names):n_row + n_p + len(names) + 2]
        dp_refs = refs[n_row + n_p + len(names) + 2:]
        _, vjp = jax.vjp(_prep_tile, *ins)
        ct_out = (jnp.concatenate([c["gla_dq_f"] + c["gla_dq_b"], c["gla_dg_f"], c["gla_dg_b"]], axis=1),
                  jnp.concatenate([c["ret_dq_f"] + c["ret_dq_b"], c["ret_dk_f"] + c["ret_dk_b"]], axis=1),
                  c["ssd_dq_f"] + c["ssd_dq_b"], c["ssd_dk_f"], c["ssd_dk_b"], c["ssd_dg_f"], c["ssd_dg_b"])
        d = vjp(ct_out)
        d_misc, d_gq, d_rq, d_rk, d_bm, d_cm = d[:6]
        dPr_ref[...] = jnp.concatenate(
            [c["gla_dv_f"] + c["gla_dv_b"], c["d_r"], d_rq, d_rk, c["ret_dv_f"] + c["ret_dv_b"], c["d_gr"],
             c["d_z"], d_gq, c["gla_dk_f"] + c["gla_dk_b"], d_misc,
             jnp.zeros((d_misc.shape[0], R_W - MISC - 128), F32)], axis=1).astype(dPr_ref.dtype)
        du_ref[...] = jnp.concatenate([c["ssd_dv_f"] + c["ssd_dv_b"] + c["d_xs"], d_bm, d_cm], axis=1)

        @pl.when(pl.program_id(0) == 0)
        def _():
            for r in dp_refs:
                r[...] = jnp.zeros_like(r)

        for r, g in zip(dp_refs, d[n_row:]):
            r[...] += g

    row = lambda a: pl.BlockSpec((tr, a.shape[1]), lambda i: (i, 0))
    return _pcall(
        body, name="prep_bwd", grid=(T // tr,),
        out_shape=(jax.ShapeDtypeStruct((T, R_W), BF16), jax.ShapeDtypeStruct((T, 1024), F32))
        + tuple(jax.ShapeDtypeStruct(p.shape, F32) for p in pp),
        in_specs=_prep_row_specs(tr) + [_whole(p) for p in pp] + [row(a) for a in ct_arrays],
        out_specs=(pl.BlockSpec((tr, R_W), lambda i: (i, 0)), pl.BlockSpec((tr, 1024), lambda i: (i, 0)))
        + tuple(_whole(p) for p in pp),
        scratch_shapes=[], sem=("arbitrary",), args=(Pr, Pr, Pr, Pr, u, u, cosE, sinE, *pp, *ct_arrays), comm=comm)


def _post_tile(ogf, ogb, r, ysf, ysb, xs, z, orf, orb, gr, gla_n, dexp, ssd_n, ret_n):
    bd = ((_iota((256, 256), 0) >> 6) == (_iota((256, 256), 1) >> 6)).astype(F32)
    og = ogf + ogb
    gla = og * lax.rsqrt(_sel(og * og, bd) * (1.0 / 64) + RMS_EPS) * gla_n * _silu(r)
    t = (ysf + ysb + dexp * xs) * _silu(z)
    ssd = t * lax.rsqrt(jnp.mean(t * t, axis=-1, keepdims=True) + RMS_EPS) * ssd_n
    o = orf + orb
    oc = o - _sel(o, bd) * (1.0 / 64)
    ret = oc * lax.rsqrt(_sel(oc * oc, bd) * (1.0 / 64) + RMS_EPS) * ret_n * _silu(gr)
    return jnp.concatenate([gla, ssd, ret], axis=1)


def _post_row_specs(tr):
    blk = lambda w, j: pl.BlockSpec((tr, w), lambda i: (i, j))
    return [blk(256, 0), blk(256, 0), blk(256, _ROFF["gla_r"] // 256), blk(512, 0), blk(512, 0), blk(512, 0),
            blk(512, _ROFF["ssd_z"] // 512), blk(256, 0), blk(256, 0), blk(256, _ROFF["ret_g"] // 256)]


def _post_fwd_call(rows, qp, tr, comm=None):
    T = rows[0].shape[0]

    def body(*refs):
        refs[-1][...] = _post_tile(*[r[...] for r in refs[:-1]]).astype(refs[-1].dtype)

    res, got = _pcall(body, name="post_fwd", grid=(T // tr,), out_shape=[jax.ShapeDtypeStruct((T, D), BF16)],
                      in_specs=_post_row_specs(tr) + [_whole(p) for p in qp],
                      out_specs=[pl.BlockSpec((tr, D), lambda i: (i, 0))], scratch_shapes=[],
                      sem=("parallel",), args=(*rows, *qp), comm=comm)
    return res[0], got


def _post_bwd_call(rows, qp, dmixed, tr):
    T = rows[0].shape[0]
    n_in = 10 + len(qp)

    def body(*refs):
        ins = [r[...] for r in refs[:n_in]]
        _, vjp = jax.vjp(_post_tile, *ins)
        d = vjp(refs[n_in][...])
        outs = refs[n_in + 1:]
        for o_ref, g in zip(outs[:7], (d[0], d[3], d[7], d[2], d[6], d[9], d[5])):
            o_ref[...] = g.astype(o_ref.dtype)

        @pl.when(pl.program_id(0) == 0)
        def _():
            for r in outs[7:]:
                r[...] = jnp.zeros_like(r)

        for r, g in zip(outs[7:], d[10:]):
            r[...] += g

    widths = [256, 512, 256, 256, 512, 256, 512]
    dts = [BF16] * 3 + [F32] * 4
    return pl.pallas_call(
        body, name="post_bwd", grid=(T // tr,),
        out_shape=tuple(jax.ShapeDtypeStruct((T, w), dt) for w, dt in zip(widths, dts))
        + tuple(jax.ShapeDtypeStruct(p.shape, F32) for p in qp),
        in_specs=_post_row_specs(tr) + [_whole(p) for p in qp] + [pl.BlockSpec((tr, D), lambda i: (i, 0))],
        out_specs=tuple(pl.BlockSpec((tr, w), lambda i: (i, 0)) for w in widths) + tuple(_whole(p) for p in qp),
        compiler_params=_params(("arbitrary",)),
    )(*rows, *qp, dmixed)


def _mixer_scan_operands(Pr, u, a_gla, a_ret, cmr, kf, kb, g8f, g8b, lg):
    gk, gv = (Pr, _ROFF["gla_k"] // 128), (Pr, _ROFF["gla_v"] // 256)
    rv = (Pr, _ROFF["ret_v"] // 256)
    return {
        "gla": {False: ((a_gla, 0), gk, gv, (a_gla, 1)), True: ((a_gla, 0), gk, gv, (a_gla, 2))},
        "ret": {False: ((a_ret, 0), (a_ret, 1), rv, (lg, 0)), True: ((a_ret, 0), (a_ret, 1), rv, (lg, 0))},
        "ssd": {False: ((cmr, 0), (kf, 0), (u, 0), (g8f, 0)), True: ((cmr, 0), (kb, 0), (u, 0), (g8b, 0))},
    }


def _post_rows(o, Pr, u):
    return [o["gla"][False][0], o["gla"][True][0], Pr, o["ssd"][False][0], o["ssd"][True][0], u, Pr,
            o["ret"][False][0], o["ret"][True][0], Pr]


def _mixer_forward(Tc, Pr, Px, cn, pp, cw8, cb, qp, comm):
    cosE, sinE, lg = cn
    u = _conv_fwd_call(Px, cw8, cb, Tc)
    prep = _prep_fwd_call(Pr, u, cosE, sinE, pp, Tc)
    ops = _mixer_scan_operands(Pr, u, *prep, lg)
    o, got = {}, {}
    for kind in ops:
        o[kind], got[kind] = _scan_fwd_call(kind, ops[kind], Tc, comm.get(kind))
    mixed, got["post"] = _post_fwd_call(_post_rows(o, Pr, u), qp, Tc, comm.get("post"))
    return mixed, (u, prep, o), got


def _mixer_backward(Tc, Pr, Px, cn, pp, cw8, cb, qp, saved, dmixed, comm):
    cosE, sinE, lg = cn
    u, prep, o = saved
    post = _post_bwd_call(_post_rows(o, Pr, u), qp, dmixed, Tc)
    d_o = dict(gla=post[0], ssd=post[1], ret=post[2])
    cts = dict(d_r=post[3], d_z=post[4], d_gr=post[5], d_xs=post[6])
    ops = _mixer_scan_operands(Pr, u, *prep, lg)
    got = {}
    for kind in ops:
        st = {rev: o[kind][rev][1] for rev in (False, True)}
        res, got[kind] = _scan_bwd_call(kind, kind != "ret", ops[kind], st, d_o[kind], Tc, comm.get(kind))
        for rev, sfx in ((False, "_f"), (True, "_b")):
            for nm, a in zip(("_dq", "_dk", "_dv", "_dg"), res[rev]):
                cts[kind + nm + sfx] = a
    pb, got["prep"] = _prep_bwd_call(Pr, u, cosE, sinE, pp, cts, Tc, comm.get("prep"))
    dPx, dcw8, dcb = _conv_bwd_call(Px, cw8, cb, pb[1], Tc)
    return pb[0], dPx, tuple(pb[2:]), dcw8, dcb[0:1], tuple(post[7:]), got


def _mix_fwd(Tc, X, w, cn, nw, mods, pp, cw8, cb, qp, comm):
    h = _norm_fwd_call(X, nw[0], mods[0], mods[1], None, Tc, BF16)
    Px, Pr = _mm(h, w[0], trans_b=True, name="mm_fwd"), _mm(h, w[1], trans_b=True, name="mm_fwd")
    mixed, saved, got = _mixer_forward(Tc, Pr, Px, cn, pp, cw8, cb, qp, comm)
    M = _mm(mixed, w[2], name="mm_fwd")
    Xn = _norm_fwd_call(M, nw[1], mods[2], jnp.zeros_like(mods[2]), X, Tc)
    return Xn, (X, nw, mods, w, cn, pp, cw8, cb, qp, h, Px, Pr, mixed, saved, M), got


def _mix_bwd(Tc, res, dXn, comm):
    X, nw, mods, w, cn, pp, cw8, cb, qp, h, Px, Pr, mixed, saved, M = res
    dM, dnw1, da_post, _ = _norm_bwd_call(M, nw[1], mods[2], dXn, Tc, out_dtype=BF16)
    dmixed = _mm(dM, w[2], trans_b=True, name="mm_dx")
    dWo = _mm_tn(mixed, dM, name="mm_dw", out_dtype=BF16)
    if "ssd" in comm:
        comm = dict(comm, ssd=(True, comm["ssd"][1] + [_rows_to(dWo)]))
    dPr, dPx, dpp, dcw8, dcb, dqp, got = _mixer_backward(Tc, Pr, Px, cn, pp, cw8, cb, qp, saved, dmixed, comm)
    dh = _mm(dPx, w[0], name="mm_dx")
    dh = _mm(dPr, w[1], name="mm_dx_acc", add=dh)
    dX, dnw0, da_pre, db_pre = _norm_bwd_call(X, nw[0], mods[0], dh, Tc, add=dXn)
    dW = tuple(_mm_tn(a, g, name="mm_dw", out_dtype=BF16) for a, g in ((dPx, h), (dPr, h))) + (dWo,)
    return dX, ((dnw0[0], dnw1[0]), (da_pre, db_pre, da_post), dpp, dcw8, dcb, dqp), dW, got


def _ffn_fwd(Tc, X, w, nw, mods):
    h = _norm_fwd_call(X, nw[0], mods[0], mods[1], None, Tc, BF16)
    U1 = _mm(h, w[0], trans_b=True, name="mm_fwd", out_dtype=BF16)
    U2 = _mm(h, w[1], trans_b=True, name="mm_fwd", out_dtype=BF16)
    act = _act_call(U1, U2)
    Fo = _mm(act, w[2], name="mm_fwd")
    Xn = _norm_fwd_call(Fo, nw[1], mods[2], jnp.zeros_like(mods[2]), X, Tc)
    return Xn, (X, nw, mods, w, h, U1, U2, act, Fo)


def _ffn_bwd(Tc, res, dXn):
    X, nw, mods, w, h, U1, U2, act, Fo = res
    dFo, dnw1, da_post, _ = _norm_bwd_call(Fo, nw[1], mods[2], dXn, Tc, out_dtype=BF16)
    dU1, dU2 = _act_call(U1, U2, _mm(dFo, w[2], trans_b=True, name="mm_dx", out_dtype=BF16))
    dh = _mm(dU1, w[0], name="mm_dx")
    dh = _mm(dU2, w[1], name="mm_dx_acc", add=dh)
    dX, dnw0, da_pre, db_pre = _norm_bwd_call(X, nw[0], mods[0], dh, Tc, add=dXn)
    dW = tuple(_mm_tn(a, g, name="mm_dw", out_dtype=BF16) for a, g in ((dU1, h), (dU2, h), (act, dFo)))
    return dX, ((dnw0[0], dnw1[0]), (da_pre, db_pre, da_post)), dW


def _rope_tables(Tl, Tc):
    rows = Tl // GRID_W
    row = jnp.repeat(jnp.arange(rows), GRID_W).astype(F32)
    col = jnp.tile(jnp.arange(GRID_W), rows).astype(F32)
    inv_freq = 10000.0 ** (-jnp.arange(16, dtype=F32) / 16)
    ang = jnp.concatenate([row[:, None] * inv_freq, col[:, None] * inv_freq], axis=-1)
    cos = jnp.concatenate([jnp.ones((Tc, 32), F32), jnp.cos(ang)], axis=0)
    sin = jnp.concatenate([jnp.zeros((Tc, 32), F32), jnp.sin(ang)], axis=0)
    return jnp.tile(cos, (1, 8)), jnp.tile(sin, (1, 8))


def _rows8(first, second):
    z = jnp.zeros((6,) + first.shape, F32)
    return jnp.concatenate([first[None], second[None], z], axis=0)


def _layer_inputs(sp, ml, mc):
    gu = sp["gla_gate_up"]
    Wg = jnp.zeros((128, 256), F32).at[0:16, 0:128].set(gu[0]).at[16:32, 128:256].set(gu[1])
    pp = (Wg, sp["gla_gate_b"].reshape(1, 256), sp["ssd_dt_bias"][0:1], sp["ssd_dt_bias"][1:2],
          -jnp.exp(sp["ssd_a_log"][0:1]), -jnp.exp(sp["ssd_a_log"][1:2]))
    qp = (sp["gla_norm"].reshape(1, 256), jnp.repeat(sp["ssd_d"], 64).reshape(1, 512),
          sp["ssd_norm"].reshape(1, 512), sp["ret_norm"].reshape(1, 256))
    mix = ((sp["norm_mix_pre"], sp["norm_mix_post"]),
           (_rows8(1.0 + mc[1], 1.0 + ml[1]), _rows8(mc[0], ml[0]), _rows8(mc[2], ml[2])),
           pp, jnp.pad(sp["ssd_conv_w"], ((0, 3), (0, 0))), sp["ssd_conv_b"].reshape(1, 1024), qp)
    ffn = ((sp["norm_ffn_pre"], sp["norm_ffn_post"]),
           (_rows8(1.0 + mc[4], 1.0 + ml[4]), _rows8(mc[3], ml[3]), _rows8(mc[5], ml[5])))
    return mix, ffn


def _rows_from(g):
    return g.reshape(N_DEV * g.shape[1], g.shape[2])


def _rows_to(f):
    return f.reshape(N_DEV, f.shape[0] // N_DEV, f.shape[1])


def _local_step(xcat, target, mod_l, mod_c, sp, Tc, weights=None, shards=None):
    Tt = xcat.shape[0]
    cosE, sinE = _rope_tables(Tt - Tc, Tc)
    log_gamma = jnp.log1p(-jnp.exp2(-5.0 - jnp.arange(4, dtype=F32)))
    lg = jnp.broadcast_to(jnp.concatenate([log_gamma, jnp.zeros((GPAD - 4,), F32)])[None, :], (Tt, GPAD))
    cn = (cosE, sinE, lg)
    dist = shards is not None
    X, saved = xcat, []
    if dist:
        g_in, g_out = _exchange_call("two", shards[0][:2], "gather_mix0")
    for l in range(DEPTH):
        (a_mix, a_ffn), pull = jax.vjp(_layer_inputs, {n: sp[n][l] for n in _SMALL},
                                       mod_l[l].reshape(6, D), mod_c[l].reshape(6, D))
        comm = {}
        if dist:
            w_in, w_out = _rows_from(g_in), _rows_from(g_out)
            more = l + 1 < DEPTH
            comm = dict(ssd=("two", [shards[l][2]] + ([shards[l + 1][1]] if more else [])),
                        ret=("two", [shards[l][3]]))
            if more:
                comm.update(gla=("two", [shards[l + 1][0]]))
        else:
            w_in, w_out, w13, w2 = weights[l]
        w_x, w_r = _split_w_in(w_in)
        X, r_mix, got = _mix_fwd(Tc, X, (w_x, w_r, w_out), cn, *a_mix, comm)
        if dist:
            w13, w2 = _rows_from(got["ssd"][0]), _rows_from(got["ret"][0])
            if more:
                g_in, g_out = got["gla"][0], got["ssd"][1]
        X, r_ffn = _ffn_fwd(Tc, X, (w13[:FFN_H], w13[FFN_H:], w2), *a_ffn)
        saved.append((r_mix, r_ffn, pull))
    loss, dX = _loss_call(X, target, Tc)
    d_sp, d_ml, d_mc = [None] * DEPTH, [None] * DEPTH, [None] * DEPTH
    gw = [[None] * 4 for _ in range(DEPTH)]
    nxt = None
    for l in reversed(range(DEPTH)):
        r_mix, r_ffn, pull = saved[l]
        dX, c_ffn, dW_ffn = _ffn_bwd(Tc, r_ffn, dX)
        g13, g2 = jnp.concatenate([dW_ffn[0], dW_ffn[1]], axis=0), dW_ffn[2]
        comm = {}
        if dist:
            comm = dict(ssd=(True, [_rows_to(g13)]), ret=(True, [_rows_to(g2)]))
            if nxt is not None:
                comm.update(gla=(True, [nxt]))
        dX, c_mix, dW_mix, got = _mix_bwd(Tc, r_mix, dX, comm)
        d_sp[l], d_ml[l], d_mc[l] = pull((c_mix, c_ffn))
        gin, gout = _merge_w_in(dW_mix[0], dW_mix[1]), dW_mix[2]
        if dist:
            gw[l][2], gw[l][1], gw[l][3] = got["ssd"][0], got["ssd"][1], got["ret"][0]
            if nxt is not None:
                gw[l + 1][0] = got["gla"][0]
            nxt = _rows_to(gin)
        else:
            gw[l] = [gin, gout, g13, g2]
    if dist:
        gw[0][0], = _exchange_call(True, [nxt], "scatter_mix0")
    d_sp = {n: jnp.stack([d_sp[l][n] for l in range(DEPTH)]) for n in _SMALL}
    return (loss, dX, jnp.stack(d_ml).reshape(DEPTH, 6 * D), jnp.stack(d_mc).reshape(DEPTH, 6 * D), d_sp, gw)


def _sum8_call(slabs, name):
    _, R, Cc = slabs.shape
    tr = _pick(R, (512, 352, 256, 128, 64, 32, 16))

    def body(*refs):
        acc = refs[0][...].astype(F32)
        for r in refs[1:N_DEV]:
            acc = acc + r[...].astype(F32)
        refs[N_DEV][...] = acc

    return pl.pallas_call(
        body, name=name, grid=(R // tr,), out_shape=jax.ShapeDtypeStruct((R, Cc), F32),
        in_specs=[pl.BlockSpec((None, tr, Cc), lambda i, d=d: (d, i, 0)) for d in range(N_DEV)],
        out_specs=pl.BlockSpec((tr, Cc), lambda i: (i, 0)), compiler_params=_params(("parallel",)),
    )(*([slabs] * N_DEV))


def _loss_call(X, target, Tc):
    Tt, W = X.shape
    tr = Tc
    nt = Tt // tr

    def body(x_ref, t_ref, loss_ref, dx_ref, acc_ref):
        i = pl.program_id(0)

        @pl.when(i == 0)
        def _():
            acc_ref[...] = jnp.zeros_like(acc_ref)
            dx_ref[...] = jnp.zeros_like(dx_ref)

        @pl.when(i > 0)
        def _():
            e = x_ref[...] - t_ref[...]
            dx_ref[...] = e * (1.0 / W)
            acc_ref[...] += jnp.sum(e * e, axis=0, keepdims=True)

        @pl.when(i == nt - 1)
        def _():
            loss_ref[...] = jnp.full(loss_ref.shape, (0.5 / W) * jnp.sum(acc_ref[...]), F32)

    loss, dx = pl.pallas_call(
        body, name="loss",
        out_shape=(jax.ShapeDtypeStruct((8, 128), F32), jax.ShapeDtypeStruct((Tt, W), F32)),
        grid=(nt,),
        in_specs=[pl.BlockSpec((tr, W), lambda i: (i, 0)),
                  pl.BlockSpec((tr, W), lambda i: (jnp.maximum(i - 1, 0), 0))],
        out_specs=(pl.BlockSpec((8, 128), lambda i: (0, 0)), pl.BlockSpec((tr, W), lambda i: (i, 0))),
        scratch_shapes=[pltpu.VMEM((1, W), F32)],
        compiler_params=_params(("arbitrary",)),
    )(X, target)
    return loss[0, 0], dx


def _adamw_call(w, g, m, v, name):
    R, Cc = w.shape
    tr = _pick(R, (512, 352, 256, 128, 64, 32, 16, 8))
    c1 = 1.0 - ADAM_B1 ** ADAM_STEP
    c2 = 1.0 - ADAM_B2 ** ADAM_STEP

    def body(w_ref, g_ref, m_ref, v_ref, d_ref, nm_ref, nv_ref):
        gv = g_ref[...]
        nm = ADAM_B1 * m_ref[...] + (1.0 - ADAM_B1) * gv
        nv = ADAM_B2 * v_ref[...] + (1.0 - ADAM_B2) * (gv * gv)
        d_ref[...] = -ADAM_LR * ((nm / c1) / (jnp.sqrt(nv / c2) + ADAM_EPS) + ADAM_WD * w_ref[...])
        nm_ref[...] = nm
        nv_ref[...] = nv

    spec = pl.BlockSpec((tr, Cc), lambda i: (i, 0))
    sh = jax.ShapeDtypeStruct((R, Cc), F32)
    return pl.pallas_call(
        body, name=name, out_shape=(sh, sh, sh), grid=(R // tr,),
        in_specs=[spec] * 4, out_specs=(spec,) * 3, compiler_params=_params(("parallel",)),
    )(w, g, m, v)


def _sum_call(xs, name):
    R, Cc = xs[0].shape
    tr = _pick(R, (512, 352, 256, 128, 64, 32, 16))
    k = len(xs)

    def body(*refs):
        acc = refs[0][...]
        for r in refs[1:k]:
            acc = acc + r[...]
        refs[k][...] = acc

    spec = pl.BlockSpec((tr, Cc), lambda i: (i, 0))
    return pl.pallas_call(
        body, name=name, grid=(R // tr,), in_specs=[spec] * k, out_shape=jax.ShapeDtypeStruct((R, Cc), F32),
        out_specs=spec, compiler_params=_params(("parallel",)),
    )(*xs)


MESH = pl.DeviceIdType.MESH
ANY = pl.BlockSpec(memory_space=pl.ANY)


def _me():
    return lax.axis_index("x"), lax.axis_index("y"), lax.axis_index("c")


_FLIPS = [(0, 0, 1), (1, 0, 0), (0, 1, 0), (1, 1, 0), (1, 0, 1), (0, 1, 1), (1, 1, 1)]


def _exchange_copies(scatter, srcs, dsts, send_sems, recv_sems, loc_sems, arrivals):
    x, y, c = _me()
    me = 4 * x + 2 * y + c
    sends, recvs, local = [], [], []
    for a in range(len(srcs)):
        for k, (dx, dy, dc) in enumerate(_FLIPS):
            px, py, pc = (1 - x if dx else x), (1 - y if dy else y), (1 - c if dc else c)
            peer = 4 * px + 2 * py + pc
            src = srcs[a].at[peer] if scatter else srcs[a]
            for lst, slab in ((sends, me), (recvs, peer)) if arrivals else ((sends, me),):
                lst.append(pltpu.make_async_remote_copy(
                    src_ref=src, dst_ref=dsts[a].at[slab], send_sem=send_sems.at[a, k], recv_sem=recv_sems.at[a, k],
                    device_id=(px, py, pc), device_id_type=MESH))
        local.append(pltpu.make_async_copy(srcs[a].at[me] if scatter else srcs[a], dsts[a].at[me], loc_sems.at[a]))
    return sends, recvs, local


def _exchange_start(scatter, srcs, dsts, sems):
    sends, _, local = _exchange_copies(scatter, srcs, dsts, *sems, arrivals=False)
    for cp in local + sends:
        cp.start()


def _exchange_wait(scatter, srcs, dsts, sems):
    sends, recvs, local = _exchange_copies(scatter, srcs, dsts, *sems, arrivals=True)
    for cp in sends:
        cp.wait_send()
    for cp in recvs:
        cp.wait_recv()
    for cp in local:
        cp.wait()


def _exchange_shapes(scatter, srcs):
    return tuple(jax.ShapeDtypeStruct(((N_DEV,) + s.shape[-2:]), s.dtype) for s in srcs)


def _exchange_sems(n):
    return [pltpu.SemaphoreType.DMA((n, 7)), pltpu.SemaphoreType.DMA((n, 7)), pltpu.SemaphoreType.DMA((n,))]


def _exchange_call(scatter, srcs, name):
    n = len(srcs)

    def body(*refs):
        if scatter == "two":
            _two_level_gather_body(n, refs[:n], refs[n:2 * n], *refs[2 * n:])
        else:
            _exchange_start(scatter, refs[:n], refs[n:2 * n], refs[2 * n:])
            _exchange_wait(scatter, refs[:n], refs[n:2 * n], refs[2 * n:])

    return pl.pallas_call(body, name=name, out_shape=_exchange_shapes(scatter, srcs), in_specs=[ANY] * n,
                          out_specs=(ANY,) * n, scratch_shapes=_exchange_sems(n))(*srcs)


def _pcall(body, *, name, grid, in_specs, out_specs, out_shape, scratch_shapes, sem, args, comm=None):
    if comm is None:
        res = pl.pallas_call(body, name=name, grid=grid, in_specs=list(in_specs), out_specs=tuple(out_specs),
                             out_shape=tuple(out_shape), scratch_shapes=list(scratch_shapes),
                             compiler_params=_params(sem))(*args)
        return tuple(res), ()
    scatter, srcs = comm
    n_in, n_out, n_c, n_s = len(in_specs), len(out_specs), len(srcs), len(scratch_shapes)

    def carrier(*refs):
        ins, c_src = refs[:n_in], refs[n_in:n_in + n_c]
        outs = refs[n_in + n_c:n_in + n_c + n_out]
        c_dst = refs[n_in + n_c + n_out:n_in + 2 * n_c + n_out]
        scr = refs[n_in + 2 * n_c + n_out:n_in + 2 * n_c + n_out + n_s]
        first = pl.program_id(0) == 0
        last = pl.program_id(0) == grid[0] - 1
        for ax in range(1, len(grid)):
            first = jnp.logical_and(first, pl.program_id(ax) == 0)
            last = jnp.logical_and(last, pl.program_id(ax) == grid[ax] - 1)

        two_level = scatter == "two"

        @pl.when(first)
        def _():
            if two_level:
                _two_level_gather("start", n_c, c_src, c_dst, *refs[-3:])
            else:
                _exchange_start(scatter, c_src, c_dst, refs[-3:])

        body(*ins, *outs, *scr)

        if two_level:
            @pl.when(pl.program_id(0) == (3 * grid[0]) // 4)
            def _():
                _two_level_gather("pass", n_c, c_src, c_dst, *refs[-3:])

        @pl.when(last)
        def _():
            if two_level:
                _two_level_gather("finish", n_c, c_src, c_dst, *refs[-3:])
            else:
                _exchange_wait(scatter, c_src, c_dst, refs[-3:])

    res = pl.pallas_call(
        carrier, name=name + "_x", grid=grid, in_specs=list(in_specs) + [ANY] * n_c,
        out_specs=tuple(out_specs) + (ANY,) * n_c, out_shape=tuple(out_shape) + _exchange_shapes(scatter, srcs),
        scratch_shapes=list(scratch_shapes) + _exchange_sems(n_c),
        compiler_params=_params(("arbitrary",) * len(grid)))(*args, *srcs)
    return tuple(res[:n_out]), tuple(res[n_out:])


def _two_level_gather(phase, n_arr, x_refs, out_refs, send_sems, recv_sems, local_sems):
    x, y, c = _me()
    me, sibling = (x, y, c), (x, y, 1 - c)
    chips = [(1 - x, y), (x, 1 - y), (1 - x, 1 - y)]

    def slab(a, px, py, pc):
        return out_refs[a].at[4 * px + 2 * py + pc]

    def copy(a, k, block, to, src=None):
        return pltpu.make_async_remote_copy(
            src_ref=slab(a, *block) if src is None else src, dst_ref=slab(a, *block),
            send_sem=send_sems.at[a, k], recv_sem=recv_sems.at[a, k], device_id=to, device_id_type=MESH)

    def first(a):
        return [copy(a, 0, me, sibling, src=x_refs[a])] + [copy(a, 1 + j, me, (*chip, c), src=x_refs[a])
                                                            for j, chip in enumerate(chips)]

    if phase == "start":
        for a in range(n_arr):
            pltpu.make_async_copy(x_refs[a], slab(a, *me), local_sems.at[a]).start()
        for a in range(n_arr):
            for cp in first(a):
                cp.start()
    elif phase == "pass":
        for j, chip in enumerate(chips):
            for a in range(n_arr):
                copy(a, 1 + j, (*chip, c), me).wait_recv()
                copy(a, 4 + j, (*chip, c), sibling).start()
    else:
        for a in range(n_arr):
            copy(a, 0, sibling, me).wait_recv()
            for j, chip in enumerate(chips):
                copy(a, 4 + j, (*chip, 1 - c), me).wait_recv()
        for a in range(n_arr):
            for cp in first(a) + [copy(a, 4 + j, (*chip, c), sibling) for j, chip in enumerate(chips)]:
                cp.wait_send()
            pltpu.make_async_copy(x_refs[a], slab(a, *me), local_sems.at[a]).wait()


def _two_level_gather_body(n_arr, x_refs, out_refs, send_sems, recv_sems, local_sems):
    for phase in ("start", "pass", "finish"):
        _two_level_gather(phase, n_arr, x_refs, out_refs, send_sems, recv_sems, local_sems)


def _gather_small(x, name):
    def body(x_ref, out_ref, send_sems, recv_sems, local_sems):
        _two_level_gather_body(1, [x_ref], [out_ref], send_sems, recv_sems, local_sems)

    vm = pl.BlockSpec(memory_space=pltpu.VMEM)
    return pl.pallas_call(
        body, name=name,
        out_shape=jax.ShapeDtypeStruct((N_DEV,) + x.shape, x.dtype),
        in_specs=[vm], out_specs=vm,
        scratch_shapes=[pltpu.SemaphoreType.DMA((1, 7)), pltpu.SemaphoreType.DMA((1, 7)),
                        pltpu.SemaphoreType.DMA((1,))],
    )(x)


_SMALL = ["norm_mix_pre", "norm_mix_post", "norm_ffn_pre", "norm_ffn_post", "gla_gate_up", "gla_gate_b",
          "gla_norm", "ssd_conv_w", "ssd_conv_b", "ssd_dt_bias", "ssd_a_log", "ssd_d", "ssd_norm", "ret_norm"]


def _pack(arrs):
    flat = jnp.concatenate([a.reshape(-1) for a in arrs])
    n = flat.shape[0]
    npad = -(-n // 1024) * 1024
    return jnp.pad(flat, (0, npad - n)).reshape(npad // 128, 128)


def _unpack(buf, shapes):
    flat = buf.reshape(-1)
    out, o = [], 0
    for s in shapes:
        n = math.prod(s)
        out.append(flat[o:o + n].reshape(s))
        o += n
    return out


def kernel(x, c, ctx, c_ctx, ada_w, ada_b, norm_mix_pre, norm_mix_post, norm_ffn_pre, norm_ffn_post, w_in, w_out, gla_gate_up, gla_gate_b, gla_norm, ssd_conv_w, ssd_conv_b, ssd_dt_bias, ssd_a_log, ssd_d, ssd_norm, ret_norm, ffn_w13, ffn_w2, loss_target, m_c_ctx, m_ada_w, m_ada_b, m_norm_mix_pre, m_norm_mix_post, m_norm_ffn_pre, m_norm_ffn_post, m_w_in, m_w_out, m_gla_gate_up, m_gla_gate_b, m_gla_norm, m_ssd_conv_w, m_ssd_conv_b, m_ssd_dt_bias, m_ssd_a_log, m_ssd_d, m_ssd_norm, m_ret_norm, m_ffn_w13, m_ffn_w2, v_c_ctx, v_ada_w, v_ada_b, v_norm_mix_pre, v_norm_mix_post, v_norm_ffn_pre, v_norm_ffn_post, v_w_in, v_w_out, v_gla_gate_up, v_gla_gate_b, v_gla_norm, v_ssd_conv_w, v_ssd_conv_b, v_ssd_dt_bias, v_ssd_a_log, v_ssd_d, v_ssd_norm, v_ret_norm, v_ffn_w13, v_ffn_w2):
    P_ = dict(c_ctx=c_ctx, ada_w=ada_w, ada_b=ada_b, norm_mix_pre=norm_mix_pre, norm_mix_post=norm_mix_post,
              norm_ffn_pre=norm_ffn_pre, norm_ffn_post=norm_ffn_post, w_in=w_in, w_out=w_out,
              gla_gate_up=gla_gate_up, gla_gate_b=gla_gate_b, gla_norm=gla_norm, ssd_conv_w=ssd_conv_w,
              ssd_conv_b=ssd_conv_b, ssd_dt_bias=ssd_dt_bias, ssd_a_log=ssd_a_log, ssd_d=ssd_d,
              ssd_norm=ssd_norm, ret_norm=ret_norm, ffn_w13=ffn_w13, ffn_w2=ffn_w2)
    M_ = dict(c_ctx=m_c_ctx, ada_w=m_ada_w, ada_b=m_ada_b, norm_mix_pre=m_norm_mix_pre,
              norm_mix_post=m_norm_mix_post, norm_ffn_pre=m_norm_ffn_pre, norm_ffn_post=m_norm_ffn_post,
              w_in=m_w_in, w_out=m_w_out, gla_gate_up=m_gla_gate_up, gla_gate_b=m_gla_gate_b,
              gla_norm=m_gla_norm, ssd_conv_w=m_ssd_conv_w, ssd_conv_b=m_ssd_conv_b, ssd_dt_bias=m_ssd_dt_bias,
              ssd_a_log=m_ssd_a_log, ssd_d=m_ssd_d, ssd_norm=m_ssd_norm, ret_norm=m_ret_norm,
              ffn_w13=m_ffn_w13, ffn_w2=m_ffn_w2)
    V_ = dict(c_ctx=v_c_ctx, ada_w=v_ada_w, ada_b=v_ada_b, norm_mix_pre=v_norm_mix_pre,
              norm_mix_post=v_norm_mix_post, norm_ffn_pre=v_norm_ffn_pre, norm_ffn_post=v_norm_ffn_post,
              w_in=v_w_in, w_out=v_w_out, gla_gate_up=v_gla_gate_up, gla_gate_b=v_gla_gate_b,
              gla_norm=v_gla_norm, ssd_conv_w=v_ssd_conv_w, ssd_conv_b=v_ssd_conv_b, ssd_dt_bias=v_ssd_dt_bias,
              ssd_a_log=v_ssd_a_log, ssd_d=v_ssd_d, ssd_norm=v_ssd_norm, ret_norm=v_ret_norm,
              ffn_w13=v_ffn_w13, ffn_w2=v_ffn_w2)
    order = ["c_ctx", "ada_w", "ada_b", "norm_mix_pre", "norm_mix_post", "norm_ffn_pre", "norm_ffn_post", "w_in",
             "w_out", "gla_gate_up", "gla_gate_b", "gla_norm", "ssd_conv_w", "ssd_conv_b", "ssd_dt_bias",
             "ssd_a_log", "ssd_d", "ssd_norm", "ret_norm", "ffn_w13", "ffn_w2"]

    mx, my, mc_ = _me()
    me = 4 * mx + 2 * my + mc_
    Tl, Tc = x.shape[1], ctx.shape[1]
    n_in, n_out, n_13, n_2 = w_in.shape[2], w_out.shape[1], ffn_w13.shape[2], ffn_w2.shape[1]
    n_ada = ada_w.shape[2]

    shards = [[w_in[l].T.astype(BF16), w_out[l].astype(BF16), ffn_w13[l].T.astype(BF16), ffn_w2[l].astype(BF16)]
              for l in range(DEPTH)]

    cw = ssd_conv_w.shape[2]
    small_in = jnp.concatenate([jnp.pad(c, ((0, 7), (0, 0))).reshape(-1),
                                ssd_conv_w.reshape(-1)]).reshape(-1, 128)
    n_c_rows = 8 * D // 128
    small_in = jnp.pad(small_in, ((0, -small_in.shape[0] % 8), (0, 0)))
    gathered = _gather_small(small_in, "gather_c_conv")
    c_all = gathered[:, :n_c_rows].reshape(N_DEV, 8, D)[:, 0]
    conv_rows = DEPTH * 5 * cw // 128
    conv_full = gathered[:, n_c_rows:n_c_rows + conv_rows].reshape(N_DEV, DEPTH, 5, cw)
    conv_full = jnp.moveaxis(conv_full, 0, 2).reshape(DEPTH, 5, N_DEV * cw)
    c9 = jnp.concatenate([c_all, c_ctx[None], jnp.zeros((7, D), F32)], axis=0)
    s9 = c9 * jax.nn.sigmoid(c9)
    mod_piece = jnp.concatenate([_mm(s9, ada_w[l], name="mm_mod") for l in range(DEPTH)], axis=0)
    mod_g = _gather_small(mod_piece, "gather_mod")
    mod_all = jnp.moveaxis(mod_g.reshape(N_DEV, DEPTH, 16, n_ada), 0, 2).reshape(DEPTH, 16, N_DEV * n_ada)
    mod_all = mod_all + ada_b[:, None, :]
    mod_l = lax.dynamic_index_in_dim(mod_all, me, axis=1, keepdims=False)
    mod_c = mod_all[:, 8]

    sp = {n: P_[n] for n in _SMALL}
    sp["ssd_conv_w"] = conv_full
    xcat = jnp.concatenate([ctx[0], x[0]], axis=0)
    loss_local, d_xcat, d_mod_l, d_mod_c, d_sp, gw = _local_step(xcat, loss_target[0], mod_l, mod_c, sp, Tc,
                                                                 shards=shards)
    loss = lax.psum(loss_local, ("x", "y", "c"))
    grad_x = d_xcat[Tc:][None]

    G = {n: jnp.stack([_sum8_call(gw[l][a], f"sum_{n}") for l in range(DEPTH)])
         for a, n in enumerate(["w_in", "w_out", "ffn_w13", "ffn_w2"])}
    G["w_in"], G["ffn_w13"] = jnp.swapaxes(G["w_in"], 1, 2), jnp.swapaxes(G["ffn_w13"], 1, 2)

    dmod_rows = jnp.concatenate([d_mod_l, d_mod_c], axis=0)
    dmod_g = _gather_small(dmod_rows, "gather_dmod").reshape(N_DEV, 2, DEPTH, 6 * D)
    dl = jnp.moveaxis(dmod_g[:, 0], 0, 1)
    dc = dmod_g[:, 1, :, :]
    dc_tot = dc[0]
    for d_ in range(1, N_DEV):
        dc_tot = dc_tot + dc[d_]
    dmod9 = jnp.concatenate([dl, dc_tot[:, None, :], jnp.zeros((DEPTH, 7, 6 * D), F32)], axis=1)
    g_ada_b = dmod9[:, 0]
    for r_ in range(1, 9):
        g_ada_b = g_ada_b + dmod9[:, r_]
    dmod9_mine = lax.dynamic_slice_in_dim(dmod9, me * n_ada, n_ada, axis=2)
    s9T = jnp.pad(s9.T, ((0, 0), (0, 112)))
    g_ada_w = jnp.stack([_mm(s9T, jnp.pad(dmod9_mine[l], ((0, 112), (0, 0))), name="mm_dada")
                         for l in range(DEPTH)])
    ds9 = _mm(dmod9_mine[0], ada_w[0], trans_b=True, name="mm_ds9")
    for l in range(1, DEPTH):
        ds9 = _mm(dmod9_mine[l], ada_w[l], trans_b=True, name="mm_ds9_acc", add=ds9)
    ds_ctx_part = ds9[8]

    small_names = [n for n in _SMALL]
    small_parts = [d_sp[n] for n in small_names] + [ds_ctx_part]
    packed = _pack(small_parts)
    allp = _gather_small(packed, "gather_small_grads")
    summed = _sum_call([allp[d_] for d_ in range(N_DEV)], "sum_small_grads")
    parts = _unpack(summed, [p.shape for p in small_parts])
    for n, p in zip(small_names, parts[:-1]):
        G[n] = p
    sig = jax.nn.sigmoid(c_ctx)
    G["c_ctx"] = parts[-1] * (sig * (1.0 + c_ctx * (1.0 - sig)))
    G["ssd_conv_w"] = lax.dynamic_slice_in_dim(G["ssd_conv_w"], me * cw, cw, axis=2)
    G["ada_w"] = g_ada_w
    G["ada_b"] = g_ada_b

    delta, new_m, new_v = {}, {}, {}
    for n in ["ada_w", "w_in", "w_out", "ffn_w13", "ffn_w2"]:
        sh = P_[n].shape
        f2 = lambda a: a.reshape(sh[0] * sh[1], sh[2])
        d_, m_, v_ = _adamw_call(f2(P_[n]), f2(G[n]), f2(M_[n]), f2(V_[n]), f"adamw_{n}")
        delta[n], new_m[n], new_v[n] = d_.reshape(sh), m_.reshape(sh), v_.reshape(sh)
    rest = [n for n in order if n not in delta]
    shapes = [P_[n].shape for n in rest]
    d_, m_, v_ = _adamw_call(_pack([P_[n] for n in rest]), _pack([G[n] for n in rest]),
                             _pack([M_[n] for n in rest]), _pack([V_[n] for n in rest]), "adamw_small")
    for n, a, b, e in zip(rest, _unpack(d_, shapes), _unpack(m_, shapes), _unpack(v_, shapes)):
        delta[n], new_m[n], new_v[n] = a, b, e

    return (loss, grad_x, *[G[n] for n in order], *[delta[n] for n in order],
            *[new_m[n] for n in order], *[new_v[n] for n in order])
```
